```python
import jax, jax.numpy as jnp
from jax import lax
import numpy as np

D_MODEL = 1024
BATCH = 16
SEQ = 2048
DEPTH = 1

D_MIX = D_MODEL
HEAD_DIM = 64
ATT_WIDTH = D_MIX // 2
ATT_Q_HEADS = ATT_WIDTH // HEAD_DIM
ATT_KV_HEADS = 2
ATT_GROUP = ATT_Q_HEADS // ATT_KV_HEADS
ATT_KV_WIDTH = ATT_KV_HEADS * HEAD_DIM
Q_BLOCK = 128
ROPE_THETA = 10000.0
GRID_W = 64
RWKV_WIDTH = D_MIX - ATT_WIDTH
RWKV_HEADS = RWKV_WIDTH // HEAD_DIM
DECAY_RANK = 64
ICLR_RANK = 64
N_DIRS = 2
RWKV_SHIFT_WIDTH = 3 * RWKV_WIDTH + DECAY_RANK + ICLR_RANK
IN_SPLITS = (ATT_WIDTH, ATT_KV_WIDTH, ATT_KV_WIDTH, ATT_WIDTH, RWKV_SHIFT_WIDTH, RWKV_WIDTH)
IN_WIDTH = ATT_WIDTH + 2 * ATT_KV_WIDTH + ATT_WIDTH + RWKV_SHIFT_WIDTH + RWKV_WIDTH
DECAY_SCALE = 0.6065306597126334
NORM_EPS = 1e-6
GN_EPS = 64e-5
L2_EPS = 1e-12

kernel_name = "hybrid_gqa_rwkv7_parallel_heads"


def rms_norm(x, g, eps=NORM_EPS):
    xf = x.astype(jnp.float32)
    y = xf * lax.rsqrt(jnp.mean(xf * xf, axis=-1, keepdims=True) + eps)
    return (y * g.astype(jnp.float32)).astype(x.dtype)


def axial_rope_tables(T, dtype):
    rows = T // GRID_W
    row = jnp.repeat(jnp.arange(rows, dtype=jnp.float32), GRID_W)
    col = jnp.tile(jnp.arange(GRID_W, dtype=jnp.float32), rows)
    n_freq = HEAD_DIM // 4
    inv_freq = ROPE_THETA ** (-jnp.arange(n_freq, dtype=jnp.float32) / n_freq)
    ang_r = row[:, None, None] * inv_freq
    ang_c = col[:, None, None] * inv_freq
    return (jnp.cos(ang_r).astype(dtype), jnp.sin(ang_r).astype(dtype),
            jnp.cos(ang_c).astype(dtype), jnp.sin(ang_c).astype(dtype))


def rotate_axis(x, cos, sin):
    n = x.shape[-1] // 2
    x1, x2 = x[..., :n], x[..., n:]
    return jnp.concatenate([x1 * cos - x2 * sin, x2 * cos + x1 * sin], axis=-1)


def apply_axial_rope(x, tables):
    cos_r, sin_r, cos_c, sin_c = tables
    half = HEAD_DIM // 2
    return jnp.concatenate([rotate_axis(x[..., :half], cos_r, sin_r),
                            rotate_axis(x[..., half:], cos_c, sin_c)], axis=-1)


def axial_gqa_attention(q, k, v, q_norm_g, k_norm_g):
    B, T, _ = q.shape
    q = rms_norm(q.reshape(B, T, ATT_Q_HEADS, HEAD_DIM), q_norm_g)
    k = rms_norm(k.reshape(B, T, ATT_KV_HEADS, HEAD_DIM), k_norm_g)
    v = v.reshape(B, T, ATT_KV_HEADS, HEAD_DIM)
    tables = axial_rope_tables(T, q.dtype)
    q = apply_axial_rope(q, tables)
    k = apply_axial_rope(k, tables)
    n_blk = T // Q_BLOCK
    q_blocks = q.reshape(B, n_blk, Q_BLOCK, ATT_KV_HEADS, ATT_GROUP, HEAD_DIM).transpose(1, 0, 3, 4, 2, 5)
    k_t = k.transpose(0, 2, 1, 3)
    v_t = v.transpose(0, 2, 1, 3)
    scale = HEAD_DIM ** -0.5

    def attend(q_blk):
        s = jnp.einsum('bkgqd,bksd->bkgqs', q_blk, k_t).astype(jnp.float32) * scale
        p = jax.nn.softmax(s, axis=-1).astype(v_t.dtype)
        return jnp.einsum('bkgqs,bksd->bkgqd', p, v_t)

    o = lax.map(attend, q_blocks)
    return o.transpose(1, 0, 4, 2, 3, 5).reshape(B, T, ATT_WIDTH)


def centered_shift(x, taps):
    xp = jnp.pad(x, ((0, 0), (1, 1), (0, 0)))
    return taps[0] * xp[:, :-2] + taps[1] * xp[:, 1:-1] + taps[2] * xp[:, 2:]


def bidir_rwkv7(r, k, v, wd, ad, w_up, w0, a_up, a0, k_k, k_a, r_k, gn_w, gn_b):
    f32 = jnp.float32
    B, T, _ = r.shape
    H, N = RWKV_HEADS, HEAD_DIM
    rf, kf, vf = r.astype(f32), k.astype(f32), v.astype(f32)
    heads = lambda z: z.reshape(z.shape[:-1] + (H, N))
    w = jnp.exp(-DECAY_SCALE * jax.nn.sigmoid(
        w0.astype(f32)[:, None, None, :]
        + jnp.einsum('btr,zrc->zbtc', jnp.tanh(wd.astype(f32)), w_up.astype(f32))))
    a = jax.nn.sigmoid(a0.astype(f32)[:, None, None, :]
                       + jnp.einsum('btr,zrc->zbtc', ad.astype(f32), a_up.astype(f32)))
    kk = heads(kf * k_k.astype(f32))
    kk = kk * lax.rsqrt(jnp.sum(kk * kk, axis=-1, keepdims=True) + L2_EPS)
    kt = kf[None] * (1.0 + (a - 1.0) * k_a.astype(f32))
    akk = heads(a) * kk[None]

    both = lambda z: jnp.stack([z, z])

    def time_major(z):
        z = jnp.stack([z[0], jnp.flip(z[1], axis=1)])
        return jnp.moveaxis(z, 2, 0)

    xs = (time_major(heads(w)), time_major(heads(kt)), time_major(both(heads(vf))),
          time_major(both(heads(rf))), time_major(both(kk)), time_major(akk))
    S0 = jnp.zeros((N_DIRS, B, H, N, N), f32)

    def step(S, inp):
        w_t, k_t, v_t, r_t, kk_t, akk_t = inp
        S = (S * w_t[..., None, :]
             - jnp.einsum('zbhvk,zbhk->zbhv', S, kk_t)[..., None] * akk_t[..., None, :]
             + v_t[..., :, None] * k_t[..., None, :])
        return S, jnp.einsum('zbhvk,zbhk->zbhv', S, r_t)

    _, ys = lax.scan(step, S0, xs)
    ys = jnp.moveaxis(ys, 0, 2)
    y = ys[0] + jnp.flip(ys[1], axis=1)
    mu = jnp.mean(y, axis=-1, keepdims=True)
    var = jnp.mean((y - mu) ** 2, axis=-1, keepdims=True)
    y = (y - mu) * lax.rsqrt(var + GN_EPS) * heads(gn_w.astype(f32)) + heads(gn_b.astype(f32))
    bonus = jnp.einsum('bthn,zbthn,hn->bth', heads(rf), heads(kt), r_k.astype(f32))[..., None] * heads(vf)
    return (y + bonus).reshape(B, T, RWKV_WIDTH)


def _fwd_setup_inputs(seed: int = 0) -> dict:
    key = jax.random.key(seed)
    ks = jax.random.split(key, 24)
    L, D = DEPTH, D_MODEL
    nrm = jax.random.normal
    x = nrm(ks[0], (BATCH, SEQ, D), jnp.float32)
    c = nrm(ks[1], (BATCH, D), jnp.float32)
    w_ada = nrm(ks[2], (L, D, 3 * D), jnp.float32) * D ** -0.5
    b_ada = 0.01 * nrm(ks[3], (L, 3 * D), jnp.float32)
    g_pre = 1.0 + 0.05 * nrm(ks[4], (L, D), jnp.float32)
    w_in = nrm(ks[5], (L, D, IN_WIDTH), jnp.float32) * D ** -0.5
    q_norm_g = 1.0 + 0.05 * nrm(ks[6], (L, HEAD_DIM), jnp.float32)
    k_norm_g = 1.0 + 0.05 * nrm(ks[7], (L, HEAD_DIM), jnp.float32)
    shift_taps = (jnp.array([0.25, 1.0, 0.25], jnp.float32)[None, :, None]
                  + 0.05 * nrm(ks[8], (L, 3, RWKV_SHIFT_WIDTH), jnp.float32))
    w_up = 0.1 * nrm(ks[9], (L, N_DIRS, DECAY_RANK, RWKV_WIDTH), jnp.float32)
    w0 = jax.random.uniform(ks[10], (L, N_DIRS, RWKV_WIDTH), jnp.float32, -3.0, 3.0)
    a_up = 0.1 * nrm(ks[11], (L, N_DIRS, ICLR_RANK, RWKV_WIDTH), jnp.float32)
    a0 = 0.5 * nrm(ks[12], (L, N_DIRS, RWKV_WIDTH), jnp.float32)
    k_k = 0.85 + 0.05 * nrm(ks[13], (L, RWKV_WIDTH), jnp.float32)
    k_a = 1.0 + 0.05 * nrm(ks[14], (L, RWKV_WIDTH), jnp.float32)
    r_k = 0.1 * nrm(ks[15], (L, RWKV_HEADS, HEAD_DIM), jnp.float32)
    gn_w = 1.0 + 0.05 * nrm(ks[16], (L, RWKV_WIDTH), jnp.float32)
    gn_b = 0.01 * nrm(ks[17], (L, RWKV_WIDTH), jnp.float32)
    w_out = nrm(ks[18], (L, D_MIX, D), jnp.float32) * D_MIX ** -0.5
    g_post = 1.0 + 0.05 * nrm(ks[19], (L, D), jnp.float32)
    return {"x": x, "c": c, "w_ada": w_ada, "b_ada": b_ada, "g_pre": g_pre, "w_in": w_in,
            "q_norm_g": q_norm_g, "k_norm_g": k_norm_g, "shift_taps": shift_taps,
            "w_up": w_up, "w0": w0, "a_up": a_up, "a0": a0, "k_k": k_k, "k_a": k_a,
            "r_k": r_k, "gn_w": gn_w, "gn_b": gn_b, "w_out": w_out, "g_post": g_post}


def _fwd_reference(x, c, w_ada, b_ada, g_pre, w_in, q_norm_g, k_norm_g, shift_taps,
              w_up, w0, a_up, a0, k_k, k_a, r_k, gn_w, gn_b, w_out, g_post):
    split_idx = [int(i) for i in np.cumsum(IN_SPLITS)[:-1]]
    rwkv_idx = [RWKV_WIDTH, 2 * RWKV_WIDTH, 3 * RWKV_WIDTH, 3 * RWKV_WIDTH + DECAY_RANK]
    c_act = jax.nn.silu(c)
    for l in range(DEPTH):
        mod = jnp.einsum('bd,de->be', c_act, w_ada[l]) + b_ada[l]
        shift, scale, gate = jnp.split(mod, 3, axis=-1)
        h = rms_norm(x, g_pre[l]) * (1.0 + scale[:, None, :]) + shift[:, None, :]
        proj = jnp.einsum('btd,dp->btp', h, w_in[l])
        q, k, v, g_att, rwkv_in, g_rwkv = jnp.split(proj, split_idx, axis=-1)
        y_att = axial_gqa_attention(q, k, v, q_norm_g[l], k_norm_g[l])
        rwkv_in = centered_shift(rwkv_in, shift_taps[l])
        r, kr, vr, wd, ad = jnp.split(rwkv_in, rwkv_idx, axis=-1)
        y_rwkv = bidir_rwkv7(r, kr, vr, wd, ad, w_up[l], w0[l], a_up[l], a0[l],
                             k_k[l], k_a[l], r_k[l], gn_w[l], gn_b[l]).astype(x.dtype)
        mixed = jnp.concatenate([y_att * jax.nn.silu(g_att), y_rwkv * jax.nn.silu(g_rwkv)], axis=-1)
        out = jnp.einsum('btm,md->btd', mixed, w_out[l])
        x = x + gate[:, None, :] * rms_norm(out, g_post[l])
    return x


import jax as _jax
import jax.numpy as _jnp

TWIN_FORMAT = 'train_step'
FWD_PARAMS = ['x', 'c', 'w_ada', 'b_ada', 'g_pre', 'w_in', 'q_norm_g', 'k_norm_g', 'shift_taps', 'w_up', 'w0', 'a_up', 'a0', 'k_k', 'k_a', 'r_k', 'gn_w', 'gn_b', 'w_out', 'g_post']
TWIN_WEIGHTS = ['w_ada', 'b_ada', 'g_pre', 'w_in', 'q_norm_g', 'k_norm_g', 'shift_taps', 'w_up', 'w0', 'a_up', 'a0', 'k_k', 'k_a', 'r_k', 'gn_w', 'gn_b', 'w_out', 'g_post']
TWIN_DIFF_INPUT = 'x'
TWIN_INPUTS = ['x', 'c', 'w_ada', 'b_ada', 'g_pre', 'w_in', 'q_norm_g', 'k_norm_g', 'shift_taps', 'w_up', 'w0', 'a_up', 'a0', 'k_k', 'k_a', 'r_k', 'gn_w', 'gn_b', 'w_out', 'g_post', 'loss_target', 'm_w_ada', 'm_b_ada', 'm_g_pre', 'm_w_in', 'm_q_norm_g', 'm_k_norm_g', 'm_shift_taps', 'm_w_up', 'm_w0', 'm_a_up', 'm_a0', 'm_k_k', 'm_k_a', 'm_r_k', 'm_gn_w', 'm_gn_b', 'm_w_out', 'm_g_post', 'v_w_ada', 'v_b_ada', 'v_g_pre', 'v_w_in', 'v_q_norm_g', 'v_k_norm_g', 'v_shift_taps', 'v_w_up', 'v_w0', 'v_a_up', 'v_a0', 'v_k_k', 'v_k_a', 'v_r_k', 'v_gn_w', 'v_gn_b', 'v_w_out', 'v_g_post']
TWIN_OUTPUTS = ['loss', 'grad_x', 'grad_w_ada', 'grad_b_ada', 'grad_g_pre', 'grad_w_in', 'grad_q_norm_g', 'grad_k_norm_g', 'grad_shift_taps', 'grad_w_up', 'grad_w0', 'grad_a_up', 'grad_a0', 'grad_k_k', 'grad_k_a', 'grad_r_k', 'grad_gn_w', 'grad_gn_b', 'grad_w_out', 'grad_g_post', 'delta_w_ada', 'delta_b_ada', 'delta_g_pre', 'delta_w_in', 'delta_q_norm_g', 'delta_k_norm_g', 'delta_shift_taps', 'delta_w_up', 'delta_w0', 'delta_a_up', 'delta_a0', 'delta_k_k', 'delta_k_a', 'delta_r_k', 'delta_gn_w', 'delta_gn_b', 'delta_w_out', 'delta_g_post', 'new_m_w_ada', 'new_m_b_ada', 'new_m_g_pre', 'new_m_w_in', 'new_m_q_norm_g', 'new_m_k_norm_g', 'new_m_shift_taps', 'new_m_w_up', 'new_m_w0', 'new_m_a_up', 'new_m_a0', 'new_m_k_k', 'new_m_k_a', 'new_m_r_k', 'new_m_gn_w', 'new_m_gn_b', 'new_m_w_out', 'new_m_g_post', 'new_v_w_ada', 'new_v_b_ada', 'new_v_g_pre', 'new_v_w_in', 'new_v_q_norm_g', 'new_v_k_norm_g', 'new_v_shift_taps', 'new_v_w_up', 'new_v_w0', 'new_v_a_up', 'new_v_a0', 'new_v_k_k', 'new_v_k_a', 'new_v_r_k', 'new_v_gn_w', 'new_v_gn_b', 'new_v_w_out', 'new_v_g_post']
TWIN_LEAF_KINDS = {'loss': 'loss', 'grad_x': 'grad_x', 'grad_w_ada': 'grad_w', 'grad_b_ada': 'grad_w', 'grad_g_pre': 'grad_w', 'grad_w_in': 'grad_w', 'grad_q_norm_g': 'grad_w', 'grad_k_norm_g': 'grad_w', 'grad_shift_taps': 'grad_w', 'grad_w_up': 'grad_w', 'grad_w0': 'grad_w', 'grad_a_up': 'grad_w', 'grad_a0': 'grad_w', 'grad_k_k': 'grad_w', 'grad_k_a': 'grad_w', 'grad_r_k': 'grad_w', 'grad_gn_w': 'grad_w', 'grad_gn_b': 'grad_w', 'grad_w_out': 'grad_w', 'grad_g_post': 'grad_w', 'delta_w_ada': 'delta_w', 'delta_b_ada': 'delta_w', 'delta_g_pre': 'delta_w', 'delta_w_in': 'delta_w', 'delta_q_norm_g': 'delta_w', 'delta_k_norm_g': 'delta_w', 'delta_shift_taps': 'delta_w', 'delta_w_up': 'delta_w', 'delta_w0': 'delta_w', 'delta_a_up': 'delta_w', 'delta_a0': 'delta_w', 'delta_k_k': 'delta_w', 'delta_k_a': 'delta_w', 'delta_r_k': 'delta_w', 'delta_gn_w': 'delta_w', 'delta_gn_b': 'delta_w', 'delta_w_out': 'delta_w', 'delta_g_post': 'delta_w', 'new_m_w_ada': 'new_m', 'new_m_b_ada': 'new_m', 'new_m_g_pre': 'new_m', 'new_m_w_in': 'new_m', 'new_m_q_norm_g': 'new_m', 'new_m_k_norm_g': 'new_m', 'new_m_shift_taps': 'new_m', 'new_m_w_up': 'new_m', 'new_m_w0': 'new_m', 'new_m_a_up': 'new_m', 'new_m_a0': 'new_m', 'new_m_k_k': 'new_m', 'new_m_k_a': 'new_m', 'new_m_r_k': 'new_m', 'new_m_gn_w': 'new_m', 'new_m_gn_b': 'new_m', 'new_m_w_out': 'new_m', 'new_m_g_post': 'new_m', 'new_v_w_ada': 'new_v', 'new_v_b_ada': 'new_v', 'new_v_g_pre': 'new_v', 'new_v_w_in': 'new_v', 'new_v_q_norm_g': 'new_v', 'new_v_k_norm_g': 'new_v', 'new_v_shift_taps': 'new_v', 'new_v_w_up': 'new_v', 'new_v_w0': 'new_v', 'new_v_a_up': 'new_v', 'new_v_a0': 'new_v', 'new_v_k_k': 'new_v', 'new_v_k_a': 'new_v', 'new_v_r_k': 'new_v', 'new_v_gn_w': 'new_v', 'new_v_gn_b': 'new_v', 'new_v_w_out': 'new_v', 'new_v_g_post': 'new_v'}


def _forward(args):
    return _fwd_reference(*[args[k] for k in FWD_PARAMS])


def _output_shape():
    out = _jax.eval_shape(lambda: _forward(_fwd_setup_inputs(0)))
    return out.shape, out.dtype

N_MICROBATCH = 1
ADAM_LR = 0.001
ADAM_B1 = 0.9
ADAM_B2 = 0.999
ADAM_EPS = 1e-08
ADAM_WD = 0.01
ADAM_STEP = 10
PER_EXAMPLE_BATCH_AXIS = {'x': 0, 'c': 0, 'loss_target': 0}
SHARED_INPUTS = []
_WEIGHT_DTYPES = {'w_ada': _jnp.float32, 'b_ada': _jnp.float32, 'g_pre': _jnp.float32, 'w_in': _jnp.float32, 'q_norm_g': _jnp.float32, 'k_norm_g': _jnp.float32, 'shift_taps': _jnp.float32, 'w_up': _jnp.float32, 'w0': _jnp.float32, 'a_up': _jnp.float32, 'a0': _jnp.float32, 'k_k': _jnp.float32, 'k_a': _jnp.float32, 'r_k': _jnp.float32, 'gn_w': _jnp.float32, 'gn_b': _jnp.float32, 'w_out': _jnp.float32, 'g_post': _jnp.float32}
MOMENT_SCALE = {'w_ada': 3.528660e+00, 'b_ada': 6.581688e+00, 'g_pre': 3.200460e-01, 'w_in': 2.893794e-01, 'q_norm_g': 2.531328e-02, 'k_norm_g': 2.485209e-02, 'shift_taps': 3.009416e-01, 'w_up': 1.594344e-02, 'w0': 3.775153e-02, 'a_up': 7.642075e-02, 'a0': 5.479955e-02, 'k_k': 1.521917e-01, 'k_a': 3.282672e-01, 'r_k': 2.573371e+00, 'gn_w': 1.228900e-01, 'gn_b': 3.252831e-01, 'w_out': 5.131335e-01, 'g_post': 1.489184e+01}


def _to_microbatches(a, axis):
    t = _jnp.moveaxis(a, axis, 0)
    t = t.reshape((N_MICROBATCH, t.shape[0] // N_MICROBATCH) + t.shape[1:])
    return _jnp.moveaxis(t, 1, axis + 1)


def setup_inputs(seed: int = 0) -> dict:
    inp = _fwd_setup_inputs(seed)
    key = _jax.random.fold_in(_jax.random.key(seed), 7919)
    shape, _ = _output_shape()
    out = dict(inp)
    out["loss_target"] = _jax.random.normal(_jax.random.fold_in(key, 0), shape, _jnp.float32)
    for i, name in enumerate(TWIN_WEIGHTS):
        w = inp[name].astype(_jnp.float32)
        if MOMENT_SCALE is None:
            s = _jnp.sqrt(_jnp.mean(_jnp.square(w)) + 1e-30)
        else:
            s = MOMENT_SCALE[name]
        km, kv = _jax.random.split(_jax.random.fold_in(key, i + 1))
        out[name] = w
        out["m_" + name] = s * _jax.random.normal(km, w.shape, _jnp.float32)
        out["v_" + name] = (s * s) * _jax.random.uniform(kv, w.shape, _jnp.float32, 0.5, 1.5)
    if N_MICROBATCH > 1:
        for name, axis in PER_EXAMPLE_BATCH_AXIS.items():
            out[name] = _to_microbatches(out[name], axis)
    return {'x': out['x'], 'c': out['c'], 'w_ada': out['w_ada'], 'b_ada': out['b_ada'], 'g_pre': out['g_pre'], 'w_in': out['w_in'], 'q_norm_g': out['q_norm_g'], 'k_norm_g': out['k_norm_g'], 'shift_taps': out['shift_taps'], 'w_up': out['w_up'], 'w0': out['w0'], 'a_up': out['a_up'], 'a0': out['a0'], 'k_k': out['k_k'], 'k_a': out['k_a'], 'r_k': out['r_k'], 'gn_w': out['gn_w'], 'gn_b': out['gn_b'], 'w_out': out['w_out'], 'g_post': out['g_post'], 'loss_target': out['loss_target'], 'm_w_ada': out['m_w_ada'], 'm_b_ada': out['m_b_ada'], 'm_g_pre': out['m_g_pre'], 'm_w_in': out['m_w_in'], 'm_q_norm_g': out['m_q_norm_g'], 'm_k_norm_g': out['m_k_norm_g'], 'm_shift_taps': out['m_shift_taps'], 'm_w_up': out['m_w_up'], 'm_w0': out['m_w0'], 'm_a_up': out['m_a_up'], 'm_a0': out['m_a0'], 'm_k_k': out['m_k_k'], 'm_k_a': out['m_k_a'], 'm_r_k': out['m_r_k'], 'm_gn_w': out['m_gn_w'], 'm_gn_b': out['m_gn_b'], 'm_w_out': out['m_w_out'], 'm_g_post': out['m_g_post'], 'v_w_ada': out['v_w_ada'], 'v_b_ada': out['v_b_ada'], 'v_g_pre': out['v_g_pre'], 'v_w_in': out['v_w_in'], 'v_q_norm_g': out['v_q_norm_g'], 'v_k_norm_g': out['v_k_norm_g'], 'v_shift_taps': out['v_shift_taps'], 'v_w_up': out['v_w_up'], 'v_w0': out['v_w0'], 'v_a_up': out['v_a_up'], 'v_a0': out['v_a0'], 'v_k_k': out['v_k_k'], 'v_k_a': out['v_k_a'], 'v_r_k': out['v_r_k'], 'v_gn_w': out['v_gn_w'], 'v_gn_b': out['v_gn_b'], 'v_w_out': out['v_w_out'], 'v_g_post': out['v_g_post']}


def _loss(weights, diff, rest, loss_target):
    with _jax.named_scope("forward"):
        args = {**rest, TWIN_DIFF_INPUT: diff, **{k: w.astype(_WEIGHT_DTYPES[k]) for k, w in weights.items()}}
        y = _forward(args)
    with _jax.named_scope("loss_head"):
        err = _jnp.square(y.astype(_jnp.float32) - loss_target)
        return 0.5 * _jnp.sum(_jnp.mean(err, axis=-1)) if err.ndim else 0.5 * err


def _adamw(w, g, m, v):
    m = ADAM_B1 * m + (1.0 - ADAM_B1) * g
    v = ADAM_B2 * v + (1.0 - ADAM_B2) * _jnp.square(g)
    m_hat = m / (1.0 - ADAM_B1 ** ADAM_STEP)
    v_hat = v / (1.0 - ADAM_B2 ** ADAM_STEP)
    delta = -ADAM_LR * (m_hat / (_jnp.sqrt(v_hat) + ADAM_EPS) + ADAM_WD * w)
    return delta, m, v


def reference(x, c, w_ada, b_ada, g_pre, w_in, q_norm_g, k_norm_g, shift_taps, w_up, w0, a_up, a0, k_k, k_a, r_k, gn_w, gn_b, w_out, g_post, loss_target, m_w_ada, m_b_ada, m_g_pre, m_w_in, m_q_norm_g, m_k_norm_g, m_shift_taps, m_w_up, m_w0, m_a_up, m_a0, m_k_k, m_k_a, m_r_k, m_gn_w, m_gn_b, m_w_out, m_g_post, v_w_ada, v_b_ada, v_g_pre, v_w_in, v_q_norm_g, v_k_norm_g, v_shift_taps, v_w_up, v_w0, v_a_up, v_a0, v_k_k, v_k_a, v_r_k, v_gn_w, v_gn_b, v_w_out, v_g_post):
    given = dict(x=x, c=c, w_ada=w_ada, b_ada=b_ada, g_pre=g_pre, w_in=w_in, q_norm_g=q_norm_g, k_norm_g=k_norm_g, shift_taps=shift_taps, w_up=w_up, w0=w0, a_up=a_up, a0=a0, k_k=k_k, k_a=k_a, r_k=r_k, gn_w=gn_w, gn_b=gn_b, w_out=w_out, g_post=g_post, loss_target=loss_target, m_w_ada=m_w_ada, m_b_ada=m_b_ada, m_g_pre=m_g_pre, m_w_in=m_w_in, m_q_norm_g=m_q_norm_g, m_k_norm_g=m_k_norm_g, m_shift_taps=m_shift_taps, m_w_up=m_w_up, m_w0=m_w0, m_a_up=m_a_up, m_a0=m_a0, m_k_k=m_k_k, m_k_a=m_k_a, m_r_k=m_r_k, m_gn_w=m_gn_w, m_gn_b=m_gn_b, m_w_out=m_w_out, m_g_post=m_g_post, v_w_ada=v_w_ada, v_b_ada=v_b_ada, v_g_pre=v_g_pre, v_w_in=v_w_in, v_q_norm_g=v_q_norm_g, v_k_norm_g=v_k_norm_g, v_shift_taps=v_shift_taps, v_w_up=v_w_up, v_w0=v_w0, v_a_up=v_a_up, v_a0=v_a0, v_k_k=v_k_k, v_k_a=v_k_a, v_r_k=v_r_k, v_gn_w=v_gn_w, v_gn_b=v_gn_b, v_w_out=v_w_out, v_g_post=v_g_post)
    weights = {n: given[n] for n in TWIN_WEIGHTS}
    shared = {n: given[n] for n in SHARED_INPUTS}
    per_example = {n: given[n] for n in ['x', 'c']}
    grad_fn = _jax.value_and_grad(_loss, argnums=(0, 1))

    def one_microbatch(ex, loss_target):
        ex = dict(ex)
        diff = ex.pop(TWIN_DIFF_INPUT)
        return grad_fn(weights, diff, {**shared, **ex}, loss_target)

    if N_MICROBATCH == 1:
        loss, (grad_w, grad_x) = one_microbatch(per_example, given["loss_target"])
    else:
        def body(carry, xs):
            loss_sum, grad_sum = carry
            l_k, (gw_k, gx_k) = one_microbatch(xs[0], xs[1])
            with _jax.named_scope("update"):
                return (loss_sum + l_k, _jax.tree.map(_jnp.add, grad_sum, gw_k)), gx_k

        init = (_jnp.zeros((), _jnp.float32), _jax.tree.map(_jnp.zeros_like, weights))
        (loss, grad_w), grad_x = _jax.lax.scan(body, init, (per_example, given["loss_target"]))
    with _jax.named_scope("update"):
        delta_w, new_m, new_v = {}, {}, {}
        for n in TWIN_WEIGHTS:
            delta_w[n], new_m[n], new_v[n] = _adamw(weights[n], grad_w[n], given["m_" + n], given["v_" + n])
    return (loss, grad_x, *[grad_w[n] for n in TWIN_WEIGHTS], *[delta_w[n] for n in TWIN_WEIGHTS],
            *[new_m[n] for n in TWIN_WEIGHTS], *[new_v[n] for n in TWIN_WEIGHTS])
```

```python
import functools

import jax
import jax.numpy as jnp
from jax import lax
from jax.experimental import pallas as pl
from jax.experimental.pallas import tpu as pltpu

F32 = jnp.float32
MXU_DTYPE = jnp.bfloat16
MESH = pl.DeviceIdType.MESH
NDEV = 8

D_MODEL = 1024
HEAD_DIM = 64
ATT_W = 512
KV_W = 128
RWKV_W = 512
LORA_W = 128
SHIFT_W = 3 * RWKV_W + LORA_W
GRID_W = 64
ROPE_THETA = 10000.0
DECAY_SCALE = 0.6065306597126334
NORM_EPS = 1e-6
GN_EPS = 64e-5
L2_EPS = 1e-12
ATT_SCALE = HEAD_DIM ** -0.5
C_Q, C_K, C_V, C_GA, C_RIN, C_GRW, C_END = 0, 512, 640, 768, 1280, 2944, 3456

ADAM_LR, ADAM_B1, ADAM_B2, ADAM_EPS, ADAM_WD, ADAM_STEP = 0.001, 0.9, 0.999, 1e-08, 0.01, 10

ROW_TILE = 256
SCAN_CHUNK = 16
VMEM_LIMIT = 56 * 1024 * 1024


def _cp(sem=None):
    return pltpu.CompilerParams(dimension_semantics=sem, vmem_limit_bytes=VMEM_LIMIT)


def _dot(a, b, dims=(((1,), (0,)), ((), ()))):
    return lax.dot_general(a.astype(MXU_DTYPE), b.astype(MXU_DTYPE), dims, preferred_element_type=F32)


def _dot_nt(a, b):
    return _dot(a, b, (((1,), (1,)), ((), ())))


def _dot_tn(a, b):
    return _dot(a, b, (((0,), (0,)), ((), ())))


def _seg_dot(xb, bd):
    n = xb.shape[1]
    if n <= 256:
        return jnp.dot(xb, bd[:n, :n], preferred_element_type=F32)
    parts = [jnp.dot(xb[:, c:c + 256], bd, preferred_element_type=F32) for c in range(0, n, 256)]
    return jnp.concatenate(parts, axis=1)


def _split3(x):
    hi = x.astype(MXU_DTYPE)
    r1 = x - hi.astype(F32)
    mid = r1.astype(MXU_DTYPE)
    lo = (r1 - mid.astype(F32)).astype(MXU_DTYPE)
    return hi, mid, lo


def _segsum_raw(x, bd):
    hi, mid, lo = _split3(x)
    return _seg_dot(hi, bd) + _seg_dot(mid, bd) + _seg_dot(lo, bd)


@jax.custom_vjp
def _segsum_d(x, bd):
    return _segsum_raw(x, bd)


def _segsum_d_fwd(x, bd):
    return _segsum_raw(x, bd), bd


def _segsum_d_bwd(bd, ct):
    return _segsum_raw(ct, bd), jnp.zeros_like(bd)


_segsum_d.defvjp(_segsum_d_fwd, _segsum_d_bwd)


def _segsum2(x, bd):
    hi = x.astype(MXU_DTYPE)
    lo = (x - hi.astype(F32)).astype(MXU_DTYPE)
    return _seg_dot(hi, bd) + _seg_dot(lo, bd)


def _rope_tables(T):
    t = jnp.arange(T, dtype=F32)
    row = jnp.floor(t / GRID_W)
    col = t - row * GRID_W
    n_freq = HEAD_DIM // 4
    inv_freq = ROPE_THETA ** (-jnp.arange(n_freq, dtype=F32) / n_freq)
    d = jnp.arange(HEAD_DIM)
    pos = jnp.where((d < HEAD_DIM // 2)[None, :], row[:, None], col[:, None])
    ang = pos * inv_freq[d % n_freq][None, :]
    sign = jnp.where((d % 32) < 16, -1.0, 1.0).astype(F32)[None, :]
    cos = jnp.cos(ang)
    sin = jnp.sin(ang) * sign
    return jnp.tile(cos, (1, 2)), jnp.tile(sin, (1, 2))


def _rope_raw(x, cos, sin):
    n = x.shape[1]
    lane = lax.broadcasted_iota(jnp.int32, (1, n), 1)
    first = (lane % 32) < 16
    partner = jnp.where(first, pltpu.roll(x, n - 16, 1), pltpu.roll(x, 16, 1))
    return x * cos + partner * sin


@jax.custom_vjp
def _rope_d(x, cos, sin):
    return _rope_raw(x, cos, sin)


def _rope_d_fwd(x, cos, sin):
    return _rope_raw(x, cos, sin), (cos, sin)


def _rope_d_bwd(res, ct):
    cos, sin = res
    return _rope_raw(ct, cos, -sin), jnp.zeros_like(cos), jnp.zeros_like(sin)


_rope_d.defvjp(_rope_d_fwd, _rope_d_bwd)


def _rms(x, g):
    return x * lax.rsqrt(jnp.mean(x * x, axis=-1, keepdims=True) + NORM_EPS) * g


def _pre_fn(x, shift, scale, g_pre):
    return _rms(x, g_pre) * (1.0 + scale) + shift


def _qk_fn(q, g, cos, sin, bd, scale, diff):
    segsum = _segsum_d if diff else _segsum_raw
    rope = _rope_d if diff else _rope_raw
    qn = q * lax.rsqrt(segsum(q * q, bd) * (1.0 / HEAD_DIM) + NORM_EPS) * g
    return rope(qn, cos, sin) * scale


def _silu(x):
    return x * jax.nn.sigmoid(x)


def _rwkv_pw(k, pw0, pw1, pa0, pa1, w0, a0, k_k, k_a, bd, diff):
    segsum = _segsum_d if diff else _segsum_raw
    kk = k * k_k
    kk = kk * lax.rsqrt(segsum(kk * kk, bd) + L2_EPS)
    ws, kts, akks = [], [], []
    for z, (pw, pa) in enumerate(((pw0, pa0), (pw1, pa1))):
        w = jnp.exp(-DECAY_SCALE * jax.nn.sigmoid(w0[z:z + 1, :] + pw))
        a = jax.nn.sigmoid(a0[z:z + 1, :] + pa)
        ws.append(w)
        kts.append(k * (1.0 + (a - 1.0) * k_a))
        akks.append(a * kk)
    return ws[0], ws[1], kts[0], kts[1], akks[0], akks[1], kk


def _mix_fn(y_att, g_att, ys, r, v, kts, g_rw, gn_w, gn_b, r_k, bd, diff):
    segsum = _segsum_d if diff else _segsum_raw
    mu = segsum(ys, bd) * (1.0 / HEAD_DIM)
    d = ys - mu
    var = segsum(d * d, bd) * (1.0 / HEAD_DIM)
    yn = d * lax.rsqrt(var + GN_EPS) * gn_w + gn_b
    bonus = segsum(r * kts * r_k, bd) * v
    return y_att * _silu(g_att), (yn + bonus) * _silu(g_rw)


def _loss_fn(out, x, tgt, gate, g_post):
    e = x + gate * _rms(out, g_post) - tgt
    s = jnp.sum(e * e, axis=1, keepdims=True)
    return jnp.sum(s, axis=0, keepdims=True) * (0.5 / D_MODEL)


def _exchange(arrays, scatter, name):
    n = len(arrays)
    out_shape = tuple(
        jax.ShapeDtypeStruct((NDEV,) + tuple(a.shape[1:] if sc else a.shape), a.dtype)
        for a, sc in zip(arrays, scatter))

    def body(*refs):
        ins, outs = refs[:n], refs[n:2 * n]
        send_sems, recv_sems, local_sems = refs[2 * n:]
        ix, iy, ic = lax.axis_index("x"), lax.axis_index("y"), lax.axis_index("c")
        me = 4 * ix + 2 * iy + ic

        def src(k, p):
            return ins[k].at[p] if scatter[k] else ins[k]

        local = [pltpu.make_async_copy(src(k, me), outs[k].at[me], local_sems.at[k]) for k in range(n)]
        for cp in local:
            cp.start()
        sends, recvs = [], []
        for m in range(1, NDEV):
            px = 1 - ix if (m >> 2) & 1 else ix
            py = 1 - iy if (m >> 1) & 1 else iy
            pc = 1 - ic if m & 1 else ic
            p = 4 * px + 2 * py + pc
            for k in range(n):
                common = dict(send_sem=send_sems.at[k, m - 1], recv_sem=recv_sems.at[k, m - 1],
                              device_id=(px, py, pc), device_id_type=MESH)
                sends.append(pltpu.make_async_remote_copy(src_ref=src(k, p), dst_ref=outs[k].at[me], **common))
                recvs.append(pltpu.make_async_remote_copy(src_ref=src(k, p), dst_ref=outs[k].at[p], **common))
        for cp in sends:
            cp.start()
        for cp in recvs:
            cp.wait_recv()
        for cp in sends:
            cp.wait_send()
        for cp in local:
            cp.wait()

    any_spec = pl.BlockSpec(memory_space=pl.ANY)
    return pl.pallas_call(
        body, name=name, out_shape=out_shape,
        in_specs=[any_spec] * n, out_specs=tuple([any_spec] * n),
        scratch_shapes=[pltpu.SemaphoreType.DMA((n, NDEV - 1)), pltpu.SemaphoreType.DMA((n, NDEV - 1)),
                        pltpu.SemaphoreType.DMA((n,))],
    )(*arrays)


def _mod_call(c_all, w_ada, b_cols):
    def body(c_ref, w_ref, b_ref, o_ref):
        o_ref[...] = _dot(_silu(c_ref[...]), w_ref[...]) + b_ref[...]

    return pl.pallas_call(body, name="mod_fwd",
                          out_shape=jax.ShapeDtypeStruct((c_all.shape[0], w_ada.shape[1]), F32))(c_all, w_ada, b_cols)


def _wada_grad_call(c_all, dmod_cols):
    def body(c_ref, d_ref, o_ref):
        o_ref[...] = _dot_tn(_silu(c_ref[...]), d_ref[...])

    return pl.pallas_call(body, name="w_ada_grad",
                          out_shape=jax.ShapeDtypeStruct((c_all.shape[1], dmod_cols.shape[1]), F32))(c_all, dmod_cols)


def _full(shape):
    nd = len(shape)
    return pl.BlockSpec(shape, lambda *_: (0,) * nd)


def _in_proj_call(x2, shift, scale, g_pre, w_in, qg, kg, cos, sin, bd, T):
    R = x2.shape[0]
    TT = min(ROW_TILE, T)
    tpe = T // TT

    def body(x_ref, sh_ref, sc_ref, gp_ref, w_ref, qg_ref, kg_ref, cos_ref, sin_ref, bd_ref,
             hb_ref, qr_ref, kpad_ref, vpad_ref, qraw_ref, kraw_ref, gatt_ref, rin_ref, grw_ref):
        h = _pre_fn(x_ref[...], sh_ref[0], sc_ref[0], gp_ref[...])
        hb = h.astype(MXU_DTYPE)
        hb_ref[...] = hb

        def proj(c0, c1):
            return jnp.dot(hb, w_ref[:, c0:c1], preferred_element_type=F32)

        q = proj(C_Q, C_K)
        k = proj(C_K, C_V)
        v = proj(C_V, C_GA)
        gatt_ref[...] = proj(C_GA, C_RIN)
        rin_ref[...] = proj(C_RIN, C_GRW)
        grw_ref[...] = proj(C_GRW, C_END)
        qraw_ref[...] = q
        kraw_ref[...] = k
        cos, sin, bd = cos_ref[...], sin_ref[...], bd_ref[...]
        qr = _qk_fn(q, qg_ref[...], jnp.tile(cos, (1, 4)), jnp.tile(sin, (1, 4)), bd, ATT_SCALE, False)
        qr_ref[...] = qr.astype(MXU_DTYPE)
        kr = _qk_fn(k, kg_ref[...], cos, sin, bd, 1.0, False)
        left = lax.broadcasted_iota(jnp.int32, (1, KV_W), 1) < HEAD_DIM
        for ref, val in ((kpad_ref, kr), (vpad_ref, v)):
            h0l = jnp.where(left, val, 0.0)
            h1r = jnp.where(left, 0.0, val)
            ref[0] = h0l.astype(MXU_DTYPE)
            ref[1] = pltpu.roll(h0l, HEAD_DIM, 1).astype(MXU_DTYPE)
            ref[2] = pltpu.roll(h1r, HEAD_DIM, 1).astype(MXU_DTYPE)
            ref[3] = h1r.astype(MXU_DTYPE)

    row = lambda w: pl.BlockSpec((TT, w), lambda i: (i, 0))
    per_ex = pl.BlockSpec((1, 1, D_MODEL), lambda i: (i // tpe, 0, 0))
    tab = pl.BlockSpec((TT, KV_W), lambda i: (i % tpe, 0))
    pad = pl.BlockSpec((4, TT, KV_W), lambda i: (0, i, 0))
    sds = jax.ShapeDtypeStruct
    return pl.pallas_call(
        body, name="in_proj", grid=(R // TT,),
        in_specs=[row(D_MODEL), per_ex, per_ex, _full((1, D_MODEL)), _full(w_in.shape), _full((1, ATT_W)),
                  _full((1, KV_W)), tab, tab, _full((256, 256))],
        out_specs=(row(D_MODEL), row(ATT_W), pad, pad, row(ATT_W), row(KV_W), row(ATT_W), row(SHIFT_W), row(RWKV_W)),
        out_shape=(sds((R, D_MODEL), MXU_DTYPE), sds((R, ATT_W), MXU_DTYPE), sds((4, R, KV_W), MXU_DTYPE),
                   sds((4, R, KV_W), MXU_DTYPE), sds((R, ATT_W), F32), sds((R, KV_W), F32), sds((R, ATT_W), F32),
                   sds((R, SHIFT_W), F32), sds((R, RWKV_W), F32)),
        compiler_params=_cp(("arbitrary",)),
    )(x2, shift, scale, g_pre, w_in, qg, kg, cos, sin, bd)


def _softmax_rows(s):
    m = jnp.max(s, axis=1, keepdims=True)
    e = jnp.exp(s - m)
    return e / jnp.sum(e, axis=1, keepdims=True)


def _att_specs(T, TQ):
    nq = T // TQ
    qspec = pl.BlockSpec((TQ, KV_W), lambda b, p, i: (b * nq + i, p))
    side = lambda s: pl.BlockSpec((None, T, KV_W), lambda b, p, i: (2 * (p // 2) + s, b, 0))
    return nq, qspec, side


def _att_fwd_call(qr, kpad, vpad, B, T):
    TQ = min(ROW_TILE, T)
    nq, qspec, side = _att_specs(T, TQ)

    def body(q_ref, kl_ref, kr_ref, vl_ref, vr_ref, o_ref):
        q = q_ref[...]
        pa = _softmax_rows(_dot_nt(q, kl_ref[...]))
        pb = _softmax_rows(_dot_nt(q, kr_ref[...]))
        o_ref[...] = _dot(pa, vl_ref[...]) + _dot(pb, vr_ref[...])

    return pl.pallas_call(
        body, name="att_fwd", grid=(B, 4, nq),
        in_specs=[qspec, side(0), side(1), side(0), side(1)], out_specs=qspec,
        out_shape=jax.ShapeDtypeStruct((B * T, ATT_W), F32),
        compiler_params=_cp(("arbitrary",) * 3),
    )(qr, kpad, kpad, vpad, vpad)


def _att_bwd_call(qr, kpad, vpad, d_o, B, T):
    TQ = min(ROW_TILE, T)
    nq, qspec, side = _att_specs(T, TQ)

    def body(q_ref, kl_ref, kr_ref, vl_ref, vr_ref, do_ref, dq_ref, dk_ref, dv_ref):
        i = pl.program_id(2)
        q, do = q_ref[...], do_ref[...]
        left = lax.broadcasted_iota(jnp.int32, (1, KV_W), 1) < HEAD_DIM
        dq = jnp.zeros((TQ, KV_W), F32)
        dk = jnp.zeros((T, KV_W), F32)
        dv = jnp.zeros((T, KV_W), F32)
        for k_ref, v_ref, mask in ((kl_ref, vl_ref, left), (kr_ref, vr_ref, jnp.logical_not(left))):
            kk, vv = k_ref[...], v_ref[...]
            p = _softmax_rows(_dot_nt(q, kk))
            dp = _dot_nt(do, vv)
            ds = p * (dp - jnp.sum(p * dp, axis=1, keepdims=True))
            dq = dq + _dot(ds, kk)
            dk = dk + _dot_tn(ds, jnp.where(mask, q, jnp.zeros_like(q)))
            dv = dv + _dot_tn(p, jnp.where(mask, do, 0.0))
        dq_ref[...] = dq

        @pl.when(i == 0)
        def _():
            dk_ref[...] = dk
            dv_ref[...] = dv

        @pl.when(i > 0)
        def _():
            dk_ref[...] += dk
            dv_ref[...] += dv

    acc = pl.BlockSpec((None, T, KV_W), lambda b, p, i: (p, b, 0))
    sds = jax.ShapeDtypeStruct
    return pl.pallas_call(
        body, name="att_bwd", grid=(B, 4, nq),
        in_specs=[qspec, side(0), side(1), side(0), side(1), qspec], out_specs=(qspec, acc, acc),
        out_shape=(sds((B * T, ATT_W), F32), sds((4, B * T, KV_W), F32), sds((4, B * T, KV_W), F32)),
        compiler_params=_cp(("arbitrary",) * 3),
    )(qr, kpad, kpad, vpad, vpad, d_o)


def _shift_specs(R, T, TT, width):
    tpe = T // TT
    nb8 = R // 8
    cur = pl.BlockSpec((TT, width), lambda i: (i, 0))
    prev = pl.BlockSpec((8, width), lambda i: (jnp.maximum(i * (TT // 8) - 1, 0), 0))
    nxt = pl.BlockSpec((8, width), lambda i: (jnp.minimum((i + 1) * (TT // 8), nb8 - 1), 0))
    return tpe, cur, prev, nxt


def _neighbours(cur, prev8, next8, i, tpe, TT):
    rows = lax.broadcasted_iota(jnp.int32, (TT, 1), 0)
    first = jnp.where(i % tpe == 0, 0.0, 1.0)
    last = jnp.where(i % tpe == tpe - 1, 0.0, 1.0)
    before = jnp.where(rows == 0, prev8[7:8, :] * first, pltpu.roll(cur, 1, 0))
    after = jnp.where(rows == TT - 1, next8[0:1, :] * last, pltpu.roll(cur, TT - 1, 0))
    return before, after


def _shift_fwd_call(x, taps, T):
    R, width = x.shape
    TT = min(ROW_TILE, T)
    tpe, cur, prev, nxt = _shift_specs(R, T, TT, width)

    def body(x_ref, p_ref, n_ref, t_ref, o_ref):
        xc = x_ref[...]
        before, after = _neighbours(xc, p_ref[...], n_ref[...], pl.program_id(0), tpe, TT)
        o_ref[...] = t_ref[0:1, :] * before + t_ref[1:2, :] * xc + t_ref[2:3, :] * after

    return pl.pallas_call(
        body, name="shift_fwd", grid=(R // TT,), in_specs=[cur, prev, nxt, _full(taps.shape)], out_specs=cur,
        out_shape=jax.ShapeDtypeStruct((R, width), F32), compiler_params=_cp(("arbitrary",)),
    )(x, x, x, taps)


def _shift_bwd_call(x, d, taps, T):
    R, width = x.shape
    TT = min(ROW_TILE, T)
    tpe, cur, prev, nxt = _shift_specs(R, T, TT, width)

    def body(x_ref, xp_ref, xn_ref, d_ref, dp_ref, dn_ref, t_ref, dx_ref, dt_ref):
        i = pl.program_id(0)
        xc, dc = x_ref[...], d_ref[...]
        d_before, d_after = _neighbours(dc, dp_ref[...], dn_ref[...], i, tpe, TT)
        dx_ref[...] = t_ref[2:3, :] * d_before + t_ref[1:2, :] * dc + t_ref[0:1, :] * d_after
        x_before, x_after = _neighbours(xc, xp_ref[...], xn_ref[...], i, tpe, TT)
        @pl.when(i == 0)
        def _():
            dt_ref[...] = jnp.zeros_like(dt_ref)

        for j, xs in enumerate((x_before, xc, x_after)):
            dt_ref[j:j + 1, :] += jnp.sum(dc * xs, axis=0, keepdims=True)

    return pl.pallas_call(
        body, name="shift_bwd", grid=(R // TT,),
        in_specs=[cur, prev, nxt, cur, prev, nxt, _full(taps.shape)], out_specs=(cur, _full((8, width))),
        out_shape=(jax.ShapeDtypeStruct((R, width), F32), jax.ShapeDtypeStruct((8, width), F32)),
        compiler_params=_cp(("arbitrary",)),
    )(x, x, x, d, d, d, taps)


def _lora_in(wa):
    lane = lax.broadcasted_iota(jnp.int32, (1, LORA_W), 1)
    return jnp.where(lane < LORA_W // 2, jnp.tanh(wa), wa)


def _rwkv_prep_call(shifted, wup, aup, w0, a0, k_k, k_a, bd, T):
    R = shifted.shape[0]
    TT = min(ROW_TILE, T)

    def body(k_ref, wa_ref, wup_ref, aup_ref, w0_ref, a0_ref, kk_ref, ka_ref, bd_ref, w_o, kt_o, akk_o, kk_o):
        twa = _lora_in(wa_ref[...])
        pre = [_dot(twa, m_ref[z]) for m_ref in (wup_ref, aup_ref) for z in range(2)]
        outs = _rwkv_pw(k_ref[...], pre[0], pre[1], pre[2], pre[3], w0_ref[...], a0_ref[...], kk_ref[...],
                        ka_ref[...], bd_ref[...], False)
        w_o[0], w_o[1], kt_o[0], kt_o[1], akk_o[0], akk_o[1] = outs[:6]
        kk_o[...] = outs[6]

    col = lambda c, w: pl.BlockSpec((TT, w), lambda i: (i, c))
    two = pl.BlockSpec((2, TT, RWKV_W), lambda i: (0, i, 0))
    sds = jax.ShapeDtypeStruct
    return pl.pallas_call(
        body, name="rwkv_prep", grid=(R // TT,),
        in_specs=[col(1, RWKV_W), col(3 * RWKV_W // LORA_W, LORA_W), _full(wup.shape), _full(aup.shape),
                  _full((2, RWKV_W)), _full((2, RWKV_W)), _full((1, RWKV_W)), _full((1, RWKV_W)), _full((256, 256))],
        out_specs=(two, two, two, col(0, RWKV_W)),
        out_shape=(sds((2, R, RWKV_W), F32),) * 3 + (sds((R, RWKV_W), F32),),
        compiler_params=_cp(("arbitrary",)),
    )(shifted, shifted, wup, aup, w0, a0, k_k, k_a, bd)


def _rwkv_prep_bwd_call(shifted, cts, wup, aup, w0, a0, k_k, k_a, bd, T):
    R = shifted.shape[0]
    TT = min(ROW_TILE, T)

    def body(k_ref, wa_ref, dw_ref, dkt_ref, dakk_ref, dkk_ref, dr_ref, dv_ref, dr2_ref, dv2_ref, dkts_ref,
             wup_ref, aup_ref, w0_ref, a0_ref, kk_ref, ka_ref, bd_ref,
             dsh_ref, gwup_ref, gaup_ref, gw0_ref, ga0_ref, gkk_ref, gka_ref):
        i = pl.program_id(0)
        wa = wa_ref[...]
        twa = _lora_in(wa)
        pre = [_dot(twa, m_ref[z]) for m_ref in (wup_ref, aup_ref) for z in range(2)]
        fn = functools.partial(_rwkv_pw, bd=bd_ref[...], diff=True)
        _, vjp = jax.vjp(fn, k_ref[...], pre[0], pre[1], pre[2], pre[3], w0_ref[...], a0_ref[...], kk_ref[...],
                         ka_ref[...])
        dkts = dkts_ref[...]
        dk, dpw0, dpw1, dpa0, dpa1, gw0, ga0, gkk, gka = vjp(
            (dw_ref[0], dw_ref[1], dkt_ref[0] + dkts, dkt_ref[1] + dkts, dakk_ref[0], dakk_ref[1],
             dkk_ref[0] + dkk_ref[1]))
        dtwa = (_dot_nt(dpw0, wup_ref[0]) + _dot_nt(dpw1, wup_ref[1]) + _dot_nt(dpa0, aup_ref[0])
                + _dot_nt(dpa1, aup_ref[1]))
        lane = lax.broadcasted_iota(jnp.int32, (1, LORA_W), 1)
        dsh_ref[:, 0:RWKV_W] = dr_ref[0] + dr_ref[1] + dr2_ref[...]
        dsh_ref[:, RWKV_W:2 * RWKV_W] = dk
        dsh_ref[:, 2 * RWKV_W:3 * RWKV_W] = dv_ref[0] + dv_ref[1] + dv2_ref[...]
        dsh_ref[:, 3 * RWKV_W:] = jnp.where(lane < LORA_W // 2, dtwa * (1.0 - twa * twa), dtwa)
        acc = ((gwup_ref.at[0], _dot_tn(twa, dpw0)), (gwup_ref.at[1], _dot_tn(twa, dpw1)),
               (gaup_ref.at[0], _dot_tn(twa, dpa0)), (gaup_ref.at[1], _dot_tn(twa, dpa1)),
               (gw0_ref, gw0), (ga0_ref, ga0), (gkk_ref, gkk), (gka_ref, gka))

        @pl.when(i == 0)
        def _():
            for ref, val in acc:
                ref[...] = val

        @pl.when(i > 0)
        def _():
            for ref, val in acc:
                ref[...] += val

    col = lambda c, w: pl.BlockSpec((TT, w), lambda i: (i, c))
    two = pl.BlockSpec((2, TT, RWKV_W), lambda i: (0, i, 0))
    one = col(0, RWKV_W)
    sds = jax.ShapeDtypeStruct
    return pl.pallas_call(
        body, name="rwkv_prep_bwd", grid=(R // TT,),
        in_specs=[col(1, RWKV_W), col(3 * RWKV_W // LORA_W, LORA_W), two, two, two, two, two, two, one, one, one,
                  _full(wup.shape), _full(aup.shape), _full((2, RWKV_W)), _full((2, RWKV_W)), _full((1, RWKV_W)),
                  _full((1, RWKV_W)), _full((256, 256))],
        out_specs=(pl.BlockSpec((TT, SHIFT_W), lambda i: (i, 0)), _full(wup.shape), _full(aup.shape),
                   _full((2, RWKV_W)), _full((2, RWKV_W)), _full((1, RWKV_W)), _full((1, RWKV_W))),
        out_shape=(sds((R, SHIFT_W), F32), sds(wup.shape, F32), sds(aup.shape, F32), sds((2, RWKV_W), F32),
                   sds((2, RWKV_W), F32), sds((1, RWKV_W), F32), sds((1, RWKV_W), F32)),
        compiler_params=_cp(("arbitrary",)),
    )(shifted, shifted, *cts, wup, aup, w0, a0, k_k, k_a, bd)


def _colform(row, eye_b, bd):
    hi = row.astype(MXU_DTYPE)
    lo = (row - hi.astype(F32)).astype(MXU_DTYPE)
    return _seg_dot(eye_b * hi, bd) + _seg_dot(eye_b * lo, bd)


def _colsum(x):
    return jnp.sum(x, axis=0, keepdims=True)


def _scan_step(S, w, kt, akk, kk, v, r, eye_b, eye_f, bd, want_y):
    sab = _segsum2(S * kk, bd)
    vb = _colform(v, eye_b, bd)
    Sn = S * w - sab * akk + vb * kt
    y = _colsum(eye_f * _segsum2(Sn * r, bd)) if want_y else None
    return Sn, y, sab, vb


def _scan_specs(B, T, C, nC):
    def blk(z, col, rev):
        idx = (lambda g: (z, 0, nC - 1 - g, col)) if rev else (lambda g: (z, 0, g, col))
        return pl.BlockSpec((None, B, C, RWKV_W), idx)

    def blk3(col, rev):
        idx = (lambda g: (0, nC - 1 - g, col)) if rev else (lambda g: (0, g, col))
        return pl.BlockSpec((B, C, RWKV_W), idx)

    return blk, blk3


def _scan_fwd_call(w, kt, akk, kk, shifted, eye_b, eye_f, bd, B, T):
    C = min(SCAN_CHUNK, T)
    nC = T // C
    blk, blk3 = _scan_specs(B, T, C, nC)

    def body(w0, kt0, akk0, kk0, v0, r0, w1, kt1, akk1, kk1, v1, r1, eb_ref, ef_ref, bd_ref, y0, y1, ck, S):
        @pl.when(pl.program_id(0) == 0)
        def _():
            S[...] = jnp.zeros_like(S)

        ck[...] = S[...]
        dirs = ((w0, kt0, akk0, kk0, v0, r0, y0), (w1, kt1, akk1, kk1, v1, r1, y1))

        def step(s, carry):
            for z in range(2):
                row = s if z == 0 else C - 1 - s
                wr, ktr, akkr, kkr, vr, rr, yr = dirs[z]
                for b in range(B):
                    ld = lambda ref: ref[b, pl.ds(row, 1), :]
                    Sn, y, _, _ = _scan_step(S[z * B + b], ld(wr), ld(ktr), ld(akkr), ld(kkr), ld(vr), ld(rr),
                                             eb_ref[...], ef_ref[...], bd_ref[...], True)
                    S[z * B + b] = Sn
                    yr[b, pl.ds(row, 1), :] = y
            return carry

        lax.fori_loop(0, C, step, 0)

    ins, specs = [], []
    for z, rev in ((0, False), (1, True)):
        ins += [w, kt, akk, kk, shifted, shifted]
        specs += [blk(z, 0, rev), blk(z, 0, rev), blk(z, 0, rev), blk3(0, rev), blk3(2, rev), blk3(0, rev)]
    sds = jax.ShapeDtypeStruct
    return pl.pallas_call(
        body, name="scan_fwd", grid=(nC,),
        in_specs=specs + [_full((HEAD_DIM, RWKV_W)), _full((HEAD_DIM, RWKV_W)), _full((256, 256))],
        out_specs=(blk3(0, False), blk3(0, True), pl.BlockSpec((None, 2 * B, HEAD_DIM, RWKV_W), lambda g: (g, 0, 0, 0))),
        out_shape=(sds((B, T, RWKV_W), F32), sds((B, T, RWKV_W), F32), sds((nC, 2 * B, HEAD_DIM, RWKV_W), F32)),
        scratch_shapes=[pltpu.VMEM((2 * B, HEAD_DIM, RWKV_W), F32)],
        compiler_params=_cp(("arbitrary",)),
    )(*ins, eye_b, eye_f, bd)


def _scan_bwd_call(w, kt, akk, kk, shifted, dys, ck, eye_b, eye_f, bd, B, T):
    C = min(SCAN_CHUNK, T)
    nC = T // C
    blk, blk3 = _scan_specs(B, T, C, nC)
    nin = 7

    def body(*refs):
        d0, d1 = refs[:nin], refs[nin:2 * nin]
        ck_ref, eb_ref, ef_ref, bd_ref = refs[2 * nin:2 * nin + 4]
        outs = refs[2 * nin + 4:2 * nin + 10]
        o0, o1 = refs[2 * nin + 4:2 * nin + 10], refs[2 * nin + 10:2 * nin + 16]
        Sbuf, SAB, VB, G = refs[2 * nin + 16:]
        del outs

        @pl.when(pl.program_id(0) == 0)
        def _():
            G[...] = jnp.zeros_like(G)

        Sbuf[0] = ck_ref[...]
        dirs = (d0 + (o0,), d1 + (o1,))

        def rows(z, s, b):
            row = s if z == 0 else C - 1 - s
            return row, (lambda ref: ref[b, pl.ds(row, 1), :])

        def fwd(s, carry):
            for z in range(2):
                wr, ktr, akkr, kkr, vr, rr, _, _ = dirs[z]
                for b in range(B):
                    _, ld = rows(z, s, b)
                    c = z * B + b
                    Sn, _, sab, vb = _scan_step(Sbuf[s, c], ld(wr), ld(ktr), ld(akkr), ld(kkr), ld(vr), ld(rr),
                                                eb_ref[...], ef_ref[...], bd_ref[...], False)
                    Sbuf[s + 1, c] = Sn
                    SAB[s, c] = sab
                    VB[s, c] = vb
            return carry

        lax.fori_loop(0, C, fwd, 0)

        def bwd(it, carry):
            s = C - 1 - it
            for z in range(2):
                wr, ktr, akkr, kkr, vr, rr, dyr, (dw_o, dkt_o, dakk_o, dkk_o, dr_o, dv_o) = dirs[z]
                for b in range(B):
                    row, ld = rows(z, s, b)
                    c = z * B + b
                    S, Sn, sab, vb = Sbuf[s, c], Sbuf[s + 1, c], SAB[s, c], VB[s, c]
                    dyb = _colform(ld(dyr), eb_ref[...], bd_ref[...])
                    Gc = G[c] + dyb * ld(rr)
                    dsab = -_segsum2(Gc * ld(akkr), bd_ref[...])
                    G[c] = Gc * ld(wr) + dsab * ld(kkr)
                    dvb = _segsum2(Gc * ld(ktr), bd_ref[...])
                    for ref, val in ((dr_o, _colsum(Sn * dyb)), (dkt_o, _colsum(Gc * vb)),
                                     (dv_o, _colsum(ef_ref[...] * dvb)), (dw_o, _colsum(Gc * S)),
                                     (dakk_o, -_colsum(Gc * sab)), (dkk_o, _colsum(dsab * S))):
                        ref[b, pl.ds(row, 1), :] = val
            return carry

        lax.fori_loop(0, C, bwd, 0)

    ins, specs = [], []
    for z, rev in ((0, True), (1, False)):
        ins += [w, kt, akk, kk, shifted, shifted, dys]
        specs += [blk(z, 0, rev), blk(z, 0, rev), blk(z, 0, rev), blk3(0, rev), blk3(2, rev), blk3(0, rev), blk3(0, rev)]
    ins += [ck, eye_b, eye_f, bd]
    specs += [pl.BlockSpec((None, 2 * B, HEAD_DIM, RWKV_W), lambda g: (nC - 1 - g, 0, 0, 0)),
              _full((HEAD_DIM, RWKV_W)), _full((HEAD_DIM, RWKV_W)), _full((256, 256))]
    sds = jax.ShapeDtypeStruct
    out_specs = tuple(blk3(0, True) for _ in range(6)) + tuple(blk3(0, False) for _ in range(6))
    res = pl.pallas_call(
        body, name="scan_bwd", grid=(nC,), in_specs=specs, out_specs=out_specs,
        out_shape=tuple(sds((B, T, RWKV_W), F32) for _ in range(12)),
        scratch_shapes=[pltpu.VMEM((C + 1, 2 * B, HEAD_DIM, RWKV_W), F32), pltpu.VMEM((C, 2 * B, HEAD_DIM, RWKV_W), F32),
                        pltpu.VMEM((C, 2 * B, HEAD_DIM, RWKV_W), F32), pltpu.VMEM((2 * B, HEAD_DIM, RWKV_W), F32)],
        compiler_params=_cp(("arbitrary",)),
    )(*ins)
    return [jnp.stack([res[j], res[6 + j]]) for j in range(6)]


def _out_head_call(x2, tgt2, gate, y_att, g_att, ys, shifted, kt, g_rw, w_out, g_post, gn_w, gn_b, r_k, bd, T):
    R = x2.shape[0]
    TT = min(ROW_TILE, T)
    tpe = T // TT

    def body(x_ref, t_ref, gate_ref, ya_ref, ga_ref, ys_ref, r_ref, v_ref, kt_ref, grw_ref, w_ref, gp_ref, gnw_ref,
             gnb_ref, rk_ref, bd_ref,
             loss_o, dy_o, dya_o, dga_o, dys_o, dr_o, dv_o, dkts_o, dgrw_o, dgate_o, gw_o, ggp_o, ggnw_o, ggnb_o, grk_o):
        i = pl.program_id(0)
        bd = bd_ref[...]
        mix = functools.partial(_mix_fn, bd=bd, diff=True)
        (ma, mr), mix_vjp = jax.vjp(mix, ya_ref[...], ga_ref[...], ys_ref[0] + ys_ref[1], r_ref[...], v_ref[...],
                                    kt_ref[0] + kt_ref[1], grw_ref[...], gnw_ref[...], gnb_ref[...], rk_ref[...])
        out = _dot(ma, w_ref[0:ATT_W, :]) + _dot(mr, w_ref[ATT_W:, :])
        loss, loss_vjp = jax.vjp(_loss_fn, out, x_ref[...], t_ref[...], gate_ref[0], gp_ref[...])
        d_out, dy, _, dgate, dgp = loss_vjp(jnp.ones((1, 1), F32))
        dy_o[...] = dy
        dma = _dot_nt(d_out, w_ref[0:ATT_W, :])
        dmr = _dot_nt(d_out, w_ref[ATT_W:, :])
        dya_o[...], dga_o[...], dys_o[...], dr_o[...], dv_o[...], dkts_o[...], dgrw_o[...], dgnw, dgnb, drk = \
            mix_vjp((dma, dmr))
        gw = jnp.concatenate([_dot_tn(ma, d_out), _dot_tn(mr, d_out)], axis=0)
        acc = ((loss_o, jnp.broadcast_to(loss, (8, 128))), (gw_o, gw), (ggp_o, dgp), (ggnw_o, dgnw), (ggnb_o, dgnb),
               (grk_o, drk))

        @pl.when(i == 0)
        def _():
            for ref, val in acc:
                ref[...] = val

        @pl.when(i > 0)
        def _():
            for ref, val in acc:
                ref[...] += val

        @pl.when(i % tpe == 0)
        def _():
            dgate_o[0] = dgate

        @pl.when(i % tpe > 0)
        def _():
            dgate_o[0] += dgate

    row = lambda w, c=0: pl.BlockSpec((TT, w), lambda i: (i, c))
    two = pl.BlockSpec((2, TT, RWKV_W), lambda i: (0, i, 0))
    per_ex = pl.BlockSpec((1, 1, D_MODEL), lambda i: (i // tpe, 0, 0))
    sds = jax.ShapeDtypeStruct
    r512 = sds((R, RWKV_W), F32)
    return pl.pallas_call(
        body, name="out_head", grid=(R // TT,),
        in_specs=[row(D_MODEL), row(D_MODEL), per_ex, row(ATT_W), row(ATT_W), two, row(RWKV_W, 0), row(RWKV_W, 2), two,
                  row(RWKV_W), _full(w_out.shape), _full((1, D_MODEL)), _full((1, RWKV_W)), _full((1, RWKV_W)),
                  _full((1, RWKV_W)), _full((256, 256))],
        out_specs=(_full((8, 128)), row(D_MODEL), row(ATT_W), row(ATT_W), row(RWKV_W), row(RWKV_W), row(RWKV_W),
                   row(RWKV_W), row(RWKV_W), per_ex, _full((D_MODEL, D_MODEL)), _full((1, D_MODEL)), _full((1, RWKV_W)),
                   _full((1, RWKV_W)), _full((1, RWKV_W))),
        out_shape=(sds((8, 128), F32), sds((R, D_MODEL), F32), r512, r512, r512, r512, r512, r512, r512,
                   sds((R // T, 1, D_MODEL), F32), sds((D_MODEL, D_MODEL), F32), sds((1, D_MODEL), F32),
                   sds((1, RWKV_W), F32), sds((1, RWKV_W), F32), sds((1, RWKV_W), F32)),
        compiler_params=_cp(("arbitrary",)),
    )(x2, tgt2, gate, y_att, g_att, ys, shifted, shifted, kt, g_rw, w_out, g_post, gn_w, gn_b, r_k, bd)


def _in_proj_bwd_call(x2, dy, shift, scale, g_pre, w_in, qg, kg, cos, sin, bd, q_raw, k_raw, dqr, dkp, dvp,
                      d_gatt, d_rin, d_grw, T):
    R = x2.shape[0]
    TT = min(ROW_TILE, T)
    tpe = T // TT

    def body(x_ref, dy_ref, sh_ref, sc_ref, gp_ref, w_ref, qg_ref, kg_ref, cos_ref, sin_ref, bd_ref, q_ref, k_ref,
             dqr_ref, dkp_ref, dvp_ref, dga_ref, drin_ref, dgrw_ref,
             dx_o, dproj_o, dsh_o, dsc_o, ggp_o, gqg_o, gkg_o):
        i = pl.program_id(0)
        cos, sin, bd = cos_ref[...], sin_ref[...], bd_ref[...]
        left = lax.broadcasted_iota(jnp.int32, (1, KV_W), 1) < HEAD_DIM

        def kv_grad(ref):
            a = ref[0] + ref[1]
            b = ref[2] + ref[3]
            return jnp.where(left, a + pltpu.roll(a, HEAD_DIM, 1), b + pltpu.roll(b, HEAD_DIM, 1))

        qfn = functools.partial(_qk_fn, cos=jnp.tile(cos, (1, 4)), sin=jnp.tile(sin, (1, 4)), bd=bd, scale=ATT_SCALE,
                                diff=True)
        _, q_vjp = jax.vjp(qfn, q_ref[...], qg_ref[...])
        dq, gqg = q_vjp(dqr_ref[...])
        kfn = functools.partial(_qk_fn, cos=cos, sin=sin, bd=bd, scale=1.0, diff=True)
        _, k_vjp = jax.vjp(kfn, k_ref[...], kg_ref[...])
        dk, gkg = k_vjp(kv_grad(dkp_ref))
        pieces = ((C_Q, C_K, dq), (C_K, C_V, dk), (C_V, C_GA, kv_grad(dvp_ref)), (C_GA, C_RIN, dga_ref[...]),
                  (C_RIN, C_GRW, drin_ref[...]), (C_GRW, C_END, dgrw_ref[...]))
        dh = jnp.zeros((TT, D_MODEL), F32)
        for c0, c1, val in pieces:
            vb = val.astype(MXU_DTYPE)
            dproj_o[:, c0:c1] = vb
            dh = dh + _dot_nt(vb, w_ref[:, c0:c1])
        _, pre_vjp = jax.vjp(_pre_fn, x_ref[...], sh_ref[0], sc_ref[0], gp_ref[...])
        dx, dsh, dsc, ggp = pre_vjp(dh)
        dx_o[...] = dx + dy_ref[...]
        acc = ((ggp_o, ggp), (gqg_o, gqg), (gkg_o, gkg))

        @pl.when(i == 0)
        def _():
            for ref, val in acc:
                ref[...] = val

        @pl.when(i > 0)
        def _():
            for ref, val in acc:
                ref[...] += val

        @pl.when(i % tpe == 0)
        def _():
            dsh_o[0] = dsh
            dsc_o[0] = dsc

        @pl.when(i % tpe > 0)
        def _():
            dsh_o[0] += dsh
            dsc_o[0] += dsc

    row = lambda w: pl.BlockSpec((TT, w), lambda i: (i, 0))
    per_ex = pl.BlockSpec((1, 1, D_MODEL), lambda i: (i // tpe, 0, 0))
    tab = pl.BlockSpec((TT, KV_W), lambda i: (i % tpe, 0))
    pad = pl.BlockSpec((4, TT, KV_W), lambda i: (0, i, 0))
    sds = jax.ShapeDtypeStruct
    nb = R // T
    return pl.pallas_call(
        body, name="in_proj_bwd", grid=(R // TT,),
        in_specs=[row(D_MODEL), row(D_MODEL), per_ex, per_ex, _full((1, D_MODEL)), _full(w_in.shape), _full((1, ATT_W)),
                  _full((1, KV_W)), tab, tab, _full((256, 256)), row(ATT_W), row(KV_W), row(ATT_W), pad, pad,
                  row(ATT_W), row(SHIFT_W), row(RWKV_W)],
        out_specs=(row(D_MODEL), row(C_END), per_ex, per_ex, _full((1, D_MODEL)), _full((1, ATT_W)), _full((1, KV_W))),
        out_shape=(sds((R, D_MODEL), F32), sds((R, C_END), MXU_DTYPE), sds((nb, 1, D_MODEL), F32),
                   sds((nb, 1, D_MODEL), F32), sds((1, D_MODEL), F32), sds((1, ATT_W), F32), sds((1, KV_W), F32)),
        compiler_params=_cp(("arbitrary",)),
    )(x2, dy, shift, scale, g_pre, w_in, qg, kg, cos, sin, bd, q_raw, k_raw, dqr, dkp, dvp, d_gatt, d_rin, d_grw)


def _w_in_grad_call(hb, dproj, T):
    R = hb.shape[0]
    TT = min(ROW_TILE, T)
    CB = 1152

    def body(h_ref, d_ref, o_ref):
        g = _dot_tn(h_ref[...], d_ref[...])

        @pl.when(pl.program_id(1) == 0)
        def _():
            o_ref[...] = g

        @pl.when(pl.program_id(1) > 0)
        def _():
            o_ref[...] += g

    return pl.pallas_call(
        body, name="w_in_grad", grid=(C_END // CB, R // TT),
        in_specs=[pl.BlockSpec((TT, D_MODEL), lambda j, i: (i, 0)), pl.BlockSpec((TT, CB), lambda j, i: (i, j))],
        out_specs=pl.BlockSpec((D_MODEL, CB), lambda j, i: (0, j)),
        out_shape=jax.ShapeDtypeStruct((D_MODEL, C_END), F32), compiler_params=_cp(("arbitrary", "arbitrary")),
    )(hb, dproj)


def _adam_call(parts, w, m, v, name, row_tile=None):
    P, M, N = parts.shape
    TM = M if row_tile is None else row_tile

    def body(p_ref, w_ref, m_ref, v_ref, g_o, d_o, m_o, v_o):
        g = p_ref[0]
        for j in range(1, P):
            g = g + p_ref[j]
        m2 = ADAM_B1 * m_ref[...] + (1.0 - ADAM_B1) * g
        v2 = ADAM_B2 * v_ref[...] + (1.0 - ADAM_B2) * jnp.square(g)
        m_hat = m2 / (1.0 - ADAM_B1 ** ADAM_STEP)
        v_hat = v2 / (1.0 - ADAM_B2 ** ADAM_STEP)
        g_o[...] = g
        d_o[...] = -ADAM_LR * (m_hat / (jnp.sqrt(v_hat) + ADAM_EPS) + ADAM_WD * w_ref[...])
        m_o[...] = m2
        v_o[...] = v2

    blk = pl.BlockSpec((TM, N), lambda i: (i, 0))
    return pl.pallas_call(
        body, name=name, grid=(M // TM,),
        in_specs=[pl.BlockSpec((P, TM, N), lambda i: (0, i, 0)), blk, blk, blk], out_specs=(blk,) * 4,
        out_shape=(jax.ShapeDtypeStruct((M, N), F32),) * 4, compiler_params=_cp(("arbitrary",)),
    )(parts, w, m, v)


_SMALL_ROWS = 136


def _pack_small(taps, w_up, w0, a_up, a0):
    flat = jnp.concatenate([taps.reshape(-1), w_up.reshape(-1), w0.reshape(-1), a_up.reshape(-1), a0.reshape(-1)])
    return jnp.pad(flat, (0, _SMALL_ROWS * 128 - flat.shape[0])).reshape(_SMALL_ROWS, 128)


def _unpack_small(packed):
    n = packed.shape[0]
    flat = packed.reshape(n, -1)
    out, o = [], 0
    for shape in ((3, 208), (2, 64, 64), (2, 64), (2, 64, 64), (2, 64)):
        size = 1
        for s in shape:
            size *= s
        out.append(flat[:, o:o + size].reshape((n,) + shape))
        o += size
    return out


def _cols_to_full(blocks):
    nd = blocks.ndim
    moved = jnp.moveaxis(blocks, 0, nd - 2)
    return moved.reshape(moved.shape[:-2] + (moved.shape[-2] * moved.shape[-1],))


def _full_to_cols(full):
    k = full.shape[-1] // NDEV
    return jnp.moveaxis(full.reshape(full.shape[:-1] + (NDEV, k)), -2, 0)


_REP_SIZES = (("g_pre", 1024), ("q_norm_g", 64), ("k_norm_g", 64), ("k_k", 512), ("k_a", 512), ("r_k", 512),
              ("gn_w", 512), ("gn_b", 512), ("g_post", 1024))
_REP_ROWS = 40


def kernel(x, c, w_ada, b_ada, g_pre, w_in, q_norm_g, k_norm_g, shift_taps, w_up, w0, a_up, a0, k_k, k_a, r_k, gn_w, gn_b, w_out, g_post, loss_target, m_w_ada, m_b_ada, m_g_pre, m_w_in, m_q_norm_g, m_k_norm_g, m_shift_taps, m_w_up, m_w0, m_a_up, m_a0, m_k_k, m_k_a, m_r_k, m_gn_w, m_gn_b, m_w_out, m_g_post, v_w_ada, v_b_ada, v_g_pre, v_w_in, v_q_norm_g, v_k_norm_g, v_shift_taps, v_w_up, v_w0, v_a_up, v_a0, v_k_k, v_k_a, v_r_k, v_gn_w, v_gn_b, v_w_out, v_g_post):
    B, T, _ = x.shape
    R = B * T
    me = 4 * lax.axis_index("x") + 2 * lax.axis_index("y") + lax.axis_index("c")
    x2 = x.reshape(R, D_MODEL)
    tgt2 = loss_target.reshape(R, D_MODEL)

    seg = jnp.arange(256) // HEAD_DIM
    bd = (seg[:, None] == seg[None, :]).astype(MXU_DTYPE)
    eye = (jnp.arange(HEAD_DIM)[:, None] == (jnp.arange(RWKV_W) % HEAD_DIM)[None, :])
    eye_b, eye_f = eye.astype(MXU_DTYPE), eye.astype(F32)
    cos, sin = _rope_tables(T)

    c_g, w_in_g, w_out_g, small_g = _exchange(
        [c, w_in[0].astype(MXU_DTYPE), w_out[0].astype(MXU_DTYPE),
         _pack_small(shift_taps[0], w_up[0], w0[0], a_up[0], a0[0])], [False] * 4, "gather_params")
    c_all = c_g.reshape(NDEV * B, D_MODEL)
    w_in_f = _cols_to_full(w_in_g)
    w_out_f = w_out_g.reshape(D_MODEL, D_MODEL)
    taps_b, w_up_b, w0_b, a_up_b, a0_b = _unpack_small(small_g)
    taps_f = jnp.pad(_cols_to_full(taps_b), ((0, 5), (0, 0)))
    w_up_f, a_up_f = _cols_to_full(w_up_b), _cols_to_full(a_up_b)
    w0_f, a0_f = _cols_to_full(w0_b), _cols_to_full(a0_b)
    wup_pad = jnp.pad(w_up_f, ((0, 0), (0, 64), (0, 0))).astype(MXU_DTYPE)
    aup_pad = jnp.pad(a_up_f, ((0, 0), (64, 0), (0, 0))).astype(MXU_DTYPE)

    ncol = w_ada.shape[2]
    b_cols = lax.dynamic_slice(b_ada, (0, me * ncol), (1, ncol))
    mod_cols = _mod_call(c_all, w_ada[0].astype(MXU_DTYPE), b_cols)
    (mod_g,) = _exchange([mod_cols], [False], "gather_mod")
    mod = lax.dynamic_slice(_cols_to_full(mod_g), (me * B, 0), (B, 3 * D_MODEL))
    shift, scale, gate = [mod[:, j * D_MODEL:(j + 1) * D_MODEL].reshape(B, 1, D_MODEL) for j in range(3)]

    qg = jnp.tile(q_norm_g, (1, ATT_W // HEAD_DIM))
    kg = jnp.tile(k_norm_g, (1, KV_W // HEAD_DIM))
    rk_row = r_k.reshape(1, RWKV_W)

    hb, qr, kpad, vpad, q_raw, k_raw, g_att, rin, g_rw = _in_proj_call(
        x2, shift, scale, g_pre, w_in_f, qg, kg, cos, sin, bd, T)
    y_att = _att_fwd_call(qr, kpad, vpad, B, T)
    shifted = _shift_fwd_call(rin, taps_f, T)
    w_s, kt_s, akk_s, kk_s = _rwkv_prep_call(shifted, wup_pad, aup_pad, w0_f, a0_f, k_k, k_a, bd, T)
    sh3 = shifted.reshape(B, T, SHIFT_W)
    r4 = lambda a: a.reshape(2, B, T, RWKV_W)
    y0, y1, ck = _scan_fwd_call(r4(w_s), r4(kt_s), r4(akk_s), kk_s.reshape(B, T, RWKV_W), sh3, eye_b, eye_f, bd, B, T)
    ys = jnp.stack([y0.reshape(R, RWKV_W), y1.reshape(R, RWKV_W)])

    (loss_blk, dy, d_yatt, d_gatt, d_ys, d_r2, d_v2, d_kts, d_grw, d_gate, g_wout, g_gpost, g_gnw, g_gnb,
     g_rk) = _out_head_call(x2, tgt2, gate, y_att, g_att, ys, shifted, kt_s, g_rw, w_out_f, g_post, gn_w, gn_b, rk_row,
                            bd, T)
    scan_cts = _scan_bwd_call(r4(w_s), r4(kt_s), r4(akk_s), kk_s.reshape(B, T, RWKV_W), sh3,
                              d_ys.reshape(B, T, RWKV_W), ck, eye_b, eye_f, bd, B, T)
    scan_cts = [a.reshape(2, R, RWKV_W) for a in scan_cts]
    d_shifted, g_wup, g_aup, g_w0, g_a0, g_kk, g_ka = _rwkv_prep_bwd_call(
        shifted, scan_cts + [d_r2, d_v2, d_kts], wup_pad, aup_pad, w0_f, a0_f, k_k, k_a, bd, T)
    d_rin, g_taps = _shift_bwd_call(rin, d_shifted, taps_f, T)
    dqr, dkp, dvp = _att_bwd_call(qr, kpad, vpad, d_yatt, B, T)
    grad_x, dproj, d_shift, d_scale, g_gpre, g_qg, g_kg = _in_proj_bwd_call(
        x2, dy, shift, scale, g_pre, w_in_f, qg, kg, cos, sin, bd, q_raw, k_raw, dqr, dkp, dvp, d_gatt, d_rin, d_grw, T)
    g_win = _w_in_grad_call(hb, dproj, T)

    rep = jnp.concatenate([g_gpre.reshape(-1), g_qg.reshape(-1, HEAD_DIM).sum(0), g_kg.reshape(-1, HEAD_DIM).sum(0),
                           g_kk.reshape(-1), g_ka.reshape(-1), g_rk.reshape(-1), g_gnw.reshape(-1), g_gnb.reshape(-1),
                           g_gpost.reshape(-1)])
    rep = jnp.pad(rep, (0, _REP_ROWS * 128 - rep.shape[0])).reshape(_REP_ROWS, 128)
    dmod = jnp.concatenate([d_shift, d_scale, d_gate], axis=2).reshape(B, 3 * D_MODEL)
    small_parts = jax.vmap(_pack_small)(_full_to_cols(g_taps[:3]), _full_to_cols(g_wup[:, :64, :]), _full_to_cols(g_w0),
                                        _full_to_cols(g_aup[:, 64:, :]), _full_to_cols(g_a0))
    p_win, p_wout, p_small, dmod_g, rep_g = _exchange(
        [_full_to_cols(g_win), g_wout.reshape(NDEV, D_MODEL // NDEV, D_MODEL), small_parts, dmod, rep],
        [True, True, True, False, False], "reduce_grads")
    dmod_all = dmod_g.reshape(NDEV * B, 3 * D_MODEL)
    g_wada = _wada_grad_call(c_all, lax.dynamic_slice(dmod_all, (0, me * ncol), (NDEV * B, ncol)))

    res = {}

    def adam(name, parts, w, m, v, row_tile=None):
        shape = w.shape
        two_d = (-1, shape[-1])
        out = _adam_call(parts.reshape((parts.shape[0],) + w.reshape(two_d).shape), w.reshape(two_d), m.reshape(two_d),
                         v.reshape(two_d), "adam_" + name, row_tile)
        res[name] = [o.reshape(shape) for o in out]

    adam("w_ada", g_wada[None], w_ada, m_w_ada, v_w_ada)
    adam("b_ada", dmod_all.reshape(NDEV * B, 1, 3 * D_MODEL), b_ada, m_b_ada, v_b_ada)
    adam("w_in", p_win, w_in, m_w_in, v_w_in, 128)
    adam("w_out", p_wout, w_out, m_w_out, v_w_out)
    taps_p, wup_p, w0_p, aup_p, a0_p = _unpack_small(p_small)
    adam("shift_taps", taps_p, shift_taps, m_shift_taps, v_shift_taps)
    adam("w_up", wup_p, w_up, m_w_up, v_w_up)
    adam("w0", w0_p, w0, m_w0, v_w0)
    adam("a_up", aup_p, a_up, m_a_up, v_a_up)
    adam("a0", a0_p, a0, m_a0, v_a0)
    rep_flat = rep_g.reshape(NDEV, -1)
    off = 0
    given = dict(g_pre=(g_pre, m_g_pre, v_g_pre), q_norm_g=(q_norm_g, m_q_norm_g, v_q_norm_g),
                 k_norm_g=(k_norm_g, m_k_norm_g, v_k_norm_g), k_k=(k_k, m_k_k, v_k_k), k_a=(k_a, m_k_a, v_k_a),
                 r_k=(r_k, m_r_k, v_r_k), gn_w=(gn_w, m_gn_w, v_gn_w), gn_b=(gn_b, m_gn_b, v_gn_b),
                 g_post=(g_post, m_g_post, v_g_post))
    for name, size in _REP_SIZES:
        adam(name, rep_flat[:, off:off + size], *given[name])
        off += size

    loss = lax.psum(loss_blk[0, 0], ("x", "y", "c"))
    order = ["w_ada", "b_ada", "g_pre", "w_in", "q_norm_g", "k_norm_g", "shift_taps", "w_up", "w0", "a_up", "a0", "k_k",
             "k_a", "r_k", "gn_w", "gn_b", "w_out", "g_post"]
    return (loss, grad_x.reshape(B, T, D_MODEL), *[res[n][0] for n in order], *[res[n][1] for n in order],
            *[res[n][2] for n in order], *[res[n][3] for n in order])
```

```python
import functools

import jax
import jax.numpy as jnp
from jax import lax
from jax.experimental import pallas as pl
from jax.experimental.pallas import tpu as pltpu

F32 = jnp.float32
MXU_DTYPE = jnp.bfloat16
MESH = pl.DeviceIdType.MESH
NDEV = 8

D_MODEL = 1024
HEAD_DIM = 64
ATT_W = 512
KV_W = 128
RWKV_W = 512
LORA_W = 128
SHIFT_W = 3 * RWKV_W + LORA_W
GRID_W = 64
ROPE_THETA = 10000.0
DECAY_SCALE = 0.6065306597126334
NORM_EPS = 1e-6
GN_EPS = 64e-5
L2_EPS = 1e-12
ATT_SCALE = HEAD_DIM ** -0.5
C_Q, C_K, C_V, C_GA, C_RIN, C_GRW, C_END = 0, 512, 640, 768, 1280, 2944, 3456

ADAM_LR, ADAM_B1, ADAM_B2, ADAM_EPS, ADAM_WD, ADAM_STEP = 0.001, 0.9, 0.999, 1e-08, 0.01, 10

ROW_TILE = 256
SCAN_CHUNK = 16
VMEM_LIMIT = 56 * 1024 * 1024


def _cp(sem=None):
    return pltpu.CompilerParams(dimension_semantics=sem, vmem_limit_bytes=VMEM_LIMIT)


def _dot(a, b, dims=(((1,), (0,)), ((), ()))):
    return lax.dot_general(a.astype(MXU_DTYPE), b.astype(MXU_DTYPE), dims, preferred_element_type=F32)


def _dot_nt(a, b):
    return _dot(a, b, (((1,), (1,)), ((), ())))


def _dot_tn(a, b):
    return _dot(a, b, (((0,), (0,)), ((), ())))


def _seg_dot(xb, bd):
    n = xb.shape[1]
    if n <= 256:
        return jnp.dot(xb, bd[:n, :n], preferred_element_type=F32)
    parts = [jnp.dot(xb[:, c:c + 256], bd, preferred_element_type=F32) for c in range(0, n, 256)]
    return jnp.concatenate(parts, axis=1)


def _split3(x):
    hi = x.astype(MXU_DTYPE)
    r1 = x - hi.astype(F32)
    mid = r1.astype(MXU_DTYPE)
    lo = (r1 - mid.astype(F32)).astype(MXU_DTYPE)
    return hi, mid, lo


def _segsum_raw(x, bd):
    hi, mid, lo = _split3(x)
    return _seg_dot(hi, bd) + _seg_dot(mid, bd) + _seg_dot(lo, bd)


@jax.custom_vjp
def _segsum_d(x, bd):
    return _segsum_raw(x, bd)


def _segsum_d_fwd(x, bd):
    return _segsum_raw(x, bd), bd


def _segsum_d_bwd(bd, ct):
    return _segsum_raw(ct, bd), jnp.zeros_like(bd)


_segsum_d.defvjp(_segsum_d_fwd, _segsum_d_bwd)


def _rope_tables(T):
    t = jnp.arange(T, dtype=F32)
    row = jnp.floor(t / GRID_W)
    col = t - row * GRID_W
    n_freq = HEAD_DIM // 4
    inv_freq = ROPE_THETA ** (-jnp.arange(n_freq, dtype=F32) / n_freq)
    d = jnp.arange(HEAD_DIM)
    pos = jnp.where((d < HEAD_DIM // 2)[None, :], row[:, None], col[:, None])
    ang = pos * inv_freq[d % n_freq][None, :]
    sign = jnp.where((d % 32) < 16, -1.0, 1.0).astype(F32)[None, :]
    cos = jnp.cos(ang)
    sin = jnp.sin(ang) * sign
    return jnp.tile(cos, (1, 2)), jnp.tile(sin, (1, 2))


def _rope_raw(x, cos, sin):
    n = x.shape[1]
    lane = lax.broadcasted_iota(jnp.int32, (1, n), 1)
    first = (lane % 32) < 16
    partner = jnp.where(first, pltpu.roll(x, n - 16, 1), pltpu.roll(x, 16, 1))
    return x * cos + partner * sin


@jax.custom_vjp
def _rope_d(x, cos, sin):
    return _rope_raw(x, cos, sin)


def _rope_d_fwd(x, cos, sin):
    return _rope_raw(x, cos, sin), (cos, sin)


def _rope_d_bwd(res, ct):
    cos, sin = res
    return _rope_raw(ct, cos, -sin), jnp.zeros_like(cos), jnp.zeros_like(sin)


_rope_d.defvjp(_rope_d_fwd, _rope_d_bwd)


def _rms(x, g):
    return x * lax.rsqrt(jnp.mean(x * x, axis=-1, keepdims=True) + NORM_EPS) * g


def _pre_fn(x, shift, scale, g_pre):
    return _rms(x, g_pre) * (1.0 + scale) + shift


def _qk_fn(q, g, cos, sin, bd, scale, diff):
    segsum = _segsum_d if diff else _segsum_raw
    rope = _rope_d if diff else _rope_raw
    qn = q * lax.rsqrt(segsum(q * q, bd) * (1.0 / HEAD_DIM) + NORM_EPS) * g
    return rope(qn, cos, sin) * scale


def _silu(x):
    return x * jax.nn.sigmoid(x)


def _rwkv_pw(k, pw0, pw1, pa0, pa1, w0, a0, k_k, k_a, bd, diff):
    segsum = _segsum_d if diff else _segsum_raw
    kk = k * k_k
    kk = kk * lax.rsqrt(segsum(kk * kk, bd) + L2_EPS)
    ws, kts, akks = [], [], []
    for z, (pw, pa) in enumerate(((pw0, pa0), (pw1, pa1))):
        w = jnp.exp(-DECAY_SCALE * jax.nn.sigmoid(w0[z:z + 1, :] + pw))
        a = jax.nn.sigmoid(a0[z:z + 1, :] + pa)
        ws.append(w)
        kts.append(k * (1.0 + (a - 1.0) * k_a))
        akks.append(a * kk)
    return ws[0], ws[1], kts[0], kts[1], akks[0], akks[1], kk


def _mix_fn(y_att, g_att, ys, r, v, kts, g_rw, gn_w, gn_b, r_k, bd, diff):
    segsum = _segsum_d if diff else _segsum_raw
    mu = segsum(ys, bd) * (1.0 / HEAD_DIM)
    d = ys - mu
    var = segsum(d * d, bd) * (1.0 / HEAD_DIM)
    yn = d * lax.rsqrt(var + GN_EPS) * gn_w + gn_b
    bonus = segsum(r * kts * r_k, bd) * v
    return y_att * _silu(g_att), (yn + bonus) * _silu(g_rw)


def _loss_fn(out, x, tgt, gate, g_post):
    e = x + gate * _rms(out, g_post) - tgt
    s = jnp.sum(e * e, axis=1, keepdims=True)
    return jnp.sum(s, axis=0, keepdims=True) * (0.5 / D_MODEL)


def _exchange(arrays, scatter, name):
    n = len(arrays)
    out_shape = tuple(
        jax.ShapeDtypeStruct((NDEV,) + tuple(a.shape[1:] if sc else a.shape), a.dtype)
        for a, sc in zip(arrays, scatter))

    def body(*refs):
        ins, outs = refs[:n], refs[n:2 * n]
        send_sems, recv_sems, local_sems = refs[2 * n:]
        ix, iy, ic = lax.axis_index("x"), lax.axis_index("y"), lax.axis_index("c")
        me = 4 * ix + 2 * iy + ic

        def src(k, p):
            return ins[k].at[p] if scatter[k] else ins[k]

        local = [pltpu.make_async_copy(src(k, me), outs[k].at[me], local_sems.at[k]) for k in range(n)]
        for cp in local:
            cp.start()
        sends, recvs = [], []
        for m in range(1, NDEV):
            px = 1 - ix if (m >> 2) & 1 else ix
            py = 1 - iy if (m >> 1) & 1 else iy
            pc = 1 - ic if m & 1 else ic
            p = 4 * px + 2 * py + pc
            for k in range(n):
                common = dict(send_sem=send_sems.at[k, m - 1], recv_sem=recv_sems.at[k, m - 1],
                              device_id=(px, py, pc), device_id_type=MESH)
                sends.append(pltpu.make_async_remote_copy(src_ref=src(k, p), dst_ref=outs[k].at[me], **common))
                recvs.append(pltpu.make_async_remote_copy(src_ref=src(k, p), dst_ref=outs[k].at[p], **common))
        for cp in sends:
            cp.start()
        for cp in recvs:
            cp.wait_recv()
        for cp in sends:
            cp.wait_send()
        for cp in local:
            cp.wait()

    any_spec = pl.BlockSpec(memory_space=pl.ANY)
    return pl.pallas_call(
        body, name=name, out_shape=out_shape,
        in_specs=[any_spec] * n, out_specs=tuple([any_spec] * n),
        scratch_shapes=[pltpu.SemaphoreType.DMA((n, NDEV - 1)), pltpu.SemaphoreType.DMA((n, NDEV - 1)),
                        pltpu.SemaphoreType.DMA((n,))],
    )(*arrays)


def _mod_call(c_all, w_ada, b_cols):
    def body(c_ref, w_ref, b_ref, o_ref):
        o_ref[...] = _dot(_silu(c_ref[...]), w_ref[...]) + b_ref[...]

    return pl.pallas_call(body, name="mod_fwd",
                          out_shape=jax.ShapeDtypeStruct((c_all.shape[0], w_ada.shape[1]), F32))(c_all, w_ada, b_cols)


def _wada_grad_call(c_all, dmod_cols):
    def body(c_ref, d_ref, o_ref):
        o_ref[...] = _dot_tn(_silu(c_ref[...]), d_ref[...])

    return pl.pallas_call(body, name="w_ada_grad",
                          out_shape=jax.ShapeDtypeStruct((c_all.shape[1], dmod_cols.shape[1]), F32))(c_all, dmod_cols)


def _full(shape):
    nd = len(shape)
    return pl.BlockSpec(shape, lambda *_: (0,) * nd)


def _in_proj_call(x2, shift, scale, g_pre, w_in, qg, kg, cos, sin, bd, T):
    R = x2.shape[0]
    TT = min(ROW_TILE, T)
    tpe = T // TT

    def body(x_ref, sh_ref, sc_ref, gp_ref, w_ref, qg_ref, kg_ref, cos_ref, sin_ref, bd_ref,
             hb_ref, qr_ref, kpad_ref, vpad_ref, qraw_ref, kraw_ref, gatt_ref, rin_ref, grw_ref):
        h = _pre_fn(x_ref[...], sh_ref[0], sc_ref[0], gp_ref[...])
        hb = h.astype(MXU_DTYPE)
        hb_ref[...] = hb

        def proj(c0, c1):
            return jnp.dot(hb, w_ref[:, c0:c1], preferred_element_type=F32)

        q = proj(C_Q, C_K)
        k = proj(C_K, C_V)
        v = proj(C_V, C_GA)
        gatt_ref[...] = proj(C_GA, C_RIN)
        rin_ref[...] = proj(C_RIN, C_GRW)
        grw_ref[...] = proj(C_GRW, C_END)
        qraw_ref[...] = q
        kraw_ref[...] = k
        cos, sin, bd = cos_ref[...], sin_ref[...], bd_ref[...]
        qr = _qk_fn(q, qg_ref[...], jnp.tile(cos, (1, 4)), jnp.tile(sin, (1, 4)), bd, ATT_SCALE, False)
        qr_ref[...] = qr.astype(MXU_DTYPE)
        kr = _qk_fn(k, kg_ref[...], cos, sin, bd, 1.0, False)
        left = lax.broadcasted_iota(jnp.int32, (1, KV_W), 1) < HEAD_DIM
        for ref, val in ((kpad_ref, kr), (vpad_ref, v)):
            h0l = jnp.where(left, val, 0.0)
            h1r = jnp.where(left, 0.0, val)
            ref[0] = h0l.astype(MXU_DTYPE)
            ref[1] = pltpu.roll(h0l, HEAD_DIM, 1).astype(MXU_DTYPE)
            ref[2] = pltpu.roll(h1r, HEAD_DIM, 1).astype(MXU_DTYPE)
            ref[3] = h1r.astype(MXU_DTYPE)

    row = lambda w: pl.BlockSpec((TT, w), lambda i: (i, 0))
    per_ex = pl.BlockSpec((1, 1, D_MODEL), lambda i: (i // tpe, 0, 0))
    tab = pl.BlockSpec((TT, KV_W), lambda i: (i % tpe, 0))
    pad = pl.BlockSpec((4, TT, KV_W), lambda i: (0, i, 0))
    sds = jax.ShapeDtypeStruct
    return pl.pallas_call(
        body, name="in_proj", grid=(R // TT,),
        in_specs=[row(D_MODEL), per_ex, per_ex, _full((1, D_MODEL)), _full(w_in.shape), _full((1, ATT_W)),
                  _full((1, KV_W)), tab, tab, _full((256, 256))],
        out_specs=(row(D_MODEL), row(ATT_W), pad, pad, row(ATT_W), row(KV_W), row(ATT_W), row(SHIFT_W), row(RWKV_W)),
        out_shape=(sds((R, D_MODEL), MXU_DTYPE), sds((R, ATT_W), MXU_DTYPE), sds((4, R, KV_W), MXU_DTYPE),
                   sds((4, R, KV_W), MXU_DTYPE), sds((R, ATT_W), F32), sds((R, KV_W), F32), sds((R, ATT_W), F32),
                   sds((R, SHIFT_W), F32), sds((R, RWKV_W), F32)),
        compiler_params=_cp(("arbitrary",)),
    )(x2, shift, scale, g_pre, w_in, qg, kg, cos, sin, bd)


def _softmax_rows(s):
    m = jnp.max(s, axis=1, keepdims=True)
    e = jnp.exp(s - m)
    return e / jnp.sum(e, axis=1, keepdims=True)


def _att_specs(T, TQ):
    nq = T // TQ
    qspec = pl.BlockSpec((TQ, KV_W), lambda b, p, i: (b * nq + i, p))
    side = lambda s: pl.BlockSpec((None, T, KV_W), lambda b, p, i: (2 * (p // 2) + s, b, 0))
    return nq, qspec, side


def _att_fwd_call(qr, kpad, vpad, B, T):
    TQ = min(ROW_TILE, T)
    nq, qspec, side = _att_specs(T, TQ)

    def body(q_ref, kl_ref, kr_ref, vl_ref, vr_ref, o_ref):
        q = q_ref[...]
        pa = _softmax_rows(_dot_nt(q, kl_ref[...]))
        pb = _softmax_rows(_dot_nt(q, kr_ref[...]))
        o_ref[...] = _dot(pa, vl_ref[...]) + _dot(pb, vr_ref[...])

    return pl.pallas_call(
        body, name="att_fwd", grid=(B, 4, nq),
        in_specs=[qspec, side(0), side(1), side(0), side(1)], out_specs=qspec,
        out_shape=jax.ShapeDtypeStruct((B * T, ATT_W), F32),
        compiler_params=_cp(("arbitrary",) * 3),
    )(qr, kpad, kpad, vpad, vpad)


def _att_bwd_call(qr, kpad, vpad, d_o, B, T):
    TQ = min(ROW_TILE, T)
    nq, qspec, side = _att_specs(T, TQ)

    def body(q_ref, kl_ref, kr_ref, vl_ref, vr_ref, do_ref, dq_ref, dk_ref, dv_ref):
        i = pl.program_id(2)
        q, do = q_ref[...], do_ref[...]
        left = lax.broadcasted_iota(jnp.int32, (1, KV_W), 1) < HEAD_DIM
        dq = jnp.zeros((TQ, KV_W), F32)
        dk = jnp.zeros((T, KV_W), F32)
        dv = jnp.zeros((T, KV_W), F32)
        for k_ref, v_ref, mask in ((kl_ref, vl_ref, left), (kr_ref, vr_ref, jnp.logical_not(left))):
            kk, vv = k_ref[...], v_ref[...]
            p = _softmax_rows(_dot_nt(q, kk))
            dp = _dot_nt(do, vv)
            ds = p * (dp - jnp.sum(p * dp, axis=1, keepdims=True))
            dq = dq + _dot(ds, kk)
            dk = dk + _dot_tn(ds, jnp.where(mask, q, jnp.zeros_like(q)))
            dv = dv + _dot_tn(p, jnp.where(mask, do, 0.0))
        dq_ref[...] = dq

        @pl.when(i == 0)
        def _():
            dk_ref[...] = dk
            dv_ref[...] = dv

        @pl.when(i > 0)
        def _():
            dk_ref[...] += dk
            dv_ref[...] += dv

    acc = pl.BlockSpec((None, T, KV_W), lambda b, p, i: (p, b, 0))
    sds = jax.ShapeDtypeStruct
    return pl.pallas_call(
        body, name="att_bwd", grid=(B, 4, nq),
        in_specs=[qspec, side(0), side(1), side(0), side(1), qspec], out_specs=(qspec, acc, acc),
        out_shape=(sds((B * T, ATT_W), F32), sds((4, B * T, KV_W), F32), sds((4, B * T, KV_W), F32)),
        compiler_params=_cp(("arbitrary",) * 3),
    )(qr, kpad, kpad, vpad, vpad, d_o)


def _shift_specs(R, T, TT, width):
    tpe = T // TT
    nb8 = R // 8
    cur = pl.BlockSpec((TT, width), lambda i: (i, 0))
    prev = pl.BlockSpec((8, width), lambda i: (jnp.maximum(i * (TT // 8) - 1, 0), 0))
    nxt = pl.BlockSpec((8, width), lambda i: (jnp.minimum((i + 1) * (TT // 8), nb8 - 1), 0))
    return tpe, cur, prev, nxt


def _neighbours(cur, prev8, next8, i, tpe, TT):
    rows = lax.broadcasted_iota(jnp.int32, (TT, 1), 0)
    first = jnp.where(i % tpe == 0, 0.0, 1.0)
    last = jnp.where(i % tpe == tpe - 1, 0.0, 1.0)
    before = jnp.where(rows == 0, prev8[7:8, :] * first, pltpu.roll(cur, 1, 0))
    after = jnp.where(rows == TT - 1, next8[0:1, :] * last, pltpu.roll(cur, TT - 1, 0))
    return before, after


def _shift_fwd_call(x, taps, T):
    R, width = x.shape
    TT = min(ROW_TILE, T)
    tpe, cur, prev, nxt = _shift_specs(R, T, TT, width)

    def body(x_ref, p_ref, n_ref, t_ref, o_ref):
        xc = x_ref[...]
        before, after = _neighbours(xc, p_ref[...], n_ref[...], pl.program_id(0), tpe, TT)
        o_ref[...] = t_ref[0:1, :] * before + t_ref[1:2, :] * xc + t_ref[2:3, :] * after

    return pl.pallas_call(
        body, name="shift_fwd", grid=(R // TT,), in_specs=[cur, prev, nxt, _full(taps.shape)], out_specs=cur,
        out_shape=jax.ShapeDtypeStruct((R, width), F32), compiler_params=_cp(("arbitrary",)),
    )(x, x, x, taps)


def _shift_bwd_call(x, d, taps, T):
    R, width = x.shape
    TT = min(ROW_TILE, T)
    tpe, cur, prev, nxt = _shift_specs(R, T, TT, width)

    def body(x_ref, xp_ref, xn_ref, d_ref, dp_ref, dn_ref, t_ref, dx_ref, dt_ref):
        i = pl.program_id(0)
        xc, dc = x_ref[...], d_ref[...]
        d_before, d_after = _neighbours(dc, dp_ref[...], dn_ref[...], i, tpe, TT)
        dx_ref[...] = t_ref[2:3, :] * d_before + t_ref[1:2, :] * dc + t_ref[0:1, :] * d_after
        x_before, x_after = _neighbours(xc, xp_ref[...], xn_ref[...], i, tpe, TT)
        @pl.when(i == 0)
        def _():
            dt_ref[...] = jnp.zeros_like(dt_ref)

        for j, xs in enumerate((x_before, xc, x_after)):
            dt_ref[j:j + 1, :] += jnp.sum(dc * xs, axis=0, keepdims=True)

    return pl.pallas_call(
        body, name="shift_bwd", grid=(R // TT,),
        in_specs=[cur, prev, nxt, cur, prev, nxt, _full(taps.shape)], out_specs=(cur, _full((8, width))),
        out_shape=(jax.ShapeDtypeStruct((R, width), F32), jax.ShapeDtypeStruct((8, width), F32)),
        compiler_params=_cp(("arbitrary",)),
    )(x, x, x, d, d, d, taps)


def _lora_in(wa):
    lane = lax.broadcasted_iota(jnp.int32, (1, LORA_W), 1)
    return jnp.where(lane < LORA_W // 2, jnp.tanh(wa), wa)


def _rwkv_prep_call(shifted, wup, aup, w0, a0, k_k, k_a, bd, T):
    R = shifted.shape[0]
    TT = min(ROW_TILE, T)

    def body(k_ref, wa_ref, wup_ref, aup_ref, w0_ref, a0_ref, kk_ref, ka_ref, bd_ref, w_o, kt_o, akk_o, kk_o):
        twa = _lora_in(wa_ref[...])
        pre = [_dot(twa, m_ref[z]) for m_ref in (wup_ref, aup_ref) for z in range(2)]
        outs = _rwkv_pw(k_ref[...], pre[0], pre[1], pre[2], pre[3], w0_ref[...], a0_ref[...], kk_ref[...],
                        ka_ref[...], bd_ref[...], False)
        w_o[0], w_o[1], kt_o[0], kt_o[1], akk_o[0], akk_o[1] = outs[:6]
        kk_o[...] = outs[6]

    col = lambda c, w: pl.BlockSpec((TT, w), lambda i: (i, c))
    two = pl.BlockSpec((2, TT, RWKV_W), lambda i: (0, i, 0))
    sds = jax.ShapeDtypeStruct
    return pl.pallas_call(
        body, name="rwkv_prep", grid=(R // TT,),
        in_specs=[col(1, RWKV_W), col(3 * RWKV_W // LORA_W, LORA_W), _full(wup.shape), _full(aup.shape),
                  _full((2, RWKV_W)), _full((2, RWKV_W)), _full((1, RWKV_W)), _full((1, RWKV_W)), _full((256, 256))],
        out_specs=(two, two, two, col(0, RWKV_W)),
        out_shape=(sds((2, R, RWKV_W), F32),) * 3 + (sds((R, RWKV_W), F32),),
        compiler_params=_cp(("arbitrary",)),
    )(shifted, shifted, wup, aup, w0, a0, k_k, k_a, bd)


def _rwkv_prep_bwd_call(shifted, cts, wup, aup, w0, a0, k_k, k_a, bd, T):
    R = shifted.shape[0]
    TT = min(ROW_TILE, T)

    def body(k_ref, wa_ref, dw_ref, dkt_ref, dakk_ref, dkk_ref, dr_ref, dv_ref, dr2_ref, dv2_ref, dkts_ref,
             wup_ref, aup_ref, w0_ref, a0_ref, kk_ref, ka_ref, bd_ref,
             dsh_ref, gwup_ref, gaup_ref, gw0_ref, ga0_ref, gkk_ref, gka_ref):
        i = pl.program_id(0)
        wa = wa_ref[...]
        twa = _lora_in(wa)
        pre = [_dot(twa, m_ref[z]) for m_ref in (wup_ref, aup_ref) for z in range(2)]
        fn = functools.partial(_rwkv_pw, bd=bd_ref[...], diff=True)
        _, vjp = jax.vjp(fn, k_ref[...], pre[0], pre[1], pre[2], pre[3], w0_ref[...], a0_ref[...], kk_ref[...],
                         ka_ref[...])
        dkts = dkts_ref[...]
        dk, dpw0, dpw1, dpa0, dpa1, gw0, ga0, gkk, gka = vjp(
            (dw_ref[0], dw_ref[1], dkt_ref[0] + dkts, dkt_ref[1] + dkts, dakk_ref[0], dakk_ref[1],
             dkk_ref[0] + dkk_ref[1]))
        dtwa = (_dot_nt(dpw0, wup_ref[0]) + _dot_nt(dpw1, wup_ref[1]) + _dot_nt(dpa0, aup_ref[0])
                + _dot_nt(dpa1, aup_ref[1]))
        lane = lax.broadcasted_iota(jnp.int32, (1, LORA_W), 1)
        dsh_ref[:, 0:RWKV_W] = dr_ref[0] + dr_ref[1] + dr2_ref[...]
        dsh_ref[:, RWKV_W:2 * RWKV_W] = dk
        dsh_ref[:, 2 * RWKV_W:3 * RWKV_W] = dv_ref[0] + dv_ref[1] + dv2_ref[...]
        dsh_ref[:, 3 * RWKV_W:] = jnp.where(lane < LORA_W // 2, dtwa * (1.0 - twa * twa), dtwa)
        acc = ((gwup_ref.at[0], _dot_tn(twa, dpw0)), (gwup_ref.at[1], _dot_tn(twa, dpw1)),
               (gaup_ref.at[0], _dot_tn(twa, dpa0)), (gaup_ref.at[1], _dot_tn(twa, dpa1)),
               (gw0_ref, gw0), (ga0_ref, ga0), (gkk_ref, gkk), (gka_ref, gka))

        @pl.when(i == 0)
        def _():
            for ref, val in acc:
                ref[...] = val

        @pl.when(i > 0)
        def _():
            for ref, val in acc:
                ref[...] += val

    col = lambda c, w: pl.BlockSpec((TT, w), lambda i: (i, c))
    two = pl.BlockSpec((2, TT, RWKV_W), lambda i: (0, i, 0))
    one = col(0, RWKV_W)
    sds = jax.ShapeDtypeStruct
    return pl.pallas_call(
        body, name="rwkv_prep_bwd", grid=(R // TT,),
        in_specs=[col(1, RWKV_W), col(3 * RWKV_W // LORA_W, LORA_W), two, two, two, two, two, two, one, one, one,
                  _full(wup.shape), _full(aup.shape), _full((2, RWKV_W)), _full((2, RWKV_W)), _full((1, RWKV_W)),
                  _full((1, RWKV_W)), _full((256, 256))],
        out_specs=(pl.BlockSpec((TT, SHIFT_W), lambda i: (i, 0)), _full(wup.shape), _full(aup.shape),
                   _full((2, RWKV_W)), _full((2, RWKV_W)), _full((1, RWKV_W)), _full((1, RWKV_W))),
        out_shape=(sds((R, SHIFT_W), F32), sds(wup.shape, F32), sds(aup.shape, F32), sds((2, RWKV_W), F32),
                   sds((2, RWKV_W), F32), sds((1, RWKV_W), F32), sds((1, RWKV_W), F32)),
        compiler_params=_cp(("arbitrary",)),
    )(shifted, shifted, *cts, wup, aup, w0, a0, k_k, k_a, bd)


def _col_lhs(row, eye_b):
    return eye_b * row.astype(MXU_DTYPE)


def _hi_lo(x):
    hi = x.astype(MXU_DTYPE)
    return hi, (x - hi.astype(F32)).astype(MXU_DTYPE)


def _colsum(x):
    return jnp.sum(x, axis=0, keepdims=True)


def _stacked_segsum(tiles, bd):
    res = _seg_dot(jnp.concatenate(tiles, axis=0), bd)
    return [res[j * HEAD_DIM:(j + 1) * HEAD_DIM] for j in range(len(tiles))]


def _scan_specs(B, T, C, nC):
    def blk(z, col, rev):
        idx = (lambda g: (z, 0, nC - 1 - g, col)) if rev else (lambda g: (z, 0, g, col))
        return pl.BlockSpec((None, B, C, RWKV_W), idx)

    def blk3(col, rev):
        idx = (lambda g: (0, nC - 1 - g, col)) if rev else (lambda g: (0, g, col))
        return pl.BlockSpec((B, C, RWKV_W), idx)

    return blk, blk3


def _scan_fwd_call(w, kt, akk, kk, shifted, eye_b, eye_f, bd, B, T):
    C = min(SCAN_CHUNK, T)
    nC = T // C
    blk, blk3 = _scan_specs(B, T, C, nC)

    def body(w0, kt0, akk0, kk0, v0, r0, w1, kt1, akk1, kk1, v1, r1, eb_ref, ef_ref, bd_ref, y0, y1, ck, S):
        @pl.when(pl.program_id(0) == 0)
        def _():
            S[...] = jnp.zeros_like(S)

        ck[...] = S[...]
        dirs = ((w0, kt0, akk0, kk0, v0, r0, y0), (w1, kt1, akk1, kk1, v1, r1, y1))

        def step(s, carry):
            for z in range(2):
                row = s if z == 0 else C - 1 - s
                prev = jnp.maximum(s - 1, 0) if z == 0 else jnp.minimum(C - s, C - 1)
                wr, ktr, akkr, kkr, vr, rr, yr = dirs[z]
                tiles = []
                for b in range(B):
                    Sc = S[z * B + b]
                    tiles += [*_hi_lo(Sc * kkr[b, pl.ds(row, 1), :]), _col_lhs(vr[b, pl.ds(row, 1), :], eb_ref[...]),
                              (Sc * rr[b, pl.ds(prev, 1), :]).astype(MXU_DTYPE)]
                res = _stacked_segsum(tiles, bd_ref[...])
                for b in range(B):
                    sab_hi, sab_lo, vb, yb = res[4 * b:4 * b + 4]
                    ld = lambda ref: ref[b, pl.ds(row, 1), :]
                    S[z * B + b] = S[z * B + b] * ld(wr) - (sab_hi + sab_lo) * ld(akkr) + vb * ld(ktr)
                    yr[b, pl.ds(prev, 1), :] = _colsum(ef_ref[...] * yb)
            return carry

        lax.fori_loop(0, C, step, 0)
        for z in range(2):
            last = C - 1 if z == 0 else 0
            rr, yr = dirs[z][5], dirs[z][6]
            res = _stacked_segsum([(S[z * B + b] * rr[b, last:last + 1, :]).astype(MXU_DTYPE) for b in range(B)],
                                  bd_ref[...])
            for b in range(B):
                yr[b, last:last + 1, :] = _colsum(ef_ref[...] * res[b])

    ins, specs = [], []
    for z, rev in ((0, False), (1, True)):
        ins += [w, kt, akk, kk, shifted, shifted]
        specs += [blk(z, 0, rev), blk(z, 0, rev), blk(z, 0, rev), blk3(0, rev), blk3(2, rev), blk3(0, rev)]
    sds = jax.ShapeDtypeStruct
    return pl.pallas_call(
        body, name="scan_fwd", grid=(nC,),
        in_specs=specs + [_full((HEAD_DIM, RWKV_W)), _full((HEAD_DIM, RWKV_W)), _full((256, 256))],
        out_specs=(blk3(0, False), blk3(0, True), pl.BlockSpec((None, 2 * B, HEAD_DIM, RWKV_W), lambda g: (g, 0, 0, 0))),
        out_shape=(sds((B, T, RWKV_W), F32), sds((B, T, RWKV_W), F32), sds((nC, 2 * B, HEAD_DIM, RWKV_W), F32)),
        scratch_shapes=[pltpu.VMEM((2 * B, HEAD_DIM, RWKV_W), F32)],
        compiler_params=_cp(("arbitrary",)),
    )(*ins, eye_b, eye_f, bd)


def _scan_bwd_call(w, kt, akk, kk, shifted, dys, ck, eye_b, eye_f, bd, B, T):
    C = min(SCAN_CHUNK, T)
    nC = T // C
    blk, blk3 = _scan_specs(B, T, C, nC)
    nin = 7

    def body(*refs):
        d0, d1 = refs[:nin], refs[nin:2 * nin]
        ck_ref, eb_ref, ef_ref, bd_ref = refs[2 * nin:2 * nin + 4]
        o0, o1 = refs[2 * nin + 4:2 * nin + 10], refs[2 * nin + 10:2 * nin + 16]
        Sbuf, SAB, VB, DYB, G = refs[2 * nin + 16:]

        @pl.when(pl.program_id(0) == 0)
        def _():
            G[...] = jnp.zeros_like(G)

        Sbuf[0] = ck_ref[...]
        dirs = (d0 + (o0,), d1 + (o1,))

        def fwd(s, carry):
            for z in range(2):
                row = s if z == 0 else C - 1 - s
                wr, ktr, akkr, kkr, vr, rr, dyr, _ = dirs[z]
                tiles = []
                for b in range(B):
                    tiles += [*_hi_lo(Sbuf[s, z * B + b] * kkr[b, pl.ds(row, 1), :]),
                              _col_lhs(vr[b, pl.ds(row, 1), :], eb_ref[...]),
                              _col_lhs(dyr[b, pl.ds(row, 1), :], eb_ref[...])]
                res = _stacked_segsum(tiles, bd_ref[...])
                for b in range(B):
                    c = z * B + b
                    sab_hi, sab_lo, vb, dyb = res[4 * b:4 * b + 4]
                    sab = sab_hi + sab_lo
                    ld = lambda ref: ref[b, pl.ds(row, 1), :]
                    Sbuf[s + 1, c] = Sbuf[s, c] * ld(wr) - sab * ld(akkr) + vb * ld(ktr)
                    SAB[s, c] = sab
                    VB[s, c] = vb
                    DYB[s, c] = dyb
            return carry

        lax.fori_loop(0, C, fwd, 0)

        def bwd(it, carry):
            s = C - 1 - it
            for z in range(2):
                row = s if z == 0 else C - 1 - s
                wr, ktr, akkr, kkr, vr, rr, dyr, (dw_o, dkt_o, dakk_o, dkk_o, dr_o, dv_o) = dirs[z]
                tiles, Gcs = [], []
                for b in range(B):
                    c = z * B + b
                    Gc = G[c] + DYB[s, c] * rr[b, pl.ds(row, 1), :]
                    Gcs.append(Gc)
                    tiles += [*_hi_lo(Gc * akkr[b, pl.ds(row, 1), :]), (Gc * ktr[b, pl.ds(row, 1), :]).astype(MXU_DTYPE)]
                res = _stacked_segsum(tiles, bd_ref[...])
                for b in range(B):
                    c = z * B + b
                    Gc = Gcs[b]
                    dsab = -(res[3 * b] + res[3 * b + 1])
                    dvb = res[3 * b + 2]
                    ld = lambda ref: ref[b, pl.ds(row, 1), :]
                    G[c] = Gc * ld(wr) + dsab * ld(kkr)
                    S = Sbuf[s, c]
                    for ref, val in ((dr_o, _colsum(Sbuf[s + 1, c] * DYB[s, c])), (dkt_o, _colsum(Gc * VB[s, c])),
                                     (dv_o, _colsum(ef_ref[...] * dvb)), (dw_o, _colsum(Gc * S)),
                                     (dakk_o, -_colsum(Gc * SAB[s, c])), (dkk_o, _colsum(dsab * S))):
                        ref[b, pl.ds(row, 1), :] = val
            return carry

        lax.fori_loop(0, C, bwd, 0)

    ins, specs = [], []
    for z, rev in ((0, True), (1, False)):
        ins += [w, kt, akk, kk, shifted, shifted, dys]
        specs += [blk(z, 0, rev), blk(z, 0, rev), blk(z, 0, rev), blk3(0, rev), blk3(2, rev), blk3(0, rev), blk3(0, rev)]
    ins += [ck, eye_b, eye_f, bd]
    specs += [pl.BlockSpec((None, 2 * B, HEAD_DIM, RWKV_W), lambda g: (nC - 1 - g, 0, 0, 0)),
              _full((HEAD_DIM, RWKV_W)), _full((HEAD_DIM, RWKV_W)), _full((256, 256))]
    sds = jax.ShapeDtypeStruct
    out_specs = tuple(blk3(0, True) for _ in range(6)) + tuple(blk3(0, False) for _ in range(6))
    res = pl.pallas_call(
        body, name="scan_bwd", grid=(nC,), in_specs=specs, out_specs=out_specs,
        out_shape=tuple(sds((B, T, RWKV_W), F32) for _ in range(12)),
        scratch_shapes=[pltpu.VMEM((C + 1, 2 * B, HEAD_DIM, RWKV_W), F32)] +
                       [pltpu.VMEM((C, 2 * B, HEAD_DIM, RWKV_W), F32)] * 3 + [pltpu.VMEM((2 * B, HEAD_DIM, RWKV_W), F32)],
        compiler_params=_cp(("arbitrary",)),
    )(*ins)
    return [jnp.stack([res[j], res[6 + j]]) for j in range(6)]


def _out_head_call(x2, tgt2, gate, y_att, g_att, ys, shifted, kt, g_rw, w_out, g_post, gn_w, gn_b, r_k, bd, T):
    R = x2.shape[0]
    TT = min(ROW_TILE, T)
    tpe = T // TT

    def body(x_ref, t_ref, gate_ref, ya_ref, ga_ref, ys_ref, r_ref, v_ref, kt_ref, grw_ref, w_ref, gp_ref, gnw_ref,
             gnb_ref, rk_ref, bd_ref,
             loss_o, dy_o, dya_o, dga_o, dys_o, dr_o, dv_o, dkts_o, dgrw_o, dgate_o, gw_o, ggp_o, ggnw_o, ggnb_o, grk_o):
        i = pl.program_id(0)
        bd = bd_ref[...]
        mix = functools.partial(_mix_fn, bd=bd, diff=True)
        (ma, mr), mix_vjp = jax.vjp(mix, ya_ref[...], ga_ref[...], ys_ref[0] + ys_ref[1], r_ref[...], v_ref[...],
                                    kt_ref[0] + kt_ref[1], grw_ref[...], gnw_ref[...], gnb_ref[...], rk_ref[...])
        out = _dot(ma, w_ref[0:ATT_W, :]) + _dot(mr, w_ref[ATT_W:, :])
        loss, loss_vjp = jax.vjp(_loss_fn, out, x_ref[...], t_ref[...], gate_ref[0], gp_ref[...])
        d_out, dy, _, dgate, dgp = loss_vjp(jnp.ones((1, 1), F32))
        dy_o[...] = dy
        dma = _dot_nt(d_out, w_ref[0:ATT_W, :])
        dmr = _dot_nt(d_out, w_ref[ATT_W:, :])
        dya_o[...], dga_o[...], dys_o[...], dr_o[...], dv_o[...], dkts_o[...], dgrw_o[...], dgnw, dgnb, drk = \
            mix_vjp((dma, dmr))
        gw = jnp.concatenate([_dot_tn(ma, d_out), _dot_tn(mr, d_out)], axis=0)
        acc = ((loss_o, jnp.broadcast_to(loss, (8, 128))), (gw_o, gw), (ggp_o, dgp), (ggnw_o, dgnw), (ggnb_o, dgnb),
               (grk_o, drk))

        @pl.when(i == 0)
        def _():
            for ref, val in acc:
                ref[...] = val

        @pl.when(i > 0)
        def _():
            for ref, val in acc:
                ref[...] += val

        @pl.when(i % tpe == 0)
        def _():
            dgate_o[0] = dgate

        @pl.when(i % tpe > 0)
        def _():
            dgate_o[0] += dgate

    row = lambda w, c=0: pl.BlockSpec((TT, w), lambda i: (i, c))
    two = pl.BlockSpec((2, TT, RWKV_W), lambda i: (0, i, 0))
    per_ex = pl.BlockSpec((1, 1, D_MODEL), lambda i: (i // tpe, 0, 0))
    sds = jax.ShapeDtypeStruct
    r512 = sds((R, RWKV_W), F32)
    return pl.pallas_call(
        body, name="out_head", grid=(R // TT,),
        in_specs=[row(D_MODEL), row(D_MODEL), per_ex, row(ATT_W), row(ATT_W), two, row(RWKV_W, 0), row(RWKV_W, 2), two,
                  row(RWKV_W), _full(w_out.shape), _full((1, D_MODEL)), _full((1, RWKV_W)), _full((1, RWKV_W)),
                  _full((1, RWKV_W)), _full((256, 256))],
        out_specs=(_full((8, 128)), row(D_MODEL), row(ATT_W), row(ATT_W), row(RWKV_W), row(RWKV_W), row(RWKV_W),
                   row(RWKV_W), row(RWKV_W), per_ex, _full((D_MODEL, D_MODEL)), _full((1, D_MODEL)), _full((1, RWKV_W)),
                   _full((1, RWKV_W)), _full((1, RWKV_W))),
        out_shape=(sds((8, 128), F32), sds((R, D_MODEL), F32), r512, r512, r512, r512, r512, r512, r512,
                   sds((R // T, 1, D_MODEL), F32), sds((D_MODEL, D_MODEL), F32), sds((1, D_MODEL), F32),
                   sds((1, RWKV_W), F32), sds((1, RWKV_W), F32), sds((1, RWKV_W), F32)),
        compiler_params=_cp(("arbitrary",)),
    )(x2, tgt2, gate, y_att, g_att, ys, shifted, shifted, kt, g_rw, w_out, g_post, gn_w, gn_b, r_k, bd)


def _in_proj_bwd_call(x2, dy, shift, scale, g_pre, w_in, qg, kg, cos, sin, bd, q_raw, k_raw, dqr, dkp, dvp,
                      d_gatt, d_rin, d_grw, T):
    R = x2.shape[0]
    TT = min(ROW_TILE, T)
    tpe = T // TT

    def body(x_ref, dy_ref, sh_ref, sc_ref, gp_ref, w_ref, qg_ref, kg_ref, cos_ref, sin_ref, bd_ref, q_ref, k_ref,
             dqr_ref, dkp_ref, dvp_ref, dga_ref, drin_ref, dgrw_ref,
             dx_o, dproj_o, dsh_o, dsc_o, ggp_o, gqg_o, gkg_o):
        i = pl.program_id(0)
        cos, sin, bd = cos_ref[...], sin_ref[...], bd_ref[...]
        left = lax.broadcasted_iota(jnp.int32, (1, KV_W), 1) < HEAD_DIM

        def kv_grad(ref):
            a = ref[0] + ref[1]
            b = ref[2] + ref[3]
            return jnp.where(left, a + pltpu.roll(a, HEAD_DIM, 1), b + pltpu.roll(b, HEAD_DIM, 1))

        qfn = functools.partial(_qk_fn, cos=jnp.tile(cos, (1, 4)), sin=jnp.tile(sin, (1, 4)), bd=bd, scale=ATT_SCALE,
                                diff=True)
        _, q_vjp = jax.vjp(qfn, q_ref[...], qg_ref[...])
        dq, gqg = q_vjp(dqr_ref[...])
        kfn = functools.partial(_qk_fn, cos=cos, sin=sin, bd=bd, scale=1.0, diff=True)
        _, k_vjp = jax.vjp(kfn, k_ref[...], kg_ref[...])
        dk, gkg = k_vjp(kv_grad(dkp_ref))
        pieces = ((C_Q, C_K, dq), (C_K, C_V, dk), (C_V, C_GA, kv_grad(dvp_ref)), (C_GA, C_RIN, dga_ref[...]),
                  (C_RIN, C_GRW, drin_ref[...]), (C_GRW, C_END, dgrw_ref[...]))
        dh = jnp.zeros((TT, D_MODEL), F32)
        for c0, c1, val in pieces:
            vb = val.astype(MXU_DTYPE)
            dproj_o[:, c0:c1] = vb
            dh = dh + _dot_nt(vb, w_ref[:, c0:c1])
        _, pre_vjp = jax.vjp(_pre_fn, x_ref[...], sh_ref[0], sc_ref[0], gp_ref[...])
        dx, dsh, dsc, ggp = pre_vjp(dh)
        dx_o[...] = dx + dy_ref[...]
        acc = ((ggp_o, ggp), (gqg_o, gqg), (gkg_o, gkg))

        @pl.when(i == 0)
        def _():
            for ref, val in acc:
                ref[...] = val

        @pl.when(i > 0)
        def _():
            for ref, val in acc:
                ref[...] += val

        @pl.when(i % tpe == 0)
        def _():
            dsh_o[0] = dsh
            dsc_o[0] = dsc

        @pl.when(i % tpe > 0)
        def _():
            dsh_o[0] += dsh
            dsc_o[0] += dsc

    row = lambda w: pl.BlockSpec((TT, w), lambda i: (i, 0))
    per_ex = pl.BlockSpec((1, 1, D_MODEL), lambda i: (i // tpe, 0, 0))
    tab = pl.BlockSpec((TT, KV_W), lambda i: (i % tpe, 0))
    pad = pl.BlockSpec((4, TT, KV_W), lambda i: (0, i, 0))
    sds = jax.ShapeDtypeStruct
    nb = R // T
    return pl.pallas_call(
        body, name="in_proj_bwd", grid=(R // TT,),
        in_specs=[row(D_MODEL), row(D_MODEL), per_ex, per_ex, _full((1, D_MODEL)), _full(w_in.shape), _full((1, ATT_W)),
                  _full((1, KV_W)), tab, tab, _full((256, 256)), row(ATT_W), row(KV_W), row(ATT_W), pad, pad,
                  row(ATT_W), row(SHIFT_W), row(RWKV_W)],
        out_specs=(row(D_MODEL), row(C_END), per_ex, per_ex, _full((1, D_MODEL)), _full((1, ATT_W)), _full((1, KV_W))),
        out_shape=(sds((R, D_MODEL), F32), sds((R, C_END), MXU_DTYPE), sds((nb, 1, D_MODEL), F32),
                   sds((nb, 1, D_MODEL), F32), sds((1, D_MODEL), F32), sds((1, ATT_W), F32), sds((1, KV_W), F32)),
        compiler_params=_cp(("arbitrary",)),
    )(x2, dy, shift, scale, g_pre, w_in, qg, kg, cos, sin, bd, q_raw, k_raw, dqr, dkp, dvp, d_gatt, d_rin, d_grw)


def _w_in_grad_call(hb, dproj, T):
    R = hb.shape[0]
    TT = min(ROW_TILE, T)
    CB = 1152

    def body(h_ref, d_ref, o_ref):
        g = _dot_tn(h_ref[...], d_ref[...])

        @pl.when(pl.program_id(1) == 0)
        def _():
            o_ref[...] = g

        @pl.when(pl.program_id(1) > 0)
        def _():
            o_ref[...] += g

    return pl.pallas_call(
        body, name="w_in_grad", grid=(C_END // CB, R // TT),
        in_specs=[pl.BlockSpec((TT, D_MODEL), lambda j, i: (i, 0)), pl.BlockSpec((TT, CB), lambda j, i: (i, j))],
        out_specs=pl.BlockSpec((D_MODEL, CB), lambda j, i: (0, j)),
        out_shape=jax.ShapeDtypeStruct((D_MODEL, C_END), F32), compiler_params=_cp(("arbitrary", "arbitrary")),
    )(hb, dproj)


def _adam_call(parts, w, m, v, name, row_tile=None):
    P, M, N = parts.shape
    TM = M if row_tile is None else row_tile

    def body(p_ref, w_ref, m_ref, v_ref, g_o, d_o, m_o, v_o):
        g = p_ref[0]
        for j in range(1, P):
            g = g + p_ref[j]
        m2 = ADAM_B1 * m_ref[...] + (1.0 - ADAM_B1) * g
        v2 = ADAM_B2 * v_ref[...] + (1.0 - ADAM_B2) * jnp.square(g)
        m_hat = m2 / (1.0 - ADAM_B1 ** ADAM_STEP)
        v_hat = v2 / (1.0 - ADAM_B2 ** ADAM_STEP)
        g_o[...] = g
        d_o[...] = -ADAM_LR * (m_hat / (jnp.sqrt(v_hat) + ADAM_EPS) + ADAM_WD * w_ref[...])
        m_o[...] = m2
        v_o[...] = v2

    blk = pl.BlockSpec((TM, N), lambda i: (i, 0))
    return pl.pallas_call(
        body, name=name, grid=(M // TM,),
        in_specs=[pl.BlockSpec((P, TM, N), lambda i: (0, i, 0)), blk, blk, blk], out_specs=(blk,) * 4,
        out_shape=(jax.ShapeDtypeStruct((M, N), F32),) * 4, compiler_params=_cp(("arbitrary",)),
    )(parts, w, m, v)


_SMALL_ROWS = 136


def _pack_small(taps, w_up, w0, a_up, a0):
    flat = jnp.concatenate([taps.reshape(-1), w_up.reshape(-1), w0.reshape(-1), a_up.reshape(-1), a0.reshape(-1)])
    return jnp.pad(flat, (0, _SMALL_ROWS * 128 - flat.shape[0])).reshape(_SMALL_ROWS, 128)


def _unpack_small(packed):
    n = packed.shape[0]
    flat = packed.reshape(n, -1)
    out, o = [], 0
    for shape in ((3, 208), (2, 64, 64), (2, 64), (2, 64, 64), (2, 64)):
        size = 1
        for s in shape:
            size *= s
        out.append(flat[:, o:o + size].reshape((n,) + shape))
        o += size
    return out


def _cols_to_full(blocks):
    nd = blocks.ndim
    moved = jnp.moveaxis(blocks, 0, nd - 2)
    return moved.reshape(moved.shape[:-2] + (moved.shape[-2] * moved.shape[-1],))


def _full_to_cols(full):
    k = full.shape[-1] // NDEV
    return jnp.moveaxis(full.reshape(full.shape[:-1] + (NDEV, k)), -2, 0)


_REP_SIZES = (("g_pre", 1024), ("q_norm_g", 64), ("k_norm_g", 64), ("k_k", 512), ("k_a", 512), ("r_k", 512),
              ("gn_w", 512), ("gn_b", 512), ("g_post", 1024))
_REP_ROWS = 40


def kernel(x, c, w_ada, b_ada, g_pre, w_in, q_norm_g, k_norm_g, shift_taps, w_up, w0, a_up, a0, k_k, k_a, r_k, gn_w, gn_b, w_out, g_post, loss_target, m_w_ada, m_b_ada, m_g_pre, m_w_in, m_q_norm_g, m_k_norm_g, m_shift_taps, m_w_up, m_w0, m_a_up, m_a0, m_k_k, m_k_a, m_r_k, m_gn_w, m_gn_b, m_w_out, m_g_post, v_w_ada, v_b_ada, v_g_pre, v_w_in, v_q_norm_g, v_k_norm_g, v_shift_taps, v_w_up, v_w0, v_a_up, v_a0, v_k_k, v_k_a, v_r_k, v_gn_w, v_gn_b, v_w_out, v_g_post):
    B, T, _ = x.shape
    R = B * T
    me = 4 * lax.axis_index("x") + 2 * lax.axis_index("y") + lax.axis_index("c")
    x2 = x.reshape(R, D_MODEL)
    tgt2 = loss_target.reshape(R, D_MODEL)

    seg = jnp.arange(256) // HEAD_DIM
    bd = (seg[:, None] == seg[None, :]).astype(MXU_DTYPE)
    eye = (jnp.arange(HEAD_DIM)[:, None] == (jnp.arange(RWKV_W) % HEAD_DIM)[None, :])
    eye_b, eye_f = eye.astype(MXU_DTYPE), eye.astype(F32)
    cos, sin = _rope_tables(T)

    c_g, w_in_g, w_out_g, small_g = _exchange(
        [c, w_in[0].astype(MXU_DTYPE), w_out[0].astype(MXU_DTYPE),
         _pack_small(shift_taps[0], w_up[0], w0[0], a_up[0], a0[0])], [False] * 4, "gather_params")
    c_all = c_g.reshape(NDEV * B, D_MODEL)
    w_in_f = _cols_to_full(w_in_g)
    w_out_f = w_out_g.reshape(D_MODEL, D_MODEL)
    taps_b, w_up_b, w0_b, a_up_b, a0_b = _unpack_small(small_g)
    taps_f = jnp.pad(_cols_to_full(taps_b), ((0, 5), (0, 0)))
    w_up_f, a_up_f = _cols_to_full(w_up_b), _cols_to_full(a_up_b)
    w0_f, a0_f = _cols_to_full(w0_b), _cols_to_full(a0_b)
    wup_pad = jnp.pad(w_up_f, ((0, 0), (0, 64), (0, 0))).astype(MXU_DTYPE)
    aup_pad = jnp.pad(a_up_f, ((0, 0), (64, 0), (0, 0))).astype(MXU_DTYPE)

    ncol = w_ada.shape[2]
    b_cols = lax.dynamic_slice(b_ada, (0, me * ncol), (1, ncol))
    mod_cols = _mod_call(c_all, w_ada[0].astype(MXU_DTYPE), b_cols)
    (mod_g,) = _exchange([mod_cols], [False], "gather_mod")
    mod = lax.dynamic_slice(_cols_to_full(mod_g), (me * B, 0), (B, 3 * D_MODEL))
    shift, scale, gate = [mod[:, j * D_MODEL:(j + 1) * D_MODEL].reshape(B, 1, D_MODEL) for j in range(3)]

    qg = jnp.tile(q_norm_g, (1, ATT_W // HEAD_DIM))
    kg = jnp.tile(k_norm_g, (1, KV_W // HEAD_DIM))
    rk_row = r_k.reshape(1, RWKV_W)

    hb, qr, kpad, vpad, q_raw, k_raw, g_att, rin, g_rw = _in_proj_call(
        x2, shift, scale, g_pre, w_in_f, qg, kg, cos, sin, bd, T)
    y_att = _att_fwd_call(qr, kpad, vpad, B, T)
    shifted = _shift_fwd_call(rin, taps_f, T)
    w_s, kt_s, akk_s, kk_s = _rwkv_prep_call(shifted, wup_pad, aup_pad, w0_f, a0_f, k_k, k_a, bd, T)
    sh3 = shifted.reshape(B, T, SHIFT_W)
    r4 = lambda a: a.reshape(2, B, T, RWKV_W)
    y0, y1, ck = _scan_fwd_call(r4(w_s), r4(kt_s), r4(akk_s), kk_s.reshape(B, T, RWKV_W), sh3, eye_b, eye_f, bd, B, T)
    ys = jnp.stack([y0.reshape(R, RWKV_W), y1.reshape(R, RWKV_W)])

    (loss_blk, dy, d_yatt, d_gatt, d_ys, d_r2, d_v2, d_kts, d_grw, d_gate, g_wout, g_gpost, g_gnw, g_gnb,
     g_rk) = _out_head_call(x2, tgt2, gate, y_att, g_att, ys, shifted, kt_s, g_rw, w_out_f, g_post, gn_w, gn_b, rk_row,
                            bd, T)
    scan_cts = _scan_bwd_call(r4(w_s), r4(kt_s), r4(akk_s), kk_s.reshape(B, T, RWKV_W), sh3,
                              d_ys.reshape(B, T, RWKV_W), ck, eye_b, eye_f, bd, B, T)
    scan_cts = [a.reshape(2, R, RWKV_W) for a in scan_cts]
    d_shifted, g_wup, g_aup, g_w0, g_a0, g_kk, g_ka = _rwkv_prep_bwd_call(
        shifted, scan_cts + [d_r2, d_v2, d_kts], wup_pad, aup_pad, w0_f, a0_f, k_k, k_a, bd, T)
    d_rin, g_taps = _shift_bwd_call(rin, d_shifted, taps_f, T)
    dqr, dkp, dvp = _att_bwd_call(qr, kpad, vpad, d_yatt, B, T)
    grad_x, dproj, d_shift, d_scale, g_gpre, g_qg, g_kg = _in_proj_bwd_call(
        x2, dy, shift, scale, g_pre, w_in_f, qg, kg, cos, sin, bd, q_raw, k_raw, dqr, dkp, dvp, d_gatt, d_rin, d_grw, T)
    g_win = _w_in_grad_call(hb, dproj, T)

    rep = jnp.concatenate([g_gpre.reshape(-1), g_qg.reshape(-1, HEAD_DIM).sum(0), g_kg.reshape(-1, HEAD_DIM).sum(0),
                           g_kk.reshape(-1), g_ka.reshape(-1), g_rk.reshape(-1), g_gnw.reshape(-1), g_gnb.reshape(-1),
                           g_gpost.reshape(-1)])
    rep = jnp.pad(rep, (0, _REP_ROWS * 128 - rep.shape[0])).reshape(_REP_ROWS, 128)
    dmod = jnp.concatenate([d_shift, d_scale, d_gate], axis=2).reshape(B, 3 * D_MODEL)
    small_parts = jax.vmap(_pack_small)(_full_to_cols(g_taps[:3]), _full_to_cols(g_wup[:, :64, :]), _full_to_cols(g_w0),
                                        _full_to_cols(g_aup[:, 64:, :]), _full_to_cols(g_a0))
    p_win, p_wout, p_small, dmod_g, rep_g = _exchange(
        [_full_to_cols(g_win), g_wout.reshape(NDEV, D_MODEL // NDEV, D_MODEL), small_parts, dmod, rep],
        [True, True, True, False, False], "reduce_grads")
    dmod_all = dmod_g.reshape(NDEV * B, 3 * D_MODEL)
    g_wada = _wada_grad_call(c_all, lax.dynamic_slice(dmod_all, (0, me * ncol), (NDEV * B, ncol)))

    res = {}

    def adam(name, parts, w, m, v, row_tile=None):
        shape = w.shape
        two_d = (-1, shape[-1])
        out = _adam_call(parts.reshape((parts.shape[0],) + w.reshape(two_d).shape), w.reshape(two_d), m.reshape(two_d),
                         v.reshape(two_d), "adam_" + name, row_tile)
        res[name] = [o.reshape(shape) for o in out]

    adam("w_ada", g_wada[None], w_ada, m_w_ada, v_w_ada)
    adam("b_ada", dmod_all.reshape(NDEV * B, 1, 3 * D_MODEL), b_ada, m_b_ada, v_b_ada)
    adam("w_in", p_win, w_in, m_w_in, v_w_in, 128)
    adam("w_out", p_wout, w_out, m_w_out, v_w_out)
    taps_p, wup_p, w0_p, aup_p, a0_p = _unpack_small(p_small)
    adam("shift_taps", taps_p, shift_taps, m_shift_taps, v_shift_taps)
    adam("w_up", wup_p, w_up, m_w_up, v_w_up)
    adam("w0", w0_p, w0, m_w0, v_w0)
    adam("a_up", aup_p, a_up, m_a_up, v_a_up)
    adam("a0", a0_p, a0, m_a0, v_a0)
    rep_flat = rep_g.reshape(NDEV, -1)
    off = 0
    given = dict(g_pre=(g_pre, m_g_pre, v_g_pre), q_norm_g=(q_norm_g, m_q_norm_g, v_q_norm_g),
                 k_norm_g=(k_norm_g, m_k_norm_g, v_k_norm_g), k_k=(k_k, m_k_k, v_k_k), k_a=(k_a, m_k_a, v_k_a),
                 r_k=(r_k, m_r_k, v_r_k), gn_w=(gn_w, m_gn_w, v_gn_w), gn_b=(gn_b, m_gn_b, v_gn_b),
                 g_post=(g_post, m_g_post, v_g_post))
    for name, size in _REP_SIZES:
        adam(name, rep_flat[:, off:off + size], *given[name])
        off += size

    loss = lax.psum(loss_blk[0, 0], ("x", "y", "c"))
    order = ["w_ada", "b_ada", "g_pre", "w_in", "q_norm_g", "k_norm_g", "shift_taps", "w_up", "w0", "a_up", "a0", "k_k",
             "k_a", "r_k", "gn_w", "gn_b", "w_out", "g_post"]
    return (loss, grad_x.reshape(B, T, D_MODEL), *[res[n][0] for n in order], *[res[n][1] for n in order],
            *[res[n][2] for n in order], *[res[n][3] for n in order])
```

```python
import functools

import jax
import jax.numpy as jnp
from jax import lax
from jax.experimental import pallas as pl
from jax.experimental.pallas import tpu as pltpu

F32 = jnp.float32
MXU_DTYPE = jnp.bfloat16
MESH = pl.DeviceIdType.MESH
NDEV = 8

D_MODEL = 1024
HEAD_DIM = 64
ATT_W = 512
KV_W = 128
RWKV_W = 512
LORA_W = 128
SHIFT_W = 3 * RWKV_W + LORA_W
GRID_W = 64
ROPE_THETA = 10000.0
DECAY_SCALE = 0.6065306597126334
NORM_EPS = 1e-6
GN_EPS = 64e-5
L2_EPS = 1e-12
ATT_SCALE = HEAD_DIM ** -0.5
C_Q, C_K, C_V, C_GA, C_RIN, C_GRW, C_END = 0, 512, 640, 768, 1280, 2944, 3456

ADAM_LR, ADAM_B1, ADAM_B2, ADAM_EPS, ADAM_WD, ADAM_STEP = 0.001, 0.9, 0.999, 1e-08, 0.01, 10

ROW_TILE = 256
SCAN_CHUNK = 16
SCAN_UNROLL = 4
VMEM_LIMIT = 56 * 1024 * 1024


def _cp(sem=None):
    return pltpu.CompilerParams(dimension_semantics=sem, vmem_limit_bytes=VMEM_LIMIT)


def _dot(a, b, dims=(((1,), (0,)), ((), ()))):
    return lax.dot_general(a.astype(MXU_DTYPE), b.astype(MXU_DTYPE), dims, preferred_element_type=F32)


def _dot_nt(a, b):
    return _dot(a, b, (((1,), (1,)), ((), ())))


def _dot_tn(a, b):
    return _dot(a, b, (((0,), (0,)), ((), ())))


def _seg_dot(xb, bd):
    n = xb.shape[1]
    if n <= 256:
        return jnp.dot(xb, bd[:n, :n], preferred_element_type=F32)
    parts = [jnp.dot(xb[:, c:c + 256], bd, preferred_element_type=F32) for c in range(0, n, 256)]
    return jnp.concatenate(parts, axis=1)


def _split3(x):
    hi = x.astype(MXU_DTYPE)
    r1 = x - hi.astype(F32)
    mid = r1.astype(MXU_DTYPE)
    lo = (r1 - mid.astype(F32)).astype(MXU_DTYPE)
    return hi, mid, lo


def _segsum_raw(x, bd):
    hi, mid, lo = _split3(x)
    return _seg_dot(hi, bd) + _seg_dot(mid, bd) + _seg_dot(lo, bd)


@jax.custom_vjp
def _segsum_d(x, bd):
    return _segsum_raw(x, bd)


def _segsum_d_fwd(x, bd):
    return _segsum_raw(x, bd), bd


def _segsum_d_bwd(bd, ct):
    return _segsum_raw(ct, bd), jnp.zeros_like(bd)


_segsum_d.defvjp(_segsum_d_fwd, _segsum_d_bwd)


def _rope_tables(T):
    t = jnp.arange(T, dtype=F32)
    row = jnp.floor(t / GRID_W)
    col = t - row * GRID_W
    n_freq = HEAD_DIM // 4
    inv_freq = ROPE_THETA ** (-jnp.arange(n_freq, dtype=F32) / n_freq)
    d = jnp.arange(HEAD_DIM)
    pos = jnp.where((d < HEAD_DIM // 2)[None, :], row[:, None], col[:, None])
    ang = pos * inv_freq[d % n_freq][None, :]
    sign = jnp.where((d % 32) < 16, -1.0, 1.0).astype(F32)[None, :]
    cos = jnp.cos(ang)
    sin = jnp.sin(ang) * sign
    return jnp.tile(cos, (1, 2)), jnp.tile(sin, (1, 2))


def _rope_raw(x, cos, sin):
    n = x.shape[1]
    lane = lax.broadcasted_iota(jnp.int32, (1, n), 1)
    first = (lane % 32) < 16
    partner = jnp.where(first, pltpu.roll(x, n - 16, 1), pltpu.roll(x, 16, 1))
    return x * cos + partner * sin


@jax.custom_vjp
def _rope_d(x, cos, sin):
    return _rope_raw(x, cos, sin)


def _rope_d_fwd(x, cos, sin):
    return _rope_raw(x, cos, sin), (cos, sin)


def _rope_d_bwd(res, ct):
    cos, sin = res
    return _rope_raw(ct, cos, -sin), jnp.zeros_like(cos), jnp.zeros_like(sin)


_rope_d.defvjp(_rope_d_fwd, _rope_d_bwd)


def _rms(x, g):
    return x * lax.rsqrt(jnp.mean(x * x, axis=-1, keepdims=True) + NORM_EPS) * g


def _pre_fn(x, shift, scale, g_pre):
    return _rms(x, g_pre) * (1.0 + scale) + shift


def _qk_fn(q, g, cos, sin, bd, scale, diff):
    segsum = _segsum_d if diff else _segsum_raw
    rope = _rope_d if diff else _rope_raw
    qn = q * lax.rsqrt(segsum(q * q, bd) * (1.0 / HEAD_DIM) + NORM_EPS) * g
    return rope(qn, cos, sin) * scale


def _silu(x):
    return x * jax.nn.sigmoid(x)


def _rwkv_pw(k, pw0, pw1, pa0, pa1, w0, a0, k_k, k_a, bd, diff):
    segsum = _segsum_d if diff else _segsum_raw
    kk = k * k_k
    kk = kk * lax.rsqrt(segsum(kk * kk, bd) + L2_EPS)
    ws, kts, akks = [], [], []
    for z, (pw, pa) in enumerate(((pw0, pa0), (pw1, pa1))):
        w = jnp.exp(-DECAY_SCALE * jax.nn.sigmoid(w0[z:z + 1, :] + pw))
        a = jax.nn.sigmoid(a0[z:z + 1, :] + pa)
        ws.append(w)
        kts.append(k * (1.0 + (a - 1.0) * k_a))
        akks.append(a * kk)
    return ws[0], ws[1], kts[0], kts[1], akks[0], akks[1], kk


def _mix_fn(y_att, g_att, ys, r, v, kts, g_rw, gn_w, gn_b, r_k, bd, diff):
    segsum = _segsum_d if diff else _segsum_raw
    mu = segsum(ys, bd) * (1.0 / HEAD_DIM)
    d = ys - mu
    var = segsum(d * d, bd) * (1.0 / HEAD_DIM)
    yn = d * lax.rsqrt(var + GN_EPS) * gn_w + gn_b
    bonus = segsum(r * kts * r_k, bd) * v
    return y_att * _silu(g_att), (yn + bonus) * _silu(g_rw)


def _loss_fn(out, x, tgt, gate, g_post):
    e = x + gate * _rms(out, g_post) - tgt
    s = jnp.sum(e * e, axis=1, keepdims=True)
    return jnp.sum(s, axis=0, keepdims=True) * (0.5 / D_MODEL)


def _exchange(arrays, scatter, name):
    n = len(arrays)
    out_shape = tuple(
        jax.ShapeDtypeStruct((NDEV,) + tuple(a.shape[1:] if sc else a.shape), a.dtype)
        for a, sc in zip(arrays, scatter))

    def body(*refs):
        ins, outs = refs[:n], refs[n:2 * n]
        send_sems, recv_sems, local_sems = refs[2 * n:]
        ix, iy, ic = lax.axis_index("x"), lax.axis_index("y"), lax.axis_index("c")
        me = 4 * ix + 2 * iy + ic

        def src(k, p):
            return ins[k].at[p] if scatter[k] else ins[k]

        local = [pltpu.make_async_copy(src(k, me), outs[k].at[me], local_sems.at[k]) for k in range(n)]
        for cp in local:
            cp.start()
        sends, recvs = [], []
        for m in range(1, NDEV):
            px = 1 - ix if (m >> 2) & 1 else ix
            py = 1 - iy if (m >> 1) & 1 else iy
            pc = 1 - ic if m & 1 else ic
            p = 4 * px + 2 * py + pc
            for k in range(n):
                common = dict(send_sem=send_sems.at[k, m - 1], recv_sem=recv_sems.at[k, m - 1],
                              device_id=(px, py, pc), device_id_type=MESH)
                sends.append(pltpu.make_async_remote_copy(src_ref=src(k, p), dst_ref=outs[k].at[me], **common))
                recvs.append(pltpu.make_async_remote_copy(src_ref=src(k, p), dst_ref=outs[k].at[p], **common))
        for cp in sends:
            cp.start()
        for cp in recvs:
            cp.wait_recv()
        for cp in sends:
            cp.wait_send()
        for cp in local:
            cp.wait()

    any_spec = pl.BlockSpec(memory_space=pl.ANY)
    return pl.pallas_call(
        body, name=name, out_shape=out_shape,
        in_specs=[any_spec] * n, out_specs=tuple([any_spec] * n),
        scratch_shapes=[pltpu.SemaphoreType.DMA((n, NDEV - 1)), pltpu.SemaphoreType.DMA((n, NDEV - 1)),
                        pltpu.SemaphoreType.DMA((n,))],
    )(*arrays)


def _mod_call(c_all, w_ada, b_cols):
    def body(c_ref, w_ref, b_ref, o_ref):
        o_ref[...] = _dot(_silu(c_ref[...]), w_ref[...]) + b_ref[...]

    return pl.pallas_call(body, name="mod_fwd",
                          out_shape=jax.ShapeDtypeStruct((c_all.shape[0], w_ada.shape[1]), F32))(c_all, w_ada, b_cols)


def _wada_grad_call(c_all, dmod_cols):
    def body(c_ref, d_ref, o_ref):
        o_ref[...] = _dot_tn(_silu(c_ref[...]), d_ref[...])

    return pl.pallas_call(body, name="w_ada_grad",
                          out_shape=jax.ShapeDtypeStruct((c_all.shape[1], dmod_cols.shape[1]), F32))(c_all, dmod_cols)


def _full(shape):
    nd = len(shape)
    return pl.BlockSpec(shape, lambda *_: (0,) * nd)


def _in_proj_call(x2, shift, scale, g_pre, w_in, qg, kg, cos, sin, bd, T):
    R = x2.shape[0]
    TT = min(ROW_TILE, T)
    tpe = T // TT

    def body(x_ref, sh_ref, sc_ref, gp_ref, w_ref, qg_ref, kg_ref, cos_ref, sin_ref, bd_ref,
             hb_ref, qr_ref, kpad_ref, vpad_ref, qraw_ref, kraw_ref, gatt_ref, rin_ref, grw_ref):
        h = _pre_fn(x_ref[...], sh_ref[0], sc_ref[0], gp_ref[...])
        hb = h.astype(MXU_DTYPE)
        hb_ref[...] = hb

        def proj(c0, c1):
            return jnp.dot(hb, w_ref[:, c0:c1], preferred_element_type=F32)

        q = proj(C_Q, C_K)
        k = proj(C_K, C_V)
        v = proj(C_V, C_GA)
        gatt_ref[...] = proj(C_GA, C_RIN)
        rin_ref[...] = proj(C_RIN, C_GRW)
        grw_ref[...] = proj(C_GRW, C_END)
        qraw_ref[...] = q
        kraw_ref[...] = k
        cos, sin, bd = cos_ref[...], sin_ref[...], bd_ref[...]
        qr = _qk_fn(q, qg_ref[...], jnp.tile(cos, (1, 4)), jnp.tile(sin, (1, 4)), bd, ATT_SCALE, False)
        qr_ref[...] = qr.astype(MXU_DTYPE)
        kr = _qk_fn(k, kg_ref[...], cos, sin, bd, 1.0, False)
        left = lax.broadcasted_iota(jnp.int32, (1, KV_W), 1) < HEAD_DIM
        for ref, val in ((kpad_ref, kr), (vpad_ref, v)):
            h0l = jnp.where(left, val, 0.0)
            h1r = jnp.where(left, 0.0, val)
            ref[0] = h0l.astype(MXU_DTYPE)
            ref[1] = pltpu.roll(h0l, HEAD_DIM, 1).astype(MXU_DTYPE)
            ref[2] = pltpu.roll(h1r, HEAD_DIM, 1).astype(MXU_DTYPE)
            ref[3] = h1r.astype(MXU_DTYPE)

    row = lambda w: pl.BlockSpec((TT, w), lambda i: (i, 0))
    per_ex = pl.BlockSpec((1, 1, D_MODEL), lambda i: (i // tpe, 0, 0))
    tab = pl.BlockSpec((TT, KV_W), lambda i: (i % tpe, 0))
    pad = pl.BlockSpec((4, TT, KV_W), lambda i: (0, i, 0))
    sds = jax.ShapeDtypeStruct
    return pl.pallas_call(
        body, name="in_proj", grid=(R // TT,),
        in_specs=[row(D_MODEL), per_ex, per_ex, _full((1, D_MODEL)), _full(w_in.shape), _full((1, ATT_W)),
                  _full((1, KV_W)), tab, tab, _full((256, 256))],
        out_specs=(row(D_MODEL), row(ATT_W), pad, pad, row(ATT_W), row(KV_W), row(ATT_W), row(SHIFT_W), row(RWKV_W)),
        out_shape=(sds((R, D_MODEL), MXU_DTYPE), sds((R, ATT_W), MXU_DTYPE), sds((4, R, KV_W), MXU_DTYPE),
                   sds((4, R, KV_W), MXU_DTYPE), sds((R, ATT_W), F32), sds((R, KV_W), F32), sds((R, ATT_W), F32),
                   sds((R, SHIFT_W), F32), sds((R, RWKV_W), F32)),
        compiler_params=_cp(("arbitrary",)),
    )(x2, shift, scale, g_pre, w_in, qg, kg, cos, sin, bd)


def _softmax_rows(s):
    m = jnp.max(s, axis=1, keepdims=True)
    e = jnp.exp(s - m)
    return e / jnp.sum(e, axis=1, keepdims=True)


def _att_specs(T, TQ):
    nq = T // TQ
    qspec = pl.BlockSpec((TQ, KV_W), lambda b, p, i: (b * nq + i, p))
    side = lambda s: pl.BlockSpec((None, T, KV_W), lambda b, p, i: (2 * (p // 2) + s, b, 0))
    return nq, qspec, side


def _att_fwd_call(qr, kpad, vpad, B, T):
    TQ = min(ROW_TILE, T)
    nq, qspec, side = _att_specs(T, TQ)

    def body(q_ref, kl_ref, kr_ref, vl_ref, vr_ref, o_ref):
        q = q_ref[...]
        pa = _softmax_rows(_dot_nt(q, kl_ref[...]))
        pb = _softmax_rows(_dot_nt(q, kr_ref[...]))
        o_ref[...] = _dot(pa, vl_ref[...]) + _dot(pb, vr_ref[...])

    return pl.pallas_call(
        body, name="att_fwd", grid=(B, 4, nq),
        in_specs=[qspec, side(0), side(1), side(0), side(1)], out_specs=qspec,
        out_shape=jax.ShapeDtypeStruct((B * T, ATT_W), F32),
        compiler_params=_cp(("arbitrary",) * 3),
    )(qr, kpad, kpad, vpad, vpad)


def _att_bwd_call(qr, kpad, vpad, d_o, B, T):
    TQ = min(ROW_TILE, T)
    nq, qspec, side = _att_specs(T, TQ)

    def body(q_ref, kl_ref, kr_ref, vl_ref, vr_ref, do_ref, dq_ref, dk_ref, dv_ref):
        i = pl.program_id(2)
        q, do = q_ref[...], do_ref[...]
        left = lax.broadcasted_iota(jnp.int32, (1, KV_W), 1) < HEAD_DIM
        dq = jnp.zeros((TQ, KV_W), F32)
        dk = jnp.zeros((T, KV_W), F32)
        dv = jnp.zeros((T, KV_W), F32)
        for k_ref, v_ref, mask in ((kl_ref, vl_ref, left), (kr_ref, vr_ref, jnp.logical_not(left))):
            kk, vv = k_ref[...], v_ref[...]
            p = _softmax_rows(_dot_nt(q, kk))
            dp = _dot_nt(do, vv)
            ds = p * (dp - jnp.sum(p * dp, axis=1, keepdims=True))
            dq = dq + _dot(ds, kk)
            dk = dk + _dot_tn(ds, jnp.where(mask, q, jnp.zeros_like(q)))
            dv = dv + _dot_tn(p, jnp.where(mask, do, 0.0))
        dq_ref[...] = dq

        @pl.when(i == 0)
        def _():
            dk_ref[...] = dk
            dv_ref[...] = dv

        @pl.when(i > 0)
        def _():
            dk_ref[...] += dk
            dv_ref[...] += dv

    acc = pl.BlockSpec((None, T, KV_W), lambda b, p, i: (p, b, 0))
    sds = jax.ShapeDtypeStruct
    return pl.pallas_call(
        body, name="att_bwd", grid=(B, 4, nq),
        in_specs=[qspec, side(0), side(1), side(0), side(1), qspec], out_specs=(qspec, acc, acc),
        out_shape=(sds((B * T, ATT_W), F32), sds((4, B * T, KV_W), F32), sds((4, B * T, KV_W), F32)),
        compiler_params=_cp(("arbitrary",) * 3),
    )(qr, kpad, kpad, vpad, vpad, d_o)


def _shift_specs(R, T, TT, width):
    tpe = T // TT
    nb8 = R // 8
    cur = pl.BlockSpec((TT, width), lambda i: (i, 0))
    prev = pl.BlockSpec((8, width), lambda i: (jnp.maximum(i * (TT // 8) - 1, 0), 0))
    nxt = pl.BlockSpec((8, width), lambda i: (jnp.minimum((i + 1) * (TT // 8), nb8 - 1), 0))
    return tpe, cur, prev, nxt


def _neighbours(cur, prev8, next8, i, tpe, TT):
    rows = lax.broadcasted_iota(jnp.int32, (TT, 1), 0)
    first = jnp.where(i % tpe == 0, 0.0, 1.0)
    last = jnp.where(i % tpe == tpe - 1, 0.0, 1.0)
    before = jnp.where(rows == 0, prev8[7:8, :] * first, pltpu.roll(cur, 1, 0))
    after = jnp.where(rows == TT - 1, next8[0:1, :] * last, pltpu.roll(cur, TT - 1, 0))
    return before, after


def _shift_fwd_call(x, taps, T):
    R, width = x.shape
    TT = min(ROW_TILE, T)
    tpe, cur, prev, nxt = _shift_specs(R, T, TT, width)

    def body(x_ref, p_ref, n_ref, t_ref, o_ref):
        xc = x_ref[...]
        before, after = _neighbours(xc, p_ref[...], n_ref[...], pl.program_id(0), tpe, TT)
        o_ref[...] = t_ref[0:1, :] * before + t_ref[1:2, :] * xc + t_ref[2:3, :] * after

    return pl.pallas_call(
        body, name="shift_fwd", grid=(R // TT,), in_specs=[cur, prev, nxt, _full(taps.shape)], out_specs=cur,
        out_shape=jax.ShapeDtypeStruct((R, width), F32), compiler_params=_cp(("arbitrary",)),
    )(x, x, x, taps)


def _shift_bwd_call(x, d, taps, T):
    R, width = x.shape
    TT = min(ROW_TILE, T)
    tpe, cur, prev, nxt = _shift_specs(R, T, TT, width)

    def body(x_ref, xp_ref, xn_ref, d_ref, dp_ref, dn_ref, t_ref, dx_ref, dt_ref):
        i = pl.program_id(0)
        xc, dc = x_ref[...], d_ref[...]
        d_before, d_after = _neighbours(dc, dp_ref[...], dn_ref[...], i, tpe, TT)
        dx_ref[...] = t_ref[2:3, :] * d_before + t_ref[1:2, :] * dc + t_ref[0:1, :] * d_after
        x_before, x_after = _neighbours(xc, xp_ref[...], xn_ref[...], i, tpe, TT)
        @pl.when(i == 0)
        def _():
            dt_ref[...] = jnp.zeros_like(dt_ref)

        for j, xs in enumerate((x_before, xc, x_after)):
            dt_ref[j:j + 1, :] += jnp.sum(dc * xs, axis=0, keepdims=True)

    return pl.pallas_call(
        body, name="shift_bwd", grid=(R // TT,),
        in_specs=[cur, prev, nxt, cur, prev, nxt, _full(taps.shape)], out_specs=(cur, _full((8, width))),
        out_shape=(jax.ShapeDtypeStruct((R, width), F32), jax.ShapeDtypeStruct((8, width), F32)),
        compiler_params=_cp(("arbitrary",)),
    )(x, x, x, d, d, d, taps)


def _lora_in(wa):
    lane = lax.broadcasted_iota(jnp.int32, (1, LORA_W), 1)
    return jnp.where(lane < LORA_W // 2, jnp.tanh(wa), wa)


def _rwkv_prep_call(shifted, wup, aup, w0, a0, k_k, k_a, bd, T):
    R = shifted.shape[0]
    TT = min(ROW_TILE, T)

    def body(k_ref, wa_ref, wup_ref, aup_ref, w0_ref, a0_ref, kk_ref, ka_ref, bd_ref, w_o, kt_o, akk_o, kk_o):
        twa = _lora_in(wa_ref[...])
        pre = [_dot(twa, m_ref[z]) for m_ref in (wup_ref, aup_ref) for z in range(2)]
        outs = _rwkv_pw(k_ref[...], pre[0], pre[1], pre[2], pre[3], w0_ref[...], a0_ref[...], kk_ref[...],
                        ka_ref[...], bd_ref[...], False)
        w_o[0], w_o[1], kt_o[0], kt_o[1], akk_o[0], akk_o[1] = outs[:6]
        kk_o[...] = outs[6]

    col = lambda c, w: pl.BlockSpec((TT, w), lambda i: (i, c))
    two = pl.BlockSpec((2, TT, RWKV_W), lambda i: (0, i, 0))
    sds = jax.ShapeDtypeStruct
    return pl.pallas_call(
        body, name="rwkv_prep", grid=(R // TT,),
        in_specs=[col(1, RWKV_W), col(3 * RWKV_W // LORA_W, LORA_W), _full(wup.shape), _full(aup.shape),
                  _full((2, RWKV_W)), _full((2, RWKV_W)), _full((1, RWKV_W)), _full((1, RWKV_W)), _full((256, 256))],
        out_specs=(two, two, two, col(0, RWKV_W)),
        out_shape=(sds((2, R, RWKV_W), F32),) * 3 + (sds((R, RWKV_W), F32),),
        compiler_params=_cp(("arbitrary",)),
    )(shifted, shifted, wup, aup, w0, a0, k_k, k_a, bd)


def _rwkv_prep_bwd_call(shifted, cts, wup, aup, w0, a0, k_k, k_a, bd, T):
    R = shifted.shape[0]
    TT = min(ROW_TILE, T)

    def body(k_ref, wa_ref, dw0, dkt0, dakk0, dkk0, dr0, dv0, dw1, dkt1, dakk1, dkk1, dr1, dv1, dr2_ref, dv2_ref, dkts_ref,
             wup_ref, aup_ref, w0_ref, a0_ref, kk_ref, ka_ref, bd_ref,
             dsh_ref, gwup_ref, gaup_ref, gw0_ref, ga0_ref, gkk_ref, gka_ref):
        dw_ref, dkt_ref, dakk_ref, dkk_ref, dr_ref, dv_ref = ((dw0, dw1), (dkt0, dkt1), (dakk0, dakk1), (dkk0, dkk1),
                                                              (dr0, dr1), (dv0, dv1))
        i = pl.program_id(0)
        wa = wa_ref[...]
        twa = _lora_in(wa)
        pre = [_dot(twa, m_ref[z]) for m_ref in (wup_ref, aup_ref) for z in range(2)]
        fn = functools.partial(_rwkv_pw, bd=bd_ref[...], diff=True)
        _, vjp = jax.vjp(fn, k_ref[...], pre[0], pre[1], pre[2], pre[3], w0_ref[...], a0_ref[...], kk_ref[...],
                         ka_ref[...])
        dkts = dkts_ref[...]
        dk, dpw0, dpw1, dpa0, dpa1, gw0, ga0, gkk, gka = vjp(
            (dw_ref[0][...], dw_ref[1][...], dkt_ref[0][...] + dkts, dkt_ref[1][...] + dkts, dakk_ref[0][...],
             dakk_ref[1][...], dkk_ref[0][...] + dkk_ref[1][...]))
        dtwa = (_dot_nt(dpw0, wup_ref[0]) + _dot_nt(dpw1, wup_ref[1]) + _dot_nt(dpa0, aup_ref[0])
                + _dot_nt(dpa1, aup_ref[1]))
        lane = lax.broadcasted_iota(jnp.int32, (1, LORA_W), 1)
        dsh_ref[:, 0:RWKV_W] = dr_ref[0][...] + dr_ref[1][...] + dr2_ref[...]
        dsh_ref[:, RWKV_W:2 * RWKV_W] = dk
        dsh_ref[:, 2 * RWKV_W:3 * RWKV_W] = dv_ref[0][...] + dv_ref[1][...] + dv2_ref[...]
        dsh_ref[:, 3 * RWKV_W:] = jnp.where(lane < LORA_W // 2, dtwa * (1.0 - twa * twa), dtwa)
        acc = ((gwup_ref.at[0], _dot_tn(twa, dpw0)), (gwup_ref.at[1], _dot_tn(twa, dpw1)),
               (gaup_ref.at[0], _dot_tn(twa, dpa0)), (gaup_ref.at[1], _dot_tn(twa, dpa1)),
               (gw0_ref, gw0), (ga0_ref, ga0), (gkk_ref, gkk), (gka_ref, gka))

        @pl.when(i == 0)
        def _():
            for ref, val in acc:
                ref[...] = val

        @pl.when(i > 0)
        def _():
            for ref, val in acc:
                ref[...] += val

    col = lambda c, w: pl.BlockSpec((TT, w), lambda i: (i, c))
    one = col(0, RWKV_W)
    sds = jax.ShapeDtypeStruct
    return pl.pallas_call(
        body, name="rwkv_prep_bwd", grid=(R // TT,),
        in_specs=[col(1, RWKV_W), col(3 * RWKV_W // LORA_W, LORA_W)] + [one] * 15 + [
                  _full(wup.shape), _full(aup.shape), _full((2, RWKV_W)), _full((2, RWKV_W)), _full((1, RWKV_W)),
                  _full((1, RWKV_W)), _full((256, 256))],
        out_specs=(pl.BlockSpec((TT, SHIFT_W), lambda i: (i, 0)), _full(wup.shape), _full(aup.shape),
                   _full((2, RWKV_W)), _full((2, RWKV_W)), _full((1, RWKV_W)), _full((1, RWKV_W))),
        out_shape=(sds((R, SHIFT_W), F32), sds(wup.shape, F32), sds(aup.shape, F32), sds((2, RWKV_W), F32),
                   sds((2, RWKV_W), F32), sds((1, RWKV_W), F32), sds((1, RWKV_W), F32)),
        compiler_params=_cp(("arbitrary",)),
    )(shifted, shifted, *cts, wup, aup, w0, a0, k_k, k_a, bd)


def _col_lhs(row, eye_b):
    return eye_b * row.astype(MXU_DTYPE)


def _colsum(x):
    return jnp.sum(x, axis=0, keepdims=True)


def _stacked_segsum(tiles, bd):
    res = _seg_dot(jnp.concatenate(tiles, axis=0), bd)
    return [res[j * HEAD_DIM:(j + 1) * HEAD_DIM] for j in range(len(tiles))]


def _scan_specs(B, T, C, nC):
    def blk(z, col, rev):
        idx = (lambda g: (z, 0, nC - 1 - g, col)) if rev else (lambda g: (z, 0, g, col))
        return pl.BlockSpec((None, B, C, RWKV_W), idx)

    def blk3(col, rev):
        idx = (lambda g: (0, nC - 1 - g, col)) if rev else (lambda g: (0, g, col))
        return pl.BlockSpec((B, C, RWKV_W), idx)

    return blk, blk3


def _scan_fwd_call(w, kt, akk, kk, shifted, eye_b, eye_f, bd, B, T):
    C = min(SCAN_CHUNK, T)
    nC = T // C
    blk, blk3 = _scan_specs(B, T, C, nC)

    def body(w0, kt0, akk0, kk0, v0, r0, w1, kt1, akk1, kk1, v1, r1, eb_ref, ef_ref, bd_ref, y0, y1, ck, S):
        @pl.when(pl.program_id(0) == 0)
        def _():
            S[...] = jnp.zeros_like(S)

        ck[...] = S[...]
        dirs = ((w0, kt0, akk0, kk0, v0, r0, y0), (w1, kt1, akk1, kk1, v1, r1, y1))

        def step(s, carry):
            for z in range(2):
                row = s if z == 0 else C - 1 - s
                prev = jnp.maximum(s - 1, 0) if z == 0 else jnp.minimum(C - s, C - 1)
                wr, ktr, akkr, kkr, vr, rr, yr = dirs[z]
                tiles = []
                for b in range(B):
                    Sb = S[z * B + b].astype(MXU_DTYPE)
                    tiles += [Sb * kkr[b, pl.ds(row, 1), :].astype(MXU_DTYPE),
                              _col_lhs(vr[b, pl.ds(row, 1), :], eb_ref[...]),
                              Sb * rr[b, pl.ds(prev, 1), :].astype(MXU_DTYPE)]
                res = _stacked_segsum(tiles, bd_ref[...])
                for b in range(B):
                    sab, vb, yb = res[3 * b:3 * b + 3]
                    ld = lambda ref: ref[b, pl.ds(row, 1), :]
                    S[z * B + b] = S[z * B + b] * ld(wr) - sab * ld(akkr) + vb * ld(ktr)
                    yr[b, pl.ds(prev, 1), :] = _colsum(ef_ref[...] * yb)
            return carry

        lax.fori_loop(0, C, step, 0, unroll=SCAN_UNROLL)
        for z in range(2):
            last = C - 1 if z == 0 else 0
            rr, yr = dirs[z][5], dirs[z][6]
            res = _stacked_segsum([S[z * B + b].astype(MXU_DTYPE) * rr[b, last:last + 1, :].astype(MXU_DTYPE)
                                   for b in range(B)], bd_ref[...])
            for b in range(B):
                yr[b, last:last + 1, :] = _colsum(ef_ref[...] * res[b])

    ins, specs = [], []
    for z, rev in ((0, False), (1, True)):
        ins += [w, kt, akk, kk, shifted, shifted]
        specs += [blk(z, 0, rev), blk(z, 0, rev), blk(z, 0, rev), blk3(0, rev), blk3(2, rev), blk3(0, rev)]
    sds = jax.ShapeDtypeStruct
    return pl.pallas_call(
        body, name="scan_fwd", grid=(nC,),
        in_specs=specs + [_full((HEAD_DIM, RWKV_W)), _full((HEAD_DIM, RWKV_W)), _full((256, 256))],
        out_specs=(blk3(0, False), blk3(0, True), pl.BlockSpec((None, 2 * B, HEAD_DIM, RWKV_W), lambda g: (g, 0, 0, 0))),
        out_shape=(sds((B, T, RWKV_W), F32), sds((B, T, RWKV_W), F32), sds((nC, 2 * B, HEAD_DIM, RWKV_W), F32)),
        scratch_shapes=[pltpu.VMEM((2 * B, HEAD_DIM, RWKV_W), F32)],
        compiler_params=_cp(("arbitrary",)),
    )(*ins, eye_b, eye_f, bd)


def _scan_bwd_call(w, kt, akk, kk, shifted, dys, ck, eye_b, eye_f, bd, B, T):
    C = min(SCAN_CHUNK, T)
    nC = T // C
    blk, blk3 = _scan_specs(B, T, C, nC)
    nin = 7

    def body(*refs):
        d0, d1 = refs[:nin], refs[nin:2 * nin]
        ck_ref, eb_ref, ef_ref, bd_ref = refs[2 * nin:2 * nin + 4]
        o0, o1 = refs[2 * nin + 4:2 * nin + 10], refs[2 * nin + 10:2 * nin + 16]
        Sbuf, SAB, VB, DYB, G = refs[2 * nin + 16:]

        @pl.when(pl.program_id(0) == 0)
        def _():
            G[...] = jnp.zeros_like(G)

        Sbuf[0] = ck_ref[...]
        dirs = (d0 + (o0,), d1 + (o1,))

        def fwd(s, carry):
            for z in range(2):
                row = s if z == 0 else C - 1 - s
                wr, ktr, akkr, kkr, vr, rr, dyr, outs = dirs[z]
                tiles = []
                for b in range(B):
                    tiles += [Sbuf[s, z * B + b].astype(MXU_DTYPE) * kkr[b, pl.ds(row, 1), :].astype(MXU_DTYPE),
                              _col_lhs(vr[b, pl.ds(row, 1), :], eb_ref[...]),
                              _col_lhs(dyr[b, pl.ds(row, 1), :], eb_ref[...])]
                res = _stacked_segsum(tiles, bd_ref[...])
                for b in range(B):
                    c = z * B + b
                    sab, vb, dyb = res[3 * b:3 * b + 3]
                    ld = lambda ref: ref[b, pl.ds(row, 1), :]
                    Sn = Sbuf[s, c] * ld(wr) - sab * ld(akkr) + vb * ld(ktr)
                    Sbuf[s + 1, c] = Sn
                    SAB[s, c] = sab
                    VB[s, c] = vb
                    DYB[s, c] = dyb
                    outs[4][b, pl.ds(row, 1), :] = _colsum(Sn * dyb)
            return carry

        lax.fori_loop(0, C, fwd, 0, unroll=SCAN_UNROLL)

        def bwd(it, carry):
            s = C - 1 - it
            for z in range(2):
                row = s if z == 0 else C - 1 - s
                wr, ktr, akkr, kkr, vr, rr, dyr, (dw_o, dkt_o, dakk_o, dkk_o, _, dv_o) = dirs[z]
                tiles, Gcs = [], []
                for b in range(B):
                    c = z * B + b
                    Gc = G[c] + DYB[s, c] * rr[b, pl.ds(row, 1), :]
                    Gcs.append(Gc)
                    Gb = Gc.astype(MXU_DTYPE)
                    tiles += [Gb * akkr[b, pl.ds(row, 1), :].astype(MXU_DTYPE),
                              Gb * ktr[b, pl.ds(row, 1), :].astype(MXU_DTYPE)]
                res = _stacked_segsum(tiles, bd_ref[...])
                for b in range(B):
                    c = z * B + b
                    Gc = Gcs[b]
                    gab, dvb = res[2 * b], res[2 * b + 1]
                    ld = lambda ref: ref[b, pl.ds(row, 1), :]
                    G[c] = Gc * ld(wr) - gab * ld(kkr)
                    S = Sbuf[s, c]
                    for ref, val in ((dkt_o, _colsum(Gc * VB[s, c])), (dv_o, _colsum(ef_ref[...] * dvb)),
                                     (dw_o, _colsum(Gc * S)),
                                     (dakk_o, -_colsum(Gc * SAB[s, c])), (dkk_o, -_colsum(gab * S))):
                        ref[b, pl.ds(row, 1), :] = val
            return carry

        lax.fori_loop(0, C, bwd, 0, unroll=SCAN_UNROLL)

    ins, specs = [], []
    for z, rev in ((0, True), (1, False)):
        ins += [w, kt, akk, kk, shifted, shifted, dys]
        specs += [blk(z, 0, rev), blk(z, 0, rev), blk(z, 0, rev), blk3(0, rev), blk3(2, rev), blk3(0, rev), blk3(0, rev)]
    ins += [ck, eye_b, eye_f, bd]
    specs += [pl.BlockSpec((None, 2 * B, HEAD_DIM, RWKV_W), lambda g: (nC - 1 - g, 0, 0, 0)),
              _full((HEAD_DIM, RWKV_W)), _full((HEAD_DIM, RWKV_W)), _full((256, 256))]
    sds = jax.ShapeDtypeStruct
    out_specs = tuple(blk3(0, True) for _ in range(6)) + tuple(blk3(0, False) for _ in range(6))
    res = pl.pallas_call(
        body, name="scan_bwd", grid=(nC,), in_specs=specs, out_specs=out_specs,
        out_shape=tuple(sds((B, T, RWKV_W), F32) for _ in range(12)),
        scratch_shapes=[pltpu.VMEM((C + 1, 2 * B, HEAD_DIM, RWKV_W), F32)] +
                       [pltpu.VMEM((C, 2 * B, HEAD_DIM, RWKV_W), F32)] * 3 + [pltpu.VMEM((2 * B, HEAD_DIM, RWKV_W), F32)],
        compiler_params=_cp(("arbitrary",)),
    )(*ins)
    return list(res)


def _out_head_call(x2, tgt2, gate, y_att, g_att, y0, y1, shifted, kt, g_rw, w_out, g_post, gn_w, gn_b, r_k, bd, T):
    R = x2.shape[0]
    TT = min(ROW_TILE, T)
    tpe = T // TT

    def body(x_ref, t_ref, gate_ref, ya_ref, ga_ref, y0_ref, y1_ref, r_ref, v_ref, kt_ref, grw_ref, w_ref, gp_ref,
             gnw_ref, gnb_ref, rk_ref, bd_ref,
             loss_o, dy_o, dya_o, dga_o, dys_o, dr_o, dv_o, dkts_o, dgrw_o, dgate_o, gw_o, ggp_o, ggnw_o, ggnb_o, grk_o):
        i = pl.program_id(0)
        bd = bd_ref[...]
        mix = functools.partial(_mix_fn, bd=bd, diff=True)
        (ma, mr), mix_vjp = jax.vjp(mix, ya_ref[...], ga_ref[...], y0_ref[...] + y1_ref[...], r_ref[...], v_ref[...],
                                    kt_ref[0] + kt_ref[1], grw_ref[...], gnw_ref[...], gnb_ref[...], rk_ref[...])
        out = _dot(ma, w_ref[0:ATT_W, :]) + _dot(mr, w_ref[ATT_W:, :])
        loss, loss_vjp = jax.vjp(_loss_fn, out, x_ref[...], t_ref[...], gate_ref[0], gp_ref[...])
        d_out, dy, _, dgate, dgp = loss_vjp(jnp.ones((1, 1), F32))
        dy_o[...] = dy
        dma = _dot_nt(d_out, w_ref[0:ATT_W, :])
        dmr = _dot_nt(d_out, w_ref[ATT_W:, :])
        dya_o[...], dga_o[...], dys_o[...], dr_o[...], dv_o[...], dkts_o[...], dgrw_o[...], dgnw, dgnb, drk = \
            mix_vjp((dma, dmr))
        gw = jnp.concatenate([_dot_tn(ma, d_out), _dot_tn(mr, d_out)], axis=0)
        acc = ((loss_o, jnp.broadcast_to(loss, (8, 128))), (gw_o, gw), (ggp_o, dgp), (ggnw_o, dgnw), (ggnb_o, dgnb),
               (grk_o, drk))

        @pl.when(i == 0)
        def _():
            for ref, val in acc:
                ref[...] = val

        @pl.when(i > 0)
        def _():
            for ref, val in acc:
                ref[...] += val

        @pl.when(i % tpe == 0)
        def _():
            dgate_o[0] = dgate

        @pl.when(i % tpe > 0)
        def _():
            dgate_o[0] += dgate

    row = lambda w, c=0: pl.BlockSpec((TT, w), lambda i: (i, c))
    two = pl.BlockSpec((2, TT, RWKV_W), lambda i: (0, i, 0))
    per_ex = pl.BlockSpec((1, 1, D_MODEL), lambda i: (i // tpe, 0, 0))
    sds = jax.ShapeDtypeStruct
    r512 = sds((R, RWKV_W), F32)
    return pl.pallas_call(
        body, name="out_head", grid=(R // TT,),
        in_specs=[row(D_MODEL), row(D_MODEL), per_ex, row(ATT_W), row(ATT_W), row(RWKV_W), row(RWKV_W), row(RWKV_W, 0),
                  row(RWKV_W, 2), two,
                  row(RWKV_W), _full(w_out.shape), _full((1, D_MODEL)), _full((1, RWKV_W)), _full((1, RWKV_W)),
                  _full((1, RWKV_W)), _full((256, 256))],
        out_specs=(_full((8, 128)), row(D_MODEL), row(ATT_W), row(ATT_W), row(RWKV_W), row(RWKV_W), row(RWKV_W),
                   row(RWKV_W), row(RWKV_W), per_ex, _full((D_MODEL, D_MODEL)), _full((1, D_MODEL)), _full((1, RWKV_W)),
                   _full((1, RWKV_W)), _full((1, RWKV_W))),
        out_shape=(sds((8, 128), F32), sds((R, D_MODEL), F32), r512, r512, r512, r512, r512, r512, r512,
                   sds((R // T, 1, D_MODEL), F32), sds((D_MODEL, D_MODEL), F32), sds((1, D_MODEL), F32),
                   sds((1, RWKV_W), F32), sds((1, RWKV_W), F32), sds((1, RWKV_W), F32)),
        compiler_params=_cp(("arbitrary",)),
    )(x2, tgt2, gate, y_att, g_att, y0, y1, shifted, shifted, kt, g_rw, w_out, g_post, gn_w, gn_b, r_k, bd)


def _in_proj_bwd_call(x2, dy, shift, scale, g_pre, w_in, qg, kg, cos, sin, bd, q_raw, k_raw, dqr, dkp, dvp,
                      d_gatt, d_rin, d_grw, T):
    R = x2.shape[0]
    TT = min(ROW_TILE, T)
    tpe = T // TT

    def body(x_ref, dy_ref, sh_ref, sc_ref, gp_ref, w_ref, qg_ref, kg_ref, cos_ref, sin_ref, bd_ref, q_ref, k_ref,
             dqr_ref, dkp_ref, dvp_ref, dga_ref, drin_ref, dgrw_ref,
             dx_o, dproj_o, dsh_o, dsc_o, ggp_o, gqg_o, gkg_o):
        i = pl.program_id(0)
        cos, sin, bd = cos_ref[...], sin_ref[...], bd_ref[...]
        left = lax.broadcasted_iota(jnp.int32, (1, KV_W), 1) < HEAD_DIM

        def kv_grad(ref):
            a = ref[0] + ref[1]
            b = ref[2] + ref[3]
            return jnp.where(left, a + pltpu.roll(a, HEAD_DIM, 1), b + pltpu.roll(b, HEAD_DIM, 1))

        qfn = functools.partial(_qk_fn, cos=jnp.tile(cos, (1, 4)), sin=jnp.tile(sin, (1, 4)), bd=bd, scale=ATT_SCALE,
                                diff=True)
        _, q_vjp = jax.vjp(qfn, q_ref[...], qg_ref[...])
        dq, gqg = q_vjp(dqr_ref[...])
        kfn = functools.partial(_qk_fn, cos=cos, sin=sin, bd=bd, scale=1.0, diff=True)
        _, k_vjp = jax.vjp(kfn, k_ref[...], kg_ref[...])
        dk, gkg = k_vjp(kv_grad(dkp_ref))
        pieces = ((C_Q, C_K, dq), (C_K, C_V, dk), (C_V, C_GA, kv_grad(dvp_ref)), (C_GA, C_RIN, dga_ref[...]),
                  (C_RIN, C_GRW, drin_ref[...]), (C_GRW, C_END, dgrw_ref[...]))
        dh = jnp.zeros((TT, D_MODEL), F32)
        for c0, c1, val in pieces:
            vb = val.astype(MXU_DTYPE)
            dproj_o[:, c0:c1] = vb
            dh = dh + _dot_nt(vb, w_ref[:, c0:c1])
        _, pre_vjp = jax.vjp(_pre_fn, x_ref[...], sh_ref[0], sc_ref[0], gp_ref[...])
        dx, dsh, dsc, ggp = pre_vjp(dh)
        dx_o[...] = dx + dy_ref[...]
        acc = ((ggp_o, ggp), (gqg_o, gqg), (gkg_o, gkg))

        @pl.when(i == 0)
        def _():
            for ref, val in acc:
                ref[...] = val

        @pl.when(i > 0)
        def _():
            for ref, val in acc:
                ref[...] += val

        @pl.when(i % tpe == 0)
        def _():
            dsh_o[0] = dsh
            dsc_o[0] = dsc

        @pl.when(i % tpe > 0)
        def _():
            dsh_o[0] += dsh
            dsc_o[0] += dsc

    row = lambda w: pl.BlockSpec((TT, w), lambda i: (i, 0))
    per_ex = pl.BlockSpec((1, 1, D_MODEL), lambda i: (i // tpe, 0, 0))
    tab = pl.BlockSpec((TT, KV_W), lambda i: (i % tpe, 0))
    pad = pl.BlockSpec((4, TT, KV_W), lambda i: (0, i, 0))
    sds = jax.ShapeDtypeStruct
    nb = R // T
    return pl.pallas_call(
        body, name="in_proj_bwd", grid=(R // TT,),
        in_specs=[row(D_MODEL), row(D_MODEL), per_ex, per_ex, _full((1, D_MODEL)), _full(w_in.shape), _full((1, ATT_W)),
                  _full((1, KV_W)), tab, tab, _full((256, 256)), row(ATT_W), row(KV_W), row(ATT_W), pad, pad,
                  row(ATT_W), row(SHIFT_W), row(RWKV_W)],
        out_specs=(row(D_MODEL), row(C_END), per_ex, per_ex, _full((1, D_MODEL)), _full((1, ATT_W)), _full((1, KV_W))),
        out_shape=(sds((R, D_MODEL), F32), sds((R, C_END), MXU_DTYPE), sds((nb, 1, D_MODEL), F32),
                   sds((nb, 1, D_MODEL), F32), sds((1, D_MODEL), F32), sds((1, ATT_W), F32), sds((1, KV_W), F32)),
        compiler_params=_cp(("arbitrary",)),
    )(x2, dy, shift, scale, g_pre, w_in, qg, kg, cos, sin, bd, q_raw, k_raw, dqr, dkp, dvp, d_gatt, d_rin, d_grw)


def _w_in_grad_call(hb, dproj, T):
    R = hb.shape[0]
    TT = min(ROW_TILE, T)
    CB = 1152

    def body(h_ref, d_ref, o_ref):
        g = _dot_tn(h_ref[...], d_ref[...])

        @pl.when(pl.program_id(1) == 0)
        def _():
            o_ref[...] = g

        @pl.when(pl.program_id(1) > 0)
        def _():
            o_ref[...] += g

    return pl.pallas_call(
        body, name="w_in_grad", grid=(C_END // CB, R // TT),
        in_specs=[pl.BlockSpec((TT, D_MODEL), lambda j, i: (i, 0)), pl.BlockSpec((TT, CB), lambda j, i: (i, j))],
        out_specs=pl.BlockSpec((D_MODEL, CB), lambda j, i: (0, j)),
        out_shape=jax.ShapeDtypeStruct((D_MODEL, C_END), F32), compiler_params=_cp(("arbitrary", "arbitrary")),
    )(hb, dproj)


def _adam_call(parts, w, m, v, name, row_tile=None):
    P, M, N = parts.shape
    TM = M if row_tile is None else row_tile

    def body(p_ref, w_ref, m_ref, v_ref, g_o, d_o, m_o, v_o):
        g = p_ref[0]
        for j in range(1, P):
            g = g + p_ref[j]
        m2 = ADAM_B1 * m_ref[...] + (1.0 - ADAM_B1) * g
        v2 = ADAM_B2 * v_ref[...] + (1.0 - ADAM_B2) * jnp.square(g)
        m_hat = m2 / (1.0 - ADAM_B1 ** ADAM_STEP)
        v_hat = v2 / (1.0 - ADAM_B2 ** ADAM_STEP)
        g_o[...] = g
        d_o[...] = -ADAM_LR * (m_hat / (jnp.sqrt(v_hat) + ADAM_EPS) + ADAM_WD * w_ref[...])
        m_o[...] = m2
        v_o[...] = v2

    blk = pl.BlockSpec((TM, N), lambda i: (i, 0))
    return pl.pallas_call(
        body, name=name, grid=(M // TM,),
        in_specs=[pl.BlockSpec((P, TM, N), lambda i: (0, i, 0)), blk, blk, blk], out_specs=(blk,) * 4,
        out_shape=(jax.ShapeDtypeStruct((M, N), F32),) * 4, compiler_params=_cp(("arbitrary",)),
    )(parts, w, m, v)


_SMALL_ROWS = 136


def _pack_small(taps, w_up, w0, a_up, a0):
    flat = jnp.concatenate([taps.reshape(-1), w_up.reshape(-1), w0.reshape(-1), a_up.reshape(-1), a0.reshape(-1)])
    return jnp.pad(flat, (0, _SMALL_ROWS * 128 - flat.shape[0])).reshape(_SMALL_ROWS, 128)


def _unpack_small(packed):
    n = packed.shape[0]
    flat = packed.reshape(n, -1)
    out, o = [], 0
    for shape in ((3, 208), (2, 64, 64), (2, 64), (2, 64, 64), (2, 64)):
        size = 1
        for s in shape:
            size *= s
        out.append(flat[:, o:o + size].reshape((n,) + shape))
        o += size
    return out


def _cols_to_full(blocks):
    nd = blocks.ndim
    moved = jnp.moveaxis(blocks, 0, nd - 2)
    return moved.reshape(moved.shape[:-2] + (moved.shape[-2] * moved.shape[-1],))


def _full_to_cols(full):
    k = full.shape[-1] // NDEV
    return jnp.moveaxis(full.reshape(full.shape[:-1] + (NDEV, k)), -2, 0)


_REP_SIZES = (("g_pre", 1024), ("q_norm_g", 64), ("k_norm_g", 64), ("k_k", 512), ("k_a", 512), ("r_k", 512),
              ("gn_w", 512), ("gn_b", 512), ("g_post", 1024))
_REP_ROWS = 40


def kernel(x, c, w_ada, b_ada, g_pre, w_in, q_norm_g, k_norm_g, shift_taps, w_up, w0, a_up, a0, k_k, k_a, r_k, gn_w, gn_b, w_out, g_post, loss_target, m_w_ada, m_b_ada, m_g_pre, m_w_in, m_q_norm_g, m_k_norm_g, m_shift_taps, m_w_up, m_w0, m_a_up, m_a0, m_k_k, m_k_a, m_r_k, m_gn_w, m_gn_b, m_w_out, m_g_post, v_w_ada, v_b_ada, v_g_pre, v_w_in, v_q_norm_g, v_k_norm_g, v_shift_taps, v_w_up, v_w0, v_a_up, v_a0, v_k_k, v_k_a, v_r_k, v_gn_w, v_gn_b, v_w_out, v_g_post):
    B, T, _ = x.shape
    R = B * T
    me = 4 * lax.axis_index("x") + 2 * lax.axis_index("y") + lax.axis_index("c")
    x2 = x.reshape(R, D_MODEL)
    tgt2 = loss_target.reshape(R, D_MODEL)

    seg = jnp.arange(256) // HEAD_DIM
    bd = (seg[:, None] == seg[None, :]).astype(MXU_DTYPE)
    eye = (jnp.arange(HEAD_DIM)[:, None] == (jnp.arange(RWKV_W) % HEAD_DIM)[None, :])
    eye_b, eye_f = eye.astype(MXU_DTYPE), eye.astype(F32)
    cos, sin = _rope_tables(T)

    c_g, w_in_g, w_out_g, small_g = _exchange(
        [c, w_in[0].astype(MXU_DTYPE), w_out[0].astype(MXU_DTYPE),
         _pack_small(shift_taps[0], w_up[0], w0[0], a_up[0], a0[0])], [False] * 4, "gather_params")
    c_all = c_g.reshape(NDEV * B, D_MODEL)
    w_in_f = _cols_to_full(w_in_g)
    w_out_f = w_out_g.reshape(D_MODEL, D_MODEL)
    taps_b, w_up_b, w0_b, a_up_b, a0_b = _unpack_small(small_g)
    taps_f = jnp.pad(_cols_to_full(taps_b), ((0, 5), (0, 0)))
    w_up_f, a_up_f = _cols_to_full(w_up_b), _cols_to_full(a_up_b)
    w0_f, a0_f = _cols_to_full(w0_b), _cols_to_full(a0_b)
    wup_pad = jnp.pad(w_up_f, ((0, 0), (0, 64), (0, 0))).astype(MXU_DTYPE)
    aup_pad = jnp.pad(a_up_f, ((0, 0), (64, 0), (0, 0))).astype(MXU_DTYPE)

    ncol = w_ada.shape[2]
    b_cols = lax.dynamic_slice(b_ada, (0, me * ncol), (1, ncol))
    mod_cols = _mod_call(c_all, w_ada[0].astype(MXU_DTYPE), b_cols)
    (mod_g,) = _exchange([mod_cols], [False], "gather_mod")
    mod = lax.dynamic_slice(_cols_to_full(mod_g), (me * B, 0), (B, 3 * D_MODEL))
    shift, scale, gate = [mod[:, j * D_MODEL:(j + 1) * D_MODEL].reshape(B, 1, D_MODEL) for j in range(3)]

    qg = jnp.tile(q_norm_g, (1, ATT_W // HEAD_DIM))
    kg = jnp.tile(k_norm_g, (1, KV_W // HEAD_DIM))
    rk_row = r_k.reshape(1, RWKV_W)

    hb, qr, kpad, vpad, q_raw, k_raw, g_att, rin, g_rw = _in_proj_call(
        x2, shift, scale, g_pre, w_in_f, qg, kg, cos, sin, bd, T)
    y_att = _att_fwd_call(qr, kpad, vpad, B, T)
    shifted = _shift_fwd_call(rin, taps_f, T)
    w_s, kt_s, akk_s, kk_s = _rwkv_prep_call(shifted, wup_pad, aup_pad, w0_f, a0_f, k_k, k_a, bd, T)
    sh3 = shifted.reshape(B, T, SHIFT_W)
    r4 = lambda a: a.reshape(2, B, T, RWKV_W)
    y0, y1, ck = _scan_fwd_call(r4(w_s), r4(kt_s), r4(akk_s), kk_s.reshape(B, T, RWKV_W), sh3, eye_b, eye_f, bd, B, T)

    (loss_blk, dy, d_yatt, d_gatt, d_ys, d_r2, d_v2, d_kts, d_grw, d_gate, g_wout, g_gpost, g_gnw, g_gnb,
     g_rk) = _out_head_call(x2, tgt2, gate, y_att, g_att, y0.reshape(R, RWKV_W), y1.reshape(R, RWKV_W), shifted, kt_s,
                            g_rw, w_out_f, g_post, gn_w, gn_b, rk_row, bd, T)
    scan_cts = _scan_bwd_call(r4(w_s), r4(kt_s), r4(akk_s), kk_s.reshape(B, T, RWKV_W), sh3,
                              d_ys.reshape(B, T, RWKV_W), ck, eye_b, eye_f, bd, B, T)
    scan_cts = [a.reshape(R, RWKV_W) for a in scan_cts]
    d_shifted, g_wup, g_aup, g_w0, g_a0, g_kk, g_ka = _rwkv_prep_bwd_call(
        shifted, scan_cts + [d_r2, d_v2, d_kts], wup_pad, aup_pad, w0_f, a0_f, k_k, k_a, bd, T)
    d_rin, g_taps = _shift_bwd_call(rin, d_shifted, taps_f, T)
    dqr, dkp, dvp = _att_bwd_call(qr, kpad, vpad, d_yatt, B, T)
    grad_x, dproj, d_shift, d_scale, g_gpre, g_qg, g_kg = _in_proj_bwd_call(
        x2, dy, shift, scale, g_pre, w_in_f, qg, kg, cos, sin, bd, q_raw, k_raw, dqr, dkp, dvp, d_gatt, d_rin, d_grw, T)
    g_win = _w_in_grad_call(hb, dproj, T)

    rep = jnp.concatenate([g_gpre.reshape(-1), g_qg.reshape(-1, HEAD_DIM).sum(0), g_kg.reshape(-1, HEAD_DIM).sum(0),
                           g_kk.reshape(-1), g_ka.reshape(-1), g_rk.reshape(-1), g_gnw.reshape(-1), g_gnb.reshape(-1),
                           g_gpost.reshape(-1)])
    rep = jnp.pad(rep, (0, _REP_ROWS * 128 - rep.shape[0])).reshape(_REP_ROWS, 128)
    dmod = jnp.concatenate([d_shift, d_scale, d_gate], axis=2).reshape(B, 3 * D_MODEL)
    small_parts = jax.vmap(_pack_small)(_full_to_cols(g_taps[:3]), _full_to_cols(g_wup[:, :64, :]), _full_to_cols(g_w0),
                                        _full_to_cols(g_aup[:, 64:, :]), _full_to_cols(g_a0))
    p_win, p_wout, p_small, dmod_g, rep_g = _exchange(
        [_full_to_cols(g_win), g_wout.reshape(NDEV, D_MODEL // NDEV, D_MODEL), small_parts, dmod, rep],
        [True, True, True, False, False], "reduce_grads")
    dmod_all = dmod_g.reshape(NDEV * B, 3 * D_MODEL)
    g_wada = _wada_grad_call(c_all, lax.dynamic_slice(dmod_all, (0, me * ncol), (NDEV * B, ncol)))

    res = {}

    def adam(name, parts, w, m, v, row_tile=None):
        shape = w.shape
        two_d = (-1, shape[-1])
        out = _adam_call(parts.reshape((parts.shape[0],) + w.reshape(two_d).shape), w.reshape(two_d), m.reshape(two_d),
                         v.reshape(two_d), "adam_" + name, row_tile)
        res[name] = [o.reshape(shape) for o in out]

    adam("w_ada", g_wada[None], w_ada, m_w_ada, v_w_ada)
    adam("b_ada", dmod_all.reshape(NDEV * B, 1, 3 * D_MODEL), b_ada, m_b_ada, v_b_ada)
    adam("w_in", p_win, w_in, m_w_in, v_w_in, 128)
    adam("w_out", p_wout, w_out, m_w_out, v_w_out)
    taps_p, wup_p, w0_p, aup_p, a0_p = _unpack_small(p_small)
    adam("shift_taps", taps_p, shift_taps, m_shift_taps, v_shift_taps)
    adam("w_up", wup_p, w_up, m_w_up, v_w_up)
    adam("w0", w0_p, w0, m_w0, v_w0)
    adam("a_up", aup_p, a_up, m_a_up, v_a_up)
    adam("a0", a0_p, a0, m_a0, v_a0)
    rep_flat = rep_g.reshape(NDEV, -1)
    off = 0
    given = dict(g_pre=(g_pre, m_g_pre, v_g_pre), q_norm_g=(q_norm_g, m_q_norm_g, v_q_norm_g),
                 k_norm_g=(k_norm_g, m_k_norm_g, v_k_norm_g), k_k=(k_k, m_k_k, v_k_k), k_a=(k_a, m_k_a, v_k_a),
                 r_k=(r_k, m_r_k, v_r_k), gn_w=(gn_w, m_gn_w, v_gn_w), gn_b=(gn_b, m_gn_b, v_gn_b),
                 g_post=(g_post, m_g_post, v_g_post))
    for name, size in _REP_SIZES:
        adam(name, rep_flat[:, off:off + size], *given[name])
        off += size

    loss = lax.psum(loss_blk[0, 0], ("x", "y", "c"))
    order = ["w_ada", "b_ada", "g_pre", "w_in", "q_norm_g", "k_norm_g", "shift_taps", "w_up", "w0", "a_up", "a0", "k_k",
             "k_a", "r_k", "gn_w", "gn_b", "w_out", "g_post"]
    return (loss, grad_x.reshape(B, T, D_MODEL), *[res[n][0] for n in order], *[res[n][1] for n in order],
            *[res[n][2] for n in order], *[res[n][3] for n in order])
```

```python
import functools

import jax
import jax.numpy as jnp
from jax import lax
from jax.experimental import pallas as pl
from jax.experimental.pallas import tpu as pltpu

F32 = jnp.float32
MXU_DTYPE = jnp.bfloat16
MESH = pl.DeviceIdType.MESH
NDEV = 8

D_MODEL = 1024
HEAD_DIM = 64
ATT_W = 512
KV_W = 128
RWKV_W = 512
LORA_W = 128
SHIFT_W = 3 * RWKV_W + LORA_W
GRID_W = 64
ROPE_THETA = 10000.0
DECAY_SCALE = 0.6065306597126334
NORM_EPS = 1e-6
GN_EPS = 64e-5
L2_EPS = 1e-12
ATT_SCALE = HEAD_DIM ** -0.5
C_Q, C_K, C_V, C_GA, C_RIN, C_GRW, C_END = 0, 512, 640, 768, 1280, 2944, 3456

ADAM_LR, ADAM_B1, ADAM_B2, ADAM_EPS, ADAM_WD, ADAM_STEP = 0.001, 0.9, 0.999, 1e-08, 0.01, 10

ROW_TILE = 256
SCAN_CHUNK = 16
SCAN_UNROLL = 4
VMEM_LIMIT = 56 * 1024 * 1024


def _cp(sem=None):
    return pltpu.CompilerParams(dimension_semantics=sem, vmem_limit_bytes=VMEM_LIMIT)


def _dot(a, b, dims=(((1,), (0,)), ((), ()))):
    return lax.dot_general(a.astype(MXU_DTYPE), b.astype(MXU_DTYPE), dims, preferred_element_type=F32)


def _dot_nt(a, b):
    return _dot(a, b, (((1,), (1,)), ((), ())))


def _dot_tn(a, b):
    return _dot(a, b, (((0,), (0,)), ((), ())))


def _seg_dot(xb, bd):
    n = xb.shape[1]
    if n <= 256:
        return jnp.dot(xb, bd[:n, :n], preferred_element_type=F32)
    parts = [jnp.dot(xb[:, c:c + 256], bd, preferred_element_type=F32) for c in range(0, n, 256)]
    return jnp.concatenate(parts, axis=1)


def _split3(x):
    hi = x.astype(MXU_DTYPE)
    r1 = x - hi.astype(F32)
    mid = r1.astype(MXU_DTYPE)
    lo = (r1 - mid.astype(F32)).astype(MXU_DTYPE)
    return hi, mid, lo


def _segsum_raw(x, bd):
    hi, mid, lo = _split3(x)
    return _seg_dot(hi, bd) + _seg_dot(mid, bd) + _seg_dot(lo, bd)


@jax.custom_vjp
def _segsum_d(x, bd):
    return _segsum_raw(x, bd)


def _segsum_d_fwd(x, bd):
    return _segsum_raw(x, bd), bd


def _segsum_d_bwd(bd, ct):
    return _segsum_raw(ct, bd), jnp.zeros_like(bd)


_segsum_d.defvjp(_segsum_d_fwd, _segsum_d_bwd)


def _rope_tables(T):
    t = jnp.arange(T, dtype=F32)
    row = jnp.floor(t / GRID_W)
    col = t - row * GRID_W
    n_freq = HEAD_DIM // 4
    inv_freq = ROPE_THETA ** (-jnp.arange(n_freq, dtype=F32) / n_freq)
    d = jnp.arange(HEAD_DIM)
    pos = jnp.where((d < HEAD_DIM // 2)[None, :], row[:, None], col[:, None])
    ang = pos * inv_freq[d % n_freq][None, :]
    sign = jnp.where((d % 32) < 16, -1.0, 1.0).astype(F32)[None, :]
    cos = jnp.cos(ang)
    sin = jnp.sin(ang) * sign
    return jnp.tile(cos, (1, 2)), jnp.tile(sin, (1, 2))


def _rope_raw(x, cos, sin):
    n = x.shape[1]
    lane = lax.broadcasted_iota(jnp.int32, (1, n), 1)
    first = (lane % 32) < 16
    partner = jnp.where(first, pltpu.roll(x, n - 16, 1), pltpu.roll(x, 16, 1))
    return x * cos + partner * sin


@jax.custom_vjp
def _rope_d(x, cos, sin):
    return _rope_raw(x, cos, sin)


def _rope_d_fwd(x, cos, sin):
    return _rope_raw(x, cos, sin), (cos, sin)


def _rope_d_bwd(res, ct):
    cos, sin = res
    return _rope_raw(ct, cos, -sin), jnp.zeros_like(cos), jnp.zeros_like(sin)


_rope_d.defvjp(_rope_d_fwd, _rope_d_bwd)


def _rms(x, g):
    return x * lax.rsqrt(jnp.mean(x * x, axis=-1, keepdims=True) + NORM_EPS) * g


def _pre_fn(x, shift, scale, g_pre):
    return _rms(x, g_pre) * (1.0 + scale) + shift


def _qk_fn(q, g, cos, sin, bd, scale, diff):
    segsum = _segsum_d if diff else _segsum_raw
    rope = _rope_d if diff else _rope_raw
    qn = q * lax.rsqrt(segsum(q * q, bd) * (1.0 / HEAD_DIM) + NORM_EPS) * g
    return rope(qn, cos, sin) * scale


def _silu(x):
    return x * jax.nn.sigmoid(x)


def _rwkv_pw(k, pw0, pw1, pa0, pa1, w0, a0, k_k, k_a, bd, diff):
    segsum = _segsum_d if diff else _segsum_raw
    kk = k * k_k
    kk = kk * lax.rsqrt(segsum(kk * kk, bd) + L2_EPS)
    ws, kts, akks = [], [], []
    for z, (pw, pa) in enumerate(((pw0, pa0), (pw1, pa1))):
        w = jnp.exp(-DECAY_SCALE * jax.nn.sigmoid(w0[z:z + 1, :] + pw))
        a = jax.nn.sigmoid(a0[z:z + 1, :] + pa)
        ws.append(w)
        kts.append(k * (1.0 + (a - 1.0) * k_a))
        akks.append(a * kk)
    return ws[0], ws[1], kts[0], kts[1], akks[0], akks[1], kk


def _mix_fn(y_att, g_att, ys, r, v, kts, g_rw, gn_w, gn_b, r_k, bd, diff):
    segsum = _segsum_d if diff else _segsum_raw
    mu = segsum(ys, bd) * (1.0 / HEAD_DIM)
    d = ys - mu
    var = segsum(d * d, bd) * (1.0 / HEAD_DIM)
    yn = d * lax.rsqrt(var + GN_EPS) * gn_w + gn_b
    bonus = segsum(r * kts * r_k, bd) * v
    return y_att * _silu(g_att), (yn + bonus) * _silu(g_rw)


def _loss_fn(out, x, tgt, gate, g_post):
    e = x + gate * _rms(out, g_post) - tgt
    s = jnp.sum(e * e, axis=1, keepdims=True)
    return jnp.sum(s, axis=0, keepdims=True) * (0.5 / D_MODEL)


def _exchange(arrays, scatter, name):
    n = len(arrays)
    out_shape = tuple(
        jax.ShapeDtypeStruct((NDEV,) + tuple(a.shape[1:] if sc else a.shape), a.dtype)
        for a, sc in zip(arrays, scatter))

    def body(*refs):
        ins, outs = refs[:n], refs[n:2 * n]
        send_sems, recv_sems, local_sems = refs[2 * n:]
        ix, iy, ic = lax.axis_index("x"), lax.axis_index("y"), lax.axis_index("c")
        me = 4 * ix + 2 * iy + ic

        def src(k, p):
            return ins[k].at[p] if scatter[k] else ins[k]

        local = [pltpu.make_async_copy(src(k, me), outs[k].at[me], local_sems.at[k]) for k in range(n)]
        for cp in local:
            cp.start()
        sends, recvs = [], []
        for m in range(1, NDEV):
            px = 1 - ix if (m >> 2) & 1 else ix
            py = 1 - iy if (m >> 1) & 1 else iy
            pc = 1 - ic if m & 1 else ic
            p = 4 * px + 2 * py + pc
            for k in range(n):
                common = dict(send_sem=send_sems.at[k, m - 1], recv_sem=recv_sems.at[k, m - 1],
                              device_id=(px, py, pc), device_id_type=MESH)
                sends.append(pltpu.make_async_remote_copy(src_ref=src(k, p), dst_ref=outs[k].at[me], **common))
                recvs.append(pltpu.make_async_remote_copy(src_ref=src(k, p), dst_ref=outs[k].at[p], **common))
        for cp in sends:
            cp.start()
        for cp in recvs:
            cp.wait_recv()
        for cp in sends:
            cp.wait_send()
        for cp in local:
            cp.wait()

    any_spec = pl.BlockSpec(memory_space=pl.ANY)
    return pl.pallas_call(
        body, name=name, out_shape=out_shape,
        in_specs=[any_spec] * n, out_specs=tuple([any_spec] * n),
        scratch_shapes=[pltpu.SemaphoreType.DMA((n, NDEV - 1)), pltpu.SemaphoreType.DMA((n, NDEV - 1)),
                        pltpu.SemaphoreType.DMA((n,))],
    )(*arrays)


def _mod_call(c_all, w_ada, b_cols):
    def body(c_ref, w_ref, b_ref, o_ref):
        o_ref[...] = _dot(_silu(c_ref[...]), w_ref[...]) + b_ref[...]

    return pl.pallas_call(body, name="mod_fwd",
                          out_shape=jax.ShapeDtypeStruct((c_all.shape[0], w_ada.shape[1]), F32))(c_all, w_ada, b_cols)


def _wada_grad_call(c_all, dmod_cols):
    def body(c_ref, d_ref, o_ref):
        o_ref[...] = _dot_tn(_silu(c_ref[...]), d_ref[...])

    return pl.pallas_call(body, name="w_ada_grad",
                          out_shape=jax.ShapeDtypeStruct((c_all.shape[1], dmod_cols.shape[1]), F32))(c_all, dmod_cols)


def _full(shape):
    nd = len(shape)
    return pl.BlockSpec(shape, lambda *_: (0,) * nd)


def _in_proj_call(x2, shift, scale, g_pre, w_in, qg, kg, cos, sin, bd, T):
    R = x2.shape[0]
    TT = min(ROW_TILE, T)
    tpe = T // TT

    def body(x_ref, sh_ref, sc_ref, gp_ref, w_ref, qg_ref, kg_ref, cos_ref, sin_ref, bd_ref,
             hb_ref, qr_ref, kpad_ref, vpad_ref, qraw_ref, kraw_ref, gatt_ref, rin_ref, grw_ref):
        h = _pre_fn(x_ref[...], sh_ref[0], sc_ref[0], gp_ref[...])
        hb = h.astype(MXU_DTYPE)
        hb_ref[...] = hb

        def proj(c0, c1):
            return jnp.dot(hb, w_ref[:, c0:c1], preferred_element_type=F32)

        q = proj(C_Q, C_K)
        k = proj(C_K, C_V)
        v = proj(C_V, C_GA)
        gatt_ref[...] = proj(C_GA, C_RIN)
        rin_ref[...] = proj(C_RIN, C_GRW)
        grw_ref[...] = proj(C_GRW, C_END)
        qraw_ref[...] = q
        kraw_ref[...] = k
        cos, sin, bd = cos_ref[...], sin_ref[...], bd_ref[...]
        qr = _qk_fn(q, qg_ref[...], jnp.tile(cos, (1, 4)), jnp.tile(sin, (1, 4)), bd, ATT_SCALE, False)
        qr_ref[...] = qr.astype(MXU_DTYPE)
        kr = _qk_fn(k, kg_ref[...], cos, sin, bd, 1.0, False)
        left = lax.broadcasted_iota(jnp.int32, (1, KV_W), 1) < HEAD_DIM
        for ref, val in ((kpad_ref, kr), (vpad_ref, v)):
            h0l = jnp.where(left, val, 0.0)
            h1r = jnp.where(left, 0.0, val)
            ref[0] = h0l.astype(MXU_DTYPE)
            ref[1] = pltpu.roll(h0l, HEAD_DIM, 1).astype(MXU_DTYPE)
            ref[2] = pltpu.roll(h1r, HEAD_DIM, 1).astype(MXU_DTYPE)
            ref[3] = h1r.astype(MXU_DTYPE)

    row = lambda w: pl.BlockSpec((TT, w), lambda i: (i, 0))
    per_ex = pl.BlockSpec((1, 1, D_MODEL), lambda i: (i // tpe, 0, 0))
    tab = pl.BlockSpec((TT, KV_W), lambda i: (i % tpe, 0))
    pad = pl.BlockSpec((4, TT, KV_W), lambda i: (0, i, 0))
    sds = jax.ShapeDtypeStruct
    return pl.pallas_call(
        body, name="in_proj", grid=(R // TT,),
        in_specs=[row(D_MODEL), per_ex, per_ex, _full((1, D_MODEL)), _full(w_in.shape), _full((1, ATT_W)),
                  _full((1, KV_W)), tab, tab, _full((256, 256))],
        out_specs=(row(D_MODEL), row(ATT_W), pad, pad, row(ATT_W), row(KV_W), row(ATT_W), row(SHIFT_W), row(RWKV_W)),
        out_shape=(sds((R, D_MODEL), MXU_DTYPE), sds((R, ATT_W), MXU_DTYPE), sds((4, R, KV_W), MXU_DTYPE),
                   sds((4, R, KV_W), MXU_DTYPE), sds((R, ATT_W), F32), sds((R, KV_W), F32), sds((R, ATT_W), F32),
                   sds((R, SHIFT_W), F32), sds((R, RWKV_W), F32)),
        compiler_params=_cp(("arbitrary",)),
    )(x2, shift, scale, g_pre, w_in, qg, kg, cos, sin, bd)


def _softmax_rows(s):
    m = jnp.max(s, axis=1, keepdims=True)
    e = jnp.exp(s - m)
    return e / jnp.sum(e, axis=1, keepdims=True)


def _att_specs(T, TQ):
    nq = T // TQ
    qspec = pl.BlockSpec((TQ, KV_W), lambda b, p, i: (b * nq + i, p))
    side = lambda s: pl.BlockSpec((None, T, KV_W), lambda b, p, i: (2 * (p // 2) + s, b, 0))
    return nq, qspec, side


def _att_fwd_call(qr, kpad, vpad, B, T):
    TQ = min(ROW_TILE, T)
    nq, qspec, side = _att_specs(T, TQ)

    def body(q_ref, kl_ref, kr_ref, vl_ref, vr_ref, o_ref):
        q = q_ref[...]
        pa = _softmax_rows(_dot_nt(q, kl_ref[...]))
        pb = _softmax_rows(_dot_nt(q, kr_ref[...]))
        o_ref[...] = _dot(pa, vl_ref[...]) + _dot(pb, vr_ref[...])

    return pl.pallas_call(
        body, name="att_fwd", grid=(B, 4, nq),
        in_specs=[qspec, side(0), side(1), side(0), side(1)], out_specs=qspec,
        out_shape=jax.ShapeDtypeStruct((B * T, ATT_W), F32),
        compiler_params=_cp(("arbitrary",) * 3),
    )(qr, kpad, kpad, vpad, vpad)


def _att_bwd_call(qr, kpad, vpad, d_o, B, T):
    TQ = min(ROW_TILE, T)
    nq, qspec, side = _att_specs(T, TQ)

    def body(q_ref, kl_ref, kr_ref, vl_ref, vr_ref, do_ref, dq_ref, dk_ref, dv_ref):
        i = pl.program_id(2)
        q, do = q_ref[...], do_ref[...]
        left = lax.broadcasted_iota(jnp.int32, (1, KV_W), 1) < HEAD_DIM
        dq = jnp.zeros((TQ, KV_W), F32)
        dk = jnp.zeros((T, KV_W), F32)
        dv = jnp.zeros((T, KV_W), F32)
        for k_ref, v_ref, mask in ((kl_ref, vl_ref, left), (kr_ref, vr_ref, jnp.logical_not(left))):
            kk, vv = k_ref[...], v_ref[...]
            p = _softmax_rows(_dot_nt(q, kk))
            dp = _dot_nt(do, vv)
            ds = p * (dp - jnp.sum(p * dp, axis=1, keepdims=True))
            dq = dq + _dot(ds, kk)
            dk = dk + _dot_tn(ds, jnp.where(mask, q, jnp.zeros_like(q)))
            dv = dv + _dot_tn(p, jnp.where(mask, do, 0.0))
        dq_ref[...] = dq

        @pl.when(i == 0)
        def _():
            dk_ref[...] = dk
            dv_ref[...] = dv

        @pl.when(i > 0)
        def _():
            dk_ref[...] += dk
            dv_ref[...] += dv

    acc = pl.BlockSpec((None, T, KV_W), lambda b, p, i: (p, b, 0))
    sds = jax.ShapeDtypeStruct
    return pl.pallas_call(
        body, name="att_bwd", grid=(B, 4, nq),
        in_specs=[qspec, side(0), side(1), side(0), side(1), qspec], out_specs=(qspec, acc, acc),
        out_shape=(sds((B * T, ATT_W), F32), sds((4, B * T, KV_W), F32), sds((4, B * T, KV_W), F32)),
        compiler_params=_cp(("arbitrary",) * 3),
    )(qr, kpad, kpad, vpad, vpad, d_o)


def _shift_specs(R, T, TT, width):
    tpe = T // TT
    nb8 = R // 8
    cur = pl.BlockSpec((TT, width), lambda i: (i, 0))
    prev = pl.BlockSpec((8, width), lambda i: (jnp.maximum(i * (TT // 8) - 1, 0), 0))
    nxt = pl.BlockSpec((8, width), lambda i: (jnp.minimum((i + 1) * (TT // 8), nb8 - 1), 0))
    return tpe, cur, prev, nxt


def _neighbours(cur, prev8, next8, i, tpe, TT):
    rows = lax.broadcasted_iota(jnp.int32, (TT, 1), 0)
    first = jnp.where(i % tpe == 0, 0.0, 1.0)
    last = jnp.where(i % tpe == tpe - 1, 0.0, 1.0)
    before = jnp.where(rows == 0, prev8[7:8, :] * first, pltpu.roll(cur, 1, 0))
    after = jnp.where(rows == TT - 1, next8[0:1, :] * last, pltpu.roll(cur, TT - 1, 0))
    return before, after


def _shift_fwd_call(x, taps, T):
    R, width = x.shape
    TT = min(ROW_TILE, T)
    tpe, cur, prev, nxt = _shift_specs(R, T, TT, width)

    def body(x_ref, p_ref, n_ref, t_ref, o_ref):
        xc = x_ref[...]
        before, after = _neighbours(xc, p_ref[...], n_ref[...], pl.program_id(0), tpe, TT)
        o_ref[...] = t_ref[0:1, :] * before + t_ref[1:2, :] * xc + t_ref[2:3, :] * after

    return pl.pallas_call(
        body, name="shift_fwd", grid=(R // TT,), in_specs=[cur, prev, nxt, _full(taps.shape)], out_specs=cur,
        out_shape=jax.ShapeDtypeStruct((R, width), F32), compiler_params=_cp(("arbitrary",)),
    )(x, x, x, taps)


def _shift_bwd_call(x, d, taps, T):
    R, width = x.shape
    TT = min(ROW_TILE, T)
    tpe, cur, prev, nxt = _shift_specs(R, T, TT, width)

    def body(x_ref, xp_ref, xn_ref, d_ref, dp_ref, dn_ref, t_ref, dx_ref, dt_ref):
        i = pl.program_id(0)
        xc, dc = x_ref[...], d_ref[...]
        d_before, d_after = _neighbours(dc, dp_ref[...], dn_ref[...], i, tpe, TT)
        dx_ref[...] = t_ref[2:3, :] * d_before + t_ref[1:2, :] * dc + t_ref[0:1, :] * d_after
        x_before, x_after = _neighbours(xc, xp_ref[...], xn_ref[...], i, tpe, TT)
        @pl.when(i == 0)
        def _():
            dt_ref[...] = jnp.zeros_like(dt_ref)

        for j, xs in enumerate((x_before, xc, x_after)):
            dt_ref[j:j + 1, :] += jnp.sum(dc * xs, axis=0, keepdims=True)

    return pl.pallas_call(
        body, name="shift_bwd", grid=(R // TT,),
        in_specs=[cur, prev, nxt, cur, prev, nxt, _full(taps.shape)], out_specs=(cur, _full((8, width))),
        out_shape=(jax.ShapeDtypeStruct((R, width), F32), jax.ShapeDtypeStruct((8, width), F32)),
        compiler_params=_cp(("arbitrary",)),
    )(x, x, x, d, d, d, taps)


def _lora_in(wa):
    lane = lax.broadcasted_iota(jnp.int32, (1, LORA_W), 1)
    return jnp.where(lane < LORA_W // 2, jnp.tanh(wa), wa)


def _rwkv_prep_call(shifted, wup, aup, w0, a0, k_k, k_a, bd, T):
    R = shifted.shape[0]
    TT = min(ROW_TILE, T)

    def body(k_ref, wa_ref, wup_ref, aup_ref, w0_ref, a0_ref, kk_ref, ka_ref, bd_ref, w_o, kt_o, akk_o, kk_o):
        twa = _lora_in(wa_ref[...])
        pre = [_dot(twa, m_ref[z]) for m_ref in (wup_ref, aup_ref) for z in range(2)]
        outs = _rwkv_pw(k_ref[...], pre[0], pre[1], pre[2], pre[3], w0_ref[...], a0_ref[...], kk_ref[...],
                        ka_ref[...], bd_ref[...], False)
        w_o[0], w_o[1], kt_o[0], kt_o[1], akk_o[0], akk_o[1] = outs[:6]
        kk_o[...] = outs[6]

    col = lambda c, w: pl.BlockSpec((TT, w), lambda i: (i, c))
    two = pl.BlockSpec((2, TT, RWKV_W), lambda i: (0, i, 0))
    sds = jax.ShapeDtypeStruct
    return pl.pallas_call(
        body, name="rwkv_prep", grid=(R // TT,),
        in_specs=[col(1, RWKV_W), col(3 * RWKV_W // LORA_W, LORA_W), _full(wup.shape), _full(aup.shape),
                  _full((2, RWKV_W)), _full((2, RWKV_W)), _full((1, RWKV_W)), _full((1, RWKV_W)), _full((256, 256))],
        out_specs=(two, two, two, col(0, RWKV_W)),
        out_shape=(sds((2, R, RWKV_W), F32),) * 3 + (sds((R, RWKV_W), F32),),
        compiler_params=_cp(("arbitrary",)),
    )(shifted, shifted, wup, aup, w0, a0, k_k, k_a, bd)


def _rwkv_prep_bwd_call(shifted, cts, wup, aup, w0, a0, k_k, k_a, bd, T):
    R = shifted.shape[0]
    TT = min(ROW_TILE, T)

    def body(k_ref, wa_ref, dw0, dkt0, dakk0, dkk0, dr0, dv0, dw1, dkt1, dakk1, dkk1, dr1, dv1, dr2_ref, dv2_ref, dkts_ref,
             wup_ref, aup_ref, w0_ref, a0_ref, kk_ref, ka_ref, bd_ref,
             dsh_ref, gwup_ref, gaup_ref, gw0_ref, ga0_ref, gkk_ref, gka_ref):
        dw_ref, dkt_ref, dakk_ref, dkk_ref, dr_ref, dv_ref = ((dw0, dw1), (dkt0, dkt1), (dakk0, dakk1), (dkk0, dkk1),
                                                              (dr0, dr1), (dv0, dv1))
        i = pl.program_id(0)
        wa = wa_ref[...]
        twa = _lora_in(wa)
        pre = [_dot(twa, m_ref[z]) for m_ref in (wup_ref, aup_ref) for z in range(2)]
        fn = functools.partial(_rwkv_pw, bd=bd_ref[...], diff=True)
        _, vjp = jax.vjp(fn, k_ref[...], pre[0], pre[1], pre[2], pre[3], w0_ref[...], a0_ref[...], kk_ref[...],
                         ka_ref[...])
        dkts = dkts_ref[...]
        dk, dpw0, dpw1, dpa0, dpa1, gw0, ga0, gkk, gka = vjp(
            (dw_ref[0][...], dw_ref[1][...], dkt_ref[0][...] + dkts, dkt_ref[1][...] + dkts, dakk_ref[0][...],
             dakk_ref[1][...], dkk_ref[0][...] + dkk_ref[1][...]))
        dtwa = (_dot_nt(dpw0, wup_ref[0]) + _dot_nt(dpw1, wup_ref[1]) + _dot_nt(dpa0, aup_ref[0])
                + _dot_nt(dpa1, aup_ref[1]))
        lane = lax.broadcasted_iota(jnp.int32, (1, LORA_W), 1)
        dsh_ref[:, 0:RWKV_W] = dr_ref[0][...] + dr_ref[1][...] + dr2_ref[...]
        dsh_ref[:, RWKV_W:2 * RWKV_W] = dk
        dsh_ref[:, 2 * RWKV_W:3 * RWKV_W] = dv_ref[0][...] + dv_ref[1][...] + dv2_ref[...]
        dsh_ref[:, 3 * RWKV_W:] = jnp.where(lane < LORA_W // 2, dtwa * (1.0 - twa * twa), dtwa)
        acc = ((gwup_ref.at[0], _dot_tn(twa, dpw0)), (gwup_ref.at[1], _dot_tn(twa, dpw1)),
               (gaup_ref.at[0], _dot_tn(twa, dpa0)), (gaup_ref.at[1], _dot_tn(twa, dpa1)),
               (gw0_ref, gw0), (ga0_ref, ga0), (gkk_ref, gkk), (gka_ref, gka))

        @pl.when(i == 0)
        def _():
            for ref, val in acc:
                ref[...] = val

        @pl.when(i > 0)
        def _():
            for ref, val in acc:
                ref[...] += val

    col = lambda c, w: pl.BlockSpec((TT, w), lambda i: (i, c))
    one = col(0, RWKV_W)
    sds = jax.ShapeDtypeStruct
    return pl.pallas_call(
        body, name="rwkv_prep_bwd", grid=(R // TT,),
        in_specs=[col(1, RWKV_W), col(3 * RWKV_W // LORA_W, LORA_W)] + [one] * 15 + [
                  _full(wup.shape), _full(aup.shape), _full((2, RWKV_W)), _full((2, RWKV_W)), _full((1, RWKV_W)),
                  _full((1, RWKV_W)), _full((256, 256))],
        out_specs=(pl.BlockSpec((TT, SHIFT_W), lambda i: (i, 0)), _full(wup.shape), _full(aup.shape),
                   _full((2, RWKV_W)), _full((2, RWKV_W)), _full((1, RWKV_W)), _full((1, RWKV_W))),
        out_shape=(sds((R, SHIFT_W), F32), sds(wup.shape, F32), sds(aup.shape, F32), sds((2, RWKV_W), F32),
                   sds((2, RWKV_W), F32), sds((1, RWKV_W), F32), sds((1, RWKV_W), F32)),
        compiler_params=_cp(("arbitrary",)),
    )(shifted, shifted, *cts, wup, aup, w0, a0, k_k, k_a, bd)


def _col_lhs(row, eye_b):
    return eye_b * row.astype(MXU_DTYPE)


def _colsum(x):
    return jnp.sum(x, axis=0, keepdims=True)


def _stacked_segsum(tiles, bd):
    res = _seg_dot(jnp.concatenate(tiles, axis=0), bd)
    return [res[j * HEAD_DIM:(j + 1) * HEAD_DIM] for j in range(len(tiles))]


def _scan_specs(B, T, C, nC):
    def blk(z, col, rev):
        idx = (lambda g: (z, 0, nC - 1 - g, col)) if rev else (lambda g: (z, 0, g, col))
        return pl.BlockSpec((None, B, C, RWKV_W), idx)

    def blk3(col, rev):
        idx = (lambda g: (0, nC - 1 - g, col)) if rev else (lambda g: (0, g, col))
        return pl.BlockSpec((B, C, RWKV_W), idx)

    return blk, blk3


def _scan_fwd_call(w, kt, akk, kk, shifted, eye_b, eye_f, bd, B, T):
    C = min(SCAN_CHUNK, T)
    nC = T // C
    blk, blk3 = _scan_specs(B, T, C, nC)

    def body(w0, kt0, akk0, kk0, v0, r0, w1, kt1, akk1, kk1, v1, r1, eb_ref, ef_ref, bd_ref, y0, y1, st, S):
        @pl.when(pl.program_id(0) == 0)
        def _():
            S[...] = jnp.zeros_like(S)

        st[0] = S[...].astype(MXU_DTYPE)
        dirs = ((w0, kt0, akk0, kk0, v0, r0, y0), (w1, kt1, akk1, kk1, v1, r1, y1))

        def step(s, carry):
            for z in range(2):
                row = s if z == 0 else C - 1 - s
                prev = jnp.maximum(s - 1, 0) if z == 0 else jnp.minimum(C - s, C - 1)
                wr, ktr, akkr, kkr, vr, rr, yr = dirs[z]
                tiles = []
                for b in range(B):
                    Sb = st[s, z * B + b]
                    tiles += [Sb * kkr[b, pl.ds(row, 1), :].astype(MXU_DTYPE),
                              _col_lhs(vr[b, pl.ds(row, 1), :], eb_ref[...]),
                              Sb * rr[b, pl.ds(prev, 1), :].astype(MXU_DTYPE)]
                res = _stacked_segsum(tiles, bd_ref[...])
                for b in range(B):
                    c = z * B + b
                    sab, vb, yb = res[3 * b:3 * b + 3]
                    ld = lambda ref: ref[b, pl.ds(row, 1), :]
                    Sn = S[c] * ld(wr) - sab * ld(akkr) + vb * ld(ktr)
                    S[c] = Sn
                    st[s + 1, c] = Sn.astype(MXU_DTYPE)
                    yr[b, pl.ds(prev, 1), :] = _colsum(ef_ref[...] * yb)
            return carry

        lax.fori_loop(0, C, step, 0, unroll=SCAN_UNROLL)
        for z in range(2):
            last = C - 1 if z == 0 else 0
            rr, yr = dirs[z][5], dirs[z][6]
            res = _stacked_segsum([st[C, z * B + b] * rr[b, last:last + 1, :].astype(MXU_DTYPE) for b in range(B)],
                                  bd_ref[...])
            for b in range(B):
                yr[b, last:last + 1, :] = _colsum(ef_ref[...] * res[b])

    ins, specs = [], []
    for z, rev in ((0, False), (1, True)):
        ins += [w, kt, akk, kk, shifted, shifted]
        specs += [blk(z, 0, rev), blk(z, 0, rev), blk(z, 0, rev), blk3(0, rev), blk3(2, rev), blk3(0, rev)]
    sds = jax.ShapeDtypeStruct
    return pl.pallas_call(
        body, name="scan_fwd", grid=(nC,),
        in_specs=specs + [_full((HEAD_DIM, RWKV_W)), _full((HEAD_DIM, RWKV_W)), _full((256, 256))],
        out_specs=(blk3(0, False), blk3(0, True),
                   pl.BlockSpec((None, C + 1, 2 * B, HEAD_DIM, RWKV_W), lambda g: (g, 0, 0, 0, 0))),
        out_shape=(sds((B, T, RWKV_W), F32), sds((B, T, RWKV_W), F32),
                   sds((nC, C + 1, 2 * B, HEAD_DIM, RWKV_W), MXU_DTYPE)),
        scratch_shapes=[pltpu.VMEM((2 * B, HEAD_DIM, RWKV_W), F32)],
        compiler_params=_cp(("arbitrary",)),
    )(*ins, eye_b, eye_f, bd)


def _scan_bwd_call(w, kt, akk, kk, shifted, dys, st, eye_b, eye_f, bd, B, T):
    C = min(SCAN_CHUNK, T)
    nC = T // C
    blk, blk3 = _scan_specs(B, T, C, nC)
    nin = 7

    def body(*refs):
        d0, d1 = refs[:nin], refs[nin:2 * nin]
        st_ref, eb_ref, ef_ref, bd_ref = refs[2 * nin:2 * nin + 4]
        o0, o1 = refs[2 * nin + 4:2 * nin + 10], refs[2 * nin + 10:2 * nin + 16]
        COL, G = refs[2 * nin + 16:]

        @pl.when(pl.program_id(0) == 0)
        def _():
            G[...] = jnp.zeros_like(G)

        dirs = (d0 + (o0,), d1 + (o1,))

        def column_operands(s, z):
            row = s if z == 0 else C - 1 - s
            _, _, _, kkr, vr, _, dyr, _ = dirs[z]
            tiles = []
            for b in range(B):
                tiles += [st_ref[s, z * B + b] * kkr[b, pl.ds(row, 1), :].astype(MXU_DTYPE),
                          _col_lhs(vr[b, pl.ds(row, 1), :], eb_ref[...]),
                          _col_lhs(dyr[b, pl.ds(row, 1), :], eb_ref[...])]
            return tiles

        def keep_columns(res, z):
            for b in range(B):
                for k in range(3):
                    COL[k, z * B + b] = res[3 * b + k]

        for z in range(2):
            keep_columns(_stacked_segsum(column_operands(C - 1, z), bd_ref[...]), z)

        def bwd(it, carry):
            s = C - 1 - it
            for z in range(2):
                row = s if z == 0 else C - 1 - s
                wr, ktr, akkr, kkr, vr, rr, dyr, (dw_o, dkt_o, dakk_o, dkk_o, dr_o, dv_o) = dirs[z]
                tiles, Gcs = [], []
                for b in range(B):
                    c = z * B + b
                    Gc = G[c] + COL[2, c] * rr[b, pl.ds(row, 1), :]
                    Gcs.append(Gc)
                    Gb = Gc.astype(MXU_DTYPE)
                    tiles += [Gb * akkr[b, pl.ds(row, 1), :].astype(MXU_DTYPE),
                              Gb * ktr[b, pl.ds(row, 1), :].astype(MXU_DTYPE)]
                res = _stacked_segsum(tiles + column_operands(jnp.maximum(s - 1, 0), z), bd_ref[...])
                for b in range(B):
                    c = z * B + b
                    Gc = Gcs[b]
                    gab, dvb = res[2 * b], res[2 * b + 1]
                    ld = lambda ref: ref[b, pl.ds(row, 1), :]
                    G[c] = Gc * ld(wr) - gab * ld(kkr)
                    S = st_ref[s, c].astype(F32)
                    for ref, val in ((dr_o, _colsum(st_ref[s + 1, c].astype(F32) * COL[2, c])),
                                     (dkt_o, _colsum(Gc * COL[1, c])), (dv_o, _colsum(ef_ref[...] * dvb)),
                                     (dw_o, _colsum(Gc * S)), (dakk_o, -_colsum(Gc * COL[0, c])),
                                     (dkk_o, -_colsum(gab * S))):
                        ref[b, pl.ds(row, 1), :] = val
                keep_columns(res[2 * B:], z)
            return carry

        lax.fori_loop(0, C, bwd, 0, unroll=SCAN_UNROLL)

    ins, specs = [], []
    for z, rev in ((0, True), (1, False)):
        ins += [w, kt, akk, kk, shifted, shifted, dys]
        specs += [blk(z, 0, rev), blk(z, 0, rev), blk(z, 0, rev), blk3(0, rev), blk3(2, rev), blk3(0, rev), blk3(0, rev)]
    ins += [st, eye_b, eye_f, bd]
    specs += [pl.BlockSpec((None, C + 1, 2 * B, HEAD_DIM, RWKV_W), lambda g: (nC - 1 - g, 0, 0, 0, 0)),
              _full((HEAD_DIM, RWKV_W)), _full((HEAD_DIM, RWKV_W)), _full((256, 256))]
    sds = jax.ShapeDtypeStruct
    out_specs = tuple(blk3(0, True) for _ in range(6)) + tuple(blk3(0, False) for _ in range(6))
    res = pl.pallas_call(
        body, name="scan_bwd", grid=(nC,), in_specs=specs, out_specs=out_specs,
        out_shape=tuple(sds((B, T, RWKV_W), F32) for _ in range(12)),
        scratch_shapes=[pltpu.VMEM((3, 2 * B, HEAD_DIM, RWKV_W), F32), pltpu.VMEM((2 * B, HEAD_DIM, RWKV_W), F32)],
        compiler_params=_cp(("arbitrary",)),
    )(*ins)
    return list(res)


def _out_head_call(x2, tgt2, gate, y_att, g_att, y0, y1, shifted, kt, g_rw, w_out, g_post, gn_w, gn_b, r_k, bd, T):
    R = x2.shape[0]
    TT = min(ROW_TILE, T)
    tpe = T // TT

    def body(x_ref, t_ref, gate_ref, ya_ref, ga_ref, y0_ref, y1_ref, r_ref, v_ref, kt_ref, grw_ref, w_ref, gp_ref,
             gnw_ref, gnb_ref, rk_ref, bd_ref,
             loss_o, dy_o, dya_o, dga_o, dys_o, dr_o, dv_o, dkts_o, dgrw_o, dgate_o, gw_o, ggp_o, ggnw_o, ggnb_o, grk_o):
        i = pl.program_id(0)
        bd = bd_ref[...]
        mix = functools.partial(_mix_fn, bd=bd, diff=True)
        (ma, mr), mix_vjp = jax.vjp(mix, ya_ref[...], ga_ref[...], y0_ref[...] + y1_ref[...], r_ref[...], v_ref[...],
                                    kt_ref[0] + kt_ref[1], grw_ref[...], gnw_ref[...], gnb_ref[...], rk_ref[...])
        out = _dot(ma, w_ref[0:ATT_W, :]) + _dot(mr, w_ref[ATT_W:, :])
        loss, loss_vjp = jax.vjp(_loss_fn, out, x_ref[...], t_ref[...], gate_ref[0], gp_ref[...])
        d_out, dy, _, dgate, dgp = loss_vjp(jnp.ones((1, 1), F32))
        dy_o[...] = dy
        dma = _dot_nt(d_out, w_ref[0:ATT_W, :])
        dmr = _dot_nt(d_out, w_ref[ATT_W:, :])
        dya_o[...], dga_o[...], dys_o[...], dr_o[...], dv_o[...], dkts_o[...], dgrw_o[...], dgnw, dgnb, drk = \
            mix_vjp((dma, dmr))
        gw = jnp.concatenate([_dot_tn(ma, d_out), _dot_tn(mr, d_out)], axis=0)
        acc = ((loss_o, jnp.broadcast_to(loss, (8, 128))), (gw_o, gw), (ggp_o, dgp), (ggnw_o, dgnw), (ggnb_o, dgnb),
               (grk_o, drk))

        @pl.when(i == 0)
        def _():
            for ref, val in acc:
                ref[...] = val

        @pl.when(i > 0)
        def _():
            for ref, val in acc:
                ref[...] += val

        @pl.when(i % tpe == 0)
        def _():
            dgate_o[0] = dgate

        @pl.when(i % tpe > 0)
        def _():
            dgate_o[0] += dgate

    row = lambda w, c=0: pl.BlockSpec((TT, w), lambda i: (i, c))
    two = pl.BlockSpec((2, TT, RWKV_W), lambda i: (0, i, 0))
    per_ex = pl.BlockSpec((1, 1, D_MODEL), lambda i: (i // tpe, 0, 0))
    sds = jax.ShapeDtypeStruct
    r512 = sds((R, RWKV_W), F32)
    return pl.pallas_call(
        body, name="out_head", grid=(R // TT,),
        in_specs=[row(D_MODEL), row(D_MODEL), per_ex, row(ATT_W), row(ATT_W), row(RWKV_W), row(RWKV_W), row(RWKV_W, 0),
                  row(RWKV_W, 2), two,
                  row(RWKV_W), _full(w_out.shape), _full((1, D_MODEL)), _full((1, RWKV_W)), _full((1, RWKV_W)),
                  _full((1, RWKV_W)), _full((256, 256))],
        out_specs=(_full((8, 128)), row(D_MODEL), row(ATT_W), row(ATT_W), row(RWKV_W), row(RWKV_W), row(RWKV_W),
                   row(RWKV_W), row(RWKV_W), per_ex, _full((D_MODEL, D_MODEL)), _full((1, D_MODEL)), _full((1, RWKV_W)),
                   _full((1, RWKV_W)), _full((1, RWKV_W))),
        out_shape=(sds((8, 128), F32), sds((R, D_MODEL), F32), r512, r512, r512, r512, r512, r512, r512,
                   sds((R // T, 1, D_MODEL), F32), sds((D_MODEL, D_MODEL), F32), sds((1, D_MODEL), F32),
                   sds((1, RWKV_W), F32), sds((1, RWKV_W), F32), sds((1, RWKV_W), F32)),
        compiler_params=_cp(("arbitrary",)),
    )(x2, tgt2, gate, y_att, g_att, y0, y1, shifted, shifted, kt, g_rw, w_out, g_post, gn_w, gn_b, r_k, bd)


def _in_proj_bwd_call(x2, dy, shift, scale, g_pre, w_in, qg, kg, cos, sin, bd, q_raw, k_raw, dqr, dkp, dvp,
                      d_gatt, d_rin, d_grw, T):
    R = x2.shape[0]
    TT = min(ROW_TILE, T)
    tpe = T // TT

    def body(x_ref, dy_ref, sh_ref, sc_ref, gp_ref, w_ref, qg_ref, kg_ref, cos_ref, sin_ref, bd_ref, q_ref, k_ref,
             dqr_ref, dkp_ref, dvp_ref, dga_ref, drin_ref, dgrw_ref,
             dx_o, dproj_o, dsh_o, dsc_o, ggp_o, gqg_o, gkg_o):
        i = pl.program_id(0)
        cos, sin, bd = cos_ref[...], sin_ref[...], bd_ref[...]
        left = lax.broadcasted_iota(jnp.int32, (1, KV_W), 1) < HEAD_DIM

        def kv_grad(ref):
            a = ref[0] + ref[1]
            b = ref[2] + ref[3]
            return jnp.where(left, a + pltpu.roll(a, HEAD_DIM, 1), b + pltpu.roll(b, HEAD_DIM, 1))

        qfn = functools.partial(_qk_fn, cos=jnp.tile(cos, (1, 4)), sin=jnp.tile(sin, (1, 4)), bd=bd, scale=ATT_SCALE,
                                diff=True)
        _, q_vjp = jax.vjp(qfn, q_ref[...], qg_ref[...])
        dq, gqg = q_vjp(dqr_ref[...])
        kfn = functools.partial(_qk_fn, cos=cos, sin=sin, bd=bd, scale=1.0, diff=True)
        _, k_vjp = jax.vjp(kfn, k_ref[...], kg_ref[...])
        dk, gkg = k_vjp(kv_grad(dkp_ref))
        pieces = ((C_Q, C_K, dq), (C_K, C_V, dk), (C_V, C_GA, kv_grad(dvp_ref)), (C_GA, C_RIN, dga_ref[...]),
                  (C_RIN, C_GRW, drin_ref[...]), (C_GRW, C_END, dgrw_ref[...]))
        dh = jnp.zeros((TT, D_MODEL), F32)
        for c0, c1, val in pieces:
            vb = val.astype(MXU_DTYPE)
            dproj_o[:, c0:c1] = vb
            dh = dh + _dot_nt(vb, w_ref[:, c0:c1])
        _, pre_vjp = jax.vjp(_pre_fn, x_ref[...], sh_ref[0], sc_ref[0], gp_ref[...])
        dx, dsh, dsc, ggp = pre_vjp(dh)
        dx_o[...] = dx + dy_ref[...]
        acc = ((ggp_o, ggp), (gqg_o, gqg), (gkg_o, gkg))

        @pl.when(i == 0)
        def _():
            for ref, val in acc:
                ref[...] = val

        @pl.when(i > 0)
        def _():
            for ref, val in acc:
                ref[...] += val

        @pl.when(i % tpe == 0)
        def _():
            dsh_o[0] = dsh
            dsc_o[0] = dsc

        @pl.when(i % tpe > 0)
        def _():
            dsh_o[0] += dsh
            dsc_o[0] += dsc

    row = lambda w: pl.BlockSpec((TT, w), lambda i: (i, 0))
    per_ex = pl.BlockSpec((1, 1, D_MODEL), lambda i: (i // tpe, 0, 0))
    tab = pl.BlockSpec((TT, KV_W), lambda i: (i % tpe, 0))
    pad = pl.BlockSpec((4, TT, KV_W), lambda i: (0, i, 0))
    sds = jax.ShapeDtypeStruct
    nb = R // T
    return pl.pallas_call(
        body, name="in_proj_bwd", grid=(R // TT,),
        in_specs=[row(D_MODEL), row(D_MODEL), per_ex, per_ex, _full((1, D_MODEL)), _full(w_in.shape), _full((1, ATT_W)),
                  _full((1, KV_W)), tab, tab, _full((256, 256)), row(ATT_W), row(KV_W), row(ATT_W), pad, pad,
                  row(ATT_W), row(SHIFT_W), row(RWKV_W)],
        out_specs=(row(D_MODEL), row(C_END), per_ex, per_ex, _full((1, D_MODEL)), _full((1, ATT_W)), _full((1, KV_W))),
        out_shape=(sds((R, D_MODEL), F32), sds((R, C_END), MXU_DTYPE), sds((nb, 1, D_MODEL), F32),
                   sds((nb, 1, D_MODEL), F32), sds((1, D_MODEL), F32), sds((1, ATT_W), F32), sds((1, KV_W), F32)),
        compiler_params=_cp(("arbitrary",)),
    )(x2, dy, shift, scale, g_pre, w_in, qg, kg, cos, sin, bd, q_raw, k_raw, dqr, dkp, dvp, d_gatt, d_rin, d_grw)


def _w_in_grad_call(hb, dproj, T):
    R = hb.shape[0]
    TT = min(ROW_TILE, T)
    CB = 1152

    def body(h_ref, d_ref, o_ref):
        g = _dot_tn(h_ref[...], d_ref[...])

        @pl.when(pl.program_id(1) == 0)
        def _():
            o_ref[...] = g

        @pl.when(pl.program_id(1) > 0)
        def _():
            o_ref[...] += g

    return pl.pallas_call(
        body, name="w_in_grad", grid=(C_END // CB, R // TT),
        in_specs=[pl.BlockSpec((TT, D_MODEL), lambda j, i: (i, 0)), pl.BlockSpec((TT, CB), lambda j, i: (i, j))],
        out_specs=pl.BlockSpec((D_MODEL, CB), lambda j, i: (0, j)),
        out_shape=jax.ShapeDtypeStruct((D_MODEL, C_END), F32), compiler_params=_cp(("arbitrary", "arbitrary")),
    )(hb, dproj)


def _adam_call(parts, w, m, v, name, row_tile=None):
    P, M, N = parts.shape
    TM = M if row_tile is None else row_tile

    def body(p_ref, w_ref, m_ref, v_ref, g_o, d_o, m_o, v_o):
        g = p_ref[0]
        for j in range(1, P):
            g = g + p_ref[j]
        m2 = ADAM_B1 * m_ref[...] + (1.0 - ADAM_B1) * g
        v2 = ADAM_B2 * v_ref[...] + (1.0 - ADAM_B2) * jnp.square(g)
        m_hat = m2 / (1.0 - ADAM_B1 ** ADAM_STEP)
        v_hat = v2 / (1.0 - ADAM_B2 ** ADAM_STEP)
        g_o[...] = g
        d_o[...] = -ADAM_LR * (m_hat / (jnp.sqrt(v_hat) + ADAM_EPS) + ADAM_WD * w_ref[...])
        m_o[...] = m2
        v_o[...] = v2

    blk = pl.BlockSpec((TM, N), lambda i: (i, 0))
    return pl.pallas_call(
        body, name=name, grid=(M // TM,),
        in_specs=[pl.BlockSpec((P, TM, N), lambda i: (0, i, 0)), blk, blk, blk], out_specs=(blk,) * 4,
        out_shape=(jax.ShapeDtypeStruct((M, N), F32),) * 4, compiler_params=_cp(("arbitrary",)),
    )(parts, w, m, v)


_SMALL_ROWS = 136


def _pack_small(taps, w_up, w0, a_up, a0):
    flat = jnp.concatenate([taps.reshape(-1), w_up.reshape(-1), w0.reshape(-1), a_up.reshape(-1), a0.reshape(-1)])
    return jnp.pad(flat, (0, _SMALL_ROWS * 128 - flat.shape[0])).reshape(_SMALL_ROWS, 128)


def _unpack_small(packed):
    n = packed.shape[0]
    flat = packed.reshape(n, -1)
    out, o = [], 0
    for shape in ((3, 208), (2, 64, 64), (2, 64), (2, 64, 64), (2, 64)):
        size = 1
        for s in shape:
            size *= s
        out.append(flat[:, o:o + size].reshape((n,) + shape))
        o += size
    return out


def _cols_to_full(blocks):
    nd = blocks.ndim
    moved = jnp.moveaxis(blocks, 0, nd - 2)
    return moved.reshape(moved.shape[:-2] + (moved.shape[-2] * moved.shape[-1],))


def _full_to_cols(full):
    k = full.shape[-1] // NDEV
    return jnp.moveaxis(full.reshape(full.shape[:-1] + (NDEV, k)), -2, 0)


_REP_SIZES = (("g_pre", 1024), ("q_norm_g", 64), ("k_norm_g", 64), ("k_k", 512), ("k_a", 512), ("r_k", 512),
              ("gn_w", 512), ("gn_b", 512), ("g_post", 1024))
_REP_ROWS = 40


def kernel(x, c, w_ada, b_ada, g_pre, w_in, q_norm_g, k_norm_g, shift_taps, w_up, w0, a_up, a0, k_k, k_a, r_k, gn_w, gn_b, w_out, g_post, loss_target, m_w_ada, m_b_ada, m_g_pre, m_w_in, m_q_norm_g, m_k_norm_g, m_shift_taps, m_w_up, m_w0, m_a_up, m_a0, m_k_k, m_k_a, m_r_k, m_gn_w, m_gn_b, m_w_out, m_g_post, v_w_ada, v_b_ada, v_g_pre, v_w_in, v_q_norm_g, v_k_norm_g, v_shift_taps, v_w_up, v_w0, v_a_up, v_a0, v_k_k, v_k_a, v_r_k, v_gn_w, v_gn_b, v_w_out, v_g_post):
    B, T, _ = x.shape
    R = B * T
    me = 4 * lax.axis_index("x") + 2 * lax.axis_index("y") + lax.axis_index("c")
    x2 = x.reshape(R, D_MODEL)
    tgt2 = loss_target.reshape(R, D_MODEL)

    seg = jnp.arange(256) // HEAD_DIM
    bd = (seg[:, None] == seg[None, :]).astype(MXU_DTYPE)
    eye = (jnp.arange(HEAD_DIM)[:, None] == (jnp.arange(RWKV_W) % HEAD_DIM)[None, :])
    eye_b, eye_f = eye.astype(MXU_DTYPE), eye.astype(F32)
    cos, sin = _rope_tables(T)

    c_g, w_in_g, w_out_g, small_g = _exchange(
        [c, w_in[0].astype(MXU_DTYPE), w_out[0].astype(MXU_DTYPE),
         _pack_small(shift_taps[0], w_up[0], w0[0], a_up[0], a0[0])], [False] * 4, "gather_params")
    c_all = c_g.reshape(NDEV * B, D_MODEL)
    w_in_f = _cols_to_full(w_in_g)
    w_out_f = w_out_g.reshape(D_MODEL, D_MODEL)
    taps_b, w_up_b, w0_b, a_up_b, a0_b = _unpack_small(small_g)
    taps_f = jnp.pad(_cols_to_full(taps_b), ((0, 5), (0, 0)))
    w_up_f, a_up_f = _cols_to_full(w_up_b), _cols_to_full(a_up_b)
    w0_f, a0_f = _cols_to_full(w0_b), _cols_to_full(a0_b)
    wup_pad = jnp.pad(w_up_f, ((0, 0), (0, 64), (0, 0))).astype(MXU_DTYPE)
    aup_pad = jnp.pad(a_up_f, ((0, 0), (64, 0), (0, 0))).astype(MXU_DTYPE)

    ncol = w_ada.shape[2]
    b_cols = lax.dynamic_slice(b_ada, (0, me * ncol), (1, ncol))
    mod_cols = _mod_call(c_all, w_ada[0].astype(MXU_DTYPE), b_cols)
    (mod_g,) = _exchange([mod_cols], [False], "gather_mod")
    mod = lax.dynamic_slice(_cols_to_full(mod_g), (me * B, 0), (B, 3 * D_MODEL))
    shift, scale, gate = [mod[:, j * D_MODEL:(j + 1) * D_MODEL].reshape(B, 1, D_MODEL) for j in range(3)]

    qg = jnp.tile(q_norm_g, (1, ATT_W // HEAD_DIM))
    kg = jnp.tile(k_norm_g, (1, KV_W // HEAD_DIM))
    rk_row = r_k.reshape(1, RWKV_W)

    hb, qr, kpad, vpad, q_raw, k_raw, g_att, rin, g_rw = _in_proj_call(
        x2, shift, scale, g_pre, w_in_f, qg, kg, cos, sin, bd, T)
    y_att = _att_fwd_call(qr, kpad, vpad, B, T)
    shifted = _shift_fwd_call(rin, taps_f, T)
    w_s, kt_s, akk_s, kk_s = _rwkv_prep_call(shifted, wup_pad, aup_pad, w0_f, a0_f, k_k, k_a, bd, T)
    sh3 = shifted.reshape(B, T, SHIFT_W)
    r4 = lambda a: a.reshape(2, B, T, RWKV_W)
    y0, y1, st = _scan_fwd_call(r4(w_s), r4(kt_s), r4(akk_s), kk_s.reshape(B, T, RWKV_W), sh3, eye_b, eye_f, bd, B, T)

    (loss_blk, dy, d_yatt, d_gatt, d_ys, d_r2, d_v2, d_kts, d_grw, d_gate, g_wout, g_gpost, g_gnw, g_gnb,
     g_rk) = _out_head_call(x2, tgt2, gate, y_att, g_att, y0.reshape(R, RWKV_W), y1.reshape(R, RWKV_W), shifted, kt_s,
                            g_rw, w_out_f, g_post, gn_w, gn_b, rk_row, bd, T)
    scan_cts = _scan_bwd_call(r4(w_s), r4(kt_s), r4(akk_s), kk_s.reshape(B, T, RWKV_W), sh3,
                              d_ys.reshape(B, T, RWKV_W), st, eye_b, eye_f, bd, B, T)
    scan_cts = [a.reshape(R, RWKV_W) for a in scan_cts]
    d_shifted, g_wup, g_aup, g_w0, g_a0, g_kk, g_ka = _rwkv_prep_bwd_call(
        shifted, scan_cts + [d_r2, d_v2, d_kts], wup_pad, aup_pad, w0_f, a0_f, k_k, k_a, bd, T)
    d_rin, g_taps = _shift_bwd_call(rin, d_shifted, taps_f, T)
    dqr, dkp, dvp = _att_bwd_call(qr, kpad, vpad, d_yatt, B, T)
    grad_x, dproj, d_shift, d_scale, g_gpre, g_qg, g_kg = _in_proj_bwd_call(
        x2, dy, shift, scale, g_pre, w_in_f, qg, kg, cos, sin, bd, q_raw, k_raw, dqr, dkp, dvp, d_gatt, d_rin, d_grw, T)
    g_win = _w_in_grad_call(hb, dproj, T)

    rep = jnp.concatenate([g_gpre.reshape(-1), g_qg.reshape(-1, HEAD_DIM).sum(0), g_kg.reshape(-1, HEAD_DIM).sum(0),
                           g_kk.reshape(-1), g_ka.reshape(-1), g_rk.reshape(-1), g_gnw.reshape(-1), g_gnb.reshape(-1),
                           g_gpost.reshape(-1)])
    rep = jnp.pad(rep, (0, _REP_ROWS * 128 - rep.shape[0])).reshape(_REP_ROWS, 128)
    dmod = jnp.concatenate([d_shift, d_scale, d_gate], axis=2).reshape(B, 3 * D_MODEL)
    small_parts = jax.vmap(_pack_small)(_full_to_cols(g_taps[:3]), _full_to_cols(g_wup[:, :64, :]), _full_to_cols(g_w0),
                                        _full_to_cols(g_aup[:, 64:, :]), _full_to_cols(g_a0))
    p_win, p_wout, p_small, dmod_g, rep_g = _exchange(
        [_full_to_cols(g_win), g_wout.reshape(NDEV, D_MODEL // NDEV, D_MODEL), small_parts, dmod, rep],
        [True, True, True, False, False], "reduce_grads")
    dmod_all = dmod_g.reshape(NDEV * B, 3 * D_MODEL)
    g_wada = _wada_grad_call(c_all, lax.dynamic_slice(dmod_all, (0, me * ncol), (NDEV * B, ncol)))

    res = {}

    def adam(name, parts, w, m, v, row_tile=None):
        shape = w.shape
        two_d = (-1, shape[-1])
        out = _adam_call(parts.reshape((parts.shape[0],) + w.reshape(two_d).shape), w.reshape(two_d), m.reshape(two_d),
                         v.reshape(two_d), "adam_" + name, row_tile)
        res[name] = [o.reshape(shape) for o in out]

    adam("w_ada", g_wada[None], w_ada, m_w_ada, v_w_ada)
    adam("b_ada", dmod_all.reshape(NDEV * B, 1, 3 * D_MODEL), b_ada, m_b_ada, v_b_ada)
    adam("w_in", p_win, w_in, m_w_in, v_w_in, 128)
    adam("w_out", p_wout, w_out, m_w_out, v_w_out)
    taps_p, wup_p, w0_p, aup_p, a0_p = _unpack_small(p_small)
    adam("shift_taps", taps_p, shift_taps, m_shift_taps, v_shift_taps)
    adam("w_up", wup_p, w_up, m_w_up, v_w_up)
    adam("w0", w0_p, w0, m_w0, v_w0)
    adam("a_up", aup_p, a_up, m_a_up, v_a_up)
    adam("a0", a0_p, a0, m_a0, v_a0)
    rep_flat = rep_g.reshape(NDEV, -1)
    off = 0
    given = dict(g_pre=(g_pre, m_g_pre, v_g_pre), q_norm_g=(q_norm_g, m_q_norm_g, v_q_norm_g),
                 k_norm_g=(k_norm_g, m_k_norm_g, v_k_norm_g), k_k=(k_k, m_k_k, v_k_k), k_a=(k_a, m_k_a, v_k_a),
                 r_k=(r_k, m_r_k, v_r_k), gn_w=(gn_w, m_gn_w, v_gn_w), gn_b=(gn_b, m_gn_b, v_gn_b),
                 g_post=(g_post, m_g_post, v_g_post))
    for name, size in _REP_SIZES:
        adam(name, rep_flat[:, off:off + size], *given[name])
        off += size

    loss = lax.psum(loss_blk[0, 0], ("x", "y", "c"))
    order = ["w_ada", "b_ada", "g_pre", "w_in", "q_norm_g", "k_norm_g", "shift_taps", "w_up", "w0", "a_up", "a0", "k_k",
             "k_a", "r_k", "gn_w", "gn_b", "w_out", "g_post"]
    return (loss, grad_x.reshape(B, T, D_MODEL), *[res[n][0] for n in order], *[res[n][1] for n in order],
            *[res[n][2] for n in order], *[res[n][3] for n in order])
```

```python
import functools

import jax
import jax.numpy as jnp
from jax import lax
from jax.experimental import pallas as pl
from jax.experimental.pallas import tpu as pltpu

F32 = jnp.float32
MXU_DTYPE = jnp.bfloat16
MESH = pl.DeviceIdType.MESH
NDEV = 8

D_MODEL = 1024
HEAD_DIM = 64
ATT_W = 512
KV_W = 128
RWKV_W = 512
LORA_W = 128
SHIFT_W = 3 * RWKV_W + LORA_W
GRID_W = 64
ROPE_THETA = 10000.0
DECAY_SCALE = 0.6065306597126334
NORM_EPS = 1e-6
GN_EPS = 64e-5
L2_EPS = 1e-12
ATT_SCALE = HEAD_DIM ** -0.5
C_Q, C_K, C_V, C_GA, C_RIN, C_GRW, C_END = 0, 512, 640, 768, 1280, 2944, 3456

ADAM_LR, ADAM_B1, ADAM_B2, ADAM_EPS, ADAM_WD, ADAM_STEP = 0.001, 0.9, 0.999, 1e-08, 0.01, 10

ROW_TILE = 256
SCAN_CHUNK = 32
SCAN_UNROLL = 4
VMEM_LIMIT = 56 * 1024 * 1024


def _cp(sem=None):
    return pltpu.CompilerParams(dimension_semantics=sem, vmem_limit_bytes=VMEM_LIMIT)


def _dot(a, b, dims=(((1,), (0,)), ((), ()))):
    return lax.dot_general(a.astype(MXU_DTYPE), b.astype(MXU_DTYPE), dims, preferred_element_type=F32)


def _dot_nt(a, b):
    return _dot(a, b, (((1,), (1,)), ((), ())))


def _dot_tn(a, b):
    return _dot(a, b, (((0,), (0,)), ((), ())))


def _seg_dot(xb, bd):
    n = xb.shape[1]
    if n <= 256:
        return jnp.dot(xb, bd[:n, :n], preferred_element_type=F32)
    parts = [jnp.dot(xb[:, c:c + 256], bd, preferred_element_type=F32) for c in range(0, n, 256)]
    return jnp.concatenate(parts, axis=1)


def _split3(x):
    hi = x.astype(MXU_DTYPE)
    r1 = x - hi.astype(F32)
    mid = r1.astype(MXU_DTYPE)
    lo = (r1 - mid.astype(F32)).astype(MXU_DTYPE)
    return hi, mid, lo


def _segsum_raw(x, bd):
    hi, mid, lo = _split3(x)
    return _seg_dot(hi, bd) + _seg_dot(mid, bd) + _seg_dot(lo, bd)


@jax.custom_vjp
def _segsum_d(x, bd):
    return _segsum_raw(x, bd)


def _segsum_d_fwd(x, bd):
    return _segsum_raw(x, bd), bd


def _segsum_d_bwd(bd, ct):
    return _segsum_raw(ct, bd), jnp.zeros_like(bd)


_segsum_d.defvjp(_segsum_d_fwd, _segsum_d_bwd)


def _rope_tables(T):
    t = jnp.arange(T, dtype=F32)
    row = jnp.floor(t / GRID_W)
    col = t - row * GRID_W
    n_freq = HEAD_DIM // 4
    inv_freq = ROPE_THETA ** (-jnp.arange(n_freq, dtype=F32) / n_freq)
    d = jnp.arange(HEAD_DIM)
    pos = jnp.where((d < HEAD_DIM // 2)[None, :], row[:, None], col[:, None])
    ang = pos * inv_freq[d % n_freq][None, :]
    sign = jnp.where((d % 32) < 16, -1.0, 1.0).astype(F32)[None, :]
    cos = jnp.cos(ang)
    sin = jnp.sin(ang) * sign
    return jnp.tile(cos, (1, 2)), jnp.tile(sin, (1, 2))


def _rope_raw(x, cos, sin):
    n = x.shape[1]
    lane = lax.broadcasted_iota(jnp.int32, (1, n), 1)
    first = (lane % 32) < 16
    partner = jnp.where(first, pltpu.roll(x, n - 16, 1), pltpu.roll(x, 16, 1))
    return x * cos + partner * sin


@jax.custom_vjp
def _rope_d(x, cos, sin):
    return _rope_raw(x, cos, sin)


def _rope_d_fwd(x, cos, sin):
    return _rope_raw(x, cos, sin), (cos, sin)


def _rope_d_bwd(res, ct):
    cos, sin = res
    return _rope_raw(ct, cos, -sin), jnp.zeros_like(cos), jnp.zeros_like(sin)


_rope_d.defvjp(_rope_d_fwd, _rope_d_bwd)


def _rms(x, g):
    return x * lax.rsqrt(jnp.mean(x * x, axis=-1, keepdims=True) + NORM_EPS) * g


def _pre_fn(x, shift, scale, g_pre):
    return _rms(x, g_pre) * (1.0 + scale) + shift


def _qk_fn(q, g, cos, sin, bd, scale, diff):
    segsum = _segsum_d if diff else _segsum_raw
    rope = _rope_d if diff else _rope_raw
    qn = q * lax.rsqrt(segsum(q * q, bd) * (1.0 / HEAD_DIM) + NORM_EPS) * g
    return rope(qn, cos, sin) * scale


def _silu(x):
    return x * jax.nn.sigmoid(x)


def _rwkv_pw(k, pw0, pw1, pa0, pa1, w0, a0, k_k, k_a, bd, diff):
    segsum = _segsum_d if diff else _segsum_raw
    kk = k * k_k
    kk = kk * lax.rsqrt(segsum(kk * kk, bd) + L2_EPS)
    ws, kts, akks = [], [], []
    for z, (pw, pa) in enumerate(((pw0, pa0), (pw1, pa1))):
        w = jnp.exp(-DECAY_SCALE * jax.nn.sigmoid(w0[z:z + 1, :] + pw))
        a = jax.nn.sigmoid(a0[z:z + 1, :] + pa)
        ws.append(w)
        kts.append(k * (1.0 + (a - 1.0) * k_a))
        akks.append(a * kk)
    return ws[0], ws[1], kts[0], kts[1], akks[0], akks[1], kk


def _mix_fn(y_att, g_att, ys, r, v, kts, g_rw, gn_w, gn_b, r_k, bd, diff):
    segsum = _segsum_d if diff else _segsum_raw
    mu = segsum(ys, bd) * (1.0 / HEAD_DIM)
    d = ys - mu
    var = segsum(d * d, bd) * (1.0 / HEAD_DIM)
    yn = d * lax.rsqrt(var + GN_EPS) * gn_w + gn_b
    bonus = segsum(r * kts * r_k, bd) * v
    return y_att * _silu(g_att), (yn + bonus) * _silu(g_rw)


def _loss_fn(out, x, tgt, gate, g_post):
    e = x + gate * _rms(out, g_post) - tgt
    s = jnp.sum(e * e, axis=1, keepdims=True)
    return jnp.sum(s, axis=0, keepdims=True) * (0.5 / D_MODEL)


def _exchange(arrays, scatter, name):
    n = len(arrays)
    out_shape = tuple(
        jax.ShapeDtypeStruct((NDEV,) + tuple(a.shape[1:] if sc else a.shape), a.dtype)
        for a, sc in zip(arrays, scatter))
    chips = (4, 2, 6)

    def body(*refs):
        ins, outs = refs[:n], refs[n:2 * n]
        send_sems, recv_sems, local_sems = refs[2 * n:]
        ix, iy, ic = lax.axis_index("x"), lax.axis_index("y"), lax.axis_index("c")
        me = 4 * ix + 2 * iy + ic

        def peer(m):
            px = 1 - ix if (m >> 2) & 1 else ix
            py = 1 - iy if (m >> 1) & 1 else iy
            pc = 1 - ic if m & 1 else ic
            return (px, py, pc), 4 * px + 2 * py + pc

        def copy(k, j, src_ref, slot, to):
            return pltpu.make_async_remote_copy(src_ref=src_ref, dst_ref=outs[k].at[slot], send_sem=send_sems.at[k, j],
                                                recv_sem=recv_sems.at[k, j], device_id=to, device_id_type=MESH)

        local = [pltpu.make_async_copy(ins[k].at[me] if scatter[k] else ins[k], outs[k].at[me], local_sems.at[k])
                 for k in range(n)]
        for cp in local:
            cp.start()
        sends, arrivals, forwards = [], [], []
        for k in range(n):
            if scatter[k]:
                for m in range(1, NDEV):
                    to, p = peer(m)
                    sends.append(copy(k, m - 1, ins[k].at[p], me, to))
                    arrivals.append(copy(k, m - 1, ins[k].at[p], p, to))
            else:
                sib, sib_slot = peer(1)
                sends.append(copy(k, 0, ins[k], me, sib))
                for j, m in enumerate(chips):
                    to, p = peer(m)
                    sends.append(copy(k, 1 + j, ins[k], me, to))
                    forwards.append((copy(k, 1 + j, ins[k], p, to), copy(k, 4 + j, outs[k].at[p], p, sib)))
                    arrivals.append(copy(k, 4 + j, ins[k], peer(m ^ 1)[1], sib))
                arrivals.append(copy(k, 0, ins[k], sib_slot, sib))
        for cp in sends:
            cp.start()
        for arrived, onward in forwards:
            arrived.wait_recv()
            onward.start()
        for cp in arrivals:
            cp.wait_recv()
        for cp in sends + [onward for _, onward in forwards]:
            cp.wait_send()
        for cp in local:
            cp.wait()

    any_spec = pl.BlockSpec(memory_space=pl.ANY)
    return pl.pallas_call(
        body, name=name, out_shape=out_shape,
        in_specs=[any_spec] * n, out_specs=tuple([any_spec] * n),
        scratch_shapes=[pltpu.SemaphoreType.DMA((n, NDEV - 1)), pltpu.SemaphoreType.DMA((n, NDEV - 1)),
                        pltpu.SemaphoreType.DMA((n,))],
    )(*arrays)


def _mod_call(c_all, w_ada, b_cols):
    def body(c_ref, w_ref, b_ref, o_ref):
        o_ref[...] = _dot(_silu(c_ref[...]), w_ref[...]) + b_ref[...]

    return pl.pallas_call(body, name="mod_fwd",
                          out_shape=jax.ShapeDtypeStruct((c_all.shape[0], w_ada.shape[1]), F32))(c_all, w_ada, b_cols)


def _wada_grad_call(c_all, dmod_cols):
    def body(c_ref, d_ref, o_ref):
        o_ref[...] = _dot_tn(_silu(c_ref[...]), d_ref[...])

    return pl.pallas_call(body, name="w_ada_grad",
                          out_shape=jax.ShapeDtypeStruct((c_all.shape[1], dmod_cols.shape[1]), F32))(c_all, dmod_cols)


def _full(shape):
    nd = len(shape)
    return pl.BlockSpec(shape, lambda *_: (0,) * nd)


def _in_proj_call(x2, shift, scale, g_pre, w_in, qg, kg, cos, sin, bd, T):
    R = x2.shape[0]
    TT = min(ROW_TILE, T)
    tpe = T // TT

    def body(x_ref, sh_ref, sc_ref, gp_ref, w_ref, qg_ref, kg_ref, cos_ref, sin_ref, bd_ref,
             hb_ref, qr_ref, kpad_ref, vpad_ref, qraw_ref, kraw_ref, gatt_ref, rin_ref, grw_ref):
        h = _pre_fn(x_ref[...], sh_ref[0], sc_ref[0], gp_ref[...])
        hb = h.astype(MXU_DTYPE)
        hb_ref[...] = hb

        def proj(c0, c1):
            return jnp.dot(hb, w_ref[:, c0:c1], preferred_element_type=F32)

        q = proj(C_Q, C_K)
        k = proj(C_K, C_V)
        v = proj(C_V, C_GA)
        gatt_ref[...] = proj(C_GA, C_RIN)
        rin_ref[...] = proj(C_RIN, C_GRW)
        grw_ref[...] = proj(C_GRW, C_END)
        qraw_ref[...] = q
        kraw_ref[...] = k
        cos, sin, bd = cos_ref[...], sin_ref[...], bd_ref[...]
        qr = _qk_fn(q, qg_ref[...], jnp.tile(cos, (1, 4)), jnp.tile(sin, (1, 4)), bd, ATT_SCALE, False)
        qr_ref[...] = qr.astype(MXU_DTYPE)
        kr = _qk_fn(k, kg_ref[...], cos, sin, bd, 1.0, False)
        left = lax.broadcasted_iota(jnp.int32, (1, KV_W), 1) < HEAD_DIM
        for ref, val in ((kpad_ref, kr), (vpad_ref, v)):
            h0l = jnp.where(left, val, 0.0)
            h1r = jnp.where(left, 0.0, val)
            ref[0] = h0l.astype(MXU_DTYPE)
            ref[1] = pltpu.roll(h0l, HEAD_DIM, 1).astype(MXU_DTYPE)
            ref[2] = pltpu.roll(h1r, HEAD_DIM, 1).astype(MXU_DTYPE)
            ref[3] = h1r.astype(MXU_DTYPE)

    row = lambda w: pl.BlockSpec((TT, w), lambda i: (i, 0))
    per_ex = pl.BlockSpec((1, 1, D_MODEL), lambda i: (i // tpe, 0, 0))
    tab = pl.BlockSpec((TT, KV_W), lambda i: (i % tpe, 0))
    pad = pl.BlockSpec((4, TT, KV_W), lambda i: (0, i, 0))
    sds = jax.ShapeDtypeStruct
    return pl.pallas_call(
        body, name="in_proj", grid=(R // TT,),
        in_specs=[row(D_MODEL), per_ex, per_ex, _full((1, D_MODEL)), _full(w_in.shape), _full((1, ATT_W)),
                  _full((1, KV_W)), tab, tab, _full((256, 256))],
        out_specs=(row(D_MODEL), row(ATT_W), pad, pad, row(ATT_W), row(KV_W), row(ATT_W), row(SHIFT_W), row(RWKV_W)),
        out_shape=(sds((R, D_MODEL), MXU_DTYPE), sds((R, ATT_W), MXU_DTYPE), sds((4, R, KV_W), MXU_DTYPE),
                   sds((4, R, KV_W), MXU_DTYPE), sds((R, ATT_W), F32), sds((R, KV_W), F32), sds((R, ATT_W), F32),
                   sds((R, SHIFT_W), F32), sds((R, RWKV_W), F32)),
        compiler_params=_cp(("arbitrary",)),
    )(x2, shift, scale, g_pre, w_in, qg, kg, cos, sin, bd)


def _softmax_rows(s):
    m = jnp.max(s, axis=1, keepdims=True)
    e = jnp.exp(s - m)
    return e / jnp.sum(e, axis=1, keepdims=True)


def _att_specs(T, TQ):
    nq = T // TQ
    qspec = pl.BlockSpec((TQ, KV_W), lambda b, p, i: (b * nq + i, p))
    side = lambda s: pl.BlockSpec((None, T, KV_W), lambda b, p, i: (2 * (p // 2) + s, b, 0))
    return nq, qspec, side


def _att_fwd_call(qr, kpad, vpad, B, T):
    TQ = min(ROW_TILE, T)
    nq, qspec, side = _att_specs(T, TQ)

    def body(q_ref, kl_ref, kr_ref, vl_ref, vr_ref, o_ref):
        q = q_ref[...]
        pa = _softmax_rows(_dot_nt(q, kl_ref[...]))
        pb = _softmax_rows(_dot_nt(q, kr_ref[...]))
        o_ref[...] = _dot(pa, vl_ref[...]) + _dot(pb, vr_ref[...])

    return pl.pallas_call(
        body, name="att_fwd", grid=(B, 4, nq),
        in_specs=[qspec, side(0), side(1), side(0), side(1)], out_specs=qspec,
        out_shape=jax.ShapeDtypeStruct((B * T, ATT_W), F32),
        compiler_params=_cp(("arbitrary",) * 3),
    )(qr, kpad, kpad, vpad, vpad)


def _att_bwd_call(qr, kpad, vpad, d_o, B, T):
    TQ = min(ROW_TILE, T)
    nq, qspec, side = _att_specs(T, TQ)

    def body(q_ref, kl_ref, kr_ref, vl_ref, vr_ref, do_ref, dq_ref, dk_ref, dv_ref):
        i = pl.program_id(2)
        q, do = q_ref[...], do_ref[...]
        left = lax.broadcasted_iota(jnp.int32, (1, KV_W), 1) < HEAD_DIM
        dq = jnp.zeros((TQ, KV_W), F32)
        dk = jnp.zeros((T, KV_W), F32)
        dv = jnp.zeros((T, KV_W), F32)
        for k_ref, v_ref, mask in ((kl_ref, vl_ref, left), (kr_ref, vr_ref, jnp.logical_not(left))):
            kk, vv = k_ref[...], v_ref[...]
            p = _softmax_rows(_dot_nt(q, kk))
            dp = _dot_nt(do, vv)
            ds = p * (dp - jnp.sum(p * dp, axis=1, keepdims=True))
            dq = dq + _dot(ds, kk)
            dk = dk + _dot_tn(ds, jnp.where(mask, q, jnp.zeros_like(q)))
            dv = dv + _dot_tn(p, jnp.where(mask, do, 0.0))
        dq_ref[...] = dq

        @pl.when(i == 0)
        def _():
            dk_ref[...] = dk
            dv_ref[...] = dv

        @pl.when(i > 0)
        def _():
            dk_ref[...] += dk
            dv_ref[...] += dv

    acc = pl.BlockSpec((None, T, KV_W), lambda b, p, i: (p, b, 0))
    sds = jax.ShapeDtypeStruct
    return pl.pallas_call(
        body, name="att_bwd", grid=(B, 4, nq),
        in_specs=[qspec, side(0), side(1), side(0), side(1), qspec], out_specs=(qspec, acc, acc),
        out_shape=(sds((B * T, ATT_W), F32), sds((4, B * T, KV_W), F32), sds((4, B * T, KV_W), F32)),
        compiler_params=_cp(("arbitrary",) * 3),
    )(qr, kpad, kpad, vpad, vpad, d_o)


def _shift_specs(R, T, TT, width):
    tpe = T // TT
    nb8 = R // 8
    cur = pl.BlockSpec((TT, width), lambda i: (i, 0))
    prev = pl.BlockSpec((8, width), lambda i: (jnp.maximum(i * (TT // 8) - 1, 0), 0))
    nxt = pl.BlockSpec((8, width), lambda i: (jnp.minimum((i + 1) * (TT // 8), nb8 - 1), 0))
    return tpe, cur, prev, nxt


def _neighbours(cur, prev8, next8, i, tpe, TT):
    rows = lax.broadcasted_iota(jnp.int32, (TT, 1), 0)
    first = jnp.where(i % tpe == 0, 0.0, 1.0)
    last = jnp.where(i % tpe == tpe - 1, 0.0, 1.0)
    before = jnp.where(rows == 0, prev8[7:8, :] * first, pltpu.roll(cur, 1, 0))
    after = jnp.where(rows == TT - 1, next8[0:1, :] * last, pltpu.roll(cur, TT - 1, 0))
    return before, after


def _shift_fwd_call(x, taps, T):
    R, width = x.shape
    TT = min(ROW_TILE, T)
    tpe, cur, prev, nxt = _shift_specs(R, T, TT, width)

    def body(x_ref, p_ref, n_ref, t_ref, o_ref):
        xc = x_ref[...]
        before, after = _neighbours(xc, p_ref[...], n_ref[...], pl.program_id(0), tpe, TT)
        o_ref[...] = t_ref[0:1, :] * before + t_ref[1:2, :] * xc + t_ref[2:3, :] * after

    return pl.pallas_call(
        body, name="shift_fwd", grid=(R // TT,), in_specs=[cur, prev, nxt, _full(taps.shape)], out_specs=cur,
        out_shape=jax.ShapeDtypeStruct((R, width), F32), compiler_params=_cp(("arbitrary",)),
    )(x, x, x, taps)


def _shift_bwd_call(x, d, taps, T):
    R, width = x.shape
    TT = min(ROW_TILE, T)
    tpe, cur, prev, nxt = _shift_specs(R, T, TT, width)

    def body(x_ref, xp_ref, xn_ref, d_ref, dp_ref, dn_ref, t_ref, dx_ref, dt_ref):
        i = pl.program_id(0)
        xc, dc = x_ref[...], d_ref[...]
        d_before, d_after = _neighbours(dc, dp_ref[...], dn_ref[...], i, tpe, TT)
        dx_ref[...] = t_ref[2:3, :] * d_before + t_ref[1:2, :] * dc + t_ref[0:1, :] * d_after
        x_before, x_after = _neighbours(xc, xp_ref[...], xn_ref[...], i, tpe, TT)
        @pl.when(i == 0)
        def _():
            dt_ref[...] = jnp.zeros_like(dt_ref)

        for j, xs in enumerate((x_before, xc, x_after)):
            dt_ref[j:j + 1, :] += jnp.sum(dc * xs, axis=0, keepdims=True)

    return pl.pallas_call(
        body, name="shift_bwd", grid=(R // TT,),
        in_specs=[cur, prev, nxt, cur, prev, nxt, _full(taps.shape)], out_specs=(cur, _full((8, width))),
        out_shape=(jax.ShapeDtypeStruct((R, width), F32), jax.ShapeDtypeStruct((8, width), F32)),
        compiler_params=_cp(("arbitrary",)),
    )(x, x, x, d, d, d, taps)


def _lora_in(wa):
    lane = lax.broadcasted_iota(jnp.int32, (1, LORA_W), 1)
    return jnp.where(lane < LORA_W // 2, jnp.tanh(wa), wa)


def _rwkv_prep_call(shifted, wup, aup, w0, a0, k_k, k_a, bd, T):
    R = shifted.shape[0]
    TT = min(ROW_TILE, T)

    def body(k_ref, wa_ref, wup_ref, aup_ref, w0_ref, a0_ref, kk_ref, ka_ref, bd_ref, w_o, kt_o, akk_o, kk_o):
        twa = _lora_in(wa_ref[...])
        pre = [_dot(twa, m_ref[z]) for m_ref in (wup_ref, aup_ref) for z in range(2)]
        outs = _rwkv_pw(k_ref[...], pre[0], pre[1], pre[2], pre[3], w0_ref[...], a0_ref[...], kk_ref[...],
                        ka_ref[...], bd_ref[...], False)
        w_o[0], w_o[1], kt_o[0], kt_o[1], akk_o[0], akk_o[1] = outs[:6]
        kk_o[...] = outs[6]

    col = lambda c, w: pl.BlockSpec((TT, w), lambda i: (i, c))
    two = pl.BlockSpec((2, TT, RWKV_W), lambda i: (0, i, 0))
    sds = jax.ShapeDtypeStruct
    return pl.pallas_call(
        body, name="rwkv_prep", grid=(R // TT,),
        in_specs=[col(1, RWKV_W), col(3 * RWKV_W // LORA_W, LORA_W), _full(wup.shape), _full(aup.shape),
                  _full((2, RWKV_W)), _full((2, RWKV_W)), _full((1, RWKV_W)), _full((1, RWKV_W)), _full((256, 256))],
        out_specs=(two, two, two, col(0, RWKV_W)),
        out_shape=(sds((2, R, RWKV_W), F32),) * 3 + (sds((R, RWKV_W), F32),),
        compiler_params=_cp(("arbitrary",)),
    )(shifted, shifted, wup, aup, w0, a0, k_k, k_a, bd)


def _rwkv_prep_bwd_call(shifted, cts, wup, aup, w0, a0, k_k, k_a, bd, T):
    R = shifted.shape[0]
    TT = min(ROW_TILE, T)

    def body(k_ref, wa_ref, dw0, dkt0, dakk0, dkk0, dr0, dv0, dw1, dkt1, dakk1, dkk1, dr1, dv1, dr2_ref, dv2_ref, dkts_ref,
             wup_ref, aup_ref, w0_ref, a0_ref, kk_ref, ka_ref, bd_ref,
             dsh_ref, gwup_ref, gaup_ref, gw0_ref, ga0_ref, gkk_ref, gka_ref):
        dw_ref, dkt_ref, dakk_ref, dkk_ref, dr_ref, dv_ref = ((dw0, dw1), (dkt0, dkt1), (dakk0, dakk1), (dkk0, dkk1),
                                                              (dr0, dr1), (dv0, dv1))
        i = pl.program_id(0)
        wa = wa_ref[...]
        twa = _lora_in(wa)
        pre = [_dot(twa, m_ref[z]) for m_ref in (wup_ref, aup_ref) for z in range(2)]
        fn = functools.partial(_rwkv_pw, bd=bd_ref[...], diff=True)
        _, vjp = jax.vjp(fn, k_ref[...], pre[0], pre[1], pre[2], pre[3], w0_ref[...], a0_ref[...], kk_ref[...],
                         ka_ref[...])
        dkts = dkts_ref[...]
        dk, dpw0, dpw1, dpa0, dpa1, gw0, ga0, gkk, gka = vjp(
            (dw_ref[0][...], dw_ref[1][...], dkt_ref[0][...] + dkts, dkt_ref[1][...] + dkts, dakk_ref[0][...],
             dakk_ref[1][...], dkk_ref[0][...] + dkk_ref[1][...]))
        dtwa = (_dot_nt(dpw0, wup_ref[0]) + _dot_nt(dpw1, wup_ref[1]) + _dot_nt(dpa0, aup_ref[0])
                + _dot_nt(dpa1, aup_ref[1]))
        lane = lax.broadcasted_iota(jnp.int32, (1, LORA_W), 1)
        dsh_ref[:, 0:RWKV_W] = dr_ref[0][...] + dr_ref[1][...] + dr2_ref[...]
        dsh_ref[:, RWKV_W:2 * RWKV_W] = dk
        dsh_ref[:, 2 * RWKV_W:3 * RWKV_W] = dv_ref[0][...] + dv_ref[1][...] + dv2_ref[...]
        dsh_ref[:, 3 * RWKV_W:] = jnp.where(lane < LORA_W // 2, dtwa * (1.0 - twa * twa), dtwa)
        acc = ((gwup_ref.at[0], _dot_tn(twa, dpw0)), (gwup_ref.at[1], _dot_tn(twa, dpw1)),
               (gaup_ref.at[0], _dot_tn(twa, dpa0)), (gaup_ref.at[1], _dot_tn(twa, dpa1)),
               (gw0_ref, gw0), (ga0_ref, ga0), (gkk_ref, gkk), (gka_ref, gka))

        @pl.when(i == 0)
        def _():
            for ref, val in acc:
                ref[...] = val

        @pl.when(i > 0)
        def _():
            for ref, val in acc:
                ref[...] += val

    col = lambda c, w: pl.BlockSpec((TT, w), lambda i: (i, c))
    one = col(0, RWKV_W)
    sds = jax.ShapeDtypeStruct
    return pl.pallas_call(
        body, name="rwkv_prep_bwd", grid=(R // TT,),
        in_specs=[col(1, RWKV_W), col(3 * RWKV_W // LORA_W, LORA_W)] + [one] * 15 + [
                  _full(wup.shape), _full(aup.shape), _full((2, RWKV_W)), _full((2, RWKV_W)), _full((1, RWKV_W)),
                  _full((1, RWKV_W)), _full((256, 256))],
        out_specs=(pl.BlockSpec((TT, SHIFT_W), lambda i: (i, 0)), _full(wup.shape), _full(aup.shape),
                   _full((2, RWKV_W)), _full((2, RWKV_W)), _full((1, RWKV_W)), _full((1, RWKV_W))),
        out_shape=(sds((R, SHIFT_W), F32), sds(wup.shape, F32), sds(aup.shape, F32), sds((2, RWKV_W), F32),
                   sds((2, RWKV_W), F32), sds((1, RWKV_W), F32), sds((1, RWKV_W), F32)),
        compiler_params=_cp(("arbitrary",)),
    )(shifted, shifted, *cts, wup, aup, w0, a0, k_k, k_a, bd)


def _col_lhs(row, eye_b):
    return eye_b * row.astype(MXU_DTYPE)


def _colsum(x):
    return jnp.sum(x, axis=0, keepdims=True)


def _stacked_segsum(tiles, bd):
    res = _seg_dot(jnp.concatenate(tiles, axis=0), bd)
    return [res[j * HEAD_DIM:(j + 1) * HEAD_DIM] for j in range(len(tiles))]


def _scan_specs(B, T, C, nC):
    def blk(z, col, rev):
        idx = (lambda g: (z, 0, nC - 1 - g, col)) if rev else (lambda g: (z, 0, g, col))
        return pl.BlockSpec((None, B, C, RWKV_W), idx)

    def blk3(col, rev):
        idx = (lambda g: (0, nC - 1 - g, col)) if rev else (lambda g: (0, g, col))
        return pl.BlockSpec((B, C, RWKV_W), idx)

    return blk, blk3


def _scan_fwd_call(w, kt, akk, kk, shifted, eye_b, eye_f, bd, B, T):
    C = min(SCAN_CHUNK, T)
    nC = T // C
    blk, blk3 = _scan_specs(B, T, C, nC)

    def body(w0, kt0, akk0, kk0, v0, r0, w1, kt1, akk1, kk1, v1, r1, eb_ref, ef_ref, bd_ref, y0, y1, st, S):
        @pl.when(pl.program_id(0) == 0)
        def _():
            S[...] = jnp.zeros_like(S)

        st[0] = S[...].astype(MXU_DTYPE)
        dirs = ((w0, kt0, akk0, kk0, v0, r0, y0), (w1, kt1, akk1, kk1, v1, r1, y1))

        def step(s, carry):
            for z in range(2):
                row = s if z == 0 else C - 1 - s
                prev = jnp.maximum(s - 1, 0) if z == 0 else jnp.minimum(C - s, C - 1)
                wr, ktr, akkr, kkr, vr, rr, yr = dirs[z]
                tiles = []
                for b in range(B):
                    Sb = st[s, z * B + b]
                    tiles += [Sb * kkr[b, pl.ds(row, 1), :].astype(MXU_DTYPE),
                              _col_lhs(vr[b, pl.ds(row, 1), :], eb_ref[...]),
                              Sb * rr[b, pl.ds(prev, 1), :].astype(MXU_DTYPE)]
                res = _stacked_segsum(tiles, bd_ref[...])
                for b in range(B):
                    c = z * B + b
                    sab, vb, yb = res[3 * b:3 * b + 3]
                    ld = lambda ref: ref[b, pl.ds(row, 1), :]
                    Sn = S[c] * ld(wr) - sab * ld(akkr) + vb * ld(ktr)
                    S[c] = Sn
                    st[s + 1, c] = Sn.astype(MXU_DTYPE)
                    yr[b, pl.ds(prev, 1), :] = _colsum(ef_ref[...] * yb)
            return carry

        lax.fori_loop(0, C, step, 0, unroll=SCAN_UNROLL)
        for z in range(2):
            last = C - 1 if z == 0 else 0
            rr, yr = dirs[z][5], dirs[z][6]
            res = _stacked_segsum([st[C, z * B + b] * rr[b, last:last + 1, :].astype(MXU_DTYPE) for b in range(B)],
                                  bd_ref[...])
            for b in range(B):
                yr[b, last:last + 1, :] = _colsum(ef_ref[...] * res[b])

    ins, specs = [], []
    for z, rev in ((0, False), (1, True)):
        ins += [w, kt, akk, kk, shifted, shifted]
        specs += [blk(z, 0, rev), blk(z, 0, rev), blk(z, 0, rev), blk3(0, rev), blk3(2, rev), blk3(0, rev)]
    sds = jax.ShapeDtypeStruct
    return pl.pallas_call(
        body, name="scan_fwd", grid=(nC,),
        in_specs=specs + [_full((HEAD_DIM, RWKV_W)), _full((HEAD_DIM, RWKV_W)), _full((256, 256))],
        out_specs=(blk3(0, False), blk3(0, True),
                   pl.BlockSpec((None, C + 1, 2 * B, HEAD_DIM, RWKV_W), lambda g: (g, 0, 0, 0, 0))),
        out_shape=(sds((B, T, RWKV_W), F32), sds((B, T, RWKV_W), F32),
                   sds((nC, C + 1, 2 * B, HEAD_DIM, RWKV_W), MXU_DTYPE)),
        scratch_shapes=[pltpu.VMEM((2 * B, HEAD_DIM, RWKV_W), F32)],
        compiler_params=_cp(("arbitrary",)),
    )(*ins, eye_b, eye_f, bd)


def _scan_bwd_call(w, kt, akk, kk, shifted, dys, st, eye_b, eye_f, bd, B, T):
    C = min(SCAN_CHUNK, T)
    nC = T // C
    blk, blk3 = _scan_specs(B, T, C, nC)
    nin = 7

    def body(*refs):
        d0, d1 = refs[:nin], refs[nin:2 * nin]
        st_ref, eb_ref, ef_ref, bd_ref = refs[2 * nin:2 * nin + 4]
        o0, o1 = refs[2 * nin + 4:2 * nin + 10], refs[2 * nin + 10:2 * nin + 16]
        COL, G = refs[2 * nin + 16:]

        @pl.when(pl.program_id(0) == 0)
        def _():
            G[...] = jnp.zeros_like(G)

        dirs = (d0 + (o0,), d1 + (o1,))

        def column_operands(s, z):
            row = s if z == 0 else C - 1 - s
            _, _, _, kkr, vr, _, dyr, _ = dirs[z]
            tiles = []
            for b in range(B):
                tiles += [st_ref[s, z * B + b] * kkr[b, pl.ds(row, 1), :].astype(MXU_DTYPE),
                          _col_lhs(vr[b, pl.ds(row, 1), :], eb_ref[...]),
                          _col_lhs(dyr[b, pl.ds(row, 1), :], eb_ref[...])]
            return tiles

        def keep_columns(res, z):
            for b in range(B):
                for k in range(3):
                    COL[k, z * B + b] = res[3 * b + k]

        for z in range(2):
            keep_columns(_stacked_segsum(column_operands(C - 1, z), bd_ref[...]), z)

        def bwd(it, carry):
            s = C - 1 - it
            for z in range(2):
                row = s if z == 0 else C - 1 - s
                wr, ktr, akkr, kkr, vr, rr, dyr, (dw_o, dkt_o, dakk_o, dkk_o, dr_o, dv_o) = dirs[z]
                tiles, Gcs = [], []
                for b in range(B):
                    c = z * B + b
                    Gc = G[c] + COL[2, c] * rr[b, pl.ds(row, 1), :]
                    Gcs.append(Gc)
                    Gb = Gc.astype(MXU_DTYPE)
                    tiles += [Gb * akkr[b, pl.ds(row, 1), :].astype(MXU_DTYPE),
                              Gb * ktr[b, pl.ds(row, 1), :].astype(MXU_DTYPE)]
                res = _stacked_segsum(tiles + column_operands(jnp.maximum(s - 1, 0), z), bd_ref[...])
                for b in range(B):
                    c = z * B + b
                    Gc = Gcs[b]
                    gab, dvb = res[2 * b], res[2 * b + 1]
                    ld = lambda ref: ref[b, pl.ds(row, 1), :]
                    G[c] = Gc * ld(wr) - gab * ld(kkr)
                    S = st_ref[s, c].astype(F32)
                    for ref, val in ((dr_o, _colsum(st_ref[s + 1, c].astype(F32) * COL[2, c])),
                                     (dkt_o, _colsum(Gc * COL[1, c])), (dv_o, _colsum(ef_ref[...] * dvb)),
                                     (dw_o, _colsum(Gc * S)), (dakk_o, -_colsum(Gc * COL[0, c])),
                                     (dkk_o, -_colsum(gab * S))):
                        ref[b, pl.ds(row, 1), :] = val
                keep_columns(res[2 * B:], z)
            return carry

        lax.fori_loop(0, C, bwd, 0, unroll=SCAN_UNROLL)

    ins, specs = [], []
    for z, rev in ((0, True), (1, False)):
        ins += [w, kt, akk, kk, shifted, shifted, dys]
        specs += [blk(z, 0, rev), blk(z, 0, rev), blk(z, 0, rev), blk3(0, rev), blk3(2, rev), blk3(0, rev), blk3(0, rev)]
    ins += [st, eye_b, eye_f, bd]
    specs += [pl.BlockSpec((None, C + 1, 2 * B, HEAD_DIM, RWKV_W), lambda g: (nC - 1 - g, 0, 0, 0, 0)),
              _full((HEAD_DIM, RWKV_W)), _full((HEAD_DIM, RWKV_W)), _full((256, 256))]
    sds = jax.ShapeDtypeStruct
    out_specs = tuple(blk3(0, True) for _ in range(6)) + tuple(blk3(0, False) for _ in range(6))
    res = pl.pallas_call(
        body, name="scan_bwd", grid=(nC,), in_specs=specs, out_specs=out_specs,
        out_shape=tuple(sds((B, T, RWKV_W), F32) for _ in range(12)),
        scratch_shapes=[pltpu.VMEM((3, 2 * B, HEAD_DIM, RWKV_W), F32), pltpu.VMEM((2 * B, HEAD_DIM, RWKV_W), F32)],
        compiler_params=_cp(("arbitrary",)),
    )(*ins)
    return list(res)


def _out_head_call(x2, tgt2, gate, y_att, g_att, y0, y1, shifted, kt, g_rw, w_out, g_post, gn_w, gn_b, r_k, bd, T):
    R = x2.shape[0]
    TT = min(ROW_TILE, T)
    tpe = T // TT

    def body(x_ref, t_ref, gate_ref, ya_ref, ga_ref, y0_ref, y1_ref, r_ref, v_ref, kt_ref, grw_ref, w_ref, gp_ref,
             gnw_ref, gnb_ref, rk_ref, bd_ref,
             loss_o, dy_o, dya_o, dga_o, dys_o, dr_o, dv_o, dkts_o, dgrw_o, dgate_o, gw_o, ggp_o, ggnw_o, ggnb_o, grk_o):
        i = pl.program_id(0)
        bd = bd_ref[...]
        mix = functools.partial(_mix_fn, bd=bd, diff=True)
        (ma, mr), mix_vjp = jax.vjp(mix, ya_ref[...], ga_ref[...], y0_ref[...] + y1_ref[...], r_ref[...], v_ref[...],
                                    kt_ref[0] + kt_ref[1], grw_ref[...], gnw_ref[...], gnb_ref[...], rk_ref[...])
        out = _dot(ma, w_ref[0:ATT_W, :]) + _dot(mr, w_ref[ATT_W:, :])
        loss, loss_vjp = jax.vjp(_loss_fn, out, x_ref[...], t_ref[...], gate_ref[0], gp_ref[...])
        d_out, dy, _, dgate, dgp = loss_vjp(jnp.ones((1, 1), F32))
        dy_o[...] = dy
        dma = _dot_nt(d_out, w_ref[0:ATT_W, :])
        dmr = _dot_nt(d_out, w_ref[ATT_W:, :])
        dya_o[...], dga_o[...], dys_o[...], dr_o[...], dv_o[...], dkts_o[...], dgrw_o[...], dgnw, dgnb, drk = \
            mix_vjp((dma, dmr))
        gw = jnp.concatenate([_dot_tn(ma, d_out), _dot_tn(mr, d_out)], axis=0)
        acc = ((loss_o, jnp.broadcast_to(loss, (8, 128))), (gw_o, gw), (ggp_o, dgp), (ggnw_o, dgnw), (ggnb_o, dgnb),
               (grk_o, drk))

        @pl.when(i == 0)
        def _():
            for ref, val in acc:
                ref[...] = val

        @pl.when(i > 0)
        def _():
            for ref, val in acc:
                ref[...] += val

        @pl.when(i % tpe == 0)
        def _():
            dgate_o[0] = dgate

        @pl.when(i % tpe > 0)
        def _():
            dgate_o[0] += dgate

    row = lambda w, c=0: pl.BlockSpec((TT, w), lambda i: (i, c))
    two = pl.BlockSpec((2, TT, RWKV_W), lambda i: (0, i, 0))
    per_ex = pl.BlockSpec((1, 1, D_MODEL), lambda i: (i // tpe, 0, 0))
    sds = jax.ShapeDtypeStruct
    r512 = sds((R, RWKV_W), F32)
    return pl.pallas_call(
        body, name="out_head", grid=(R // TT,),
        in_specs=[row(D_MODEL), row(D_MODEL), per_ex, row(ATT_W), row(ATT_W), row(RWKV_W), row(RWKV_W), row(RWKV_W, 0),
                  row(RWKV_W, 2), two,
                  row(RWKV_W), _full(w_out.shape), _full((1, D_MODEL)), _full((1, RWKV_W)), _full((1, RWKV_W)),
                  _full((1, RWKV_W)), _full((256, 256))],
        out_specs=(_full((8, 128)), row(D_MODEL), row(ATT_W), row(ATT_W), row(RWKV_W), row(RWKV_W), row(RWKV_W),
                   row(RWKV_W), row(RWKV_W), per_ex, _full((D_MODEL, D_MODEL)), _full((1, D_MODEL)), _full((1, RWKV_W)),
                   _full((1, RWKV_W)), _full((1, RWKV_W))),
        out_shape=(sds((8, 128), F32), sds((R, D_MODEL), F32), r512, r512, r512, r512, r512, r512, r512,
                   sds((R // T, 1, D_MODEL), F32), sds((D_MODEL, D_MODEL), F32), sds((1, D_MODEL), F32),
                   sds((1, RWKV_W), F32), sds((1, RWKV_W), F32), sds((1, RWKV_W), F32)),
        compiler_params=_cp(("arbitrary",)),
    )(x2, tgt2, gate, y_att, g_att, y0, y1, shifted, shifted, kt, g_rw, w_out, g_post, gn_w, gn_b, r_k, bd)


def _in_proj_bwd_call(x2, dy, shift, scale, g_pre, w_in, qg, kg, cos, sin, bd, q_raw, k_raw, dqr, dkp, dvp,
                      d_gatt, d_rin, d_grw, T):
    R = x2.shape[0]
    TT = min(ROW_TILE, T)
    tpe = T // TT

    def body(x_ref, dy_ref, sh_ref, sc_ref, gp_ref, w_ref, qg_ref, kg_ref, cos_ref, sin_ref, bd_ref, q_ref, k_ref,
             dqr_ref, dkp_ref, dvp_ref, dga_ref, drin_ref, dgrw_ref,
             dx_o, dproj_o, dsh_o, dsc_o, ggp_o, gqg_o, gkg_o):
        i = pl.program_id(0)
        cos, sin, bd = cos_ref[...], sin_ref[...], bd_ref[...]
        left = lax.broadcasted_iota(jnp.int32, (1, KV_W), 1) < HEAD_DIM

        def kv_grad(ref):
            a = ref[0] + ref[1]
            b = ref[2] + ref[3]
            return jnp.where(left, a + pltpu.roll(a, HEAD_DIM, 1), b + pltpu.roll(b, HEAD_DIM, 1))

        qfn = functools.partial(_qk_fn, cos=jnp.tile(cos, (1, 4)), sin=jnp.tile(sin, (1, 4)), bd=bd, scale=ATT_SCALE,
                                diff=True)
        _, q_vjp = jax.vjp(qfn, q_ref[...], qg_ref[...])
        dq, gqg = q_vjp(dqr_ref[...])
        kfn = functools.partial(_qk_fn, cos=cos, sin=sin, bd=bd, scale=1.0, diff=True)
        _, k_vjp = jax.vjp(kfn, k_ref[...], kg_ref[...])
        dk, gkg = k_vjp(kv_grad(dkp_ref))
        pieces = ((C_Q, C_K, dq), (C_K, C_V, dk), (C_V, C_GA, kv_grad(dvp_ref)), (C_GA, C_RIN, dga_ref[...]),
                  (C_RIN, C_GRW, drin_ref[...]), (C_GRW, C_END, dgrw_ref[...]))
        dh = jnp.zeros((TT, D_MODEL), F32)
        for c0, c1, val in pieces:
            vb = val.astype(MXU_DTYPE)
            dproj_o[:, c0:c1] = vb
            dh = dh + _dot_nt(vb, w_ref[:, c0:c1])
        _, pre_vjp = jax.vjp(_pre_fn, x_ref[...], sh_ref[0], sc_ref[0], gp_ref[...])
        dx, dsh, dsc, ggp = pre_vjp(dh)
        dx_o[...] = dx + dy_ref[...]
        acc = ((ggp_o, ggp), (gqg_o, gqg), (gkg_o, gkg))

        @pl.when(i == 0)
        def _():
            for ref, val in acc:
                ref[...] = val

        @pl.when(i > 0)
        def _():
            for ref, val in acc:
                ref[...] += val

        @pl.when(i % tpe == 0)
        def _():
            dsh_o[0] = dsh
            dsc_o[0] = dsc

        @pl.when(i % tpe > 0)
        def _():
            dsh_o[0] += dsh
            dsc_o[0] += dsc

    row = lambda w: pl.BlockSpec((TT, w), lambda i: (i, 0))
    per_ex = pl.BlockSpec((1, 1, D_MODEL), lambda i: (i // tpe, 0, 0))
    tab = pl.BlockSpec((TT, KV_W), lambda i: (i % tpe, 0))
    pad = pl.BlockSpec((4, TT, KV_W), lambda i: (0, i, 0))
    sds = jax.ShapeDtypeStruct
    nb = R // T
    return pl.pallas_call(
        body, name="in_proj_bwd", grid=(R // TT,),
        in_specs=[row(D_MODEL), row(D_MODEL), per_ex, per_ex, _full((1, D_MODEL)), _full(w_in.shape), _full((1, ATT_W)),
                  _full((1, KV_W)), tab, tab, _full((256, 256)), row(ATT_W), row(KV_W), row(ATT_W), pad, pad,
                  row(ATT_W), row(SHIFT_W), row(RWKV_W)],
        out_specs=(row(D_MODEL), row(C_END), per_ex, per_ex, _full((1, D_MODEL)), _full((1, ATT_W)), _full((1, KV_W))),
        out_shape=(sds((R, D_MODEL), F32), sds((R, C_END), MXU_DTYPE), sds((nb, 1, D_MODEL), F32),
                   sds((nb, 1, D_MODEL), F32), sds((1, D_MODEL), F32), sds((1, ATT_W), F32), sds((1, KV_W), F32)),
        compiler_params=_cp(("arbitrary",)),
    )(x2, dy, shift, scale, g_pre, w_in, qg, kg, cos, sin, bd, q_raw, k_raw, dqr, dkp, dvp, d_gatt, d_rin, d_grw)


def _w_in_grad_call(hb, dproj, T):
    R = hb.shape[0]
    TT = min(ROW_TILE, T)
    CB = 1152

    def body(h_ref, d_ref, o_ref):
        g = _dot_tn(h_ref[...], d_ref[...])

        @pl.when(pl.program_id(1) == 0)
        def _():
            o_ref[...] = g

        @pl.when(pl.program_id(1) > 0)
        def _():
            o_ref[...] += g

    return pl.pallas_call(
        body, name="w_in_grad", grid=(C_END // CB, R // TT),
        in_specs=[pl.BlockSpec((TT, D_MODEL), lambda j, i: (i, 0)), pl.BlockSpec((TT, CB), lambda j, i: (i, j))],
        out_specs=pl.BlockSpec((D_MODEL, CB), lambda j, i: (0, j)),
        out_shape=jax.ShapeDtypeStruct((D_MODEL, C_END), F32), compiler_params=_cp(("arbitrary", "arbitrary")),
    )(hb, dproj)


def _adam_call(parts, w, m, v, name, row_tile=None):
    P, M, N = parts.shape
    TM = M if row_tile is None else row_tile

    def body(p_ref, w_ref, m_ref, v_ref, g_o, d_o, m_o, v_o):
        g = p_ref[0].astype(F32)
        for j in range(1, P):
            g = g + p_ref[j].astype(F32)
        m2 = ADAM_B1 * m_ref[...] + (1.0 - ADAM_B1) * g
        v2 = ADAM_B2 * v_ref[...] + (1.0 - ADAM_B2) * jnp.square(g)
        m_hat = m2 / (1.0 - ADAM_B1 ** ADAM_STEP)
        v_hat = v2 / (1.0 - ADAM_B2 ** ADAM_STEP)
        g_o[...] = g
        d_o[...] = -ADAM_LR * (m_hat / (jnp.sqrt(v_hat) + ADAM_EPS) + ADAM_WD * w_ref[...])
        m_o[...] = m2
        v_o[...] = v2

    blk = pl.BlockSpec((TM, N), lambda i: (i, 0))
    return pl.pallas_call(
        body, name=name, grid=(M // TM,),
        in_specs=[pl.BlockSpec((P, TM, N), lambda i: (0, i, 0)), blk, blk, blk], out_specs=(blk,) * 4,
        out_shape=(jax.ShapeDtypeStruct((M, N), F32),) * 4, compiler_params=_cp(("arbitrary",)),
    )(parts, w, m, v)


_SMALL_ROWS = 136


def _pack_small(taps, w_up, w0, a_up, a0):
    flat = jnp.concatenate([taps.reshape(-1), w_up.reshape(-1), w0.reshape(-1), a_up.reshape(-1), a0.reshape(-1)])
    return jnp.pad(flat, (0, _SMALL_ROWS * 128 - flat.shape[0])).reshape(_SMALL_ROWS, 128)


def _unpack_small(packed):
    n = packed.shape[0]
    flat = packed.reshape(n, -1)
    out, o = [], 0
    for shape in ((3, 208), (2, 64, 64), (2, 64), (2, 64, 64), (2, 64)):
        size = 1
        for s in shape:
            size *= s
        out.append(flat[:, o:o + size].reshape((n,) + shape))
        o += size
    return out


def _cols_to_full(blocks):
    nd = blocks.ndim
    moved = jnp.moveaxis(blocks, 0, nd - 2)
    return moved.reshape(moved.shape[:-2] + (moved.shape[-2] * moved.shape[-1],))


def _full_to_cols(full):
    k = full.shape[-1] // NDEV
    return jnp.moveaxis(full.reshape(full.shape[:-1] + (NDEV, k)), -2, 0)


_REP_SIZES = (("g_pre", 1024), ("q_norm_g", 64), ("k_norm_g", 64), ("k_k", 512), ("k_a", 512), ("r_k", 512),
              ("gn_w", 512), ("gn_b", 512), ("g_post", 1024))
_REP_ROWS = 40


def kernel(x, c, w_ada, b_ada, g_pre, w_in, q_norm_g, k_norm_g, shift_taps, w_up, w0, a_up, a0, k_k, k_a, r_k, gn_w, gn_b, w_out, g_post, loss_target, m_w_ada, m_b_ada, m_g_pre, m_w_in, m_q_norm_g, m_k_norm_g, m_shift_taps, m_w_up, m_w0, m_a_up, m_a0, m_k_k, m_k_a, m_r_k, m_gn_w, m_gn_b, m_w_out, m_g_post, v_w_ada, v_b_ada, v_g_pre, v_w_in, v_q_norm_g, v_k_norm_g, v_shift_taps, v_w_up, v_w0, v_a_up, v_a0, v_k_k, v_k_a, v_r_k, v_gn_w, v_gn_b, v_w_out, v_g_post):
    B, T, _ = x.shape
    R = B * T
    me = 4 * lax.axis_index("x") + 2 * lax.axis_index("y") + lax.axis_index("c")
    x2 = x.reshape(R, D_MODEL)
    tgt2 = loss_target.reshape(R, D_MODEL)

    seg = jnp.arange(256) // HEAD_DIM
    bd = (seg[:, None] == seg[None, :]).astype(MXU_DTYPE)
    eye = (jnp.arange(HEAD_DIM)[:, None] == (jnp.arange(RWKV_W) % HEAD_DIM)[None, :])
    eye_b, eye_f = eye.astype(MXU_DTYPE), eye.astype(F32)
    cos, sin = _rope_tables(T)

    c_g, w_in_g, w_out_g, small_g = _exchange(
        [c, w_in[0].astype(MXU_DTYPE), w_out[0].astype(MXU_DTYPE),
         _pack_small(shift_taps[0], w_up[0], w0[0], a_up[0], a0[0])], [False] * 4, "gather_params")
    c_all = c_g.reshape(NDEV * B, D_MODEL)
    w_in_f = _cols_to_full(w_in_g)
    w_out_f = w_out_g.reshape(D_MODEL, D_MODEL)
    taps_b, w_up_b, w0_b, a_up_b, a0_b = _unpack_small(small_g)
    taps_f = jnp.pad(_cols_to_full(taps_b), ((0, 5), (0, 0)))
    w_up_f, a_up_f = _cols_to_full(w_up_b), _cols_to_full(a_up_b)
    w0_f, a0_f = _cols_to_full(w0_b), _cols_to_full(a0_b)
    wup_pad = jnp.pad(w_up_f, ((0, 0), (0, 64), (0, 0))).astype(MXU_DTYPE)
    aup_pad = jnp.pad(a_up_f, ((0, 0), (64, 0), (0, 0))).astype(MXU_DTYPE)

    ncol = w_ada.shape[2]
    b_cols = lax.dynamic_slice(b_ada, (0, me * ncol), (1, ncol))
    mod_cols = _mod_call(c_all, w_ada[0].astype(MXU_DTYPE), b_cols)
    (mod_g,) = _exchange([mod_cols], [False], "gather_mod")
    mod = lax.dynamic_slice(_cols_to_full(mod_g), (me * B, 0), (B, 3 * D_MODEL))
    shift, scale, gate = [mod[:, j * D_MODEL:(j + 1) * D_MODEL].reshape(B, 1, D_MODEL) for j in range(3)]

    qg = jnp.tile(q_norm_g, (1, ATT_W // HEAD_DIM))
    kg = jnp.tile(k_norm_g, (1, KV_W // HEAD_DIM))
    rk_row = r_k.reshape(1, RWKV_W)

    hb, qr, kpad, vpad, q_raw, k_raw, g_att, rin, g_rw = _in_proj_call(
        x2, shift, scale, g_pre, w_in_f, qg, kg, cos, sin, bd, T)
    y_att = _att_fwd_call(qr, kpad, vpad, B, T)
    shifted = _shift_fwd_call(rin, taps_f, T)
    w_s, kt_s, akk_s, kk_s = _rwkv_prep_call(shifted, wup_pad, aup_pad, w0_f, a0_f, k_k, k_a, bd, T)
    sh3 = shifted.reshape(B, T, SHIFT_W)
    r4 = lambda a: a.reshape(2, B, T, RWKV_W)
    y0, y1, st = _scan_fwd_call(r4(w_s), r4(kt_s), r4(akk_s), kk_s.reshape(B, T, RWKV_W), sh3, eye_b, eye_f, bd, B, T)

    (loss_blk, dy, d_yatt, d_gatt, d_ys, d_r2, d_v2, d_kts, d_grw, d_gate, g_wout, g_gpost, g_gnw, g_gnb,
     g_rk) = _out_head_call(x2, tgt2, gate, y_att, g_att, y0.reshape(R, RWKV_W), y1.reshape(R, RWKV_W), shifted, kt_s,
                            g_rw, w_out_f, g_post, gn_w, gn_b, rk_row, bd, T)
    scan_cts = _scan_bwd_call(r4(w_s), r4(kt_s), r4(akk_s), kk_s.reshape(B, T, RWKV_W), sh3,
                              d_ys.reshape(B, T, RWKV_W), st, eye_b, eye_f, bd, B, T)
    scan_cts = [a.reshape(R, RWKV_W) for a in scan_cts]
    d_shifted, g_wup, g_aup, g_w0, g_a0, g_kk, g_ka = _rwkv_prep_bwd_call(
        shifted, scan_cts + [d_r2, d_v2, d_kts], wup_pad, aup_pad, w0_f, a0_f, k_k, k_a, bd, T)
    d_rin, g_taps = _shift_bwd_call(rin, d_shifted, taps_f, T)
    dqr, dkp, dvp = _att_bwd_call(qr, kpad, vpad, d_yatt, B, T)
    grad_x, dproj, d_shift, d_scale, g_gpre, g_qg, g_kg = _in_proj_bwd_call(
        x2, dy, shift, scale, g_pre, w_in_f, qg, kg, cos, sin, bd, q_raw, k_raw, dqr, dkp, dvp, d_gatt, d_rin, d_grw, T)
    g_win = _w_in_grad_call(hb, dproj, T)

    rep = jnp.concatenate([g_gpre.reshape(-1), g_qg.reshape(-1, HEAD_DIM).sum(0), g_kg.reshape(-1, HEAD_DIM).sum(0),
                           g_kk.reshape(-1), g_ka.reshape(-1), g_rk.reshape(-1), g_gnw.reshape(-1), g_gnb.reshape(-1),
                           g_gpost.reshape(-1), loss_blk[0, :1]])
    rep = jnp.pad(rep, (0, _REP_ROWS * 128 - rep.shape[0])).reshape(_REP_ROWS, 128)
    dmod = jnp.concatenate([d_shift, d_scale, d_gate], axis=2).reshape(B, 3 * D_MODEL)
    small_parts = jax.vmap(_pack_small)(_full_to_cols(g_taps[:3]), _full_to_cols(g_wup[:, :64, :]), _full_to_cols(g_w0),
                                        _full_to_cols(g_aup[:, 64:, :]), _full_to_cols(g_a0))
    p_win, p_wout, p_small, dmod_g, rep_g = _exchange(
        [_full_to_cols(g_win).astype(MXU_DTYPE), g_wout.reshape(NDEV, D_MODEL // NDEV, D_MODEL).astype(MXU_DTYPE),
         small_parts, dmod, rep],
        [True, True, True, False, False], "reduce_grads")
    dmod_all = dmod_g.reshape(NDEV * B, 3 * D_MODEL)
    g_wada = _wada_grad_call(c_all, lax.dynamic_slice(dmod_all, (0, me * ncol), (NDEV * B, ncol)))

    res = {}

    def adam(name, parts, w, m, v, row_tile=None):
        shape = w.shape
        two_d = (-1, shape[-1])
        out = _adam_call(parts.reshape((parts.shape[0],) + w.reshape(two_d).shape), w.reshape(two_d), m.reshape(two_d),
                         v.reshape(two_d), "adam_" + name, row_tile)
        res[name] = [o.reshape(shape) for o in out]

    adam("w_ada", g_wada[None], w_ada, m_w_ada, v_w_ada)
    adam("b_ada", dmod_all.reshape(NDEV * B, 1, 3 * D_MODEL), b_ada, m_b_ada, v_b_ada)
    adam("w_in", p_win, w_in, m_w_in, v_w_in, 128)
    adam("w_out", p_wout, w_out, m_w_out, v_w_out)
    taps_p, wup_p, w0_p, aup_p, a0_p = _unpack_small(p_small)
    adam("shift_taps", taps_p, shift_taps, m_shift_taps, v_shift_taps)
    adam("w_up", wup_p, w_up, m_w_up, v_w_up)
    adam("w0", w0_p, w0, m_w0, v_w0)
    adam("a_up", aup_p, a_up, m_a_up, v_a_up)
    adam("a0", a0_p, a0, m_a0, v_a0)
    rep_flat = rep_g.reshape(NDEV, -1)
    off = 0
    given = dict(g_pre=(g_pre, m_g_pre, v_g_pre), q_norm_g=(q_norm_g, m_q_norm_g, v_q_norm_g),
                 k_norm_g=(k_norm_g, m_k_norm_g, v_k_norm_g), k_k=(k_k, m_k_k, v_k_k), k_a=(k_a, m_k_a, v_k_a),
                 r_k=(r_k, m_r_k, v_r_k), gn_w=(gn_w, m_gn_w, v_gn_w), gn_b=(gn_b, m_gn_b, v_gn_b),
                 g_post=(g_post, m_g_post, v_g_post))
    for name, size in _REP_SIZES:
        adam(name, rep_flat[:, off:off + size], *given[name])
        off += size

    loss = jnp.sum(rep_flat[:, off])
    order = ["w_ada", "b_ada", "g_pre", "w_in", "q_norm_g", "k_norm_g", "shift_taps", "w_up", "w0", "a_up", "a0", "k_k",
             "k_a", "r_k", "gn_w", "gn_b", "w_out", "g_post"]
    return (loss, grad_x.reshape(B, T, D_MODEL), *[res[n][0] for n in order], *[res[n][1] for n in order],
            *[res[n][2] for n in order], *[res[n][3] for n in order])
```

```python
import functools

import jax
import jax.numpy as jnp
from jax import lax
from jax.experimental import pallas as pl
from jax.experimental.pallas import tpu as pltpu

F32 = jnp.float32
MXU_DTYPE = jnp.bfloat16
MESH = pl.DeviceIdType.MESH
NDEV = 8

D_MODEL = 1024
HEAD_DIM = 64
ATT_W = 512
KV_W = 128
RWKV_W = 512
LORA_W = 128
SHIFT_W = 3 * RWKV_W + LORA_W
GRID_W = 64
ROPE_THETA = 10000.0
DECAY_SCALE = 0.6065306597126334
NORM_EPS = 1e-6
GN_EPS = 64e-5
L2_EPS = 1e-12
ATT_SCALE = HEAD_DIM ** -0.5
C_Q, C_K, C_V, C_GA, C_RIN, C_GRW, C_END = 0, 512, 640, 768, 1280, 2944, 3456

ADAM_LR, ADAM_B1, ADAM_B2, ADAM_EPS, ADAM_WD, ADAM_STEP = 0.001, 0.9, 0.999, 1e-08, 0.01, 10

ROW_TILE = 256
SCAN_CHUNK = 32
SCAN_UNROLL = 4
VMEM_LIMIT = 56 * 1024 * 1024


def _cp(sem=None):
    return pltpu.CompilerParams(dimension_semantics=sem, vmem_limit_bytes=VMEM_LIMIT)


def _dot(a, b, dims=(((1,), (0,)), ((), ()))):
    return lax.dot_general(a.astype(MXU_DTYPE), b.astype(MXU_DTYPE), dims, preferred_element_type=F32)


def _dot_nt(a, b):
    return _dot(a, b, (((1,), (1,)), ((), ())))


def _dot_tn(a, b):
    return _dot(a, b, (((0,), (0,)), ((), ())))


def _seg_dot(xb, bd):
    n = xb.shape[1]
    if n <= 256:
        return jnp.dot(xb, bd[:n, :n], preferred_element_type=F32)
    parts = [jnp.dot(xb[:, c:c + 256], bd, preferred_element_type=F32) for c in range(0, n, 256)]
    return jnp.concatenate(parts, axis=1)


def _split3(x):
    hi = x.astype(MXU_DTYPE)
    r1 = x - hi.astype(F32)
    mid = r1.astype(MXU_DTYPE)
    lo = (r1 - mid.astype(F32)).astype(MXU_DTYPE)
    return hi, mid, lo


def _segsum_raw(x, bd):
    hi, mid, lo = _split3(x)
    return _seg_dot(hi, bd) + _seg_dot(mid, bd) + _seg_dot(lo, bd)


@jax.custom_vjp
def _segsum_d(x, bd):
    return _segsum_raw(x, bd)


def _segsum_d_fwd(x, bd):
    return _segsum_raw(x, bd), bd


def _segsum_d_bwd(bd, ct):
    return _segsum_raw(ct, bd), jnp.zeros_like(bd)


_segsum_d.defvjp(_segsum_d_fwd, _segsum_d_bwd)


def _rope_tables(T):
    t = jnp.arange(T, dtype=F32)
    row = jnp.floor(t / GRID_W)
    col = t - row * GRID_W
    n_freq = HEAD_DIM // 4
    inv_freq = ROPE_THETA ** (-jnp.arange(n_freq, dtype=F32) / n_freq)
    d = jnp.arange(HEAD_DIM)
    pos = jnp.where((d < HEAD_DIM // 2)[None, :], row[:, None], col[:, None])
    ang = pos * inv_freq[d % n_freq][None, :]
    sign = jnp.where((d % 32) < 16, -1.0, 1.0).astype(F32)[None, :]
    cos = jnp.cos(ang)
    sin = jnp.sin(ang) * sign
    return jnp.tile(cos, (1, 2)), jnp.tile(sin, (1, 2))


def _rope_raw(x, cos, sin):
    n = x.shape[1]
    lane = lax.broadcasted_iota(jnp.int32, (1, n), 1)
    first = (lane % 32) < 16
    partner = jnp.where(first, pltpu.roll(x, n - 16, 1), pltpu.roll(x, 16, 1))
    return x * cos + partner * sin


@jax.custom_vjp
def _rope_d(x, cos, sin):
    return _rope_raw(x, cos, sin)


def _rope_d_fwd(x, cos, sin):
    return _rope_raw(x, cos, sin), (cos, sin)


def _rope_d_bwd(res, ct):
    cos, sin = res
    return _rope_raw(ct, cos, -sin), jnp.zeros_like(cos), jnp.zeros_like(sin)


_rope_d.defvjp(_rope_d_fwd, _rope_d_bwd)


def _rms(x, g):
    return x * lax.rsqrt(jnp.mean(x * x, axis=-1, keepdims=True) + NORM_EPS) * g


def _pre_fn(x, shift, scale, g_pre):
    return _rms(x, g_pre) * (1.0 + scale) + shift


def _qk_fn(q, g, cos, sin, bd, scale, diff):
    segsum = _segsum_d if diff else _segsum_raw
    rope = _rope_d if diff else _rope_raw
    qn = q * lax.rsqrt(segsum(q * q, bd) * (1.0 / HEAD_DIM) + NORM_EPS) * g
    return rope(qn, cos, sin) * scale


def _silu(x):
    return x * jax.nn.sigmoid(x)


def _rwkv_pw(k, pw0, pw1, pa0, pa1, w0, a0, k_k, k_a, bd, diff):
    segsum = _segsum_d if diff else _segsum_raw
    kk = k * k_k
    kk = kk * lax.rsqrt(segsum(kk * kk, bd) + L2_EPS)
    ws, kts, akks = [], [], []
    for z, (pw, pa) in enumerate(((pw0, pa0), (pw1, pa1))):
        w = jnp.exp(-DECAY_SCALE * jax.nn.sigmoid(w0[z:z + 1, :] + pw))
        a = jax.nn.sigmoid(a0[z:z + 1, :] + pa)
        ws.append(w)
        kts.append(k * (1.0 + (a - 1.0) * k_a))
        akks.append(a * kk)
    return ws[0], ws[1], kts[0], kts[1], akks[0], akks[1], kk


def _mix_fn(y_att, g_att, ys, r, v, kts, g_rw, gn_w, gn_b, r_k, bd, diff):
    segsum = _segsum_d if diff else _segsum_raw
    mu = segsum(ys, bd) * (1.0 / HEAD_DIM)
    d = ys - mu
    var = segsum(d * d, bd) * (1.0 / HEAD_DIM)
    yn = d * lax.rsqrt(var + GN_EPS) * gn_w + gn_b
    bonus = segsum(r * kts * r_k, bd) * v
    return y_att * _silu(g_att), (yn + bonus) * _silu(g_rw)


def _loss_fn(out, x, tgt, gate, g_post):
    e = x + gate * _rms(out, g_post) - tgt
    s = jnp.sum(e * e, axis=1, keepdims=True)
    return jnp.sum(s, axis=0, keepdims=True) * (0.5 / D_MODEL)


def _exchange(arrays, scatter, name):
    n = len(arrays)
    out_shape = tuple(
        jax.ShapeDtypeStruct((NDEV,) + tuple(a.shape[1:] if sc else a.shape), a.dtype)
        for a, sc in zip(arrays, scatter))
    chips = (4, 2, 6)

    def body(*refs):
        ins, outs = refs[:n], refs[n:2 * n]
        send_sems, recv_sems, local_sems = refs[2 * n:]
        ix, iy, ic = lax.axis_index("x"), lax.axis_index("y"), lax.axis_index("c")
        me = 4 * ix + 2 * iy + ic

        def peer(m):
            px = 1 - ix if (m >> 2) & 1 else ix
            py = 1 - iy if (m >> 1) & 1 else iy
            pc = 1 - ic if m & 1 else ic
            return (px, py, pc), 4 * px + 2 * py + pc

        def copy(k, j, src_ref, slot, to):
            return pltpu.make_async_remote_copy(src_ref=src_ref, dst_ref=outs[k].at[slot], send_sem=send_sems.at[k, j],
                                                recv_sem=recv_sems.at[k, j], device_id=to, device_id_type=MESH)

        local = [pltpu.make_async_copy(ins[k].at[me] if scatter[k] else ins[k], outs[k].at[me], local_sems.at[k])
                 for k in range(n)]
        for cp in local:
            cp.start()
        sends, arrivals, forwards = [], [], []
        for k in range(n):
            if scatter[k]:
                for m in range(1, NDEV):
                    to, p = peer(m)
                    sends.append(copy(k, m - 1, ins[k].at[p], me, to))
                    arrivals.append(copy(k, m - 1, ins[k].at[p], p, to))
            else:
                sib, sib_slot = peer(1)
                sends.append(copy(k, 0, ins[k], me, sib))
                for j, m in enumerate(chips):
                    to, p = peer(m)
                    sends.append(copy(k, 1 + j, ins[k], me, to))
                    forwards.append((copy(k, 1 + j, ins[k], p, to), copy(k, 4 + j, outs[k].at[p], p, sib)))
                    arrivals.append(copy(k, 4 + j, ins[k], peer(m ^ 1)[1], sib))
                arrivals.append(copy(k, 0, ins[k], sib_slot, sib))
        for cp in sends:
            cp.start()
        for arrived, onward in forwards:
            arrived.wait_recv()
            onward.start()
        for cp in arrivals:
            cp.wait_recv()
        for cp in sends + [onward for _, onward in forwards]:
            cp.wait_send()
        for cp in local:
            cp.wait()

    any_spec = pl.BlockSpec(memory_space=pl.ANY)
    return pl.pallas_call(
        body, name=name, out_shape=out_shape,
        in_specs=[any_spec] * n, out_specs=tuple([any_spec] * n),
        scratch_shapes=[pltpu.SemaphoreType.DMA((n, NDEV - 1)), pltpu.SemaphoreType.DMA((n, NDEV - 1)),
                        pltpu.SemaphoreType.DMA((n,))],
    )(*arrays)


def _mod_call(c_all, w_ada, b_cols):
    def body(c_ref, w_ref, b_ref, o_ref):
        o_ref[...] = _dot(_silu(c_ref[...]), w_ref[...]) + b_ref[...]

    return pl.pallas_call(body, name="mod_fwd",
                          out_shape=jax.ShapeDtypeStruct((c_all.shape[0], w_ada.shape[1]), F32))(c_all, w_ada, b_cols)


def _wada_grad_call(c_all, dmod_cols):
    def body(c_ref, d_ref, o_ref):
        o_ref[...] = _dot_tn(_silu(c_ref[...]), d_ref[...])

    return pl.pallas_call(body, name="w_ada_grad",
                          out_shape=jax.ShapeDtypeStruct((c_all.shape[1], dmod_cols.shape[1]), F32))(c_all, dmod_cols)


def _full(shape):
    nd = len(shape)
    return pl.BlockSpec(shape, lambda *_: (0,) * nd)


def _in_proj_call(x2, shift, scale, g_pre, w_in, qg, kg, cos, sin, bd, T):
    R = x2.shape[0]
    TT = min(ROW_TILE, T)
    tpe = T // TT

    def body(x_ref, sh_ref, sc_ref, gp_ref, w_ref, qg_ref, kg_ref, cos_ref, sin_ref, bd_ref,
             hb_ref, qr_ref, kpad_ref, vpad_ref, qraw_ref, kraw_ref, gatt_ref, rin_ref, grw_ref):
        h = _pre_fn(x_ref[...], sh_ref[0], sc_ref[0], gp_ref[...])
        hb = h.astype(MXU_DTYPE)
        hb_ref[...] = hb

        def proj(c0, c1):
            return jnp.dot(hb, w_ref[:, c0:c1], preferred_element_type=F32)

        q = proj(C_Q, C_K)
        k = proj(C_K, C_V)
        v = proj(C_V, C_GA)
        gatt_ref[...] = proj(C_GA, C_RIN)
        rin_ref[...] = proj(C_RIN, C_GRW)
        grw_ref[...] = proj(C_GRW, C_END)
        qraw_ref[...] = q
        kraw_ref[...] = k
        cos, sin, bd = cos_ref[...], sin_ref[...], bd_ref[...]
        qr = _qk_fn(q, qg_ref[...], jnp.tile(cos, (1, 4)), jnp.tile(sin, (1, 4)), bd, ATT_SCALE, False)
        qr_ref[...] = qr.astype(MXU_DTYPE)
        kr = _qk_fn(k, kg_ref[...], cos, sin, bd, 1.0, False)
        left = lax.broadcasted_iota(jnp.int32, (1, KV_W), 1) < HEAD_DIM
        for ref, val in ((kpad_ref, kr), (vpad_ref, v)):
            h0l = jnp.where(left, val, 0.0)
            h1r = jnp.where(left, 0.0, val)
            ref[0] = h0l.astype(MXU_DTYPE)
            ref[1] = pltpu.roll(h0l, HEAD_DIM, 1).astype(MXU_DTYPE)
            ref[2] = pltpu.roll(h1r, HEAD_DIM, 1).astype(MXU_DTYPE)
            ref[3] = h1r.astype(MXU_DTYPE)

    row = lambda w: pl.BlockSpec((TT, w), lambda i: (i, 0))
    per_ex = pl.BlockSpec((1, 1, D_MODEL), lambda i: (i // tpe, 0, 0))
    tab = pl.BlockSpec((TT, KV_W), lambda i: (i % tpe, 0))
    pad = pl.BlockSpec((4, TT, KV_W), lambda i: (0, i, 0))
    sds = jax.ShapeDtypeStruct
    return pl.pallas_call(
        body, name="in_proj", grid=(R // TT,),
        in_specs=[row(D_MODEL), per_ex, per_ex, _full((1, D_MODEL)), _full(w_in.shape), _full((1, ATT_W)),
                  _full((1, KV_W)), tab, tab, _full((256, 256))],
        out_specs=(row(D_MODEL), row(ATT_W), pad, pad, row(ATT_W), row(KV_W), row(ATT_W), row(SHIFT_W), row(RWKV_W)),
        out_shape=(sds((R, D_MODEL), MXU_DTYPE), sds((R, ATT_W), MXU_DTYPE), sds((4, R, KV_W), MXU_DTYPE),
                   sds((4, R, KV_W), MXU_DTYPE), sds((R, ATT_W), F32), sds((R, KV_W), F32), sds((R, ATT_W), F32),
                   sds((R, SHIFT_W), F32), sds((R, RWKV_W), F32)),
        compiler_params=_cp(("arbitrary",)),
    )(x2, shift, scale, g_pre, w_in, qg, kg, cos, sin, bd)


def _softmax_rows(s):
    m = jnp.max(s, axis=1, keepdims=True)
    e = jnp.exp(s - m)
    return e / jnp.sum(e, axis=1, keepdims=True)


def _att_specs(T, TQ):
    nq = T // TQ
    qspec = pl.BlockSpec((TQ, KV_W), lambda b, p, i: (b * nq + i, p))
    side = lambda s: pl.BlockSpec((None, T, KV_W), lambda b, p, i: (2 * (p // 2) + s, b, 0))
    return nq, qspec, side


def _att_fwd_call(qr, kpad, vpad, B, T):
    TQ = min(ROW_TILE, T)
    nq, qspec, side = _att_specs(T, TQ)

    def body(q_ref, kl_ref, kr_ref, vl_ref, vr_ref, o_ref):
        q = q_ref[...]
        pa = _softmax_rows(_dot_nt(q, kl_ref[...]))
        pb = _softmax_rows(_dot_nt(q, kr_ref[...]))
        o_ref[...] = _dot(pa, vl_ref[...]) + _dot(pb, vr_ref[...])

    return pl.pallas_call(
        body, name="att_fwd", grid=(B, 4, nq),
        in_specs=[qspec, side(0), side(1), side(0), side(1)], out_specs=qspec,
        out_shape=jax.ShapeDtypeStruct((B * T, ATT_W), F32),
        compiler_params=_cp(("arbitrary",) * 3),
    )(qr, kpad, kpad, vpad, vpad)


def _att_bwd_call(qr, kpad, vpad, d_o, B, T):
    TQ = min(ROW_TILE, T)
    nq, qspec, side = _att_specs(T, TQ)

    def body(q_ref, kl_ref, kr_ref, vl_ref, vr_ref, do_ref, dq_ref, dk_ref, dv_ref):
        i = pl.program_id(2)
        q, do = q_ref[...], do_ref[...]
        left = lax.broadcasted_iota(jnp.int32, (1, KV_W), 1) < HEAD_DIM
        dq = jnp.zeros((TQ, KV_W), F32)
        dk = jnp.zeros((T, KV_W), F32)
        dv = jnp.zeros((T, KV_W), F32)
        for k_ref, v_ref, mask in ((kl_ref, vl_ref, left), (kr_ref, vr_ref, jnp.logical_not(left))):
            kk, vv = k_ref[...], v_ref[...]
            p = _softmax_rows(_dot_nt(q, kk))
            dp = _dot_nt(do, vv)
            ds = p * (dp - jnp.sum(p * dp, axis=1, keepdims=True))
            dq = dq + _dot(ds, kk)
            dk = dk + _dot_tn(ds, jnp.where(mask, q, jnp.zeros_like(q)))
            dv = dv + _dot_tn(p, jnp.where(mask, do, 0.0))
        dq_ref[...] = dq

        @pl.when(i == 0)
        def _():
            dk_ref[...] = dk
            dv_ref[...] = dv

        @pl.when(i > 0)
        def _():
            dk_ref[...] += dk
            dv_ref[...] += dv

    acc = pl.BlockSpec((None, T, KV_W), lambda b, p, i: (p, b, 0))
    sds = jax.ShapeDtypeStruct
    return pl.pallas_call(
        body, name="att_bwd", grid=(B, 4, nq),
        in_specs=[qspec, side(0), side(1), side(0), side(1), qspec], out_specs=(qspec, acc, acc),
        out_shape=(sds((B * T, ATT_W), F32), sds((4, B * T, KV_W), F32), sds((4, B * T, KV_W), F32)),
        compiler_params=_cp(("arbitrary",) * 3),
    )(qr, kpad, kpad, vpad, vpad, d_o)


def _shift_specs(R, T, TT, width):
    tpe = T // TT
    nb8 = R // 8
    cur = pl.BlockSpec((TT, width), lambda i: (i, 0))
    prev = pl.BlockSpec((8, width), lambda i: (jnp.maximum(i * (TT // 8) - 1, 0), 0))
    nxt = pl.BlockSpec((8, width), lambda i: (jnp.minimum((i + 1) * (TT // 8), nb8 - 1), 0))
    return tpe, cur, prev, nxt


def _neighbours(cur, prev8, next8, i, tpe, TT):
    rows = lax.broadcasted_iota(jnp.int32, (TT, 1), 0)
    first = jnp.where(i % tpe == 0, 0.0, 1.0)
    last = jnp.where(i % tpe == tpe - 1, 0.0, 1.0)
    before = jnp.where(rows == 0, prev8[7:8, :] * first, pltpu.roll(cur, 1, 0))
    after = jnp.where(rows == TT - 1, next8[0:1, :] * last, pltpu.roll(cur, TT - 1, 0))
    return before, after


def _shift_fwd_call(x, taps, T):
    R, width = x.shape
    TT = min(ROW_TILE, T)
    tpe, cur, prev, nxt = _shift_specs(R, T, TT, width)

    def body(x_ref, p_ref, n_ref, t_ref, o_ref):
        xc = x_ref[...]
        before, after = _neighbours(xc, p_ref[...], n_ref[...], pl.program_id(0), tpe, TT)
        o_ref[...] = t_ref[0:1, :] * before + t_ref[1:2, :] * xc + t_ref[2:3, :] * after

    return pl.pallas_call(
        body, name="shift_fwd", grid=(R // TT,), in_specs=[cur, prev, nxt, _full(taps.shape)], out_specs=cur,
        out_shape=jax.ShapeDtypeStruct((R, width), F32), compiler_params=_cp(("arbitrary",)),
    )(x, x, x, taps)


def _shift_bwd_call(x, d, taps, T):
    R, width = x.shape
    TT = min(ROW_TILE, T)
    tpe, cur, prev, nxt = _shift_specs(R, T, TT, width)

    def body(x_ref, xp_ref, xn_ref, d_ref, dp_ref, dn_ref, t_ref, dx_ref, dt_ref):
        i = pl.program_id(0)
        xc, dc = x_ref[...], d_ref[...]
        d_before, d_after = _neighbours(dc, dp_ref[...], dn_ref[...], i, tpe, TT)
        dx_ref[...] = t_ref[2:3, :] * d_before + t_ref[1:2, :] * dc + t_ref[0:1, :] * d_after
        x_before, x_after = _neighbours(xc, xp_ref[...], xn_ref[...], i, tpe, TT)
        @pl.when(i == 0)
        def _():
            dt_ref[...] = jnp.zeros_like(dt_ref)

        for j, xs in enumerate((x_before, xc, x_after)):
            dt_ref[j:j + 1, :] += jnp.sum(dc * xs, axis=0, keepdims=True)

    return pl.pallas_call(
        body, name="shift_bwd", grid=(R // TT,),
        in_specs=[cur, prev, nxt, cur, prev, nxt, _full(taps.shape)], out_specs=(cur, _full((8, width))),
        out_shape=(jax.ShapeDtypeStruct((R, width), F32), jax.ShapeDtypeStruct((8, width), F32)),
        compiler_params=_cp(("arbitrary",)),
    )(x, x, x, d, d, d, taps)


def _lora_in(wa):
    lane = lax.broadcasted_iota(jnp.int32, (1, LORA_W), 1)
    return jnp.where(lane < LORA_W // 2, jnp.tanh(wa), wa)


def _rwkv_prep_call(shifted, wup, aup, w0, a0, k_k, k_a, bd, T):
    R = shifted.shape[0]
    TT = min(ROW_TILE, T)

    def body(k_ref, wa_ref, wup_ref, aup_ref, w0_ref, a0_ref, kk_ref, ka_ref, bd_ref, w_o, kt_o, akk_o, kk_o):
        twa = _lora_in(wa_ref[...])
        pre = [_dot(twa, m_ref[z]) for m_ref in (wup_ref, aup_ref) for z in range(2)]
        outs = _rwkv_pw(k_ref[...], pre[0], pre[1], pre[2], pre[3], w0_ref[...], a0_ref[...], kk_ref[...],
                        ka_ref[...], bd_ref[...], False)
        w_o[0], w_o[1], kt_o[0], kt_o[1], akk_o[0], akk_o[1] = outs[:6]
        kk_o[...] = outs[6]

    col = lambda c, w: pl.BlockSpec((TT, w), lambda i: (i, c))
    two = pl.BlockSpec((2, TT, RWKV_W), lambda i: (0, i, 0))
    sds = jax.ShapeDtypeStruct
    return pl.pallas_call(
        body, name="rwkv_prep", grid=(R // TT,),
        in_specs=[col(1, RWKV_W), col(3 * RWKV_W // LORA_W, LORA_W), _full(wup.shape), _full(aup.shape),
                  _full((2, RWKV_W)), _full((2, RWKV_W)), _full((1, RWKV_W)), _full((1, RWKV_W)), _full((256, 256))],
        out_specs=(two, two, two, col(0, RWKV_W)),
        out_shape=(sds((2, R, RWKV_W), F32),) * 3 + (sds((R, RWKV_W), F32),),
        compiler_params=_cp(("arbitrary",)),
    )(shifted, shifted, wup, aup, w0, a0, k_k, k_a, bd)


def _rwkv_prep_bwd_call(shifted, cts, wup, aup, w0, a0, k_k, k_a, bd, T):
    R = shifted.shape[0]
    TT = min(ROW_TILE, T)

    def body(k_ref, wa_ref, dw0, dkt0, dakk0, dkk0, dr0, dv0, dw1, dkt1, dakk1, dkk1, dr1, dv1, dr2_ref, dv2_ref, dkts_ref,
             wup_ref, aup_ref, w0_ref, a0_ref, kk_ref, ka_ref, bd_ref,
             dsh_ref, gwup_ref, gaup_ref, gw0_ref, ga0_ref, gkk_ref, gka_ref):
        dw_ref, dkt_ref, dakk_ref, dkk_ref, dr_ref, dv_ref = ((dw0, dw1), (dkt0, dkt1), (dakk0, dakk1), (dkk0, dkk1),
                                                              (dr0, dr1), (dv0, dv1))
        i = pl.program_id(0)
        wa = wa_ref[...]
        twa = _lora_in(wa)
        pre = [_dot(twa, m_ref[z]) for m_ref in (wup_ref, aup_ref) for z in range(2)]
        fn = functools.partial(_rwkv_pw, bd=bd_ref[...], diff=True)
        _, vjp = jax.vjp(fn, k_ref[...], pre[0], pre[1], pre[2], pre[3], w0_ref[...], a0_ref[...], kk_ref[...],
                         ka_ref[...])
        dkts = dkts_ref[...]
        dk, dpw0, dpw1, dpa0, dpa1, gw0, ga0, gkk, gka = vjp(
            (dw_ref[0][...], dw_ref[1][...], dkt_ref[0][...] + dkts, dkt_ref[1][...] + dkts, dakk_ref[0][...],
             dakk_ref[1][...], dkk_ref[0][...] + dkk_ref[1][...]))
        dtwa = (_dot_nt(dpw0, wup_ref[0]) + _dot_nt(dpw1, wup_ref[1]) + _dot_nt(dpa0, aup_ref[0])
                + _dot_nt(dpa1, aup_ref[1]))
        lane = lax.broadcasted_iota(jnp.int32, (1, LORA_W), 1)
        dsh_ref[:, 0:RWKV_W] = dr_ref[0][...] + dr_ref[1][...] + dr2_ref[...]
        dsh_ref[:, RWKV_W:2 * RWKV_W] = dk
        dsh_ref[:, 2 * RWKV_W:3 * RWKV_W] = dv_ref[0][...] + dv_ref[1][...] + dv2_ref[...]
        dsh_ref[:, 3 * RWKV_W:] = jnp.where(lane < LORA_W // 2, dtwa * (1.0 - twa * twa), dtwa)
        acc = ((gwup_ref.at[0], _dot_tn(twa, dpw0)), (gwup_ref.at[1], _dot_tn(twa, dpw1)),
               (gaup_ref.at[0], _dot_tn(twa, dpa0)), (gaup_ref.at[1], _dot_tn(twa, dpa1)),
               (gw0_ref, gw0), (ga0_ref, ga0), (gkk_ref, gkk), (gka_ref, gka))

        @pl.when(i == 0)
        def _():
            for ref, val in acc:
                ref[...] = val

        @pl.when(i > 0)
        def _():
            for ref, val in acc:
                ref[...] += val

    col = lambda c, w: pl.BlockSpec((TT, w), lambda i: (i, c))
    one = col(0, RWKV_W)
    sds = jax.ShapeDtypeStruct
    return pl.pallas_call(
        body, name="rwkv_prep_bwd", grid=(R // TT,),
        in_specs=[col(1, RWKV_W), col(3 * RWKV_W // LORA_W, LORA_W)] + [one] * 15 + [
                  _full(wup.shape), _full(aup.shape), _full((2, RWKV_W)), _full((2, RWKV_W)), _full((1, RWKV_W)),
                  _full((1, RWKV_W)), _full((256, 256))],
        out_specs=(pl.BlockSpec((TT, SHIFT_W), lambda i: (i, 0)), _full(wup.shape), _full(aup.shape),
                   _full((2, RWKV_W)), _full((2, RWKV_W)), _full((1, RWKV_W)), _full((1, RWKV_W))),
        out_shape=(sds((R, SHIFT_W), F32), sds(wup.shape, F32), sds(aup.shape, F32), sds((2, RWKV_W), F32),
                   sds((2, RWKV_W), F32), sds((1, RWKV_W), F32), sds((1, RWKV_W), F32)),
        compiler_params=_cp(("arbitrary",)),
    )(shifted, shifted, *cts, wup, aup, w0, a0, k_k, k_a, bd)


def _col_lhs(row, eye_b):
    return eye_b * row.astype(MXU_DTYPE)


def _colsum(x):
    return jnp.sum(x, axis=0, keepdims=True)


def _stacked_segsum(tiles, bd):
    res = _seg_dot(jnp.concatenate(tiles, axis=0), bd)
    return [res[j * HEAD_DIM:(j + 1) * HEAD_DIM] for j in range(len(tiles))]


def _scan_specs(B, T, C, nC):
    def blk(z, col, rev):
        idx = (lambda g: (z, 0, nC - 1 - g, col)) if rev else (lambda g: (z, 0, g, col))
        return pl.BlockSpec((None, B, C, RWKV_W), idx)

    def blk3(col, rev):
        idx = (lambda g: (0, nC - 1 - g, col)) if rev else (lambda g: (0, g, col))
        return pl.BlockSpec((B, C, RWKV_W), idx)

    return blk, blk3


def _scan_fwd_call(w, kt, akk, kk, shifted, eye_b, eye_f, bd, B, T):
    C = min(SCAN_CHUNK, T)
    nC = T // C
    blk, blk3 = _scan_specs(B, T, C, nC)

    def body(w0, kt0, akk0, kk0, v0, r0, w1, kt1, akk1, kk1, v1, r1, eb_ref, ef_ref, bd_ref, y0, y1, st, S):
        @pl.when(pl.program_id(0) == 0)
        def _():
            S[...] = jnp.zeros_like(S)

        st[0] = S[...].astype(MXU_DTYPE)
        dirs = ((w0, kt0, akk0, kk0, v0, r0, y0), (w1, kt1, akk1, kk1, v1, r1, y1))

        def step(s, carry):
            for z in range(2):
                row = s if z == 0 else C - 1 - s
                prev = jnp.maximum(s - 1, 0) if z == 0 else jnp.minimum(C - s, C - 1)
                wr, ktr, akkr, kkr, vr, rr, yr = dirs[z]
                tiles = []
                for b in range(B):
                    Sb = st[s, z * B + b]
                    tiles += [Sb * kkr[b, pl.ds(row, 1), :].astype(MXU_DTYPE),
                              _col_lhs(vr[b, pl.ds(row, 1), :], eb_ref[...]),
                              Sb * rr[b, pl.ds(prev, 1), :].astype(MXU_DTYPE)]
                res = _stacked_segsum(tiles, bd_ref[...])
                for b in range(B):
                    c = z * B + b
                    sab, vb, yb = res[3 * b:3 * b + 3]
                    ld = lambda ref: ref[b, pl.ds(row, 1), :]
                    Sn = S[c] * ld(wr) - sab * ld(akkr) + vb * ld(ktr)
                    S[c] = Sn
                    st[s + 1, c] = Sn.astype(MXU_DTYPE)
                    yr[b, pl.ds(prev, 1), :] = _colsum(ef_ref[...] * yb)
            return carry

        lax.fori_loop(0, C, step, 0, unroll=SCAN_UNROLL)
        for z in range(2):
            last = C - 1 if z == 0 else 0
            rr, yr = dirs[z][5], dirs[z][6]
            res = _stacked_segsum([st[C, z * B + b] * rr[b, last:last + 1, :].astype(MXU_DTYPE) for b in range(B)],
                                  bd_ref[...])
            for b in range(B):
                yr[b, last:last + 1, :] = _colsum(ef_ref[...] * res[b])

    ins, specs = [], []
    for z, rev in ((0, False), (1, True)):
        ins += [w, kt, akk, kk, shifted, shifted]
        specs += [blk(z, 0, rev), blk(z, 0, rev), blk(z, 0, rev), blk3(0, rev), blk3(2, rev), blk3(0, rev)]
    sds = jax.ShapeDtypeStruct
    return pl.pallas_call(
        body, name="scan_fwd", grid=(nC,),
        in_specs=specs + [_full((HEAD_DIM, RWKV_W)), _full((HEAD_DIM, RWKV_W)), _full((256, 256))],
        out_specs=(blk3(0, False), blk3(0, True),
                   pl.BlockSpec((None, C + 1, 2 * B, HEAD_DIM, RWKV_W), lambda g: (g, 0, 0, 0, 0))),
        out_shape=(sds((B, T, RWKV_W), F32), sds((B, T, RWKV_W), F32),
                   sds((nC, C + 1, 2 * B, HEAD_DIM, RWKV_W), MXU_DTYPE)),
        scratch_shapes=[pltpu.VMEM((2 * B, HEAD_DIM, RWKV_W), F32)],
        compiler_params=_cp(("arbitrary",)),
    )(*ins, eye_b, eye_f, bd)


def _scan_bwd_call(w, kt, akk, kk, shifted, dys, st, eye_b, eye_f, bd, B, T):
    C = min(SCAN_CHUNK, T)
    nC = T // C
    blk, blk3 = _scan_specs(B, T, C, nC)
    nin = 7

    def body(*refs):
        d0, d1 = refs[:nin], refs[nin:2 * nin]
        st_ref, eb_ref, ef_ref, sel_ref, bd_ref = refs[2 * nin:2 * nin + 5]
        o0, o1 = refs[2 * nin + 5:2 * nin + 11], refs[2 * nin + 11:2 * nin + 17]
        COL, DYC, G = refs[2 * nin + 17:]

        @pl.when(pl.program_id(0) == 0)
        def _():
            G[...] = jnp.zeros_like(G)

        dirs = (d0 + (o0,), d1 + (o1,))

        def column_operands(s, z):
            row = s if z == 0 else C - 1 - s
            _, _, _, kkr, vr, _, dyr, _ = dirs[z]
            tiles = []
            for b in range(B):
                tiles += [st_ref[s, z * B + b] * kkr[b, pl.ds(row, 1), :].astype(MXU_DTYPE),
                          _col_lhs(vr[b, pl.ds(row, 1), :], eb_ref[...]),
                          _col_lhs(dyr[b, pl.ds(row, 1), :], eb_ref[...])]
            return tiles

        def keep_columns(res, z):
            for b in range(B):
                for k in range(3):
                    COL[k, z * B + b] = res[3 * b + k].astype(MXU_DTYPE)
                DYC[z * B + b] = res[3 * b + 2]

        for z in range(2):
            keep_columns(_stacked_segsum(column_operands(C - 1, z), bd_ref[...]), z)

        def bwd(it, carry):
            s = C - 1 - it
            for z in range(2):
                row = s if z == 0 else C - 1 - s
                wr, ktr, akkr, kkr, vr, rr, dyr, (dw_o, dkt_o, dakk_o, dkk_o, dr_o, dv_o) = dirs[z]
                tiles, Gcs = [], []
                for b in range(B):
                    c = z * B + b
                    Gc = G[c] + DYC[c] * rr[b, pl.ds(row, 1), :]
                    Gb = Gc.astype(MXU_DTYPE)
                    Gcs.append((Gc, Gb))
                    tiles += [Gb * akkr[b, pl.ds(row, 1), :].astype(MXU_DTYPE),
                              Gb * ktr[b, pl.ds(row, 1), :].astype(MXU_DTYPE)]
                res = _stacked_segsum(tiles + column_operands(jnp.maximum(s - 1, 0), z), bd_ref[...])
                for b in range(B):
                    c = z * B + b
                    Gc, Gb = Gcs[b]
                    gab, dvb = res[2 * b], res[2 * b + 1]
                    ld = lambda ref: ref[b, pl.ds(row, 1), :]
                    G[c] = Gc * ld(wr) - gab * ld(kkr)
                    Sb = st_ref[s, c]
                    prods = jnp.concatenate([st_ref[s + 1, c] * COL[2, c], Gb * COL[1, c], Gb * Sb, Gb * COL[0, c]], axis=0)
                    sums = jnp.dot(sel_ref[...], prods, preferred_element_type=F32)
                    for k, (ref, sign) in enumerate(((dr_o, 1.0), (dkt_o, 1.0), (dw_o, 1.0), (dakk_o, -1.0))):
                        ref[b, pl.ds(row, 1), :] = sign * sums[k:k + 1, :]
                    dv_o[b, pl.ds(row, 1), :] = _colsum(ef_ref[...] * dvb)
                    dkk_o[b, pl.ds(row, 1), :] = -_colsum(gab * Sb.astype(F32))
                keep_columns(res[2 * B:], z)
            return carry

        lax.fori_loop(0, C, bwd, 0, unroll=SCAN_UNROLL)

    ins, specs = [], []
    for z, rev in ((0, True), (1, False)):
        ins += [w, kt, akk, kk, shifted, shifted, dys]
        specs += [blk(z, 0, rev), blk(z, 0, rev), blk(z, 0, rev), blk3(0, rev), blk3(2, rev), blk3(0, rev), blk3(0, rev)]
    sel = (jnp.arange(16)[:, None] == (jnp.arange(4 * HEAD_DIM) // HEAD_DIM)[None, :]).astype(MXU_DTYPE)
    ins += [st, eye_b, eye_f, sel, bd]
    specs += [pl.BlockSpec((None, C + 1, 2 * B, HEAD_DIM, RWKV_W), lambda g: (nC - 1 - g, 0, 0, 0, 0)),
              _full((HEAD_DIM, RWKV_W)), _full((HEAD_DIM, RWKV_W)), _full(sel.shape), _full((256, 256))]
    sds = jax.ShapeDtypeStruct
    out_specs = tuple(blk3(0, True) for _ in range(6)) + tuple(blk3(0, False) for _ in range(6))
    res = pl.pallas_call(
        body, name="scan_bwd", grid=(nC,), in_specs=specs, out_specs=out_specs,
        out_shape=tuple(sds((B, T, RWKV_W), F32) for _ in range(12)),
        scratch_shapes=[pltpu.VMEM((3, 2 * B, HEAD_DIM, RWKV_W), MXU_DTYPE), pltpu.VMEM((2 * B, HEAD_DIM, RWKV_W), F32),
                        pltpu.VMEM((2 * B, HEAD_DIM, RWKV_W), F32)],
        compiler_params=_cp(("arbitrary",)),
    )(*ins)
    return list(res)


def _out_head_call(x2, tgt2, gate, y_att, g_att, y0, y1, shifted, kt, g_rw, w_out, g_post, gn_w, gn_b, r_k, bd, T):
    R = x2.shape[0]
    TT = min(ROW_TILE, T)
    tpe = T // TT

    def body(x_ref, t_ref, gate_ref, ya_ref, ga_ref, y0_ref, y1_ref, r_ref, v_ref, kt_ref, grw_ref, w_ref, gp_ref,
             gnw_ref, gnb_ref, rk_ref, bd_ref,
             loss_o, dy_o, dya_o, dga_o, dys_o, dr_o, dv_o, dkts_o, dgrw_o, dgate_o, gw_o, ggp_o, ggnw_o, ggnb_o, grk_o):
        i = pl.program_id(0)
        bd = bd_ref[...]
        mix = functools.partial(_mix_fn, bd=bd, diff=True)
        (ma, mr), mix_vjp = jax.vjp(mix, ya_ref[...], ga_ref[...], y0_ref[...] + y1_ref[...], r_ref[...], v_ref[...],
                                    kt_ref[0] + kt_ref[1], grw_ref[...], gnw_ref[...], gnb_ref[...], rk_ref[...])
        out = _dot(ma, w_ref[0:ATT_W, :]) + _dot(mr, w_ref[ATT_W:, :])
        loss, loss_vjp = jax.vjp(_loss_fn, out, x_ref[...], t_ref[...], gate_ref[0], gp_ref[...])
        d_out, dy, _, dgate, dgp = loss_vjp(jnp.ones((1, 1), F32))
        dy_o[...] = dy
        dma = _dot_nt(d_out, w_ref[0:ATT_W, :])
        dmr = _dot_nt(d_out, w_ref[ATT_W:, :])
        dya_o[...], dga_o[...], dys_o[...], dr_o[...], dv_o[...], dkts_o[...], dgrw_o[...], dgnw, dgnb, drk = \
            mix_vjp((dma, dmr))
        gw = jnp.concatenate([_dot_tn(ma, d_out), _dot_tn(mr, d_out)], axis=0)
        acc = ((loss_o, jnp.broadcast_to(loss, (8, 128))), (gw_o, gw), (ggp_o, dgp), (ggnw_o, dgnw), (ggnb_o, dgnb),
               (grk_o, drk))

        @pl.when(i == 0)
        def _():
            for ref, val in acc:
                ref[...] = val

        @pl.when(i > 0)
        def _():
            for ref, val in acc:
                ref[...] += val

        @pl.when(i % tpe == 0)
        def _():
            dgate_o[0] = dgate

        @pl.when(i % tpe > 0)
        def _():
            dgate_o[0] += dgate

    row = lambda w, c=0: pl.BlockSpec((TT, w), lambda i: (i, c))
    two = pl.BlockSpec((2, TT, RWKV_W), lambda i: (0, i, 0))
    per_ex = pl.BlockSpec((1, 1, D_MODEL), lambda i: (i // tpe, 0, 0))
    sds = jax.ShapeDtypeStruct
    r512 = sds((R, RWKV_W), F32)
    return pl.pallas_call(
        body, name="out_head", grid=(R // TT,),
        in_specs=[row(D_MODEL), row(D_MODEL), per_ex, row(ATT_W), row(ATT_W), row(RWKV_W), row(RWKV_W), row(RWKV_W, 0),
                  row(RWKV_W, 2), two,
                  row(RWKV_W), _full(w_out.shape), _full((1, D_MODEL)), _full((1, RWKV_W)), _full((1, RWKV_W)),
                  _full((1, RWKV_W)), _full((256, 256))],
        out_specs=(_full((8, 128)), row(D_MODEL), row(ATT_W), row(ATT_W), row(RWKV_W), row(RWKV_W), row(RWKV_W),
                   row(RWKV_W), row(RWKV_W), per_ex, _full((D_MODEL, D_MODEL)), _full((1, D_MODEL)), _full((1, RWKV_W)),
                   _full((1, RWKV_W)), _full((1, RWKV_W))),
        out_shape=(sds((8, 128), F32), sds((R, D_MODEL), F32), r512, r512, r512, r512, r512, r512, r512,
                   sds((R // T, 1, D_MODEL), F32), sds((D_MODEL, D_MODEL), F32), sds((1, D_MODEL), F32),
                   sds((1, RWKV_W), F32), sds((1, RWKV_W), F32), sds((1, RWKV_W), F32)),
        compiler_params=_cp(("arbitrary",)),
    )(x2, tgt2, gate, y_att, g_att, y0, y1, shifted, shifted, kt, g_rw, w_out, g_post, gn_w, gn_b, r_k, bd)


def _in_proj_bwd_call(x2, dy, shift, scale, g_pre, w_in, qg, kg, cos, sin, bd, q_raw, k_raw, dqr, dkp, dvp,
                      d_gatt, d_rin, d_grw, T):
    R = x2.shape[0]
    TT = min(ROW_TILE, T)
    tpe = T // TT

    def body(x_ref, dy_ref, sh_ref, sc_ref, gp_ref, w_ref, qg_ref, kg_ref, cos_ref, sin_ref, bd_ref, q_ref, k_ref,
             dqr_ref, dkp_ref, dvp_ref, dga_ref, drin_ref, dgrw_ref,
             dx_o, dproj_o, dsh_o, dsc_o, ggp_o, gqg_o, gkg_o):
        i = pl.program_id(0)
        cos, sin, bd = cos_ref[...], sin_ref[...], bd_ref[...]
        left = lax.broadcasted_iota(jnp.int32, (1, KV_W), 1) < HEAD_DIM

        def kv_grad(ref):
            a = ref[0] + ref[1]
            b = ref[2] + ref[3]
            return jnp.where(left, a + pltpu.roll(a, HEAD_DIM, 1), b + pltpu.roll(b, HEAD_DIM, 1))

        qfn = functools.partial(_qk_fn, cos=jnp.tile(cos, (1, 4)), sin=jnp.tile(sin, (1, 4)), bd=bd, scale=ATT_SCALE,
                                diff=True)
        _, q_vjp = jax.vjp(qfn, q_ref[...], qg_ref[...])
        dq, gqg = q_vjp(dqr_ref[...])
        kfn = functools.partial(_qk_fn, cos=cos, sin=sin, bd=bd, scale=1.0, diff=True)
        _, k_vjp = jax.vjp(kfn, k_ref[...], kg_ref[...])
        dk, gkg = k_vjp(kv_grad(dkp_ref))
        pieces = ((C_Q, C_K, dq), (C_K, C_V, dk), (C_V, C_GA, kv_grad(dvp_ref)), (C_GA, C_RIN, dga_ref[...]),
                  (C_RIN, C_GRW, drin_ref[...]), (C_GRW, C_END, dgrw_ref[...]))
        dh = jnp.zeros((TT, D_MODEL), F32)
        for c0, c1, val in pieces:
            vb = val.astype(MXU_DTYPE)
            dproj_o[:, c0:c1] = vb
            dh = dh + _dot_nt(vb, w_ref[:, c0:c1])
        _, pre_vjp = jax.vjp(_pre_fn, x_ref[...], sh_ref[0], sc_ref[0], gp_ref[...])
        dx, dsh, dsc, ggp = pre_vjp(dh)
        dx_o[...] = dx + dy_ref[...]
        acc = ((ggp_o, ggp), (gqg_o, gqg), (gkg_o, gkg))

        @pl.when(i == 0)
        def _():
            for ref, val in acc:
                ref[...] = val

        @pl.when(i > 0)
        def _():
            for ref, val in acc:
                ref[...] += val

        @pl.when(i % tpe == 0)
        def _():
            dsh_o[0] = dsh
            dsc_o[0] = dsc

        @pl.when(i % tpe > 0)
        def _():
            dsh_o[0] += dsh
            dsc_o[0] += dsc

    row = lambda w: pl.BlockSpec((TT, w), lambda i: (i, 0))
    per_ex = pl.BlockSpec((1, 1, D_MODEL), lambda i: (i // tpe, 0, 0))
    tab = pl.BlockSpec((TT, KV_W), lambda i: (i % tpe, 0))
    pad = pl.BlockSpec((4, TT, KV_W), lambda i: (0, i, 0))
    sds = jax.ShapeDtypeStruct
    nb = R // T
    return pl.pallas_call(
        body, name="in_proj_bwd", grid=(R // TT,),
        in_specs=[row(D_MODEL), row(D_MODEL), per_ex, per_ex, _full((1, D_MODEL)), _full(w_in.shape), _full((1, ATT_W)),
                  _full((1, KV_W)), tab, tab, _full((256, 256)), row(ATT_W), row(KV_W), row(ATT_W), pad, pad,
                  row(ATT_W), row(SHIFT_W), row(RWKV_W)],
        out_specs=(row(D_MODEL), row(C_END), per_ex, per_ex, _full((1, D_MODEL)), _full((1, ATT_W)), _full((1, KV_W))),
        out_shape=(sds((R, D_MODEL), F32), sds((R, C_END), MXU_DTYPE), sds((nb, 1, D_MODEL), F32),
                   sds((nb, 1, D_MODEL), F32), sds((1, D_MODEL), F32), sds((1, ATT_W), F32), sds((1, KV_W), F32)),
        compiler_params=_cp(("arbitrary",)),
    )(x2, dy, shift, scale, g_pre, w_in, qg, kg, cos, sin, bd, q_raw, k_raw, dqr, dkp, dvp, d_gatt, d_rin, d_grw)


def _w_in_grad_call(hb, dproj, T):
    R = hb.shape[0]
    TT = min(ROW_TILE, T)
    CB = 1152

    def body(h_ref, d_ref, o_ref):
        g = _dot_tn(h_ref[...], d_ref[...])

        @pl.when(pl.program_id(1) == 0)
        def _():
            o_ref[...] = g

        @pl.when(pl.program_id(1) > 0)
        def _():
            o_ref[...] += g

    return pl.pallas_call(
        body, name="w_in_grad", grid=(C_END // CB, R // TT),
        in_specs=[pl.BlockSpec((TT, D_MODEL), lambda j, i: (i, 0)), pl.BlockSpec((TT, CB), lambda j, i: (i, j))],
        out_specs=pl.BlockSpec((D_MODEL, CB), lambda j, i: (0, j)),
        out_shape=jax.ShapeDtypeStruct((D_MODEL, C_END), F32), compiler_params=_cp(("arbitrary", "arbitrary")),
    )(hb, dproj)


def _adam_call(parts, w, m, v, name, row_tile=None):
    P, M, N = parts.shape
    TM = M if row_tile is None else row_tile

    def body(p_ref, w_ref, m_ref, v_ref, g_o, d_o, m_o, v_o):
        g = p_ref[0].astype(F32)
        for j in range(1, P):
            g = g + p_ref[j].astype(F32)
        m2 = ADAM_B1 * m_ref[...] + (1.0 - ADAM_B1) * g
        v2 = ADAM_B2 * v_ref[...] + (1.0 - ADAM_B2) * jnp.square(g)
        m_hat = m2 / (1.0 - ADAM_B1 ** ADAM_STEP)
        v_hat = v2 / (1.0 - ADAM_B2 ** ADAM_STEP)
        g_o[...] = g
        d_o[...] = -ADAM_LR * (m_hat / (jnp.sqrt(v_hat) + ADAM_EPS) + ADAM_WD * w_ref[...])
        m_o[...] = m2
        v_o[...] = v2

    blk = pl.BlockSpec((TM, N), lambda i: (i, 0))
    return pl.pallas_call(
        body, name=name, grid=(M // TM,),
        in_specs=[pl.BlockSpec((P, TM, N), lambda i: (0, i, 0)), blk, blk, blk], out_specs=(blk,) * 4,
        out_shape=(jax.ShapeDtypeStruct((M, N), F32),) * 4, compiler_params=_cp(("arbitrary",)),
    )(parts, w, m, v)


_SMALL_ROWS = 136


def _pack_small(taps, w_up, w0, a_up, a0):
    flat = jnp.concatenate([taps.reshape(-1), w_up.reshape(-1), w0.reshape(-1), a_up.reshape(-1), a0.reshape(-1)])
    return jnp.pad(flat, (0, _SMALL_ROWS * 128 - flat.shape[0])).reshape(_SMALL_ROWS, 128)


def _unpack_small(packed):
    n = packed.shape[0]
    flat = packed.reshape(n, -1)
    out, o = [], 0
    for shape in ((3, 208), (2, 64, 64), (2, 64), (2, 64, 64), (2, 64)):
        size = 1
        for s in shape:
            size *= s
        out.append(flat[:, o:o + size].reshape((n,) + shape))
        o += size
    return out


def _cols_to_full(blocks):
    nd = blocks.ndim
    moved = jnp.moveaxis(blocks, 0, nd - 2)
    return moved.reshape(moved.shape[:-2] + (moved.shape[-2] * moved.shape[-1],))


def _full_to_cols(full):
    k = full.shape[-1] // NDEV
    return jnp.moveaxis(full.reshape(full.shape[:-1] + (NDEV, k)), -2, 0)


_REP_SIZES = (("g_pre", 1024), ("q_norm_g", 64), ("k_norm_g", 64), ("k_k", 512), ("k_a", 512), ("r_k", 512),
              ("gn_w", 512), ("gn_b", 512), ("g_post", 1024))
_REP_ROWS = 40


def kernel(x, c, w_ada, b_ada, g_pre, w_in, q_norm_g, k_norm_g, shift_taps, w_up, w0, a_up, a0, k_k, k_a, r_k, gn_w, gn_b, w_out, g_post, loss_target, m_w_ada, m_b_ada, m_g_pre, m_w_in, m_q_norm_g, m_k_norm_g, m_shift_taps, m_w_up, m_w0, m_a_up, m_a0, m_k_k, m_k_a, m_r_k, m_gn_w, m_gn_b, m_w_out, m_g_post, v_w_ada, v_b_ada, v_g_pre, v_w_in, v_q_norm_g, v_k_norm_g, v_shift_taps, v_w_up, v_w0, v_a_up, v_a0, v_k_k, v_k_a, v_r_k, v_gn_w, v_gn_b, v_w_out, v_g_post):
    B, T, _ = x.shape
    R = B * T
    me = 4 * lax.axis_index("x") + 2 * lax.axis_index("y") + lax.axis_index("c")
    x2 = x.reshape(R, D_MODEL)
    tgt2 = loss_target.reshape(R, D_MODEL)

    seg = jnp.arange(256) // HEAD_DIM
    bd = (seg[:, None] == seg[None, :]).astype(MXU_DTYPE)
    eye = (jnp.arange(HEAD_DIM)[:, None] == (jnp.arange(RWKV_W) % HEAD_DIM)[None, :])
    eye_b, eye_f = eye.astype(MXU_DTYPE), eye.astype(F32)
    cos, sin = _rope_tables(T)

    c_g, w_in_g, w_out_g, small_g = _exchange(
        [c, w_in[0].astype(MXU_DTYPE), w_out[0].astype(MXU_DTYPE),
         _pack_small(shift_taps[0], w_up[0], w0[0], a_up[0], a0[0])], [False] * 4, "gather_params")
    c_all = c_g.reshape(NDEV * B, D_MODEL)
    w_in_f = _cols_to_full(w_in_g)
    w_out_f = w_out_g.reshape(D_MODEL, D_MODEL)
    taps_b, w_up_b, w0_b, a_up_b, a0_b = _unpack_small(small_g)
    taps_f = jnp.pad(_cols_to_full(taps_b), ((0, 5), (0, 0)))
    w_up_f, a_up_f = _cols_to_full(w_up_b), _cols_to_full(a_up_b)
    w0_f, a0_f = _cols_to_full(w0_b), _cols_to_full(a0_b)
    wup_pad = jnp.pad(w_up_f, ((0, 0), (0, 64), (0, 0))).astype(MXU_DTYPE)
    aup_pad = jnp.pad(a_up_f, ((0, 0), (64, 0), (0, 0))).astype(MXU_DTYPE)

    ncol = w_ada.shape[2]
    b_cols = lax.dynamic_slice(b_ada, (0, me * ncol), (1, ncol))
    mod_cols = _mod_call(c_all, w_ada[0].astype(MXU_DTYPE), b_cols)
    (mod_g,) = _exchange([mod_cols], [False], "gather_mod")
    mod = lax.dynamic_slice(_cols_to_full(mod_g), (me * B, 0), (B, 3 * D_MODEL))
    shift, scale, gate = [mod[:, j * D_MODEL:(j + 1) * D_MODEL].reshape(B, 1, D_MODEL) for j in range(3)]

    qg = jnp.tile(q_norm_g, (1, ATT_W // HEAD_DIM))
    kg = jnp.tile(k_norm_g, (1, KV_W // HEAD_DIM))
    rk_row = r_k.reshape(1, RWKV_W)

    hb, qr, kpad, vpad, q_raw, k_raw, g_att, rin, g_rw = _in_proj_call(
        x2, shift, scale, g_pre, w_in_f, qg, kg, cos, sin, bd, T)
    y_att = _att_fwd_call(qr, kpad, vpad, B, T)
    shifted = _shift_fwd_call(rin, taps_f, T)
    w_s, kt_s, akk_s, kk_s = _rwkv_prep_call(shifted, wup_pad, aup_pad, w0_f, a0_f, k_k, k_a, bd, T)
    sh3 = shifted.reshape(B, T, SHIFT_W)
    r4 = lambda a: a.reshape(2, B, T, RWKV_W)
    y0, y1, st = _scan_fwd_call(r4(w_s), r4(kt_s), r4(akk_s), kk_s.reshape(B, T, RWKV_W), sh3, eye_b, eye_f, bd, B, T)

    (loss_blk, dy, d_yatt, d_gatt, d_ys, d_r2, d_v2, d_kts, d_grw, d_gate, g_wout, g_gpost, g_gnw, g_gnb,
     g_rk) = _out_head_call(x2, tgt2, gate, y_att, g_att, y0.reshape(R, RWKV_W), y1.reshape(R, RWKV_W), shifted, kt_s,
                            g_rw, w_out_f, g_post, gn_w, gn_b, rk_row, bd, T)
    scan_cts = _scan_bwd_call(r4(w_s), r4(kt_s), r4(akk_s), kk_s.reshape(B, T, RWKV_W), sh3,
                              d_ys.reshape(B, T, RWKV_W), st, eye_b, eye_f, bd, B, T)
    scan_cts = [a.reshape(R, RWKV_W) for a in scan_cts]
    d_shifted, g_wup, g_aup, g_w0, g_a0, g_kk, g_ka = _rwkv_prep_bwd_call(
        shifted, scan_cts + [d_r2, d_v2, d_kts], wup_pad, aup_pad, w0_f, a0_f, k_k, k_a, bd, T)
    d_rin, g_taps = _shift_bwd_call(rin, d_shifted, taps_f, T)
    dqr, dkp, dvp = _att_bwd_call(qr, kpad, vpad, d_yatt, B, T)
    grad_x, dproj, d_shift, d_scale, g_gpre, g_qg, g_kg = _in_proj_bwd_call(
        x2, dy, shift, scale, g_pre, w_in_f, qg, kg, cos, sin, bd, q_raw, k_raw, dqr, dkp, dvp, d_gatt, d_rin, d_grw, T)
    g_win = _w_in_grad_call(hb, dproj, T)

    rep = jnp.concatenate([g_gpre.reshape(-1), g_qg.reshape(-1, HEAD_DIM).sum(0), g_kg.reshape(-1, HEAD_DIM).sum(0),
                           g_kk.reshape(-1), g_ka.reshape(-1), g_rk.reshape(-1), g_gnw.reshape(-1), g_gnb.reshape(-1),
                           g_gpost.reshape(-1), loss_blk[0, :1]])
    rep = jnp.pad(rep, (0, _REP_ROWS * 128 - rep.shape[0])).reshape(_REP_ROWS, 128)
    dmod = jnp.concatenate([d_shift, d_scale, d_gate], axis=2).reshape(B, 3 * D_MODEL)
    small_parts = jax.vmap(_pack_small)(_full_to_cols(g_taps[:3]), _full_to_cols(g_wup[:, :64, :]), _full_to_cols(g_w0),
                                        _full_to_cols(g_aup[:, 64:, :]), _full_to_cols(g_a0))
    p_win, p_wout, p_small, dmod_g, rep_g = _exchange(
        [_full_to_cols(g_win).astype(MXU_DTYPE), g_wout.reshape(NDEV, D_MODEL // NDEV, D_MODEL).astype(MXU_DTYPE),
         small_parts, dmod, rep],
        [True, True, True, False, False], "reduce_grads")
    dmod_all = dmod_g.reshape(NDEV * B, 3 * D_MODEL)
    g_wada = _wada_grad_call(c_all, lax.dynamic_slice(dmod_all, (0, me * ncol), (NDEV * B, ncol)))

    res = {}

    def adam(name, parts, w, m, v, row_tile=None):
        shape = w.shape
        two_d = (-1, shape[-1])
        out = _adam_call(parts.reshape((parts.shape[0],) + w.reshape(two_d).shape), w.reshape(two_d), m.reshape(two_d),
                         v.reshape(two_d), "adam_" + name, row_tile)
        res[name] = [o.reshape(shape) for o in out]

    adam("w_ada", g_wada[None], w_ada, m_w_ada, v_w_ada)
    adam("b_ada", dmod_all.reshape(NDEV * B, 1, 3 * D_MODEL), b_ada, m_b_ada, v_b_ada)
    adam("w_in", p_win, w_in, m_w_in, v_w_in, 128)
    adam("w_out", p_wout, w_out, m_w_out, v_w_out)
    taps_p, wup_p, w0_p, aup_p, a0_p = _unpack_small(p_small)
    adam("shift_taps", taps_p, shift_taps, m_shift_taps, v_shift_taps)
    adam("w_up", wup_p, w_up, m_w_up, v_w_up)
    adam("w0", w0_p, w0, m_w0, v_w0)
    adam("a_up", aup_p, a_up, m_a_up, v_a_up)
    adam("a0", a0_p, a0, m_a0, v_a0)
    rep_flat = rep_g.reshape(NDEV, -1)
    off = 0
    given = dict(g_pre=(g_pre, m_g_pre, v_g_pre), q_norm_g=(q_norm_g, m_q_norm_g, v_q_norm_g),
                 k_norm_g=(k_norm_g, m_k_norm_g, v_k_norm_g), k_k=(k_k, m_k_k, v_k_k), k_a=(k_a, m_k_a, v_k_a),
                 r_k=(r_k, m_r_k, v_r_k), gn_w=(gn_w, m_gn_w, v_gn_w), gn_b=(gn_b, m_gn_b, v_gn_b),
                 g_post=(g_post, m_g_post, v_g_post))
    for name, size in _REP_SIZES:
        adam(name, rep_flat[:, off:off + size], *given[name])
        off += size

    loss = jnp.sum(rep_flat[:, off])
    order = ["w_ada", "b_ada", "g_pre", "w_in", "q_norm_g", "k_norm_g", "shift_taps", "w_up", "w0", "a_up", "a0", "k_k",
             "k_a", "r_k", "gn_w", "gn_b", "w_out", "g_post"]
    return (loss, grad_x.reshape(B, T, D_MODEL), *[res[n][0] for n in order], *[res[n][1] for n in order],
            *[res[n][2] for n in order], *[res[n][3] for n in order])
```

```python
import functools

import jax
import jax.numpy as jnp
from jax import lax
from jax.experimental import pallas as pl
from jax.experimental.pallas import tpu as pltpu

F32 = jnp.float32
MXU_DTYPE = jnp.bfloat16
MESH = pl.DeviceIdType.MESH
NDEV = 8

D_MODEL = 1024
HEAD_DIM = 64
ATT_W = 512
KV_W = 128
RWKV_W = 512
LORA_W = 128
SHIFT_W = 3 * RWKV_W + LORA_W
GRID_W = 64
ROPE_THETA = 10000.0
DECAY_SCALE = 0.6065306597126334
NORM_EPS = 1e-6
GN_EPS = 64e-5
L2_EPS = 1e-12
ATT_SCALE = HEAD_DIM ** -0.5
C_Q, C_K, C_V, C_GA, C_RIN, C_GRW, C_END = 0, 512, 640, 768, 1280, 2944, 3456

ADAM_LR, ADAM_B1, ADAM_B2, ADAM_EPS, ADAM_WD, ADAM_STEP = 0.001, 0.9, 0.999, 1e-08, 0.01, 10

ROW_TILE = 256
ATT_TILE_FWD = 256
ATT_TILE_BWD = 512
SCAN_CHUNK = 64
SCAN_UNROLL = 4
VMEM_LIMIT = 56 * 1024 * 1024


def _cp(sem=None):
    return pltpu.CompilerParams(dimension_semantics=sem, vmem_limit_bytes=VMEM_LIMIT)


def _dot(a, b, dims=(((1,), (0,)), ((), ()))):
    return lax.dot_general(a.astype(MXU_DTYPE), b.astype(MXU_DTYPE), dims, preferred_element_type=F32)


def _dot_nt(a, b):
    return _dot(a, b, (((1,), (1,)), ((), ())))


def _dot_tn(a, b):
    return _dot(a, b, (((0,), (0,)), ((), ())))


def _seg_dot(xb, bd):
    n = xb.shape[1]
    if n <= 256:
        return jnp.dot(xb, bd[:n, :n], preferred_element_type=F32)
    parts = [jnp.dot(xb[:, c:c + 256], bd, preferred_element_type=F32) for c in range(0, n, 256)]
    return jnp.concatenate(parts, axis=1)


def _split3(x):
    hi = x.astype(MXU_DTYPE)
    r1 = x - hi.astype(F32)
    mid = r1.astype(MXU_DTYPE)
    lo = (r1 - mid.astype(F32)).astype(MXU_DTYPE)
    return hi, mid, lo


def _segsum_raw(x, bd):
    hi, mid, lo = _split3(x)
    return _seg_dot(hi, bd) + _seg_dot(mid, bd) + _seg_dot(lo, bd)


@jax.custom_vjp
def _segsum_d(x, bd):
    return _segsum_raw(x, bd)


def _segsum_d_fwd(x, bd):
    return _segsum_raw(x, bd), bd


def _segsum_d_bwd(bd, ct):
    return _segsum_raw(ct, bd), jnp.zeros_like(bd)


_segsum_d.defvjp(_segsum_d_fwd, _segsum_d_bwd)


def _rope_tables(T):
    t = jnp.arange(T, dtype=F32)
    row = jnp.floor(t / GRID_W)
    col = t - row * GRID_W
    n_freq = HEAD_DIM // 4
    inv_freq = ROPE_THETA ** (-jnp.arange(n_freq, dtype=F32) / n_freq)
    d = jnp.arange(HEAD_DIM)
    pos = jnp.where((d < HEAD_DIM // 2)[None, :], row[:, None], col[:, None])
    ang = pos * inv_freq[d % n_freq][None, :]
    sign = jnp.where((d % 32) < 16, -1.0, 1.0).astype(F32)[None, :]
    cos = jnp.cos(ang)
    sin = jnp.sin(ang) * sign
    return jnp.tile(cos, (1, 2)), jnp.tile(sin, (1, 2))


def _rope_raw(x, cos, sin):
    n = x.shape[1]
    lane = lax.broadcasted_iota(jnp.int32, (1, n), 1)
    first = (lane % 32) < 16
    partner = jnp.where(first, pltpu.roll(x, n - 16, 1), pltpu.roll(x, 16, 1))
    return x * cos + partner * sin


@jax.custom_vjp
def _rope_d(x, cos, sin):
    return _rope_raw(x, cos, sin)


def _rope_d_fwd(x, cos, sin):
    return _rope_raw(x, cos, sin), (cos, sin)


def _rope_d_bwd(res, ct):
    cos, sin = res
    return _rope_raw(ct, cos, -sin), jnp.zeros_like(cos), jnp.zeros_like(sin)


_rope_d.defvjp(_rope_d_fwd, _rope_d_bwd)


def _rms(x, g):
    return x * lax.rsqrt(jnp.mean(x * x, axis=-1, keepdims=True) + NORM_EPS) * g


def _pre_fn(x, shift, scale, g_pre):
    return _rms(x, g_pre) * (1.0 + scale) + shift


def _qk_fn(q, g, cos, sin, bd, scale, diff):
    segsum = _segsum_d if diff else _segsum_raw
    rope = _rope_d if diff else _rope_raw
    qn = q * lax.rsqrt(segsum(q * q, bd) * (1.0 / HEAD_DIM) + NORM_EPS) * g
    return rope(qn, cos, sin) * scale


def _silu(x):
    return x * jax.nn.sigmoid(x)


def _rwkv_pw(k, pw0, pw1, pa0, pa1, w0, a0, k_k, k_a, bd, diff):
    segsum = _segsum_d if diff else _segsum_raw
    kk = k * k_k
    kk = kk * lax.rsqrt(segsum(kk * kk, bd) + L2_EPS)
    ws, kts, akks = [], [], []
    for z, (pw, pa) in enumerate(((pw0, pa0), (pw1, pa1))):
        w = jnp.exp(-DECAY_SCALE * jax.nn.sigmoid(w0[z:z + 1, :] + pw))
        a = jax.nn.sigmoid(a0[z:z + 1, :] + pa)
        ws.append(w)
        kts.append(k * (1.0 + (a - 1.0) * k_a))
        akks.append(a * kk)
    return ws[0], ws[1], kts[0], kts[1], akks[0], akks[1], kk


def _mix_fn(y_att, g_att, ys, r, v, kts, g_rw, gn_w, gn_b, r_k, bd, diff):
    segsum = _segsum_d if diff else _segsum_raw
    mu = segsum(ys, bd) * (1.0 / HEAD_DIM)
    d = ys - mu
    var = segsum(d * d, bd) * (1.0 / HEAD_DIM)
    yn = d * lax.rsqrt(var + GN_EPS) * gn_w + gn_b
    bonus = segsum(r * kts * r_k, bd) * v
    return y_att * _silu(g_att), (yn + bonus) * _silu(g_rw)


def _loss_fn(out, x, tgt, gate, g_post):
    e = x + gate * _rms(out, g_post) - tgt
    s = jnp.sum(e * e, axis=1, keepdims=True)
    return jnp.sum(s, axis=0, keepdims=True) * (0.5 / D_MODEL)


def _exchange(arrays, scatter, name):
    n = len(arrays)
    out_shape = tuple(
        jax.ShapeDtypeStruct((NDEV,) + tuple(a.shape[1:] if sc else a.shape), a.dtype)
        for a, sc in zip(arrays, scatter))
    chips = (4, 2, 6)

    def body(*refs):
        ins, outs = refs[:n], refs[n:2 * n]
        send_sems, recv_sems, local_sems = refs[2 * n:]
        ix, iy, ic = lax.axis_index("x"), lax.axis_index("y"), lax.axis_index("c")
        me = 4 * ix + 2 * iy + ic

        def peer(m):
            px = 1 - ix if (m >> 2) & 1 else ix
            py = 1 - iy if (m >> 1) & 1 else iy
            pc = 1 - ic if m & 1 else ic
            return (px, py, pc), 4 * px + 2 * py + pc

        def copy(k, j, src_ref, slot, to):
            return pltpu.make_async_remote_copy(src_ref=src_ref, dst_ref=outs[k].at[slot], send_sem=send_sems.at[k, j],
                                                recv_sem=recv_sems.at[k, j], device_id=to, device_id_type=MESH)

        local = [pltpu.make_async_copy(ins[k].at[me] if scatter[k] else ins[k], outs[k].at[me], local_sems.at[k])
                 for k in range(n)]
        for cp in local:
            cp.start()
        sends, arrivals, forwards = [], [], []
        for k in range(n):
            if scatter[k]:
                for m in range(1, NDEV):
                    to, p = peer(m)
                    sends.append(copy(k, m - 1, ins[k].at[p], me, to))
                    arrivals.append(copy(k, m - 1, ins[k].at[p], p, to))
            else:
                sib, sib_slot = peer(1)
                sends.append(copy(k, 0, ins[k], me, sib))
                for j, m in enumerate(chips):
                    to, p = peer(m)
                    sends.append(copy(k, 1 + j, ins[k], me, to))
                    forwards.append((copy(k, 1 + j, ins[k], p, to), copy(k, 4 + j, outs[k].at[p], p, sib)))
                    arrivals.append(copy(k, 4 + j, ins[k], peer(m ^ 1)[1], sib))
                arrivals.append(copy(k, 0, ins[k], sib_slot, sib))
        for cp in sends:
            cp.start()
        for arrived, onward in forwards:
            arrived.wait_recv()
            onward.start()
        for cp in arrivals:
            cp.wait_recv()
        for cp in sends + [onward for _, onward in forwards]:
            cp.wait_send()
        for cp in local:
            cp.wait()

    any_spec = pl.BlockSpec(memory_space=pl.ANY)
    return pl.pallas_call(
        body, name=name, out_shape=out_shape,
        in_specs=[any_spec] * n, out_specs=tuple([any_spec] * n),
        scratch_shapes=[pltpu.SemaphoreType.DMA((n, NDEV - 1)), pltpu.SemaphoreType.DMA((n, NDEV - 1)),
                        pltpu.SemaphoreType.DMA((n,))],
    )(*arrays)


def _mod_call(c_all, w_ada, b_cols):
    def body(c_ref, w_ref, b_ref, o_ref):
        o_ref[...] = _dot(_silu(c_ref[...]), w_ref[...]) + b_ref[...]

    return pl.pallas_call(body, name="mod_fwd",
                          out_shape=jax.ShapeDtypeStruct((c_all.shape[0], w_ada.shape[1]), F32))(c_all, w_ada, b_cols)


def _wada_grad_call(c_all, dmod_cols):
    def body(c_ref, d_ref, o_ref):
        o_ref[...] = _dot_tn(_silu(c_ref[...]), d_ref[...])

    return pl.pallas_call(body, name="w_ada_grad",
                          out_shape=jax.ShapeDtypeStruct((c_all.shape[1], dmod_cols.shape[1]), F32))(c_all, dmod_cols)


def _full(shape):
    nd = len(shape)
    return pl.BlockSpec(shape, lambda *_: (0,) * nd)


def _in_proj_call(x2, shift, scale, g_pre, w_in, qg, kg, cos, sin, bd, T):
    R = x2.shape[0]
    TT = min(ROW_TILE, T)
    tpe = T // TT

    def body(x_ref, sh_ref, sc_ref, gp_ref, w_ref, qg_ref, kg_ref, cos_ref, sin_ref, bd_ref,
             hb_ref, qr_ref, kpad_ref, vpad_ref, qraw_ref, kraw_ref, gatt_ref, rin_ref, grw_ref):
        h = _pre_fn(x_ref[...], sh_ref[0], sc_ref[0], gp_ref[...])
        hb = h.astype(MXU_DTYPE)
        hb_ref[...] = hb

        def proj(c0, c1):
            return jnp.dot(hb, w_ref[:, c0:c1], preferred_element_type=F32)

        q = proj(C_Q, C_K)
        k = proj(C_K, C_V)
        v = proj(C_V, C_GA)
        gatt_ref[...] = proj(C_GA, C_RIN)
        rin_ref[...] = proj(C_RIN, C_GRW)
        grw_ref[...] = proj(C_GRW, C_END)
        qraw_ref[...] = q
        kraw_ref[...] = k
        cos, sin, bd = cos_ref[...], sin_ref[...], bd_ref[...]
        qr = _qk_fn(q, qg_ref[...], jnp.tile(cos, (1, 4)), jnp.tile(sin, (1, 4)), bd, ATT_SCALE, False)
        qr_ref[...] = qr.astype(MXU_DTYPE)
        kr = _qk_fn(k, kg_ref[...], cos, sin, bd, 1.0, False)
        left = lax.broadcasted_iota(jnp.int32, (1, KV_W), 1) < HEAD_DIM
        for ref, val in ((kpad_ref, kr), (vpad_ref, v)):
            h0l = jnp.where(left, val, 0.0)
            h1r = jnp.where(left, 0.0, val)
            ref[0] = h0l.astype(MXU_DTYPE)
            ref[1] = pltpu.roll(h0l, HEAD_DIM, 1).astype(MXU_DTYPE)
            ref[2] = pltpu.roll(h1r, HEAD_DIM, 1).astype(MXU_DTYPE)
            ref[3] = h1r.astype(MXU_DTYPE)

    row = lambda w: pl.BlockSpec((TT, w), lambda i: (i, 0))
    per_ex = pl.BlockSpec((1, 1, D_MODEL), lambda i: (i // tpe, 0, 0))
    tab = pl.BlockSpec((TT, KV_W), lambda i: (i % tpe, 0))
    pad = pl.BlockSpec((4, TT, KV_W), lambda i: (0, i, 0))
    sds = jax.ShapeDtypeStruct
    return pl.pallas_call(
        body, name="in_proj", grid=(R // TT,),
        in_specs=[row(D_MODEL), per_ex, per_ex, _full((1, D_MODEL)), _full(w_in.shape), _full((1, ATT_W)),
                  _full((1, KV_W)), tab, tab, _full((256, 256))],
        out_specs=(row(D_MODEL), row(ATT_W), pad, pad, row(ATT_W), row(KV_W), row(ATT_W), row(SHIFT_W), row(RWKV_W)),
        out_shape=(sds((R, D_MODEL), MXU_DTYPE), sds((R, ATT_W), MXU_DTYPE), sds((4, R, KV_W), MXU_DTYPE),
                   sds((4, R, KV_W), MXU_DTYPE), sds((R, ATT_W), F32), sds((R, KV_W), F32), sds((R, ATT_W), F32),
                   sds((R, SHIFT_W), F32), sds((R, RWKV_W), F32)),
        compiler_params=_cp(("arbitrary",)),
    )(x2, shift, scale, g_pre, w_in, qg, kg, cos, sin, bd)


def _softmax_parts(s):
    e = jnp.exp(s - jnp.max(s, axis=1, keepdims=True))
    return e, 1.0 / jnp.sum(e, axis=1, keepdims=True)


def _att_specs(T, TQ):
    nq = T // TQ
    qspec = pl.BlockSpec((TQ, KV_W), lambda b, p, i: (b * nq + i, p))
    side = lambda s: pl.BlockSpec((None, T, KV_W), lambda b, p, i: (2 * (p // 2) + s, b, 0))
    return nq, qspec, side


def _att_fwd_call(qr, kpad, vpad, B, T):
    TQ = min(ATT_TILE_FWD, T)
    nq, qspec, side = _att_specs(T, TQ)

    def body(q_ref, kl_ref, kr_ref, vl_ref, vr_ref, o_ref):
        q = q_ref[...]
        ea, inv_a = _softmax_parts(_dot_nt(q, kl_ref[...]))
        eb, inv_b = _softmax_parts(_dot_nt(q, kr_ref[...]))
        o_ref[...] = _dot(ea, vl_ref[...]) * inv_a + _dot(eb, vr_ref[...]) * inv_b

    return pl.pallas_call(
        body, name="att_fwd", grid=(B, 4, nq),
        in_specs=[qspec, side(0), side(1), side(0), side(1)], out_specs=qspec,
        out_shape=jax.ShapeDtypeStruct((B * T, ATT_W), F32),
        compiler_params=_cp(("arbitrary",) * 3),
    )(qr, kpad, kpad, vpad, vpad)


def _att_bwd_call(qr, kpad, vpad, d_o, B, T):
    TQ = min(ATT_TILE_BWD, T)
    nq, qspec, side = _att_specs(T, TQ)

    def body(q_ref, kl_ref, kr_ref, vl_ref, vr_ref, do_ref, dq_ref, dk_ref, dv_ref):
        i = pl.program_id(2)
        q, do = q_ref[...], do_ref[...]
        left = lax.broadcasted_iota(jnp.int32, (1, KV_W), 1) < HEAD_DIM
        dq = jnp.zeros((TQ, KV_W), F32)
        dk = jnp.zeros((T, KV_W), F32)
        dv = jnp.zeros((T, KV_W), F32)
        for k_ref, v_ref, mask in ((kl_ref, vl_ref, left), (kr_ref, vr_ref, jnp.logical_not(left))):
            kk, vv = k_ref[...], v_ref[...]
            e, inv = _softmax_parts(_dot_nt(q, kk))
            dp = _dot_nt(do, vv)
            ds = e * (dp - inv * jnp.sum(e * dp, axis=1, keepdims=True))
            dq = dq + _dot(ds, kk) * inv
            dk = dk + _dot_tn(ds, jnp.where(mask, q * inv, 0.0))
            dv = dv + _dot_tn(e, jnp.where(mask, do * inv, 0.0))
        dq_ref[...] = dq

        @pl.when(i == 0)
        def _():
            dk_ref[...] = dk
            dv_ref[...] = dv

        @pl.when(i > 0)
        def _():
            dk_ref[...] += dk
            dv_ref[...] += dv

    acc = pl.BlockSpec((None, T, KV_W), lambda b, p, i: (p, b, 0))
    sds = jax.ShapeDtypeStruct
    return pl.pallas_call(
        body, name="att_bwd", grid=(B, 4, nq),
        in_specs=[qspec, side(0), side(1), side(0), side(1), qspec], out_specs=(qspec, acc, acc),
        out_shape=(sds((B * T, ATT_W), F32), sds((4, B * T, KV_W), F32), sds((4, B * T, KV_W), F32)),
        compiler_params=_cp(("arbitrary",) * 3),
    )(qr, kpad, kpad, vpad, vpad, d_o)


def _shift_specs(R, T, TT, width):
    tpe = T // TT
    nb8 = R // 8
    cur = pl.BlockSpec((TT, width), lambda i: (i, 0))
    prev = pl.BlockSpec((8, width), lambda i: (jnp.maximum(i * (TT // 8) - 1, 0), 0))
    nxt = pl.BlockSpec((8, width), lambda i: (jnp.minimum((i + 1) * (TT // 8), nb8 - 1), 0))
    return tpe, cur, prev, nxt


def _neighbours(cur, prev8, next8, i, tpe, TT):
    rows = lax.broadcasted_iota(jnp.int32, (TT, 1), 0)
    first = jnp.where(i % tpe == 0, 0.0, 1.0)
    last = jnp.where(i % tpe == tpe - 1, 0.0, 1.0)
    before = jnp.where(rows == 0, prev8[7:8, :] * first, pltpu.roll(cur, 1, 0))
    after = jnp.where(rows == TT - 1, next8[0:1, :] * last, pltpu.roll(cur, TT - 1, 0))
    return before, after


def _shift_fwd_call(x, taps, T):
    R, width = x.shape
    TT = min(ROW_TILE, T)
    tpe, cur, prev, nxt = _shift_specs(R, T, TT, width)

    def body(x_ref, p_ref, n_ref, t_ref, o_ref):
        xc = x_ref[...]
        before, after = _neighbours(xc, p_ref[...], n_ref[...], pl.program_id(0), tpe, TT)
        o_ref[...] = t_ref[0:1, :] * before + t_ref[1:2, :] * xc + t_ref[2:3, :] * after

    return pl.pallas_call(
        body, name="shift_fwd", grid=(R // TT,), in_specs=[cur, prev, nxt, _full(taps.shape)], out_specs=cur,
        out_shape=jax.ShapeDtypeStruct((R, width), F32), compiler_params=_cp(("arbitrary",)),
    )(x, x, x, taps)


def _shift_bwd_call(x, d, taps, T):
    R, width = x.shape
    TT = min(ROW_TILE, T)
    tpe, cur, prev, nxt = _shift_specs(R, T, TT, width)

    def body(x_ref, xp_ref, xn_ref, d_ref, dp_ref, dn_ref, t_ref, dx_ref, dt_ref):
        i = pl.program_id(0)
        xc, dc = x_ref[...], d_ref[...]
        d_before, d_after = _neighbours(dc, dp_ref[...], dn_ref[...], i, tpe, TT)
        dx_ref[...] = t_ref[2:3, :] * d_before + t_ref[1:2, :] * dc + t_ref[0:1, :] * d_after
        x_before, x_after = _neighbours(xc, xp_ref[...], xn_ref[...], i, tpe, TT)
        @pl.when(i == 0)
        def _():
            dt_ref[...] = jnp.zeros_like(dt_ref)

        for j, xs in enumerate((x_before, xc, x_after)):
            dt_ref[j:j + 1, :] += jnp.sum(dc * xs, axis=0, keepdims=True)

    return pl.pallas_call(
        body, name="shift_bwd", grid=(R // TT,),
        in_specs=[cur, prev, nxt, cur, prev, nxt, _full(taps.shape)], out_specs=(cur, _full((8, width))),
        out_shape=(jax.ShapeDtypeStruct((R, width), F32), jax.ShapeDtypeStruct((8, width), F32)),
        compiler_params=_cp(("arbitrary",)),
    )(x, x, x, d, d, d, taps)


def _lora_in(wa):
    lane = lax.broadcasted_iota(jnp.int32, (1, LORA_W), 1)
    return jnp.where(lane < LORA_W // 2, jnp.tanh(wa), wa)


def _rwkv_prep_call(shifted, wup, aup, w0, a0, k_k, k_a, bd, T):
    R = shifted.shape[0]
    TT = min(ROW_TILE, T)

    def body(k_ref, wa_ref, wup_ref, aup_ref, w0_ref, a0_ref, kk_ref, ka_ref, bd_ref, w_o, kt_o, akk_o, kk_o):
        twa = _lora_in(wa_ref[...])
        pre = [_dot(twa, m_ref[z]) for m_ref in (wup_ref, aup_ref) for z in range(2)]
        outs = _rwkv_pw(k_ref[...], pre[0], pre[1], pre[2], pre[3], w0_ref[...], a0_ref[...], kk_ref[...],
                        ka_ref[...], bd_ref[...], False)
        w_o[0], w_o[1], kt_o[0], kt_o[1], akk_o[0], akk_o[1] = outs[:6]
        kk_o[...] = outs[6]

    col = lambda c, w: pl.BlockSpec((TT, w), lambda i: (i, c))
    two = pl.BlockSpec((2, TT, RWKV_W), lambda i: (0, i, 0))
    sds = jax.ShapeDtypeStruct
    return pl.pallas_call(
        body, name="rwkv_prep", grid=(R // TT,),
        in_specs=[col(1, RWKV_W), col(3 * RWKV_W // LORA_W, LORA_W), _full(wup.shape), _full(aup.shape),
                  _full((2, RWKV_W)), _full((2, RWKV_W)), _full((1, RWKV_W)), _full((1, RWKV_W)), _full((256, 256))],
        out_specs=(two, two, two, col(0, RWKV_W)),
        out_shape=(sds((2, R, RWKV_W), F32),) * 3 + (sds((R, RWKV_W), F32),),
        compiler_params=_cp(("arbitrary",)),
    )(shifted, shifted, wup, aup, w0, a0, k_k, k_a, bd)


def _rwkv_prep_bwd_call(shifted, cts, wup, aup, w0, a0, k_k, k_a, bd, T):
    R = shifted.shape[0]
    TT = min(ROW_TILE, T)

    def body(k_ref, wa_ref, dw0, dkt0, dakk0, dkk0, dr0, dv0, dw1, dkt1, dakk1, dkk1, dr1, dv1, dr2_ref, dv2_ref, dkts_ref,
             wup_ref, aup_ref, w0_ref, a0_ref, kk_ref, ka_ref, bd_ref,
             dsh_ref, gwup_ref, gaup_ref, gw0_ref, ga0_ref, gkk_ref, gka_ref):
        dw_ref, dkt_ref, dakk_ref, dkk_ref, dr_ref, dv_ref = ((dw0, dw1), (dkt0, dkt1), (dakk0, dakk1), (dkk0, dkk1),
                                                              (dr0, dr1), (dv0, dv1))
        i = pl.program_id(0)
        wa = wa_ref[...]
        twa = _lora_in(wa)
        pre = [_dot(twa, m_ref[z]) for m_ref in (wup_ref, aup_ref) for z in range(2)]
        fn = functools.partial(_rwkv_pw, bd=bd_ref[...], diff=True)
        _, vjp = jax.vjp(fn, k_ref[...], pre[0], pre[1], pre[2], pre[3], w0_ref[...], a0_ref[...], kk_ref[...],
                         ka_ref[...])
        dkts = dkts_ref[...]
        dk, dpw0, dpw1, dpa0, dpa1, gw0, ga0, gkk, gka = vjp(
            (dw_ref[0][...], dw_ref[1][...], dkt_ref[0][...] + dkts, dkt_ref[1][...] + dkts, dakk_ref[0][...],
             dakk_ref[1][...], dkk_ref[0][...] + dkk_ref[1][...]))
        dtwa = (_dot_nt(dpw0, wup_ref[0]) + _dot_nt(dpw1, wup_ref[1]) + _dot_nt(dpa0, aup_ref[0])
                + _dot_nt(dpa1, aup_ref[1]))
        lane = lax.broadcasted_iota(jnp.int32, (1, LORA_W), 1)
        dsh_ref[:, 0:RWKV_W] = dr_ref[0][...] + dr_ref[1][...] + dr2_ref[...]
        dsh_ref[:, RWKV_W:2 * RWKV_W] = dk
        dsh_ref[:, 2 * RWKV_W:3 * RWKV_W] = dv_ref[0][...] + dv_ref[1][...] + dv2_ref[...]
        dsh_ref[:, 3 * RWKV_W:] = jnp.where(lane < LORA_W // 2, dtwa * (1.0 - twa * twa), dtwa)
        acc = ((gwup_ref.at[0], _dot_tn(twa, dpw0)), (gwup_ref.at[1], _dot_tn(twa, dpw1)),
               (gaup_ref.at[0], _dot_tn(twa, dpa0)), (gaup_ref.at[1], _dot_tn(twa, dpa1)),
               (gw0_ref, gw0), (ga0_ref, ga0), (gkk_ref, gkk), (gka_ref, gka))

        @pl.when(i == 0)
        def _():
            for ref, val in acc:
                ref[...] = val

        @pl.when(i > 0)
        def _():
            for ref, val in acc:
                ref[...] += val

    col = lambda c, w: pl.BlockSpec((TT, w), lambda i: (i, c))
    one = col(0, RWKV_W)
    sds = jax.ShapeDtypeStruct
    return pl.pallas_call(
        body, name="rwkv_prep_bwd", grid=(R // TT,),
        in_specs=[col(1, RWKV_W), col(3 * RWKV_W // LORA_W, LORA_W)] + [one] * 15 + [
                  _full(wup.shape), _full(aup.shape), _full((2, RWKV_W)), _full((2, RWKV_W)), _full((1, RWKV_W)),
                  _full((1, RWKV_W)), _full((256, 256))],
        out_specs=(pl.BlockSpec((TT, SHIFT_W), lambda i: (i, 0)), _full(wup.shape), _full(aup.shape),
                   _full((2, RWKV_W)), _full((2, RWKV_W)), _full((1, RWKV_W)), _full((1, RWKV_W))),
        out_shape=(sds((R, SHIFT_W), F32), sds(wup.shape, F32), sds(aup.shape, F32), sds((2, RWKV_W), F32),
                   sds((2, RWKV_W), F32), sds((1, RWKV_W), F32), sds((1, RWKV_W), F32)),
        compiler_params=_cp(("arbitrary",)),
    )(shifted, shifted, *cts, wup, aup, w0, a0, k_k, k_a, bd)


def _col_lhs(row, eye_b):
    return eye_b * row.astype(MXU_DTYPE)


def _colsum(x):
    return jnp.sum(x, axis=0, keepdims=True)


def _stacked_segsum(tiles, bd):
    res = _seg_dot(jnp.concatenate(tiles, axis=0), bd)
    return [res[j * HEAD_DIM:(j + 1) * HEAD_DIM] for j in range(len(tiles))]


def _scan_specs(B, T, C, nC):
    def blk(z, col, rev):
        idx = (lambda g: (z, 0, nC - 1 - g, col)) if rev else (lambda g: (z, 0, g, col))
        return pl.BlockSpec((None, B, C, RWKV_W), idx)

    def blk3(col, rev):
        idx = (lambda g: (0, nC - 1 - g, col)) if rev else (lambda g: (0, g, col))
        return pl.BlockSpec((B, C, RWKV_W), idx)

    return blk, blk3


def _scan_fwd_call(w, kt, akk, kk, shifted, eye_b, eye_f, bd, B, T):
    C = min(SCAN_CHUNK, T)
    nC = T // C
    blk, blk3 = _scan_specs(B, T, C, nC)

    def body(w0, kt0, akk0, kk0, v0, r0, w1, kt1, akk1, kk1, v1, r1, eb_ref, ef_ref, bd_ref, y0, y1, st, S):
        @pl.when(pl.program_id(0) == 0)
        def _():
            S[...] = jnp.zeros_like(S)

        st[0] = S[...].astype(MXU_DTYPE)
        dirs = ((w0, kt0, akk0, kk0, v0, r0, y0), (w1, kt1, akk1, kk1, v1, r1, y1))

        def step(s, carry):
            for z in range(2):
                row = s if z == 0 else C - 1 - s
                prev = jnp.maximum(s - 1, 0) if z == 0 else jnp.minimum(C - s, C - 1)
                wr, ktr, akkr, kkr, vr, rr, yr = dirs[z]
                tiles = []
                for b in range(B):
                    Sb = st[s, z * B + b]
                    tiles += [Sb * kkr[b, pl.ds(row, 1), :].astype(MXU_DTYPE),
                              _col_lhs(vr[b, pl.ds(row, 1), :], eb_ref[...]),
                              Sb * rr[b, pl.ds(prev, 1), :].astype(MXU_DTYPE)]
                res = _stacked_segsum(tiles, bd_ref[...])
                for b in range(B):
                    c = z * B + b
                    sab, vb, yb = res[3 * b:3 * b + 3]
                    ld = lambda ref: ref[b, pl.ds(row, 1), :]
                    Sn = S[c] * ld(wr) - sab * ld(akkr) + vb * ld(ktr)
                    S[c] = Sn
                    st[s + 1, c] = Sn.astype(MXU_DTYPE)
                    yr[b, pl.ds(prev, 1), :] = _colsum(ef_ref[...] * yb)
            return carry

        lax.fori_loop(0, C, step, 0, unroll=SCAN_UNROLL)
        for z in range(2):
            last = C - 1 if z == 0 else 0
            rr, yr = dirs[z][5], dirs[z][6]
            res = _stacked_segsum([st[C, z * B + b] * rr[b, last:last + 1, :].astype(MXU_DTYPE) for b in range(B)],
                                  bd_ref[...])
            for b in range(B):
                yr[b, last:last + 1, :] = _colsum(ef_ref[...] * res[b])

    ins, specs = [], []
    for z, rev in ((0, False), (1, True)):
        ins += [w, kt, akk, kk, shifted, shifted]
        specs += [blk(z, 0, rev), blk(z, 0, rev), blk(z, 0, rev), blk3(0, rev), blk3(2, rev), blk3(0, rev)]
    sds = jax.ShapeDtypeStruct
    return pl.pallas_call(
        body, name="scan_fwd", grid=(nC,),
        in_specs=specs + [_full((HEAD_DIM, RWKV_W)), _full((HEAD_DIM, RWKV_W)), _full((256, 256))],
        out_specs=(blk3(0, False), blk3(0, True),
                   pl.BlockSpec((None, C + 1, 2 * B, HEAD_DIM, RWKV_W), lambda g: (g, 0, 0, 0, 0))),
        out_shape=(sds((B, T, RWKV_W), F32), sds((B, T, RWKV_W), F32),
                   sds((nC, C + 1, 2 * B, HEAD_DIM, RWKV_W), MXU_DTYPE)),
        scratch_shapes=[pltpu.VMEM((2 * B, HEAD_DIM, RWKV_W), F32)],
        compiler_params=_cp(("arbitrary",)),
    )(*ins, eye_b, eye_f, bd)


def _scan_bwd_call(w, kt, akk, kk, shifted, dys, st, eye_b, eye_f, bd, B, T):
    C = min(SCAN_CHUNK, T)
    nC = T // C
    blk, blk3 = _scan_specs(B, T, C, nC)
    nin = 7

    def body(*refs):
        d0, d1 = refs[:nin], refs[nin:2 * nin]
        st_ref, eb_ref, ef_ref, sel_ref, bd_ref = refs[2 * nin:2 * nin + 5]
        o0, o1 = refs[2 * nin + 5:2 * nin + 11], refs[2 * nin + 11:2 * nin + 17]
        COL, DYC, G = refs[2 * nin + 17:]

        @pl.when(pl.program_id(0) == 0)
        def _():
            G[...] = jnp.zeros_like(G)

        dirs = (d0 + (o0,), d1 + (o1,))

        def column_operands(s, z):
            row = s if z == 0 else C - 1 - s
            _, _, _, kkr, vr, _, dyr, _ = dirs[z]
            tiles = []
            for b in range(B):
                tiles += [st_ref[s, z * B + b] * kkr[b, pl.ds(row, 1), :].astype(MXU_DTYPE),
                          _col_lhs(vr[b, pl.ds(row, 1), :], eb_ref[...]),
                          _col_lhs(dyr[b, pl.ds(row, 1), :], eb_ref[...])]
            return tiles

        def keep_columns(res, z):
            for b in range(B):
                for k in range(3):
                    COL[k, z * B + b] = res[3 * b + k].astype(MXU_DTYPE)
                DYC[z * B + b] = res[3 * b + 2]

        for z in range(2):
            keep_columns(_stacked_segsum(column_operands(C - 1, z), bd_ref[...]), z)

        def bwd(it, carry):
            s = C - 1 - it
            for z in range(2):
                row = s if z == 0 else C - 1 - s
                wr, ktr, akkr, kkr, vr, rr, dyr, (dw_o, dkt_o, dakk_o, dkk_o, dr_o, dv_o) = dirs[z]
                tiles, Gcs = [], []
                for b in range(B):
                    c = z * B + b
                    Gc = G[c] + DYC[c] * rr[b, pl.ds(row, 1), :]
                    Gb = Gc.astype(MXU_DTYPE)
                    Gcs.append((Gc, Gb))
                    tiles += [Gb * akkr[b, pl.ds(row, 1), :].astype(MXU_DTYPE),
                              Gb * ktr[b, pl.ds(row, 1), :].astype(MXU_DTYPE)]
                res = _stacked_segsum(tiles + column_operands(jnp.maximum(s - 1, 0), z), bd_ref[...])
                for b in range(B):
                    c = z * B + b
                    Gc, Gb = Gcs[b]
                    gab, dvb = res[2 * b], res[2 * b + 1]
                    ld = lambda ref: ref[b, pl.ds(row, 1), :]
                    G[c] = Gc * ld(wr) - gab * ld(kkr)
                    Sb = st_ref[s, c]
                    prods = jnp.concatenate([st_ref[s + 1, c] * COL[2, c], Gb * COL[1, c], Gb * Sb, Gb * COL[0, c]], axis=0)
                    sums = jnp.dot(sel_ref[...], prods, preferred_element_type=F32)
                    for k, (ref, sign) in enumerate(((dr_o, 1.0), (dkt_o, 1.0), (dw_o, 1.0), (dakk_o, -1.0))):
                        ref[b, pl.ds(row, 1), :] = sign * sums[k:k + 1, :]
                    dv_o[b, pl.ds(row, 1), :] = _colsum(ef_ref[...] * dvb)
                    dkk_o[b, pl.ds(row, 1), :] = -_colsum(gab * Sb.astype(F32))
                keep_columns(res[2 * B:], z)
            return carry

        lax.fori_loop(0, C, bwd, 0, unroll=SCAN_UNROLL)

    ins, specs = [], []
    for z, rev in ((0, True), (1, False)):
        ins += [w, kt, akk, kk, shifted, shifted, dys]
        specs += [blk(z, 0, rev), blk(z, 0, rev), blk(z, 0, rev), blk3(0, rev), blk3(2, rev), blk3(0, rev), blk3(0, rev)]
    sel = (jnp.arange(16)[:, None] == (jnp.arange(4 * HEAD_DIM) // HEAD_DIM)[None, :]).astype(MXU_DTYPE)
    ins += [st, eye_b, eye_f, sel, bd]
    specs += [pl.BlockSpec((None, C + 1, 2 * B, HEAD_DIM, RWKV_W), lambda g: (nC - 1 - g, 0, 0, 0, 0)),
              _full((HEAD_DIM, RWKV_W)), _full((HEAD_DIM, RWKV_W)), _full(sel.shape), _full((256, 256))]
    sds = jax.ShapeDtypeStruct
    out_specs = tuple(blk3(0, True) for _ in range(6)) + tuple(blk3(0, False) for _ in range(6))
    res = pl.pallas_call(
        body, name="scan_bwd", grid=(nC,), in_specs=specs, out_specs=out_specs,
        out_shape=tuple(sds((B, T, RWKV_W), F32) for _ in range(12)),
        scratch_shapes=[pltpu.VMEM((3, 2 * B, HEAD_DIM, RWKV_W), MXU_DTYPE), pltpu.VMEM((2 * B, HEAD_DIM, RWKV_W), F32),
                        pltpu.VMEM((2 * B, HEAD_DIM, RWKV_W), F32)],
        compiler_params=_cp(("arbitrary",)),
    )(*ins)
    return list(res)


def _out_head_call(x2, tgt2, gate, y_att, g_att, y0, y1, shifted, kt, g_rw, w_out, g_post, gn_w, gn_b, r_k, bd, T):
    R = x2.shape[0]
    TT = min(ROW_TILE, T)
    tpe = T // TT

    def body(x_ref, t_ref, gate_ref, ya_ref, ga_ref, y0_ref, y1_ref, r_ref, v_ref, kt_ref, grw_ref, w_ref, gp_ref,
             gnw_ref, gnb_ref, rk_ref, bd_ref,
             loss_o, dy_o, dya_o, dga_o, dys_o, dr_o, dv_o, dkts_o, dgrw_o, dgate_o, gw_o, ggp_o, ggnw_o, ggnb_o, grk_o):
        i = pl.program_id(0)
        bd = bd_ref[...]
        mix = functools.partial(_mix_fn, bd=bd, diff=True)
        (ma, mr), mix_vjp = jax.vjp(mix, ya_ref[...], ga_ref[...], y0_ref[...] + y1_ref[...], r_ref[...], v_ref[...],
                                    kt_ref[0] + kt_ref[1], grw_ref[...], gnw_ref[...], gnb_ref[...], rk_ref[...])
        out = _dot(ma, w_ref[0:ATT_W, :]) + _dot(mr, w_ref[ATT_W:, :])
        loss, loss_vjp = jax.vjp(_loss_fn, out, x_ref[...], t_ref[...], gate_ref[0], gp_ref[...])
        d_out, dy, _, dgate, dgp = loss_vjp(jnp.ones((1, 1), F32))
        dy_o[...] = dy
        dma = _dot_nt(d_out, w_ref[0:ATT_W, :])
        dmr = _dot_nt(d_out, w_ref[ATT_W:, :])
        dya_o[...], dga_o[...], dys_o[...], dr_o[...], dv_o[...], dkts_o[...], dgrw_o[...], dgnw, dgnb, drk = \
            mix_vjp((dma, dmr))
        gw = jnp.concatenate([_dot_tn(ma, d_out), _dot_tn(mr, d_out)], axis=0)
        acc = ((loss_o, jnp.broadcast_to(loss, (8, 128))), (gw_o, gw), (ggp_o, dgp), (ggnw_o, dgnw), (ggnb_o, dgnb),
               (grk_o, drk))

        @pl.when(i == 0)
        def _():
            for ref, val in acc:
                ref[...] = val

        @pl.when(i > 0)
        def _():
            for ref, val in acc:
                ref[...] += val

        @pl.when(i % tpe == 0)
        def _():
            dgate_o[0] = dgate

        @pl.when(i % tpe > 0)
        def _():
            dgate_o[0] += dgate

    row = lambda w, c=0: pl.BlockSpec((TT, w), lambda i: (i, c))
    two = pl.BlockSpec((2, TT, RWKV_W), lambda i: (0, i, 0))
    per_ex = pl.BlockSpec((1, 1, D_MODEL), lambda i: (i // tpe, 0, 0))
    sds = jax.ShapeDtypeStruct
    r512 = sds((R, RWKV_W), F32)
    return pl.pallas_call(
        body, name="out_head", grid=(R // TT,),
        in_specs=[row(D_MODEL), row(D_MODEL), per_ex, row(ATT_W), row(ATT_W), row(RWKV_W), row(RWKV_W), row(RWKV_W, 0),
                  row(RWKV_W, 2), two,
                  row(RWKV_W), _full(w_out.shape), _full((1, D_MODEL)), _full((1, RWKV_W)), _full((1, RWKV_W)),
                  _full((1, RWKV_W)), _full((256, 256))],
        out_specs=(_full((8, 128)), row(D_MODEL), row(ATT_W), row(ATT_W), row(RWKV_W), row(RWKV_W), row(RWKV_W),
                   row(RWKV_W), row(RWKV_W), per_ex, _full((D_MODEL, D_MODEL)), _full((1, D_MODEL)), _full((1, RWKV_W)),
                   _full((1, RWKV_W)), _full((1, RWKV_W))),
        out_shape=(sds((8, 128), F32), sds((R, D_MODEL), F32), r512, r512, r512, r512, r512, r512, r512,
                   sds((R // T, 1, D_MODEL), F32), sds((D_MODEL, D_MODEL), F32), sds((1, D_MODEL), F32),
                   sds((1, RWKV_W), F32), sds((1, RWKV_W), F32), sds((1, RWKV_W), F32)),
        compiler_params=_cp(("arbitrary",)),
    )(x2, tgt2, gate, y_att, g_att, y0, y1, shifted, shifted, kt, g_rw, w_out, g_post, gn_w, gn_b, r_k, bd)


def _in_proj_bwd_call(x2, dy, shift, scale, g_pre, w_in, qg, kg, cos, sin, bd, q_raw, k_raw, dqr, dkp, dvp,
                      d_gatt, d_rin, d_grw, T):
    R = x2.shape[0]
    TT = min(ROW_TILE, T)
    tpe = T // TT

    def body(x_ref, dy_ref, sh_ref, sc_ref, gp_ref, w_ref, qg_ref, kg_ref, cos_ref, sin_ref, bd_ref, q_ref, k_ref,
             dqr_ref, dkp_ref, dvp_ref, dga_ref, drin_ref, dgrw_ref,
             dx_o, dproj_o, dsh_o, dsc_o, ggp_o, gqg_o, gkg_o):
        i = pl.program_id(0)
        cos, sin, bd = cos_ref[...], sin_ref[...], bd_ref[...]
        left = lax.broadcasted_iota(jnp.int32, (1, KV_W), 1) < HEAD_DIM

        def kv_grad(ref):
            a = ref[0] + ref[1]
            b = ref[2] + ref[3]
            return jnp.where(left, a + pltpu.roll(a, HEAD_DIM, 1), b + pltpu.roll(b, HEAD_DIM, 1))

        qfn = functools.partial(_qk_fn, cos=jnp.tile(cos, (1, 4)), sin=jnp.tile(sin, (1, 4)), bd=bd, scale=ATT_SCALE,
                                diff=True)
        _, q_vjp = jax.vjp(qfn, q_ref[...], qg_ref[...])
        dq, gqg = q_vjp(dqr_ref[...])
        kfn = functools.partial(_qk_fn, cos=cos, sin=sin, bd=bd, scale=1.0, diff=True)
        _, k_vjp = jax.vjp(kfn, k_ref[...], kg_ref[...])
        dk, gkg = k_vjp(kv_grad(dkp_ref))
        pieces = ((C_Q, C_K, dq), (C_K, C_V, dk), (C_V, C_GA, kv_grad(dvp_ref)), (C_GA, C_RIN, dga_ref[...]),
                  (C_RIN, C_GRW, drin_ref[...]), (C_GRW, C_END, dgrw_ref[...]))
        dh = jnp.zeros((TT, D_MODEL), F32)
        for c0, c1, val in pieces:
            vb = val.astype(MXU_DTYPE)
            dproj_o[:, c0:c1] = vb
            dh = dh + _dot_nt(vb, w_ref[:, c0:c1])
        _, pre_vjp = jax.vjp(_pre_fn, x_ref[...], sh_ref[0], sc_ref[0], gp_ref[...])
        dx, dsh, dsc, ggp = pre_vjp(dh)
        dx_o[...] = dx + dy_ref[...]
        acc = ((ggp_o, ggp), (gqg_o, gqg), (gkg_o, gkg))

        @pl.when(i == 0)
        def _():
            for ref, val in acc:
                ref[...] = val

        @pl.when(i > 0)
        def _():
            for ref, val in acc:
                ref[...] += val

        @pl.when(i % tpe == 0)
        def _():
            dsh_o[0] = dsh
            dsc_o[0] = dsc

        @pl.when(i % tpe > 0)
        def _():
            dsh_o[0] += dsh
            dsc_o[0] += dsc

    row = lambda w: pl.BlockSpec((TT, w), lambda i: (i, 0))
    per_ex = pl.BlockSpec((1, 1, D_MODEL), lambda i: (i // tpe, 0, 0))
    tab = pl.BlockSpec((TT, KV_W), lambda i: (i % tpe, 0))
    pad = pl.BlockSpec((4, TT, KV_W), lambda i: (0, i, 0))
    sds = jax.ShapeDtypeStruct
    nb = R // T
    return pl.pallas_call(
        body, name="in_proj_bwd", grid=(R // TT,),
        in_specs=[row(D_MODEL), row(D_MODEL), per_ex, per_ex, _full((1, D_MODEL)), _full(w_in.shape), _full((1, ATT_W)),
                  _full((1, KV_W)), tab, tab, _full((256, 256)), row(ATT_W), row(KV_W), row(ATT_W), pad, pad,
                  row(ATT_W), row(SHIFT_W), row(RWKV_W)],
        out_specs=(row(D_MODEL), row(C_END), per_ex, per_ex, _full((1, D_MODEL)), _full((1, ATT_W)), _full((1, KV_W))),
        out_shape=(sds((R, D_MODEL), F32), sds((R, C_END), MXU_DTYPE), sds((nb, 1, D_MODEL), F32),
                   sds((nb, 1, D_MODEL), F32), sds((1, D_MODEL), F32), sds((1, ATT_W), F32), sds((1, KV_W), F32)),
        compiler_params=_cp(("arbitrary",)),
    )(x2, dy, shift, scale, g_pre, w_in, qg, kg, cos, sin, bd, q_raw, k_raw, dqr, dkp, dvp, d_gatt, d_rin, d_grw)


def _w_in_grad_call(hb, dproj, T):
    R = hb.shape[0]
    TT = min(ROW_TILE, T)
    CB = 1152

    def body(h_ref, d_ref, o_ref):
        g = _dot_tn(h_ref[...], d_ref[...])

        @pl.when(pl.program_id(1) == 0)
        def _():
            o_ref[...] = g

        @pl.when(pl.program_id(1) > 0)
        def _():
            o_ref[...] += g

    return pl.pallas_call(
        body, name="w_in_grad", grid=(C_END // CB, R // TT),
        in_specs=[pl.BlockSpec((TT, D_MODEL), lambda j, i: (i, 0)), pl.BlockSpec((TT, CB), lambda j, i: (i, j))],
        out_specs=pl.BlockSpec((D_MODEL, CB), lambda j, i: (0, j)),
        out_shape=jax.ShapeDtypeStruct((D_MODEL, C_END), F32), compiler_params=_cp(("arbitrary", "arbitrary")),
    )(hb, dproj)


def _adam_call(parts, w, m, v, name, row_tile=None):
    P, M, N = parts.shape
    TM = M if row_tile is None else row_tile

    def body(p_ref, w_ref, m_ref, v_ref, g_o, d_o, m_o, v_o):
        g = p_ref[0].astype(F32)
        for j in range(1, P):
            g = g + p_ref[j].astype(F32)
        m2 = ADAM_B1 * m_ref[...] + (1.0 - ADAM_B1) * g
        v2 = ADAM_B2 * v_ref[...] + (1.0 - ADAM_B2) * jnp.square(g)
        m_hat = m2 / (1.0 - ADAM_B1 ** ADAM_STEP)
        v_hat = v2 / (1.0 - ADAM_B2 ** ADAM_STEP)
        g_o[...] = g
        d_o[...] = -ADAM_LR * (m_hat / (jnp.sqrt(v_hat) + ADAM_EPS) + ADAM_WD * w_ref[...])
        m_o[...] = m2
        v_o[...] = v2

    blk = pl.BlockSpec((TM, N), lambda i: (i, 0))
    return pl.pallas_call(
        body, name=name, grid=(M // TM,),
        in_specs=[pl.BlockSpec((P, TM, N), lambda i: (0, i, 0)), blk, blk, blk], out_specs=(blk,) * 4,
        out_shape=(jax.ShapeDtypeStruct((M, N), F32),) * 4, compiler_params=_cp(("arbitrary",)),
    )(parts, w, m, v)


_SMALL_ROWS = 136


def _pack_small(taps, w_up, w0, a_up, a0):
    flat = jnp.concatenate([taps.reshape(-1), w_up.reshape(-1), w0.reshape(-1), a_up.reshape(-1), a0.reshape(-1)])
    return jnp.pad(flat, (0, _SMALL_ROWS * 128 - flat.shape[0])).reshape(_SMALL_ROWS, 128)


def _unpack_small(packed):
    n = packed.shape[0]
    flat = packed.reshape(n, -1)
    out, o = [], 0
    for shape in ((3, 208), (2, 64, 64), (2, 64), (2, 64, 64), (2, 64)):
        size = 1
        for s in shape:
            size *= s
        out.append(flat[:, o:o + size].reshape((n,) + shape))
        o += size
    return out


def _cols_to_full(blocks):
    nd = blocks.ndim
    moved = jnp.moveaxis(blocks, 0, nd - 2)
    return moved.reshape(moved.shape[:-2] + (moved.shape[-2] * moved.shape[-1],))


def _full_to_cols(full):
    k = full.shape[-1] // NDEV
    return jnp.moveaxis(full.reshape(full.shape[:-1] + (NDEV, k)), -2, 0)


_REP_SIZES = (("g_pre", 1024), ("q_norm_g", 64), ("k_norm_g", 64), ("k_k", 512), ("k_a", 512), ("r_k", 512),
              ("gn_w", 512), ("gn_b", 512), ("g_post", 1024))
_REP_ROWS = 40


def kernel(x, c, w_ada, b_ada, g_pre, w_in, q_norm_g, k_norm_g, shift_taps, w_up, w0, a_up, a0, k_k, k_a, r_k, gn_w, gn_b, w_out, g_post, loss_target, m_w_ada, m_b_ada, m_g_pre, m_w_in, m_q_norm_g, m_k_norm_g, m_shift_taps, m_w_up, m_w0, m_a_up, m_a0, m_k_k, m_k_a, m_r_k, m_gn_w, m_gn_b, m_w_out, m_g_post, v_w_ada, v_b_ada, v_g_pre, v_w_in, v_q_norm_g, v_k_norm_g, v_shift_taps, v_w_up, v_w0, v_a_up, v_a0, v_k_k, v_k_a, v_r_k, v_gn_w, v_gn_b, v_w_out, v_g_post):
    B, T, _ = x.shape
    R = B * T
    me = 4 * lax.axis_index("x") + 2 * lax.axis_index("y") + lax.axis_index("c")
    x2 = x.reshape(R, D_MODEL)
    tgt2 = loss_target.reshape(R, D_MODEL)

    seg = jnp.arange(256) // HEAD_DIM
    bd = (seg[:, None] == seg[None, :]).astype(MXU_DTYPE)
    eye = (jnp.arange(HEAD_DIM)[:, None] == (jnp.arange(RWKV_W) % HEAD_DIM)[None, :])
    eye_b, eye_f = eye.astype(MXU_DTYPE), eye.astype(F32)
    cos, sin = _rope_tables(T)

    c_g, w_in_g, w_out_g, small_g = _exchange(
        [c, w_in[0].astype(MXU_DTYPE), w_out[0].astype(MXU_DTYPE),
         _pack_small(shift_taps[0], w_up[0], w0[0], a_up[0], a0[0])], [False] * 4, "gather_params")
    c_all = c_g.reshape(NDEV * B, D_MODEL)
    w_in_f = _cols_to_full(w_in_g)
    w_out_f = w_out_g.reshape(D_MODEL, D_MODEL)
    taps_b, w_up_b, w0_b, a_up_b, a0_b = _unpack_small(small_g)
    taps_f = jnp.pad(_cols_to_full(taps_b), ((0, 5), (0, 0)))
    w_up_f, a_up_f = _cols_to_full(w_up_b), _cols_to_full(a_up_b)
    w0_f, a0_f = _cols_to_full(w0_b), _cols_to_full(a0_b)
    wup_pad = jnp.pad(w_up_f, ((0, 0), (0, 64), (0, 0))).astype(MXU_DTYPE)
    aup_pad = jnp.pad(a_up_f, ((0, 0), (64, 0), (0, 0))).astype(MXU_DTYPE)

    ncol = w_ada.shape[2]
    b_cols = lax.dynamic_slice(b_ada, (0, me * ncol), (1, ncol))
    mod_cols = _mod_call(c_all, w_ada[0].astype(MXU_DTYPE), b_cols)
    (mod_g,) = _exchange([mod_cols], [False], "gather_mod")
    mod = lax.dynamic_slice(_cols_to_full(mod_g), (me * B, 0), (B, 3 * D_MODEL))
    shift, scale, gate = [mod[:, j * D_MODEL:(j + 1) * D_MODEL].reshape(B, 1, D_MODEL) for j in range(3)]

    qg = jnp.tile(q_norm_g, (1, ATT_W // HEAD_DIM))
    kg = jnp.tile(k_norm_g, (1, KV_W // HEAD_DIM))
    rk_row = r_k.reshape(1, RWKV_W)

    hb, qr, kpad, vpad, q_raw, k_raw, g_att, rin, g_rw = _in_proj_call(
        x2, shift, scale, g_pre, w_in_f, qg, kg, cos, sin, bd, T)
    y_att = _att_fwd_call(qr, kpad, vpad, B, T)
    shifted = _shift_fwd_call(rin, taps_f, T)
    w_s, kt_s, akk_s, kk_s = _rwkv_prep_call(shifted, wup_pad, aup_pad, w0_f, a0_f, k_k, k_a, bd, T)
    sh3 = shifted.reshape(B, T, SHIFT_W)
    r4 = lambda a: a.reshape(2, B, T, RWKV_W)
    y0, y1, st = _scan_fwd_call(r4(w_s), r4(kt_s), r4(akk_s), kk_s.reshape(B, T, RWKV_W), sh3, eye_b, eye_f, bd, B, T)

    (loss_blk, dy, d_yatt, d_gatt, d_ys, d_r2, d_v2, d_kts, d_grw, d_gate, g_wout, g_gpost, g_gnw, g_gnb,
     g_rk) = _out_head_call(x2, tgt2, gate, y_att, g_att, y0.reshape(R, RWKV_W), y1.reshape(R, RWKV_W), shifted, kt_s,
                            g_rw, w_out_f, g_post, gn_w, gn_b, rk_row, bd, T)
    scan_cts = _scan_bwd_call(r4(w_s), r4(kt_s), r4(akk_s), kk_s.reshape(B, T, RWKV_W), sh3,
                              d_ys.reshape(B, T, RWKV_W), st, eye_b, eye_f, bd, B, T)
    scan_cts = [a.reshape(R, RWKV_W) for a in scan_cts]
    d_shifted, g_wup, g_aup, g_w0, g_a0, g_kk, g_ka = _rwkv_prep_bwd_call(
        shifted, scan_cts + [d_r2, d_v2, d_kts], wup_pad, aup_pad, w0_f, a0_f, k_k, k_a, bd, T)
    d_rin, g_taps = _shift_bwd_call(rin, d_shifted, taps_f, T)
    dqr, dkp, dvp = _att_bwd_call(qr, kpad, vpad, d_yatt, B, T)
    grad_x, dproj, d_shift, d_scale, g_gpre, g_qg, g_kg = _in_proj_bwd_call(
        x2, dy, shift, scale, g_pre, w_in_f, qg, kg, cos, sin, bd, q_raw, k_raw, dqr, dkp, dvp, d_gatt, d_rin, d_grw, T)
    g_win = _w_in_grad_call(hb, dproj, T)

    rep = jnp.concatenate([g_gpre.reshape(-1), g_qg.reshape(-1, HEAD_DIM).sum(0), g_kg.reshape(-1, HEAD_DIM).sum(0),
                           g_kk.reshape(-1), g_ka.reshape(-1), g_rk.reshape(-1), g_gnw.reshape(-1), g_gnb.reshape(-1),
                           g_gpost.reshape(-1), loss_blk[0, :1]])
    rep = jnp.pad(rep, (0, _REP_ROWS * 128 - rep.shape[0])).reshape(_REP_ROWS, 128)
    dmod = jnp.concatenate([d_shift, d_scale, d_gate], axis=2).reshape(B, 3 * D_MODEL)
    small_parts = jax.vmap(_pack_small)(_full_to_cols(g_taps[:3]), _full_to_cols(g_wup[:, :64, :]), _full_to_cols(g_w0),
                                        _full_to_cols(g_aup[:, 64:, :]), _full_to_cols(g_a0))
    p_win, p_wout, p_small, dmod_g, rep_g = _exchange(
        [_full_to_cols(g_win).astype(MXU_DTYPE), g_wout.reshape(NDEV, D_MODEL // NDEV, D_MODEL).astype(MXU_DTYPE),
         small_parts, dmod, rep],
        [True, True, True, False, False], "reduce_grads")
    dmod_all = dmod_g.reshape(NDEV * B, 3 * D_MODEL)
    g_wada = _wada_grad_call(c_all, lax.dynamic_slice(dmod_all, (0, me * ncol), (NDEV * B, ncol)))

    res = {}

    def adam(name, parts, w, m, v, row_tile=None):
        shape = w.shape
        two_d = (-1, shape[-1])
        out = _adam_call(parts.reshape((parts.shape[0],) + w.reshape(two_d).shape), w.reshape(two_d), m.reshape(two_d),
                         v.reshape(two_d), "adam_" + name, row_tile)
        res[name] = [o.reshape(shape) for o in out]

    adam("w_ada", g_wada[None], w_ada, m_w_ada, v_w_ada)
    adam("b_ada", dmod_all.reshape(NDEV * B, 1, 3 * D_MODEL), b_ada, m_b_ada, v_b_ada)
    adam("w_in", p_win, w_in, m_w_in, v_w_in, 128)
    adam("w_out", p_wout, w_out, m_w_out, v_w_out)
    taps_p, wup_p, w0_p, aup_p, a0_p = _unpack_small(p_small)
    adam("shift_taps", taps_p, shift_taps, m_shift_taps, v_shift_taps)
    adam("w_up", wup_p, w_up, m_w_up, v_w_up)
    adam("w0", w0_p, w0, m_w0, v_w0)
    adam("a_up", aup_p, a_up, m_a_up, v_a_up)
    adam("a0", a0_p, a0, m_a0, v_a0)
    rep_flat = rep_g.reshape(NDEV, -1)
    off = 0
    given = dict(g_pre=(g_pre, m_g_pre, v_g_pre), q_norm_g=(q_norm_g, m_q_norm_g, v_q_norm_g),
                 k_norm_g=(k_norm_g, m_k_norm_g, v_k_norm_g), k_k=(k_k, m_k_k, v_k_k), k_a=(k_a, m_k_a, v_k_a),
                 r_k=(r_k, m_r_k, v_r_k), gn_w=(gn_w, m_gn_w, v_gn_w), gn_b=(gn_b, m_gn_b, v_gn_b),
                 g_post=(g_post, m_g_post, v_g_post))
    for name, size in _REP_SIZES:
        adam(name, rep_flat[:, off:off + size], *given[name])
        off += size

    loss = jnp.sum(rep_flat[:, off])
    order = ["w_ada", "b_ada", "g_pre", "w_in", "q_norm_g", "k_norm_g", "shift_taps", "w_up", "w0", "a_up", "a0", "k_k",
             "k_a", "r_k", "gn_w", "gn_b", "w_out", "g_post"]
    return (loss, grad_x.reshape(B, T, D_MODEL), *[res[n][0] for n in order], *[res[n][1] for n in order],
            *[res[n][2] for n in order], *[res[n][3] for n in order])
```

```python
import functools

import jax
import jax.numpy as jnp
from jax import lax
from jax.experimental import pallas as pl
from jax.experimental.pallas import tpu as pltpu

F32 = jnp.float32
MXU_DTYPE = jnp.bfloat16
MESH = pl.DeviceIdType.MESH
NDEV = 8

D_MODEL = 1024
HEAD_DIM = 64
ATT_W = 512
KV_W = 128
RWKV_W = 512
LORA_W = 128
SHIFT_W = 3 * RWKV_W + LORA_W
GRID_W = 64
ROPE_THETA = 10000.0
DECAY_SCALE = 0.6065306597126334
NORM_EPS = 1e-6
GN_EPS = 64e-5
L2_EPS = 1e-12
ATT_SCALE = HEAD_DIM ** -0.5
C_Q, C_K, C_V, C_GA, C_RIN, C_GRW, C_END = 0, 512, 640, 768, 1280, 2944, 3456

ADAM_LR, ADAM_B1, ADAM_B2, ADAM_EPS, ADAM_WD, ADAM_STEP = 0.001, 0.9, 0.999, 1e-08, 0.01, 10

ROW_TILE = 256
ATT_TILE_FWD = 256
ATT_TILE_BWD = 512
SCAN_CHUNK = 64
SCAN_UNROLL = 8
VMEM_LIMIT = 56 * 1024 * 1024


def _cp(sem=None):
    return pltpu.CompilerParams(dimension_semantics=sem, vmem_limit_bytes=VMEM_LIMIT)


def _dot(a, b, dims=(((1,), (0,)), ((), ()))):
    return lax.dot_general(a.astype(MXU_DTYPE), b.astype(MXU_DTYPE), dims, preferred_element_type=F32)


def _dot_nt(a, b):
    return _dot(a, b, (((1,), (1,)), ((), ())))


def _dot_tn(a, b):
    return _dot(a, b, (((0,), (0,)), ((), ())))


def _seg_dot(xb, bd):
    n = xb.shape[1]
    if n <= 256:
        return jnp.dot(xb, bd[:n, :n], preferred_element_type=F32)
    parts = [jnp.dot(xb[:, c:c + 256], bd, preferred_element_type=F32) for c in range(0, n, 256)]
    return jnp.concatenate(parts, axis=1)


def _split3(x):
    hi = x.astype(MXU_DTYPE)
    r1 = x - hi.astype(F32)
    mid = r1.astype(MXU_DTYPE)
    lo = (r1 - mid.astype(F32)).astype(MXU_DTYPE)
    return hi, mid, lo


def _segsum_raw(x, bd):
    hi, mid, lo = _split3(x)
    return _seg_dot(hi, bd) + _seg_dot(mid, bd) + _seg_dot(lo, bd)


@jax.custom_vjp
def _segsum_d(x, bd):
    return _segsum_raw(x, bd)


def _segsum_d_fwd(x, bd):
    return _segsum_raw(x, bd), bd


def _segsum_d_bwd(bd, ct):
    return _segsum_raw(ct, bd), jnp.zeros_like(bd)


_segsum_d.defvjp(_segsum_d_fwd, _segsum_d_bwd)


def _rope_tables(T):
    t = jnp.arange(T, dtype=F32)
    row = jnp.floor(t / GRID_W)
    col = t - row * GRID_W
    n_freq = HEAD_DIM // 4
    inv_freq = ROPE_THETA ** (-jnp.arange(n_freq, dtype=F32) / n_freq)
    d = jnp.arange(HEAD_DIM)
    pos = jnp.where((d < HEAD_DIM // 2)[None, :], row[:, None], col[:, None])
    ang = pos * inv_freq[d % n_freq][None, :]
    sign = jnp.where((d % 32) < 16, -1.0, 1.0).astype(F32)[None, :]
    cos = jnp.cos(ang)
    sin = jnp.sin(ang) * sign
    return jnp.tile(cos, (1, 2)), jnp.tile(sin, (1, 2))


def _rope_raw(x, cos, sin):
    n = x.shape[1]
    lane = lax.broadcasted_iota(jnp.int32, (1, n), 1)
    first = (lane % 32) < 16
    partner = jnp.where(first, pltpu.roll(x, n - 16, 1), pltpu.roll(x, 16, 1))
    return x * cos + partner * sin


@jax.custom_vjp
def _rope_d(x, cos, sin):
    return _rope_raw(x, cos, sin)


def _rope_d_fwd(x, cos, sin):
    return _rope_raw(x, cos, sin), (cos, sin)


def _rope_d_bwd(res, ct):
    cos, sin = res
    return _rope_raw(ct, cos, -sin), jnp.zeros_like(cos), jnp.zeros_like(sin)


_rope_d.defvjp(_rope_d_fwd, _rope_d_bwd)


def _rms(x, g):
    return x * lax.rsqrt(jnp.mean(x * x, axis=-1, keepdims=True) + NORM_EPS) * g


def _pre_fn(x, shift, scale, g_pre):
    return _rms(x, g_pre) * (1.0 + scale) + shift


def _qk_fn(q, g, cos, sin, bd, scale, diff):
    segsum = _segsum_d if diff else _segsum_raw
    rope = _rope_d if diff else _rope_raw
    qn = q * lax.rsqrt(segsum(q * q, bd) * (1.0 / HEAD_DIM) + NORM_EPS) * g
    return rope(qn, cos, sin) * scale


def _silu(x):
    return x * jax.nn.sigmoid(x)


def _rwkv_pw(k, pw0, pw1, pa0, pa1, w0, a0, k_k, k_a, bd, diff):
    segsum = _segsum_d if diff else _segsum_raw
    kk = k * k_k
    kk = kk * lax.rsqrt(segsum(kk * kk, bd) + L2_EPS)
    ws, kts, akks = [], [], []
    for z, (pw, pa) in enumerate(((pw0, pa0), (pw1, pa1))):
        w = jnp.exp(-DECAY_SCALE * jax.nn.sigmoid(w0[z:z + 1, :] + pw))
        a = jax.nn.sigmoid(a0[z:z + 1, :] + pa)
        ws.append(w)
        kts.append(k * (1.0 + (a - 1.0) * k_a))
        akks.append(a * kk)
    return ws[0], ws[1], kts[0], kts[1], akks[0], akks[1], kk


def _mix_fn(y_att, g_att, ys, r, v, kts, g_rw, gn_w, gn_b, r_k, bd, diff):
    segsum = _segsum_d if diff else _segsum_raw
    mu = segsum(ys, bd) * (1.0 / HEAD_DIM)
    d = ys - mu
    var = segsum(d * d, bd) * (1.0 / HEAD_DIM)
    yn = d * lax.rsqrt(var + GN_EPS) * gn_w + gn_b
    bonus = segsum(r * kts * r_k, bd) * v
    return y_att * _silu(g_att), (yn + bonus) * _silu(g_rw)


def _loss_fn(out, x, tgt, gate, g_post):
    e = x + gate * _rms(out, g_post) - tgt
    s = jnp.sum(e * e, axis=1, keepdims=True)
    return jnp.sum(s, axis=0, keepdims=True) * (0.5 / D_MODEL)


def _exchange(arrays, modes, name):
    n = len(arrays)
    out_shape = tuple(
        jax.ShapeDtypeStruct(((NDEV,) + tuple(a.shape)) if mode == "all" else tuple(a.shape), a.dtype)
        for a, mode in zip(arrays, modes))
    chips = (4, 2, 6)

    def body(*refs):
        ins, outs = refs[:n], refs[n:2 * n]
        send_sems, recv_sems, local_sems = refs[2 * n:]
        ix, iy, ic = lax.axis_index("x"), lax.axis_index("y"), lax.axis_index("c")
        me = 4 * ix + 2 * iy + ic

        def peer(m):
            px = 1 - ix if (m >> 2) & 1 else ix
            py = 1 - iy if (m >> 1) & 1 else iy
            pc = 1 - ic if m & 1 else ic
            return (px, py, pc), 4 * px + 2 * py + pc

        def copy(k, j, src_ref, slot, to):
            return pltpu.make_async_remote_copy(src_ref=src_ref, dst_ref=outs[k].at[slot], send_sem=send_sems.at[k, j],
                                                recv_sem=recv_sems.at[k, j], device_id=to, device_id_type=MESH)

        local, sends, arrivals, forwards = [], [], [], []
        for k in range(n):
            if modes[k] == "scatter":
                local.append(pltpu.make_async_copy(ins[k].at[me], outs[k].at[me], local_sems.at[k]))
                for m in range(1, NDEV):
                    to, p = peer(m)
                    sends.append(copy(k, m - 1, ins[k].at[p], me, to))
                    arrivals.append(copy(k, m - 1, ins[k].at[p], p, to))
            elif modes[k] == "chips":
                mine = me // 2
                local.append(pltpu.make_async_copy(ins[k].at[mine], outs[k].at[mine], local_sems.at[k]))
                for j, m in enumerate(chips):
                    to, p = peer(m)
                    sends.append(copy(k, j, ins[k].at[p // 2], mine, to))
                    arrivals.append(copy(k, j, ins[k].at[p // 2], p // 2, to))
            else:
                local.append(pltpu.make_async_copy(ins[k], outs[k].at[me], local_sems.at[k]))
                sib, sib_slot = peer(1)
                sends.append(copy(k, 0, ins[k], me, sib))
                for j, m in enumerate(chips):
                    to, p = peer(m)
                    sends.append(copy(k, 1 + j, ins[k], me, to))
                    forwards.append((copy(k, 1 + j, ins[k], p, to), copy(k, 4 + j, outs[k].at[p], p, sib)))
                    arrivals.append(copy(k, 4 + j, ins[k], peer(m ^ 1)[1], sib))
                arrivals.append(copy(k, 0, ins[k], sib_slot, sib))
        for cp in local + sends:
            cp.start()
        for arrived, onward in forwards:
            arrived.wait_recv()
            onward.start()
        for cp in arrivals:
            cp.wait_recv()
        for cp in sends + [onward for _, onward in forwards]:
            cp.wait_send()
        for cp in local:
            cp.wait()

    any_spec = pl.BlockSpec(memory_space=pl.ANY)
    return pl.pallas_call(
        body, name=name, out_shape=out_shape,
        in_specs=[any_spec] * n, out_specs=tuple([any_spec] * n),
        scratch_shapes=[pltpu.SemaphoreType.DMA((n, NDEV - 1)), pltpu.SemaphoreType.DMA((n, NDEV - 1)),
                        pltpu.SemaphoreType.DMA((n,))],
    )(*arrays)


def _pair_sum_call(mine, send, name):
    n = len(mine)

    def body(*refs):
        mine_r, send_r, out_r, land_r = (refs[j * n:(j + 1) * n] for j in range(4))
        send_sems, recv_sems = refs[4 * n:]
        sibling = (lax.axis_index("x"), lax.axis_index("y"), 1 - lax.axis_index("c"))
        swaps = [pltpu.make_async_remote_copy(src_ref=send_r[k], dst_ref=land_r[k], send_sem=send_sems.at[k],
                                              recv_sem=recv_sems.at[k], device_id=sibling, device_id_type=MESH)
                 for k in range(n)]
        for cp in swaps:
            cp.start()
        for k, cp in enumerate(swaps):
            cp.wait()
            out_r[k][...] = (mine_r[k][...].astype(F32) + land_r[k][...].astype(F32)).astype(out_r[k].dtype)

    return pl.pallas_call(
        body, name=name, out_shape=tuple(jax.ShapeDtypeStruct(a.shape, a.dtype) for a in mine),
        scratch_shapes=[pltpu.VMEM(a.shape, a.dtype) for a in mine] + [pltpu.SemaphoreType.DMA((n,)),
                                                                         pltpu.SemaphoreType.DMA((n,))],
        compiler_params=pltpu.CompilerParams(vmem_limit_bytes=VMEM_LIMIT),
    )(*mine, *send)


def _mod_call(c_all, w_ada, b_cols):
    def body(c_ref, w_ref, b_ref, o_ref):
        o_ref[...] = _dot(_silu(c_ref[...]), w_ref[...]) + b_ref[...]

    return pl.pallas_call(body, name="mod_fwd",
                          out_shape=jax.ShapeDtypeStruct((c_all.shape[0], w_ada.shape[1]), F32))(c_all, w_ada, b_cols)


def _wada_grad_call(c_all, dmod_cols):
    def body(c_ref, d_ref, o_ref):
        o_ref[...] = _dot_tn(_silu(c_ref[...]), d_ref[...])

    return pl.pallas_call(body, name="w_ada_grad",
                          out_shape=jax.ShapeDtypeStruct((c_all.shape[1], dmod_cols.shape[1]), F32))(c_all, dmod_cols)


def _full(shape):
    nd = len(shape)
    return pl.BlockSpec(shape, lambda *_: (0,) * nd)


def _in_proj_call(x2, shift, scale, g_pre, w_in, qg, kg, cos, sin, bd, T):
    R = x2.shape[0]
    TT = min(ROW_TILE, T)
    tpe = T // TT

    def body(x_ref, sh_ref, sc_ref, gp_ref, w_ref, qg_ref, kg_ref, cos_ref, sin_ref, bd_ref,
             hb_ref, qr_ref, kpad_ref, vpad_ref, qraw_ref, kraw_ref, gatt_ref, rin_ref, grw_ref):
        h = _pre_fn(x_ref[...], sh_ref[0], sc_ref[0], gp_ref[...])
        hb = h.astype(MXU_DTYPE)
        hb_ref[...] = hb

        def proj(c0, c1):
            return jnp.dot(hb, w_ref[:, c0:c1], preferred_element_type=F32)

        q = proj(C_Q, C_K)
        k = proj(C_K, C_V)
        v = proj(C_V, C_GA)
        gatt_ref[...] = proj(C_GA, C_RIN)
        rin_ref[...] = proj(C_RIN, C_GRW)
        grw_ref[...] = proj(C_GRW, C_END)
        qraw_ref[...] = q
        kraw_ref[...] = k
        cos, sin, bd = cos_ref[...], sin_ref[...], bd_ref[...]
        qr = _qk_fn(q, qg_ref[...], jnp.tile(cos, (1, 4)), jnp.tile(sin, (1, 4)), bd, ATT_SCALE, False)
        qr_ref[...] = qr.astype(MXU_DTYPE)
        kr = _qk_fn(k, kg_ref[...], cos, sin, bd, 1.0, False)
        left = lax.broadcasted_iota(jnp.int32, (1, KV_W), 1) < HEAD_DIM
        for ref, val in ((kpad_ref, kr), (vpad_ref, v)):
            h0l = jnp.where(left, val, 0.0)
            h1r = jnp.where(left, 0.0, val)
            ref[0] = h0l.astype(MXU_DTYPE)
            ref[1] = pltpu.roll(h0l, HEAD_DIM, 1).astype(MXU_DTYPE)
            ref[2] = pltpu.roll(h1r, HEAD_DIM, 1).astype(MXU_DTYPE)
            ref[3] = h1r.astype(MXU_DTYPE)

    row = lambda w: pl.BlockSpec((TT, w), lambda i: (i, 0))
    per_ex = pl.BlockSpec((1, 1, D_MODEL), lambda i: (i // tpe, 0, 0))
    tab = pl.BlockSpec((TT, KV_W), lambda i: (i % tpe, 0))
    pad = pl.BlockSpec((4, TT, KV_W), lambda i: (0, i, 0))
    sds = jax.ShapeDtypeStruct
    return pl.pallas_call(
        body, name="in_proj", grid=(R // TT,),
        in_specs=[row(D_MODEL), per_ex, per_ex, _full((1, D_MODEL)), _full(w_in.shape), _full((1, ATT_W)),
                  _full((1, KV_W)), tab, tab, _full((256, 256))],
        out_specs=(row(D_MODEL), row(ATT_W), pad, pad, row(ATT_W), row(KV_W), row(ATT_W), row(SHIFT_W), row(RWKV_W)),
        out_shape=(sds((R, D_MODEL), MXU_DTYPE), sds((R, ATT_W), MXU_DTYPE), sds((4, R, KV_W), MXU_DTYPE),
                   sds((4, R, KV_W), MXU_DTYPE), sds((R, ATT_W), F32), sds((R, KV_W), F32), sds((R, ATT_W), F32),
                   sds((R, SHIFT_W), F32), sds((R, RWKV_W), F32)),
        compiler_params=_cp(("arbitrary",)),
    )(x2, shift, scale, g_pre, w_in, qg, kg, cos, sin, bd)


def _softmax_parts(s):
    e = jnp.exp(s - jnp.max(s, axis=1, keepdims=True))
    return e, 1.0 / jnp.sum(e, axis=1, keepdims=True)


def _att_specs(T, TQ):
    nq = T // TQ
    qspec = pl.BlockSpec((TQ, KV_W), lambda b, p, i: (b * nq + i, p))
    side = lambda s: pl.BlockSpec((None, T, KV_W), lambda b, p, i: (2 * (p // 2) + s, b, 0))
    return nq, qspec, side


def _att_fwd_call(qr, kpad, vpad, B, T):
    TQ = min(ATT_TILE_FWD, T)
    nq, qspec, side = _att_specs(T, TQ)

    def body(q_ref, kl_ref, kr_ref, vl_ref, vr_ref, o_ref):
        q = q_ref[...]
        ea, inv_a = _softmax_parts(_dot_nt(q, kl_ref[...]))
        eb, inv_b = _softmax_parts(_dot_nt(q, kr_ref[...]))
        o_ref[...] = _dot(ea, vl_ref[...]) * inv_a + _dot(eb, vr_ref[...]) * inv_b

    return pl.pallas_call(
        body, name="att_fwd", grid=(B, 4, nq),
        in_specs=[qspec, side(0), side(1), side(0), side(1)], out_specs=qspec,
        out_shape=jax.ShapeDtypeStruct((B * T, ATT_W), F32),
        compiler_params=_cp(("arbitrary",) * 3),
    )(qr, kpad, kpad, vpad, vpad)


def _att_bwd_call(qr, kpad, vpad, d_o, B, T):
    TQ = min(ATT_TILE_BWD, T)
    nq, qspec, side = _att_specs(T, TQ)

    def body(q_ref, kl_ref, kr_ref, vl_ref, vr_ref, do_ref, dq_ref, dk_ref, dv_ref):
        i = pl.program_id(2)
        q, do = q_ref[...], do_ref[...]
        left = lax.broadcasted_iota(jnp.int32, (1, KV_W), 1) < HEAD_DIM
        dq = jnp.zeros((TQ, KV_W), F32)
        dk = jnp.zeros((T, KV_W), F32)
        dv = jnp.zeros((T, KV_W), F32)
        for k_ref, v_ref, mask in ((kl_ref, vl_ref, left), (kr_ref, vr_ref, jnp.logical_not(left))):
            kk, vv = k_ref[...], v_ref[...]
            e, inv = _softmax_parts(_dot_nt(q, kk))
            dp = _dot_nt(do, vv)
            ds = e * (dp - inv * jnp.sum(e * dp, axis=1, keepdims=True))
            dq = dq + _dot(ds, kk) * inv
            dk = dk + _dot_tn(ds, jnp.where(mask, q * inv, 0.0))
            dv = dv + _dot_tn(e, jnp.where(mask, do * inv, 0.0))
        dq_ref[...] = dq

        @pl.when(i == 0)
        def _():
            dk_ref[...] = dk
            dv_ref[...] = dv

        @pl.when(i > 0)
        def _():
            dk_ref[...] += dk
            dv_ref[...] += dv

    acc = pl.BlockSpec((None, T, KV_W), lambda b, p, i: (p, b, 0))
    sds = jax.ShapeDtypeStruct
    return pl.pallas_call(
        body, name="att_bwd", grid=(B, 4, nq),
        in_specs=[qspec, side(0), side(1), side(0), side(1), qspec], out_specs=(qspec, acc, acc),
        out_shape=(sds((B * T, ATT_W), F32), sds((4, B * T, KV_W), F32), sds((4, B * T, KV_W), F32)),
        compiler_params=_cp(("arbitrary",) * 3),
    )(qr, kpad, kpad, vpad, vpad, d_o)


def _shift_specs(R, T, TT, width):
    tpe = T // TT
    nb8 = R // 8
    cur = pl.BlockSpec((TT, width), lambda i: (i, 0))
    prev = pl.BlockSpec((8, width), lambda i: (jnp.maximum(i * (TT // 8) - 1, 0), 0))
    nxt = pl.BlockSpec((8, width), lambda i: (jnp.minimum((i + 1) * (TT // 8), nb8 - 1), 0))
    return tpe, cur, prev, nxt


def _neighbours(cur, prev8, next8, i, tpe, TT):
    rows = lax.broadcasted_iota(jnp.int32, (TT, 1), 0)
    first = jnp.where(i % tpe == 0, 0.0, 1.0)
    last = jnp.where(i % tpe == tpe - 1, 0.0, 1.0)
    before = jnp.where(rows == 0, prev8[7:8, :] * first, pltpu.roll(cur, 1, 0))
    after = jnp.where(rows == TT - 1, next8[0:1, :] * last, pltpu.roll(cur, TT - 1, 0))
    return before, after


def _shift_fwd_call(x, taps, T):
    R, width = x.shape
    TT = min(ROW_TILE, T)
    tpe, cur, prev, nxt = _shift_specs(R, T, TT, width)

    def body(x_ref, p_ref, n_ref, t_ref, o_ref):
        xc = x_ref[...]
        before, after = _neighbours(xc, p_ref[...], n_ref[...], pl.program_id(0), tpe, TT)
        o_ref[...] = t_ref[0:1, :] * before + t_ref[1:2, :] * xc + t_ref[2:3, :] * after

    return pl.pallas_call(
        body, name="shift_fwd", grid=(R // TT,), in_specs=[cur, prev, nxt, _full(taps.shape)], out_specs=cur,
        out_shape=jax.ShapeDtypeStruct((R, width), F32), compiler_params=_cp(("arbitrary",)),
    )(x, x, x, taps)


def _shift_bwd_call(x, d, taps, T):
    R, width = x.shape
    TT = min(ROW_TILE, T)
    tpe, cur, prev, nxt = _shift_specs(R, T, TT, width)

    def body(x_ref, xp_ref, xn_ref, d_ref, dp_ref, dn_ref, t_ref, dx_ref, dt_ref):
        i = pl.program_id(0)
        xc, dc = x_ref[...], d_ref[...]
        d_before, d_after = _neighbours(dc, dp_ref[...], dn_ref[...], i, tpe, TT)
        dx_ref[...] = t_ref[2:3, :] * d_before + t_ref[1:2, :] * dc + t_ref[0:1, :] * d_after
        x_before, x_after = _neighbours(xc, xp_ref[...], xn_ref[...], i, tpe, TT)
        @pl.when(i == 0)
        def _():
            dt_ref[...] = jnp.zeros_like(dt_ref)

        for j, xs in enumerate((x_before, xc, x_after)):
            dt_ref[j:j + 1, :] += jnp.sum(dc * xs, axis=0, keepdims=True)

    return pl.pallas_call(
        body, name="shift_bwd", grid=(R // TT,),
        in_specs=[cur, prev, nxt, cur, prev, nxt, _full(taps.shape)], out_specs=(cur, _full((8, width))),
        out_shape=(jax.ShapeDtypeStruct((R, width), F32), jax.ShapeDtypeStruct((8, width), F32)),
        compiler_params=_cp(("arbitrary",)),
    )(x, x, x, d, d, d, taps)


def _lora_in(wa):
    lane = lax.broadcasted_iota(jnp.int32, (1, LORA_W), 1)
    return jnp.where(lane < LORA_W // 2, jnp.tanh(wa), wa)


def _rwkv_prep_call(shifted, wup, aup, w0, a0, k_k, k_a, bd, T):
    R = shifted.shape[0]
    TT = min(ROW_TILE, T)

    def body(k_ref, wa_ref, wup_ref, aup_ref, w0_ref, a0_ref, kk_ref, ka_ref, bd_ref, w_o, kt_o, akk_o, kk_o):
        twa = _lora_in(wa_ref[...])
        pre = [_dot(twa, m_ref[z]) for m_ref in (wup_ref, aup_ref) for z in range(2)]
        outs = _rwkv_pw(k_ref[...], pre[0], pre[1], pre[2], pre[3], w0_ref[...], a0_ref[...], kk_ref[...],
                        ka_ref[...], bd_ref[...], False)
        w_o[0], w_o[1], kt_o[0], kt_o[1], akk_o[0], akk_o[1] = outs[:6]
        kk_o[...] = outs[6]

    col = lambda c, w: pl.BlockSpec((TT, w), lambda i: (i, c))
    two = pl.BlockSpec((2, TT, RWKV_W), lambda i: (0, i, 0))
    sds = jax.ShapeDtypeStruct
    return pl.pallas_call(
        body, name="rwkv_prep", grid=(R // TT,),
        in_specs=[col(1, RWKV_W), col(3 * RWKV_W // LORA_W, LORA_W), _full(wup.shape), _full(aup.shape),
                  _full((2, RWKV_W)), _full((2, RWKV_W)), _full((1, RWKV_W)), _full((1, RWKV_W)), _full((256, 256))],
        out_specs=(two, two, two, col(0, RWKV_W)),
        out_shape=(sds((2, R, RWKV_W), F32),) * 3 + (sds((R, RWKV_W), F32),),
        compiler_params=_cp(("arbitrary",)),
    )(shifted, shifted, wup, aup, w0, a0, k_k, k_a, bd)


def _rwkv_prep_bwd_call(shifted, cts, wup, aup, w0, a0, k_k, k_a, bd, T):
    R = shifted.shape[0]
    TT = min(ROW_TILE, T)

    def body(k_ref, wa_ref, dw0, dkt0, dakk0, dkk0, dr0, dv0, dw1, dkt1, dakk1, dkk1, dr1, dv1, dr2_ref, dv2_ref, dkts_ref,
             wup_ref, aup_ref, w0_ref, a0_ref, kk_ref, ka_ref, bd_ref,
             dsh_ref, gwup_ref, gaup_ref, gw0_ref, ga0_ref, gkk_ref, gka_ref):
        dw_ref, dkt_ref, dakk_ref, dkk_ref, dr_ref, dv_ref = ((dw0, dw1), (dkt0, dkt1), (dakk0, dakk1), (dkk0, dkk1),
                                                              (dr0, dr1), (dv0, dv1))
        i = pl.program_id(0)
        wa = wa_ref[...]
        twa = _lora_in(wa)
        pre = [_dot(twa, m_ref[z]) for m_ref in (wup_ref, aup_ref) for z in range(2)]
        fn = functools.partial(_rwkv_pw, bd=bd_ref[...], diff=True)
        _, vjp = jax.vjp(fn, k_ref[...], pre[0], pre[1], pre[2], pre[3], w0_ref[...], a0_ref[...], kk_ref[...],
                         ka_ref[...])
        dkts = dkts_ref[...]
        dk, dpw0, dpw1, dpa0, dpa1, gw0, ga0, gkk, gka = vjp(
            (dw_ref[0][...], dw_ref[1][...], dkt_ref[0][...] + dkts, dkt_ref[1][...] + dkts, dakk_ref[0][...],
             dakk_ref[1][...], dkk_ref[0][...] + dkk_ref[1][...]))
        dtwa = (_dot_nt(dpw0, wup_ref[0]) + _dot_nt(dpw1, wup_ref[1]) + _dot_nt(dpa0, aup_ref[0])
                + _dot_nt(dpa1, aup_ref[1]))
        lane = lax.broadcasted_iota(jnp.int32, (1, LORA_W), 1)
        dsh_ref[:, 0:RWKV_W] = dr_ref[0][...] + dr_ref[1][...] + dr2_ref[...]
        dsh_ref[:, RWKV_W:2 * RWKV_W] = dk
        dsh_ref[:, 2 * RWKV_W:3 * RWKV_W] = dv_ref[0][...] + dv_ref[1][...] + dv2_ref[...]
        dsh_ref[:, 3 * RWKV_W:] = jnp.where(lane < LORA_W // 2, dtwa * (1.0 - twa * twa), dtwa)
        acc = ((gwup_ref.at[0], _dot_tn(twa, dpw0)), (gwup_ref.at[1], _dot_tn(twa, dpw1)),
               (gaup_ref.at[0], _dot_tn(twa, dpa0)), (gaup_ref.at[1], _dot_tn(twa, dpa1)),
               (gw0_ref, gw0), (ga0_ref, ga0), (gkk_ref, gkk), (gka_ref, gka))

        @pl.when(i == 0)
        def _():
            for ref, val in acc:
                ref[...] = val

        @pl.when(i > 0)
        def _():
            for ref, val in acc:
                ref[...] += val

    col = lambda c, w: pl.BlockSpec((TT, w), lambda i: (i, c))
    one = col(0, RWKV_W)
    sds = jax.ShapeDtypeStruct
    return pl.pallas_call(
        body, name="rwkv_prep_bwd", grid=(R // TT,),
        in_specs=[col(1, RWKV_W), col(3 * RWKV_W // LORA_W, LORA_W)] + [one] * 15 + [
                  _full(wup.shape), _full(aup.shape), _full((2, RWKV_W)), _full((2, RWKV_W)), _full((1, RWKV_W)),
                  _full((1, RWKV_W)), _full((256, 256))],
        out_specs=(pl.BlockSpec((TT, SHIFT_W), lambda i: (i, 0)), _full(wup.shape), _full(aup.shape),
                   _full((2, RWKV_W)), _full((2, RWKV_W)), _full((1, RWKV_W)), _full((1, RWKV_W))),
        out_shape=(sds((R, SHIFT_W), F32), sds(wup.shape, F32), sds(aup.shape, F32), sds((2, RWKV_W), F32),
                   sds((2, RWKV_W), F32), sds((1, RWKV_W), F32), sds((1, RWKV_W), F32)),
        compiler_params=_cp(("arbitrary",)),
    )(shifted, shifted, *cts, wup, aup, w0, a0, k_k, k_a, bd)


def _col_lhs(row, eye_b):
    return eye_b * row.astype(MXU_DTYPE)


def _colsum(x):
    return jnp.sum(x, axis=0, keepdims=True)


def _stacked_segsum(tiles, bd):
    res = _seg_dot(jnp.concatenate(tiles, axis=0), bd)
    return [res[j * HEAD_DIM:(j + 1) * HEAD_DIM] for j in range(len(tiles))]


def _scan_specs(B, T, C, nC):
    def blk(z, col, rev):
        idx = (lambda g: (z, 0, nC - 1 - g, col)) if rev else (lambda g: (z, 0, g, col))
        return pl.BlockSpec((None, B, C, RWKV_W), idx)

    def blk3(col, rev):
        idx = (lambda g: (0, nC - 1 - g, col)) if rev else (lambda g: (0, g, col))
        return pl.BlockSpec((B, C, RWKV_W), idx)

    return blk, blk3


def _scan_fwd_call(w, kt, akk, kk, shifted, eye_b, eye_f, bd, B, T):
    C = min(SCAN_CHUNK, T)
    nC = T // C
    blk, blk3 = _scan_specs(B, T, C, nC)

    def body(w0, kt0, akk0, kk0, v0, r0, w1, kt1, akk1, kk1, v1, r1, eb_ref, ef_ref, bd_ref, y0, y1, st, S):
        @pl.when(pl.program_id(0) == 0)
        def _():
            S[...] = jnp.zeros_like(S)

        st[0] = S[...].astype(MXU_DTYPE)
        dirs = ((w0, kt0, akk0, kk0, v0, r0, y0), (w1, kt1, akk1, kk1, v1, r1, y1))

        def step(s, carry):
            for z in range(2):
                row = s if z == 0 else C - 1 - s
                prev = jnp.maximum(s - 1, 0) if z == 0 else jnp.minimum(C - s, C - 1)
                wr, ktr, akkr, kkr, vr, rr, yr = dirs[z]
                tiles = []
                for b in range(B):
                    Sb = st[s, z * B + b]
                    tiles += [Sb * kkr[b, pl.ds(row, 1), :].astype(MXU_DTYPE),
                              _col_lhs(vr[b, pl.ds(row, 1), :], eb_ref[...]),
                              Sb * rr[b, pl.ds(prev, 1), :].astype(MXU_DTYPE)]
                res = _stacked_segsum(tiles, bd_ref[...])
                for b in range(B):
                    c = z * B + b
                    sab, vb, yb = res[3 * b:3 * b + 3]
                    ld = lambda ref: ref[b, pl.ds(row, 1), :]
                    Sn = S[c] * ld(wr) - sab * ld(akkr) + vb * ld(ktr)
                    S[c] = Sn
                    st[s + 1, c] = Sn.astype(MXU_DTYPE)
                    yr[b, pl.ds(prev, 1), :] = _colsum(ef_ref[...] * yb)
            return carry

        lax.fori_loop(0, C, step, 0, unroll=SCAN_UNROLL)
        for z in range(2):
            last = C - 1 if z == 0 else 0
            rr, yr = dirs[z][5], dirs[z][6]
            res = _stacked_segsum([st[C, z * B + b] * rr[b, last:last + 1, :].astype(MXU_DTYPE) for b in range(B)],
                                  bd_ref[...])
            for b in range(B):
                yr[b, last:last + 1, :] = _colsum(ef_ref[...] * res[b])

    ins, specs = [], []
    for z, rev in ((0, False), (1, True)):
        ins += [w, kt, akk, kk, shifted, shifted]
        specs += [blk(z, 0, rev), blk(z, 0, rev), blk(z, 0, rev), blk3(0, rev), blk3(2, rev), blk3(0, rev)]
    sds = jax.ShapeDtypeStruct
    return pl.pallas_call(
        body, name="scan_fwd", grid=(nC,),
        in_specs=specs + [_full((HEAD_DIM, RWKV_W)), _full((HEAD_DIM, RWKV_W)), _full((256, 256))],
        out_specs=(blk3(0, False), blk3(0, True),
                   pl.BlockSpec((None, C + 1, 2 * B, HEAD_DIM, RWKV_W), lambda g: (g, 0, 0, 0, 0))),
        out_shape=(sds((B, T, RWKV_W), F32), sds((B, T, RWKV_W), F32),
                   sds((nC, C + 1, 2 * B, HEAD_DIM, RWKV_W), MXU_DTYPE)),
        scratch_shapes=[pltpu.VMEM((2 * B, HEAD_DIM, RWKV_W), F32)],
        compiler_params=_cp(("arbitrary",)),
    )(*ins, eye_b, eye_f, bd)


def _scan_bwd_call(w, kt, akk, kk, shifted, dys, st, eye_b, eye_f, bd, B, T):
    C = min(SCAN_CHUNK, T)
    nC = T // C
    blk, blk3 = _scan_specs(B, T, C, nC)
    nin = 7

    def body(*refs):
        d0, d1 = refs[:nin], refs[nin:2 * nin]
        st_ref, eb_ref, ef_ref, sel_ref, bd_ref = refs[2 * nin:2 * nin + 5]
        o0, o1 = refs[2 * nin + 5:2 * nin + 11], refs[2 * nin + 11:2 * nin + 17]
        COL, DYC, G = refs[2 * nin + 17:]

        @pl.when(pl.program_id(0) == 0)
        def _():
            G[...] = jnp.zeros_like(G)

        dirs = (d0 + (o0,), d1 + (o1,))

        def column_operands(s, z):
            row = s if z == 0 else C - 1 - s
            _, _, _, kkr, vr, _, dyr, _ = dirs[z]
            tiles = []
            for b in range(B):
                tiles += [st_ref[s, z * B + b] * kkr[b, pl.ds(row, 1), :].astype(MXU_DTYPE),
                          _col_lhs(vr[b, pl.ds(row, 1), :], eb_ref[...]),
                          _col_lhs(dyr[b, pl.ds(row, 1), :], eb_ref[...])]
            return tiles

        def keep_columns(res, z):
            for b in range(B):
                for k in range(3):
                    COL[k, z * B + b] = res[3 * b + k].astype(MXU_DTYPE)
                DYC[z * B + b] = res[3 * b + 2]

        for z in range(2):
            keep_columns(_stacked_segsum(column_operands(C - 1, z), bd_ref[...]), z)

        def bwd(it, carry):
            s = C - 1 - it
            for z in range(2):
                row = s if z == 0 else C - 1 - s
                wr, ktr, akkr, kkr, vr, rr, dyr, (dw_o, dkt_o, dakk_o, dkk_o, dr_o, dv_o) = dirs[z]
                tiles, Gcs = [], []
                for b in range(B):
                    c = z * B + b
                    Gc = G[c] + DYC[c] * rr[b, pl.ds(row, 1), :]
                    Gb = Gc.astype(MXU_DTYPE)
                    Gcs.append((Gc, Gb))
                    tiles += [Gb * akkr[b, pl.ds(row, 1), :].astype(MXU_DTYPE),
                              Gb * ktr[b, pl.ds(row, 1), :].astype(MXU_DTYPE)]
                res = _stacked_segsum(tiles + column_operands(jnp.maximum(s - 1, 0), z), bd_ref[...])
                for b in range(B):
                    c = z * B + b
                    Gc, Gb = Gcs[b]
                    gab, dvb = res[2 * b], res[2 * b + 1]
                    ld = lambda ref: ref[b, pl.ds(row, 1), :]
                    G[c] = Gc * ld(wr) - gab * ld(kkr)
                    Sb = st_ref[s, c]
                    prods = jnp.concatenate([st_ref[s + 1, c] * COL[2, c], Gb * COL[1, c], Gb * Sb, Gb * COL[0, c]], axis=0)
                    sums = jnp.dot(sel_ref[...], prods, preferred_element_type=F32)
                    for k, (ref, sign) in enumerate(((dr_o, 1.0), (dkt_o, 1.0), (dw_o, 1.0), (dakk_o, -1.0))):
                        ref[b, pl.ds(row, 1), :] = sign * sums[k:k + 1, :]
                    dv_o[b, pl.ds(row, 1), :] = _colsum(ef_ref[...] * dvb)
                    dkk_o[b, pl.ds(row, 1), :] = -_colsum(gab * Sb.astype(F32))
                keep_columns(res[2 * B:], z)
            return carry

        lax.fori_loop(0, C, bwd, 0, unroll=SCAN_UNROLL)

    ins, specs = [], []
    for z, rev in ((0, True), (1, False)):
        ins += [w, kt, akk, kk, shifted, shifted, dys]
        specs += [blk(z, 0, rev), blk(z, 0, rev), blk(z, 0, rev), blk3(0, rev), blk3(2, rev), blk3(0, rev), blk3(0, rev)]
    sel = (jnp.arange(16)[:, None] == (jnp.arange(4 * HEAD_DIM) // HEAD_DIM)[None, :]).astype(MXU_DTYPE)
    ins += [st, eye_b, eye_f, sel, bd]
    specs += [pl.BlockSpec((None, C + 1, 2 * B, HEAD_DIM, RWKV_W), lambda g: (nC - 1 - g, 0, 0, 0, 0)),
              _full((HEAD_DIM, RWKV_W)), _full((HEAD_DIM, RWKV_W)), _full(sel.shape), _full((256, 256))]
    sds = jax.ShapeDtypeStruct
    out_specs = tuple(blk3(0, True) for _ in range(6)) + tuple(blk3(0, False) for _ in range(6))
    res = pl.pallas_call(
        body, name="scan_bwd", grid=(nC,), in_specs=specs, out_specs=out_specs,
        out_shape=tuple(sds((B, T, RWKV_W), F32) for _ in range(12)),
        scratch_shapes=[pltpu.VMEM((3, 2 * B, HEAD_DIM, RWKV_W), MXU_DTYPE), pltpu.VMEM((2 * B, HEAD_DIM, RWKV_W), F32),
                        pltpu.VMEM((2 * B, HEAD_DIM, RWKV_W), F32)],
        compiler_params=_cp(("arbitrary",)),
    )(*ins)
    return list(res)


def _out_head_call(x2, tgt2, gate, y_att, g_att, y0, y1, shifted, kt, g_rw, w_out, g_post, gn_w, gn_b, r_k, bd, T):
    R = x2.shape[0]
    TT = min(ROW_TILE, T)
    tpe = T // TT

    def body(x_ref, t_ref, gate_ref, ya_ref, ga_ref, y0_ref, y1_ref, r_ref, v_ref, kt_ref, grw_ref, w_ref, gp_ref,
             gnw_ref, gnb_ref, rk_ref, bd_ref,
             loss_o, dy_o, dya_o, dga_o, dys_o, dr_o, dv_o, dkts_o, dgrw_o, dgate_o, gw_o, ggp_o, ggnw_o, ggnb_o, grk_o):
        i = pl.program_id(0)
        bd = bd_ref[...]
        mix = functools.partial(_mix_fn, bd=bd, diff=True)
        (ma, mr), mix_vjp = jax.vjp(mix, ya_ref[...], ga_ref[...], y0_ref[...] + y1_ref[...], r_ref[...], v_ref[...],
                                    kt_ref[0] + kt_ref[1], grw_ref[...], gnw_ref[...], gnb_ref[...], rk_ref[...])
        out = _dot(ma, w_ref[0:ATT_W, :]) + _dot(mr, w_ref[ATT_W:, :])
        loss, loss_vjp = jax.vjp(_loss_fn, out, x_ref[...], t_ref[...], gate_ref[0], gp_ref[...])
        d_out, dy, _, dgate, dgp = loss_vjp(jnp.ones((1, 1), F32))
        dy_o[...] = dy
        dma = _dot_nt(d_out, w_ref[0:ATT_W, :])
        dmr = _dot_nt(d_out, w_ref[ATT_W:, :])
        dya_o[...], dga_o[...], dys_o[...], dr_o[...], dv_o[...], dkts_o[...], dgrw_o[...], dgnw, dgnb, drk = \
            mix_vjp((dma, dmr))
        gw = jnp.concatenate([_dot_tn(ma, d_out), _dot_tn(mr, d_out)], axis=0)
        acc = ((loss_o, jnp.broadcast_to(loss, (8, 128))), (gw_o, gw), (ggp_o, dgp), (ggnw_o, dgnw), (ggnb_o, dgnb),
               (grk_o, drk))

        @pl.when(i == 0)
        def _():
            for ref, val in acc:
                ref[...] = val

        @pl.when(i > 0)
        def _():
            for ref, val in acc:
                ref[...] += val

        @pl.when(i % tpe == 0)
        def _():
            dgate_o[0] = dgate

        @pl.when(i % tpe > 0)
        def _():
            dgate_o[0] += dgate

    row = lambda w, c=0: pl.BlockSpec((TT, w), lambda i: (i, c))
    two = pl.BlockSpec((2, TT, RWKV_W), lambda i: (0, i, 0))
    per_ex = pl.BlockSpec((1, 1, D_MODEL), lambda i: (i // tpe, 0, 0))
    sds = jax.ShapeDtypeStruct
    r512 = sds((R, RWKV_W), F32)
    return pl.pallas_call(
        body, name="out_head", grid=(R // TT,),
        in_specs=[row(D_MODEL), row(D_MODEL), per_ex, row(ATT_W), row(ATT_W), row(RWKV_W), row(RWKV_W), row(RWKV_W, 0),
                  row(RWKV_W, 2), two,
                  row(RWKV_W), _full(w_out.shape), _full((1, D_MODEL)), _full((1, RWKV_W)), _full((1, RWKV_W)),
                  _full((1, RWKV_W)), _full((256, 256))],
        out_specs=(_full((8, 128)), row(D_MODEL), row(ATT_W), row(ATT_W), row(RWKV_W), row(RWKV_W), row(RWKV_W),
                   row(RWKV_W), row(RWKV_W), per_ex, _full((D_MODEL, D_MODEL)), _full((1, D_MODEL)), _full((1, RWKV_W)),
                   _full((1, RWKV_W)), _full((1, RWKV_W))),
        out_shape=(sds((8, 128), F32), sds((R, D_MODEL), F32), r512, r512, r512, r512, r512, r512, r512,
                   sds((R // T, 1, D_MODEL), F32), sds((D_MODEL, D_MODEL), F32), sds((1, D_MODEL), F32),
                   sds((1, RWKV_W), F32), sds((1, RWKV_W), F32), sds((1, RWKV_W), F32)),
        compiler_params=_cp(("arbitrary",)),
    )(x2, tgt2, gate, y_att, g_att, y0, y1, shifted, shifted, kt, g_rw, w_out, g_post, gn_w, gn_b, r_k, bd)


def _in_proj_bwd_call(x2, dy, shift, scale, g_pre, w_in, qg, kg, cos, sin, bd, q_raw, k_raw, dqr, dkp, dvp,
                      d_gatt, d_rin, d_grw, T):
    R = x2.shape[0]
    TT = min(ROW_TILE, T)
    tpe = T // TT

    def body(x_ref, dy_ref, sh_ref, sc_ref, gp_ref, w_ref, qg_ref, kg_ref, cos_ref, sin_ref, bd_ref, q_ref, k_ref,
             dqr_ref, dkp_ref, dvp_ref, dga_ref, drin_ref, dgrw_ref,
             dx_o, dproj_o, dsh_o, dsc_o, ggp_o, gqg_o, gkg_o):
        i = pl.program_id(0)
        cos, sin, bd = cos_ref[...], sin_ref[...], bd_ref[...]
        left = lax.broadcasted_iota(jnp.int32, (1, KV_W), 1) < HEAD_DIM

        def kv_grad(ref):
            a = ref[0] + ref[1]
            b = ref[2] + ref[3]
            return jnp.where(left, a + pltpu.roll(a, HEAD_DIM, 1), b + pltpu.roll(b, HEAD_DIM, 1))

        qfn = functools.partial(_qk_fn, cos=jnp.tile(cos, (1, 4)), sin=jnp.tile(sin, (1, 4)), bd=bd, scale=ATT_SCALE,
                                diff=True)
        _, q_vjp = jax.vjp(qfn, q_ref[...], qg_ref[...])
        dq, gqg = q_vjp(dqr_ref[...])
        kfn = functools.partial(_qk_fn, cos=cos, sin=sin, bd=bd, scale=1.0, diff=True)
        _, k_vjp = jax.vjp(kfn, k_ref[...], kg_ref[...])
        dk, gkg = k_vjp(kv_grad(dkp_ref))
        pieces = ((C_Q, C_K, dq), (C_K, C_V, dk), (C_V, C_GA, kv_grad(dvp_ref)), (C_GA, C_RIN, dga_ref[...]),
                  (C_RIN, C_GRW, drin_ref[...]), (C_GRW, C_END, dgrw_ref[...]))
        dh = jnp.zeros((TT, D_MODEL), F32)
        for c0, c1, val in pieces:
            vb = val.astype(MXU_DTYPE)
            dproj_o[:, c0:c1] = vb
            dh = dh + _dot_nt(vb, w_ref[:, c0:c1])
        _, pre_vjp = jax.vjp(_pre_fn, x_ref[...], sh_ref[0], sc_ref[0], gp_ref[...])
        dx, dsh, dsc, ggp = pre_vjp(dh)
        dx_o[...] = dx + dy_ref[...]
        acc = ((ggp_o, ggp), (gqg_o, gqg), (gkg_o, gkg))

        @pl.when(i == 0)
        def _():
            for ref, val in acc:
                ref[...] = val

        @pl.when(i > 0)
        def _():
            for ref, val in acc:
                ref[...] += val

        @pl.when(i % tpe == 0)
        def _():
            dsh_o[0] = dsh
            dsc_o[0] = dsc

        @pl.when(i % tpe > 0)
        def _():
            dsh_o[0] += dsh
            dsc_o[0] += dsc

    row = lambda w: pl.BlockSpec((TT, w), lambda i: (i, 0))
    per_ex = pl.BlockSpec((1, 1, D_MODEL), lambda i: (i // tpe, 0, 0))
    tab = pl.BlockSpec((TT, KV_W), lambda i: (i % tpe, 0))
    pad = pl.BlockSpec((4, TT, KV_W), lambda i: (0, i, 0))
    sds = jax.ShapeDtypeStruct
    nb = R // T
    return pl.pallas_call(
        body, name="in_proj_bwd", grid=(R // TT,),
        in_specs=[row(D_MODEL), row(D_MODEL), per_ex, per_ex, _full((1, D_MODEL)), _full(w_in.shape), _full((1, ATT_W)),
                  _full((1, KV_W)), tab, tab, _full((256, 256)), row(ATT_W), row(KV_W), row(ATT_W), pad, pad,
                  row(ATT_W), row(SHIFT_W), row(RWKV_W)],
        out_specs=(row(D_MODEL), row(C_END), per_ex, per_ex, _full((1, D_MODEL)), _full((1, ATT_W)), _full((1, KV_W))),
        out_shape=(sds((R, D_MODEL), F32), sds((R, C_END), MXU_DTYPE), sds((nb, 1, D_MODEL), F32),
                   sds((nb, 1, D_MODEL), F32), sds((1, D_MODEL), F32), sds((1, ATT_W), F32), sds((1, KV_W), F32)),
        compiler_params=_cp(("arbitrary",)),
    )(x2, dy, shift, scale, g_pre, w_in, qg, kg, cos, sin, bd, q_raw, k_raw, dqr, dkp, dvp, d_gatt, d_rin, d_grw)


def _w_in_grad_call(hb, dproj, T):
    R = hb.shape[0]
    TT = min(ROW_TILE, T)
    CB = 1152

    def body(h_ref, d_ref, o_ref):
        g = _dot_tn(h_ref[...], d_ref[...])

        @pl.when(pl.program_id(1) == 0)
        def _():
            o_ref[...] = g

        @pl.when(pl.program_id(1) > 0)
        def _():
            o_ref[...] += g

    return pl.pallas_call(
        body, name="w_in_grad", grid=(C_END // CB, R // TT),
        in_specs=[pl.BlockSpec((TT, D_MODEL), lambda j, i: (i, 0)), pl.BlockSpec((TT, CB), lambda j, i: (i, j))],
        out_specs=pl.BlockSpec((D_MODEL, CB), lambda j, i: (0, j)),
        out_shape=jax.ShapeDtypeStruct((D_MODEL, C_END), F32), compiler_params=_cp(("arbitrary", "arbitrary")),
    )(hb, dproj)


def _adam_call(parts, w, m, v, name, row_tile=None):
    P, M, N = parts.shape
    TM = M if row_tile is None else row_tile

    def body(p_ref, w_ref, m_ref, v_ref, g_o, d_o, m_o, v_o):
        g = p_ref[0].astype(F32)
        for j in range(1, P):
            g = g + p_ref[j].astype(F32)
        m2 = ADAM_B1 * m_ref[...] + (1.0 - ADAM_B1) * g
        v2 = ADAM_B2 * v_ref[...] + (1.0 - ADAM_B2) * jnp.square(g)
        m_hat = m2 / (1.0 - ADAM_B1 ** ADAM_STEP)
        v_hat = v2 / (1.0 - ADAM_B2 ** ADAM_STEP)
        g_o[...] = g
        d_o[...] = -ADAM_LR * (m_hat / (jnp.sqrt(v_hat) + ADAM_EPS) + ADAM_WD * w_ref[...])
        m_o[...] = m2
        v_o[...] = v2

    blk = pl.BlockSpec((TM, N), lambda i: (i, 0))
    return pl.pallas_call(
        body, name=name, grid=(M // TM,),
        in_specs=[pl.BlockSpec((P, TM, N), lambda i: (0, i, 0)), blk, blk, blk], out_specs=(blk,) * 4,
        out_shape=(jax.ShapeDtypeStruct((M, N), F32),) * 4, compiler_params=_cp(("arbitrary",)),
    )(parts, w, m, v)


_SMALL_ROWS = 136


def _pack_small(taps, w_up, w0, a_up, a0):
    flat = jnp.concatenate([taps.reshape(-1), w_up.reshape(-1), w0.reshape(-1), a_up.reshape(-1), a0.reshape(-1)])
    return jnp.pad(flat, (0, _SMALL_ROWS * 128 - flat.shape[0])).reshape(_SMALL_ROWS, 128)


def _unpack_small(packed):
    n = packed.shape[0]
    flat = packed.reshape(n, -1)
    out, o = [], 0
    for shape in ((3, 208), (2, 64, 64), (2, 64), (2, 64, 64), (2, 64)):
        size = 1
        for s in shape:
            size *= s
        out.append(flat[:, o:o + size].reshape((n,) + shape))
        o += size
    return out


def _cols_to_full(blocks):
    nd = blocks.ndim
    moved = jnp.moveaxis(blocks, 0, nd - 2)
    return moved.reshape(moved.shape[:-2] + (moved.shape[-2] * moved.shape[-1],))


def _full_to_cols(full):
    k = full.shape[-1] // NDEV
    return jnp.moveaxis(full.reshape(full.shape[:-1] + (NDEV, k)), -2, 0)


_REP_SIZES = (("g_pre", 1024), ("q_norm_g", 64), ("k_norm_g", 64), ("k_k", 512), ("k_a", 512), ("r_k", 512),
              ("gn_w", 512), ("gn_b", 512), ("g_post", 1024))
_REP_ROWS = 40


def kernel(x, c, w_ada, b_ada, g_pre, w_in, q_norm_g, k_norm_g, shift_taps, w_up, w0, a_up, a0, k_k, k_a, r_k, gn_w, gn_b, w_out, g_post, loss_target, m_w_ada, m_b_ada, m_g_pre, m_w_in, m_q_norm_g, m_k_norm_g, m_shift_taps, m_w_up, m_w0, m_a_up, m_a0, m_k_k, m_k_a, m_r_k, m_gn_w, m_gn_b, m_w_out, m_g_post, v_w_ada, v_b_ada, v_g_pre, v_w_in, v_q_norm_g, v_k_norm_g, v_shift_taps, v_w_up, v_w0, v_a_up, v_a0, v_k_k, v_k_a, v_r_k, v_gn_w, v_gn_b, v_w_out, v_g_post):
    B, T, _ = x.shape
    R = B * T
    me = 4 * lax.axis_index("x") + 2 * lax.axis_index("y") + lax.axis_index("c")
    x2 = x.reshape(R, D_MODEL)
    tgt2 = loss_target.reshape(R, D_MODEL)

    seg = jnp.arange(256) // HEAD_DIM
    bd = (seg[:, None] == seg[None, :]).astype(MXU_DTYPE)
    eye = (jnp.arange(HEAD_DIM)[:, None] == (jnp.arange(RWKV_W) % HEAD_DIM)[None, :])
    eye_b, eye_f = eye.astype(MXU_DTYPE), eye.astype(F32)
    cos, sin = _rope_tables(T)

    c_g, w_in_g, w_out_g, small_g = _exchange(
        [c, w_in[0].astype(MXU_DTYPE), w_out[0].astype(MXU_DTYPE),
         _pack_small(shift_taps[0], w_up[0], w0[0], a_up[0], a0[0])], ["all"] * 4, "gather_params")
    c_all = c_g.reshape(NDEV * B, D_MODEL)
    w_in_f = _cols_to_full(w_in_g)
    w_out_f = w_out_g.reshape(D_MODEL, D_MODEL)
    taps_b, w_up_b, w0_b, a_up_b, a0_b = _unpack_small(small_g)
    taps_f = jnp.pad(_cols_to_full(taps_b), ((0, 5), (0, 0)))
    w_up_f, a_up_f = _cols_to_full(w_up_b), _cols_to_full(a_up_b)
    w0_f, a0_f = _cols_to_full(w0_b), _cols_to_full(a0_b)
    wup_pad = jnp.pad(w_up_f, ((0, 0), (0, 64), (0, 0))).astype(MXU_DTYPE)
    aup_pad = jnp.pad(a_up_f, ((0, 0), (64, 0), (0, 0))).astype(MXU_DTYPE)

    ncol = w_ada.shape[2]
    b_cols = lax.dynamic_slice(b_ada, (0, me * ncol), (1, ncol))
    mod_cols = _mod_call(c_all, w_ada[0].astype(MXU_DTYPE), b_cols)
    (mod_g,) = _exchange([mod_cols], ["all"], "gather_mod")
    mod = lax.dynamic_slice(_cols_to_full(mod_g), (me * B, 0), (B, 3 * D_MODEL))
    shift, scale, gate = [mod[:, j * D_MODEL:(j + 1) * D_MODEL].reshape(B, 1, D_MODEL) for j in range(3)]

    qg = jnp.tile(q_norm_g, (1, ATT_W // HEAD_DIM))
    kg = jnp.tile(k_norm_g, (1, KV_W // HEAD_DIM))
    rk_row = r_k.reshape(1, RWKV_W)

    hb, qr, kpad, vpad, q_raw, k_raw, g_att, rin, g_rw = _in_proj_call(
        x2, shift, scale, g_pre, w_in_f, qg, kg, cos, sin, bd, T)
    y_att = _att_fwd_call(qr, kpad, vpad, B, T)
    shifted = _shift_fwd_call(rin, taps_f, T)
    w_s, kt_s, akk_s, kk_s = _rwkv_prep_call(shifted, wup_pad, aup_pad, w0_f, a0_f, k_k, k_a, bd, T)
    sh3 = shifted.reshape(B, T, SHIFT_W)
    r4 = lambda a: a.reshape(2, B, T, RWKV_W)
    y0, y1, st = _scan_fwd_call(r4(w_s), r4(kt_s), r4(akk_s), kk_s.reshape(B, T, RWKV_W), sh3, eye_b, eye_f, bd, B, T)

    (loss_blk, dy, d_yatt, d_gatt, d_ys, d_r2, d_v2, d_kts, d_grw, d_gate, g_wout, g_gpost, g_gnw, g_gnb,
     g_rk) = _out_head_call(x2, tgt2, gate, y_att, g_att, y0.reshape(R, RWKV_W), y1.reshape(R, RWKV_W), shifted, kt_s,
                            g_rw, w_out_f, g_post, gn_w, gn_b, rk_row, bd, T)
    scan_cts = _scan_bwd_call(r4(w_s), r4(kt_s), r4(akk_s), kk_s.reshape(B, T, RWKV_W), sh3,
                              d_ys.reshape(B, T, RWKV_W), st, eye_b, eye_f, bd, B, T)
    scan_cts = [a.reshape(R, RWKV_W) for a in scan_cts]
    d_shifted, g_wup, g_aup, g_w0, g_a0, g_kk, g_ka = _rwkv_prep_bwd_call(
        shifted, scan_cts + [d_r2, d_v2, d_kts], wup_pad, aup_pad, w0_f, a0_f, k_k, k_a, bd, T)
    d_rin, g_taps = _shift_bwd_call(rin, d_shifted, taps_f, T)
    dqr, dkp, dvp = _att_bwd_call(qr, kpad, vpad, d_yatt, B, T)
    grad_x, dproj, d_shift, d_scale, g_gpre, g_qg, g_kg = _in_proj_bwd_call(
        x2, dy, shift, scale, g_pre, w_in_f, qg, kg, cos, sin, bd, q_raw, k_raw, dqr, dkp, dvp, d_gatt, d_rin, d_grw, T)
    g_win = _w_in_grad_call(hb, dproj, T)

    rep = jnp.concatenate([g_gpre.reshape(-1), g_qg.reshape(-1, HEAD_DIM).sum(0), g_kg.reshape(-1, HEAD_DIM).sum(0),
                           g_kk.reshape(-1), g_ka.reshape(-1), g_rk.reshape(-1), g_gnw.reshape(-1), g_gnb.reshape(-1),
                           g_gpost.reshape(-1), loss_blk[0, :1]])
    rep = jnp.pad(rep, (0, _REP_ROWS * 128 - rep.shape[0])).reshape(_REP_ROWS, 128)
    dmod = jnp.concatenate([d_shift, d_scale, d_gate], axis=2).reshape(B, 3 * D_MODEL)
    small_parts = jax.vmap(_pack_small)(_full_to_cols(g_taps[:3]), _full_to_cols(g_wup[:, :64, :]), _full_to_cols(g_w0),
                                        _full_to_cols(g_aup[:, 64:, :]), _full_to_cols(g_a0))
    core = lax.axis_index("c")
    halves = [a.reshape((NDEV // 2, 2) + a.shape[1:]).astype(MXU_DTYPE)
              for a in (_full_to_cols(g_win), g_wout.reshape(NDEV, D_MODEL // NDEV, D_MODEL))]
    pick = lambda a, j: lax.dynamic_index_in_dim(a, j, axis=1, keepdims=False)
    s_win, s_wout = _pair_sum_call([pick(a, core) for a in halves], [pick(a, 1 - core) for a in halves], "reduce_pair")
    p_win, p_wout, p_small, dmod_g, rep_g = _exchange(
        [s_win, s_wout, small_parts, dmod, rep], ["chips", "chips", "scatter", "all", "all"], "reduce_grads")
    dmod_all = dmod_g.reshape(NDEV * B, 3 * D_MODEL)
    g_wada = _wada_grad_call(c_all, lax.dynamic_slice(dmod_all, (0, me * ncol), (NDEV * B, ncol)))

    res = {}

    def adam(name, parts, w, m, v, row_tile=None):
        shape = w.shape
        two_d = (-1, shape[-1])
        out = _adam_call(parts.reshape((parts.shape[0],) + w.reshape(two_d).shape), w.reshape(two_d), m.reshape(two_d),
                         v.reshape(two_d), "adam_" + name, row_tile)
        res[name] = [o.reshape(shape) for o in out]

    adam("w_ada", g_wada[None], w_ada, m_w_ada, v_w_ada)
    adam("b_ada", dmod_all.reshape(NDEV * B, 1, 3 * D_MODEL), b_ada, m_b_ada, v_b_ada)
    adam("w_in", p_win, w_in, m_w_in, v_w_in, 128)
    adam("w_out", p_wout, w_out, m_w_out, v_w_out)
    taps_p, wup_p, w0_p, aup_p, a0_p = _unpack_small(p_small)
    adam("shift_taps", taps_p, shift_taps, m_shift_taps, v_shift_taps)
    adam("w_up", wup_p, w_up, m_w_up, v_w_up)
    adam("w0", w0_p, w0, m_w0, v_w0)
    adam("a_up", aup_p, a_up, m_a_up, v_a_up)
    adam("a0", a0_p, a0, m_a0, v_a0)
    rep_flat = rep_g.reshape(NDEV, -1)
    off = 0
    given = dict(g_pre=(g_pre, m_g_pre, v_g_pre), q_norm_g=(q_norm_g, m_q_norm_g, v_q_norm_g),
                 k_norm_g=(k_norm_g, m_k_norm_g, v_k_norm_g), k_k=(k_k, m_k_k, v_k_k), k_a=(k_a, m_k_a, v_k_a),
                 r_k=(r_k, m_r_k, v_r_k), gn_w=(gn_w, m_gn_w, v_gn_w), gn_b=(gn_b, m_gn_b, v_gn_b),
                 g_post=(g_post, m_g_post, v_g_post))
    for name, size in _REP_SIZES:
        adam(name, rep_flat[:, off:off + size], *given[name])
        off += size

    loss = jnp.sum(rep_flat[:, off])
    order = ["w_ada", "b_ada", "g_pre", "w_in", "q_norm_g", "k_norm_g", "shift_taps", "w_up", "w0", "a_up", "a0", "k_k",
             "k_a", "r_k", "gn_w", "gn_b", "w_out", "g_post"]
    return (loss, grad_x.reshape(B, T, D_MODEL), *[res[n][0] for n in order], *[res[n][1] for n in order],
            *[res[n][2] for n in order], *[res[n][3] for n in order])
```

```python
import functools

import jax
import jax.numpy as jnp
from jax import lax
from jax.experimental import pallas as pl
from jax.experimental.pallas import tpu as pltpu

F32 = jnp.float32
MXU_DTYPE = jnp.bfloat16
MESH = pl.DeviceIdType.MESH
NDEV = 8

D_MODEL = 1024
HEAD_DIM = 64
ATT_W = 512
KV_W = 128
RWKV_W = 512
LORA_W = 128
SHIFT_W = 3 * RWKV_W + LORA_W
GRID_W = 64
ROPE_THETA = 10000.0
DECAY_SCALE = 0.6065306597126334
NORM_EPS = 1e-6
GN_EPS = 64e-5
L2_EPS = 1e-12
ATT_SCALE = HEAD_DIM ** -0.5
C_Q, C_K, C_V, C_GA, C_RIN, C_GRW, C_END = 0, 512, 640, 768, 1280, 2944, 3456

ADAM_LR, ADAM_B1, ADAM_B2, ADAM_EPS, ADAM_WD, ADAM_STEP = 0.001, 0.9, 0.999, 1e-08, 0.01, 10

ROW_TILE = 256
W_GRAD_ROWS = 1024
ATT_TILE_FWD = 256
ATT_TILE_BWD = 512
SCAN_CHUNK = 64
SCAN_UNROLL = 8
VMEM_LIMIT = 56 * 1024 * 1024


def _cp(sem=None):
    return pltpu.CompilerParams(dimension_semantics=sem, vmem_limit_bytes=VMEM_LIMIT)


def _dot(a, b, dims=(((1,), (0,)), ((), ()))):
    return lax.dot_general(a.astype(MXU_DTYPE), b.astype(MXU_DTYPE), dims, preferred_element_type=F32)


def _dot_nt(a, b):
    return _dot(a, b, (((1,), (1,)), ((), ())))


def _dot_tn(a, b):
    return _dot(a, b, (((0,), (0,)), ((), ())))


def _seg_dot(xb, bd):
    n = xb.shape[1]
    if n <= 256:
        return jnp.dot(xb, bd[:n, :n], preferred_element_type=F32)
    parts = [jnp.dot(xb[:, c:c + 256], bd, preferred_element_type=F32) for c in range(0, n, 256)]
    return jnp.concatenate(parts, axis=1)


def _split3(x):
    hi = x.astype(MXU_DTYPE)
    r1 = x - hi.astype(F32)
    mid = r1.astype(MXU_DTYPE)
    lo = (r1 - mid.astype(F32)).astype(MXU_DTYPE)
    return hi, mid, lo


def _segsum_raw(x, bd):
    hi, mid, lo = _split3(x)
    return _seg_dot(hi, bd) + _seg_dot(mid, bd) + _seg_dot(lo, bd)


@jax.custom_vjp
def _segsum_d(x, bd):
    return _segsum_raw(x, bd)


def _segsum_d_fwd(x, bd):
    return _segsum_raw(x, bd), bd


def _segsum_d_bwd(bd, ct):
    return _segsum_raw(ct, bd), jnp.zeros_like(bd)


_segsum_d.defvjp(_segsum_d_fwd, _segsum_d_bwd)


def _rope_tables(T):
    t = jnp.arange(T, dtype=F32)
    row = jnp.floor(t / GRID_W)
    col = t - row * GRID_W
    n_freq = HEAD_DIM // 4
    inv_freq = ROPE_THETA ** (-jnp.arange(n_freq, dtype=F32) / n_freq)
    d = jnp.arange(HEAD_DIM)
    pos = jnp.where((d < HEAD_DIM // 2)[None, :], row[:, None], col[:, None])
    ang = pos * inv_freq[d % n_freq][None, :]
    sign = jnp.where((d % 32) < 16, -1.0, 1.0).astype(F32)[None, :]
    cos = jnp.cos(ang)
    sin = jnp.sin(ang) * sign
    return jnp.tile(cos, (1, 2)), jnp.tile(sin, (1, 2))


def _rope_raw(x, cos, sin):
    n = x.shape[1]
    lane = lax.broadcasted_iota(jnp.int32, (1, n), 1)
    first = (lane % 32) < 16
    partner = jnp.where(first, pltpu.roll(x, n - 16, 1), pltpu.roll(x, 16, 1))
    return x * cos + partner * sin


@jax.custom_vjp
def _rope_d(x, cos, sin):
    return _rope_raw(x, cos, sin)


def _rope_d_fwd(x, cos, sin):
    return _rope_raw(x, cos, sin), (cos, sin)


def _rope_d_bwd(res, ct):
    cos, sin = res
    return _rope_raw(ct, cos, -sin), jnp.zeros_like(cos), jnp.zeros_like(sin)


_rope_d.defvjp(_rope_d_fwd, _rope_d_bwd)


def _rms(x, g):
    return x * lax.rsqrt(jnp.mean(x * x, axis=-1, keepdims=True) + NORM_EPS) * g


def _pre_fn(x, shift, scale, g_pre):
    return _rms(x, g_pre) * (1.0 + scale) + shift


def _qk_fn(q, g, cos, sin, bd, scale, diff):
    segsum = _segsum_d if diff else _segsum_raw
    rope = _rope_d if diff else _rope_raw
    qn = q * lax.rsqrt(segsum(q * q, bd) * (1.0 / HEAD_DIM) + NORM_EPS) * g
    return rope(qn, cos, sin) * scale


def _silu(x):
    return x * jax.nn.sigmoid(x)


def _rwkv_pw(k, pw0, pw1, pa0, pa1, w0, a0, k_k, k_a, bd, diff):
    segsum = _segsum_d if diff else _segsum_raw
    kk = k * k_k
    kk = kk * lax.rsqrt(segsum(kk * kk, bd) + L2_EPS)
    ws, kts, akks = [], [], []
    for z, (pw, pa) in enumerate(((pw0, pa0), (pw1, pa1))):
        w = jnp.exp(-DECAY_SCALE * jax.nn.sigmoid(w0[z:z + 1, :] + pw))
        a = jax.nn.sigmoid(a0[z:z + 1, :] + pa)
        ws.append(w)
        kts.append(k * (1.0 + (a - 1.0) * k_a))
        akks.append(a * kk)
    return ws[0], ws[1], kts[0], kts[1], akks[0], akks[1], kk


def _mix_fn(y_att, g_att, ys, r, v, kts, g_rw, gn_w, gn_b, r_k, bd, diff):
    segsum = _segsum_d if diff else _segsum_raw
    mu = segsum(ys, bd) * (1.0 / HEAD_DIM)
    d = ys - mu
    var = segsum(d * d, bd) * (1.0 / HEAD_DIM)
    yn = d * lax.rsqrt(var + GN_EPS) * gn_w + gn_b
    bonus = segsum(r * kts * r_k, bd) * v
    return y_att * _silu(g_att), (yn + bonus) * _silu(g_rw)


def _loss_fn(out, x, tgt, gate, g_post):
    e = x + gate * _rms(out, g_post) - tgt
    s = jnp.sum(e * e, axis=1, keepdims=True)
    return jnp.sum(s, axis=0, keepdims=True) * (0.5 / D_MODEL)


def _exchange(arrays, modes, name):
    n = len(arrays)
    out_shape = tuple(
        jax.ShapeDtypeStruct(((NDEV,) + tuple(a.shape)) if mode == "all" else tuple(a.shape), a.dtype)
        for a, mode in zip(arrays, modes))
    chips = (4, 2, 6)

    def body(*refs):
        ins, outs = refs[:n], refs[n:2 * n]
        send_sems, recv_sems, local_sems = refs[2 * n:]
        ix, iy, ic = lax.axis_index("x"), lax.axis_index("y"), lax.axis_index("c")
        me = 4 * ix + 2 * iy + ic

        def peer(m):
            px = 1 - ix if (m >> 2) & 1 else ix
            py = 1 - iy if (m >> 1) & 1 else iy
            pc = 1 - ic if m & 1 else ic
            return (px, py, pc), 4 * px + 2 * py + pc

        def copy(k, j, src_ref, slot, to):
            return pltpu.make_async_remote_copy(src_ref=src_ref, dst_ref=outs[k].at[slot], send_sem=send_sems.at[k, j],
                                                recv_sem=recv_sems.at[k, j], device_id=to, device_id_type=MESH)

        local, sends, arrivals, forwards = [], [], [], []
        for k in range(n):
            if modes[k] == "scatter":
                local.append(pltpu.make_async_copy(ins[k].at[me], outs[k].at[me], local_sems.at[k]))
                for m in range(1, NDEV):
                    to, p = peer(m)
                    sends.append(copy(k, m - 1, ins[k].at[p], me, to))
                    arrivals.append(copy(k, m - 1, ins[k].at[p], p, to))
            elif modes[k] == "chips":
                mine = me // 2
                local.append(pltpu.make_async_copy(ins[k].at[mine], outs[k].at[mine], local_sems.at[k]))
                for j, m in enumerate(chips):
                    to, p = peer(m)
                    sends.append(copy(k, j, ins[k].at[p // 2], mine, to))
                    arrivals.append(copy(k, j, ins[k].at[p // 2], p // 2, to))
            else:
                local.append(pltpu.make_async_copy(ins[k], outs[k].at[me], local_sems.at[k]))
                sib, sib_slot = peer(1)
                sends.append(copy(k, 0, ins[k], me, sib))
                for j, m in enumerate(chips):
                    to, p = peer(m)
                    sends.append(copy(k, 1 + j, ins[k], me, to))
                    forwards.append((copy(k, 1 + j, ins[k], p, to), copy(k, 4 + j, outs[k].at[p], p, sib)))
                    arrivals.append(copy(k, 4 + j, ins[k], peer(m ^ 1)[1], sib))
                arrivals.append(copy(k, 0, ins[k], sib_slot, sib))
        for cp in local + sends:
            cp.start()
        for arrived, onward in forwards:
            arrived.wait_recv()
            onward.start()
        for cp in arrivals:
            cp.wait_recv()
        for cp in sends + [onward for _, onward in forwards]:
            cp.wait_send()
        for cp in local:
            cp.wait()

    any_spec = pl.BlockSpec(memory_space=pl.ANY)
    return pl.pallas_call(
        body, name=name, out_shape=out_shape,
        in_specs=[any_spec] * n, out_specs=tuple([any_spec] * n),
        scratch_shapes=[pltpu.SemaphoreType.DMA((n, NDEV - 1)), pltpu.SemaphoreType.DMA((n, NDEV - 1)),
                        pltpu.SemaphoreType.DMA((n,))],
    )(*arrays)


def _pair_sum_call(mine, send, name):
    n = len(mine)

    def body(*refs):
        mine_r, send_r, out_r, land_r = (refs[j * n:(j + 1) * n] for j in range(4))
        send_sems, recv_sems = refs[4 * n:]
        sibling = (lax.axis_index("x"), lax.axis_index("y"), 1 - lax.axis_index("c"))
        swaps = [pltpu.make_async_remote_copy(src_ref=send_r[k], dst_ref=land_r[k], send_sem=send_sems.at[k],
                                              recv_sem=recv_sems.at[k], device_id=sibling, device_id_type=MESH)
                 for k in range(n)]
        for cp in swaps:
            cp.start()
        for k, cp in enumerate(swaps):
            cp.wait()
            out_r[k][...] = (mine_r[k][...].astype(F32) + land_r[k][...].astype(F32)).astype(out_r[k].dtype)

    return pl.pallas_call(
        body, name=name, out_shape=tuple(jax.ShapeDtypeStruct(a.shape, a.dtype) for a in mine),
        scratch_shapes=[pltpu.VMEM(a.shape, a.dtype) for a in mine] + [pltpu.SemaphoreType.DMA((n,)),
                                                                         pltpu.SemaphoreType.DMA((n,))],
        compiler_params=pltpu.CompilerParams(vmem_limit_bytes=VMEM_LIMIT),
    )(*mine, *send)


def _mod_call(c_all, w_ada, b_cols):
    def body(c_ref, w_ref, b_ref, o_ref):
        o_ref[...] = _dot(_silu(c_ref[...]), w_ref[...]) + b_ref[...]

    return pl.pallas_call(body, name="mod_fwd",
                          out_shape=jax.ShapeDtypeStruct((c_all.shape[0], w_ada.shape[1]), F32))(c_all, w_ada, b_cols)


def _wada_grad_call(c_all, dmod_cols):
    def body(c_ref, d_ref, o_ref):
        o_ref[...] = _dot_tn(_silu(c_ref[...]), d_ref[...])

    return pl.pallas_call(body, name="w_ada_grad",
                          out_shape=jax.ShapeDtypeStruct((c_all.shape[1], dmod_cols.shape[1]), F32))(c_all, dmod_cols)


def _full(shape):
    nd = len(shape)
    return pl.BlockSpec(shape, lambda *_: (0,) * nd)


def _in_proj_call(x2, shift, scale, g_pre, w_in, qg, kg, cos, sin, bd, T):
    R = x2.shape[0]
    TT = min(ROW_TILE, T)
    tpe = T // TT

    def body(x_ref, sh_ref, sc_ref, gp_ref, w_ref, qg_ref, kg_ref, cos_ref, sin_ref, bd_ref,
             hb_ref, qr_ref, kpad_ref, vpad_ref, qraw_ref, kraw_ref, gatt_ref, rin_ref, grw_ref):
        h = _pre_fn(x_ref[...], sh_ref[0], sc_ref[0], gp_ref[...])
        hb = h.astype(MXU_DTYPE)
        hb_ref[...] = hb

        def proj(c0, c1):
            return jnp.dot(hb, w_ref[:, c0:c1], preferred_element_type=F32)

        q = proj(C_Q, C_K)
        k = proj(C_K, C_V)
        v = proj(C_V, C_GA)
        gatt_ref[...] = proj(C_GA, C_RIN)
        rin_ref[...] = proj(C_RIN, C_GRW)
        grw_ref[...] = proj(C_GRW, C_END)
        qraw_ref[...] = q
        kraw_ref[...] = k
        cos, sin, bd = cos_ref[...], sin_ref[...], bd_ref[...]
        qr = _qk_fn(q, qg_ref[...], jnp.tile(cos, (1, 4)), jnp.tile(sin, (1, 4)), bd, ATT_SCALE, False)
        qr_ref[...] = qr.astype(MXU_DTYPE)
        kr = _qk_fn(k, kg_ref[...], cos, sin, bd, 1.0, False)
        left = lax.broadcasted_iota(jnp.int32, (1, KV_W), 1) < HEAD_DIM
        for ref, val in ((kpad_ref, kr), (vpad_ref, v)):
            h0l = jnp.where(left, val, 0.0)
            h1r = jnp.where(left, 0.0, val)
            ref[0] = h0l.astype(MXU_DTYPE)
            ref[1] = pltpu.roll(h0l, HEAD_DIM, 1).astype(MXU_DTYPE)
            ref[2] = pltpu.roll(h1r, HEAD_DIM, 1).astype(MXU_DTYPE)
            ref[3] = h1r.astype(MXU_DTYPE)

    row = lambda w: pl.BlockSpec((TT, w), lambda i: (i, 0))
    per_ex = pl.BlockSpec((1, 1, D_MODEL), lambda i: (i // tpe, 0, 0))
    tab = pl.BlockSpec((TT, KV_W), lambda i: (i % tpe, 0))
    pad = pl.BlockSpec((4, TT, KV_W), lambda i: (0, i, 0))
    sds = jax.ShapeDtypeStruct
    return pl.pallas_call(
        body, name="in_proj", grid=(R // TT,),
        in_specs=[row(D_MODEL), per_ex, per_ex, _full((1, D_MODEL)), _full(w_in.shape), _full((1, ATT_W)),
                  _full((1, KV_W)), tab, tab, _full((256, 256))],
        out_specs=(row(D_MODEL), row(ATT_W), pad, pad, row(ATT_W), row(KV_W), row(ATT_W), row(SHIFT_W), row(RWKV_W)),
        out_shape=(sds((R, D_MODEL), MXU_DTYPE), sds((R, ATT_W), MXU_DTYPE), sds((4, R, KV_W), MXU_DTYPE),
                   sds((4, R, KV_W), MXU_DTYPE), sds((R, ATT_W), F32), sds((R, KV_W), F32), sds((R, ATT_W), F32),
                   sds((R, SHIFT_W), F32), sds((R, RWKV_W), F32)),
        compiler_params=_cp(("arbitrary",)),
    )(x2, shift, scale, g_pre, w_in, qg, kg, cos, sin, bd)


def _softmax_parts(s):
    e = jnp.exp(s - jnp.max(s, axis=1, keepdims=True))
    return e, 1.0 / jnp.sum(e, axis=1, keepdims=True)


def _att_specs(T, TQ):
    nq = T // TQ
    qspec = pl.BlockSpec((TQ, KV_W), lambda b, p, i: (b * nq + i, p))
    side = lambda s: pl.BlockSpec((None, T, KV_W), lambda b, p, i: (2 * (p // 2) + s, b, 0))
    return nq, qspec, side


def _att_fwd_call(qr, kpad, vpad, B, T):
    TQ = min(ATT_TILE_FWD, T)
    nq, qspec, side = _att_specs(T, TQ)

    def body(q_ref, kl_ref, kr_ref, vl_ref, vr_ref, o_ref):
        q = q_ref[...]
        ea, inv_a = _softmax_parts(_dot_nt(q, kl_ref[...]))
        eb, inv_b = _softmax_parts(_dot_nt(q, kr_ref[...]))
        o_ref[...] = _dot(ea, vl_ref[...]) * inv_a + _dot(eb, vr_ref[...]) * inv_b

    return pl.pallas_call(
        body, name="att_fwd", grid=(B, 4, nq),
        in_specs=[qspec, side(0), side(1), side(0), side(1)], out_specs=qspec,
        out_shape=jax.ShapeDtypeStruct((B * T, ATT_W), F32),
        compiler_params=_cp(("arbitrary",) * 3),
    )(qr, kpad, kpad, vpad, vpad)


def _att_bwd_call(qr, kpad, vpad, d_o, B, T):
    TQ = min(ATT_TILE_BWD, T)
    nq, qspec, side = _att_specs(T, TQ)

    def body(q_ref, kl_ref, kr_ref, vl_ref, vr_ref, do_ref, dq_ref, dk_ref, dv_ref):
        i = pl.program_id(2)
        q, do = q_ref[...], do_ref[...]
        left = lax.broadcasted_iota(jnp.int32, (1, KV_W), 1) < HEAD_DIM
        dq = jnp.zeros((TQ, KV_W), F32)
        dk = jnp.zeros((T, KV_W), F32)
        dv = jnp.zeros((T, KV_W), F32)
        for k_ref, v_ref, mask in ((kl_ref, vl_ref, left), (kr_ref, vr_ref, jnp.logical_not(left))):
            kk, vv = k_ref[...], v_ref[...]
            e, inv = _softmax_parts(_dot_nt(q, kk))
            dp = _dot_nt(do, vv)
            ds = e * (dp - inv * jnp.sum(e * dp, axis=1, keepdims=True))
            dq = dq + _dot(ds, kk) * inv
            dk = dk + _dot_tn(ds, jnp.where(mask, q * inv, 0.0))
            dv = dv + _dot_tn(e, jnp.where(mask, do * inv, 0.0))
        dq_ref[...] = dq

        @pl.when(i == 0)
        def _():
            dk_ref[...] = dk
            dv_ref[...] = dv

        @pl.when(i > 0)
        def _():
            dk_ref[...] += dk
            dv_ref[...] += dv

    acc = pl.BlockSpec((None, T, KV_W), lambda b, p, i: (p, b, 0))
    sds = jax.ShapeDtypeStruct
    return pl.pallas_call(
        body, name="att_bwd", grid=(B, 4, nq),
        in_specs=[qspec, side(0), side(1), side(0), side(1), qspec], out_specs=(qspec, acc, acc),
        out_shape=(sds((B * T, ATT_W), F32), sds((4, B * T, KV_W), F32), sds((4, B * T, KV_W), F32)),
        compiler_params=_cp(("arbitrary",) * 3),
    )(qr, kpad, kpad, vpad, vpad, d_o)


def _shift_specs(R, T, TT, width):
    tpe = T // TT
    nb8 = R // 8
    cur = pl.BlockSpec((TT, width), lambda i: (i, 0))
    prev = pl.BlockSpec((8, width), lambda i: (jnp.maximum(i * (TT // 8) - 1, 0), 0))
    nxt = pl.BlockSpec((8, width), lambda i: (jnp.minimum((i + 1) * (TT // 8), nb8 - 1), 0))
    return tpe, cur, prev, nxt


def _neighbours(cur, prev8, next8, i, tpe, TT):
    rows = lax.broadcasted_iota(jnp.int32, (TT, 1), 0)
    first = jnp.where(i % tpe == 0, 0.0, 1.0)
    last = jnp.where(i % tpe == tpe - 1, 0.0, 1.0)
    before = jnp.where(rows == 0, prev8[7:8, :] * first, pltpu.roll(cur, 1, 0))
    after = jnp.where(rows == TT - 1, next8[0:1, :] * last, pltpu.roll(cur, TT - 1, 0))
    return before, after


def _shift_fwd_call(x, taps, T):
    R, width = x.shape
    TT = min(ROW_TILE, T)
    tpe, cur, prev, nxt = _shift_specs(R, T, TT, width)

    def body(x_ref, p_ref, n_ref, t_ref, o_ref):
        xc = x_ref[...]
        before, after = _neighbours(xc, p_ref[...], n_ref[...], pl.program_id(0), tpe, TT)
        o_ref[...] = t_ref[0:1, :] * before + t_ref[1:2, :] * xc + t_ref[2:3, :] * after

    return pl.pallas_call(
        body, name="shift_fwd", grid=(R // TT,), in_specs=[cur, prev, nxt, _full(taps.shape)], out_specs=cur,
        out_shape=jax.ShapeDtypeStruct((R, width), F32), compiler_params=_cp(("arbitrary",)),
    )(x, x, x, taps)


def _shift_bwd_call(x, d, taps, T):
    R, width = x.shape
    TT = min(ROW_TILE, T)
    tpe, cur, prev, nxt = _shift_specs(R, T, TT, width)

    def body(x_ref, xp_ref, xn_ref, d_ref, dp_ref, dn_ref, t_ref, dx_ref, dt_ref):
        i = pl.program_id(0)
        xc, dc = x_ref[...], d_ref[...]
        d_before, d_after = _neighbours(dc, dp_ref[...], dn_ref[...], i, tpe, TT)
        dx_ref[...] = t_ref[2:3, :] * d_before + t_ref[1:2, :] * dc + t_ref[0:1, :] * d_after
        x_before, x_after = _neighbours(xc, xp_ref[...], xn_ref[...], i, tpe, TT)
        @pl.when(i == 0)
        def _():
            dt_ref[...] = jnp.zeros_like(dt_ref)

        for j, xs in enumerate((x_before, xc, x_after)):
            dt_ref[j:j + 1, :] += jnp.sum(dc * xs, axis=0, keepdims=True)

    return pl.pallas_call(
        body, name="shift_bwd", grid=(R // TT,),
        in_specs=[cur, prev, nxt, cur, prev, nxt, _full(taps.shape)], out_specs=(cur, _full((8, width))),
        out_shape=(jax.ShapeDtypeStruct((R, width), F32), jax.ShapeDtypeStruct((8, width), F32)),
        compiler_params=_cp(("arbitrary",)),
    )(x, x, x, d, d, d, taps)


def _lora_in(wa):
    lane = lax.broadcasted_iota(jnp.int32, (1, LORA_W), 1)
    return jnp.where(lane < LORA_W // 2, jnp.tanh(wa), wa)


def _rwkv_prep_call(shifted, wup, aup, w0, a0, k_k, k_a, bd, T):
    R = shifted.shape[0]
    TT = min(ROW_TILE, T)

    def body(k_ref, wa_ref, wup_ref, aup_ref, w0_ref, a0_ref, kk_ref, ka_ref, bd_ref, w_o, kt_o, akk_o, kk_o):
        twa = _lora_in(wa_ref[...])
        pre = [_dot(twa, m_ref[z]) for m_ref in (wup_ref, aup_ref) for z in range(2)]
        outs = _rwkv_pw(k_ref[...], pre[0], pre[1], pre[2], pre[3], w0_ref[...], a0_ref[...], kk_ref[...],
                        ka_ref[...], bd_ref[...], False)
        w_o[0], w_o[1], kt_o[0], kt_o[1], akk_o[0], akk_o[1] = outs[:6]
        kk_o[...] = outs[6]

    col = lambda c, w: pl.BlockSpec((TT, w), lambda i: (i, c))
    two = pl.BlockSpec((2, TT, RWKV_W), lambda i: (0, i, 0))
    sds = jax.ShapeDtypeStruct
    return pl.pallas_call(
        body, name="rwkv_prep", grid=(R // TT,),
        in_specs=[col(1, RWKV_W), col(3 * RWKV_W // LORA_W, LORA_W), _full(wup.shape), _full(aup.shape),
                  _full((2, RWKV_W)), _full((2, RWKV_W)), _full((1, RWKV_W)), _full((1, RWKV_W)), _full((256, 256))],
        out_specs=(two, two, two, col(0, RWKV_W)),
        out_shape=(sds((2, R, RWKV_W), F32),) * 3 + (sds((R, RWKV_W), F32),),
        compiler_params=_cp(("arbitrary",)),
    )(shifted, shifted, wup, aup, w0, a0, k_k, k_a, bd)


def _rwkv_prep_bwd_call(shifted, cts, wup, aup, w0, a0, k_k, k_a, bd, T):
    R = shifted.shape[0]
    TT = min(ROW_TILE, T)

    def body(k_ref, wa_ref, dw0, dkt0, dakk0, dkk0, dr0, dv0, dw1, dkt1, dakk1, dkk1, dr1, dv1, dr2_ref, dv2_ref, dkts_ref,
             wup_ref, aup_ref, w0_ref, a0_ref, kk_ref, ka_ref, bd_ref,
             dsh_ref, gwup_ref, gaup_ref, gw0_ref, ga0_ref, gkk_ref, gka_ref):
        dw_ref, dkt_ref, dakk_ref, dkk_ref, dr_ref, dv_ref = ((dw0, dw1), (dkt0, dkt1), (dakk0, dakk1), (dkk0, dkk1),
                                                              (dr0, dr1), (dv0, dv1))
        i = pl.program_id(0)
        wa = wa_ref[...]
        twa = _lora_in(wa)
        pre = [_dot(twa, m_ref[z]) for m_ref in (wup_ref, aup_ref) for z in range(2)]
        fn = functools.partial(_rwkv_pw, bd=bd_ref[...], diff=True)
        _, vjp = jax.vjp(fn, k_ref[...], pre[0], pre[1], pre[2], pre[3], w0_ref[...], a0_ref[...], kk_ref[...],
                         ka_ref[...])
        dkts = dkts_ref[...]
        dk, dpw0, dpw1, dpa0, dpa1, gw0, ga0, gkk, gka = vjp(
            (dw_ref[0][...], dw_ref[1][...], dkt_ref[0][...] + dkts, dkt_ref[1][...] + dkts, dakk_ref[0][...],
             dakk_ref[1][...], dkk_ref[0][...] + dkk_ref[1][...]))
        dtwa = (_dot_nt(dpw0, wup_ref[0]) + _dot_nt(dpw1, wup_ref[1]) + _dot_nt(dpa0, aup_ref[0])
                + _dot_nt(dpa1, aup_ref[1]))
        lane = lax.broadcasted_iota(jnp.int32, (1, LORA_W), 1)
        dsh_ref[:, 0:RWKV_W] = dr_ref[0][...] + dr_ref[1][...] + dr2_ref[...]
        dsh_ref[:, RWKV_W:2 * RWKV_W] = dk
        dsh_ref[:, 2 * RWKV_W:3 * RWKV_W] = dv_ref[0][...] + dv_ref[1][...] + dv2_ref[...]
        dsh_ref[:, 3 * RWKV_W:] = jnp.where(lane < LORA_W // 2, dtwa * (1.0 - twa * twa), dtwa)
        acc = ((gwup_ref.at[0], _dot_tn(twa, dpw0)), (gwup_ref.at[1], _dot_tn(twa, dpw1)),
               (gaup_ref.at[0], _dot_tn(twa, dpa0)), (gaup_ref.at[1], _dot_tn(twa, dpa1)),
               (gw0_ref, gw0), (ga0_ref, ga0), (gkk_ref, gkk), (gka_ref, gka))

        @pl.when(i == 0)
        def _():
            for ref, val in acc:
                ref[...] = val

        @pl.when(i > 0)
        def _():
            for ref, val in acc:
                ref[...] += val

    col = lambda c, w: pl.BlockSpec((TT, w), lambda i: (i, c))
    one = col(0, RWKV_W)
    sds = jax.ShapeDtypeStruct
    return pl.pallas_call(
        body, name="rwkv_prep_bwd", grid=(R // TT,),
        in_specs=[col(1, RWKV_W), col(3 * RWKV_W // LORA_W, LORA_W)] + [one] * 15 + [
                  _full(wup.shape), _full(aup.shape), _full((2, RWKV_W)), _full((2, RWKV_W)), _full((1, RWKV_W)),
                  _full((1, RWKV_W)), _full((256, 256))],
        out_specs=(pl.BlockSpec((TT, SHIFT_W), lambda i: (i, 0)), _full(wup.shape), _full(aup.shape),
                   _full((2, RWKV_W)), _full((2, RWKV_W)), _full((1, RWKV_W)), _full((1, RWKV_W))),
        out_shape=(sds((R, SHIFT_W), F32), sds(wup.shape, F32), sds(aup.shape, F32), sds((2, RWKV_W), F32),
                   sds((2, RWKV_W), F32), sds((1, RWKV_W), F32), sds((1, RWKV_W), F32)),
        compiler_params=_cp(("arbitrary",)),
    )(shifted, shifted, *cts, wup, aup, w0, a0, k_k, k_a, bd)


def _col_lhs(row, eye_b):
    return eye_b * row.astype(MXU_DTYPE)


def _colsum(x):
    return jnp.sum(x, axis=0, keepdims=True)


def _stacked_segsum(tiles, bd):
    res = _seg_dot(jnp.concatenate(tiles, axis=0), bd)
    return [res[j * HEAD_DIM:(j + 1) * HEAD_DIM] for j in range(len(tiles))]


def _scan_specs(B, T, C, nC):
    def blk(z, col, rev):
        idx = (lambda g: (z, 0, nC - 1 - g, col)) if rev else (lambda g: (z, 0, g, col))
        return pl.BlockSpec((None, B, C, RWKV_W), idx)

    def blk3(col, rev):
        idx = (lambda g: (0, nC - 1 - g, col)) if rev else (lambda g: (0, g, col))
        return pl.BlockSpec((B, C, RWKV_W), idx)

    return blk, blk3


def _scan_fwd_call(w, kt, akk, kk, shifted, eye_b, eye_f, bd, B, T):
    C = min(SCAN_CHUNK, T)
    nC = T // C
    blk, blk3 = _scan_specs(B, T, C, nC)

    def body(w0, kt0, akk0, kk0, v0, r0, w1, kt1, akk1, kk1, v1, r1, eb_ref, ef_ref, bd_ref, y0, y1, st, S):
        @pl.when(pl.program_id(0) == 0)
        def _():
            S[...] = jnp.zeros_like(S)

        st[0] = S[...].astype(MXU_DTYPE)
        dirs = ((w0, kt0, akk0, kk0, v0, r0, y0), (w1, kt1, akk1, kk1, v1, r1, y1))

        def step(s, carry):
            for z in range(2):
                row = s if z == 0 else C - 1 - s
                prev = jnp.maximum(s - 1, 0) if z == 0 else jnp.minimum(C - s, C - 1)
                wr, ktr, akkr, kkr, vr, rr, yr = dirs[z]
                tiles = []
                for b in range(B):
                    Sb = st[s, z * B + b]
                    tiles += [Sb * kkr[b, pl.ds(row, 1), :].astype(MXU_DTYPE),
                              _col_lhs(vr[b, pl.ds(row, 1), :], eb_ref[...]),
                              Sb * rr[b, pl.ds(prev, 1), :].astype(MXU_DTYPE)]
                res = _stacked_segsum(tiles, bd_ref[...])
                for b in range(B):
                    c = z * B + b
                    sab, vb, yb = res[3 * b:3 * b + 3]
                    ld = lambda ref: ref[b, pl.ds(row, 1), :]
                    Sn = S[c] * ld(wr) - sab * ld(akkr) + vb * ld(ktr)
                    S[c] = Sn
                    st[s + 1, c] = Sn.astype(MXU_DTYPE)
                    yr[b, pl.ds(prev, 1), :] = _colsum(ef_ref[...] * yb)
            return carry

        lax.fori_loop(0, C, step, 0, unroll=SCAN_UNROLL)
        for z in range(2):
            last = C - 1 if z == 0 else 0
            rr, yr = dirs[z][5], dirs[z][6]
            res = _stacked_segsum([st[C, z * B + b] * rr[b, last:last + 1, :].astype(MXU_DTYPE) for b in range(B)],
                                  bd_ref[...])
            for b in range(B):
                yr[b, last:last + 1, :] = _colsum(ef_ref[...] * res[b])

    ins, specs = [], []
    for z, rev in ((0, False), (1, True)):
        ins += [w, kt, akk, kk, shifted, shifted]
        specs += [blk(z, 0, rev), blk(z, 0, rev), blk(z, 0, rev), blk3(0, rev), blk3(2, rev), blk3(0, rev)]
    sds = jax.ShapeDtypeStruct
    return pl.pallas_call(
        body, name="scan_fwd", grid=(nC,),
        in_specs=specs + [_full((HEAD_DIM, RWKV_W)), _full((HEAD_DIM, RWKV_W)), _full((256, 256))],
        out_specs=(blk3(0, False), blk3(0, True),
                   pl.BlockSpec((None, C + 1, 2 * B, HEAD_DIM, RWKV_W), lambda g: (g, 0, 0, 0, 0))),
        out_shape=(sds((B, T, RWKV_W), F32), sds((B, T, RWKV_W), F32),
                   sds((nC, C + 1, 2 * B, HEAD_DIM, RWKV_W), MXU_DTYPE)),
        scratch_shapes=[pltpu.VMEM((2 * B, HEAD_DIM, RWKV_W), F32)],
        compiler_params=_cp(("arbitrary",)),
    )(*ins, eye_b, eye_f, bd)


def _scan_bwd_call(w, kt, akk, kk, shifted, dys, st, eye_b, eye_f, bd, B, T):
    C = min(SCAN_CHUNK, T)
    nC = T // C
    blk, blk3 = _scan_specs(B, T, C, nC)
    nin = 7

    def body(*refs):
        d0, d1 = refs[:nin], refs[nin:2 * nin]
        st_ref, eb_ref, ef_ref, sel_ref, bd_ref = refs[2 * nin:2 * nin + 5]
        o0, o1 = refs[2 * nin + 5:2 * nin + 11], refs[2 * nin + 11:2 * nin + 17]
        COL, DYC, G = refs[2 * nin + 17:]

        @pl.when(pl.program_id(0) == 0)
        def _():
            G[...] = jnp.zeros_like(G)

        dirs = (d0 + (o0,), d1 + (o1,))

        def column_operands(s, z):
            row = s if z == 0 else C - 1 - s
            _, _, _, kkr, vr, _, dyr, _ = dirs[z]
            tiles = []
            for b in range(B):
                tiles += [st_ref[s, z * B + b] * kkr[b, pl.ds(row, 1), :].astype(MXU_DTYPE),
                          _col_lhs(vr[b, pl.ds(row, 1), :], eb_ref[...]),
                          _col_lhs(dyr[b, pl.ds(row, 1), :], eb_ref[...])]
            return tiles

        def keep_columns(res, z):
            for b in range(B):
                for k in range(3):
                    COL[k, z * B + b] = res[3 * b + k].astype(MXU_DTYPE)
                DYC[z * B + b] = res[3 * b + 2]

        for z in range(2):
            keep_columns(_stacked_segsum(column_operands(C - 1, z), bd_ref[...]), z)

        def bwd(it, carry):
            s = C - 1 - it
            for z in range(2):
                row = s if z == 0 else C - 1 - s
                wr, ktr, akkr, kkr, vr, rr, dyr, (dw_o, dkt_o, dakk_o, dkk_o, dr_o, dv_o) = dirs[z]
                tiles, Gcs = [], []
                for b in range(B):
                    c = z * B + b
                    Gc = G[c] + DYC[c] * rr[b, pl.ds(row, 1), :]
                    Gb = Gc.astype(MXU_DTYPE)
                    Gcs.append((Gc, Gb))
                    tiles += [Gb * akkr[b, pl.ds(row, 1), :].astype(MXU_DTYPE),
                              Gb * ktr[b, pl.ds(row, 1), :].astype(MXU_DTYPE)]
                res = _stacked_segsum(tiles + column_operands(jnp.maximum(s - 1, 0), z), bd_ref[...])
                for b in range(B):
                    c = z * B + b
                    Gc, Gb = Gcs[b]
                    gab, dvb = res[2 * b], res[2 * b + 1]
                    ld = lambda ref: ref[b, pl.ds(row, 1), :]
                    G[c] = Gc * ld(wr) - gab * ld(kkr)
                    Sb = st_ref[s, c]
                    prods = jnp.concatenate([st_ref[s + 1, c] * COL[2, c], Gb * COL[1, c], Gb * Sb, Gb * COL[0, c]], axis=0)
                    sums = jnp.dot(sel_ref[...], prods, preferred_element_type=F32)
                    for k, (ref, sign) in enumerate(((dr_o, 1.0), (dkt_o, 1.0), (dw_o, 1.0), (dakk_o, -1.0))):
                        ref[b, pl.ds(row, 1), :] = sign * sums[k:k + 1, :]
                    dv_o[b, pl.ds(row, 1), :] = _colsum(ef_ref[...] * dvb)
                    dkk_o[b, pl.ds(row, 1), :] = -_colsum(gab * Sb.astype(F32))
                keep_columns(res[2 * B:], z)
            return carry

        lax.fori_loop(0, C, bwd, 0, unroll=SCAN_UNROLL)

    ins, specs = [], []
    for z, rev in ((0, True), (1, False)):
        ins += [w, kt, akk, kk, shifted, shifted, dys]
        specs += [blk(z, 0, rev), blk(z, 0, rev), blk(z, 0, rev), blk3(0, rev), blk3(2, rev), blk3(0, rev), blk3(0, rev)]
    sel = (jnp.arange(16)[:, None] == (jnp.arange(4 * HEAD_DIM) // HEAD_DIM)[None, :]).astype(MXU_DTYPE)
    ins += [st, eye_b, eye_f, sel, bd]
    specs += [pl.BlockSpec((None, C + 1, 2 * B, HEAD_DIM, RWKV_W), lambda g: (nC - 1 - g, 0, 0, 0, 0)),
              _full((HEAD_DIM, RWKV_W)), _full((HEAD_DIM, RWKV_W)), _full(sel.shape), _full((256, 256))]
    sds = jax.ShapeDtypeStruct
    out_specs = tuple(blk3(0, True) for _ in range(6)) + tuple(blk3(0, False) for _ in range(6))
    res = pl.pallas_call(
        body, name="scan_bwd", grid=(nC,), in_specs=specs, out_specs=out_specs,
        out_shape=tuple(sds((B, T, RWKV_W), F32) for _ in range(12)),
        scratch_shapes=[pltpu.VMEM((3, 2 * B, HEAD_DIM, RWKV_W), MXU_DTYPE), pltpu.VMEM((2 * B, HEAD_DIM, RWKV_W), F32),
                        pltpu.VMEM((2 * B, HEAD_DIM, RWKV_W), F32)],
        compiler_params=_cp(("arbitrary",)),
    )(*ins)
    return list(res)


def _out_head_call(x2, tgt2, gate, y_att, g_att, y0, y1, shifted, kt, g_rw, w_out, g_post, gn_w, gn_b, r_k, bd, T):
    R = x2.shape[0]
    TT = min(ROW_TILE, T)
    tpe = T // TT

    def body(x_ref, t_ref, gate_ref, ya_ref, ga_ref, y0_ref, y1_ref, r_ref, v_ref, kt_ref, grw_ref, w_ref, gp_ref,
             gnw_ref, gnb_ref, rk_ref, bd_ref,
             loss_o, dy_o, dya_o, dga_o, dys_o, dr_o, dv_o, dkts_o, dgrw_o, dgate_o, gw_o, ggp_o, ggnw_o, ggnb_o, grk_o):
        i = pl.program_id(0)
        bd = bd_ref[...]
        mix = functools.partial(_mix_fn, bd=bd, diff=True)
        (ma, mr), mix_vjp = jax.vjp(mix, ya_ref[...], ga_ref[...], y0_ref[...] + y1_ref[...], r_ref[...], v_ref[...],
                                    kt_ref[0] + kt_ref[1], grw_ref[...], gnw_ref[...], gnb_ref[...], rk_ref[...])
        out = _dot(ma, w_ref[0:ATT_W, :]) + _dot(mr, w_ref[ATT_W:, :])
        loss, loss_vjp = jax.vjp(_loss_fn, out, x_ref[...], t_ref[...], gate_ref[0], gp_ref[...])
        d_out, dy, _, dgate, dgp = loss_vjp(jnp.ones((1, 1), F32))
        dy_o[...] = dy
        dma = _dot_nt(d_out, w_ref[0:ATT_W, :])
        dmr = _dot_nt(d_out, w_ref[ATT_W:, :])
        dya_o[...], dga_o[...], dys_o[...], dr_o[...], dv_o[...], dkts_o[...], dgrw_o[...], dgnw, dgnb, drk = \
            mix_vjp((dma, dmr))
        gw = jnp.concatenate([_dot_tn(ma, d_out), _dot_tn(mr, d_out)], axis=0)
        acc = ((loss_o, jnp.broadcast_to(loss, (8, 128))), (gw_o, gw), (ggp_o, dgp), (ggnw_o, dgnw), (ggnb_o, dgnb),
               (grk_o, drk))

        @pl.when(i == 0)
        def _():
            for ref, val in acc:
                ref[...] = val

        @pl.when(i > 0)
        def _():
            for ref, val in acc:
                ref[...] += val

        @pl.when(i % tpe == 0)
        def _():
            dgate_o[0] = dgate

        @pl.when(i % tpe > 0)
        def _():
            dgate_o[0] += dgate

    row = lambda w, c=0: pl.BlockSpec((TT, w), lambda i: (i, c))
    two = pl.BlockSpec((2, TT, RWKV_W), lambda i: (0, i, 0))
    per_ex = pl.BlockSpec((1, 1, D_MODEL), lambda i: (i // tpe, 0, 0))
    sds = jax.ShapeDtypeStruct
    r512 = sds((R, RWKV_W), F32)
    return pl.pallas_call(
        body, name="out_head", grid=(R // TT,),
        in_specs=[row(D_MODEL), row(D_MODEL), per_ex, row(ATT_W), row(ATT_W), row(RWKV_W), row(RWKV_W), row(RWKV_W, 0),
                  row(RWKV_W, 2), two,
                  row(RWKV_W), _full(w_out.shape), _full((1, D_MODEL)), _full((1, RWKV_W)), _full((1, RWKV_W)),
                  _full((1, RWKV_W)), _full((256, 256))],
        out_specs=(_full((8, 128)), row(D_MODEL), row(ATT_W), row(ATT_W), row(RWKV_W), row(RWKV_W), row(RWKV_W),
                   row(RWKV_W), row(RWKV_W), per_ex, _full((D_MODEL, D_MODEL)), _full((1, D_MODEL)), _full((1, RWKV_W)),
                   _full((1, RWKV_W)), _full((1, RWKV_W))),
        out_shape=(sds((8, 128), F32), sds((R, D_MODEL), F32), r512, r512, r512, r512, r512, r512, r512,
                   sds((R // T, 1, D_MODEL), F32), sds((D_MODEL, D_MODEL), F32), sds((1, D_MODEL), F32),
                   sds((1, RWKV_W), F32), sds((1, RWKV_W), F32), sds((1, RWKV_W), F32)),
        compiler_params=_cp(("arbitrary",)),
    )(x2, tgt2, gate, y_att, g_att, y0, y1, shifted, shifted, kt, g_rw, w_out, g_post, gn_w, gn_b, r_k, bd)


def _in_proj_bwd_call(x2, dy, shift, scale, g_pre, w_in, qg, kg, cos, sin, bd, q_raw, k_raw, dqr, dkp, dvp,
                      d_gatt, d_rin, d_grw, T):
    R = x2.shape[0]
    TT = min(ROW_TILE, T)
    tpe = T // TT

    def body(x_ref, dy_ref, sh_ref, sc_ref, gp_ref, w_ref, qg_ref, kg_ref, cos_ref, sin_ref, bd_ref, q_ref, k_ref,
             dqr_ref, dkp_ref, dvp_ref, dga_ref, drin_ref, dgrw_ref,
             dx_o, dproj_o, dsh_o, dsc_o, ggp_o, gqg_o, gkg_o):
        i = pl.program_id(0)
        cos, sin, bd = cos_ref[...], sin_ref[...], bd_ref[...]
        left = lax.broadcasted_iota(jnp.int32, (1, KV_W), 1) < HEAD_DIM

        def kv_grad(ref):
            a = ref[0] + ref[1]
            b = ref[2] + ref[3]
            return jnp.where(left, a + pltpu.roll(a, HEAD_DIM, 1), b + pltpu.roll(b, HEAD_DIM, 1))

        qfn = functools.partial(_qk_fn, cos=jnp.tile(cos, (1, 4)), sin=jnp.tile(sin, (1, 4)), bd=bd, scale=ATT_SCALE,
                                diff=True)
        _, q_vjp = jax.vjp(qfn, q_ref[...], qg_ref[...])
        dq, gqg = q_vjp(dqr_ref[...])
        kfn = functools.partial(_qk_fn, cos=cos, sin=sin, bd=bd, scale=1.0, diff=True)
        _, k_vjp = jax.vjp(kfn, k_ref[...], kg_ref[...])
        dk, gkg = k_vjp(kv_grad(dkp_ref))
        pieces = ((C_Q, C_K, dq), (C_K, C_V, dk), (C_V, C_GA, kv_grad(dvp_ref)), (C_GA, C_RIN, dga_ref[...]),
                  (C_RIN, C_GRW, drin_ref[...]), (C_GRW, C_END, dgrw_ref[...]))
        dh = jnp.zeros((TT, D_MODEL), F32)
        for c0, c1, val in pieces:
            vb = val.astype(MXU_DTYPE)
            dproj_o[:, c0:c1] = vb
            dh = dh + _dot_nt(vb, w_ref[:, c0:c1])
        _, pre_vjp = jax.vjp(_pre_fn, x_ref[...], sh_ref[0], sc_ref[0], gp_ref[...])
        dx, dsh, dsc, ggp = pre_vjp(dh)
        dx_o[...] = dx + dy_ref[...]
        acc = ((ggp_o, ggp), (gqg_o, gqg), (gkg_o, gkg))

        @pl.when(i == 0)
        def _():
            for ref, val in acc:
                ref[...] = val

        @pl.when(i > 0)
        def _():
            for ref, val in acc:
                ref[...] += val

        @pl.when(i % tpe == 0)
        def _():
            dsh_o[0] = dsh
            dsc_o[0] = dsc

        @pl.when(i % tpe > 0)
        def _():
            dsh_o[0] += dsh
            dsc_o[0] += dsc

    row = lambda w: pl.BlockSpec((TT, w), lambda i: (i, 0))
    per_ex = pl.BlockSpec((1, 1, D_MODEL), lambda i: (i // tpe, 0, 0))
    tab = pl.BlockSpec((TT, KV_W), lambda i: (i % tpe, 0))
    pad = pl.BlockSpec((4, TT, KV_W), lambda i: (0, i, 0))
    sds = jax.ShapeDtypeStruct
    nb = R // T
    return pl.pallas_call(
        body, name="in_proj_bwd", grid=(R // TT,),
        in_specs=[row(D_MODEL), row(D_MODEL), per_ex, per_ex, _full((1, D_MODEL)), _full(w_in.shape), _full((1, ATT_W)),
                  _full((1, KV_W)), tab, tab, _full((256, 256)), row(ATT_W), row(KV_W), row(ATT_W), pad, pad,
                  row(ATT_W), row(SHIFT_W), row(RWKV_W)],
        out_specs=(row(D_MODEL), row(C_END), per_ex, per_ex, _full((1, D_MODEL)), _full((1, ATT_W)), _full((1, KV_W))),
        out_shape=(sds((R, D_MODEL), F32), sds((R, C_END), MXU_DTYPE), sds((nb, 1, D_MODEL), F32),
                   sds((nb, 1, D_MODEL), F32), sds((1, D_MODEL), F32), sds((1, ATT_W), F32), sds((1, KV_W), F32)),
        compiler_params=_cp(("arbitrary",)),
    )(x2, dy, shift, scale, g_pre, w_in, qg, kg, cos, sin, bd, q_raw, k_raw, dqr, dkp, dvp, d_gatt, d_rin, d_grw)


def _w_in_grad_call(hb, dproj):
    R = hb.shape[0]
    TT = min(W_GRAD_ROWS, R)
    CB = 1152

    def body(h_ref, d_ref, o_ref):
        g = _dot_tn(h_ref[...], d_ref[...])

        @pl.when(pl.program_id(1) == 0)
        def _():
            o_ref[...] = g

        @pl.when(pl.program_id(1) > 0)
        def _():
            o_ref[...] += g

    return pl.pallas_call(
        body, name="w_in_grad", grid=(C_END // CB, R // TT),
        in_specs=[pl.BlockSpec((TT, D_MODEL), lambda j, i: (i, 0)), pl.BlockSpec((TT, CB), lambda j, i: (i, j))],
        out_specs=pl.BlockSpec((D_MODEL, CB), lambda j, i: (0, j)),
        out_shape=jax.ShapeDtypeStruct((D_MODEL, C_END), F32), compiler_params=_cp(("arbitrary", "arbitrary")),
    )(hb, dproj)


def _adam_call(parts, w, m, v, name, row_tile=None):
    P, M, N = parts.shape
    TM = M if row_tile is None else row_tile

    def body(p_ref, w_ref, m_ref, v_ref, g_o, d_o, m_o, v_o):
        g = p_ref[0].astype(F32)
        for j in range(1, P):
            g = g + p_ref[j].astype(F32)
        m2 = ADAM_B1 * m_ref[...] + (1.0 - ADAM_B1) * g
        v2 = ADAM_B2 * v_ref[...] + (1.0 - ADAM_B2) * jnp.square(g)
        m_hat = m2 / (1.0 - ADAM_B1 ** ADAM_STEP)
        v_hat = v2 / (1.0 - ADAM_B2 ** ADAM_STEP)
        g_o[...] = g
        d_o[...] = -ADAM_LR * (m_hat / (jnp.sqrt(v_hat) + ADAM_EPS) + ADAM_WD * w_ref[...])
        m_o[...] = m2
        v_o[...] = v2

    blk = pl.BlockSpec((TM, N), lambda i: (i, 0))
    return pl.pallas_call(
        body, name=name, grid=(M // TM,),
        in_specs=[pl.BlockSpec((P, TM, N), lambda i: (0, i, 0)), blk, blk, blk], out_specs=(blk,) * 4,
        out_shape=(jax.ShapeDtypeStruct((M, N), F32),) * 4, compiler_params=_cp(("arbitrary",)),
    )(parts, w, m, v)


_SMALL_ROWS = 136


def _pack_small(taps, w_up, w0, a_up, a0):
    flat = jnp.concatenate([taps.reshape(-1), w_up.reshape(-1), w0.reshape(-1), a_up.reshape(-1), a0.reshape(-1)])
    return jnp.pad(flat, (0, _SMALL_ROWS * 128 - flat.shape[0])).reshape(_SMALL_ROWS, 128)


def _unpack_small(packed):
    n = packed.shape[0]
    flat = packed.reshape(n, -1)
    out, o = [], 0
    for shape in ((3, 208), (2, 64, 64), (2, 64), (2, 64, 64), (2, 64)):
        size = 1
        for s in shape:
            size *= s
        out.append(flat[:, o:o + size].reshape((n,) + shape))
        o += size
    return out


def _cols_to_full(blocks):
    nd = blocks.ndim
    moved = jnp.moveaxis(blocks, 0, nd - 2)
    return moved.reshape(moved.shape[:-2] + (moved.shape[-2] * moved.shape[-1],))


def _full_to_cols(full):
    k = full.shape[-1] // NDEV
    return jnp.moveaxis(full.reshape(full.shape[:-1] + (NDEV, k)), -2, 0)


_REP_SIZES = (("g_pre", 1024), ("q_norm_g", 64), ("k_norm_g", 64), ("k_k", 512), ("k_a", 512), ("r_k", 512),
              ("gn_w", 512), ("gn_b", 512), ("g_post", 1024))
_REP_ROWS = 40


def kernel(x, c, w_ada, b_ada, g_pre, w_in, q_norm_g, k_norm_g, shift_taps, w_up, w0, a_up, a0, k_k, k_a, r_k, gn_w, gn_b, w_out, g_post, loss_target, m_w_ada, m_b_ada, m_g_pre, m_w_in, m_q_norm_g, m_k_norm_g, m_shift_taps, m_w_up, m_w0, m_a_up, m_a0, m_k_k, m_k_a, m_r_k, m_gn_w, m_gn_b, m_w_out, m_g_post, v_w_ada, v_b_ada, v_g_pre, v_w_in, v_q_norm_g, v_k_norm_g, v_shift_taps, v_w_up, v_w0, v_a_up, v_a0, v_k_k, v_k_a, v_r_k, v_gn_w, v_gn_b, v_w_out, v_g_post):
    B, T, _ = x.shape
    R = B * T
    me = 4 * lax.axis_index("x") + 2 * lax.axis_index("y") + lax.axis_index("c")
    x2 = x.reshape(R, D_MODEL)
    tgt2 = loss_target.reshape(R, D_MODEL)

    seg = jnp.arange(256) // HEAD_DIM
    bd = (seg[:, None] == seg[None, :]).astype(MXU_DTYPE)
    eye = (jnp.arange(HEAD_DIM)[:, None] == (jnp.arange(RWKV_W) % HEAD_DIM)[None, :])
    eye_b, eye_f = eye.astype(MXU_DTYPE), eye.astype(F32)
    cos, sin = _rope_tables(T)

    c_g, w_in_g, w_out_g, small_g = _exchange(
        [c, w_in[0].astype(MXU_DTYPE), w_out[0].astype(MXU_DTYPE),
         _pack_small(shift_taps[0], w_up[0], w0[0], a_up[0], a0[0])], ["all"] * 4, "gather_params")
    c_all = c_g.reshape(NDEV * B, D_MODEL)
    w_in_f = _cols_to_full(w_in_g)
    w_out_f = w_out_g.reshape(D_MODEL, D_MODEL)
    taps_b, w_up_b, w0_b, a_up_b, a0_b = _unpack_small(small_g)
    taps_f = jnp.pad(_cols_to_full(taps_b), ((0, 5), (0, 0)))
    w_up_f, a_up_f = _cols_to_full(w_up_b), _cols_to_full(a_up_b)
    w0_f, a0_f = _cols_to_full(w0_b), _cols_to_full(a0_b)
    wup_pad = jnp.pad(w_up_f, ((0, 0), (0, 64), (0, 0))).astype(MXU_DTYPE)
    aup_pad = jnp.pad(a_up_f, ((0, 0), (64, 0), (0, 0))).astype(MXU_DTYPE)

    ncol = w_ada.shape[2]
    b_cols = lax.dynamic_slice(b_ada, (0, me * ncol), (1, ncol))
    mod_cols = _mod_call(c_all, w_ada[0].astype(MXU_DTYPE), b_cols)
    (mod_g,) = _exchange([mod_cols], ["all"], "gather_mod")
    mod = lax.dynamic_slice(_cols_to_full(mod_g), (me * B, 0), (B, 3 * D_MODEL))
    shift, scale, gate = [mod[:, j * D_MODEL:(j + 1) * D_MODEL].reshape(B, 1, D_MODEL) for j in range(3)]

    qg = jnp.tile(q_norm_g, (1, ATT_W // HEAD_DIM))
    kg = jnp.tile(k_norm_g, (1, KV_W // HEAD_DIM))
    rk_row = r_k.reshape(1, RWKV_W)

    hb, qr, kpad, vpad, q_raw, k_raw, g_att, rin, g_rw = _in_proj_call(
        x2, shift, scale, g_pre, w_in_f, qg, kg, cos, sin, bd, T)
    y_att = _att_fwd_call(qr, kpad, vpad, B, T)
    shifted = _shift_fwd_call(rin, taps_f, T)
    w_s, kt_s, akk_s, kk_s = _rwkv_prep_call(shifted, wup_pad, aup_pad, w0_f, a0_f, k_k, k_a, bd, T)
    sh3 = shifted.reshape(B, T, SHIFT_W)
    r4 = lambda a: a.reshape(2, B, T, RWKV_W)
    y0, y1, st = _scan_fwd_call(r4(w_s), r4(kt_s), r4(akk_s), kk_s.reshape(B, T, RWKV_W), sh3, eye_b, eye_f, bd, B, T)

    (loss_blk, dy, d_yatt, d_gatt, d_ys, d_r2, d_v2, d_kts, d_grw, d_gate, g_wout, g_gpost, g_gnw, g_gnb,
     g_rk) = _out_head_call(x2, tgt2, gate, y_att, g_att, y0.reshape(R, RWKV_W), y1.reshape(R, RWKV_W), shifted, kt_s,
                            g_rw, w_out_f, g_post, gn_w, gn_b, rk_row, bd, T)
    scan_cts = _scan_bwd_call(r4(w_s), r4(kt_s), r4(akk_s), kk_s.reshape(B, T, RWKV_W), sh3,
                              d_ys.reshape(B, T, RWKV_W), st, eye_b, eye_f, bd, B, T)
    scan_cts = [a.reshape(R, RWKV_W) for a in scan_cts]
    d_shifted, g_wup, g_aup, g_w0, g_a0, g_kk, g_ka = _rwkv_prep_bwd_call(
        shifted, scan_cts + [d_r2, d_v2, d_kts], wup_pad, aup_pad, w0_f, a0_f, k_k, k_a, bd, T)
    d_rin, g_taps = _shift_bwd_call(rin, d_shifted, taps_f, T)
    dqr, dkp, dvp = _att_bwd_call(qr, kpad, vpad, d_yatt, B, T)
    grad_x, dproj, d_shift, d_scale, g_gpre, g_qg, g_kg = _in_proj_bwd_call(
        x2, dy, shift, scale, g_pre, w_in_f, qg, kg, cos, sin, bd, q_raw, k_raw, dqr, dkp, dvp, d_gatt, d_rin, d_grw, T)
    g_win = _w_in_grad_call(hb, dproj)

    rep = jnp.concatenate([g_gpre.reshape(-1), g_qg.reshape(-1, HEAD_DIM).sum(0), g_kg.reshape(-1, HEAD_DIM).sum(0),
                           g_kk.reshape(-1), g_ka.reshape(-1), g_rk.reshape(-1), g_gnw.reshape(-1), g_gnb.reshape(-1),
                           g_gpost.reshape(-1), loss_blk[0, :1]])
    rep = jnp.pad(rep, (0, _REP_ROWS * 128 - rep.shape[0])).reshape(_REP_ROWS, 128)
    dmod = jnp.concatenate([d_shift, d_scale, d_gate], axis=2).reshape(B, 3 * D_MODEL)
    small_parts = jax.vmap(_pack_small)(_full_to_cols(g_taps[:3]), _full_to_cols(g_wup[:, :64, :]), _full_to_cols(g_w0),
                                        _full_to_cols(g_aup[:, 64:, :]), _full_to_cols(g_a0))
    core = lax.axis_index("c")
    halves = [a.reshape((NDEV // 2, 2) + a.shape[1:]).astype(MXU_DTYPE)
              for a in (_full_to_cols(g_win), g_wout.reshape(NDEV, D_MODEL // NDEV, D_MODEL))]
    pick = lambda a, j: lax.dynamic_index_in_dim(a, j, axis=1, keepdims=False)
    s_win, s_wout = _pair_sum_call([pick(a, core) for a in halves], [pick(a, 1 - core) for a in halves], "reduce_pair")
    p_win, p_wout, p_small, dmod_g, rep_g = _exchange(
        [s_win, s_wout, small_parts, dmod, rep], ["chips", "chips", "scatter", "all", "all"], "reduce_grads")
    dmod_all = dmod_g.reshape(NDEV * B, 3 * D_MODEL)
    g_wada = _wada_grad_call(c_all, lax.dynamic_slice(dmod_all, (0, me * ncol), (NDEV * B, ncol)))

    res = {}

    def adam(name, parts, w, m, v, row_tile=None):
        shape = w.shape
        two_d = (-1, shape[-1])
        out = _adam_call(parts.reshape((parts.shape[0],) + w.reshape(two_d).shape), w.reshape(two_d), m.reshape(two_d),
                         v.reshape(two_d), "adam_" + name, row_tile)
        res[name] = [o.reshape(shape) for o in out]

    adam("w_ada", g_wada[None], w_ada, m_w_ada, v_w_ada)
    adam("b_ada", dmod_all.reshape(NDEV * B, 1, 3 * D_MODEL), b_ada, m_b_ada, v_b_ada)
    adam("w_in", p_win, w_in, m_w_in, v_w_in, 128)
    adam("w_out", p_wout, w_out, m_w_out, v_w_out)
    taps_p, wup_p, w0_p, aup_p, a0_p = _unpack_small(p_small)
    adam("shift_taps", taps_p, shift_taps, m_shift_taps, v_shift_taps)
    adam("w_up", wup_p, w_up, m_w_up, v_w_up)
    adam("w0", w0_p, w0, m_w0, v_w0)
    adam("a_up", aup_p, a_up, m_a_up, v_a_up)
    adam("a0", a0_p, a0, m_a0, v_a0)
    rep_flat = rep_g.reshape(NDEV, -1)
    off = 0
    given = dict(g_pre=(g_pre, m_g_pre, v_g_pre), q_norm_g=(q_norm_g, m_q_norm_g, v_q_norm_g),
                 k_norm_g=(k_norm_g, m_k_norm_g, v_k_norm_g), k_k=(k_k, m_k_k, v_k_k), k_a=(k_a, m_k_a, v_k_a),
                 r_k=(r_k, m_r_k, v_r_k), gn_w=(gn_w, m_gn_w, v_gn_w), gn_b=(gn_b, m_gn_b, v_gn_b),
                 g_post=(g_post, m_g_post, v_g_post))
    for name, size in _REP_SIZES:
        adam(name, rep_flat[:, off:off + size], *given[name])
        off += size

    loss = jnp.sum(rep_flat[:, off])
    order = ["w_ada", "b_ada", "g_pre", "w_in", "q_norm_g", "k_norm_g", "shift_taps", "w_up", "w0", "a_up", "a0", "k_k",
             "k_a", "r_k", "gn_w", "gn_b", "w_out", "g_post"]
    return (loss, grad_x.reshape(B, T, D_MODEL), *[res[n][0] for n in order], *[res[n][1] for n in order],
            *[res[n][2] for n in order], *[res[n][3] for n in order])
```

```python
import functools

import jax
import jax.numpy as jnp
from jax import lax
from jax.experimental import pallas as pl
from jax.experimental.pallas import tpu as pltpu

F32 = jnp.float32
MXU_DTYPE = jnp.bfloat16
MESH = pl.DeviceIdType.MESH
NDEV = 8

D_MODEL = 1024
HEAD_DIM = 64
ATT_W = 512
KV_W = 128
RWKV_W = 512
LORA_W = 128
SHIFT_W = 3 * RWKV_W + LORA_W
GRID_W = 64
ROPE_THETA = 10000.0
DECAY_SCALE = 0.6065306597126334
NORM_EPS = 1e-6
GN_EPS = 64e-5
L2_EPS = 1e-12
ATT_SCALE = HEAD_DIM ** -0.5
C_Q, C_K, C_V, C_GA, C_RIN, C_GRW, C_END = 0, 512, 640, 768, 1280, 2944, 3456

ADAM_LR, ADAM_B1, ADAM_B2, ADAM_EPS, ADAM_WD, ADAM_STEP = 0.001, 0.9, 0.999, 1e-08, 0.01, 10

ROW_TILE = 256
W_GRAD_ROWS = 1024
ATT_TILE_FWD = 256
ATT_TILE_BWD = 512
SCAN_CHUNK = 64
SCAN_UNROLL = 8
VMEM_LIMIT = 56 * 1024 * 1024


def _cp(sem=None):
    return pltpu.CompilerParams(dimension_semantics=sem, vmem_limit_bytes=VMEM_LIMIT)


def _dot(a, b, dims=(((1,), (0,)), ((), ()))):
    return lax.dot_general(a.astype(MXU_DTYPE), b.astype(MXU_DTYPE), dims, preferred_element_type=F32)


def _dot_nt(a, b):
    return _dot(a, b, (((1,), (1,)), ((), ())))


def _dot_tn(a, b):
    return _dot(a, b, (((0,), (0,)), ((), ())))


def _seg_dot(xb, bd):
    n = xb.shape[1]
    if n <= 256:
        return jnp.dot(xb, bd[:n, :n], preferred_element_type=F32)
    parts = [jnp.dot(xb[:, c:c + 256], bd, preferred_element_type=F32) for c in range(0, n, 256)]
    return jnp.concatenate(parts, axis=1)


def _split3(x):
    hi = x.astype(MXU_DTYPE)
    r1 = x - hi.astype(F32)
    mid = r1.astype(MXU_DTYPE)
    lo = (r1 - mid.astype(F32)).astype(MXU_DTYPE)
    return hi, mid, lo


def _segsum_raw(x, bd):
    hi, mid, lo = _split3(x)
    return _seg_dot(hi, bd) + _seg_dot(mid, bd) + _seg_dot(lo, bd)


@jax.custom_vjp
def _segsum_d(x, bd):
    return _segsum_raw(x, bd)


def _segsum_d_fwd(x, bd):
    return _segsum_raw(x, bd), bd


def _segsum_d_bwd(bd, ct):
    return _segsum_raw(ct, bd), jnp.zeros_like(bd)


_segsum_d.defvjp(_segsum_d_fwd, _segsum_d_bwd)


def _rope_tables(T):
    t = jnp.arange(T, dtype=F32)
    row = jnp.floor(t / GRID_W)
    col = t - row * GRID_W
    n_freq = HEAD_DIM // 4
    inv_freq = ROPE_THETA ** (-jnp.arange(n_freq, dtype=F32) / n_freq)
    d = jnp.arange(HEAD_DIM)
    pos = jnp.where((d < HEAD_DIM // 2)[None, :], row[:, None], col[:, None])
    ang = pos * inv_freq[d % n_freq][None, :]
    sign = jnp.where((d % 32) < 16, -1.0, 1.0).astype(F32)[None, :]
    cos = jnp.cos(ang)
    sin = jnp.sin(ang) * sign
    return jnp.tile(cos, (1, 2)), jnp.tile(sin, (1, 2))


def _rope_raw(x, cos, sin):
    n = x.shape[1]
    lane = lax.broadcasted_iota(jnp.int32, (1, n), 1)
    first = (lane % 32) < 16
    partner = jnp.where(first, pltpu.roll(x, n - 16, 1), pltpu.roll(x, 16, 1))
    return x * cos + partner * sin


@jax.custom_vjp
def _rope_d(x, cos, sin):
    return _rope_raw(x, cos, sin)


def _rope_d_fwd(x, cos, sin):
    return _rope_raw(x, cos, sin), (cos, sin)


def _rope_d_bwd(res, ct):
    cos, sin = res
    return _rope_raw(ct, cos, -sin), jnp.zeros_like(cos), jnp.zeros_like(sin)


_rope_d.defvjp(_rope_d_fwd, _rope_d_bwd)


def _rms(x, g):
    return x * lax.rsqrt(jnp.mean(x * x, axis=-1, keepdims=True) + NORM_EPS) * g


def _pre_fn(x, shift, scale, g_pre):
    return _rms(x, g_pre) * (1.0 + scale) + shift


def _qk_fn(q, g, cos, sin, bd, scale, diff):
    segsum = _segsum_d if diff else _segsum_raw
    rope = _rope_d if diff else _rope_raw
    qn = q * lax.rsqrt(segsum(q * q, bd) * (1.0 / HEAD_DIM) + NORM_EPS) * g
    return rope(qn, cos, sin) * scale


def _silu(x):
    return x * jax.nn.sigmoid(x)


def _rwkv_pw(k, pw0, pw1, pa0, pa1, w0, a0, k_k, k_a, bd, diff):
    segsum = _segsum_d if diff else _segsum_raw
    kk = k * k_k
    kk = kk * lax.rsqrt(segsum(kk * kk, bd) + L2_EPS)
    ws, kts, akks = [], [], []
    for z, (pw, pa) in enumerate(((pw0, pa0), (pw1, pa1))):
        w = jnp.exp(-DECAY_SCALE * jax.nn.sigmoid(w0[z:z + 1, :] + pw))
        a = jax.nn.sigmoid(a0[z:z + 1, :] + pa)
        ws.append(w)
        kts.append(k * (1.0 + (a - 1.0) * k_a))
        akks.append(a * kk)
    return ws[0], ws[1], kts[0], kts[1], akks[0], akks[1], kk


def _mix_fn(y_att, g_att, ys, r, v, kts, g_rw, gn_w, gn_b, r_k, bd, diff):
    segsum = _segsum_d if diff else _segsum_raw
    mu = segsum(ys, bd) * (1.0 / HEAD_DIM)
    d = ys - mu
    var = segsum(d * d, bd) * (1.0 / HEAD_DIM)
    yn = d * lax.rsqrt(var + GN_EPS) * gn_w + gn_b
    bonus = segsum(r * kts * r_k, bd) * v
    return y_att * _silu(g_att), (yn + bonus) * _silu(g_rw)


def _loss_fn(out, x, tgt, gate, g_post):
    e = x + gate * _rms(out, g_post) - tgt
    s = jnp.sum(e * e, axis=1, keepdims=True)
    return jnp.sum(s, axis=0, keepdims=True) * (0.5 / D_MODEL)


def _exchange(arrays, modes, name):
    n = len(arrays)
    out_shape = tuple(
        jax.ShapeDtypeStruct(((NDEV,) + tuple(a.shape)) if mode == "all" else tuple(a.shape), a.dtype)
        for a, mode in zip(arrays, modes))
    chips = (4, 2, 6)

    def body(*refs):
        ins, outs = refs[:n], refs[n:2 * n]
        send_sems, recv_sems, local_sems = refs[2 * n:]
        ix, iy, ic = lax.axis_index("x"), lax.axis_index("y"), lax.axis_index("c")
        me = 4 * ix + 2 * iy + ic

        def peer(m):
            px = 1 - ix if (m >> 2) & 1 else ix
            py = 1 - iy if (m >> 1) & 1 else iy
            pc = 1 - ic if m & 1 else ic
            return (px, py, pc), 4 * px + 2 * py + pc

        def copy(k, j, src_ref, slot, to):
            return pltpu.make_async_remote_copy(src_ref=src_ref, dst_ref=outs[k].at[slot], send_sem=send_sems.at[k, j],
                                                recv_sem=recv_sems.at[k, j], device_id=to, device_id_type=MESH)

        local, sends, arrivals, forwards = [], [], [], []
        for k in range(n):
            if modes[k] == "scatter":
                local.append(pltpu.make_async_copy(ins[k].at[me], outs[k].at[me], local_sems.at[k]))
                for m in range(1, NDEV):
                    to, p = peer(m)
                    sends.append(copy(k, m - 1, ins[k].at[p], me, to))
                    arrivals.append(copy(k, m - 1, ins[k].at[p], p, to))
            elif modes[k] == "chips":
                mine = me // 2
                local.append(pltpu.make_async_copy(ins[k].at[mine], outs[k].at[mine], local_sems.at[k]))
                for j, m in enumerate(chips):
                    to, p = peer(m)
                    sends.append(copy(k, j, ins[k].at[p // 2], mine, to))
                    arrivals.append(copy(k, j, ins[k].at[p // 2], p // 2, to))
            else:
                local.append(pltpu.make_async_copy(ins[k], outs[k].at[me], local_sems.at[k]))
                sib, sib_slot = peer(1)
                sends.append(copy(k, 0, ins[k], me, sib))
                for j, m in enumerate(chips):
                    to, p = peer(m)
                    sends.append(copy(k, 1 + j, ins[k], me, to))
                    forwards.append((copy(k, 1 + j, ins[k], p, to), copy(k, 4 + j, outs[k].at[p], p, sib)))
                    arrivals.append(copy(k, 4 + j, ins[k], peer(m ^ 1)[1], sib))
                arrivals.append(copy(k, 0, ins[k], sib_slot, sib))
        for cp in local + sends:
            cp.start()
        for arrived, onward in forwards:
            arrived.wait_recv()
            onward.start()
        for cp in arrivals:
            cp.wait_recv()
        for cp in sends + [onward for _, onward in forwards]:
            cp.wait_send()
        for cp in local:
            cp.wait()

    any_spec = pl.BlockSpec(memory_space=pl.ANY)
    return pl.pallas_call(
        body, name=name, out_shape=out_shape,
        in_specs=[any_spec] * n, out_specs=tuple([any_spec] * n),
        scratch_shapes=[pltpu.SemaphoreType.DMA((n, NDEV - 1)), pltpu.SemaphoreType.DMA((n, NDEV - 1)),
                        pltpu.SemaphoreType.DMA((n,))],
    )(*arrays)


def _pair_sum_call(mine, send, name):
    n = len(mine)

    def body(*refs):
        mine_r, send_r, out_r, land_r = (refs[j * n:(j + 1) * n] for j in range(4))
        send_sems, recv_sems = refs[4 * n:]
        sibling = (lax.axis_index("x"), lax.axis_index("y"), 1 - lax.axis_index("c"))
        swaps = [pltpu.make_async_remote_copy(src_ref=send_r[k], dst_ref=land_r[k], send_sem=send_sems.at[k],
                                              recv_sem=recv_sems.at[k], device_id=sibling, device_id_type=MESH)
                 for k in range(n)]
        for cp in swaps:
            cp.start()
        for k, cp in enumerate(swaps):
            cp.wait()
            out_r[k][...] = (mine_r[k][...].astype(F32) + land_r[k][...].astype(F32)).astype(out_r[k].dtype)

    return pl.pallas_call(
        body, name=name, out_shape=tuple(jax.ShapeDtypeStruct(a.shape, a.dtype) for a in mine),
        scratch_shapes=[pltpu.VMEM(a.shape, a.dtype) for a in mine] + [pltpu.SemaphoreType.DMA((n,)),
                                                                         pltpu.SemaphoreType.DMA((n,))],
        compiler_params=pltpu.CompilerParams(vmem_limit_bytes=VMEM_LIMIT),
    )(*mine, *send)


def _mod_call(c_all, w_ada, b_cols):
    def body(c_ref, w_ref, b_ref, o_ref):
        o_ref[...] = _dot(_silu(c_ref[...]), w_ref[...]) + b_ref[...]

    return pl.pallas_call(body, name="mod_fwd",
                          out_shape=jax.ShapeDtypeStruct((c_all.shape[0], w_ada.shape[1]), F32))(c_all, w_ada, b_cols)


def _wada_grad_call(c_all, dmod_cols):
    def body(c_ref, d_ref, o_ref):
        o_ref[...] = _dot_tn(_silu(c_ref[...]), d_ref[...])

    return pl.pallas_call(body, name="w_ada_grad",
                          out_shape=jax.ShapeDtypeStruct((c_all.shape[1], dmod_cols.shape[1]), F32))(c_all, dmod_cols)


def _full(shape):
    nd = len(shape)
    return pl.BlockSpec(shape, lambda *_: (0,) * nd)


def _in_proj_call(x2, shift, scale, g_pre, w_in, qg, kg, cos, sin, bd, T):
    R = x2.shape[0]
    TT = min(ROW_TILE, T)
    tpe = T // TT

    def body(x_ref, sh_ref, sc_ref, gp_ref, w_ref, qg_ref, kg_ref, cos_ref, sin_ref, bd_ref,
             hb_ref, qr_ref, kpad_ref, vpad_ref, qraw_ref, kraw_ref, gatt_ref, rin_ref, grw_ref):
        h = _pre_fn(x_ref[...], sh_ref[0], sc_ref[0], gp_ref[...])
        hb = h.astype(MXU_DTYPE)
        hb_ref[...] = hb

        def proj(c0, c1):
            return _dot_nt(hb, w_ref[c0:c1, :])

        q = proj(C_Q, C_K)
        k = proj(C_K, C_V)
        v = proj(C_V, C_GA)
        gatt_ref[...] = proj(C_GA, C_RIN)
        rin_ref[...] = proj(C_RIN, C_GRW)
        grw_ref[...] = proj(C_GRW, C_END)
        qraw_ref[...] = q
        kraw_ref[...] = k
        cos, sin, bd = cos_ref[...], sin_ref[...], bd_ref[...]
        qr = _qk_fn(q, qg_ref[...], jnp.tile(cos, (1, 4)), jnp.tile(sin, (1, 4)), bd, ATT_SCALE, False)
        qr_ref[...] = qr.astype(MXU_DTYPE)
        kr = _qk_fn(k, kg_ref[...], cos, sin, bd, 1.0, False)
        left = lax.broadcasted_iota(jnp.int32, (1, KV_W), 1) < HEAD_DIM
        for ref, val in ((kpad_ref, kr), (vpad_ref, v)):
            h0l = jnp.where(left, val, 0.0)
            h1r = jnp.where(left, 0.0, val)
            ref[0] = h0l.astype(MXU_DTYPE)
            ref[1] = pltpu.roll(h0l, HEAD_DIM, 1).astype(MXU_DTYPE)
            ref[2] = pltpu.roll(h1r, HEAD_DIM, 1).astype(MXU_DTYPE)
            ref[3] = h1r.astype(MXU_DTYPE)

    row = lambda w: pl.BlockSpec((TT, w), lambda i: (i, 0))
    per_ex = pl.BlockSpec((1, 1, D_MODEL), lambda i: (i // tpe, 0, 0))
    tab = pl.BlockSpec((TT, KV_W), lambda i: (i % tpe, 0))
    pad = pl.BlockSpec((4, TT, KV_W), lambda i: (0, i, 0))
    sds = jax.ShapeDtypeStruct
    return pl.pallas_call(
        body, name="in_proj", grid=(R // TT,),
        in_specs=[row(D_MODEL), per_ex, per_ex, _full((1, D_MODEL)), _full(w_in.shape), _full((1, ATT_W)),
                  _full((1, KV_W)), tab, tab, _full((256, 256))],
        out_specs=(row(D_MODEL), row(ATT_W), pad, pad, row(ATT_W), row(KV_W), row(ATT_W), row(SHIFT_W), row(RWKV_W)),
        out_shape=(sds((R, D_MODEL), MXU_DTYPE), sds((R, ATT_W), MXU_DTYPE), sds((4, R, KV_W), MXU_DTYPE),
                   sds((4, R, KV_W), MXU_DTYPE), sds((R, ATT_W), F32), sds((R, KV_W), F32), sds((R, ATT_W), F32),
                   sds((R, SHIFT_W), F32), sds((R, RWKV_W), F32)),
        compiler_params=_cp(("arbitrary",)),
    )(x2, shift, scale, g_pre, w_in, qg, kg, cos, sin, bd)


def _softmax_parts(s):
    e = jnp.exp(s - jnp.max(s, axis=1, keepdims=True))
    return e, 1.0 / jnp.sum(e, axis=1, keepdims=True)


def _att_specs(T, TQ):
    nq = T // TQ
    qspec = pl.BlockSpec((TQ, KV_W), lambda b, p, i: (b * nq + i, p))
    side = lambda s: pl.BlockSpec((None, T, KV_W), lambda b, p, i: (2 * (p // 2) + s, b, 0))
    return nq, qspec, side


def _att_fwd_call(qr, kpad, vpad, B, T):
    TQ = min(ATT_TILE_FWD, T)
    nq, qspec, side = _att_specs(T, TQ)

    def body(q_ref, kl_ref, kr_ref, vl_ref, vr_ref, o_ref):
        q = q_ref[...]
        ea, inv_a = _softmax_parts(_dot_nt(q, kl_ref[...]))
        eb, inv_b = _softmax_parts(_dot_nt(q, kr_ref[...]))
        o_ref[...] = _dot(ea, vl_ref[...]) * inv_a + _dot(eb, vr_ref[...]) * inv_b

    return pl.pallas_call(
        body, name="att_fwd", grid=(B, 4, nq),
        in_specs=[qspec, side(0), side(1), side(0), side(1)], out_specs=qspec,
        out_shape=jax.ShapeDtypeStruct((B * T, ATT_W), F32),
        compiler_params=_cp(("arbitrary",) * 3),
    )(qr, kpad, kpad, vpad, vpad)


def _att_bwd_call(qr, kpad, vpad, d_o, B, T):
    TQ = min(ATT_TILE_BWD, T)
    nq, qspec, side = _att_specs(T, TQ)

    def body(q_ref, kl_ref, kr_ref, vl_ref, vr_ref, do_ref, dq_ref, dk_ref, dv_ref):
        i = pl.program_id(2)
        q, do = q_ref[...], do_ref[...]
        left = lax.broadcasted_iota(jnp.int32, (1, KV_W), 1) < HEAD_DIM
        dq = jnp.zeros((TQ, KV_W), F32)
        dk = jnp.zeros((T, KV_W), F32)
        dv = jnp.zeros((T, KV_W), F32)
        for k_ref, v_ref, mask in ((kl_ref, vl_ref, left), (kr_ref, vr_ref, jnp.logical_not(left))):
            kk, vv = k_ref[...], v_ref[...]
            e, inv = _softmax_parts(_dot_nt(q, kk))
            dp = _dot_nt(do, vv)
            ds = e * (dp - inv * jnp.sum(e * dp, axis=1, keepdims=True))
            dq = dq + _dot(ds, kk) * inv
            dk = dk + _dot_tn(ds, jnp.where(mask, q * inv, 0.0))
            dv = dv + _dot_tn(e, jnp.where(mask, do * inv, 0.0))
        dq_ref[...] = dq

        @pl.when(i == 0)
        def _():
            dk_ref[...] = dk
            dv_ref[...] = dv

        @pl.when(i > 0)
        def _():
            dk_ref[...] += dk
            dv_ref[...] += dv

    acc = pl.BlockSpec((None, T, KV_W), lambda b, p, i: (p, b, 0))
    sds = jax.ShapeDtypeStruct
    return pl.pallas_call(
        body, name="att_bwd", grid=(B, 4, nq),
        in_specs=[qspec, side(0), side(1), side(0), side(1), qspec], out_specs=(qspec, acc, acc),
        out_shape=(sds((B * T, ATT_W), F32), sds((4, B * T, KV_W), F32), sds((4, B * T, KV_W), F32)),
        compiler_params=_cp(("arbitrary",) * 3),
    )(qr, kpad, kpad, vpad, vpad, d_o)


def _shift_specs(R, T, TT, width):
    tpe = T // TT
    nb8 = R // 8
    cur = pl.BlockSpec((TT, width), lambda i: (i, 0))
    prev = pl.BlockSpec((8, width), lambda i: (jnp.maximum(i * (TT // 8) - 1, 0), 0))
    nxt = pl.BlockSpec((8, width), lambda i: (jnp.minimum((i + 1) * (TT // 8), nb8 - 1), 0))
    return tpe, cur, prev, nxt


def _neighbours(cur, prev8, next8, i, tpe, TT):
    rows = lax.broadcasted_iota(jnp.int32, (TT, 1), 0)
    first = jnp.where(i % tpe == 0, 0.0, 1.0)
    last = jnp.where(i % tpe == tpe - 1, 0.0, 1.0)
    before = jnp.where(rows == 0, prev8[7:8, :] * first, pltpu.roll(cur, 1, 0))
    after = jnp.where(rows == TT - 1, next8[0:1, :] * last, pltpu.roll(cur, TT - 1, 0))
    return before, after


def _shift_fwd_call(x, taps, T):
    R, width = x.shape
    TT = min(ROW_TILE, T)
    tpe, cur, prev, nxt = _shift_specs(R, T, TT, width)

    def body(x_ref, p_ref, n_ref, t_ref, o_ref):
        xc = x_ref[...]
        before, after = _neighbours(xc, p_ref[...], n_ref[...], pl.program_id(0), tpe, TT)
        o_ref[...] = t_ref[0:1, :] * before + t_ref[1:2, :] * xc + t_ref[2:3, :] * after

    return pl.pallas_call(
        body, name="shift_fwd", grid=(R // TT,), in_specs=[cur, prev, nxt, _full(taps.shape)], out_specs=cur,
        out_shape=jax.ShapeDtypeStruct((R, width), F32), compiler_params=_cp(("arbitrary",)),
    )(x, x, x, taps)


def _shift_bwd_call(x, d, taps, T):
    R, width = x.shape
    TT = min(ROW_TILE, T)
    tpe, cur, prev, nxt = _shift_specs(R, T, TT, width)

    def body(x_ref, xp_ref, xn_ref, d_ref, dp_ref, dn_ref, t_ref, dx_ref, dt_ref):
        i = pl.program_id(0)
        xc, dc = x_ref[...], d_ref[...]
        d_before, d_after = _neighbours(dc, dp_ref[...], dn_ref[...], i, tpe, TT)
        dx_ref[...] = t_ref[2:3, :] * d_before + t_ref[1:2, :] * dc + t_ref[0:1, :] * d_after
        x_before, x_after = _neighbours(xc, xp_ref[...], xn_ref[...], i, tpe, TT)
        @pl.when(i == 0)
        def _():
            dt_ref[...] = jnp.zeros_like(dt_ref)

        for j, xs in enumerate((x_before, xc, x_after)):
            dt_ref[j:j + 1, :] += jnp.sum(dc * xs, axis=0, keepdims=True)

    return pl.pallas_call(
        body, name="shift_bwd", grid=(R // TT,),
        in_specs=[cur, prev, nxt, cur, prev, nxt, _full(taps.shape)], out_specs=(cur, _full((8, width))),
        out_shape=(jax.ShapeDtypeStruct((R, width), F32), jax.ShapeDtypeStruct((8, width), F32)),
        compiler_params=_cp(("arbitrary",)),
    )(x, x, x, d, d, d, taps)


def _lora_in(wa):
    lane = lax.broadcasted_iota(jnp.int32, (1, LORA_W), 1)
    return jnp.where(lane < LORA_W // 2, jnp.tanh(wa), wa)


def _rwkv_prep_call(shifted, wup, aup, w0, a0, k_k, k_a, bd, T):
    R = shifted.shape[0]
    TT = min(ROW_TILE, T)

    def body(k_ref, wa_ref, wup_ref, aup_ref, w0_ref, a0_ref, kk_ref, ka_ref, bd_ref, w_o, kt_o, akk_o, kk_o):
        twa = _lora_in(wa_ref[...])
        pre = [_dot(twa, m_ref[z]) for m_ref in (wup_ref, aup_ref) for z in range(2)]
        outs = _rwkv_pw(k_ref[...], pre[0], pre[1], pre[2], pre[3], w0_ref[...], a0_ref[...], kk_ref[...],
                        ka_ref[...], bd_ref[...], False)
        w_o[0], w_o[1], kt_o[0], kt_o[1], akk_o[0], akk_o[1] = outs[:6]
        kk_o[...] = outs[6]

    col = lambda c, w: pl.BlockSpec((TT, w), lambda i: (i, c))
    two = pl.BlockSpec((2, TT, RWKV_W), lambda i: (0, i, 0))
    sds = jax.ShapeDtypeStruct
    return pl.pallas_call(
        body, name="rwkv_prep", grid=(R // TT,),
        in_specs=[col(1, RWKV_W), col(3 * RWKV_W // LORA_W, LORA_W), _full(wup.shape), _full(aup.shape),
                  _full((2, RWKV_W)), _full((2, RWKV_W)), _full((1, RWKV_W)), _full((1, RWKV_W)), _full((256, 256))],
        out_specs=(two, two, two, col(0, RWKV_W)),
        out_shape=(sds((2, R, RWKV_W), F32),) * 3 + (sds((R, RWKV_W), F32),),
        compiler_params=_cp(("arbitrary",)),
    )(shifted, shifted, wup, aup, w0, a0, k_k, k_a, bd)


def _rwkv_prep_bwd_call(shifted, cts, wup, aup, w0, a0, k_k, k_a, bd, T):
    R = shifted.shape[0]
    TT = min(ROW_TILE, T)

    def body(k_ref, wa_ref, dw0, dkt0, dakk0, dkk0, dr0, dv0, dw1, dkt1, dakk1, dkk1, dr1, dv1, dr2_ref, dv2_ref, dkts_ref,
             wup_ref, aup_ref, w0_ref, a0_ref, kk_ref, ka_ref, bd_ref,
             dsh_ref, gwup_ref, gaup_ref, gw0_ref, ga0_ref, gkk_ref, gka_ref):
        dw_ref, dkt_ref, dakk_ref, dkk_ref, dr_ref, dv_ref = ((dw0, dw1), (dkt0, dkt1), (dakk0, dakk1), (dkk0, dkk1),
                                                              (dr0, dr1), (dv0, dv1))
        i = pl.program_id(0)
        wa = wa_ref[...]
        twa = _lora_in(wa)
        pre = [_dot(twa, m_ref[z]) for m_ref in (wup_ref, aup_ref) for z in range(2)]
        fn = functools.partial(_rwkv_pw, bd=bd_ref[...], diff=True)
        _, vjp = jax.vjp(fn, k_ref[...], pre[0], pre[1], pre[2], pre[3], w0_ref[...], a0_ref[...], kk_ref[...],
                         ka_ref[...])
        dkts = dkts_ref[...]
        dk, dpw0, dpw1, dpa0, dpa1, gw0, ga0, gkk, gka = vjp(
            (dw_ref[0][...], dw_ref[1][...], dkt_ref[0][...] + dkts, dkt_ref[1][...] + dkts, dakk_ref[0][...],
             dakk_ref[1][...], dkk_ref[0][...] + dkk_ref[1][...]))
        dtwa = (_dot_nt(dpw0, wup_ref[0]) + _dot_nt(dpw1, wup_ref[1]) + _dot_nt(dpa0, aup_ref[0])
                + _dot_nt(dpa1, aup_ref[1]))
        lane = lax.broadcasted_iota(jnp.int32, (1, LORA_W), 1)
        dsh_ref[:, 0:RWKV_W] = dr_ref[0][...] + dr_ref[1][...] + dr2_ref[...]
        dsh_ref[:, RWKV_W:2 * RWKV_W] = dk
        dsh_ref[:, 2 * RWKV_W:3 * RWKV_W] = dv_ref[0][...] + dv_ref[1][...] + dv2_ref[...]
        dsh_ref[:, 3 * RWKV_W:] = jnp.where(lane < LORA_W // 2, dtwa * (1.0 - twa * twa), dtwa)
        acc = ((gwup_ref.at[0], _dot_tn(twa, dpw0)), (gwup_ref.at[1], _dot_tn(twa, dpw1)),
               (gaup_ref.at[0], _dot_tn(twa, dpa0)), (gaup_ref.at[1], _dot_tn(twa, dpa1)),
               (gw0_ref, gw0), (ga0_ref, ga0), (gkk_ref, gkk), (gka_ref, gka))

        @pl.when(i == 0)
        def _():
            for ref, val in acc:
                ref[...] = val

        @pl.when(i > 0)
        def _():
            for ref, val in acc:
                ref[...] += val

    col = lambda c, w: pl.BlockSpec((TT, w), lambda i: (i, c))
    one = col(0, RWKV_W)
    sds = jax.ShapeDtypeStruct
    return pl.pallas_call(
        body, name="rwkv_prep_bwd", grid=(R // TT,),
        in_specs=[col(1, RWKV_W), col(3 * RWKV_W // LORA_W, LORA_W)] + [one] * 15 + [
                  _full(wup.shape), _full(aup.shape), _full((2, RWKV_W)), _full((2, RWKV_W)), _full((1, RWKV_W)),
                  _full((1, RWKV_W)), _full((256, 256))],
        out_specs=(pl.BlockSpec((TT, SHIFT_W), lambda i: (i, 0)), _full(wup.shape), _full(aup.shape),
                   _full((2, RWKV_W)), _full((2, RWKV_W)), _full((1, RWKV_W)), _full((1, RWKV_W))),
        out_shape=(sds((R, SHIFT_W), F32), sds(wup.shape, F32), sds(aup.shape, F32), sds((2, RWKV_W), F32),
                   sds((2, RWKV_W), F32), sds((1, RWKV_W), F32), sds((1, RWKV_W), F32)),
        compiler_params=_cp(("arbitrary",)),
    )(shifted, shifted, *cts, wup, aup, w0, a0, k_k, k_a, bd)


def _col_lhs(row, eye_b):
    return eye_b * row.astype(MXU_DTYPE)


def _colsum(x):
    return jnp.sum(x, axis=0, keepdims=True)


def _stacked_segsum(tiles, bd):
    res = _seg_dot(jnp.concatenate(tiles, axis=0), bd)
    return [res[j * HEAD_DIM:(j + 1) * HEAD_DIM] for j in range(len(tiles))]


def _scan_specs(B, T, C, nC):
    def blk(z, col, rev):
        idx = (lambda g: (z, 0, nC - 1 - g, col)) if rev else (lambda g: (z, 0, g, col))
        return pl.BlockSpec((None, B, C, RWKV_W), idx)

    def blk3(col, rev):
        idx = (lambda g: (0, nC - 1 - g, col)) if rev else (lambda g: (0, g, col))
        return pl.BlockSpec((B, C, RWKV_W), idx)

    return blk, blk3


def _scan_fwd_call(w, kt, akk, kk, shifted, eye_b, eye_f, bd, B, T):
    C = min(SCAN_CHUNK, T)
    nC = T // C
    blk, blk3 = _scan_specs(B, T, C, nC)

    def body(w0, kt0, akk0, kk0, v0, r0, w1, kt1, akk1, kk1, v1, r1, eb_ref, ef_ref, bd_ref, y0, y1, st, S):
        @pl.when(pl.program_id(0) == 0)
        def _():
            S[...] = jnp.zeros_like(S)

        st[0] = S[...].astype(MXU_DTYPE)
        dirs = ((w0, kt0, akk0, kk0, v0, r0, y0), (w1, kt1, akk1, kk1, v1, r1, y1))

        def step(s, carry):
            for z in range(2):
                row = s if z == 0 else C - 1 - s
                prev = jnp.maximum(s - 1, 0) if z == 0 else jnp.minimum(C - s, C - 1)
                wr, ktr, akkr, kkr, vr, rr, yr = dirs[z]
                tiles = []
                for b in range(B):
                    Sb = st[s, z * B + b]
                    tiles += [Sb * kkr[b, pl.ds(row, 1), :].astype(MXU_DTYPE),
                              _col_lhs(vr[b, pl.ds(row, 1), :], eb_ref[...]),
                              Sb * rr[b, pl.ds(prev, 1), :].astype(MXU_DTYPE)]
                res = _stacked_segsum(tiles, bd_ref[...])
                for b in range(B):
                    c = z * B + b
                    sab, vb, yb = res[3 * b:3 * b + 3]
                    ld = lambda ref: ref[b, pl.ds(row, 1), :]
                    Sn = S[c] * ld(wr) - sab * ld(akkr) + vb * ld(ktr)
                    S[c] = Sn
                    st[s + 1, c] = Sn.astype(MXU_DTYPE)
                    yr[b, pl.ds(prev, 1), :] = _colsum(ef_ref[...] * yb)
            return carry

        lax.fori_loop(0, C, step, 0, unroll=SCAN_UNROLL)
        for z in range(2):
            last = C - 1 if z == 0 else 0
            rr, yr = dirs[z][5], dirs[z][6]
            res = _stacked_segsum([st[C, z * B + b] * rr[b, last:last + 1, :].astype(MXU_DTYPE) for b in range(B)],
                                  bd_ref[...])
            for b in range(B):
                yr[b, last:last + 1, :] = _colsum(ef_ref[...] * res[b])

    ins, specs = [], []
    for z, rev in ((0, False), (1, True)):
        ins += [w, kt, akk, kk, shifted, shifted]
        specs += [blk(z, 0, rev), blk(z, 0, rev), blk(z, 0, rev), blk3(0, rev), blk3(2, rev), blk3(0, rev)]
    sds = jax.ShapeDtypeStruct
    return pl.pallas_call(
        body, name="scan_fwd", grid=(nC,),
        in_specs=specs + [_full((HEAD_DIM, RWKV_W)), _full((HEAD_DIM, RWKV_W)), _full((256, 256))],
        out_specs=(blk3(0, False), blk3(0, True),
                   pl.BlockSpec((None, C + 1, 2 * B, HEAD_DIM, RWKV_W), lambda g: (g, 0, 0, 0, 0))),
        out_shape=(sds((B, T, RWKV_W), F32), sds((B, T, RWKV_W), F32),
                   sds((nC, C + 1, 2 * B, HEAD_DIM, RWKV_W), MXU_DTYPE)),
        scratch_shapes=[pltpu.VMEM((2 * B, HEAD_DIM, RWKV_W), F32)],
        compiler_params=_cp(("arbitrary",)),
    )(*ins, eye_b, eye_f, bd)


def _scan_bwd_call(w, kt, akk, kk, shifted, dys, st, eye_b, eye_f, bd, B, T):
    C = min(SCAN_CHUNK, T)
    nC = T // C
    blk, blk3 = _scan_specs(B, T, C, nC)
    nin = 7

    def body(*refs):
        d0, d1 = refs[:nin], refs[nin:2 * nin]
        st_ref, eb_ref, ef_ref, sel_ref, bd_ref = refs[2 * nin:2 * nin + 5]
        o0, o1 = refs[2 * nin + 5:2 * nin + 11], refs[2 * nin + 11:2 * nin + 17]
        COL, DYC, G = refs[2 * nin + 17:]

        @pl.when(pl.program_id(0) == 0)
        def _():
            G[...] = jnp.zeros_like(G)

        dirs = (d0 + (o0,), d1 + (o1,))

        def column_operands(s, z):
            row = s if z == 0 else C - 1 - s
            _, _, _, kkr, vr, _, dyr, _ = dirs[z]
            tiles = []
            for b in range(B):
                tiles += [st_ref[s, z * B + b] * kkr[b, pl.ds(row, 1), :].astype(MXU_DTYPE),
                          _col_lhs(vr[b, pl.ds(row, 1), :], eb_ref[...]),
                          _col_lhs(dyr[b, pl.ds(row, 1), :], eb_ref[...])]
            return tiles

        def keep_columns(res, z):
            for b in range(B):
                for k in range(3):
                    COL[k, z * B + b] = res[3 * b + k].astype(MXU_DTYPE)
                DYC[z * B + b] = res[3 * b + 2]

        for z in range(2):
            keep_columns(_stacked_segsum(column_operands(C - 1, z), bd_ref[...]), z)

        def bwd(it, carry):
            s = C - 1 - it
            for z in range(2):
                row = s if z == 0 else C - 1 - s
                wr, ktr, akkr, kkr, vr, rr, dyr, (dw_o, dkt_o, dakk_o, dkk_o, dr_o, dv_o) = dirs[z]
                tiles, Gcs = [], []
                for b in range(B):
                    c = z * B + b
                    Gc = G[c] + DYC[c] * rr[b, pl.ds(row, 1), :]
                    Gb = Gc.astype(MXU_DTYPE)
                    Gcs.append((Gc, Gb))
                    tiles += [Gb * akkr[b, pl.ds(row, 1), :].astype(MXU_DTYPE),
                              Gb * ktr[b, pl.ds(row, 1), :].astype(MXU_DTYPE)]
                res = _stacked_segsum(tiles + column_operands(jnp.maximum(s - 1, 0), z), bd_ref[...])
                for b in range(B):
                    c = z * B + b
                    Gc, Gb = Gcs[b]
                    gab, dvb = res[2 * b], res[2 * b + 1]
                    ld = lambda ref: ref[b, pl.ds(row, 1), :]
                    G[c] = Gc * ld(wr) - gab * ld(kkr)
                    Sb = st_ref[s, c]
                    prods = jnp.concatenate([st_ref[s + 1, c] * COL[2, c], Gb * COL[1, c], Gb * Sb, Gb * COL[0, c]], axis=0)
                    sums = jnp.dot(sel_ref[...], prods, preferred_element_type=F32)
                    for k, (ref, sign) in enumerate(((dr_o, 1.0), (dkt_o, 1.0), (dw_o, 1.0), (dakk_o, -1.0))):
                        ref[b, pl.ds(row, 1), :] = sign * sums[k:k + 1, :]
                    dv_o[b, pl.ds(row, 1), :] = _colsum(ef_ref[...] * dvb)
                    dkk_o[b, pl.ds(row, 1), :] = -_colsum(gab * Sb.astype(F32))
                keep_columns(res[2 * B:], z)
            return carry

        lax.fori_loop(0, C, bwd, 0, unroll=SCAN_UNROLL)

    ins, specs = [], []
    for z, rev in ((0, True), (1, False)):
        ins += [w, kt, akk, kk, shifted, shifted, dys]
        specs += [blk(z, 0, rev), blk(z, 0, rev), blk(z, 0, rev), blk3(0, rev), blk3(2, rev), blk3(0, rev), blk3(0, rev)]
    sel = (jnp.arange(16)[:, None] == (jnp.arange(4 * HEAD_DIM) // HEAD_DIM)[None, :]).astype(MXU_DTYPE)
    ins += [st, eye_b, eye_f, sel, bd]
    specs += [pl.BlockSpec((None, C + 1, 2 * B, HEAD_DIM, RWKV_W), lambda g: (nC - 1 - g, 0, 0, 0, 0)),
              _full((HEAD_DIM, RWKV_W)), _full((HEAD_DIM, RWKV_W)), _full(sel.shape), _full((256, 256))]
    sds = jax.ShapeDtypeStruct
    out_specs = tuple(blk3(0, True) for _ in range(6)) + tuple(blk3(0, False) for _ in range(6))
    res = pl.pallas_call(
        body, name="scan_bwd", grid=(nC,), in_specs=specs, out_specs=out_specs,
        out_shape=tuple(sds((B, T, RWKV_W), F32) for _ in range(12)),
        scratch_shapes=[pltpu.VMEM((3, 2 * B, HEAD_DIM, RWKV_W), MXU_DTYPE), pltpu.VMEM((2 * B, HEAD_DIM, RWKV_W), F32),
                        pltpu.VMEM((2 * B, HEAD_DIM, RWKV_W), F32)],
        compiler_params=_cp(("arbitrary",)),
    )(*ins)
    return list(res)


def _out_head_call(x2, tgt2, gate, y_att, g_att, y0, y1, shifted, kt, g_rw, w_out, g_post, gn_w, gn_b, r_k, bd, T):
    R = x2.shape[0]
    TT = min(ROW_TILE, T)
    tpe = T // TT

    def body(x_ref, t_ref, gate_ref, ya_ref, ga_ref, y0_ref, y1_ref, r_ref, v_ref, kt_ref, grw_ref, w_ref, gp_ref,
             gnw_ref, gnb_ref, rk_ref, bd_ref,
             loss_o, dy_o, dya_o, dga_o, dys_o, dr_o, dv_o, dkts_o, dgrw_o, dgate_o, gw_o, ggp_o, ggnw_o, ggnb_o, grk_o):
        i = pl.program_id(0)
        bd = bd_ref[...]
        mix = functools.partial(_mix_fn, bd=bd, diff=True)
        (ma, mr), mix_vjp = jax.vjp(mix, ya_ref[...], ga_ref[...], y0_ref[...] + y1_ref[...], r_ref[...], v_ref[...],
                                    kt_ref[0] + kt_ref[1], grw_ref[...], gnw_ref[...], gnb_ref[...], rk_ref[...])
        out = _dot(ma, w_ref[0:ATT_W, :]) + _dot(mr, w_ref[ATT_W:, :])
        loss, loss_vjp = jax.vjp(_loss_fn, out, x_ref[...], t_ref[...], gate_ref[0], gp_ref[...])
        d_out, dy, _, dgate, dgp = loss_vjp(jnp.ones((1, 1), F32))
        dy_o[...] = dy
        dma = _dot_nt(d_out, w_ref[0:ATT_W, :])
        dmr = _dot_nt(d_out, w_ref[ATT_W:, :])
        dya_o[...], dga_o[...], dys_o[...], dr_o[...], dv_o[...], dkts_o[...], dgrw_o[...], dgnw, dgnb, drk = \
            mix_vjp((dma, dmr))
        gw = jnp.concatenate([_dot_tn(ma, d_out), _dot_tn(mr, d_out)], axis=0)
        acc = ((loss_o, jnp.broadcast_to(loss, (8, 128))), (gw_o, gw), (ggp_o, dgp), (ggnw_o, dgnw), (ggnb_o, dgnb),
               (grk_o, drk))

        @pl.when(i == 0)
        def _():
            for ref, val in acc:
                ref[...] = val

        @pl.when(i > 0)
        def _():
            for ref, val in acc:
                ref[...] += val

        @pl.when(i % tpe == 0)
        def _():
            dgate_o[0] = dgate

        @pl.when(i % tpe > 0)
        def _():
            dgate_o[0] += dgate

    row = lambda w, c=0: pl.BlockSpec((TT, w), lambda i: (i, c))
    two = pl.BlockSpec((2, TT, RWKV_W), lambda i: (0, i, 0))
    per_ex = pl.BlockSpec((1, 1, D_MODEL), lambda i: (i // tpe, 0, 0))
    sds = jax.ShapeDtypeStruct
    r512 = sds((R, RWKV_W), F32)
    return pl.pallas_call(
        body, name="out_head", grid=(R // TT,),
        in_specs=[row(D_MODEL), row(D_MODEL), per_ex, row(ATT_W), row(ATT_W), row(RWKV_W), row(RWKV_W), row(RWKV_W, 0),
                  row(RWKV_W, 2), two,
                  row(RWKV_W), _full(w_out.shape), _full((1, D_MODEL)), _full((1, RWKV_W)), _full((1, RWKV_W)),
                  _full((1, RWKV_W)), _full((256, 256))],
        out_specs=(_full((8, 128)), row(D_MODEL), row(ATT_W), row(ATT_W), row(RWKV_W), row(RWKV_W), row(RWKV_W),
                   row(RWKV_W), row(RWKV_W), per_ex, _full((D_MODEL, D_MODEL)), _full((1, D_MODEL)), _full((1, RWKV_W)),
                   _full((1, RWKV_W)), _full((1, RWKV_W))),
        out_shape=(sds((8, 128), F32), sds((R, D_MODEL), F32), r512, r512, r512, r512, r512, r512, r512,
                   sds((R // T, 1, D_MODEL), F32), sds((D_MODEL, D_MODEL), F32), sds((1, D_MODEL), F32),
                   sds((1, RWKV_W), F32), sds((1, RWKV_W), F32), sds((1, RWKV_W), F32)),
        compiler_params=_cp(("arbitrary",)),
    )(x2, tgt2, gate, y_att, g_att, y0, y1, shifted, shifted, kt, g_rw, w_out, g_post, gn_w, gn_b, r_k, bd)


def _in_proj_bwd_call(x2, dy, shift, scale, g_pre, w_in, qg, kg, cos, sin, bd, q_raw, k_raw, dqr, dkp, dvp,
                      d_gatt, d_rin, d_grw, T):
    R = x2.shape[0]
    TT = min(ROW_TILE, T)
    tpe = T // TT

    def body(x_ref, dy_ref, sh_ref, sc_ref, gp_ref, w_ref, qg_ref, kg_ref, cos_ref, sin_ref, bd_ref, q_ref, k_ref,
             dqr_ref, dkp_ref, dvp_ref, dga_ref, drin_ref, dgrw_ref,
             dx_o, dproj_o, dsh_o, dsc_o, ggp_o, gqg_o, gkg_o):
        i = pl.program_id(0)
        cos, sin, bd = cos_ref[...], sin_ref[...], bd_ref[...]
        left = lax.broadcasted_iota(jnp.int32, (1, KV_W), 1) < HEAD_DIM

        def kv_grad(ref):
            a = ref[0] + ref[1]
            b = ref[2] + ref[3]
            return jnp.where(left, a + pltpu.roll(a, HEAD_DIM, 1), b + pltpu.roll(b, HEAD_DIM, 1))

        qfn = functools.partial(_qk_fn, cos=jnp.tile(cos, (1, 4)), sin=jnp.tile(sin, (1, 4)), bd=bd, scale=ATT_SCALE,
                                diff=True)
        _, q_vjp = jax.vjp(qfn, q_ref[...], qg_ref[...])
        dq, gqg = q_vjp(dqr_ref[...])
        kfn = functools.partial(_qk_fn, cos=cos, sin=sin, bd=bd, scale=1.0, diff=True)
        _, k_vjp = jax.vjp(kfn, k_ref[...], kg_ref[...])
        dk, gkg = k_vjp(kv_grad(dkp_ref))
        pieces = ((C_Q, C_K, dq), (C_K, C_V, dk), (C_V, C_GA, kv_grad(dvp_ref)), (C_GA, C_RIN, dga_ref[...]),
                  (C_RIN, C_GRW, drin_ref[...]), (C_GRW, C_END, dgrw_ref[...]))
        dh = jnp.zeros((TT, D_MODEL), F32)
        for c0, c1, val in pieces:
            vb = val.astype(MXU_DTYPE)
            dproj_o[:, c0:c1] = vb
            dh = dh + _dot(vb, w_ref[c0:c1, :])
        _, pre_vjp = jax.vjp(_pre_fn, x_ref[...], sh_ref[0], sc_ref[0], gp_ref[...])
        dx, dsh, dsc, ggp = pre_vjp(dh)
        dx_o[...] = dx + dy_ref[...]
        acc = ((ggp_o, ggp), (gqg_o, gqg), (gkg_o, gkg))

        @pl.when(i == 0)
        def _():
            for ref, val in acc:
                ref[...] = val

        @pl.when(i > 0)
        def _():
            for ref, val in acc:
                ref[...] += val

        @pl.when(i % tpe == 0)
        def _():
            dsh_o[0] = dsh
            dsc_o[0] = dsc

        @pl.when(i % tpe > 0)
        def _():
            dsh_o[0] += dsh
            dsc_o[0] += dsc

    row = lambda w: pl.BlockSpec((TT, w), lambda i: (i, 0))
    per_ex = pl.BlockSpec((1, 1, D_MODEL), lambda i: (i // tpe, 0, 0))
    tab = pl.BlockSpec((TT, KV_W), lambda i: (i % tpe, 0))
    pad = pl.BlockSpec((4, TT, KV_W), lambda i: (0, i, 0))
    sds = jax.ShapeDtypeStruct
    nb = R // T
    return pl.pallas_call(
        body, name="in_proj_bwd", grid=(R // TT,),
        in_specs=[row(D_MODEL), row(D_MODEL), per_ex, per_ex, _full((1, D_MODEL)), _full(w_in.shape), _full((1, ATT_W)),
                  _full((1, KV_W)), tab, tab, _full((256, 256)), row(ATT_W), row(KV_W), row(ATT_W), pad, pad,
                  row(ATT_W), row(SHIFT_W), row(RWKV_W)],
        out_specs=(row(D_MODEL), row(C_END), per_ex, per_ex, _full((1, D_MODEL)), _full((1, ATT_W)), _full((1, KV_W))),
        out_shape=(sds((R, D_MODEL), F32), sds((R, C_END), MXU_DTYPE), sds((nb, 1, D_MODEL), F32),
                   sds((nb, 1, D_MODEL), F32), sds((1, D_MODEL), F32), sds((1, ATT_W), F32), sds((1, KV_W), F32)),
        compiler_params=_cp(("arbitrary",)),
    )(x2, dy, shift, scale, g_pre, w_in, qg, kg, cos, sin, bd, q_raw, k_raw, dqr, dkp, dvp, d_gatt, d_rin, d_grw)


def _w_in_grad_call(hb, dproj):
    R = hb.shape[0]
    TT = min(W_GRAD_ROWS, R)
    CB = 1152

    def body(h_ref, d_ref, o_ref):
        g = _dot_tn(h_ref[...], d_ref[...])

        @pl.when(pl.program_id(1) == 0)
        def _():
            o_ref[...] = g

        @pl.when(pl.program_id(1) > 0)
        def _():
            o_ref[...] += g

    return pl.pallas_call(
        body, name="w_in_grad", grid=(C_END // CB, R // TT),
        in_specs=[pl.BlockSpec((TT, D_MODEL), lambda j, i: (i, 0)), pl.BlockSpec((TT, CB), lambda j, i: (i, j))],
        out_specs=pl.BlockSpec((D_MODEL, CB), lambda j, i: (0, j)),
        out_shape=jax.ShapeDtypeStruct((D_MODEL, C_END), F32), compiler_params=_cp(("arbitrary", "arbitrary")),
    )(hb, dproj)


def _adam_call(parts, w, m, v, name, row_tile=None):
    P, M, N = parts.shape
    TM = M if row_tile is None else row_tile

    def body(p_ref, w_ref, m_ref, v_ref, g_o, d_o, m_o, v_o):
        g = p_ref[0].astype(F32)
        for j in range(1, P):
            g = g + p_ref[j].astype(F32)
        m2 = ADAM_B1 * m_ref[...] + (1.0 - ADAM_B1) * g
        v2 = ADAM_B2 * v_ref[...] + (1.0 - ADAM_B2) * jnp.square(g)
        m_hat = m2 / (1.0 - ADAM_B1 ** ADAM_STEP)
        v_hat = v2 / (1.0 - ADAM_B2 ** ADAM_STEP)
        g_o[...] = g
        d_o[...] = -ADAM_LR * (m_hat / (jnp.sqrt(v_hat) + ADAM_EPS) + ADAM_WD * w_ref[...])
        m_o[...] = m2
        v_o[...] = v2

    blk = pl.BlockSpec((TM, N), lambda i: (i, 0))
    return pl.pallas_call(
        body, name=name, grid=(M // TM,),
        in_specs=[pl.BlockSpec((P, TM, N), lambda i: (0, i, 0)), blk, blk, blk], out_specs=(blk,) * 4,
        out_shape=(jax.ShapeDtypeStruct((M, N), F32),) * 4, compiler_params=_cp(("arbitrary",)),
    )(parts, w, m, v)


_SMALL_ROWS = 136


def _pack_small(taps, w_up, w0, a_up, a0):
    flat = jnp.concatenate([taps.reshape(-1), w_up.reshape(-1), w0.reshape(-1), a_up.reshape(-1), a0.reshape(-1)])
    return jnp.pad(flat, (0, _SMALL_ROWS * 128 - flat.shape[0])).reshape(_SMALL_ROWS, 128)


def _unpack_small(packed):
    n = packed.shape[0]
    flat = packed.reshape(n, -1)
    out, o = [], 0
    for shape in ((3, 208), (2, 64, 64), (2, 64), (2, 64, 64), (2, 64)):
        size = 1
        for s in shape:
            size *= s
        out.append(flat[:, o:o + size].reshape((n,) + shape))
        o += size
    return out


def _cols_to_full(blocks):
    nd = blocks.ndim
    moved = jnp.moveaxis(blocks, 0, nd - 2)
    return moved.reshape(moved.shape[:-2] + (moved.shape[-2] * moved.shape[-1],))


def _full_to_cols(full):
    k = full.shape[-1] // NDEV
    return jnp.moveaxis(full.reshape(full.shape[:-1] + (NDEV, k)), -2, 0)


_REP_SIZES = (("g_pre", 1024), ("q_norm_g", 64), ("k_norm_g", 64), ("k_k", 512), ("k_a", 512), ("r_k", 512),
              ("gn_w", 512), ("gn_b", 512), ("g_post", 1024))
_REP_ROWS = 40


def kernel(x, c, w_ada, b_ada, g_pre, w_in, q_norm_g, k_norm_g, shift_taps, w_up, w0, a_up, a0, k_k, k_a, r_k, gn_w, gn_b, w_out, g_post, loss_target, m_w_ada, m_b_ada, m_g_pre, m_w_in, m_q_norm_g, m_k_norm_g, m_shift_taps, m_w_up, m_w0, m_a_up, m_a0, m_k_k, m_k_a, m_r_k, m_gn_w, m_gn_b, m_w_out, m_g_post, v_w_ada, v_b_ada, v_g_pre, v_w_in, v_q_norm_g, v_k_norm_g, v_shift_taps, v_w_up, v_w0, v_a_up, v_a0, v_k_k, v_k_a, v_r_k, v_gn_w, v_gn_b, v_w_out, v_g_post):
    B, T, _ = x.shape
    R = B * T
    me = 4 * lax.axis_index("x") + 2 * lax.axis_index("y") + lax.axis_index("c")
    x2 = x.reshape(R, D_MODEL)
    tgt2 = loss_target.reshape(R, D_MODEL)

    seg = jnp.arange(256) // HEAD_DIM
    bd = (seg[:, None] == seg[None, :]).astype(MXU_DTYPE)
    eye = (jnp.arange(HEAD_DIM)[:, None] == (jnp.arange(RWKV_W) % HEAD_DIM)[None, :])
    eye_b, eye_f = eye.astype(MXU_DTYPE), eye.astype(F32)
    cos, sin = _rope_tables(T)

    c_g, w_in_g, w_out_g, small_g = _exchange(
        [c, w_in[0].T.astype(MXU_DTYPE), w_out[0].astype(MXU_DTYPE),
         _pack_small(shift_taps[0], w_up[0], w0[0], a_up[0], a0[0])], ["all"] * 4, "gather_params")
    c_all = c_g.reshape(NDEV * B, D_MODEL)
    w_in_f = w_in_g.reshape(C_END, D_MODEL)
    w_out_f = w_out_g.reshape(D_MODEL, D_MODEL)
    taps_b, w_up_b, w0_b, a_up_b, a0_b = _unpack_small(small_g)
    taps_f = jnp.pad(_cols_to_full(taps_b), ((0, 5), (0, 0)))
    w_up_f, a_up_f = _cols_to_full(w_up_b), _cols_to_full(a_up_b)
    w0_f, a0_f = _cols_to_full(w0_b), _cols_to_full(a0_b)
    wup_pad = jnp.pad(w_up_f, ((0, 0), (0, 64), (0, 0))).astype(MXU_DTYPE)
    aup_pad = jnp.pad(a_up_f, ((0, 0), (64, 0), (0, 0))).astype(MXU_DTYPE)

    ncol = w_ada.shape[2]
    b_cols = lax.dynamic_slice(b_ada, (0, me * ncol), (1, ncol))
    mod_cols = _mod_call(c_all, w_ada[0].astype(MXU_DTYPE), b_cols)
    (mod_g,) = _exchange([mod_cols], ["all"], "gather_mod")
    mod = lax.dynamic_slice(_cols_to_full(mod_g), (me * B, 0), (B, 3 * D_MODEL))
    shift, scale, gate = [mod[:, j * D_MODEL:(j + 1) * D_MODEL].reshape(B, 1, D_MODEL) for j in range(3)]

    qg = jnp.tile(q_norm_g, (1, ATT_W // HEAD_DIM))
    kg = jnp.tile(k_norm_g, (1, KV_W // HEAD_DIM))
    rk_row = r_k.reshape(1, RWKV_W)

    hb, qr, kpad, vpad, q_raw, k_raw, g_att, rin, g_rw = _in_proj_call(
        x2, shift, scale, g_pre, w_in_f, qg, kg, cos, sin, bd, T)
    y_att = _att_fwd_call(qr, kpad, vpad, B, T)
    shifted = _shift_fwd_call(rin, taps_f, T)
    w_s, kt_s, akk_s, kk_s = _rwkv_prep_call(shifted, wup_pad, aup_pad, w0_f, a0_f, k_k, k_a, bd, T)
    sh3 = shifted.reshape(B, T, SHIFT_W)
    r4 = lambda a: a.reshape(2, B, T, RWKV_W)
    y0, y1, st = _scan_fwd_call(r4(w_s), r4(kt_s), r4(akk_s), kk_s.reshape(B, T, RWKV_W), sh3, eye_b, eye_f, bd, B, T)

    (loss_blk, dy, d_yatt, d_gatt, d_ys, d_r2, d_v2, d_kts, d_grw, d_gate, g_wout, g_gpost, g_gnw, g_gnb,
     g_rk) = _out_head_call(x2, tgt2, gate, y_att, g_att, y0.reshape(R, RWKV_W), y1.reshape(R, RWKV_W), shifted, kt_s,
                            g_rw, w_out_f, g_post, gn_w, gn_b, rk_row, bd, T)
    scan_cts = _scan_bwd_call(r4(w_s), r4(kt_s), r4(akk_s), kk_s.reshape(B, T, RWKV_W), sh3,
                              d_ys.reshape(B, T, RWKV_W), st, eye_b, eye_f, bd, B, T)
    scan_cts = [a.reshape(R, RWKV_W) for a in scan_cts]
    d_shifted, g_wup, g_aup, g_w0, g_a0, g_kk, g_ka = _rwkv_prep_bwd_call(
        shifted, scan_cts + [d_r2, d_v2, d_kts], wup_pad, aup_pad, w0_f, a0_f, k_k, k_a, bd, T)
    d_rin, g_taps = _shift_bwd_call(rin, d_shifted, taps_f, T)
    dqr, dkp, dvp = _att_bwd_call(qr, kpad, vpad, d_yatt, B, T)
    grad_x, dproj, d_shift, d_scale, g_gpre, g_qg, g_kg = _in_proj_bwd_call(
        x2, dy, shift, scale, g_pre, w_in_f, qg, kg, cos, sin, bd, q_raw, k_raw, dqr, dkp, dvp, d_gatt, d_rin, d_grw, T)
    g_win = _w_in_grad_call(hb, dproj)

    rep = jnp.concatenate([g_gpre.reshape(-1), g_qg.reshape(-1, HEAD_DIM).sum(0), g_kg.reshape(-1, HEAD_DIM).sum(0),
                           g_kk.reshape(-1), g_ka.reshape(-1), g_rk.reshape(-1), g_gnw.reshape(-1), g_gnb.reshape(-1),
                           g_gpost.reshape(-1), loss_blk[0, :1]])
    rep = jnp.pad(rep, (0, _REP_ROWS * 128 - rep.shape[0])).reshape(_REP_ROWS, 128)
    dmod = jnp.concatenate([d_shift, d_scale, d_gate], axis=2).reshape(B, 3 * D_MODEL)
    small_parts = jax.vmap(_pack_small)(_full_to_cols(g_taps[:3]), _full_to_cols(g_wup[:, :64, :]), _full_to_cols(g_w0),
                                        _full_to_cols(g_aup[:, 64:, :]), _full_to_cols(g_a0))
    core = lax.axis_index("c")
    halves = [a.reshape((NDEV // 2, 2) + a.shape[1:]).astype(MXU_DTYPE)
              for a in (_full_to_cols(g_win), g_wout.reshape(NDEV, D_MODEL // NDEV, D_MODEL))]
    pick = lambda a, j: lax.dynamic_index_in_dim(a, j, axis=1, keepdims=False)
    s_win, s_wout = _pair_sum_call([pick(a, core) for a in halves], [pick(a, 1 - core) for a in halves], "reduce_pair")
    p_win, p_wout, p_small, dmod_g, rep_g = _exchange(
        [s_win, s_wout, small_parts, dmod, rep], ["chips", "chips", "scatter", "all", "all"], "reduce_grads")
    dmod_all = dmod_g.reshape(NDEV * B, 3 * D_MODEL)
    g_wada = _wada_grad_call(c_all, lax.dynamic_slice(dmod_all, (0, me * ncol), (NDEV * B, ncol)))

    res = {}

    def adam(name, parts, w, m, v, row_tile=None):
        shape = w.shape
        two_d = (-1, shape[-1])
        out = _adam_call(parts.reshape((parts.shape[0],) + w.reshape(two_d).shape), w.reshape(two_d), m.reshape(two_d),
                         v.reshape(two_d), "adam_" + name, row_tile)
        res[name] = [o.reshape(shape) for o in out]

    adam("w_ada", g_wada[None], w_ada, m_w_ada, v_w_ada)
    adam("b_ada", dmod_all.reshape(NDEV * B, 1, 3 * D_MODEL), b_ada, m_b_ada, v_b_ada)
    adam("w_in", p_win, w_in, m_w_in, v_w_in, 128)
    adam("w_out", p_wout, w_out, m_w_out, v_w_out)
    taps_p, wup_p, w0_p, aup_p, a0_p = _unpack_small(p_small)
    adam("shift_taps", taps_p, shift_taps, m_shift_taps, v_shift_taps)
    adam("w_up", wup_p, w_up, m_w_up, v_w_up)
    adam("w0", w0_p, w0, m_w0, v_w0)
    adam("a_up", aup_p, a_up, m_a_up, v_a_up)
    adam("a0", a0_p, a0, m_a0, v_a0)
    rep_flat = rep_g.reshape(NDEV, -1)
    off = 0
    given = dict(g_pre=(g_pre, m_g_pre, v_g_pre), q_norm_g=(q_norm_g, m_q_norm_g, v_q_norm_g),
                 k_norm_g=(k_norm_g, m_k_norm_g, v_k_norm_g), k_k=(k_k, m_k_k, v_k_k), k_a=(k_a, m_k_a, v_k_a),
                 r_k=(r_k, m_r_k, v_r_k), gn_w=(gn_w, m_gn_w, v_gn_w), gn_b=(gn_b, m_gn_b, v_gn_b),
                 g_post=(g_post, m_g_post, v_g_post))
    for name, size in _REP_SIZES:
        adam(name, rep_flat[:, off:off + size], *given[name])
        off += size

    loss = jnp.sum(rep_flat[:, off])
    order = ["w_ada", "b_ada", "g_pre", "w_in", "q_norm_g", "k_norm_g", "shift_taps", "w_up", "w0", "a_up", "a0", "k_k",
             "k_a", "r_k", "gn_w", "gn_b", "w_out", "g_post"]
    return (loss, grad_x.reshape(B, T, D_MODEL), *[res[n][0] for n in order], *[res[n][1] for n in order],
            *[res[n][2] for n in order], *[res[n][3] for n in order])
```

```python
import functools

import jax
import jax.numpy as jnp
from jax import lax
from jax.experimental import pallas as pl
from jax.experimental.pallas import tpu as pltpu

F32 = jnp.float32
MXU_DTYPE = jnp.bfloat16
MESH = pl.DeviceIdType.MESH
NDEV = 8

D_MODEL = 1024
HEAD_DIM = 64
ATT_W = 512
KV_W = 128
RWKV_W = 512
LORA_W = 128
SHIFT_W = 3 * RWKV_W + LORA_W
GRID_W = 64
ROPE_THETA = 10000.0
DECAY_SCALE = 0.6065306597126334
NORM_EPS = 1e-6
GN_EPS = 64e-5
L2_EPS = 1e-12
ATT_SCALE = HEAD_DIM ** -0.5
C_Q, C_K, C_V, C_GA, C_RIN, C_GRW, C_END = 0, 512, 640, 768, 1280, 2944, 3456

ADAM_LR, ADAM_B1, ADAM_B2, ADAM_EPS, ADAM_WD, ADAM_STEP = 0.001, 0.9, 0.999, 1e-08, 0.01, 10

ROW_TILE = 256
W_GRAD_ROWS = 1024
ATT_TILE_FWD = 256
ATT_TILE_BWD = 512
SCAN_CHUNK = 64
SCAN_UNROLL = 16
VMEM_LIMIT = 56 * 1024 * 1024


def _cp(sem=None):
    return pltpu.CompilerParams(dimension_semantics=sem, vmem_limit_bytes=VMEM_LIMIT)


def _dot(a, b, dims=(((1,), (0,)), ((), ()))):
    return lax.dot_general(a.astype(MXU_DTYPE), b.astype(MXU_DTYPE), dims, preferred_element_type=F32)


def _dot_nt(a, b):
    return _dot(a, b, (((1,), (1,)), ((), ())))


def _dot_tn(a, b):
    return _dot(a, b, (((0,), (0,)), ((), ())))


def _seg_dot(xb, bd):
    n = xb.shape[1]
    if n <= 256:
        return jnp.dot(xb, bd[:n, :n], preferred_element_type=F32)
    parts = [jnp.dot(xb[:, c:c + 256], bd, preferred_element_type=F32) for c in range(0, n, 256)]
    return jnp.concatenate(parts, axis=1)


def _segsum_raw(x, bd):
    rows = x.shape[0]
    hi = x.astype(MXU_DTYPE)
    lo = (x - hi.astype(F32)).astype(MXU_DTYPE)
    both = _seg_dot(jnp.concatenate([hi, lo], axis=0), bd)
    return both[:rows] + both[rows:]


@jax.custom_vjp
def _segsum_d(x, bd):
    return _segsum_raw(x, bd)


def _segsum_d_fwd(x, bd):
    return _segsum_raw(x, bd), bd


def _segsum_d_bwd(bd, ct):
    return _segsum_raw(ct, bd), jnp.zeros_like(bd)


_segsum_d.defvjp(_segsum_d_fwd, _segsum_d_bwd)


def _rope_tables(T):
    t = jnp.arange(T, dtype=F32)
    row = jnp.floor(t / GRID_W)
    col = t - row * GRID_W
    n_freq = HEAD_DIM // 4
    inv_freq = ROPE_THETA ** (-jnp.arange(n_freq, dtype=F32) / n_freq)
    d = jnp.arange(HEAD_DIM)
    pos = jnp.where((d < HEAD_DIM // 2)[None, :], row[:, None], col[:, None])
    ang = pos * inv_freq[d % n_freq][None, :]
    sign = jnp.where((d % 32) < 16, -1.0, 1.0).astype(F32)[None, :]
    cos = jnp.cos(ang)
    sin = jnp.sin(ang) * sign
    return jnp.tile(cos, (1, 2)), jnp.tile(sin, (1, 2))


def _rope_raw(x, cos, sin):
    n = x.shape[1]
    lane = lax.broadcasted_iota(jnp.int32, (1, n), 1)
    first = (lane % 32) < 16
    partner = jnp.where(first, pltpu.roll(x, n - 16, 1), pltpu.roll(x, 16, 1))
    return x * cos + partner * sin


@jax.custom_vjp
def _rope_d(x, cos, sin):
    return _rope_raw(x, cos, sin)


def _rope_d_fwd(x, cos, sin):
    return _rope_raw(x, cos, sin), (cos, sin)


def _rope_d_bwd(res, ct):
    cos, sin = res
    return _rope_raw(ct, cos, -sin), jnp.zeros_like(cos), jnp.zeros_like(sin)


_rope_d.defvjp(_rope_d_fwd, _rope_d_bwd)


def _rms(x, g):
    return x * lax.rsqrt(jnp.mean(x * x, axis=-1, keepdims=True) + NORM_EPS) * g


def _pre_fn(x, shift, scale, g_pre):
    return _rms(x, g_pre) * (1.0 + scale) + shift


def _qk_fn(q, g, cos, sin, bd, scale, diff):
    segsum = _segsum_d if diff else _segsum_raw
    rope = _rope_d if diff else _rope_raw
    qn = q * lax.rsqrt(segsum(q * q, bd) * (1.0 / HEAD_DIM) + NORM_EPS) * g
    return rope(qn, cos, sin) * scale


def _silu(x):
    return x * jax.nn.sigmoid(x)


def _rwkv_pw(k, pw0, pw1, pa0, pa1, w0, a0, k_k, k_a, bd, diff):
    segsum = _segsum_d if diff else _segsum_raw
    kk = k * k_k
    kk = kk * lax.rsqrt(segsum(kk * kk, bd) + L2_EPS)
    ws, kts, akks = [], [], []
    for z, (pw, pa) in enumerate(((pw0, pa0), (pw1, pa1))):
        w = jnp.exp(-DECAY_SCALE * jax.nn.sigmoid(w0[z:z + 1, :] + pw))
        a = jax.nn.sigmoid(a0[z:z + 1, :] + pa)
        ws.append(w)
        kts.append(k * (1.0 + (a - 1.0) * k_a))
        akks.append(a * kk)
    return ws[0], ws[1], kts[0], kts[1], akks[0], akks[1], kk


def _mix_fn(y_att, g_att, ys, r, v, kts, g_rw, gn_w, gn_b, r_k, bd, diff):
    segsum = _segsum_d if diff else _segsum_raw
    mu = segsum(ys, bd) * (1.0 / HEAD_DIM)
    d = ys - mu
    var = segsum(d * d, bd) * (1.0 / HEAD_DIM)
    yn = d * lax.rsqrt(var + GN_EPS) * gn_w + gn_b
    bonus = segsum(r * kts * r_k, bd) * v
    return y_att * _silu(g_att), (yn + bonus) * _silu(g_rw)


def _loss_fn(out, x, tgt, gate, g_post):
    e = x + gate * _rms(out, g_post) - tgt
    s = jnp.sum(e * e, axis=1, keepdims=True)
    return jnp.sum(s, axis=0, keepdims=True) * (0.5 / D_MODEL)


def _exchange(arrays, modes, name):
    n = len(arrays)
    out_shape = tuple(
        jax.ShapeDtypeStruct(((NDEV,) + tuple(a.shape)) if mode == "all" else tuple(a.shape), a.dtype)
        for a, mode in zip(arrays, modes))
    chips = (4, 2, 6)

    def body(*refs):
        ins, outs = refs[:n], refs[n:2 * n]
        send_sems, recv_sems, local_sems = refs[2 * n:]
        ix, iy, ic = lax.axis_index("x"), lax.axis_index("y"), lax.axis_index("c")
        me = 4 * ix + 2 * iy + ic

        def peer(m):
            px = 1 - ix if (m >> 2) & 1 else ix
            py = 1 - iy if (m >> 1) & 1 else iy
            pc = 1 - ic if m & 1 else ic
            return (px, py, pc), 4 * px + 2 * py + pc

        def copy(k, j, src_ref, slot, to):
            return pltpu.make_async_remote_copy(src_ref=src_ref, dst_ref=outs[k].at[slot], send_sem=send_sems.at[k, j],
                                                recv_sem=recv_sems.at[k, j], device_id=to, device_id_type=MESH)

        local, sends, arrivals, forwards = [], [], [], []
        for k in range(n):
            if modes[k] == "scatter":
                local.append(pltpu.make_async_copy(ins[k].at[me], outs[k].at[me], local_sems.at[k]))
                for m in range(1, NDEV):
                    to, p = peer(m)
                    sends.append(copy(k, m - 1, ins[k].at[p], me, to))
                    arrivals.append(copy(k, m - 1, ins[k].at[p], p, to))
            elif modes[k] == "chips":
                mine = me // 2
                local.append(pltpu.make_async_copy(ins[k].at[mine], outs[k].at[mine], local_sems.at[k]))
                for j, m in enumerate(chips):
                    to, p = peer(m)
                    sends.append(copy(k, j, ins[k].at[p // 2], mine, to))
                    arrivals.append(copy(k, j, ins[k].at[p // 2], p // 2, to))
            else:
                local.append(pltpu.make_async_copy(ins[k], outs[k].at[me], local_sems.at[k]))
                sib, sib_slot = peer(1)
                sends.append(copy(k, 0, ins[k], me, sib))
                for j, m in enumerate(chips):
                    to, p = peer(m)
                    sends.append(copy(k, 1 + j, ins[k], me, to))
                    forwards.append((copy(k, 1 + j, ins[k], p, to), copy(k, 4 + j, outs[k].at[p], p, sib)))
                    arrivals.append(copy(k, 4 + j, ins[k], peer(m ^ 1)[1], sib))
                arrivals.append(copy(k, 0, ins[k], sib_slot, sib))
        for cp in local + sends:
            cp.start()
        for arrived, onward in forwards:
            arrived.wait_recv()
            onward.start()
        for cp in arrivals:
            cp.wait_recv()
        for cp in sends + [onward for _, onward in forwards]:
            cp.wait_send()
        for cp in local:
            cp.wait()

    any_spec = pl.BlockSpec(memory_space=pl.ANY)
    return pl.pallas_call(
        body, name=name, out_shape=out_shape,
        in_specs=[any_spec] * n, out_specs=tuple([any_spec] * n),
        scratch_shapes=[pltpu.SemaphoreType.DMA((n, NDEV - 1)), pltpu.SemaphoreType.DMA((n, NDEV - 1)),
                        pltpu.SemaphoreType.DMA((n,))],
    )(*arrays)


def _pair_sum_call(mine, send, name):
    n = len(mine)

    def body(*refs):
        mine_r, send_r, out_r, land_r = (refs[j * n:(j + 1) * n] for j in range(4))
        send_sems, recv_sems = refs[4 * n:]
        sibling = (lax.axis_index("x"), lax.axis_index("y"), 1 - lax.axis_index("c"))
        swaps = [pltpu.make_async_remote_copy(src_ref=send_r[k], dst_ref=land_r[k], send_sem=send_sems.at[k],
                                              recv_sem=recv_sems.at[k], device_id=sibling, device_id_type=MESH)
                 for k in range(n)]
        for cp in swaps:
            cp.start()
        for k, cp in enumerate(swaps):
            cp.wait()
            out_r[k][...] = (mine_r[k][...].astype(F32) + land_r[k][...].astype(F32)).astype(out_r[k].dtype)

    return pl.pallas_call(
        body, name=name, out_shape=tuple(jax.ShapeDtypeStruct(a.shape, a.dtype) for a in mine),
        scratch_shapes=[pltpu.VMEM(a.shape, a.dtype) for a in mine] + [pltpu.SemaphoreType.DMA((n,)),
                                                                         pltpu.SemaphoreType.DMA((n,))],
        compiler_params=pltpu.CompilerParams(vmem_limit_bytes=VMEM_LIMIT),
    )(*mine, *send)


def _mod_call(c_all, w_ada, b_cols):
    def body(c_ref, w_ref, b_ref, o_ref):
        o_ref[...] = _dot(_silu(c_ref[...]), w_ref[...]) + b_ref[...]

    return pl.pallas_call(body, name="mod_fwd",
                          out_shape=jax.ShapeDtypeStruct((c_all.shape[0], w_ada.shape[1]), F32))(c_all, w_ada, b_cols)


def _wada_grad_call(c_all, dmod_cols):
    def body(c_ref, d_ref, o_ref):
        o_ref[...] = _dot_tn(_silu(c_ref[...]), d_ref[...])

    return pl.pallas_call(body, name="w_ada_grad",
                          out_shape=jax.ShapeDtypeStruct((c_all.shape[1], dmod_cols.shape[1]), F32))(c_all, dmod_cols)


def _full(shape):
    nd = len(shape)
    return pl.BlockSpec(shape, lambda *_: (0,) * nd)


def _in_proj_call(x2, shift, scale, g_pre, w_in, qg, kg, cos, sin, bd, T):
    R = x2.shape[0]
    TT = min(ROW_TILE, T)
    tpe = T // TT

    def body(x_ref, sh_ref, sc_ref, gp_ref, w_ref, qg_ref, kg_ref, cos_ref, sin_ref, bd_ref,
             hb_ref, qr_ref, kpad_ref, vpad_ref, qraw_ref, kraw_ref, gatt_ref, rin_ref, grw_ref):
        h = _pre_fn(x_ref[...], sh_ref[0], sc_ref[0], gp_ref[...])
        hb = h.astype(MXU_DTYPE)
        hb_ref[...] = hb

        def proj(c0, c1):
            return _dot_nt(hb, w_ref[c0:c1, :])

        q = proj(C_Q, C_K)
        k = proj(C_K, C_V)
        v = proj(C_V, C_GA)
        gatt_ref[...] = proj(C_GA, C_RIN)
        rin_ref[...] = proj(C_RIN, C_GRW)
        grw_ref[...] = proj(C_GRW, C_END)
        qraw_ref[...] = q
        kraw_ref[...] = k
        cos, sin, bd = cos_ref[...], sin_ref[...], bd_ref[...]
        qr = _qk_fn(q, qg_ref[...], jnp.tile(cos, (1, 4)), jnp.tile(sin, (1, 4)), bd, ATT_SCALE, False)
        qr_ref[...] = qr.astype(MXU_DTYPE)
        kr = _qk_fn(k, kg_ref[...], cos, sin, bd, 1.0, False)
        left = lax.broadcasted_iota(jnp.int32, (1, KV_W), 1) < HEAD_DIM
        for ref, val in ((kpad_ref, kr), (vpad_ref, v)):
            h0l = jnp.where(left, val, 0.0)
            h1r = jnp.where(left, 0.0, val)
            ref[0] = h0l.astype(MXU_DTYPE)
            ref[1] = pltpu.roll(h0l, HEAD_DIM, 1).astype(MXU_DTYPE)
            ref[2] = pltpu.roll(h1r, HEAD_DIM, 1).astype(MXU_DTYPE)
            ref[3] = h1r.astype(MXU_DTYPE)

    row = lambda w: pl.BlockSpec((TT, w), lambda i: (i, 0))
    per_ex = pl.BlockSpec((1, 1, D_MODEL), lambda i: (i // tpe, 0, 0))
    tab = pl.BlockSpec((TT, KV_W), lambda i: (i % tpe, 0))
    pad = pl.BlockSpec((4, TT, KV_W), lambda i: (0, i, 0))
    sds = jax.ShapeDtypeStruct
    return pl.pallas_call(
        body, name="in_proj", grid=(R // TT,),
        in_specs=[row(D_MODEL), per_ex, per_ex, _full((1, D_MODEL)), _full(w_in.shape), _full((1, ATT_W)),
                  _full((1, KV_W)), tab, tab, _full((256, 256))],
        out_specs=(row(D_MODEL), row(ATT_W), pad, pad, row(ATT_W), row(KV_W), row(ATT_W), row(SHIFT_W), row(RWKV_W)),
        out_shape=(sds((R, D_MODEL), MXU_DTYPE), sds((R, ATT_W), MXU_DTYPE), sds((4, R, KV_W), MXU_DTYPE),
                   sds((4, R, KV_W), MXU_DTYPE), sds((R, ATT_W), F32), sds((R, KV_W), F32), sds((R, ATT_W), F32),
                   sds((R, SHIFT_W), F32), sds((R, RWKV_W), F32)),
        compiler_params=_cp(("arbitrary",)),
    )(x2, shift, scale, g_pre, w_in, qg, kg, cos, sin, bd)


def _softmax_parts(s):
    e = jnp.exp(s - jnp.max(s, axis=1, keepdims=True))
    return e, 1.0 / jnp.sum(e, axis=1, keepdims=True)


def _att_specs(T, TQ):
    nq = T // TQ
    qspec = pl.BlockSpec((TQ, KV_W), lambda b, p, i: (b * nq + i, p))
    side = lambda s: pl.BlockSpec((None, T, KV_W), lambda b, p, i: (2 * (p // 2) + s, b, 0))
    return nq, qspec, side


def _att_fwd_call(qr, kpad, vpad, B, T):
    TQ = min(ATT_TILE_FWD, T)
    nq, qspec, side = _att_specs(T, TQ)

    def body(q_ref, kl_ref, kr_ref, vl_ref, vr_ref, o_ref):
        q = q_ref[...]
        ea, inv_a = _softmax_parts(_dot_nt(q, kl_ref[...]))
        eb, inv_b = _softmax_parts(_dot_nt(q, kr_ref[...]))
        o_ref[...] = _dot(ea, vl_ref[...]) * inv_a + _dot(eb, vr_ref[...]) * inv_b

    return pl.pallas_call(
        body, name="att_fwd", grid=(B, 4, nq),
        in_specs=[qspec, side(0), side(1), side(0), side(1)], out_specs=qspec,
        out_shape=jax.ShapeDtypeStruct((B * T, ATT_W), F32),
        compiler_params=_cp(("arbitrary",) * 3),
    )(qr, kpad, kpad, vpad, vpad)


def _att_bwd_call(qr, kpad, vpad, d_o, B, T):
    TQ = min(ATT_TILE_BWD, T)
    nq, qspec, side = _att_specs(T, TQ)

    def body(q_ref, kl_ref, kr_ref, vl_ref, vr_ref, do_ref, dq_ref, dk_ref, dv_ref):
        i = pl.program_id(2)
        q, do = q_ref[...], do_ref[...]
        left = lax.broadcasted_iota(jnp.int32, (1, KV_W), 1) < HEAD_DIM
        dq = jnp.zeros((TQ, KV_W), F32)
        dk = jnp.zeros((T, KV_W), F32)
        dv = jnp.zeros((T, KV_W), F32)
        for k_ref, v_ref, mask in ((kl_ref, vl_ref, left), (kr_ref, vr_ref, jnp.logical_not(left))):
            kk, vv = k_ref[...], v_ref[...]
            e, inv = _softmax_parts(_dot_nt(q, kk))
            dp = _dot_nt(do, vv)
            ds = e * (dp - inv * jnp.sum(e * dp, axis=1, keepdims=True))
            dq = dq + _dot(ds, kk) * inv
            dk = dk + _dot_tn(ds, jnp.where(mask, q * inv, 0.0))
            dv = dv + _dot_tn(e, jnp.where(mask, do * inv, 0.0))
        dq_ref[...] = dq

        @pl.when(i == 0)
        def _():
            dk_ref[...] = dk
            dv_ref[...] = dv

        @pl.when(i > 0)
        def _():
            dk_ref[...] += dk
            dv_ref[...] += dv

    acc = pl.BlockSpec((None, T, KV_W), lambda b, p, i: (p, b, 0))
    sds = jax.ShapeDtypeStruct
    return pl.pallas_call(
        body, name="att_bwd", grid=(B, 4, nq),
        in_specs=[qspec, side(0), side(1), side(0), side(1), qspec], out_specs=(qspec, acc, acc),
        out_shape=(sds((B * T, ATT_W), F32), sds((4, B * T, KV_W), F32), sds((4, B * T, KV_W), F32)),
        compiler_params=_cp(("arbitrary",) * 3),
    )(qr, kpad, kpad, vpad, vpad, d_o)


def _shift_specs(R, T, TT, width):
    tpe = T // TT
    nb8 = R // 8
    cur = pl.BlockSpec((TT, width), lambda i: (i, 0))
    prev = pl.BlockSpec((8, width), lambda i: (jnp.maximum(i * (TT // 8) - 1, 0), 0))
    nxt = pl.BlockSpec((8, width), lambda i: (jnp.minimum((i + 1) * (TT // 8), nb8 - 1), 0))
    return tpe, cur, prev, nxt


def _neighbours(cur, prev8, next8, i, tpe, TT):
    rows = lax.broadcasted_iota(jnp.int32, (TT, 1), 0)
    first = jnp.where(i % tpe == 0, 0.0, 1.0)
    last = jnp.where(i % tpe == tpe - 1, 0.0, 1.0)
    before = jnp.where(rows == 0, prev8[7:8, :] * first, pltpu.roll(cur, 1, 0))
    after = jnp.where(rows == TT - 1, next8[0:1, :] * last, pltpu.roll(cur, TT - 1, 0))
    return before, after


def _shift_fwd_call(x, taps, T):
    R, width = x.shape
    TT = min(ROW_TILE, T)
    tpe, cur, prev, nxt = _shift_specs(R, T, TT, width)

    def body(x_ref, p_ref, n_ref, t_ref, o_ref):
        xc = x_ref[...]
        before, after = _neighbours(xc, p_ref[...], n_ref[...], pl.program_id(0), tpe, TT)
        o_ref[...] = t_ref[0:1, :] * before + t_ref[1:2, :] * xc + t_ref[2:3, :] * after

    return pl.pallas_call(
        body, name="shift_fwd", grid=(R // TT,), in_specs=[cur, prev, nxt, _full(taps.shape)], out_specs=cur,
        out_shape=jax.ShapeDtypeStruct((R, width), F32), compiler_params=_cp(("arbitrary",)),
    )(x, x, x, taps)


def _shift_bwd_call(x, d, taps, T):
    R, width = x.shape
    TT = min(ROW_TILE, T)
    tpe, cur, prev, nxt = _shift_specs(R, T, TT, width)

    def body(x_ref, xp_ref, xn_ref, d_ref, dp_ref, dn_ref, t_ref, dx_ref, dt_ref):
        i = pl.program_id(0)
        xc, dc = x_ref[...], d_ref[...]
        d_before, d_after = _neighbours(dc, dp_ref[...], dn_ref[...], i, tpe, TT)
        dx_ref[...] = t_ref[2:3, :] * d_before + t_ref[1:2, :] * dc + t_ref[0:1, :] * d_after
        x_before, x_after = _neighbours(xc, xp_ref[...], xn_ref[...], i, tpe, TT)
        @pl.when(i == 0)
        def _():
            dt_ref[...] = jnp.zeros_like(dt_ref)

        for j, xs in enumerate((x_before, xc, x_after)):
            dt_ref[j:j + 1, :] += jnp.sum(dc * xs, axis=0, keepdims=True)

    return pl.pallas_call(
        body, name="shift_bwd", grid=(R // TT,),
        in_specs=[cur, prev, nxt, cur, prev, nxt, _full(taps.shape)], out_specs=(cur, _full((8, width))),
        out_shape=(jax.ShapeDtypeStruct((R, width), F32), jax.ShapeDtypeStruct((8, width), F32)),
        compiler_params=_cp(("arbitrary",)),
    )(x, x, x, d, d, d, taps)


def _lora_in(wa):
    lane = lax.broadcasted_iota(jnp.int32, (1, LORA_W), 1)
    return jnp.where(lane < LORA_W // 2, jnp.tanh(wa), wa)


def _rwkv_prep_call(shifted, wup, aup, w0, a0, k_k, k_a, bd, T):
    R = shifted.shape[0]
    TT = min(ROW_TILE, T)

    def body(k_ref, wa_ref, wup_ref, aup_ref, w0_ref, a0_ref, kk_ref, ka_ref, bd_ref, w_o, kt_o, akk_o, kk_o):
        twa = _lora_in(wa_ref[...])
        pre = [_dot(twa, m_ref[z]) for m_ref in (wup_ref, aup_ref) for z in range(2)]
        outs = _rwkv_pw(k_ref[...], pre[0], pre[1], pre[2], pre[3], w0_ref[...], a0_ref[...], kk_ref[...],
                        ka_ref[...], bd_ref[...], False)
        w_o[0], w_o[1], kt_o[0], kt_o[1], akk_o[0], akk_o[1] = outs[:6]
        kk_o[...] = outs[6]

    col = lambda c, w: pl.BlockSpec((TT, w), lambda i: (i, c))
    two = pl.BlockSpec((2, TT, RWKV_W), lambda i: (0, i, 0))
    sds = jax.ShapeDtypeStruct
    return pl.pallas_call(
        body, name="rwkv_prep", grid=(R // TT,),
        in_specs=[col(1, RWKV_W), col(3 * RWKV_W // LORA_W, LORA_W), _full(wup.shape), _full(aup.shape),
                  _full((2, RWKV_W)), _full((2, RWKV_W)), _full((1, RWKV_W)), _full((1, RWKV_W)), _full((256, 256))],
        out_specs=(two, two, two, col(0, RWKV_W)),
        out_shape=(sds((2, R, RWKV_W), F32),) * 3 + (sds((R, RWKV_W), F32),),
        compiler_params=_cp(("arbitrary",)),
    )(shifted, shifted, wup, aup, w0, a0, k_k, k_a, bd)


def _rwkv_prep_bwd_call(shifted, cts, wup, aup, w0, a0, k_k, k_a, bd, T):
    R = shifted.shape[0]
    TT = min(ROW_TILE, T)

    def body(k_ref, wa_ref, dw0, dkt0, dakk0, dkk0, dr0, dv0, dw1, dkt1, dakk1, dkk1, dr1, dv1, dr2_ref, dv2_ref, dkts_ref,
             wup_ref, aup_ref, w0_ref, a0_ref, kk_ref, ka_ref, bd_ref,
             dsh_ref, gwup_ref, gaup_ref, gw0_ref, ga0_ref, gkk_ref, gka_ref):
        dw_ref, dkt_ref, dakk_ref, dkk_ref, dr_ref, dv_ref = ((dw0, dw1), (dkt0, dkt1), (dakk0, dakk1), (dkk0, dkk1),
                                                              (dr0, dr1), (dv0, dv1))
        i = pl.program_id(0)
        wa = wa_ref[...]
        twa = _lora_in(wa)
        pre = [_dot(twa, m_ref[z]) for m_ref in (wup_ref, aup_ref) for z in range(2)]
        fn = functools.partial(_rwkv_pw, bd=bd_ref[...], diff=True)
        _, vjp = jax.vjp(fn, k_ref[...], pre[0], pre[1], pre[2], pre[3], w0_ref[...], a0_ref[...], kk_ref[...],
                         ka_ref[...])
        dkts = dkts_ref[...]
        dk, dpw0, dpw1, dpa0, dpa1, gw0, ga0, gkk, gka = vjp(
            (dw_ref[0][...], dw_ref[1][...], dkt_ref[0][...] + dkts, dkt_ref[1][...] + dkts, dakk_ref[0][...],
             dakk_ref[1][...], dkk_ref[0][...] + dkk_ref[1][...]))
        dtwa = (_dot_nt(dpw0, wup_ref[0]) + _dot_nt(dpw1, wup_ref[1]) + _dot_nt(dpa0, aup_ref[0])
                + _dot_nt(dpa1, aup_ref[1]))
        lane = lax.broadcasted_iota(jnp.int32, (1, LORA_W), 1)
        dsh_ref[:, 0:RWKV_W] = dr_ref[0][...] + dr_ref[1][...] + dr2_ref[...]
        dsh_ref[:, RWKV_W:2 * RWKV_W] = dk
        dsh_ref[:, 2 * RWKV_W:3 * RWKV_W] = dv_ref[0][...] + dv_ref[1][...] + dv2_ref[...]
        dsh_ref[:, 3 * RWKV_W:] = jnp.where(lane < LORA_W // 2, dtwa * (1.0 - twa * twa), dtwa)
        acc = ((gwup_ref.at[0], _dot_tn(twa, dpw0)), (gwup_ref.at[1], _dot_tn(twa, dpw1)),
               (gaup_ref.at[0], _dot_tn(twa, dpa0)), (gaup_ref.at[1], _dot_tn(twa, dpa1)),
               (gw0_ref, gw0), (ga0_ref, ga0), (gkk_ref, gkk), (gka_ref, gka))

        @pl.when(i == 0)
        def _():
            for ref, val in acc:
                ref[...] = val

        @pl.when(i > 0)
        def _():
            for ref, val in acc:
                ref[...] += val

    col = lambda c, w: pl.BlockSpec((TT, w), lambda i: (i, c))
    one = col(0, RWKV_W)
    sds = jax.ShapeDtypeStruct
    return pl.pallas_call(
        body, name="rwkv_prep_bwd", grid=(R // TT,),
        in_specs=[col(1, RWKV_W), col(3 * RWKV_W // LORA_W, LORA_W)] + [one] * 15 + [
                  _full(wup.shape), _full(aup.shape), _full((2, RWKV_W)), _full((2, RWKV_W)), _full((1, RWKV_W)),
                  _full((1, RWKV_W)), _full((256, 256))],
        out_specs=(pl.BlockSpec((TT, SHIFT_W), lambda i: (i, 0)), _full(wup.shape), _full(aup.shape),
                   _full((2, RWKV_W)), _full((2, RWKV_W)), _full((1, RWKV_W)), _full((1, RWKV_W))),
        out_shape=(sds((R, SHIFT_W), F32), sds(wup.shape, F32), sds(aup.shape, F32), sds((2, RWKV_W), F32),
                   sds((2, RWKV_W), F32), sds((1, RWKV_W), F32), sds((1, RWKV_W), F32)),
        compiler_params=_cp(("arbitrary",)),
    )(shifted, shifted, *cts, wup, aup, w0, a0, k_k, k_a, bd)


def _col_lhs(row, eye_b):
    return eye_b * row.astype(MXU_DTYPE)


def _colsum(x):
    return jnp.sum(x, axis=0, keepdims=True)


def _stacked_segsum(tiles, bd):
    res = _seg_dot(jnp.concatenate(tiles, axis=0), bd)
    return [res[j * HEAD_DIM:(j + 1) * HEAD_DIM] for j in range(len(tiles))]


def _scan_specs(B, T, C, nC):
    def blk(z, col, rev):
        idx = (lambda g: (z, 0, nC - 1 - g, col)) if rev else (lambda g: (z, 0, g, col))
        return pl.BlockSpec((None, B, C, RWKV_W), idx)

    def blk3(col, rev):
        idx = (lambda g: (0, nC - 1 - g, col)) if rev else (lambda g: (0, g, col))
        return pl.BlockSpec((B, C, RWKV_W), idx)

    return blk, blk3


def _scan_fwd_call(w, kt, akk, kk, shifted, eye_b, eye_f, bd, B, T):
    C = min(SCAN_CHUNK, T)
    nC = T // C
    blk, blk3 = _scan_specs(B, T, C, nC)

    def body(w0, kt0, akk0, kk0, v0, r0, w1, kt1, akk1, kk1, v1, r1, eb_ref, ef_ref, bd_ref, y0, y1, st, S):
        @pl.when(pl.program_id(0) == 0)
        def _():
            S[...] = jnp.zeros_like(S)

        st[0] = S[...].astype(MXU_DTYPE)
        dirs = ((w0, kt0, akk0, kk0, v0, r0, y0), (w1, kt1, akk1, kk1, v1, r1, y1))

        def step(s, carry):
            for z in range(2):
                row = s if z == 0 else C - 1 - s
                prev = jnp.maximum(s - 1, 0) if z == 0 else jnp.minimum(C - s, C - 1)
                wr, ktr, akkr, kkr, vr, rr, yr = dirs[z]
                tiles = []
                for b in range(B):
                    Sb = st[s, z * B + b]
                    tiles += [Sb * kkr[b, pl.ds(row, 1), :].astype(MXU_DTYPE),
                              _col_lhs(vr[b, pl.ds(row, 1), :], eb_ref[...]),
                              Sb * rr[b, pl.ds(prev, 1), :].astype(MXU_DTYPE)]
                res = _stacked_segsum(tiles, bd_ref[...])
                for b in range(B):
                    c = z * B + b
                    sab, vb, yb = res[3 * b:3 * b + 3]
                    ld = lambda ref: ref[b, pl.ds(row, 1), :]
                    Sn = S[c] * ld(wr) - sab * ld(akkr) + vb * ld(ktr)
                    S[c] = Sn
                    st[s + 1, c] = Sn.astype(MXU_DTYPE)
                    yr[b, pl.ds(prev, 1), :] = _colsum(ef_ref[...] * yb)
            return carry

        lax.fori_loop(0, C, step, 0, unroll=SCAN_UNROLL)
        for z in range(2):
            last = C - 1 if z == 0 else 0
            rr, yr = dirs[z][5], dirs[z][6]
            res = _stacked_segsum([st[C, z * B + b] * rr[b, last:last + 1, :].astype(MXU_DTYPE) for b in range(B)],
                                  bd_ref[...])
            for b in range(B):
                yr[b, last:last + 1, :] = _colsum(ef_ref[...] * res[b])

    ins, specs = [], []
    for z, rev in ((0, False), (1, True)):
        ins += [w, kt, akk, kk, shifted, shifted]
        specs += [blk(z, 0, rev), blk(z, 0, rev), blk(z, 0, rev), blk3(0, rev), blk3(2, rev), blk3(0, rev)]
    sds = jax.ShapeDtypeStruct
    return pl.pallas_call(
        body, name="scan_fwd", grid=(nC,),
        in_specs=specs + [_full((HEAD_DIM, RWKV_W)), _full((HEAD_DIM, RWKV_W)), _full((256, 256))],
        out_specs=(blk3(0, False), blk3(0, True),
                   pl.BlockSpec((None, C + 1, 2 * B, HEAD_DIM, RWKV_W), lambda g: (g, 0, 0, 0, 0))),
        out_shape=(sds((B, T, RWKV_W), F32), sds((B, T, RWKV_W), F32),
                   sds((nC, C + 1, 2 * B, HEAD_DIM, RWKV_W), MXU_DTYPE)),
        scratch_shapes=[pltpu.VMEM((2 * B, HEAD_DIM, RWKV_W), F32)],
        compiler_params=_cp(("arbitrary",)),
    )(*ins, eye_b, eye_f, bd)


def _scan_bwd_call(w, kt, akk, kk, shifted, dys, st, eye_b, eye_f, bd, B, T):
    C = min(SCAN_CHUNK, T)
    nC = T // C
    blk, blk3 = _scan_specs(B, T, C, nC)
    nin = 7

    def body(*refs):
        d0, d1 = refs[:nin], refs[nin:2 * nin]
        st_ref, eb_ref, ef_ref, sel_ref, bd_ref = refs[2 * nin:2 * nin + 5]
        o0, o1 = refs[2 * nin + 5:2 * nin + 11], refs[2 * nin + 11:2 * nin + 17]
        COL, DYC, G = refs[2 * nin + 17:]

        @pl.when(pl.program_id(0) == 0)
        def _():
            G[...] = jnp.zeros_like(G)

        dirs = (d0 + (o0,), d1 + (o1,))

        def column_operands(s, z):
            row = s if z == 0 else C - 1 - s
            _, _, _, kkr, vr, _, dyr, _ = dirs[z]
            tiles = []
            for b in range(B):
                tiles += [st_ref[s, z * B + b] * kkr[b, pl.ds(row, 1), :].astype(MXU_DTYPE),
                          _col_lhs(vr[b, pl.ds(row, 1), :], eb_ref[...]),
                          _col_lhs(dyr[b, pl.ds(row, 1), :], eb_ref[...])]
            return tiles

        def keep_columns(res, z):
            for b in range(B):
                for k in range(3):
                    COL[k, z * B + b] = res[3 * b + k].astype(MXU_DTYPE)
                DYC[z * B + b] = res[3 * b + 2]

        for z in range(2):
            keep_columns(_stacked_segsum(column_operands(C - 1, z), bd_ref[...]), z)

        def bwd(it, carry):
            s = C - 1 - it
            for z in range(2):
                row = s if z == 0 else C - 1 - s
                wr, ktr, akkr, kkr, vr, rr, dyr, (dw_o, dkt_o, dakk_o, dkk_o, dr_o, dv_o) = dirs[z]
                tiles, Gcs = [], []
                for b in range(B):
                    c = z * B + b
                    Gc = G[c] + DYC[c] * rr[b, pl.ds(row, 1), :]
                    Gb = Gc.astype(MXU_DTYPE)
                    Gcs.append((Gc, Gb))
                    tiles += [Gb * akkr[b, pl.ds(row, 1), :].astype(MXU_DTYPE),
                              Gb * ktr[b, pl.ds(row, 1), :].astype(MXU_DTYPE)]
                res = _stacked_segsum(tiles + column_operands(jnp.maximum(s - 1, 0), z), bd_ref[...])
                for b in range(B):
                    c = z * B + b
                    Gc, Gb = Gcs[b]
                    gab, dvb = res[2 * b], res[2 * b + 1]
                    ld = lambda ref: ref[b, pl.ds(row, 1), :]
                    G[c] = Gc * ld(wr) - gab * ld(kkr)
                    Sb = st_ref[s, c]
                    prods = jnp.concatenate([st_ref[s + 1, c] * COL[2, c], Gb * COL[1, c], Gb * Sb, Gb * COL[0, c]], axis=0)
                    sums = jnp.dot(sel_ref[...], prods, preferred_element_type=F32)
                    for k, (ref, sign) in enumerate(((dr_o, 1.0), (dkt_o, 1.0), (dw_o, 1.0), (dakk_o, -1.0))):
                        ref[b, pl.ds(row, 1), :] = sign * sums[k:k + 1, :]
                    dv_o[b, pl.ds(row, 1), :] = _colsum(ef_ref[...] * dvb)
                    dkk_o[b, pl.ds(row, 1), :] = -_colsum(gab * Sb.astype(F32))
                keep_columns(res[2 * B:], z)
            return carry

        lax.fori_loop(0, C, bwd, 0, unroll=SCAN_UNROLL)

    ins, specs = [], []
    for z, rev in ((0, True), (1, False)):
        ins += [w, kt, akk, kk, shifted, shifted, dys]
        specs += [blk(z, 0, rev), blk(z, 0, rev), blk(z, 0, rev), blk3(0, rev), blk3(2, rev), blk3(0, rev), blk3(0, rev)]
    sel = (jnp.arange(16)[:, None] == (jnp.arange(4 * HEAD_DIM) // HEAD_DIM)[None, :]).astype(MXU_DTYPE)
    ins += [st, eye_b, eye_f, sel, bd]
    specs += [pl.BlockSpec((None, C + 1, 2 * B, HEAD_DIM, RWKV_W), lambda g: (nC - 1 - g, 0, 0, 0, 0)),
              _full((HEAD_DIM, RWKV_W)), _full((HEAD_DIM, RWKV_W)), _full(sel.shape), _full((256, 256))]
    sds = jax.ShapeDtypeStruct
    out_specs = tuple(blk3(0, True) for _ in range(6)) + tuple(blk3(0, False) for _ in range(6))
    res = pl.pallas_call(
        body, name="scan_bwd", grid=(nC,), in_specs=specs, out_specs=out_specs,
        out_shape=tuple(sds((B, T, RWKV_W), F32) for _ in range(12)),
        scratch_shapes=[pltpu.VMEM((3, 2 * B, HEAD_DIM, RWKV_W), MXU_DTYPE), pltpu.VMEM((2 * B, HEAD_DIM, RWKV_W), F32),
                        pltpu.VMEM((2 * B, HEAD_DIM, RWKV_W), F32)],
        compiler_params=_cp(("arbitrary",)),
    )(*ins)
    return list(res)


def _out_head_call(x2, tgt2, gate, y_att, g_att, y0, y1, shifted, kt, g_rw, w_out, g_post, gn_w, gn_b, r_k, bd, T):
    R = x2.shape[0]
    TT = min(ROW_TILE, T)
    tpe = T // TT

    def body(x_ref, t_ref, gate_ref, ya_ref, ga_ref, y0_ref, y1_ref, r_ref, v_ref, kt_ref, grw_ref, w_ref, gp_ref,
             gnw_ref, gnb_ref, rk_ref, bd_ref,
             loss_o, dy_o, dya_o, dga_o, dys_o, dr_o, dv_o, dkts_o, dgrw_o, dgate_o, gw_o, ggp_o, ggnw_o, ggnb_o, grk_o):
        i = pl.program_id(0)
        bd = bd_ref[...]
        mix = functools.partial(_mix_fn, bd=bd, diff=True)
        (ma, mr), mix_vjp = jax.vjp(mix, ya_ref[...], ga_ref[...], y0_ref[...] + y1_ref[...], r_ref[...], v_ref[...],
                                    kt_ref[0] + kt_ref[1], grw_ref[...], gnw_ref[...], gnb_ref[...], rk_ref[...])
        out = _dot(ma, w_ref[0:ATT_W, :]) + _dot(mr, w_ref[ATT_W:, :])
        loss, loss_vjp = jax.vjp(_loss_fn, out, x_ref[...], t_ref[...], gate_ref[0], gp_ref[...])
        d_out, dy, _, dgate, dgp = loss_vjp(jnp.ones((1, 1), F32))
        dy_o[...] = dy
        dma = _dot_nt(d_out, w_ref[0:ATT_W, :])
        dmr = _dot_nt(d_out, w_ref[ATT_W:, :])
        dya_o[...], dga_o[...], dys_o[...], dr_o[...], dv_o[...], dkts_o[...], dgrw_o[...], dgnw, dgnb, drk = \
            mix_vjp((dma, dmr))
        gw = jnp.concatenate([_dot_tn(ma, d_out), _dot_tn(mr, d_out)], axis=0)
        acc = ((loss_o, jnp.broadcast_to(loss, (8, 128))), (gw_o, gw), (ggp_o, dgp), (ggnw_o, dgnw), (ggnb_o, dgnb),
               (grk_o, drk))

        @pl.when(i == 0)
        def _():
            for ref, val in acc:
                ref[...] = val

        @pl.when(i > 0)
        def _():
            for ref, val in acc:
                ref[...] += val

        @pl.when(i % tpe == 0)
        def _():
            dgate_o[0] = dgate

        @pl.when(i % tpe > 0)
        def _():
            dgate_o[0] += dgate

    row = lambda w, c=0: pl.BlockSpec((TT, w), lambda i: (i, c))
    two = pl.BlockSpec((2, TT, RWKV_W), lambda i: (0, i, 0))
    per_ex = pl.BlockSpec((1, 1, D_MODEL), lambda i: (i // tpe, 0, 0))
    sds = jax.ShapeDtypeStruct
    r512 = sds((R, RWKV_W), F32)
    return pl.pallas_call(
        body, name="out_head", grid=(R // TT,),
        in_specs=[row(D_MODEL), row(D_MODEL), per_ex, row(ATT_W), row(ATT_W), row(RWKV_W), row(RWKV_W), row(RWKV_W, 0),
                  row(RWKV_W, 2), two,
                  row(RWKV_W), _full(w_out.shape), _full((1, D_MODEL)), _full((1, RWKV_W)), _full((1, RWKV_W)),
                  _full((1, RWKV_W)), _full((256, 256))],
        out_specs=(_full((8, 128)), row(D_MODEL), row(ATT_W), row(ATT_W), row(RWKV_W), row(RWKV_W), row(RWKV_W),
                   row(RWKV_W), row(RWKV_W), per_ex, _full((D_MODEL, D_MODEL)), _full((1, D_MODEL)), _full((1, RWKV_W)),
                   _full((1, RWKV_W)), _full((1, RWKV_W))),
        out_shape=(sds((8, 128), F32), sds((R, D_MODEL), F32), r512, r512, r512, r512, r512, r512, r512,
                   sds((R // T, 1, D_MODEL), F32), sds((D_MODEL, D_MODEL), F32), sds((1, D_MODEL), F32),
                   sds((1, RWKV_W), F32), sds((1, RWKV_W), F32), sds((1, RWKV_W), F32)),
        compiler_params=_cp(("arbitrary",)),
    )(x2, tgt2, gate, y_att, g_att, y0, y1, shifted, shifted, kt, g_rw, w_out, g_post, gn_w, gn_b, r_k, bd)


def _in_proj_bwd_call(x2, dy, shift, scale, g_pre, w_in, qg, kg, cos, sin, bd, q_raw, k_raw, dqr, dkp, dvp,
                      d_gatt, d_rin, d_grw, T):
    R = x2.shape[0]
    TT = min(ROW_TILE, T)
    tpe = T // TT

    def body(x_ref, dy_ref, sh_ref, sc_ref, gp_ref, w_ref, qg_ref, kg_ref, cos_ref, sin_ref, bd_ref, q_ref, k_ref,
             dqr_ref, dkp_ref, dvp_ref, dga_ref, drin_ref, dgrw_ref,
             dx_o, dproj_o, dsh_o, dsc_o, ggp_o, gqg_o, gkg_o):
        i = pl.program_id(0)
        cos, sin, bd = cos_ref[...], sin_ref[...], bd_ref[...]
        left = lax.broadcasted_iota(jnp.int32, (1, KV_W), 1) < HEAD_DIM

        def kv_grad(ref):
            a = ref[0] + ref[1]
            b = ref[2] + ref[3]
            return jnp.where(left, a + pltpu.roll(a, HEAD_DIM, 1), b + pltpu.roll(b, HEAD_DIM, 1))

        qfn = functools.partial(_qk_fn, cos=jnp.tile(cos, (1, 4)), sin=jnp.tile(sin, (1, 4)), bd=bd, scale=ATT_SCALE,
                                diff=True)
        _, q_vjp = jax.vjp(qfn, q_ref[...], qg_ref[...])
        dq, gqg = q_vjp(dqr_ref[...])
        kfn = functools.partial(_qk_fn, cos=cos, sin=sin, bd=bd, scale=1.0, diff=True)
        _, k_vjp = jax.vjp(kfn, k_ref[...], kg_ref[...])
        dk, gkg = k_vjp(kv_grad(dkp_ref))
        pieces = ((C_Q, C_K, dq), (C_K, C_V, dk), (C_V, C_GA, kv_grad(dvp_ref)), (C_GA, C_RIN, dga_ref[...]),
                  (C_RIN, C_GRW, drin_ref[...]), (C_GRW, C_END, dgrw_ref[...]))
        dh = jnp.zeros((TT, D_MODEL), F32)
        for c0, c1, val in pieces:
            vb = val.astype(MXU_DTYPE)
            dproj_o[:, c0:c1] = vb
            dh = dh + _dot(vb, w_ref[c0:c1, :])
        _, pre_vjp = jax.vjp(_pre_fn, x_ref[...], sh_ref[0], sc_ref[0], gp_ref[...])
        dx, dsh, dsc, ggp = pre_vjp(dh)
        dx_o[...] = dx + dy_ref[...]
        acc = ((ggp_o, ggp), (gqg_o, gqg), (gkg_o, gkg))

        @pl.when(i == 0)
        def _():
            for ref, val in acc:
                ref[...] = val

        @pl.when(i > 0)
        def _():
            for ref, val in acc:
                ref[...] += val

        @pl.when(i % tpe == 0)
        def _():
            dsh_o[0] = dsh
            dsc_o[0] = dsc

        @pl.when(i % tpe > 0)
        def _():
            dsh_o[0] += dsh
            dsc_o[0] += dsc

    row = lambda w: pl.BlockSpec((TT, w), lambda i: (i, 0))
    per_ex = pl.BlockSpec((1, 1, D_MODEL), lambda i: (i // tpe, 0, 0))
    tab = pl.BlockSpec((TT, KV_W), lambda i: (i % tpe, 0))
    pad = pl.BlockSpec((4, TT, KV_W), lambda i: (0, i, 0))
    sds = jax.ShapeDtypeStruct
    nb = R // T
    return pl.pallas_call(
        body, name="in_proj_bwd", grid=(R // TT,),
        in_specs=[row(D_MODEL), row(D_MODEL), per_ex, per_ex, _full((1, D_MODEL)), _full(w_in.shape), _full((1, ATT_W)),
                  _full((1, KV_W)), tab, tab, _full((256, 256)), row(ATT_W), row(KV_W), row(ATT_W), pad, pad,
                  row(ATT_W), row(SHIFT_W), row(RWKV_W)],
        out_specs=(row(D_MODEL), row(C_END), per_ex, per_ex, _full((1, D_MODEL)), _full((1, ATT_W)), _full((1, KV_W))),
        out_shape=(sds((R, D_MODEL), F32), sds((R, C_END), MXU_DTYPE), sds((nb, 1, D_MODEL), F32),
                   sds((nb, 1, D_MODEL), F32), sds((1, D_MODEL), F32), sds((1, ATT_W), F32), sds((1, KV_W), F32)),
        compiler_params=_cp(("arbitrary",)),
    )(x2, dy, shift, scale, g_pre, w_in, qg, kg, cos, sin, bd, q_raw, k_raw, dqr, dkp, dvp, d_gatt, d_rin, d_grw)


def _w_in_grad_call(hb, dproj):
    R = hb.shape[0]
    TT = min(W_GRAD_ROWS, R)
    CB = 1152

    def body(h_ref, d_ref, o_ref):
        g = _dot_tn(h_ref[...], d_ref[...])

        @pl.when(pl.program_id(1) == 0)
        def _():
            o_ref[...] = g

        @pl.when(pl.program_id(1) > 0)
        def _():
            o_ref[...] += g

    return pl.pallas_call(
        body, name="w_in_grad", grid=(C_END // CB, R // TT),
        in_specs=[pl.BlockSpec((TT, D_MODEL), lambda j, i: (i, 0)), pl.BlockSpec((TT, CB), lambda j, i: (i, j))],
        out_specs=pl.BlockSpec((D_MODEL, CB), lambda j, i: (0, j)),
        out_shape=jax.ShapeDtypeStruct((D_MODEL, C_END), F32), compiler_params=_cp(("arbitrary", "arbitrary")),
    )(hb, dproj)


def _adam_call(parts, w, m, v, name, row_tile=None):
    P, M, N = parts.shape
    TM = M if row_tile is None else row_tile

    def body(p_ref, w_ref, m_ref, v_ref, g_o, d_o, m_o, v_o):
        g = p_ref[0].astype(F32)
        for j in range(1, P):
            g = g + p_ref[j].astype(F32)
        m2 = ADAM_B1 * m_ref[...] + (1.0 - ADAM_B1) * g
        v2 = ADAM_B2 * v_ref[...] + (1.0 - ADAM_B2) * jnp.square(g)
        m_hat = m2 / (1.0 - ADAM_B1 ** ADAM_STEP)
        v_hat = v2 / (1.0 - ADAM_B2 ** ADAM_STEP)
        g_o[...] = g
        d_o[...] = -ADAM_LR * (m_hat / (jnp.sqrt(v_hat) + ADAM_EPS) + ADAM_WD * w_ref[...])
        m_o[...] = m2
        v_o[...] = v2

    blk = pl.BlockSpec((TM, N), lambda i: (i, 0))
    return pl.pallas_call(
        body, name=name, grid=(M // TM,),
        in_specs=[pl.BlockSpec((P, TM, N), lambda i: (0, i, 0)), blk, blk, blk], out_specs=(blk,) * 4,
        out_shape=(jax.ShapeDtypeStruct((M, N), F32),) * 4, compiler_params=_cp(("arbitrary",)),
    )(parts, w, m, v)


_SMALL_ROWS = 136


def _pack_small(taps, w_up, w0, a_up, a0):
    flat = jnp.concatenate([taps.reshape(-1), w_up.reshape(-1), w0.reshape(-1), a_up.reshape(-1), a0.reshape(-1)])
    return jnp.pad(flat, (0, _SMALL_ROWS * 128 - flat.shape[0])).reshape(_SMALL_ROWS, 128)


def _unpack_small(packed):
    n = packed.shape[0]
    flat = packed.reshape(n, -1)
    out, o = [], 0
    for shape in ((3, 208), (2, 64, 64), (2, 64), (2, 64, 64), (2, 64)):
        size = 1
        for s in shape:
            size *= s
        out.append(flat[:, o:o + size].reshape((n,) + shape))
        o += size
    return out


def _cols_to_full(blocks):
    nd = blocks.ndim
    moved = jnp.moveaxis(blocks, 0, nd - 2)
    return moved.reshape(moved.shape[:-2] + (moved.shape[-2] * moved.shape[-1],))


def _full_to_cols(full):
    k = full.shape[-1] // NDEV
    return jnp.moveaxis(full.reshape(full.shape[:-1] + (NDEV, k)), -2, 0)


_REP_SIZES = (("g_pre", 1024), ("q_norm_g", 64), ("k_norm_g", 64), ("k_k", 512), ("k_a", 512), ("r_k", 512),
              ("gn_w", 512), ("gn_b", 512), ("g_post", 1024))
_REP_ROWS = 40


def kernel(x, c, w_ada, b_ada, g_pre, w_in, q_norm_g, k_norm_g, shift_taps, w_up, w0, a_up, a0, k_k, k_a, r_k, gn_w, gn_b, w_out, g_post, loss_target, m_w_ada, m_b_ada, m_g_pre, m_w_in, m_q_norm_g, m_k_norm_g, m_shift_taps, m_w_up, m_w0, m_a_up, m_a0, m_k_k, m_k_a, m_r_k, m_gn_w, m_gn_b, m_w_out, m_g_post, v_w_ada, v_b_ada, v_g_pre, v_w_in, v_q_norm_g, v_k_norm_g, v_shift_taps, v_w_up, v_w0, v_a_up, v_a0, v_k_k, v_k_a, v_r_k, v_gn_w, v_gn_b, v_w_out, v_g_post):
    B, T, _ = x.shape
    R = B * T
    me = 4 * lax.axis_index("x") + 2 * lax.axis_index("y") + lax.axis_index("c")
    x2 = x.reshape(R, D_MODEL)
    tgt2 = loss_target.reshape(R, D_MODEL)

    seg = jnp.arange(256) // HEAD_DIM
    bd = (seg[:, None] == seg[None, :]).astype(MXU_DTYPE)
    eye = (jnp.arange(HEAD_DIM)[:, None] == (jnp.arange(RWKV_W) % HEAD_DIM)[None, :])
    eye_b, eye_f = eye.astype(MXU_DTYPE), eye.astype(F32)
    cos, sin = _rope_tables(T)

    c_g, w_in_g, w_out_g, small_g = _exchange(
        [c, w_in[0].T.astype(MXU_DTYPE), w_out[0].astype(MXU_DTYPE),
         _pack_small(shift_taps[0], w_up[0], w0[0], a_up[0], a0[0])], ["all"] * 4, "gather_params")
    c_all = c_g.reshape(NDEV * B, D_MODEL)
    w_in_f = w_in_g.reshape(C_END, D_MODEL)
    w_out_f = w_out_g.reshape(D_MODEL, D_MODEL)
    taps_b, w_up_b, w0_b, a_up_b, a0_b = _unpack_small(small_g)
    taps_f = jnp.pad(_cols_to_full(taps_b), ((0, 5), (0, 0)))
    w_up_f, a_up_f = _cols_to_full(w_up_b), _cols_to_full(a_up_b)
    w0_f, a0_f = _cols_to_full(w0_b), _cols_to_full(a0_b)
    wup_pad = jnp.pad(w_up_f, ((0, 0), (0, 64), (0, 0))).astype(MXU_DTYPE)
    aup_pad = jnp.pad(a_up_f, ((0, 0), (64, 0), (0, 0))).astype(MXU_DTYPE)

    ncol = w_ada.shape[2]
    b_cols = lax.dynamic_slice(b_ada, (0, me * ncol), (1, ncol))
    mod_cols = _mod_call(c_all, w_ada[0].astype(MXU_DTYPE), b_cols)
    (mod_g,) = _exchange([mod_cols], ["all"], "gather_mod")
    mod = lax.dynamic_slice(_cols_to_full(mod_g), (me * B, 0), (B, 3 * D_MODEL))
    shift, scale, gate = [mod[:, j * D_MODEL:(j + 1) * D_MODEL].reshape(B, 1, D_MODEL) for j in range(3)]

    qg = jnp.tile(q_norm_g, (1, ATT_W // HEAD_DIM))
    kg = jnp.tile(k_norm_g, (1, KV_W // HEAD_DIM))
    rk_row = r_k.reshape(1, RWKV_W)

    hb, qr, kpad, vpad, q_raw, k_raw, g_att, rin, g_rw = _in_proj_call(
        x2, shift, scale, g_pre, w_in_f, qg, kg, cos, sin, bd, T)
    y_att = _att_fwd_call(qr, kpad, vpad, B, T)
    shifted = _shift_fwd_call(rin, taps_f, T)
    w_s, kt_s, akk_s, kk_s = _rwkv_prep_call(shifted, wup_pad, aup_pad, w0_f, a0_f, k_k, k_a, bd, T)
    sh3 = shifted.reshape(B, T, SHIFT_W)
    r4 = lambda a: a.reshape(2, B, T, RWKV_W)
    y0, y1, st = _scan_fwd_call(r4(w_s), r4(kt_s), r4(akk_s), kk_s.reshape(B, T, RWKV_W), sh3, eye_b, eye_f, bd, B, T)

    (loss_blk, dy, d_yatt, d_gatt, d_ys, d_r2, d_v2, d_kts, d_grw, d_gate, g_wout, g_gpost, g_gnw, g_gnb,
     g_rk) = _out_head_call(x2, tgt2, gate, y_att, g_att, y0.reshape(R, RWKV_W), y1.reshape(R, RWKV_W), shifted, kt_s,
                            g_rw, w_out_f, g_post, gn_w, gn_b, rk_row, bd, T)
    scan_cts = _scan_bwd_call(r4(w_s), r4(kt_s), r4(akk_s), kk_s.reshape(B, T, RWKV_W), sh3,
                              d_ys.reshape(B, T, RWKV_W), st, eye_b, eye_f, bd, B, T)
    scan_cts = [a.reshape(R, RWKV_W) for a in scan_cts]
    d_shifted, g_wup, g_aup, g_w0, g_a0, g_kk, g_ka = _rwkv_prep_bwd_call(
        shifted, scan_cts + [d_r2, d_v2, d_kts], wup_pad, aup_pad, w0_f, a0_f, k_k, k_a, bd, T)
    d_rin, g_taps = _shift_bwd_call(rin, d_shifted, taps_f, T)
    dqr, dkp, dvp = _att_bwd_call(qr, kpad, vpad, d_yatt, B, T)
    grad_x, dproj, d_shift, d_scale, g_gpre, g_qg, g_kg = _in_proj_bwd_call(
        x2, dy, shift, scale, g_pre, w_in_f, qg, kg, cos, sin, bd, q_raw, k_raw, dqr, dkp, dvp, d_gatt, d_rin, d_grw, T)
    g_win = _w_in_grad_call(hb, dproj)

    rep = jnp.concatenate([g_gpre.reshape(-1), g_qg.reshape(-1, HEAD_DIM).sum(0), g_kg.reshape(-1, HEAD_DIM).sum(0),
                           g_kk.reshape(-1), g_ka.reshape(-1), g_rk.reshape(-1), g_gnw.reshape(-1), g_gnb.reshape(-1),
                           g_gpost.reshape(-1), loss_blk[0, :1]])
    rep = jnp.pad(rep, (0, _REP_ROWS * 128 - rep.shape[0])).reshape(_REP_ROWS, 128)
    dmod = jnp.concatenate([d_shift, d_scale, d_gate], axis=2).reshape(B, 3 * D_MODEL)
    small_parts = jax.vmap(_pack_small)(_full_to_cols(g_taps[:3]), _full_to_cols(g_wup[:, :64, :]), _full_to_cols(g_w0),
                                        _full_to_cols(g_aup[:, 64:, :]), _full_to_cols(g_a0))
    core = lax.axis_index("c")
    halves = [a.reshape((NDEV // 2, 2) + a.shape[1:]).astype(MXU_DTYPE)
              for a in (_full_to_cols(g_win), g_wout.reshape(NDEV, D_MODEL // NDEV, D_MODEL))]
    pick = lambda a, j: lax.dynamic_index_in_dim(a, j, axis=1, keepdims=False)
    s_win, s_wout = _pair_sum_call([pick(a, core) for a in halves], [pick(a, 1 - core) for a in halves], "reduce_pair")
    p_win, p_wout, p_small, dmod_g, rep_g = _exchange(
        [s_win, s_wout, small_parts, dmod, rep], ["chips", "chips", "scatter", "all", "all"], "reduce_grads")
    dmod_all = dmod_g.reshape(NDEV * B, 3 * D_MODEL)
    g_wada = _wada_grad_call(c_all, lax.dynamic_slice(dmod_all, (0, me * ncol), (NDEV * B, ncol)))

    res = {}

    def adam(name, parts, w, m, v, row_tile=None):
        shape = w.shape
        two_d = (-1, shape[-1])
        out = _adam_call(parts.reshape((parts.shape[0],) + w.reshape(two_d).shape), w.reshape(two_d), m.reshape(two_d),
                         v.reshape(two_d), "adam_" + name, row_tile)
        res[name] = [o.reshape(shape) for o in out]

    adam("w_ada", g_wada[None], w_ada, m_w_ada, v_w_ada)
    adam("b_ada", dmod_all.reshape(NDEV * B, 1, 3 * D_MODEL), b_ada, m_b_ada, v_b_ada)
    adam("w_in", p_win, w_in, m_w_in, v_w_in, 128)
    adam("w_out", p_wout, w_out, m_w_out, v_w_out)
    taps_p, wup_p, w0_p, aup_p, a0_p = _unpack_small(p_small)
    adam("shift_taps", taps_p, shift_taps, m_shift_taps, v_shift_taps)
    adam("w_up", wup_p, w_up, m_w_up, v_w_up)
    adam("w0", w0_p, w0, m_w0, v_w0)
    adam("a_up", aup_p, a_up, m_a_up, v_a_up)
    adam("a0", a0_p, a0, m_a0, v_a0)
    rep_flat = rep_g.reshape(NDEV, -1)
    off = 0
    given = dict(g_pre=(g_pre, m_g_pre, v_g_pre), q_norm_g=(q_norm_g, m_q_norm_g, v_q_norm_g),
                 k_norm_g=(k_norm_g, m_k_norm_g, v_k_norm_g), k_k=(k_k, m_k_k, v_k_k), k_a=(k_a, m_k_a, v_k_a),
                 r_k=(r_k, m_r_k, v_r_k), gn_w=(gn_w, m_gn_w, v_gn_w), gn_b=(gn_b, m_gn_b, v_gn_b),
                 g_post=(g_post, m_g_post, v_g_post))
    for name, size in _REP_SIZES:
        adam(name, rep_flat[:, off:off + size], *given[name])
        off += size

    loss = jnp.sum(rep_flat[:, off])
    order = ["w_ada", "b_ada", "g_pre", "w_in", "q_norm_g", "k_norm_g", "shift_taps", "w_up", "w0", "a_up", "a0", "k_k",
             "k_a", "r_k", "gn_w", "gn_b", "w_out", "g_post"]
    return (loss, grad_x.reshape(B, T, D_MODEL), *[res[n][0] for n in order], *[res[n][1] for n in order],
            *[res[n][2] for n in order], *[res[n][3] for n in order])
```

```python
import functools

import jax
import jax.numpy as jnp
from jax import lax
from jax.experimental import pallas as pl
from jax.experimental.pallas import tpu as pltpu

F32 = jnp.float32
MXU_DTYPE = jnp.bfloat16
MESH = pl.DeviceIdType.MESH
NDEV = 8

D_MODEL = 1024
HEAD_DIM = 64
ATT_W = 512
KV_W = 128
RWKV_W = 512
LORA_W = 128
SHIFT_W = 3 * RWKV_W + LORA_W
GRID_W = 64
ROPE_THETA = 10000.0
DECAY_SCALE = 0.6065306597126334
NORM_EPS = 1e-6
GN_EPS = 64e-5
L2_EPS = 1e-12
ATT_SCALE = HEAD_DIM ** -0.5
C_Q, C_K, C_V, C_GA, C_RIN, C_GRW, C_END = 0, 512, 640, 768, 1280, 2944, 3456

ADAM_LR, ADAM_B1, ADAM_B2, ADAM_EPS, ADAM_WD, ADAM_STEP = 0.001, 0.9, 0.999, 1e-08, 0.01, 10

ROW_TILE = 256
W_GRAD_ROWS = 1024
ATT_TILE_FWD = 256
ATT_TILE_BWD = 512
SCAN_CHUNK = 64
SCAN_UNROLL = 16
VMEM_LIMIT = 56 * 1024 * 1024


def _cp(sem=None):
    return pltpu.CompilerParams(dimension_semantics=sem, vmem_limit_bytes=VMEM_LIMIT)


def _dot(a, b, dims=(((1,), (0,)), ((), ()))):
    return lax.dot_general(a.astype(MXU_DTYPE), b.astype(MXU_DTYPE), dims, preferred_element_type=F32)


def _dot_nt(a, b):
    return _dot(a, b, (((1,), (1,)), ((), ())))


def _dot_tn(a, b):
    return _dot(a, b, (((0,), (0,)), ((), ())))


def _seg_dot(xb, bd):
    n = xb.shape[1]
    if n <= 256:
        return jnp.dot(xb, bd[:n, :n], preferred_element_type=F32)
    parts = [jnp.dot(xb[:, c:c + 256], bd, preferred_element_type=F32) for c in range(0, n, 256)]
    return jnp.concatenate(parts, axis=1)


def _segsum_raw(x, bd):
    rows = x.shape[0]
    hi = x.astype(MXU_DTYPE)
    lo = (x - hi.astype(F32)).astype(MXU_DTYPE)
    both = _seg_dot(jnp.concatenate([hi, lo], axis=0), bd)
    return both[:rows] + both[rows:]


@jax.custom_vjp
def _segsum_d(x, bd):
    return _segsum_raw(x, bd)


def _segsum_d_fwd(x, bd):
    return _segsum_raw(x, bd), bd


def _segsum_d_bwd(bd, ct):
    return _segsum_raw(ct, bd), jnp.zeros_like(bd)


_segsum_d.defvjp(_segsum_d_fwd, _segsum_d_bwd)


def _rope_tables(T):
    t = jnp.arange(T, dtype=F32)
    row = jnp.floor(t / GRID_W)
    col = t - row * GRID_W
    n_freq = HEAD_DIM // 4
    inv_freq = ROPE_THETA ** (-jnp.arange(n_freq, dtype=F32) / n_freq)
    d = jnp.arange(HEAD_DIM)
    pos = jnp.where((d < HEAD_DIM // 2)[None, :], row[:, None], col[:, None])
    ang = pos * inv_freq[d % n_freq][None, :]
    sign = jnp.where((d % 32) < 16, -1.0, 1.0).astype(F32)[None, :]
    cos = jnp.cos(ang)
    sin = jnp.sin(ang) * sign
    return jnp.tile(cos, (1, 2)), jnp.tile(sin, (1, 2))


def _rope_raw(x, cos, sin):
    n = x.shape[1]
    lane = lax.broadcasted_iota(jnp.int32, (1, n), 1)
    first = (lane % 32) < 16
    partner = jnp.where(first, pltpu.roll(x, n - 16, 1), pltpu.roll(x, 16, 1))
    return x * cos + partner * sin


@jax.custom_vjp
def _rope_d(x, cos, sin):
    return _rope_raw(x, cos, sin)


def _rope_d_fwd(x, cos, sin):
    return _rope_raw(x, cos, sin), (cos, sin)


def _rope_d_bwd(res, ct):
    cos, sin = res
    return _rope_raw(ct, cos, -sin), jnp.zeros_like(cos), jnp.zeros_like(sin)


_rope_d.defvjp(_rope_d_fwd, _rope_d_bwd)


def _rms(x, g):
    return x * lax.rsqrt(jnp.mean(x * x, axis=-1, keepdims=True) + NORM_EPS) * g


def _pre_fn(x, shift, scale, g_pre):
    return _rms(x, g_pre) * (1.0 + scale) + shift


def _qk_fn(q, g, cos, sin, bd, scale, diff):
    segsum = _segsum_d if diff else _segsum_raw
    rope = _rope_d if diff else _rope_raw
    qn = q * lax.rsqrt(segsum(q * q, bd) * (1.0 / HEAD_DIM) + NORM_EPS) * g
    return rope(qn, cos, sin) * scale


def _silu(x):
    return x * jax.nn.sigmoid(x)


def _rwkv_pw(k, pw0, pw1, pa0, pa1, w0, a0, k_k, k_a, bd, diff):
    segsum = _segsum_d if diff else _segsum_raw
    kk = k * k_k
    kk = kk * lax.rsqrt(segsum(kk * kk, bd) + L2_EPS)
    ws, kts, akks = [], [], []
    for z, (pw, pa) in enumerate(((pw0, pa0), (pw1, pa1))):
        w = jnp.exp(-DECAY_SCALE * jax.nn.sigmoid(w0[z:z + 1, :] + pw))
        a = jax.nn.sigmoid(a0[z:z + 1, :] + pa)
        ws.append(w)
        kts.append(k * (1.0 + (a - 1.0) * k_a))
        akks.append(a * kk)
    return ws[0], ws[1], kts[0], kts[1], akks[0], akks[1], kk


def _mix_fn(y_att, g_att, ys, r, v, kts, g_rw, gn_w, gn_b, r_k, bd, diff):
    segsum = _segsum_d if diff else _segsum_raw
    mu = segsum(ys, bd) * (1.0 / HEAD_DIM)
    d = ys - mu
    var = segsum(d * d, bd) * (1.0 / HEAD_DIM)
    yn = d * lax.rsqrt(var + GN_EPS) * gn_w + gn_b
    bonus = segsum(r * kts * r_k, bd) * v
    return y_att * _silu(g_att), (yn + bonus) * _silu(g_rw)


def _loss_fn(out, x, tgt, gate, g_post):
    e = x + gate * _rms(out, g_post) - tgt
    s = jnp.sum(e * e, axis=1, keepdims=True)
    return jnp.sum(s, axis=0, keepdims=True) * (0.5 / D_MODEL)


def _exchange(arrays, modes, name):
    n = len(arrays)
    out_shape = tuple(
        jax.ShapeDtypeStruct(((NDEV,) + tuple(a.shape)) if mode == "all" else tuple(a.shape), a.dtype)
        for a, mode in zip(arrays, modes))
    chips = (4, 2, 6)

    def body(*refs):
        ins, outs = refs[:n], refs[n:2 * n]
        send_sems, recv_sems, local_sems = refs[2 * n:]
        ix, iy, ic = lax.axis_index("x"), lax.axis_index("y"), lax.axis_index("c")
        me = 4 * ix + 2 * iy + ic

        def peer(m):
            px = 1 - ix if (m >> 2) & 1 else ix
            py = 1 - iy if (m >> 1) & 1 else iy
            pc = 1 - ic if m & 1 else ic
            return (px, py, pc), 4 * px + 2 * py + pc

        def copy(k, j, src_ref, slot, to):
            return pltpu.make_async_remote_copy(src_ref=src_ref, dst_ref=outs[k].at[slot], send_sem=send_sems.at[k, j],
                                                recv_sem=recv_sems.at[k, j], device_id=to, device_id_type=MESH)

        local, sends, arrivals, forwards = [], [], [], []
        for k in range(n):
            if modes[k] == "scatter":
                local.append(pltpu.make_async_copy(ins[k].at[me], outs[k].at[me], local_sems.at[k]))
                for m in range(1, NDEV):
                    to, p = peer(m)
                    sends.append(copy(k, m - 1, ins[k].at[p], me, to))
                    arrivals.append(copy(k, m - 1, ins[k].at[p], p, to))
            elif modes[k] == "chips":
                mine = me // 2
                local.append(pltpu.make_async_copy(ins[k].at[mine], outs[k].at[mine], local_sems.at[k]))
                for j, m in enumerate(chips):
                    to, p = peer(m)
                    sends.append(copy(k, j, ins[k].at[p // 2], mine, to))
                    arrivals.append(copy(k, j, ins[k].at[p // 2], p // 2, to))
            else:
                local.append(pltpu.make_async_copy(ins[k], outs[k].at[me], local_sems.at[k]))
                sib, sib_slot = peer(1)
                sends.append(copy(k, 0, ins[k], me, sib))
                for j, m in enumerate(chips):
                    to, p = peer(m)
                    sends.append(copy(k, 1 + j, ins[k], me, to))
                    forwards.append((copy(k, 1 + j, ins[k], p, to), copy(k, 4 + j, outs[k].at[p], p, sib)))
                    arrivals.append(copy(k, 4 + j, ins[k], peer(m ^ 1)[1], sib))
                arrivals.append(copy(k, 0, ins[k], sib_slot, sib))
        for cp in local + sends:
            cp.start()
        for arrived, onward in forwards:
            arrived.wait_recv()
            onward.start()
        for cp in arrivals:
            cp.wait_recv()
        for cp in sends + [onward for _, onward in forwards]:
            cp.wait_send()
        for cp in local:
            cp.wait()

    any_spec = pl.BlockSpec(memory_space=pl.ANY)
    return pl.pallas_call(
        body, name=name, out_shape=out_shape,
        in_specs=[any_spec] * n, out_specs=tuple([any_spec] * n),
        scratch_shapes=[pltpu.SemaphoreType.DMA((n, NDEV - 1)), pltpu.SemaphoreType.DMA((n, NDEV - 1)),
                        pltpu.SemaphoreType.DMA((n,))],
    )(*arrays)


def _pair_sum_call(mine, send, name):
    n = len(mine)

    def body(*refs):
        mine_r, send_r, out_r, land_r = (refs[j * n:(j + 1) * n] for j in range(4))
        send_sems, recv_sems = refs[4 * n:]
        sibling = (lax.axis_index("x"), lax.axis_index("y"), 1 - lax.axis_index("c"))
        swaps = [pltpu.make_async_remote_copy(src_ref=send_r[k], dst_ref=land_r[k], send_sem=send_sems.at[k],
                                              recv_sem=recv_sems.at[k], device_id=sibling, device_id_type=MESH)
                 for k in range(n)]
        for cp in swaps:
            cp.start()
        for k, cp in enumerate(swaps):
            cp.wait()
            out_r[k][...] = (mine_r[k][...].astype(F32) + land_r[k][...].astype(F32)).astype(out_r[k].dtype)

    return pl.pallas_call(
        body, name=name, out_shape=tuple(jax.ShapeDtypeStruct(a.shape, a.dtype) for a in mine),
        scratch_shapes=[pltpu.VMEM(a.shape, a.dtype) for a in mine] + [pltpu.SemaphoreType.DMA((n,)),
                                                                         pltpu.SemaphoreType.DMA((n,))],
        compiler_params=pltpu.CompilerParams(vmem_limit_bytes=VMEM_LIMIT),
    )(*mine, *send)


def _mod_call(c_all, w_ada, b_cols):
    def body(c_ref, w_ref, b_ref, o_ref):
        o_ref[...] = _dot(_silu(c_ref[...]), w_ref[...]) + b_ref[...]

    return pl.pallas_call(body, name="mod_fwd",
                          out_shape=jax.ShapeDtypeStruct((c_all.shape[0], w_ada.shape[1]), F32))(c_all, w_ada, b_cols)


def _wada_grad_call(c_all, dmod_cols):
    def body(c_ref, d_ref, o_ref):
        o_ref[...] = _dot_tn(_silu(c_ref[...]), d_ref[...])

    return pl.pallas_call(body, name="w_ada_grad",
                          out_shape=jax.ShapeDtypeStruct((c_all.shape[1], dmod_cols.shape[1]), F32))(c_all, dmod_cols)


def _full(shape):
    nd = len(shape)
    return pl.BlockSpec(shape, lambda *_: (0,) * nd)


def _in_proj_call(x2, shift, scale, g_pre, w_in, qg, kg, cos, sin, bd, T):
    R = x2.shape[0]
    TT = min(ROW_TILE, T)
    tpe = T // TT

    def body(x_ref, sh_ref, sc_ref, gp_ref, w_ref, qg_ref, kg_ref, cos_ref, sin_ref, bd_ref,
             hb_ref, qr_ref, kpad_ref, vpad_ref, qraw_ref, kraw_ref, gatt_ref, rin_ref, grw_ref):
        h = _pre_fn(x_ref[...], sh_ref[0], sc_ref[0], gp_ref[...])
        hb = h.astype(MXU_DTYPE)
        hb_ref[...] = hb

        def proj(c0, c1):
            return _dot_nt(hb, w_ref[c0:c1, :])

        q = proj(C_Q, C_K)
        k = proj(C_K, C_V)
        v = proj(C_V, C_GA)
        gatt_ref[...] = proj(C_GA, C_RIN)
        rin_ref[...] = proj(C_RIN, C_GRW)
        grw_ref[...] = proj(C_GRW, C_END)
        qraw_ref[...] = q
        kraw_ref[...] = k
        cos, sin, bd = cos_ref[...], sin_ref[...], bd_ref[...]
        qr = _qk_fn(q, qg_ref[...], jnp.tile(cos, (1, 4)), jnp.tile(sin, (1, 4)), bd, ATT_SCALE, False)
        qr_ref[...] = qr.astype(MXU_DTYPE)
        kr = _qk_fn(k, kg_ref[...], cos, sin, bd, 1.0, False)
        left = lax.broadcasted_iota(jnp.int32, (1, KV_W), 1) < HEAD_DIM
        for ref, val in ((kpad_ref, kr), (vpad_ref, v)):
            h0l = jnp.where(left, val, 0.0)
            h1r = jnp.where(left, 0.0, val)
            ref[0] = h0l.astype(MXU_DTYPE)
            ref[1] = pltpu.roll(h0l, HEAD_DIM, 1).astype(MXU_DTYPE)
            ref[2] = pltpu.roll(h1r, HEAD_DIM, 1).astype(MXU_DTYPE)
            ref[3] = h1r.astype(MXU_DTYPE)

    row = lambda w: pl.BlockSpec((TT, w), lambda i: (i, 0))
    per_ex = pl.BlockSpec((1, 1, D_MODEL), lambda i: (i // tpe, 0, 0))
    tab = pl.BlockSpec((TT, KV_W), lambda i: (i % tpe, 0))
    pad = pl.BlockSpec((4, TT, KV_W), lambda i: (0, i, 0))
    sds = jax.ShapeDtypeStruct
    return pl.pallas_call(
        body, name="in_proj", grid=(R // TT,),
        in_specs=[row(D_MODEL), per_ex, per_ex, _full((1, D_MODEL)), _full(w_in.shape), _full((1, ATT_W)),
                  _full((1, KV_W)), tab, tab, _full((256, 256))],
        out_specs=(row(D_MODEL), row(ATT_W), pad, pad, row(ATT_W), row(KV_W), row(ATT_W), row(SHIFT_W), row(RWKV_W)),
        out_shape=(sds((R, D_MODEL), MXU_DTYPE), sds((R, ATT_W), MXU_DTYPE), sds((4, R, KV_W), MXU_DTYPE),
                   sds((4, R, KV_W), MXU_DTYPE), sds((R, ATT_W), F32), sds((R, KV_W), F32), sds((R, ATT_W), F32),
                   sds((R, SHIFT_W), F32), sds((R, RWKV_W), F32)),
        compiler_params=_cp(("arbitrary",)),
    )(x2, shift, scale, g_pre, w_in, qg, kg, cos, sin, bd)


def _softmax_parts(s):
    e = jnp.exp(s - jnp.max(s, axis=1, keepdims=True))
    return e, 1.0 / jnp.sum(e, axis=1, keepdims=True)


def _att_specs(T, TQ):
    nq = T // TQ
    qspec = pl.BlockSpec((TQ, KV_W), lambda b, p, i: (b * nq + i, p))
    side = lambda s: pl.BlockSpec((None, T, KV_W), lambda b, p, i: (2 * (p // 2) + s, b, 0))
    return nq, qspec, side


def _att_fwd_call(qr, kpad, vpad, B, T):
    TQ = min(ATT_TILE_FWD, T)
    nq, qspec, side = _att_specs(T, TQ)

    def body(q_ref, kl_ref, kr_ref, vl_ref, vr_ref, o_ref):
        q = q_ref[...]
        ea, inv_a = _softmax_parts(_dot_nt(q, kl_ref[...]))
        eb, inv_b = _softmax_parts(_dot_nt(q, kr_ref[...]))
        o_ref[...] = _dot(ea, vl_ref[...]) * inv_a + _dot(eb, vr_ref[...]) * inv_b

    return pl.pallas_call(
        body, name="att_fwd", grid=(B, 4, nq),
        in_specs=[qspec, side(0), side(1), side(0), side(1)], out_specs=qspec,
        out_shape=jax.ShapeDtypeStruct((B * T, ATT_W), F32),
        compiler_params=_cp(("arbitrary",) * 3),
    )(qr, kpad, kpad, vpad, vpad)


def _att_bwd_call(qr, kpad, vpad, d_o, B, T):
    TQ = min(ATT_TILE_BWD, T)
    nq, qspec, side = _att_specs(T, TQ)

    def body(q_ref, kl_ref, kr_ref, vl_ref, vr_ref, do_ref, dq_ref, dk_ref, dv_ref):
        i = pl.program_id(2)
        q, do = q_ref[...], do_ref[...]
        left = lax.broadcasted_iota(jnp.int32, (1, KV_W), 1) < HEAD_DIM
        dq = jnp.zeros((TQ, KV_W), F32)
        dk = jnp.zeros((T, KV_W), F32)
        dv = jnp.zeros((T, KV_W), F32)
        for k_ref, v_ref, mask in ((kl_ref, vl_ref, left), (kr_ref, vr_ref, jnp.logical_not(left))):
            kk, vv = k_ref[...], v_ref[...]
            e, inv = _softmax_parts(_dot_nt(q, kk))
            dp = _dot_nt(do, vv)
            ds = e * (dp - inv * jnp.sum(e * dp, axis=1, keepdims=True))
            dq = dq + _dot(ds, kk) * inv
            dk = dk + _dot_tn(ds, jnp.where(mask, q * inv, 0.0))
            dv = dv + _dot_tn(e, jnp.where(mask, do * inv, 0.0))
        dq_ref[...] = dq

        @pl.when(i == 0)
        def _():
            dk_ref[...] = dk
            dv_ref[...] = dv

        @pl.when(i > 0)
        def _():
            dk_ref[...] += dk
            dv_ref[...] += dv

    acc = pl.BlockSpec((None, T, KV_W), lambda b, p, i: (p, b, 0))
    sds = jax.ShapeDtypeStruct
    return pl.pallas_call(
        body, name="att_bwd", grid=(B, 4, nq),
        in_specs=[qspec, side(0), side(1), side(0), side(1), qspec], out_specs=(qspec, acc, acc),
        out_shape=(sds((B * T, ATT_W), F32), sds((4, B * T, KV_W), F32), sds((4, B * T, KV_W), F32)),
        compiler_params=_cp(("arbitrary",) * 3),
    )(qr, kpad, kpad, vpad, vpad, d_o)


def _shift_specs(R, T, TT, width):
    tpe = T // TT
    nb8 = R // 8
    cur = pl.BlockSpec((TT, width), lambda i: (i, 0))
    prev = pl.BlockSpec((8, width), lambda i: (jnp.maximum(i * (TT // 8) - 1, 0), 0))
    nxt = pl.BlockSpec((8, width), lambda i: (jnp.minimum((i + 1) * (TT // 8), nb8 - 1), 0))
    return tpe, cur, prev, nxt


def _neighbours(cur, prev8, next8, i, tpe, TT):
    rows = lax.broadcasted_iota(jnp.int32, (TT, 1), 0)
    first = jnp.where(i % tpe == 0, 0.0, 1.0)
    last = jnp.where(i % tpe == tpe - 1, 0.0, 1.0)
    before = jnp.where(rows == 0, prev8[7:8, :] * first, pltpu.roll(cur, 1, 0))
    after = jnp.where(rows == TT - 1, next8[0:1, :] * last, pltpu.roll(cur, TT - 1, 0))
    return before, after


def _shift_fwd_call(x, taps, T):
    R, width = x.shape
    TT = min(ROW_TILE, T)
    tpe, cur, prev, nxt = _shift_specs(R, T, TT, width)

    def body(x_ref, p_ref, n_ref, t_ref, o_ref):
        xc = x_ref[...]
        before, after = _neighbours(xc, p_ref[...], n_ref[...], pl.program_id(0), tpe, TT)
        o_ref[...] = t_ref[0:1, :] * before + t_ref[1:2, :] * xc + t_ref[2:3, :] * after

    return pl.pallas_call(
        body, name="shift_fwd", grid=(R // TT,), in_specs=[cur, prev, nxt, _full(taps.shape)], out_specs=cur,
        out_shape=jax.ShapeDtypeStruct((R, width), F32), compiler_params=_cp(("arbitrary",)),
    )(x, x, x, taps)


def _shift_bwd_call(x, d, taps, T):
    R, width = x.shape
    TT = min(ROW_TILE, T)
    tpe, cur, prev, nxt = _shift_specs(R, T, TT, width)

    def body(x_ref, xp_ref, xn_ref, d_ref, dp_ref, dn_ref, t_ref, dx_ref, dt_ref):
        i = pl.program_id(0)
        xc, dc = x_ref[...], d_ref[...]
        d_before, d_after = _neighbours(dc, dp_ref[...], dn_ref[...], i, tpe, TT)
        dx_ref[...] = t_ref[2:3, :] * d_before + t_ref[1:2, :] * dc + t_ref[0:1, :] * d_after
        x_before, x_after = _neighbours(xc, xp_ref[...], xn_ref[...], i, tpe, TT)
        @pl.when(i == 0)
        def _():
            dt_ref[...] = jnp.zeros_like(dt_ref)

        for j, xs in enumerate((x_before, xc, x_after)):
            dt_ref[j:j + 1, :] += jnp.sum(dc * xs, axis=0, keepdims=True)

    return pl.pallas_call(
        body, name="shift_bwd", grid=(R // TT,),
        in_specs=[cur, prev, nxt, cur, prev, nxt, _full(taps.shape)], out_specs=(cur, _full((8, width))),
        out_shape=(jax.ShapeDtypeStruct((R, width), F32), jax.ShapeDtypeStruct((8, width), F32)),
        compiler_params=_cp(("arbitrary",)),
    )(x, x, x, d, d, d, taps)


def _lora_in(wa):
    lane = lax.broadcasted_iota(jnp.int32, (1, LORA_W), 1)
    return jnp.where(lane < LORA_W // 2, jnp.tanh(wa), wa)


def _rwkv_prep_call(shifted, wup, aup, w0, a0, k_k, k_a, bd, T):
    R = shifted.shape[0]
    TT = min(ROW_TILE, T)

    def body(k_ref, wa_ref, wup_ref, aup_ref, w0_ref, a0_ref, kk_ref, ka_ref, bd_ref, w_o, kt_o, akk_o, kk_o):
        twa = _lora_in(wa_ref[...])
        pre = [_dot(twa, m_ref[z]) for m_ref in (wup_ref, aup_ref) for z in range(2)]
        outs = _rwkv_pw(k_ref[...], pre[0], pre[1], pre[2], pre[3], w0_ref[...], a0_ref[...], kk_ref[...],
                        ka_ref[...], bd_ref[...], False)
        w_o[0], w_o[1], kt_o[0], kt_o[1], akk_o[0], akk_o[1] = outs[:6]
        kk_o[...] = outs[6]

    col = lambda c, w: pl.BlockSpec((TT, w), lambda i: (i, c))
    two = pl.BlockSpec((2, TT, RWKV_W), lambda i: (0, i, 0))
    sds = jax.ShapeDtypeStruct
    return pl.pallas_call(
        body, name="rwkv_prep", grid=(R // TT,),
        in_specs=[col(1, RWKV_W), col(3 * RWKV_W // LORA_W, LORA_W), _full(wup.shape), _full(aup.shape),
                  _full((2, RWKV_W)), _full((2, RWKV_W)), _full((1, RWKV_W)), _full((1, RWKV_W)), _full((256, 256))],
        out_specs=(two, two, two, col(0, RWKV_W)),
        out_shape=(sds((2, R, RWKV_W), F32),) * 3 + (sds((R, RWKV_W), F32),),
        compiler_params=_cp(("arbitrary",)),
    )(shifted, shifted, wup, aup, w0, a0, k_k, k_a, bd)


def _rwkv_prep_bwd_call(shifted, cts, wup, aup, w0, a0, k_k, k_a, bd, T):
    R = shifted.shape[0]
    TT = min(ROW_TILE, T)

    def body(k_ref, wa_ref, dw0, dkt0, dakk0, dkk0, dr0, dv0, dw1, dkt1, dakk1, dkk1, dr1, dv1, dr2_ref, dv2_ref, dkts_ref,
             wup_ref, aup_ref, w0_ref, a0_ref, kk_ref, ka_ref, bd_ref,
             dsh_ref, gwup_ref, gaup_ref, gw0_ref, ga0_ref, gkk_ref, gka_ref):
        dw_ref, dkt_ref, dakk_ref, dkk_ref, dr_ref, dv_ref = ((dw0, dw1), (dkt0, dkt1), (dakk0, dakk1), (dkk0, dkk1),
                                                              (dr0, dr1), (dv0, dv1))
        i = pl.program_id(0)
        wa = wa_ref[...]
        twa = _lora_in(wa)
        pre = [_dot(twa, m_ref[z]) for m_ref in (wup_ref, aup_ref) for z in range(2)]
        fn = functools.partial(_rwkv_pw, bd=bd_ref[...], diff=True)
        _, vjp = jax.vjp(fn, k_ref[...], pre[0], pre[1], pre[2], pre[3], w0_ref[...], a0_ref[...], kk_ref[...],
                         ka_ref[...])
        dkts = dkts_ref[...]
        dk, dpw0, dpw1, dpa0, dpa1, gw0, ga0, gkk, gka = vjp(
            (dw_ref[0][...], dw_ref[1][...], dkt_ref[0][...] + dkts, dkt_ref[1][...] + dkts, dakk_ref[0][...],
             dakk_ref[1][...], dkk_ref[0][...] + dkk_ref[1][...]))
        dtwa = (_dot_nt(dpw0, wup_ref[0]) + _dot_nt(dpw1, wup_ref[1]) + _dot_nt(dpa0, aup_ref[0])
                + _dot_nt(dpa1, aup_ref[1]))
        lane = lax.broadcasted_iota(jnp.int32, (1, LORA_W), 1)
        dsh_ref[:, 0:RWKV_W] = dr_ref[0][...] + dr_ref[1][...] + dr2_ref[...]
        dsh_ref[:, RWKV_W:2 * RWKV_W] = dk
        dsh_ref[:, 2 * RWKV_W:3 * RWKV_W] = dv_ref[0][...] + dv_ref[1][...] + dv2_ref[...]
        dsh_ref[:, 3 * RWKV_W:] = jnp.where(lane < LORA_W // 2, dtwa * (1.0 - twa * twa), dtwa)
        acc = ((gwup_ref.at[0], _dot_tn(twa, dpw0)), (gwup_ref.at[1], _dot_tn(twa, dpw1)),
               (gaup_ref.at[0], _dot_tn(twa, dpa0)), (gaup_ref.at[1], _dot_tn(twa, dpa1)),
               (gw0_ref, gw0), (ga0_ref, ga0), (gkk_ref, gkk), (gka_ref, gka))

        @pl.when(i == 0)
        def _():
            for ref, val in acc:
                ref[...] = val

        @pl.when(i > 0)
        def _():
            for ref, val in acc:
                ref[...] += val

    col = lambda c, w: pl.BlockSpec((TT, w), lambda i: (i, c))
    one = col(0, RWKV_W)
    sds = jax.ShapeDtypeStruct
    return pl.pallas_call(
        body, name="rwkv_prep_bwd", grid=(R // TT,),
        in_specs=[col(1, RWKV_W), col(3 * RWKV_W // LORA_W, LORA_W)] + [one] * 15 + [
                  _full(wup.shape), _full(aup.shape), _full((2, RWKV_W)), _full((2, RWKV_W)), _full((1, RWKV_W)),
                  _full((1, RWKV_W)), _full((256, 256))],
        out_specs=(pl.BlockSpec((TT, SHIFT_W), lambda i: (i, 0)), _full(wup.shape), _full(aup.shape),
                   _full((2, RWKV_W)), _full((2, RWKV_W)), _full((1, RWKV_W)), _full((1, RWKV_W))),
        out_shape=(sds((R, SHIFT_W), F32), sds(wup.shape, F32), sds(aup.shape, F32), sds((2, RWKV_W), F32),
                   sds((2, RWKV_W), F32), sds((1, RWKV_W), F32), sds((1, RWKV_W), F32)),
        compiler_params=_cp(("arbitrary",)),
    )(shifted, shifted, *cts, wup, aup, w0, a0, k_k, k_a, bd)


def _col_lhs(row, eye_b):
    return eye_b * row.astype(MXU_DTYPE)


def _colsum(x):
    return jnp.sum(x, axis=0, keepdims=True)


def _stacked_segsum(tiles, bd):
    res = _seg_dot(jnp.concatenate(tiles, axis=0), bd)
    return [res[j * HEAD_DIM:(j + 1) * HEAD_DIM] for j in range(len(tiles))]


def _scan_specs(B, T, C, nC):
    def blk(z, col, rev):
        idx = (lambda g: (z, 0, nC - 1 - g, col)) if rev else (lambda g: (z, 0, g, col))
        return pl.BlockSpec((None, B, C, RWKV_W), idx)

    def blk3(col, rev):
        idx = (lambda g: (0, nC - 1 - g, col)) if rev else (lambda g: (0, g, col))
        return pl.BlockSpec((B, C, RWKV_W), idx)

    return blk, blk3


def _scan_fwd_call(w, kt, akk, kk, shifted, eye_b, eye_f, bd, B, T):
    C = min(SCAN_CHUNK, T)
    nC = T // C
    blk, blk3 = _scan_specs(B, T, C, nC)

    def body(w0, kt0, akk0, kk0, v0, r0, w1, kt1, akk1, kk1, v1, r1, eb_ref, ef_ref, bd_ref, y0, y1, st, S):
        @pl.when(pl.program_id(0) == 0)
        def _():
            S[...] = jnp.zeros_like(S)

        st[0] = S[...].astype(MXU_DTYPE)
        dirs = ((w0, kt0, akk0, kk0, v0, r0, y0), (w1, kt1, akk1, kk1, v1, r1, y1))

        def step(s, carry):
            for z in range(2):
                row = s if z == 0 else C - 1 - s
                prev = jnp.maximum(s - 1, 0) if z == 0 else jnp.minimum(C - s, C - 1)
                wr, ktr, akkr, kkr, vr, rr, yr = dirs[z]
                tiles = []
                for b in range(B):
                    Sb = st[s, z * B + b]
                    tiles += [Sb * kkr[b, pl.ds(row, 1), :].astype(MXU_DTYPE),
                              _col_lhs(vr[b, pl.ds(row, 1), :], eb_ref[...]),
                              Sb * rr[b, pl.ds(prev, 1), :].astype(MXU_DTYPE)]
                res = _stacked_segsum(tiles, bd_ref[...])
                for b in range(B):
                    c = z * B + b
                    sab, vb, yb = res[3 * b:3 * b + 3]
                    ld = lambda ref: ref[b, pl.ds(row, 1), :]
                    Sn = S[c] * ld(wr) - sab * ld(akkr) + vb * ld(ktr)
                    S[c] = Sn
                    st[s + 1, c] = Sn.astype(MXU_DTYPE)
                    yr[b, pl.ds(prev, 1), :] = _colsum(ef_ref[...] * yb)
            return carry

        lax.fori_loop(0, C, step, 0, unroll=SCAN_UNROLL)
        for z in range(2):
            last = C - 1 if z == 0 else 0
            rr, yr = dirs[z][5], dirs[z][6]
            res = _stacked_segsum([st[C, z * B + b] * rr[b, last:last + 1, :].astype(MXU_DTYPE) for b in range(B)],
                                  bd_ref[...])
            for b in range(B):
                yr[b, last:last + 1, :] = _colsum(ef_ref[...] * res[b])

    ins, specs = [], []
    for z, rev in ((0, False), (1, True)):
        ins += [w, kt, akk, kk, shifted, shifted]
        specs += [blk(z, 0, rev), blk(z, 0, rev), blk(z, 0, rev), blk3(0, rev), blk3(2, rev), blk3(0, rev)]
    sds = jax.ShapeDtypeStruct
    return pl.pallas_call(
        body, name="scan_fwd", grid=(nC,),
        in_specs=specs + [_full((HEAD_DIM, RWKV_W)), _full((HEAD_DIM, RWKV_W)), _full((256, 256))],
        out_specs=(blk3(0, False), blk3(0, True),
                   pl.BlockSpec((None, C + 1, 2 * B, HEAD_DIM, RWKV_W), lambda g: (g, 0, 0, 0, 0))),
        out_shape=(sds((B, T, RWKV_W), F32), sds((B, T, RWKV_W), F32),
                   sds((nC, C + 1, 2 * B, HEAD_DIM, RWKV_W), MXU_DTYPE)),
        scratch_shapes=[pltpu.VMEM((2 * B, HEAD_DIM, RWKV_W), F32)],
        compiler_params=_cp(("arbitrary",)),
    )(*ins, eye_b, eye_f, bd)


def _scan_bwd_call(w, kt, akk, kk, shifted, v_heads, dys, st, eye_b, eye_f, bd, B, T):
    C = min(SCAN_CHUNK, T)
    nC = T // C
    blk, blk3 = _scan_specs(B, T, C, nC)
    nin = 7

    def body(*refs):
        d0, d1 = refs[:nin], refs[nin:2 * nin]
        st_ref, eb_ref, ef_ref, sel_ref, hm_ref, bd_ref = refs[2 * nin:2 * nin + 6]
        o0, o1 = refs[2 * nin + 6:2 * nin + 12], refs[2 * nin + 12:2 * nin + 18]
        COL, DYC, G = refs[2 * nin + 18:]

        @pl.when(pl.program_id(0) == 0)
        def _():
            G[...] = jnp.zeros_like(G)

        dirs = (d0 + (o0,), d1 + (o1,))

        def column_operands(s, z):
            row = s if z == 0 else C - 1 - s
            _, _, _, kkr, _, _, dyr, _ = dirs[z]
            tiles = []
            for b in range(B):
                tiles += [st_ref[s, z * B + b] * kkr[b, pl.ds(row, 1), :].astype(MXU_DTYPE),
                          _col_lhs(dyr[b, pl.ds(row, 1), :], eb_ref[...])]
            return tiles

        def keep_columns(res, z):
            for b in range(B):
                for k in range(2):
                    COL[k, z * B + b] = res[2 * b + k].astype(MXU_DTYPE)
                DYC[z * B + b] = res[2 * b + 1]

        for z in range(2):
            keep_columns(_stacked_segsum(column_operands(C - 1, z), bd_ref[...]), z)

        def bwd(it, carry):
            s = C - 1 - it
            for z in range(2):
                row = s if z == 0 else C - 1 - s
                wr, ktr, akkr, kkr, vr, rr, dyr, (dw_o, dkt_o, dakk_o, dkk_o, dr_o, dv_o) = dirs[z]
                tiles, Gcs = [], []
                for b in range(B):
                    c = z * B + b
                    Gc = G[c] + DYC[c] * rr[b, pl.ds(row, 1), :]
                    Gb = Gc.astype(MXU_DTYPE)
                    Gcs.append((Gc, Gb))
                    tiles += [Gb * akkr[b, pl.ds(row, 1), :].astype(MXU_DTYPE),
                              Gb * ktr[b, pl.ds(row, 1), :].astype(MXU_DTYPE)]
                res = _stacked_segsum(tiles + column_operands(jnp.maximum(s - 1, 0), z), bd_ref[...])
                for b in range(B):
                    c = z * B + b
                    Gc, Gb = Gcs[b]
                    gab, dvb = res[2 * b], res[2 * b + 1]
                    ld = lambda ref: ref[b, pl.ds(row, 1), :]
                    G[c] = Gc * ld(wr) - gab * ld(kkr)
                    Sb = st_ref[s, c]
                    prods = jnp.concatenate([Gb, st_ref[s + 1, c] * COL[1, c], Gb * Sb, Gb * COL[0, c],
                                             gab.astype(MXU_DTYPE) * Sb], axis=0)
                    v_rows = jnp.concatenate([vr[b, pl.ds(row, 1)][0], jnp.zeros((8, 3 * HEAD_DIM), F32)], axis=1)
                    lhs = jnp.concatenate([sel_ref[...], v_rows], axis=0).astype(MXU_DTYPE)
                    sums = jnp.dot(lhs, prods, preferred_element_type=F32)
                    for k, (ref, sign) in enumerate(((dr_o, 1.0), (dw_o, 1.0), (dakk_o, -1.0), (dkk_o, -1.0))):
                        ref[b, pl.ds(row, 1), :] = sign * sums[k:k + 1, :]
                    dkt_o[b, pl.ds(row, 1), :] = _colsum(sums[8:16] * hm_ref[...])
                    dv_o[b, pl.ds(row, 1), :] = _colsum(ef_ref[...] * dvb)
                keep_columns(res[2 * B:], z)
            return carry

        lax.fori_loop(0, C, bwd, 0, unroll=SCAN_UNROLL)

    ins, specs = [], []
    for z, rev in ((0, True), (1, False)):
        heads = pl.BlockSpec((B, C) + v_heads.shape[2:], (lambda g: (0, nC - 1 - g, 0, 0)) if rev else (lambda g: (0, g, 0, 0)))
        ins += [w, kt, akk, kk, v_heads, shifted, dys]
        specs += [blk(z, 0, rev), blk(z, 0, rev), blk(z, 0, rev), blk3(0, rev), heads, blk3(0, rev), blk3(0, rev)]
    sel = (jnp.arange(8)[:, None] + 1 == (jnp.arange(5 * HEAD_DIM) // HEAD_DIM)[None, :]).astype(F32)
    head_rows = (jnp.arange(RWKV_W // HEAD_DIM)[:, None] == (jnp.arange(RWKV_W) // HEAD_DIM)[None, :]).astype(F32)
    ins += [st, eye_b, eye_f, sel, head_rows, bd]
    specs += [pl.BlockSpec((None, C + 1, 2 * B, HEAD_DIM, RWKV_W), lambda g: (nC - 1 - g, 0, 0, 0, 0)),
              _full((HEAD_DIM, RWKV_W)), _full((HEAD_DIM, RWKV_W)), _full(sel.shape), _full(head_rows.shape),
              _full((256, 256))]
    sds = jax.ShapeDtypeStruct
    out_specs = tuple(blk3(0, True) for _ in range(6)) + tuple(blk3(0, False) for _ in range(6))
    res = pl.pallas_call(
        body, name="scan_bwd", grid=(nC,), in_specs=specs, out_specs=out_specs,
        out_shape=tuple(sds((B, T, RWKV_W), F32) for _ in range(12)),
        scratch_shapes=[pltpu.VMEM((2, 2 * B, HEAD_DIM, RWKV_W), MXU_DTYPE), pltpu.VMEM((2 * B, HEAD_DIM, RWKV_W), F32),
                        pltpu.VMEM((2 * B, HEAD_DIM, RWKV_W), F32)],
        compiler_params=_cp(("arbitrary",)),
    )(*ins)
    return list(res)


def _out_head_call(x2, tgt2, gate, y_att, g_att, y0, y1, shifted, kt, g_rw, w_out, g_post, gn_w, gn_b, r_k, bd, T):
    R = x2.shape[0]
    TT = min(ROW_TILE, T)
    tpe = T // TT

    def body(x_ref, t_ref, gate_ref, ya_ref, ga_ref, y0_ref, y1_ref, r_ref, v_ref, kt_ref, grw_ref, w_ref, gp_ref,
             gnw_ref, gnb_ref, rk_ref, bd_ref,
             loss_o, dy_o, dya_o, dga_o, dys_o, dr_o, dv_o, dkts_o, dgrw_o, dgate_o, gw_o, ggp_o, ggnw_o, ggnb_o, grk_o):
        i = pl.program_id(0)
        bd = bd_ref[...]
        mix = functools.partial(_mix_fn, bd=bd, diff=True)
        (ma, mr), mix_vjp = jax.vjp(mix, ya_ref[...], ga_ref[...], y0_ref[...] + y1_ref[...], r_ref[...], v_ref[...],
                                    kt_ref[0] + kt_ref[1], grw_ref[...], gnw_ref[...], gnb_ref[...], rk_ref[...])
        out = _dot(ma, w_ref[0:ATT_W, :]) + _dot(mr, w_ref[ATT_W:, :])
        loss, loss_vjp = jax.vjp(_loss_fn, out, x_ref[...], t_ref[...], gate_ref[0], gp_ref[...])
        d_out, dy, _, dgate, dgp = loss_vjp(jnp.ones((1, 1), F32))
        dy_o[...] = dy
        dma = _dot_nt(d_out, w_ref[0:ATT_W, :])
        dmr = _dot_nt(d_out, w_ref[ATT_W:, :])
        dya_o[...], dga_o[...], dys_o[...], dr_o[...], dv_o[...], dkts_o[...], dgrw_o[...], dgnw, dgnb, drk = \
            mix_vjp((dma, dmr))
        gw = jnp.concatenate([_dot_tn(ma, d_out), _dot_tn(mr, d_out)], axis=0)
        acc = ((loss_o, jnp.broadcast_to(loss, (8, 128))), (gw_o, gw), (ggp_o, dgp), (ggnw_o, dgnw), (ggnb_o, dgnb),
               (grk_o, drk))

        @pl.when(i == 0)
        def _():
            for ref, val in acc:
                ref[...] = val

        @pl.when(i > 0)
        def _():
            for ref, val in acc:
                ref[...] += val

        @pl.when(i % tpe == 0)
        def _():
            dgate_o[0] = dgate

        @pl.when(i % tpe > 0)
        def _():
            dgate_o[0] += dgate

    row = lambda w, c=0: pl.BlockSpec((TT, w), lambda i: (i, c))
    two = pl.BlockSpec((2, TT, RWKV_W), lambda i: (0, i, 0))
    per_ex = pl.BlockSpec((1, 1, D_MODEL), lambda i: (i // tpe, 0, 0))
    sds = jax.ShapeDtypeStruct
    r512 = sds((R, RWKV_W), F32)
    return pl.pallas_call(
        body, name="out_head", grid=(R // TT,),
        in_specs=[row(D_MODEL), row(D_MODEL), per_ex, row(ATT_W), row(ATT_W), row(RWKV_W), row(RWKV_W), row(RWKV_W, 0),
                  row(RWKV_W, 2), two,
                  row(RWKV_W), _full(w_out.shape), _full((1, D_MODEL)), _full((1, RWKV_W)), _full((1, RWKV_W)),
                  _full((1, RWKV_W)), _full((256, 256))],
        out_specs=(_full((8, 128)), row(D_MODEL), row(ATT_W), row(ATT_W), row(RWKV_W), row(RWKV_W), row(RWKV_W),
                   row(RWKV_W), row(RWKV_W), per_ex, _full((D_MODEL, D_MODEL)), _full((1, D_MODEL)), _full((1, RWKV_W)),
                   _full((1, RWKV_W)), _full((1, RWKV_W))),
        out_shape=(sds((8, 128), F32), sds((R, D_MODEL), F32), r512, r512, r512, r512, r512, r512, r512,
                   sds((R // T, 1, D_MODEL), F32), sds((D_MODEL, D_MODEL), F32), sds((1, D_MODEL), F32),
                   sds((1, RWKV_W), F32), sds((1, RWKV_W), F32), sds((1, RWKV_W), F32)),
        compiler_params=_cp(("arbitrary",)),
    )(x2, tgt2, gate, y_att, g_att, y0, y1, shifted, shifted, kt, g_rw, w_out, g_post, gn_w, gn_b, r_k, bd)


def _in_proj_bwd_call(x2, dy, shift, scale, g_pre, w_in, qg, kg, cos, sin, bd, q_raw, k_raw, dqr, dkp, dvp,
                      d_gatt, d_rin, d_grw, T):
    R = x2.shape[0]
    TT = min(ROW_TILE, T)
    tpe = T // TT

    def body(x_ref, dy_ref, sh_ref, sc_ref, gp_ref, w_ref, qg_ref, kg_ref, cos_ref, sin_ref, bd_ref, q_ref, k_ref,
             dqr_ref, dkp_ref, dvp_ref, dga_ref, drin_ref, dgrw_ref,
             dx_o, dproj_o, dsh_o, dsc_o, ggp_o, gqg_o, gkg_o):
        i = pl.program_id(0)
        cos, sin, bd = cos_ref[...], sin_ref[...], bd_ref[...]
        left = lax.broadcasted_iota(jnp.int32, (1, KV_W), 1) < HEAD_DIM

        def kv_grad(ref):
            a = ref[0] + ref[1]
            b = ref[2] + ref[3]
            return jnp.where(left, a + pltpu.roll(a, HEAD_DIM, 1), b + pltpu.roll(b, HEAD_DIM, 1))

        qfn = functools.partial(_qk_fn, cos=jnp.tile(cos, (1, 4)), sin=jnp.tile(sin, (1, 4)), bd=bd, scale=ATT_SCALE,
                                diff=True)
        _, q_vjp = jax.vjp(qfn, q_ref[...], qg_ref[...])
        dq, gqg = q_vjp(dqr_ref[...])
        kfn = functools.partial(_qk_fn, cos=cos, sin=sin, bd=bd, scale=1.0, diff=True)
        _, k_vjp = jax.vjp(kfn, k_ref[...], kg_ref[...])
        dk, gkg = k_vjp(kv_grad(dkp_ref))
        pieces = ((C_Q, C_K, dq), (C_K, C_V, dk), (C_V, C_GA, kv_grad(dvp_ref)), (C_GA, C_RIN, dga_ref[...]),
                  (C_RIN, C_GRW, drin_ref[...]), (C_GRW, C_END, dgrw_ref[...]))
        dh = jnp.zeros((TT, D_MODEL), F32)
        for c0, c1, val in pieces:
            vb = val.astype(MXU_DTYPE)
            dproj_o[:, c0:c1] = vb
            dh = dh + _dot(vb, w_ref[c0:c1, :])
        _, pre_vjp = jax.vjp(_pre_fn, x_ref[...], sh_ref[0], sc_ref[0], gp_ref[...])
        dx, dsh, dsc, ggp = pre_vjp(dh)
        dx_o[...] = dx + dy_ref[...]
        acc = ((ggp_o, ggp), (gqg_o, gqg), (gkg_o, gkg))

        @pl.when(i == 0)
        def _():
            for ref, val in acc:
                ref[...] = val

        @pl.when(i > 0)
        def _():
            for ref, val in acc:
                ref[...] += val

        @pl.when(i % tpe == 0)
        def _():
            dsh_o[0] = dsh
            dsc_o[0] = dsc

        @pl.when(i % tpe > 0)
        def _():
            dsh_o[0] += dsh
            dsc_o[0] += dsc

    row = lambda w: pl.BlockSpec((TT, w), lambda i: (i, 0))
    per_ex = pl.BlockSpec((1, 1, D_MODEL), lambda i: (i // tpe, 0, 0))
    tab = pl.BlockSpec((TT, KV_W), lambda i: (i % tpe, 0))
    pad = pl.BlockSpec((4, TT, KV_W), lambda i: (0, i, 0))
    sds = jax.ShapeDtypeStruct
    nb = R // T
    return pl.pallas_call(
        body, name="in_proj_bwd", grid=(R // TT,),
        in_specs=[row(D_MODEL), row(D_MODEL), per_ex, per_ex, _full((1, D_MODEL)), _full(w_in.shape), _full((1, ATT_W)),
                  _full((1, KV_W)), tab, tab, _full((256, 256)), row(ATT_W), row(KV_W), row(ATT_W), pad, pad,
                  row(ATT_W), row(SHIFT_W), row(RWKV_W)],
        out_specs=(row(D_MODEL), row(C_END), per_ex, per_ex, _full((1, D_MODEL)), _full((1, ATT_W)), _full((1, KV_W))),
        out_shape=(sds((R, D_MODEL), F32), sds((R, C_END), MXU_DTYPE), sds((nb, 1, D_MODEL), F32),
                   sds((nb, 1, D_MODEL), F32), sds((1, D_MODEL), F32), sds((1, ATT_W), F32), sds((1, KV_W), F32)),
        compiler_params=_cp(("arbitrary",)),
    )(x2, dy, shift, scale, g_pre, w_in, qg, kg, cos, sin, bd, q_raw, k_raw, dqr, dkp, dvp, d_gatt, d_rin, d_grw)


def _w_in_grad_call(hb, dproj):
    R = hb.shape[0]
    TT = min(W_GRAD_ROWS, R)
    CB = 1152

    def body(h_ref, d_ref, o_ref):
        g = _dot_tn(h_ref[...], d_ref[...])

        @pl.when(pl.program_id(1) == 0)
        def _():
            o_ref[...] = g

        @pl.when(pl.program_id(1) > 0)
        def _():
            o_ref[...] += g

    return pl.pallas_call(
        body, name="w_in_grad", grid=(C_END // CB, R // TT),
        in_specs=[pl.BlockSpec((TT, D_MODEL), lambda j, i: (i, 0)), pl.BlockSpec((TT, CB), lambda j, i: (i, j))],
        out_specs=pl.BlockSpec((D_MODEL, CB), lambda j, i: (0, j)),
        out_shape=jax.ShapeDtypeStruct((D_MODEL, C_END), F32), compiler_params=_cp(("arbitrary", "arbitrary")),
    )(hb, dproj)


def _adam_call(parts, w, m, v, name, row_tile=None):
    P, M, N = parts.shape
    TM = M if row_tile is None else row_tile

    def body(p_ref, w_ref, m_ref, v_ref, g_o, d_o, m_o, v_o):
        g = p_ref[0].astype(F32)
        for j in range(1, P):
            g = g + p_ref[j].astype(F32)
        m2 = ADAM_B1 * m_ref[...] + (1.0 - ADAM_B1) * g
        v2 = ADAM_B2 * v_ref[...] + (1.0 - ADAM_B2) * jnp.square(g)
        m_hat = m2 / (1.0 - ADAM_B1 ** ADAM_STEP)
        v_hat = v2 / (1.0 - ADAM_B2 ** ADAM_STEP)
        g_o[...] = g
        d_o[...] = -ADAM_LR * (m_hat / (jnp.sqrt(v_hat) + ADAM_EPS) + ADAM_WD * w_ref[...])
        m_o[...] = m2
        v_o[...] = v2

    blk = pl.BlockSpec((TM, N), lambda i: (i, 0))
    return pl.pallas_call(
        body, name=name, grid=(M // TM,),
        in_specs=[pl.BlockSpec((P, TM, N), lambda i: (0, i, 0)), blk, blk, blk], out_specs=(blk,) * 4,
        out_shape=(jax.ShapeDtypeStruct((M, N), F32),) * 4, compiler_params=_cp(("arbitrary",)),
    )(parts, w, m, v)


_SMALL_ROWS = 136


def _pack_small(taps, w_up, w0, a_up, a0):
    flat = jnp.concatenate([taps.reshape(-1), w_up.reshape(-1), w0.reshape(-1), a_up.reshape(-1), a0.reshape(-1)])
    return jnp.pad(flat, (0, _SMALL_ROWS * 128 - flat.shape[0])).reshape(_SMALL_ROWS, 128)


def _unpack_small(packed):
    n = packed.shape[0]
    flat = packed.reshape(n, -1)
    out, o = [], 0
    for shape in ((3, 208), (2, 64, 64), (2, 64), (2, 64, 64), (2, 64)):
        size = 1
        for s in shape:
            size *= s
        out.append(flat[:, o:o + size].reshape((n,) + shape))
        o += size
    return out


def _cols_to_full(blocks):
    nd = blocks.ndim
    moved = jnp.moveaxis(blocks, 0, nd - 2)
    return moved.reshape(moved.shape[:-2] + (moved.shape[-2] * moved.shape[-1],))


def _full_to_cols(full):
    k = full.shape[-1] // NDEV
    return jnp.moveaxis(full.reshape(full.shape[:-1] + (NDEV, k)), -2, 0)


_REP_SIZES = (("g_pre", 1024), ("q_norm_g", 64), ("k_norm_g", 64), ("k_k", 512), ("k_a", 512), ("r_k", 512),
              ("gn_w", 512), ("gn_b", 512), ("g_post", 1024))
_REP_ROWS = 40


def kernel(x, c, w_ada, b_ada, g_pre, w_in, q_norm_g, k_norm_g, shift_taps, w_up, w0, a_up, a0, k_k, k_a, r_k, gn_w, gn_b, w_out, g_post, loss_target, m_w_ada, m_b_ada, m_g_pre, m_w_in, m_q_norm_g, m_k_norm_g, m_shift_taps, m_w_up, m_w0, m_a_up, m_a0, m_k_k, m_k_a, m_r_k, m_gn_w, m_gn_b, m_w_out, m_g_post, v_w_ada, v_b_ada, v_g_pre, v_w_in, v_q_norm_g, v_k_norm_g, v_shift_taps, v_w_up, v_w0, v_a_up, v_a0, v_k_k, v_k_a, v_r_k, v_gn_w, v_gn_b, v_w_out, v_g_post):
    B, T, _ = x.shape
    R = B * T
    me = 4 * lax.axis_index("x") + 2 * lax.axis_index("y") + lax.axis_index("c")
    x2 = x.reshape(R, D_MODEL)
    tgt2 = loss_target.reshape(R, D_MODEL)

    seg = jnp.arange(256) // HEAD_DIM
    bd = (seg[:, None] == seg[None, :]).astype(MXU_DTYPE)
    eye = (jnp.arange(HEAD_DIM)[:, None] == (jnp.arange(RWKV_W) % HEAD_DIM)[None, :])
    eye_b, eye_f = eye.astype(MXU_DTYPE), eye.astype(F32)
    cos, sin = _rope_tables(T)

    c_g, w_in_g, w_out_g, small_g = _exchange(
        [c, w_in[0].T.astype(MXU_DTYPE), w_out[0].astype(MXU_DTYPE),
         _pack_small(shift_taps[0], w_up[0], w0[0], a_up[0], a0[0])], ["all"] * 4, "gather_params")
    c_all = c_g.reshape(NDEV * B, D_MODEL)
    w_in_f = w_in_g.reshape(C_END, D_MODEL)
    w_out_f = w_out_g.reshape(D_MODEL, D_MODEL)
    taps_b, w_up_b, w0_b, a_up_b, a0_b = _unpack_small(small_g)
    taps_f = jnp.pad(_cols_to_full(taps_b), ((0, 5), (0, 0)))
    w_up_f, a_up_f = _cols_to_full(w_up_b), _cols_to_full(a_up_b)
    w0_f, a0_f = _cols_to_full(w0_b), _cols_to_full(a0_b)
    wup_pad = jnp.pad(w_up_f, ((0, 0), (0, 64), (0, 0))).astype(MXU_DTYPE)
    aup_pad = jnp.pad(a_up_f, ((0, 0), (64, 0), (0, 0))).astype(MXU_DTYPE)

    ncol = w_ada.shape[2]
    b_cols = lax.dynamic_slice(b_ada, (0, me * ncol), (1, ncol))
    mod_cols = _mod_call(c_all, w_ada[0].astype(MXU_DTYPE), b_cols)
    (mod_g,) = _exchange([mod_cols], ["all"], "gather_mod")
    mod = lax.dynamic_slice(_cols_to_full(mod_g), (me * B, 0), (B, 3 * D_MODEL))
    shift, scale, gate = [mod[:, j * D_MODEL:(j + 1) * D_MODEL].reshape(B, 1, D_MODEL) for j in range(3)]

    qg = jnp.tile(q_norm_g, (1, ATT_W // HEAD_DIM))
    kg = jnp.tile(k_norm_g, (1, KV_W // HEAD_DIM))
    rk_row = r_k.reshape(1, RWKV_W)

    hb, qr, kpad, vpad, q_raw, k_raw, g_att, rin, g_rw = _in_proj_call(
        x2, shift, scale, g_pre, w_in_f, qg, kg, cos, sin, bd, T)
    y_att = _att_fwd_call(qr, kpad, vpad, B, T)
    shifted = _shift_fwd_call(rin, taps_f, T)
    w_s, kt_s, akk_s, kk_s = _rwkv_prep_call(shifted, wup_pad, aup_pad, w0_f, a0_f, k_k, k_a, bd, T)
    sh3 = shifted.reshape(B, T, SHIFT_W)
    r4 = lambda a: a.reshape(2, B, T, RWKV_W)
    y0, y1, st = _scan_fwd_call(r4(w_s), r4(kt_s), r4(akk_s), kk_s.reshape(B, T, RWKV_W), sh3, eye_b, eye_f, bd, B, T)

    (loss_blk, dy, d_yatt, d_gatt, d_ys, d_r2, d_v2, d_kts, d_grw, d_gate, g_wout, g_gpost, g_gnw, g_gnb,
     g_rk) = _out_head_call(x2, tgt2, gate, y_att, g_att, y0.reshape(R, RWKV_W), y1.reshape(R, RWKV_W), shifted, kt_s,
                            g_rw, w_out_f, g_post, gn_w, gn_b, rk_row, bd, T)
    v_heads = jnp.pad(sh3[:, :, 2 * RWKV_W:3 * RWKV_W].reshape(B, T, RWKV_W // HEAD_DIM, HEAD_DIM),
                      ((0, 0), (0, 0), (0, 0), (0, HEAD_DIM)))
    scan_cts = _scan_bwd_call(r4(w_s), r4(kt_s), r4(akk_s), kk_s.reshape(B, T, RWKV_W), sh3, v_heads,
                              d_ys.reshape(B, T, RWKV_W), st, eye_b, eye_f, bd, B, T)
    scan_cts = [a.reshape(R, RWKV_W) for a in scan_cts]
    d_shifted, g_wup, g_aup, g_w0, g_a0, g_kk, g_ka = _rwkv_prep_bwd_call(
        shifted, scan_cts + [d_r2, d_v2, d_kts], wup_pad, aup_pad, w0_f, a0_f, k_k, k_a, bd, T)
    d_rin, g_taps = _shift_bwd_call(rin, d_shifted, taps_f, T)
    dqr, dkp, dvp = _att_bwd_call(qr, kpad, vpad, d_yatt, B, T)
    grad_x, dproj, d_shift, d_scale, g_gpre, g_qg, g_kg = _in_proj_bwd_call(
        x2, dy, shift, scale, g_pre, w_in_f, qg, kg, cos, sin, bd, q_raw, k_raw, dqr, dkp, dvp, d_gatt, d_rin, d_grw, T)
    g_win = _w_in_grad_call(hb, dproj)

    rep = jnp.concatenate([g_gpre.reshape(-1), g_qg.reshape(-1, HEAD_DIM).sum(0), g_kg.reshape(-1, HEAD_DIM).sum(0),
                           g_kk.reshape(-1), g_ka.reshape(-1), g_rk.reshape(-1), g_gnw.reshape(-1), g_gnb.reshape(-1),
                           g_gpost.reshape(-1), loss_blk[0, :1]])
    rep = jnp.pad(rep, (0, _REP_ROWS * 128 - rep.shape[0])).reshape(_REP_ROWS, 128)
    dmod = jnp.concatenate([d_shift, d_scale, d_gate], axis=2).reshape(B, 3 * D_MODEL)
    small_parts = jax.vmap(_pack_small)(_full_to_cols(g_taps[:3]), _full_to_cols(g_wup[:, :64, :]), _full_to_cols(g_w0),
                                        _full_to_cols(g_aup[:, 64:, :]), _full_to_cols(g_a0))
    core = lax.axis_index("c")
    halves = [a.reshape((NDEV // 2, 2) + a.shape[1:]).astype(MXU_DTYPE)
              for a in (_full_to_cols(g_win), g_wout.reshape(NDEV, D_MODEL // NDEV, D_MODEL))]
    pick = lambda a, j: lax.dynamic_index_in_dim(a, j, axis=1, keepdims=False)
    s_win, s_wout = _pair_sum_call([pick(a, core) for a in halves], [pick(a, 1 - core) for a in halves], "reduce_pair")
    p_win, p_wout, p_small, dmod_g, rep_g = _exchange(
        [s_win, s_wout, small_parts, dmod, rep], ["chips", "chips", "scatter", "all", "all"], "reduce_grads")
    dmod_all = dmod_g.reshape(NDEV * B, 3 * D_MODEL)
    g_wada = _wada_grad_call(c_all, lax.dynamic_slice(dmod_all, (0, me * ncol), (NDEV * B, ncol)))

    res = {}

    def adam(name, parts, w, m, v, row_tile=None):
        shape = w.shape
        two_d = (-1, shape[-1])
        out = _adam_call(parts.reshape((parts.shape[0],) + w.reshape(two_d).shape), w.reshape(two_d), m.reshape(two_d),
                         v.reshape(two_d), "adam_" + name, row_tile)
        res[name] = [o.reshape(shape) for o in out]

    adam("w_ada", g_wada[None], w_ada, m_w_ada, v_w_ada)
    adam("b_ada", dmod_all.reshape(NDEV * B, 1, 3 * D_MODEL), b_ada, m_b_ada, v_b_ada)
    adam("w_in", p_win, w_in, m_w_in, v_w_in, 128)
    adam("w_out", p_wout, w_out, m_w_out, v_w_out)
    taps_p, wup_p, w0_p, aup_p, a0_p = _unpack_small(p_small)
    adam("shift_taps", taps_p, shift_taps, m_shift_taps, v_shift_taps)
    adam("w_up", wup_p, w_up, m_w_up, v_w_up)
    adam("w0", w0_p, w0, m_w0, v_w0)
    adam("a_up", aup_p, a_up, m_a_up, v_a_up)
    adam("a0", a0_p, a0, m_a0, v_a0)
    rep_flat = rep_g.reshape(NDEV, -1)
    off = 0
    given = dict(g_pre=(g_pre, m_g_pre, v_g_pre), q_norm_g=(q_norm_g, m_q_norm_g, v_q_norm_g),
                 k_norm_g=(k_norm_g, m_k_norm_g, v_k_norm_g), k_k=(k_k, m_k_k, v_k_k), k_a=(k_a, m_k_a, v_k_a),
                 r_k=(r_k, m_r_k, v_r_k), gn_w=(gn_w, m_gn_w, v_gn_w), gn_b=(gn_b, m_gn_b, v_gn_b),
                 g_post=(g_post, m_g_post, v_g_post))
    for name, size in _REP_SIZES:
        adam(name, rep_flat[:, off:off + size], *given[name])
        off += size

    loss = jnp.sum(rep_flat[:, off])
    order = ["w_ada", "b_ada", "g_pre", "w_in", "q_norm_g", "k_norm_g", "shift_taps", "w_up", "w0", "a_up", "a0", "k_k",
             "k_a", "r_k", "gn_w", "gn_b", "w_out", "g_post"]
    return (loss, grad_x.reshape(B, T, D_MODEL), *[res[n][0] for n in order], *[res[n][1] for n in order],
            *[res[n][2] for n in order], *[res[n][3] for n in order])
```

```python
import functools

import jax
import jax.numpy as jnp
from jax import lax
from jax.experimental import pallas as pl
from jax.experimental.pallas import tpu as pltpu

F32 = jnp.float32
MXU_DTYPE = jnp.bfloat16
MESH = pl.DeviceIdType.MESH
NDEV = 8

D_MODEL = 1024
HEAD_DIM = 64
ATT_W = 512
KV_W = 128
RWKV_W = 512
LORA_W = 128
SHIFT_W = 3 * RWKV_W + LORA_W
GRID_W = 64
ROPE_THETA = 10000.0
DECAY_SCALE = 0.6065306597126334
NORM_EPS = 1e-6
GN_EPS = 64e-5
L2_EPS = 1e-12
ATT_SCALE = HEAD_DIM ** -0.5
C_Q, C_K, C_V, C_GA, C_RIN, C_GRW, C_END = 0, 512, 640, 768, 1280, 2944, 3456

ADAM_LR, ADAM_B1, ADAM_B2, ADAM_EPS, ADAM_WD, ADAM_STEP = 0.001, 0.9, 0.999, 1e-08, 0.01, 10

ROW_TILE = 256
W_GRAD_ROWS = 1024
ATT_TILE_FWD = 256
ATT_TILE_BWD = 512
SCAN_CHUNK = 64
SCAN_UNROLL = 16
VMEM_LIMIT = 56 * 1024 * 1024


def _cp(sem=None):
    return pltpu.CompilerParams(dimension_semantics=sem, vmem_limit_bytes=VMEM_LIMIT)


def _dot(a, b, dims=(((1,), (0,)), ((), ()))):
    return lax.dot_general(a.astype(MXU_DTYPE), b.astype(MXU_DTYPE), dims, preferred_element_type=F32)


def _dot_nt(a, b):
    return _dot(a, b, (((1,), (1,)), ((), ())))


def _dot_tn(a, b):
    return _dot(a, b, (((0,), (0,)), ((), ())))


def _seg_dot(xb, bd):
    n = xb.shape[1]
    if n <= 256:
        return jnp.dot(xb, bd[:n, :n], preferred_element_type=F32)
    parts = [jnp.dot(xb[:, c:c + 256], bd, preferred_element_type=F32) for c in range(0, n, 256)]
    return jnp.concatenate(parts, axis=1)


def _segsum_raw(x, bd):
    rows = x.shape[0]
    hi = x.astype(MXU_DTYPE)
    lo = (x - hi.astype(F32)).astype(MXU_DTYPE)
    both = _seg_dot(jnp.concatenate([hi, lo], axis=0), bd)
    return both[:rows] + both[rows:]


@jax.custom_vjp
def _segsum_d(x, bd):
    return _segsum_raw(x, bd)


def _segsum_d_fwd(x, bd):
    return _segsum_raw(x, bd), bd


def _segsum_d_bwd(bd, ct):
    return _segsum_raw(ct, bd), jnp.zeros_like(bd)


_segsum_d.defvjp(_segsum_d_fwd, _segsum_d_bwd)


def _rope_tables(T):
    t = jnp.arange(T, dtype=F32)
    row = jnp.floor(t / GRID_W)
    col = t - row * GRID_W
    n_freq = HEAD_DIM // 4
    inv_freq = ROPE_THETA ** (-jnp.arange(n_freq, dtype=F32) / n_freq)
    d = jnp.arange(HEAD_DIM)
    pos = jnp.where((d < HEAD_DIM // 2)[None, :], row[:, None], col[:, None])
    ang = pos * inv_freq[d % n_freq][None, :]
    sign = jnp.where((d % 32) < 16, -1.0, 1.0).astype(F32)[None, :]
    cos = jnp.cos(ang)
    sin = jnp.sin(ang) * sign
    return jnp.tile(cos, (1, 2)), jnp.tile(sin, (1, 2))


def _rope_raw(x, cos, sin):
    n = x.shape[1]
    lane = lax.broadcasted_iota(jnp.int32, (1, n), 1)
    first = (lane % 32) < 16
    partner = jnp.where(first, pltpu.roll(x, n - 16, 1), pltpu.roll(x, 16, 1))
    return x * cos + partner * sin


@jax.custom_vjp
def _rope_d(x, cos, sin):
    return _rope_raw(x, cos, sin)


def _rope_d_fwd(x, cos, sin):
    return _rope_raw(x, cos, sin), (cos, sin)


def _rope_d_bwd(res, ct):
    cos, sin = res
    return _rope_raw(ct, cos, -sin), jnp.zeros_like(cos), jnp.zeros_like(sin)


_rope_d.defvjp(_rope_d_fwd, _rope_d_bwd)


def _rms(x, g):
    return x * lax.rsqrt(jnp.mean(x * x, axis=-1, keepdims=True) + NORM_EPS) * g


def _pre_fn(x, shift, scale, g_pre):
    return _rms(x, g_pre) * (1.0 + scale) + shift


def _qk_fn(q, g, cos, sin, bd, scale, diff):
    segsum = _segsum_d if diff else _segsum_raw
    rope = _rope_d if diff else _rope_raw
    qn = q * lax.rsqrt(segsum(q * q, bd) * (1.0 / HEAD_DIM) + NORM_EPS) * g
    return rope(qn, cos, sin) * scale


def _silu(x):
    return x * jax.nn.sigmoid(x)


def _rwkv_pw(k, pw0, pw1, pa0, pa1, w0, a0, k_k, k_a, bd, diff):
    segsum = _segsum_d if diff else _segsum_raw
    kk = k * k_k
    kk = kk * lax.rsqrt(segsum(kk * kk, bd) + L2_EPS)
    ws, kts, akks = [], [], []
    for z, (pw, pa) in enumerate(((pw0, pa0), (pw1, pa1))):
        w = jnp.exp(-DECAY_SCALE * jax.nn.sigmoid(w0[z:z + 1, :] + pw))
        a = jax.nn.sigmoid(a0[z:z + 1, :] + pa)
        ws.append(w)
        kts.append(k * (1.0 + (a - 1.0) * k_a))
        akks.append(a * kk)
    return ws[0], ws[1], kts[0], kts[1], akks[0], akks[1], kk


def _mix_fn(y_att, g_att, ys, r, v, kts, g_rw, gn_w, gn_b, r_k, bd, diff):
    segsum = _segsum_d if diff else _segsum_raw
    mu = segsum(ys, bd) * (1.0 / HEAD_DIM)
    d = ys - mu
    var = segsum(d * d, bd) * (1.0 / HEAD_DIM)
    yn = d * lax.rsqrt(var + GN_EPS) * gn_w + gn_b
    bonus = segsum(r * kts * r_k, bd) * v
    return y_att * _silu(g_att), (yn + bonus) * _silu(g_rw)


def _loss_fn(out, x, tgt, gate, g_post):
    e = x + gate * _rms(out, g_post) - tgt
    s = jnp.sum(e * e, axis=1, keepdims=True)
    return jnp.sum(s, axis=0, keepdims=True) * (0.5 / D_MODEL)


def _exchange(arrays, modes, name):
    n = len(arrays)
    out_shape = tuple(
        jax.ShapeDtypeStruct(((NDEV,) + tuple(a.shape)) if mode == "all" else tuple(a.shape), a.dtype)
        for a, mode in zip(arrays, modes))
    chips = (4, 2, 6)

    def body(*refs):
        ins, outs = refs[:n], refs[n:2 * n]
        send_sems, recv_sems, local_sems = refs[2 * n:]
        ix, iy, ic = lax.axis_index("x"), lax.axis_index("y"), lax.axis_index("c")
        me = 4 * ix + 2 * iy + ic

        def peer(m):
            px = 1 - ix if (m >> 2) & 1 else ix
            py = 1 - iy if (m >> 1) & 1 else iy
            pc = 1 - ic if m & 1 else ic
            return (px, py, pc), 4 * px + 2 * py + pc

        def copy(k, j, src_ref, slot, to):
            return pltpu.make_async_remote_copy(src_ref=src_ref, dst_ref=outs[k].at[slot], send_sem=send_sems.at[k, j],
                                                recv_sem=recv_sems.at[k, j], device_id=to, device_id_type=MESH)

        local, sends, arrivals, forwards = [], [], [], []
        for k in range(n):
            if modes[k] == "scatter":
                local.append(pltpu.make_async_copy(ins[k].at[me], outs[k].at[me], local_sems.at[k]))
                for m in range(1, NDEV):
                    to, p = peer(m)
                    sends.append(copy(k, m - 1, ins[k].at[p], me, to))
                    arrivals.append(copy(k, m - 1, ins[k].at[p], p, to))
            elif modes[k] == "chips":
                mine = me // 2
                local.append(pltpu.make_async_copy(ins[k].at[mine], outs[k].at[mine], local_sems.at[k]))
                for j, m in enumerate(chips):
                    to, p = peer(m)
                    sends.append(copy(k, j, ins[k].at[p // 2], mine, to))
                    arrivals.append(copy(k, j, ins[k].at[p // 2], p // 2, to))
            else:
                local.append(pltpu.make_async_copy(ins[k], outs[k].at[me], local_sems.at[k]))
                sib, sib_slot = peer(1)
                sends.append(copy(k, 0, ins[k], me, sib))
                for j, m in enumerate(chips):
                    to, p = peer(m)
                    sends.append(copy(k, 1 + j, ins[k], me, to))
                    forwards.append((copy(k, 1 + j, ins[k], p, to), copy(k, 4 + j, outs[k].at[p], p, sib)))
                    arrivals.append(copy(k, 4 + j, ins[k], peer(m ^ 1)[1], sib))
                arrivals.append(copy(k, 0, ins[k], sib_slot, sib))
        for cp in local + sends:
            cp.start()
        for arrived, onward in forwards:
            arrived.wait_recv()
            onward.start()
        for cp in arrivals:
            cp.wait_recv()
        for cp in sends + [onward for _, onward in forwards]:
            cp.wait_send()
        for cp in local:
            cp.wait()

    any_spec = pl.BlockSpec(memory_space=pl.ANY)
    return pl.pallas_call(
        body, name=name, out_shape=out_shape,
        in_specs=[any_spec] * n, out_specs=tuple([any_spec] * n),
        scratch_shapes=[pltpu.SemaphoreType.DMA((n, NDEV - 1)), pltpu.SemaphoreType.DMA((n, NDEV - 1)),
                        pltpu.SemaphoreType.DMA((n,))],
    )(*arrays)


def _pair_sum_call(mine, send, name):
    n = len(mine)

    def body(*refs):
        mine_r, send_r, out_r, land_r = (refs[j * n:(j + 1) * n] for j in range(4))
        send_sems, recv_sems = refs[4 * n:]
        sibling = (lax.axis_index("x"), lax.axis_index("y"), 1 - lax.axis_index("c"))
        swaps = [pltpu.make_async_remote_copy(src_ref=send_r[k], dst_ref=land_r[k], send_sem=send_sems.at[k],
                                              recv_sem=recv_sems.at[k], device_id=sibling, device_id_type=MESH)
                 for k in range(n)]
        for cp in swaps:
            cp.start()
        for k, cp in enumerate(swaps):
            cp.wait()
            out_r[k][...] = (mine_r[k][...].astype(F32) + land_r[k][...].astype(F32)).astype(out_r[k].dtype)

    return pl.pallas_call(
        body, name=name, out_shape=tuple(jax.ShapeDtypeStruct(a.shape, a.dtype) for a in mine),
        scratch_shapes=[pltpu.VMEM(a.shape, a.dtype) for a in mine] + [pltpu.SemaphoreType.DMA((n,)),
                                                                         pltpu.SemaphoreType.DMA((n,))],
        compiler_params=pltpu.CompilerParams(vmem_limit_bytes=VMEM_LIMIT),
    )(*mine, *send)


def _mod_call(c_all, w_ada, b_cols):
    def body(c_ref, w_ref, b_ref, o_ref):
        o_ref[...] = _dot(_silu(c_ref[...]), w_ref[...]) + b_ref[...]

    return pl.pallas_call(body, name="mod_fwd",
                          out_shape=jax.ShapeDtypeStruct((c_all.shape[0], w_ada.shape[1]), F32))(c_all, w_ada, b_cols)


def _wada_grad_call(c_all, dmod_cols):
    def body(c_ref, d_ref, o_ref):
        o_ref[...] = _dot_tn(_silu(c_ref[...]), d_ref[...])

    return pl.pallas_call(body, name="w_ada_grad",
                          out_shape=jax.ShapeDtypeStruct((c_all.shape[1], dmod_cols.shape[1]), F32))(c_all, dmod_cols)


def _full(shape):
    nd = len(shape)
    return pl.BlockSpec(shape, lambda *_: (0,) * nd)


def _in_proj_call(x2, shift, scale, g_pre, w_in, qg, kg, cos, sin, bd, T):
    R = x2.shape[0]
    TT = min(ROW_TILE, T)
    tpe = T // TT

    def body(x_ref, sh_ref, sc_ref, gp_ref, w_ref, qg_ref, kg_ref, cos_ref, sin_ref, bd_ref,
             hb_ref, qr_ref, kpad_ref, vpad_ref, qraw_ref, kraw_ref, gatt_ref, rin_ref, grw_ref):
        h = _pre_fn(x_ref[...], sh_ref[0], sc_ref[0], gp_ref[...])
        hb = h.astype(MXU_DTYPE)
        hb_ref[...] = hb

        def proj(c0, c1):
            return _dot_nt(hb, w_ref[c0:c1, :])

        q = proj(C_Q, C_K)
        k = proj(C_K, C_V)
        v = proj(C_V, C_GA)
        gatt_ref[...] = proj(C_GA, C_RIN)
        rin_ref[...] = proj(C_RIN, C_GRW)
        grw_ref[...] = proj(C_GRW, C_END)
        qraw_ref[...] = q
        kraw_ref[...] = k
        cos, sin, bd = cos_ref[...], sin_ref[...], bd_ref[...]
        qr = _qk_fn(q, qg_ref[...], jnp.tile(cos, (1, 4)), jnp.tile(sin, (1, 4)), bd, ATT_SCALE, False)
        qr_ref[...] = qr.astype(MXU_DTYPE)
        kr = _qk_fn(k, kg_ref[...], cos, sin, bd, 1.0, False)
        left = lax.broadcasted_iota(jnp.int32, (1, KV_W), 1) < HEAD_DIM
        for ref, val in ((kpad_ref, kr), (vpad_ref, v)):
            h0l = jnp.where(left, val, 0.0)
            h1r = jnp.where(left, 0.0, val)
            ref[0] = h0l.astype(MXU_DTYPE)
            ref[1] = pltpu.roll(h0l, HEAD_DIM, 1).astype(MXU_DTYPE)
            ref[2] = pltpu.roll(h1r, HEAD_DIM, 1).astype(MXU_DTYPE)
            ref[3] = h1r.astype(MXU_DTYPE)

    row = lambda w: pl.BlockSpec((TT, w), lambda i: (i, 0))
    per_ex = pl.BlockSpec((1, 1, D_MODEL), lambda i: (i // tpe, 0, 0))
    tab = pl.BlockSpec((TT, KV_W), lambda i: (i % tpe, 0))
    pad = pl.BlockSpec((4, TT, KV_W), lambda i: (0, i, 0))
    sds = jax.ShapeDtypeStruct
    return pl.pallas_call(
        body, name="in_proj", grid=(R // TT,),
        in_specs=[row(D_MODEL), per_ex, per_ex, _full((1, D_MODEL)), _full(w_in.shape), _full((1, ATT_W)),
                  _full((1, KV_W)), tab, tab, _full((256, 256))],
        out_specs=(row(D_MODEL), row(ATT_W), pad, pad, row(ATT_W), row(KV_W), row(ATT_W), row(SHIFT_W), row(RWKV_W)),
        out_shape=(sds((R, D_MODEL), MXU_DTYPE), sds((R, ATT_W), MXU_DTYPE), sds((4, R, KV_W), MXU_DTYPE),
                   sds((4, R, KV_W), MXU_DTYPE), sds((R, ATT_W), F32), sds((R, KV_W), F32), sds((R, ATT_W), F32),
                   sds((R, SHIFT_W), F32), sds((R, RWKV_W), F32)),
        compiler_params=_cp(("arbitrary",)),
    )(x2, shift, scale, g_pre, w_in, qg, kg, cos, sin, bd)


def _softmax_parts(s):
    e = jnp.exp(s - jnp.max(s, axis=1, keepdims=True))
    return e, 1.0 / jnp.sum(e, axis=1, keepdims=True)


def _att_specs(T, TQ):
    nq = T // TQ
    qspec = pl.BlockSpec((TQ, KV_W), lambda b, p, i: (b * nq + i, p))
    side = lambda s: pl.BlockSpec((None, T, KV_W), lambda b, p, i: (2 * (p // 2) + s, b, 0))
    return nq, qspec, side


def _att_fwd_call(qr, kpad, vpad, B, T):
    TQ = min(ATT_TILE_FWD, T)
    nq, qspec, side = _att_specs(T, TQ)

    def body(q_ref, kl_ref, kr_ref, vl_ref, vr_ref, o_ref):
        q = q_ref[...]
        ea, inv_a = _softmax_parts(_dot_nt(q, kl_ref[...]))
        eb, inv_b = _softmax_parts(_dot_nt(q, kr_ref[...]))
        o_ref[...] = _dot(ea, vl_ref[...]) * inv_a + _dot(eb, vr_ref[...]) * inv_b

    return pl.pallas_call(
        body, name="att_fwd", grid=(B, 4, nq),
        in_specs=[qspec, side(0), side(1), side(0), side(1)], out_specs=qspec,
        out_shape=jax.ShapeDtypeStruct((B * T, ATT_W), F32),
        compiler_params=_cp(("arbitrary",) * 3),
    )(qr, kpad, kpad, vpad, vpad)


def _att_bwd_call(qr, kpad, vpad, d_o, B, T):
    TQ = min(ATT_TILE_BWD, T)
    nq, qspec, side = _att_specs(T, TQ)

    def body(q_ref, kl_ref, kr_ref, vl_ref, vr_ref, do_ref, dq_ref, dk_ref, dv_ref):
        i = pl.program_id(2)
        q, do = q_ref[...], do_ref[...]
        left = lax.broadcasted_iota(jnp.int32, (1, KV_W), 1) < HEAD_DIM
        dq = jnp.zeros((TQ, KV_W), F32)
        dk = jnp.zeros((T, KV_W), F32)
        dv = jnp.zeros((T, KV_W), F32)
        for k_ref, v_ref, mask in ((kl_ref, vl_ref, left), (kr_ref, vr_ref, jnp.logical_not(left))):
            kk, vv = k_ref[...], v_ref[...]
            e, inv = _softmax_parts(_dot_nt(q, kk))
            dp = _dot_nt(do, vv)
            ds = e * (dp - inv * jnp.sum(e * dp, axis=1, keepdims=True))
            dq = dq + _dot(ds, kk) * inv
            dk = dk + _dot_tn(ds, jnp.where(mask, q * inv, 0.0))
            dv = dv + _dot_tn(e, jnp.where(mask, do * inv, 0.0))
        dq_ref[...] = dq

        @pl.when(i == 0)
        def _():
            dk_ref[...] = dk
            dv_ref[...] = dv

        @pl.when(i > 0)
        def _():
            dk_ref[...] += dk
            dv_ref[...] += dv

    acc = pl.BlockSpec((None, T, KV_W), lambda b, p, i: (p, b, 0))
    sds = jax.ShapeDtypeStruct
    return pl.pallas_call(
        body, name="att_bwd", grid=(B, 4, nq),
        in_specs=[qspec, side(0), side(1), side(0), side(1), qspec], out_specs=(qspec, acc, acc),
        out_shape=(sds((B * T, ATT_W), F32), sds((4, B * T, KV_W), F32), sds((4, B * T, KV_W), F32)),
        compiler_params=_cp(("arbitrary",) * 3),
    )(qr, kpad, kpad, vpad, vpad, d_o)


def _shift_specs(R, T, TT, width):
    tpe = T // TT
    nb8 = R // 8
    cur = pl.BlockSpec((TT, width), lambda i: (i, 0))
    prev = pl.BlockSpec((8, width), lambda i: (jnp.maximum(i * (TT // 8) - 1, 0), 0))
    nxt = pl.BlockSpec((8, width), lambda i: (jnp.minimum((i + 1) * (TT // 8), nb8 - 1), 0))
    return tpe, cur, prev, nxt


def _neighbours(cur, prev8, next8, i, tpe, TT):
    rows = lax.broadcasted_iota(jnp.int32, (TT, 1), 0)
    first = jnp.where(i % tpe == 0, 0.0, 1.0)
    last = jnp.where(i % tpe == tpe - 1, 0.0, 1.0)
    before = jnp.where(rows == 0, prev8[7:8, :] * first, pltpu.roll(cur, 1, 0))
    after = jnp.where(rows == TT - 1, next8[0:1, :] * last, pltpu.roll(cur, TT - 1, 0))
    return before, after


def _shift_fwd_call(x, taps, T):
    R, width = x.shape
    TT = min(ROW_TILE, T)
    tpe, cur, prev, nxt = _shift_specs(R, T, TT, width)

    def body(x_ref, p_ref, n_ref, t_ref, o_ref):
        xc = x_ref[...]
        before, after = _neighbours(xc, p_ref[...], n_ref[...], pl.program_id(0), tpe, TT)
        o_ref[...] = t_ref[0:1, :] * before + t_ref[1:2, :] * xc + t_ref[2:3, :] * after

    return pl.pallas_call(
        body, name="shift_fwd", grid=(R // TT,), in_specs=[cur, prev, nxt, _full(taps.shape)], out_specs=cur,
        out_shape=jax.ShapeDtypeStruct((R, width), F32), compiler_params=_cp(("arbitrary",)),
    )(x, x, x, taps)


def _shift_bwd_call(x, d, taps, T):
    R, width = x.shape
    TT = min(ROW_TILE, T)
    tpe, cur, prev, nxt = _shift_specs(R, T, TT, width)

    def body(x_ref, xp_ref, xn_ref, d_ref, dp_ref, dn_ref, t_ref, dx_ref, dt_ref):
        i = pl.program_id(0)
        xc, dc = x_ref[...], d_ref[...]
        d_before, d_after = _neighbours(dc, dp_ref[...], dn_ref[...], i, tpe, TT)
        dx_ref[...] = t_ref[2:3, :] * d_before + t_ref[1:2, :] * dc + t_ref[0:1, :] * d_after
        x_before, x_after = _neighbours(xc, xp_ref[...], xn_ref[...], i, tpe, TT)
        @pl.when(i == 0)
        def _():
            dt_ref[...] = jnp.zeros_like(dt_ref)

        for j, xs in enumerate((x_before, xc, x_after)):
            dt_ref[j:j + 1, :] += jnp.sum(dc * xs, axis=0, keepdims=True)

    return pl.pallas_call(
        body, name="shift_bwd", grid=(R // TT,),
        in_specs=[cur, prev, nxt, cur, prev, nxt, _full(taps.shape)], out_specs=(cur, _full((8, width))),
        out_shape=(jax.ShapeDtypeStruct((R, width), F32), jax.ShapeDtypeStruct((8, width), F32)),
        compiler_params=_cp(("arbitrary",)),
    )(x, x, x, d, d, d, taps)


def _lora_in(wa):
    lane = lax.broadcasted_iota(jnp.int32, (1, LORA_W), 1)
    return jnp.where(lane < LORA_W // 2, jnp.tanh(wa), wa)


def _rwkv_prep_call(shifted, wup, aup, w0, a0, k_k, k_a, bd, T):
    R = shifted.shape[0]
    TT = min(ROW_TILE, T)

    def body(k_ref, wa_ref, wup_ref, aup_ref, w0_ref, a0_ref, kk_ref, ka_ref, bd_ref, w_o, kt_o, akk_o, kk_o):
        twa = _lora_in(wa_ref[...])
        pre = [_dot(twa, m_ref[z]) for m_ref in (wup_ref, aup_ref) for z in range(2)]
        outs = _rwkv_pw(k_ref[...], pre[0], pre[1], pre[2], pre[3], w0_ref[...], a0_ref[...], kk_ref[...],
                        ka_ref[...], bd_ref[...], False)
        w_o[0], w_o[1], kt_o[0], kt_o[1], akk_o[0], akk_o[1] = outs[:6]
        kk_o[...] = outs[6]

    col = lambda c, w: pl.BlockSpec((TT, w), lambda i: (i, c))
    two = pl.BlockSpec((2, TT, RWKV_W), lambda i: (0, i, 0))
    sds = jax.ShapeDtypeStruct
    return pl.pallas_call(
        body, name="rwkv_prep", grid=(R // TT,),
        in_specs=[col(1, RWKV_W), col(3 * RWKV_W // LORA_W, LORA_W), _full(wup.shape), _full(aup.shape),
                  _full((2, RWKV_W)), _full((2, RWKV_W)), _full((1, RWKV_W)), _full((1, RWKV_W)), _full((256, 256))],
        out_specs=(two, two, two, col(0, RWKV_W)),
        out_shape=(sds((2, R, RWKV_W), F32),) * 3 + (sds((R, RWKV_W), F32),),
        compiler_params=_cp(("arbitrary",)),
    )(shifted, shifted, wup, aup, w0, a0, k_k, k_a, bd)


def _rwkv_prep_bwd_call(shifted, cts, wup, aup, w0, a0, k_k, k_a, bd, T):
    R = shifted.shape[0]
    TT = min(ROW_TILE, T)

    def body(k_ref, wa_ref, dw0, dkt0, dakk0, dkk0, dr0, dv0, dw1, dkt1, dakk1, dkk1, dr1, dv1, dr2_ref, dv2_ref, dkts_ref,
             wup_ref, aup_ref, w0_ref, a0_ref, kk_ref, ka_ref, bd_ref,
             dsh_ref, gwup_ref, gaup_ref, gw0_ref, ga0_ref, gkk_ref, gka_ref):
        dw_ref, dkt_ref, dakk_ref, dkk_ref, dr_ref, dv_ref = ((dw0, dw1), (dkt0, dkt1), (dakk0, dakk1), (dkk0, dkk1),
                                                              (dr0, dr1), (dv0, dv1))
        i = pl.program_id(0)
        wa = wa_ref[...]
        twa = _lora_in(wa)
        pre = [_dot(twa, m_ref[z]) for m_ref in (wup_ref, aup_ref) for z in range(2)]
        fn = functools.partial(_rwkv_pw, bd=bd_ref[...], diff=True)
        _, vjp = jax.vjp(fn, k_ref[...], pre[0], pre[1], pre[2], pre[3], w0_ref[...], a0_ref[...], kk_ref[...],
                         ka_ref[...])
        dkts = dkts_ref[...]
        dk, dpw0, dpw1, dpa0, dpa1, gw0, ga0, gkk, gka = vjp(
            (dw_ref[0][...], dw_ref[1][...], dkt_ref[0][...] + dkts, dkt_ref[1][...] + dkts, dakk_ref[0][...],
             dakk_ref[1][...], dkk_ref[0][...] + dkk_ref[1][...]))
        dtwa = (_dot_nt(dpw0, wup_ref[0]) + _dot_nt(dpw1, wup_ref[1]) + _dot_nt(dpa0, aup_ref[0])
                + _dot_nt(dpa1, aup_ref[1]))
        lane = lax.broadcasted_iota(jnp.int32, (1, LORA_W), 1)
        dsh_ref[:, 0:RWKV_W] = dr_ref[0][...] + dr_ref[1][...] + dr2_ref[...]
        dsh_ref[:, RWKV_W:2 * RWKV_W] = dk
        dsh_ref[:, 2 * RWKV_W:3 * RWKV_W] = dv_ref[0][...] + dv_ref[1][...] + dv2_ref[...]
        dsh_ref[:, 3 * RWKV_W:] = jnp.where(lane < LORA_W // 2, dtwa * (1.0 - twa * twa), dtwa)
        acc = ((gwup_ref.at[0], _dot_tn(twa, dpw0)), (gwup_ref.at[1], _dot_tn(twa, dpw1)),
               (gaup_ref.at[0], _dot_tn(twa, dpa0)), (gaup_ref.at[1], _dot_tn(twa, dpa1)),
               (gw0_ref, gw0), (ga0_ref, ga0), (gkk_ref, gkk), (gka_ref, gka))

        @pl.when(i == 0)
        def _():
            for ref, val in acc:
                ref[...] = val

        @pl.when(i > 0)
        def _():
            for ref, val in acc:
                ref[...] += val

    col = lambda c, w: pl.BlockSpec((TT, w), lambda i: (i, c))
    one = col(0, RWKV_W)
    sds = jax.ShapeDtypeStruct
    return pl.pallas_call(
        body, name="rwkv_prep_bwd", grid=(R // TT,),
        in_specs=[col(1, RWKV_W), col(3 * RWKV_W // LORA_W, LORA_W)] + [one] * 15 + [
                  _full(wup.shape), _full(aup.shape), _full((2, RWKV_W)), _full((2, RWKV_W)), _full((1, RWKV_W)),
                  _full((1, RWKV_W)), _full((256, 256))],
        out_specs=(pl.BlockSpec((TT, SHIFT_W), lambda i: (i, 0)), _full(wup.shape), _full(aup.shape),
                   _full((2, RWKV_W)), _full((2, RWKV_W)), _full((1, RWKV_W)), _full((1, RWKV_W))),
        out_shape=(sds((R, SHIFT_W), F32), sds(wup.shape, F32), sds(aup.shape, F32), sds((2, RWKV_W), F32),
                   sds((2, RWKV_W), F32), sds((1, RWKV_W), F32), sds((1, RWKV_W), F32)),
        compiler_params=_cp(("arbitrary",)),
    )(shifted, shifted, *cts, wup, aup, w0, a0, k_k, k_a, bd)


def _col_lhs(row, eye_b):
    return eye_b * row.astype(MXU_DTYPE)


def _colsum(x):
    return jnp.sum(x, axis=0, keepdims=True)


def _stacked_segsum(tiles, bd):
    res = _seg_dot(jnp.concatenate(tiles, axis=0), bd)
    return [res[j * HEAD_DIM:(j + 1) * HEAD_DIM] for j in range(len(tiles))]


def _scan_specs(B, T, C, nC):
    def blk(z, col, rev):
        idx = (lambda g: (z, 0, nC - 1 - g, col)) if rev else (lambda g: (z, 0, g, col))
        return pl.BlockSpec((None, B, C, RWKV_W), idx)

    def blk3(col, rev):
        idx = (lambda g: (0, nC - 1 - g, col)) if rev else (lambda g: (0, g, col))
        return pl.BlockSpec((B, C, RWKV_W), idx)

    return blk, blk3


def _scan_fwd_call(w, kt, akk, kk, shifted, eye_b, eye_f, bd, B, T):
    C = min(SCAN_CHUNK, T)
    nC = T // C
    blk, blk3 = _scan_specs(B, T, C, nC)

    def body(w0, kt0, akk0, kk0, v0, r0, w1, kt1, akk1, kk1, v1, r1, eb_ref, ef_ref, bd_ref, y0, y1, st, S):
        @pl.when(pl.program_id(0) == 0)
        def _():
            S[...] = jnp.zeros_like(S)

        st[0] = S[...].astype(MXU_DTYPE)
        dirs = ((w0, kt0, akk0, kk0, v0, r0, y0), (w1, kt1, akk1, kk1, v1, r1, y1))

        def step(s, carry):
            tiles = []
            for z in range(2):
                row = s if z == 0 else C - 1 - s
                prev = jnp.maximum(s - 1, 0) if z == 0 else jnp.minimum(C - s, C - 1)
                wr, ktr, akkr, kkr, vr, rr, yr = dirs[z]
                for b in range(B):
                    Sb = st[s, z * B + b]
                    tiles += [Sb * kkr[b, pl.ds(row, 1), :].astype(MXU_DTYPE),
                              _col_lhs(vr[b, pl.ds(row, 1), :], eb_ref[...]),
                              Sb * rr[b, pl.ds(prev, 1), :].astype(MXU_DTYPE)]
            res = _stacked_segsum(tiles, bd_ref[...])
            for z in range(2):
                row = s if z == 0 else C - 1 - s
                prev = jnp.maximum(s - 1, 0) if z == 0 else jnp.minimum(C - s, C - 1)
                wr, ktr, akkr, kkr, vr, rr, yr = dirs[z]
                for b in range(B):
                    c = z * B + b
                    sab, vb, yb = res[3 * c:3 * c + 3]
                    ld = lambda ref: ref[b, pl.ds(row, 1), :]
                    Sn = S[c] * ld(wr) - sab * ld(akkr) + vb * ld(ktr)
                    S[c] = Sn
                    st[s + 1, c] = Sn.astype(MXU_DTYPE)
                    yr[b, pl.ds(prev, 1), :] = _colsum(ef_ref[...] * yb)
            return carry

        lax.fori_loop(0, C, step, 0, unroll=SCAN_UNROLL)
        for z in range(2):
            last = C - 1 if z == 0 else 0
            rr, yr = dirs[z][5], dirs[z][6]
            res = _stacked_segsum([st[C, z * B + b] * rr[b, last:last + 1, :].astype(MXU_DTYPE) for b in range(B)],
                                  bd_ref[...])
            for b in range(B):
                yr[b, last:last + 1, :] = _colsum(ef_ref[...] * res[b])

    ins, specs = [], []
    for z, rev in ((0, False), (1, True)):
        ins += [w, kt, akk, kk, shifted, shifted]
        specs += [blk(z, 0, rev), blk(z, 0, rev), blk(z, 0, rev), blk3(0, rev), blk3(2, rev), blk3(0, rev)]
    sds = jax.ShapeDtypeStruct
    return pl.pallas_call(
        body, name="scan_fwd", grid=(nC,),
        in_specs=specs + [_full((HEAD_DIM, RWKV_W)), _full((HEAD_DIM, RWKV_W)), _full((256, 256))],
        out_specs=(blk3(0, False), blk3(0, True),
                   pl.BlockSpec((None, C + 1, 2 * B, HEAD_DIM, RWKV_W), lambda g: (g, 0, 0, 0, 0))),
        out_shape=(sds((B, T, RWKV_W), F32), sds((B, T, RWKV_W), F32),
                   sds((nC, C + 1, 2 * B, HEAD_DIM, RWKV_W), MXU_DTYPE)),
        scratch_shapes=[pltpu.VMEM((2 * B, HEAD_DIM, RWKV_W), F32)],
        compiler_params=_cp(("arbitrary",)),
    )(*ins, eye_b, eye_f, bd)


def _scan_bwd_call(w, kt, akk, kk, shifted, v_heads, dys, st, eye_b, eye_f, bd, B, T):
    C = min(SCAN_CHUNK, T)
    nC = T // C
    blk, blk3 = _scan_specs(B, T, C, nC)
    nin = 7

    def body(*refs):
        d0, d1 = refs[:nin], refs[nin:2 * nin]
        st_ref, eb_ref, ef_ref, sel_ref, hm_ref, bd_ref = refs[2 * nin:2 * nin + 6]
        o0, o1 = refs[2 * nin + 6:2 * nin + 12], refs[2 * nin + 12:2 * nin + 18]
        COL, DYC, G = refs[2 * nin + 18:]

        @pl.when(pl.program_id(0) == 0)
        def _():
            G[...] = jnp.zeros_like(G)

        dirs = (d0 + (o0,), d1 + (o1,))

        def column_operands(s, z):
            row = s if z == 0 else C - 1 - s
            _, _, _, kkr, _, _, dyr, _ = dirs[z]
            tiles = []
            for b in range(B):
                tiles += [st_ref[s, z * B + b] * kkr[b, pl.ds(row, 1), :].astype(MXU_DTYPE),
                          _col_lhs(dyr[b, pl.ds(row, 1), :], eb_ref[...])]
            return tiles

        def keep_columns(res, z):
            for b in range(B):
                for k in range(2):
                    COL[k, z * B + b] = res[2 * b + k].astype(MXU_DTYPE)
                DYC[z * B + b] = res[2 * b + 1]

        for z in range(2):
            keep_columns(_stacked_segsum(column_operands(C - 1, z), bd_ref[...]), z)

        def bwd(it, carry):
            s = C - 1 - it
            tiles, Gcs, cols = [], [], []
            for z in range(2):
                row = s if z == 0 else C - 1 - s
                _, ktr, akkr, _, _, rr, _, _ = dirs[z]
                for b in range(B):
                    c = z * B + b
                    Gc = G[c] + DYC[c] * rr[b, pl.ds(row, 1), :]
                    Gb = Gc.astype(MXU_DTYPE)
                    Gcs.append((Gc, Gb))
                    tiles += [Gb * akkr[b, pl.ds(row, 1), :].astype(MXU_DTYPE),
                              Gb * ktr[b, pl.ds(row, 1), :].astype(MXU_DTYPE)]
                cols += column_operands(jnp.maximum(s - 1, 0), z)
            res = _stacked_segsum(tiles + cols, bd_ref[...])
            for z in range(2):
                row = s if z == 0 else C - 1 - s
                wr, ktr, akkr, kkr, vr, rr, dyr, (dw_o, dkt_o, dakk_o, dkk_o, dr_o, dv_o) = dirs[z]
                for b in range(B):
                    c = z * B + b
                    Gc, Gb = Gcs[c]
                    gab, dvb = res[2 * c], res[2 * c + 1]
                    ld = lambda ref: ref[b, pl.ds(row, 1), :]
                    G[c] = Gc * ld(wr) - gab * ld(kkr)
                    Sb = st_ref[s, c]
                    prods = jnp.concatenate([Gb, st_ref[s + 1, c] * COL[1, c], Gb * Sb, Gb * COL[0, c],
                                             gab.astype(MXU_DTYPE) * Sb], axis=0)
                    v_rows = jnp.concatenate([vr[b, pl.ds(row, 1)][0], jnp.zeros((8, 3 * HEAD_DIM), F32)], axis=1)
                    lhs = jnp.concatenate([sel_ref[...], v_rows], axis=0).astype(MXU_DTYPE)
                    sums = jnp.dot(lhs, prods, preferred_element_type=F32)
                    for k, (ref, sign) in enumerate(((dr_o, 1.0), (dw_o, 1.0), (dakk_o, -1.0), (dkk_o, -1.0))):
                        ref[b, pl.ds(row, 1), :] = sign * sums[k:k + 1, :]
                    dkt_o[b, pl.ds(row, 1), :] = _colsum(sums[8:16] * hm_ref[...])
                    dv_o[b, pl.ds(row, 1), :] = _colsum(ef_ref[...] * dvb)
            for z in range(2):
                keep_columns(res[4 * B + 2 * B * z:4 * B + 2 * B * (z + 1)], z)
            return carry

        lax.fori_loop(0, C, bwd, 0, unroll=SCAN_UNROLL)

    ins, specs = [], []
    for z, rev in ((0, True), (1, False)):
        heads = pl.BlockSpec((B, C) + v_heads.shape[2:], (lambda g: (0, nC - 1 - g, 0, 0)) if rev else (lambda g: (0, g, 0, 0)))
        ins += [w, kt, akk, kk, v_heads, shifted, dys]
        specs += [blk(z, 0, rev), blk(z, 0, rev), blk(z, 0, rev), blk3(0, rev), heads, blk3(0, rev), blk3(0, rev)]
    sel = (jnp.arange(8)[:, None] + 1 == (jnp.arange(5 * HEAD_DIM) // HEAD_DIM)[None, :]).astype(F32)
    head_rows = (jnp.arange(RWKV_W // HEAD_DIM)[:, None] == (jnp.arange(RWKV_W) // HEAD_DIM)[None, :]).astype(F32)
    ins += [st, eye_b, eye_f, sel, head_rows, bd]
    specs += [pl.BlockSpec((None, C + 1, 2 * B, HEAD_DIM, RWKV_W), lambda g: (nC - 1 - g, 0, 0, 0, 0)),
              _full((HEAD_DIM, RWKV_W)), _full((HEAD_DIM, RWKV_W)), _full(sel.shape), _full(head_rows.shape),
              _full((256, 256))]
    sds = jax.ShapeDtypeStruct
    out_specs = tuple(blk3(0, True) for _ in range(6)) + tuple(blk3(0, False) for _ in range(6))
    res = pl.pallas_call(
        body, name="scan_bwd", grid=(nC,), in_specs=specs, out_specs=out_specs,
        out_shape=tuple(sds((B, T, RWKV_W), F32) for _ in range(12)),
        scratch_shapes=[pltpu.VMEM((2, 2 * B, HEAD_DIM, RWKV_W), MXU_DTYPE), pltpu.VMEM((2 * B, HEAD_DIM, RWKV_W), F32),
                        pltpu.VMEM((2 * B, HEAD_DIM, RWKV_W), F32)],
        compiler_params=_cp(("arbitrary",)),
    )(*ins)
    return list(res)


def _out_head_call(x2, tgt2, gate, y_att, g_att, y0, y1, shifted, kt, g_rw, w_out, g_post, gn_w, gn_b, r_k, bd, T):
    R = x2.shape[0]
    TT = min(ROW_TILE, T)
    tpe = T // TT

    def body(x_ref, t_ref, gate_ref, ya_ref, ga_ref, y0_ref, y1_ref, r_ref, v_ref, kt_ref, grw_ref, w_ref, gp_ref,
             gnw_ref, gnb_ref, rk_ref, bd_ref,
             loss_o, dy_o, dya_o, dga_o, dys_o, dr_o, dv_o, dkts_o, dgrw_o, dgate_o, gw_o, ggp_o, ggnw_o, ggnb_o, grk_o):
        i = pl.program_id(0)
        bd = bd_ref[...]
        mix = functools.partial(_mix_fn, bd=bd, diff=True)
        (ma, mr), mix_vjp = jax.vjp(mix, ya_ref[...], ga_ref[...], y0_ref[...] + y1_ref[...], r_ref[...], v_ref[...],
                                    kt_ref[0] + kt_ref[1], grw_ref[...], gnw_ref[...], gnb_ref[...], rk_ref[...])
        out = _dot(ma, w_ref[0:ATT_W, :]) + _dot(mr, w_ref[ATT_W:, :])
        loss, loss_vjp = jax.vjp(_loss_fn, out, x_ref[...], t_ref[...], gate_ref[0], gp_ref[...])
        d_out, dy, _, dgate, dgp = loss_vjp(jnp.ones((1, 1), F32))
        dy_o[...] = dy
        dma = _dot_nt(d_out, w_ref[0:ATT_W, :])
        dmr = _dot_nt(d_out, w_ref[ATT_W:, :])
        dya_o[...], dga_o[...], dys_o[...], dr_o[...], dv_o[...], dkts_o[...], dgrw_o[...], dgnw, dgnb, drk = \
            mix_vjp((dma, dmr))
        gw = jnp.concatenate([_dot_tn(ma, d_out), _dot_tn(mr, d_out)], axis=0)
        acc = ((loss_o, jnp.broadcast_to(loss, (8, 128))), (gw_o, gw), (ggp_o, dgp), (ggnw_o, dgnw), (ggnb_o, dgnb),
               (grk_o, drk))

        @pl.when(i == 0)
        def _():
            for ref, val in acc:
                ref[...] = val

        @pl.when(i > 0)
        def _():
            for ref, val in acc:
                ref[...] += val

        @pl.when(i % tpe == 0)
        def _():
            dgate_o[0] = dgate

        @pl.when(i % tpe > 0)
        def _():
            dgate_o[0] += dgate

    row = lambda w, c=0: pl.BlockSpec((TT, w), lambda i: (i, c))
    two = pl.BlockSpec((2, TT, RWKV_W), lambda i: (0, i, 0))
    per_ex = pl.BlockSpec((1, 1, D_MODEL), lambda i: (i // tpe, 0, 0))
    sds = jax.ShapeDtypeStruct
    r512 = sds((R, RWKV_W), F32)
    return pl.pallas_call(
        body, name="out_head", grid=(R // TT,),
        in_specs=[row(D_MODEL), row(D_MODEL), per_ex, row(ATT_W), row(ATT_W), row(RWKV_W), row(RWKV_W), row(RWKV_W, 0),
                  row(RWKV_W, 2), two,
                  row(RWKV_W), _full(w_out.shape), _full((1, D_MODEL)), _full((1, RWKV_W)), _full((1, RWKV_W)),
                  _full((1, RWKV_W)), _full((256, 256))],
        out_specs=(_full((8, 128)), row(D_MODEL), row(ATT_W), row(ATT_W), row(RWKV_W), row(RWKV_W), row(RWKV_W),
                   row(RWKV_W), row(RWKV_W), per_ex, _full((D_MODEL, D_MODEL)), _full((1, D_MODEL)), _full((1, RWKV_W)),
                   _full((1, RWKV_W)), _full((1, RWKV_W))),
        out_shape=(sds((8, 128), F32), sds((R, D_MODEL), F32), r512, r512, r512, r512, r512, r512, r512,
                   sds((R // T, 1, D_MODEL), F32), sds((D_MODEL, D_MODEL), F32), sds((1, D_MODEL), F32),
                   sds((1, RWKV_W), F32), sds((1, RWKV_W), F32), sds((1, RWKV_W), F32)),
        compiler_params=_cp(("arbitrary",)),
    )(x2, tgt2, gate, y_att, g_att, y0, y1, shifted, shifted, kt, g_rw, w_out, g_post, gn_w, gn_b, r_k, bd)


def _in_proj_bwd_call(x2, dy, shift, scale, g_pre, w_in, qg, kg, cos, sin, bd, q_raw, k_raw, dqr, dkp, dvp,
                      d_gatt, d_rin, d_grw, T):
    R = x2.shape[0]
    TT = min(ROW_TILE, T)
    tpe = T // TT

    def body(x_ref, dy_ref, sh_ref, sc_ref, gp_ref, w_ref, qg_ref, kg_ref, cos_ref, sin_ref, bd_ref, q_ref, k_ref,
             dqr_ref, dkp_ref, dvp_ref, dga_ref, drin_ref, dgrw_ref,
             dx_o, dproj_o, dsh_o, dsc_o, ggp_o, gqg_o, gkg_o):
        i = pl.program_id(0)
        cos, sin, bd = cos_ref[...], sin_ref[...], bd_ref[...]
        left = lax.broadcasted_iota(jnp.int32, (1, KV_W), 1) < HEAD_DIM

        def kv_grad(ref):
            a = ref[0] + ref[1]
            b = ref[2] + ref[3]
            return jnp.where(left, a + pltpu.roll(a, HEAD_DIM, 1), b + pltpu.roll(b, HEAD_DIM, 1))

        qfn = functools.partial(_qk_fn, cos=jnp.tile(cos, (1, 4)), sin=jnp.tile(sin, (1, 4)), bd=bd, scale=ATT_SCALE,
                                diff=True)
        _, q_vjp = jax.vjp(qfn, q_ref[...], qg_ref[...])
        dq, gqg = q_vjp(dqr_ref[...])
        kfn = functools.partial(_qk_fn, cos=cos, sin=sin, bd=bd, scale=1.0, diff=True)
        _, k_vjp = jax.vjp(kfn, k_ref[...], kg_ref[...])
        dk, gkg = k_vjp(kv_grad(dkp_ref))
        pieces = ((C_Q, C_K, dq), (C_K, C_V, dk), (C_V, C_GA, kv_grad(dvp_ref)), (C_GA, C_RIN, dga_ref[...]),
                  (C_RIN, C_GRW, drin_ref[...]), (C_GRW, C_END, dgrw_ref[...]))
        dh = jnp.zeros((TT, D_MODEL), F32)
        for c0, c1, val in pieces:
            vb = val.astype(MXU_DTYPE)
            dproj_o[:, c0:c1] = vb
            dh = dh + _dot(vb, w_ref[c0:c1, :])
        _, pre_vjp = jax.vjp(_pre_fn, x_ref[...], sh_ref[0], sc_ref[0], gp_ref[...])
        dx, dsh, dsc, ggp = pre_vjp(dh)
        dx_o[...] = dx + dy_ref[...]
        acc = ((ggp_o, ggp), (gqg_o, gqg), (gkg_o, gkg))

        @pl.when(i == 0)
        def _():
            for ref, val in acc:
                ref[...] = val

        @pl.when(i > 0)
        def _():
            for ref, val in acc:
                ref[...] += val

        @pl.when(i % tpe == 0)
        def _():
            dsh_o[0] = dsh
            dsc_o[0] = dsc

        @pl.when(i % tpe > 0)
        def _():
            dsh_o[0] += dsh
            dsc_o[0] += dsc

    row = lambda w: pl.BlockSpec((TT, w), lambda i: (i, 0))
    per_ex = pl.BlockSpec((1, 1, D_MODEL), lambda i: (i // tpe, 0, 0))
    tab = pl.BlockSpec((TT, KV_W), lambda i: (i % tpe, 0))
    pad = pl.BlockSpec((4, TT, KV_W), lambda i: (0, i, 0))
    sds = jax.ShapeDtypeStruct
    nb = R // T
    return pl.pallas_call(
        body, name="in_proj_bwd", grid=(R // TT,),
        in_specs=[row(D_MODEL), row(D_MODEL), per_ex, per_ex, _full((1, D_MODEL)), _full(w_in.shape), _full((1, ATT_W)),
                  _full((1, KV_W)), tab, tab, _full((256, 256)), row(ATT_W), row(KV_W), row(ATT_W), pad, pad,
                  row(ATT_W), row(SHIFT_W), row(RWKV_W)],
        out_specs=(row(D_MODEL), row(C_END), per_ex, per_ex, _full((1, D_MODEL)), _full((1, ATT_W)), _full((1, KV_W))),
        out_shape=(sds((R, D_MODEL), F32), sds((R, C_END), MXU_DTYPE), sds((nb, 1, D_MODEL), F32),
                   sds((nb, 1, D_MODEL), F32), sds((1, D_MODEL), F32), sds((1, ATT_W), F32), sds((1, KV_W), F32)),
        compiler_params=_cp(("arbitrary",)),
    )(x2, dy, shift, scale, g_pre, w_in, qg, kg, cos, sin, bd, q_raw, k_raw, dqr, dkp, dvp, d_gatt, d_rin, d_grw)


def _w_in_grad_call(hb, dproj):
    R = hb.shape[0]
    TT = min(W_GRAD_ROWS, R)
    CB = 1152

    def body(h_ref, d_ref, o_ref):
        g = _dot_tn(h_ref[...], d_ref[...])

        @pl.when(pl.program_id(1) == 0)
        def _():
            o_ref[...] = g

        @pl.when(pl.program_id(1) > 0)
        def _():
            o_ref[...] += g

    return pl.pallas_call(
        body, name="w_in_grad", grid=(C_END // CB, R // TT),
        in_specs=[pl.BlockSpec((TT, D_MODEL), lambda j, i: (i, 0)), pl.BlockSpec((TT, CB), lambda j, i: (i, j))],
        out_specs=pl.BlockSpec((D_MODEL, CB), lambda j, i: (0, j)),
        out_shape=jax.ShapeDtypeStruct((D_MODEL, C_END), F32), compiler_params=_cp(("arbitrary", "arbitrary")),
    )(hb, dproj)


def _adam_call(parts, w, m, v, name, row_tile=None):
    P, M, N = parts.shape
    TM = M if row_tile is None else row_tile

    def body(p_ref, w_ref, m_ref, v_ref, g_o, d_o, m_o, v_o):
        g = p_ref[0].astype(F32)
        for j in range(1, P):
            g = g + p_ref[j].astype(F32)
        m2 = ADAM_B1 * m_ref[...] + (1.0 - ADAM_B1) * g
        v2 = ADAM_B2 * v_ref[...] + (1.0 - ADAM_B2) * jnp.square(g)
        m_hat = m2 / (1.0 - ADAM_B1 ** ADAM_STEP)
        v_hat = v2 / (1.0 - ADAM_B2 ** ADAM_STEP)
        g_o[...] = g
        d_o[...] = -ADAM_LR * (m_hat / (jnp.sqrt(v_hat) + ADAM_EPS) + ADAM_WD * w_ref[...])
        m_o[...] = m2
        v_o[...] = v2

    blk = pl.BlockSpec((TM, N), lambda i: (i, 0))
    return pl.pallas_call(
        body, name=name, grid=(M // TM,),
        in_specs=[pl.BlockSpec((P, TM, N), lambda i: (0, i, 0)), blk, blk, blk], out_specs=(blk,) * 4,
        out_shape=(jax.ShapeDtypeStruct((M, N), F32),) * 4, compiler_params=_cp(("arbitrary",)),
    )(parts, w, m, v)


_SMALL_ROWS = 136


def _pack_small(taps, w_up, w0, a_up, a0):
    flat = jnp.concatenate([taps.reshape(-1), w_up.reshape(-1), w0.reshape(-1), a_up.reshape(-1), a0.reshape(-1)])
    return jnp.pad(flat, (0, _SMALL_ROWS * 128 - flat.shape[0])).reshape(_SMALL_ROWS, 128)


def _unpack_small(packed):
    n = packed.shape[0]
    flat = packed.reshape(n, -1)
    out, o = [], 0
    for shape in ((3, 208), (2, 64, 64), (2, 64), (2, 64, 64), (2, 64)):
        size = 1
        for s in shape:
            size *= s
        out.append(flat[:, o:o + size].reshape((n,) + shape))
        o += size
    return out


def _cols_to_full(blocks):
    nd = blocks.ndim
    moved = jnp.moveaxis(blocks, 0, nd - 2)
    return moved.reshape(moved.shape[:-2] + (moved.shape[-2] * moved.shape[-1],))


def _full_to_cols(full):
    k = full.shape[-1] // NDEV
    return jnp.moveaxis(full.reshape(full.shape[:-1] + (NDEV, k)), -2, 0)


_REP_SIZES = (("g_pre", 1024), ("q_norm_g", 64), ("k_norm_g", 64), ("k_k", 512), ("k_a", 512), ("r_k", 512),
              ("gn_w", 512), ("gn_b", 512), ("g_post", 1024))
_REP_ROWS = 40


def kernel(x, c, w_ada, b_ada, g_pre, w_in, q_norm_g, k_norm_g, shift_taps, w_up, w0, a_up, a0, k_k, k_a, r_k, gn_w, gn_b, w_out, g_post, loss_target, m_w_ada, m_b_ada, m_g_pre, m_w_in, m_q_norm_g, m_k_norm_g, m_shift_taps, m_w_up, m_w0, m_a_up, m_a0, m_k_k, m_k_a, m_r_k, m_gn_w, m_gn_b, m_w_out, m_g_post, v_w_ada, v_b_ada, v_g_pre, v_w_in, v_q_norm_g, v_k_norm_g, v_shift_taps, v_w_up, v_w0, v_a_up, v_a0, v_k_k, v_k_a, v_r_k, v_gn_w, v_gn_b, v_w_out, v_g_post):
    B, T, _ = x.shape
    R = B * T
    me = 4 * lax.axis_index("x") + 2 * lax.axis_index("y") + lax.axis_index("c")
    x2 = x.reshape(R, D_MODEL)
    tgt2 = loss_target.reshape(R, D_MODEL)

    seg = jnp.arange(256) // HEAD_DIM
    bd = (seg[:, None] == seg[None, :]).astype(MXU_DTYPE)
    eye = (jnp.arange(HEAD_DIM)[:, None] == (jnp.arange(RWKV_W) % HEAD_DIM)[None, :])
    eye_b, eye_f = eye.astype(MXU_DTYPE), eye.astype(F32)
    cos, sin = _rope_tables(T)

    c_g, w_in_g, w_out_g, small_g = _exchange(
        [c, w_in[0].T.astype(MXU_DTYPE), w_out[0].astype(MXU_DTYPE),
         _pack_small(shift_taps[0], w_up[0], w0[0], a_up[0], a0[0])], ["all"] * 4, "gather_params")
    c_all = c_g.reshape(NDEV * B, D_MODEL)
    w_in_f = w_in_g.reshape(C_END, D_MODEL)
    w_out_f = w_out_g.reshape(D_MODEL, D_MODEL)
    taps_b, w_up_b, w0_b, a_up_b, a0_b = _unpack_small(small_g)
    taps_f = jnp.pad(_cols_to_full(taps_b), ((0, 5), (0, 0)))
    w_up_f, a_up_f = _cols_to_full(w_up_b), _cols_to_full(a_up_b)
    w0_f, a0_f = _cols_to_full(w0_b), _cols_to_full(a0_b)
    wup_pad = jnp.pad(w_up_f, ((0, 0), (0, 64), (0, 0))).astype(MXU_DTYPE)
    aup_pad = jnp.pad(a_up_f, ((0, 0), (64, 0), (0, 0))).astype(MXU_DTYPE)

    ncol = w_ada.shape[2]
    b_cols = lax.dynamic_slice(b_ada, (0, me * ncol), (1, ncol))
    mod_cols = _mod_call(c_all, w_ada[0].astype(MXU_DTYPE), b_cols)
    (mod_g,) = _exchange([mod_cols], ["all"], "gather_mod")
    mod = lax.dynamic_slice(_cols_to_full(mod_g), (me * B, 0), (B, 3 * D_MODEL))
    shift, scale, gate = [mod[:, j * D_MODEL:(j + 1) * D_MODEL].reshape(B, 1, D_MODEL) for j in range(3)]

    qg = jnp.tile(q_norm_g, (1, ATT_W // HEAD_DIM))
    kg = jnp.tile(k_norm_g, (1, KV_W // HEAD_DIM))
    rk_row = r_k.reshape(1, RWKV_W)

    hb, qr, kpad, vpad, q_raw, k_raw, g_att, rin, g_rw = _in_proj_call(
        x2, shift, scale, g_pre, w_in_f, qg, kg, cos, sin, bd, T)
    y_att = _att_fwd_call(qr, kpad, vpad, B, T)
    shifted = _shift_fwd_call(rin, taps_f, T)
    w_s, kt_s, akk_s, kk_s = _rwkv_prep_call(shifted, wup_pad, aup_pad, w0_f, a0_f, k_k, k_a, bd, T)
    sh3 = shifted.reshape(B, T, SHIFT_W)
    r4 = lambda a: a.reshape(2, B, T, RWKV_W)
    y0, y1, st = _scan_fwd_call(r4(w_s), r4(kt_s), r4(akk_s), kk_s.reshape(B, T, RWKV_W), sh3, eye_b, eye_f, bd, B, T)

    (loss_blk, dy, d_yatt, d_gatt, d_ys, d_r2, d_v2, d_kts, d_grw, d_gate, g_wout, g_gpost, g_gnw, g_gnb,
     g_rk) = _out_head_call(x2, tgt2, gate, y_att, g_att, y0.reshape(R, RWKV_W), y1.reshape(R, RWKV_W), shifted, kt_s,
                            g_rw, w_out_f, g_post, gn_w, gn_b, rk_row, bd, T)
    v_heads = jnp.pad(sh3[:, :, 2 * RWKV_W:3 * RWKV_W].reshape(B, T, RWKV_W // HEAD_DIM, HEAD_DIM),
                      ((0, 0), (0, 0), (0, 0), (0, HEAD_DIM)))
    scan_cts = _scan_bwd_call(r4(w_s), r4(kt_s), r4(akk_s), kk_s.reshape(B, T, RWKV_W), sh3, v_heads,
                              d_ys.reshape(B, T, RWKV_W), st, eye_b, eye_f, bd, B, T)
    scan_cts = [a.reshape(R, RWKV_W) for a in scan_cts]
    d_shifted, g_wup, g_aup, g_w0, g_a0, g_kk, g_ka = _rwkv_prep_bwd_call(
        shifted, scan_cts + [d_r2, d_v2, d_kts], wup_pad, aup_pad, w0_f, a0_f, k_k, k_a, bd, T)
    d_rin, g_taps = _shift_bwd_call(rin, d_shifted, taps_f, T)
    dqr, dkp, dvp = _att_bwd_call(qr, kpad, vpad, d_yatt, B, T)
    grad_x, dproj, d_shift, d_scale, g_gpre, g_qg, g_kg = _in_proj_bwd_call(
        x2, dy, shift, scale, g_pre, w_in_f, qg, kg, cos, sin, bd, q_raw, k_raw, dqr, dkp, dvp, d_gatt, d_rin, d_grw, T)
    g_win = _w_in_grad_call(hb, dproj)

    rep = jnp.concatenate([g_gpre.reshape(-1), g_qg.reshape(-1, HEAD_DIM).sum(0), g_kg.reshape(-1, HEAD_DIM).sum(0),
                           g_kk.reshape(-1), g_ka.reshape(-1), g_rk.reshape(-1), g_gnw.reshape(-1), g_gnb.reshape(-1),
                           g_gpost.reshape(-1), loss_blk[0, :1]])
    rep = jnp.pad(rep, (0, _REP_ROWS * 128 - rep.shape[0])).reshape(_REP_ROWS, 128)
    dmod = jnp.concatenate([d_shift, d_scale, d_gate], axis=2).reshape(B, 3 * D_MODEL)
    small_parts = jax.vmap(_pack_small)(_full_to_cols(g_taps[:3]), _full_to_cols(g_wup[:, :64, :]), _full_to_cols(g_w0),
                                        _full_to_cols(g_aup[:, 64:, :]), _full_to_cols(g_a0))
    core = lax.axis_index("c")
    halves = [a.reshape((NDEV // 2, 2) + a.shape[1:]).astype(MXU_DTYPE)
              for a in (_full_to_cols(g_win), g_wout.reshape(NDEV, D_MODEL // NDEV, D_MODEL))]
    pick = lambda a, j: lax.dynamic_index_in_dim(a, j, axis=1, keepdims=False)
    s_win, s_wout = _pair_sum_call([pick(a, core) for a in halves], [pick(a, 1 - core) for a in halves], "reduce_pair")
    p_win, p_wout, p_small, dmod_g, rep_g = _exchange(
        [s_win, s_wout, small_parts, dmod, rep], ["chips", "chips", "scatter", "all", "all"], "reduce_grads")
    dmod_all = dmod_g.reshape(NDEV * B, 3 * D_MODEL)
    g_wada = _wada_grad_call(c_all, lax.dynamic_slice(dmod_all, (0, me * ncol), (NDEV * B, ncol)))

    res = {}

    def adam(name, parts, w, m, v, row_tile=None):
        shape = w.shape
        two_d = (-1, shape[-1])
        out = _adam_call(parts.reshape((parts.shape[0],) + w.reshape(two_d).shape), w.reshape(two_d), m.reshape(two_d),
                         v.reshape(two_d), "adam_" + name, row_tile)
        res[name] = [o.reshape(shape) for o in out]

    adam("w_ada", g_wada[None], w_ada, m_w_ada, v_w_ada)
    adam("b_ada", dmod_all.reshape(NDEV * B, 1, 3 * D_MODEL), b_ada, m_b_ada, v_b_ada)
    adam("w_in", p_win, w_in, m_w_in, v_w_in, 128)
    adam("w_out", p_wout, w_out, m_w_out, v_w_out)
    taps_p, wup_p, w0_p, aup_p, a0_p = _unpack_small(p_small)
    adam("shift_taps", taps_p, shift_taps, m_shift_taps, v_shift_taps)
    adam("w_up", wup_p, w_up, m_w_up, v_w_up)
    adam("w0", w0_p, w0, m_w0, v_w0)
    adam("a_up", aup_p, a_up, m_a_up, v_a_up)
    adam("a0", a0_p, a0, m_a0, v_a0)
    rep_flat = rep_g.reshape(NDEV, -1)
    off = 0
    given = dict(g_pre=(g_pre, m_g_pre, v_g_pre), q_norm_g=(q_norm_g, m_q_norm_g, v_q_norm_g),
                 k_norm_g=(k_norm_g, m_k_norm_g, v_k_norm_g), k_k=(k_k, m_k_k, v_k_k), k_a=(k_a, m_k_a, v_k_a),
                 r_k=(r_k, m_r_k, v_r_k), gn_w=(gn_w, m_gn_w, v_gn_w), gn_b=(gn_b, m_gn_b, v_gn_b),
                 g_post=(g_post, m_g_post, v_g_post))
    for name, size in _REP_SIZES:
        adam(name, rep_flat[:, off:off + size], *given[name])
        off += size

    loss = jnp.sum(rep_flat[:, off])
    order = ["w_ada", "b_ada", "g_pre", "w_in", "q_norm_g", "k_norm_g", "shift_taps", "w_up", "w0", "a_up", "a0", "k_k",
             "k_a", "r_k", "gn_w", "gn_b", "w_out", "g_post"]
    return (loss, grad_x.reshape(B, T, D_MODEL), *[res[n][0] for n in order], *[res[n][1] for n in order],
            *[res[n][2] for n in order], *[res[n][3] for n in order])
```

```python
import functools

import jax
import jax.numpy as jnp
from jax import lax
from jax.experimental import pallas as pl
from jax.experimental.pallas import tpu as pltpu

F32 = jnp.float32
MXU_DTYPE = jnp.bfloat16
MESH = pl.DeviceIdType.MESH
NDEV = 8

D_MODEL = 1024
HEAD_DIM = 64
ATT_W = 512
KV_W = 128
RWKV_W = 512
LORA_W = 128
SHIFT_W = 3 * RWKV_W + LORA_W
GRID_W = 64
ROPE_THETA = 10000.0
DECAY_SCALE = 0.6065306597126334
NORM_EPS = 1e-6
GN_EPS = 64e-5
L2_EPS = 1e-12
ATT_SCALE = HEAD_DIM ** -0.5
C_Q, C_K, C_V, C_GA, C_RIN, C_GRW, C_END = 0, 512, 640, 768, 1280, 2944, 3456

ADAM_LR, ADAM_B1, ADAM_B2, ADAM_EPS, ADAM_WD, ADAM_STEP = 0.001, 0.9, 0.999, 1e-08, 0.01, 10

ROW_TILE = 256
W_GRAD_ROWS = 2048
ATT_TILE_FWD = 256
ATT_TILE_BWD = 512
SCAN_CHUNK = 64
SCAN_UNROLL = 16
VMEM_LIMIT = 56 * 1024 * 1024


def _cp(sem=None):
    return pltpu.CompilerParams(dimension_semantics=sem, vmem_limit_bytes=VMEM_LIMIT)


def _dot(a, b, dims=(((1,), (0,)), ((), ()))):
    return lax.dot_general(a.astype(MXU_DTYPE), b.astype(MXU_DTYPE), dims, preferred_element_type=F32)


def _dot_nt(a, b):
    return _dot(a, b, (((1,), (1,)), ((), ())))


def _dot_tn(a, b):
    return _dot(a, b, (((0,), (0,)), ((), ())))


def _seg_dot(xb, bd):
    n = xb.shape[1]
    if n <= 256:
        return jnp.dot(xb, bd[:n, :n], preferred_element_type=F32)
    parts = [jnp.dot(xb[:, c:c + 256], bd, preferred_element_type=F32) for c in range(0, n, 256)]
    return jnp.concatenate(parts, axis=1)


def _segsum_raw(x, bd):
    rows = x.shape[0]
    hi = x.astype(MXU_DTYPE)
    lo = (x - hi.astype(F32)).astype(MXU_DTYPE)
    both = _seg_dot(jnp.concatenate([hi, lo], axis=0), bd)
    return both[:rows] + both[rows:]


@jax.custom_vjp
def _segsum_d(x, bd):
    return _segsum_raw(x, bd)


def _segsum_d_fwd(x, bd):
    return _segsum_raw(x, bd), bd


def _segsum_d_bwd(bd, ct):
    return _segsum_raw(ct, bd), jnp.zeros_like(bd)


_segsum_d.defvjp(_segsum_d_fwd, _segsum_d_bwd)


def _rope_tables(T):
    t = jnp.arange(T, dtype=F32)
    row = jnp.floor(t / GRID_W)
    col = t - row * GRID_W
    n_freq = HEAD_DIM // 4
    inv_freq = ROPE_THETA ** (-jnp.arange(n_freq, dtype=F32) / n_freq)
    d = jnp.arange(HEAD_DIM)
    pos = jnp.where((d < HEAD_DIM // 2)[None, :], row[:, None], col[:, None])
    ang = pos * inv_freq[d % n_freq][None, :]
    sign = jnp.where((d % 32) < 16, -1.0, 1.0).astype(F32)[None, :]
    cos = jnp.cos(ang)
    sin = jnp.sin(ang) * sign
    return jnp.tile(cos, (1, 2)), jnp.tile(sin, (1, 2))


def _rope_raw(x, cos, sin):
    n = x.shape[1]
    lane = lax.broadcasted_iota(jnp.int32, (1, n), 1)
    first = (lane % 32) < 16
    partner = jnp.where(first, pltpu.roll(x, n - 16, 1), pltpu.roll(x, 16, 1))
    return x * cos + partner * sin


@jax.custom_vjp
def _rope_d(x, cos, sin):
    return _rope_raw(x, cos, sin)


def _rope_d_fwd(x, cos, sin):
    return _rope_raw(x, cos, sin), (cos, sin)


def _rope_d_bwd(res, ct):
    cos, sin = res
    return _rope_raw(ct, cos, -sin), jnp.zeros_like(cos), jnp.zeros_like(sin)


_rope_d.defvjp(_rope_d_fwd, _rope_d_bwd)


def _rms(x, g):
    return x * lax.rsqrt(jnp.mean(x * x, axis=-1, keepdims=True) + NORM_EPS) * g


def _pre_fn(x, shift, scale, g_pre):
    return _rms(x, g_pre) * (1.0 + scale) + shift


def _qk_fn(q, g, cos, sin, bd, scale, diff):
    segsum = _segsum_d if diff else _segsum_raw
    rope = _rope_d if diff else _rope_raw
    qn = q * lax.rsqrt(segsum(q * q, bd) * (1.0 / HEAD_DIM) + NORM_EPS) * g
    return rope(qn, cos, sin) * scale


def _silu(x):
    return x * jax.nn.sigmoid(x)


def _rwkv_pw(k, pw0, pw1, pa0, pa1, w0, a0, k_k, k_a, bd, diff):
    segsum = _segsum_d if diff else _segsum_raw
    kk = k * k_k
    kk = kk * lax.rsqrt(segsum(kk * kk, bd) + L2_EPS)
    ws, kts, akks = [], [], []
    for z, (pw, pa) in enumerate(((pw0, pa0), (pw1, pa1))):
        w = jnp.exp(-DECAY_SCALE * jax.nn.sigmoid(w0[z:z + 1, :] + pw))
        a = jax.nn.sigmoid(a0[z:z + 1, :] + pa)
        ws.append(w)
        kts.append(k * (1.0 + (a - 1.0) * k_a))
        akks.append(a * kk)
    return ws[0], ws[1], kts[0], kts[1], akks[0], akks[1], kk


def _mix_fn(y_att, g_att, ys, r, v, kts, g_rw, gn_w, gn_b, r_k, bd, diff):
    segsum = _segsum_d if diff else _segsum_raw
    mu = segsum(ys, bd) * (1.0 / HEAD_DIM)
    d = ys - mu
    var = segsum(d * d, bd) * (1.0 / HEAD_DIM)
    yn = d * lax.rsqrt(var + GN_EPS) * gn_w + gn_b
    bonus = segsum(r * kts * r_k, bd) * v
    return y_att * _silu(g_att), (yn + bonus) * _silu(g_rw)


def _loss_fn(out, x, tgt, gate, g_post):
    e = x + gate * _rms(out, g_post) - tgt
    s = jnp.sum(e * e, axis=1, keepdims=True)
    return jnp.sum(s, axis=0, keepdims=True) * (0.5 / D_MODEL)


def _exchange(arrays, modes, name):
    n = len(arrays)
    out_shape = tuple(
        jax.ShapeDtypeStruct(((NDEV,) + tuple(a.shape)) if mode == "all" else tuple(a.shape), a.dtype)
        for a, mode in zip(arrays, modes))
    chips = (4, 2, 6)

    def body(*refs):
        ins, outs = refs[:n], refs[n:2 * n]
        send_sems, recv_sems, local_sems = refs[2 * n:]
        ix, iy, ic = lax.axis_index("x"), lax.axis_index("y"), lax.axis_index("c")
        me = 4 * ix + 2 * iy + ic

        def peer(m):
            px = 1 - ix if (m >> 2) & 1 else ix
            py = 1 - iy if (m >> 1) & 1 else iy
            pc = 1 - ic if m & 1 else ic
            return (px, py, pc), 4 * px + 2 * py + pc

        def copy(k, j, src_ref, slot, to):
            return pltpu.make_async_remote_copy(src_ref=src_ref, dst_ref=outs[k].at[slot], send_sem=send_sems.at[k, j],
                                                recv_sem=recv_sems.at[k, j], device_id=to, device_id_type=MESH)

        local, sends, arrivals, forwards = [], [], [], []
        for k in range(n):
            if modes[k] == "scatter":
                local.append(pltpu.make_async_copy(ins[k].at[me], outs[k].at[me], local_sems.at[k]))
                for m in range(1, NDEV):
                    to, p = peer(m)
                    sends.append(copy(k, m - 1, ins[k].at[p], me, to))
                    arrivals.append(copy(k, m - 1, ins[k].at[p], p, to))
            elif modes[k] == "chips":
                mine = me // 2
                local.append(pltpu.make_async_copy(ins[k].at[mine], outs[k].at[mine], local_sems.at[k]))
                for j, m in enumerate(chips):
                    to, p = peer(m)
                    sends.append(copy(k, j, ins[k].at[p // 2], mine, to))
                    arrivals.append(copy(k, j, ins[k].at[p // 2], p // 2, to))
            else:
                local.append(pltpu.make_async_copy(ins[k], outs[k].at[me], local_sems.at[k]))
                sib, sib_slot = peer(1)
                sends.append(copy(k, 0, ins[k], me, sib))
                for j, m in enumerate(chips):
                    to, p = peer(m)
                    sends.append(copy(k, 1 + j, ins[k], me, to))
                    forwards.append((copy(k, 1 + j, ins[k], p, to), copy(k, 4 + j, outs[k].at[p], p, sib)))
                    arrivals.append(copy(k, 4 + j, ins[k], peer(m ^ 1)[1], sib))
                arrivals.append(copy(k, 0, ins[k], sib_slot, sib))
        for cp in local + sends:
            cp.start()
        for arrived, onward in forwards:
            arrived.wait_recv()
            onward.start()
        for cp in arrivals:
            cp.wait_recv()
        for cp in sends + [onward for _, onward in forwards]:
            cp.wait_send()
        for cp in local:
            cp.wait()

    any_spec = pl.BlockSpec(memory_space=pl.ANY)
    return pl.pallas_call(
        body, name=name, out_shape=out_shape,
        in_specs=[any_spec] * n, out_specs=tuple([any_spec] * n),
        scratch_shapes=[pltpu.SemaphoreType.DMA((n, NDEV - 1)), pltpu.SemaphoreType.DMA((n, NDEV - 1)),
                        pltpu.SemaphoreType.DMA((n,))],
    )(*arrays)


def _pair_sum_call(mine, send, name):
    n = len(mine)

    def body(*refs):
        mine_r, send_r, out_r, land_r = (refs[j * n:(j + 1) * n] for j in range(4))
        send_sems, recv_sems = refs[4 * n:]
        sibling = (lax.axis_index("x"), lax.axis_index("y"), 1 - lax.axis_index("c"))
        swaps = [pltpu.make_async_remote_copy(src_ref=send_r[k], dst_ref=land_r[k], send_sem=send_sems.at[k],
                                              recv_sem=recv_sems.at[k], device_id=sibling, device_id_type=MESH)
                 for k in range(n)]
        for cp in swaps:
            cp.start()
        for k, cp in enumerate(swaps):
            cp.wait()
            out_r[k][...] = (mine_r[k][...].astype(F32) + land_r[k][...].astype(F32)).astype(out_r[k].dtype)

    return pl.pallas_call(
        body, name=name, out_shape=tuple(jax.ShapeDtypeStruct(a.shape, a.dtype) for a in mine),
        scratch_shapes=[pltpu.VMEM(a.shape, a.dtype) for a in mine] + [pltpu.SemaphoreType.DMA((n,)),
                                                                         pltpu.SemaphoreType.DMA((n,))],
        compiler_params=pltpu.CompilerParams(vmem_limit_bytes=VMEM_LIMIT),
    )(*mine, *send)


def _mod_call(c_all, w_ada, b_cols):
    def body(c_ref, w_ref, b_ref, o_ref):
        o_ref[...] = _dot(_silu(c_ref[...]), w_ref[...]) + b_ref[...]

    return pl.pallas_call(body, name="mod_fwd",
                          out_shape=jax.ShapeDtypeStruct((c_all.shape[0], w_ada.shape[1]), F32))(c_all, w_ada, b_cols)


def _wada_grad_call(c_all, dmod_cols):
    def body(c_ref, d_ref, o_ref):
        o_ref[...] = _dot_tn(_silu(c_ref[...]), d_ref[...])

    return pl.pallas_call(body, name="w_ada_grad",
                          out_shape=jax.ShapeDtypeStruct((c_all.shape[1], dmod_cols.shape[1]), F32))(c_all, dmod_cols)


def _full(shape):
    nd = len(shape)
    return pl.BlockSpec(shape, lambda *_: (0,) * nd)


def _in_proj_call(x2, shift, scale, g_pre, w_in, qg, kg, cos, sin, bd, T):
    R = x2.shape[0]
    TT = min(ROW_TILE, T)
    tpe = T // TT

    def body(x_ref, sh_ref, sc_ref, gp_ref, w_ref, qg_ref, kg_ref, cos_ref, sin_ref, bd_ref,
             hb_ref, qr_ref, kpad_ref, vpad_ref, qraw_ref, kraw_ref, gatt_ref, rin_ref, grw_ref):
        h = _pre_fn(x_ref[...], sh_ref[0], sc_ref[0], gp_ref[...])
        hb = h.astype(MXU_DTYPE)
        hb_ref[...] = hb

        def proj(c0, c1):
            return _dot_nt(hb, w_ref[c0:c1, :])

        q = proj(C_Q, C_K)
        k = proj(C_K, C_V)
        v = proj(C_V, C_GA)
        gatt_ref[...] = proj(C_GA, C_RIN)
        rin_ref[...] = proj(C_RIN, C_GRW)
        grw_ref[...] = proj(C_GRW, C_END)
        qraw_ref[...] = q
        kraw_ref[...] = k
        cos, sin, bd = cos_ref[...], sin_ref[...], bd_ref[...]
        qr = _qk_fn(q, qg_ref[...], jnp.tile(cos, (1, 4)), jnp.tile(sin, (1, 4)), bd, ATT_SCALE, False)
        qr_ref[...] = qr.astype(MXU_DTYPE)
        kr = _qk_fn(k, kg_ref[...], cos, sin, bd, 1.0, False)
        left = lax.broadcasted_iota(jnp.int32, (1, KV_W), 1) < HEAD_DIM
        for ref, val in ((kpad_ref, kr), (vpad_ref, v)):
            h0l = jnp.where(left, val, 0.0)
            h1r = jnp.where(left, 0.0, val)
            ref[0] = h0l.astype(MXU_DTYPE)
            ref[1] = pltpu.roll(h0l, HEAD_DIM, 1).astype(MXU_DTYPE)
            ref[2] = pltpu.roll(h1r, HEAD_DIM, 1).astype(MXU_DTYPE)
            ref[3] = h1r.astype(MXU_DTYPE)

    row = lambda w: pl.BlockSpec((TT, w), lambda i: (i, 0))
    per_ex = pl.BlockSpec((1, 1, D_MODEL), lambda i: (i // tpe, 0, 0))
    tab = pl.BlockSpec((TT, KV_W), lambda i: (i % tpe, 0))
    pad = pl.BlockSpec((4, TT, KV_W), lambda i: (0, i, 0))
    sds = jax.ShapeDtypeStruct
    return pl.pallas_call(
        body, name="in_proj", grid=(R // TT,),
        in_specs=[row(D_MODEL), per_ex, per_ex, _full((1, D_MODEL)), _full(w_in.shape), _full((1, ATT_W)),
                  _full((1, KV_W)), tab, tab, _full((256, 256))],
        out_specs=(row(D_MODEL), row(ATT_W), pad, pad, row(ATT_W), row(KV_W), row(ATT_W), row(SHIFT_W), row(RWKV_W)),
        out_shape=(sds((R, D_MODEL), MXU_DTYPE), sds((R, ATT_W), MXU_DTYPE), sds((4, R, KV_W), MXU_DTYPE),
                   sds((4, R, KV_W), MXU_DTYPE), sds((R, ATT_W), F32), sds((R, KV_W), F32), sds((R, ATT_W), F32),
                   sds((R, SHIFT_W), F32), sds((R, RWKV_W), F32)),
        compiler_params=_cp(("arbitrary",)),
    )(x2, shift, scale, g_pre, w_in, qg, kg, cos, sin, bd)


def _softmax_parts(s):
    e = jnp.exp(s - jnp.max(s, axis=1, keepdims=True))
    return e, 1.0 / jnp.sum(e, axis=1, keepdims=True)


def _att_specs(T, TQ):
    nq = T // TQ
    qspec = pl.BlockSpec((TQ, KV_W), lambda b, p, i: (b * nq + i, p))
    side = lambda s: pl.BlockSpec((None, T, KV_W), lambda b, p, i: (2 * (p // 2) + s, b, 0))
    return nq, qspec, side


def _att_fwd_call(qr, kpad, vpad, B, T):
    TQ = min(ATT_TILE_FWD, T)
    nq, qspec, side = _att_specs(T, TQ)

    def body(q_ref, kl_ref, kr_ref, vl_ref, vr_ref, o_ref):
        q = q_ref[...]
        ea, inv_a = _softmax_parts(_dot_nt(q, kl_ref[...]))
        eb, inv_b = _softmax_parts(_dot_nt(q, kr_ref[...]))
        o_ref[...] = _dot(ea, vl_ref[...]) * inv_a + _dot(eb, vr_ref[...]) * inv_b

    return pl.pallas_call(
        body, name="att_fwd", grid=(B, 4, nq),
        in_specs=[qspec, side(0), side(1), side(0), side(1)], out_specs=qspec,
        out_shape=jax.ShapeDtypeStruct((B * T, ATT_W), F32),
        compiler_params=_cp(("arbitrary",) * 3),
    )(qr, kpad, kpad, vpad, vpad)


def _att_bwd_call(qr, kpad, vpad, d_o, B, T):
    TQ = min(ATT_TILE_BWD, T)
    nq, qspec, side = _att_specs(T, TQ)

    def body(q_ref, kl_ref, kr_ref, vl_ref, vr_ref, do_ref, dq_ref, dk_ref, dv_ref):
        i = pl.program_id(2)
        q, do = q_ref[...], do_ref[...]
        left = lax.broadcasted_iota(jnp.int32, (1, KV_W), 1) < HEAD_DIM
        dq = jnp.zeros((TQ, KV_W), F32)
        dk = jnp.zeros((T, KV_W), F32)
        dv = jnp.zeros((T, KV_W), F32)
        for k_ref, v_ref, mask in ((kl_ref, vl_ref, left), (kr_ref, vr_ref, jnp.logical_not(left))):
            kk, vv = k_ref[...], v_ref[...]
            e, inv = _softmax_parts(_dot_nt(q, kk))
            dp = _dot_nt(do, vv)
            ds = e * (dp - inv * jnp.sum(e * dp, axis=1, keepdims=True))
            dq = dq + _dot(ds, kk) * inv
            dk = dk + _dot_tn(ds, jnp.where(mask, q * inv, 0.0))
            dv = dv + _dot_tn(e, jnp.where(mask, do * inv, 0.0))
        dq_ref[...] = dq

        @pl.when(i == 0)
        def _():
            dk_ref[...] = dk
            dv_ref[...] = dv

        @pl.when(i > 0)
        def _():
            dk_ref[...] += dk
            dv_ref[...] += dv

    acc = pl.BlockSpec((None, T, KV_W), lambda b, p, i: (p, b, 0))
    sds = jax.ShapeDtypeStruct
    return pl.pallas_call(
        body, name="att_bwd", grid=(B, 4, nq),
        in_specs=[qspec, side(0), side(1), side(0), side(1), qspec], out_specs=(qspec, acc, acc),
        out_shape=(sds((B * T, ATT_W), F32), sds((4, B * T, KV_W), F32), sds((4, B * T, KV_W), F32)),
        compiler_params=_cp(("arbitrary",) * 3),
    )(qr, kpad, kpad, vpad, vpad, d_o)


def _shift_specs(R, T, TT, width):
    tpe = T // TT
    nb8 = R // 8
    cur = pl.BlockSpec((TT, width), lambda i: (i, 0))
    prev = pl.BlockSpec((8, width), lambda i: (jnp.maximum(i * (TT // 8) - 1, 0), 0))
    nxt = pl.BlockSpec((8, width), lambda i: (jnp.minimum((i + 1) * (TT // 8), nb8 - 1), 0))
    return tpe, cur, prev, nxt


def _neighbours(cur, prev8, next8, i, tpe, TT):
    rows = lax.broadcasted_iota(jnp.int32, (TT, 1), 0)
    first = jnp.where(i % tpe == 0, 0.0, 1.0)
    last = jnp.where(i % tpe == tpe - 1, 0.0, 1.0)
    before = jnp.where(rows == 0, prev8[7:8, :] * first, pltpu.roll(cur, 1, 0))
    after = jnp.where(rows == TT - 1, next8[0:1, :] * last, pltpu.roll(cur, TT - 1, 0))
    return before, after


def _shift_fwd_call(x, taps, T):
    R, width = x.shape
    TT = min(ROW_TILE, T)
    tpe, cur, prev, nxt = _shift_specs(R, T, TT, width)

    def body(x_ref, p_ref, n_ref, t_ref, o_ref):
        xc = x_ref[...]
        before, after = _neighbours(xc, p_ref[...], n_ref[...], pl.program_id(0), tpe, TT)
        o_ref[...] = t_ref[0:1, :] * before + t_ref[1:2, :] * xc + t_ref[2:3, :] * after

    return pl.pallas_call(
        body, name="shift_fwd", grid=(R // TT,), in_specs=[cur, prev, nxt, _full(taps.shape)], out_specs=cur,
        out_shape=jax.ShapeDtypeStruct((R, width), F32), compiler_params=_cp(("arbitrary",)),
    )(x, x, x, taps)


def _shift_bwd_call(x, d, taps, T):
    R, width = x.shape
    TT = min(ROW_TILE, T)
    tpe, cur, prev, nxt = _shift_specs(R, T, TT, width)

    def body(x_ref, xp_ref, xn_ref, d_ref, dp_ref, dn_ref, t_ref, dx_ref, dt_ref):
        i = pl.program_id(0)
        xc, dc = x_ref[...], d_ref[...]
        d_before, d_after = _neighbours(dc, dp_ref[...], dn_ref[...], i, tpe, TT)
        dx_ref[...] = t_ref[2:3, :] * d_before + t_ref[1:2, :] * dc + t_ref[0:1, :] * d_after
        x_before, x_after = _neighbours(xc, xp_ref[...], xn_ref[...], i, tpe, TT)
        @pl.when(i == 0)
        def _():
            dt_ref[...] = jnp.zeros_like(dt_ref)

        for j, xs in enumerate((x_before, xc, x_after)):
            dt_ref[j:j + 1, :] += jnp.sum(dc * xs, axis=0, keepdims=True)

    return pl.pallas_call(
        body, name="shift_bwd", grid=(R // TT,),
        in_specs=[cur, prev, nxt, cur, prev, nxt, _full(taps.shape)], out_specs=(cur, _full((8, width))),
        out_shape=(jax.ShapeDtypeStruct((R, width), F32), jax.ShapeDtypeStruct((8, width), F32)),
        compiler_params=_cp(("arbitrary",)),
    )(x, x, x, d, d, d, taps)


def _lora_in(wa):
    lane = lax.broadcasted_iota(jnp.int32, (1, LORA_W), 1)
    return jnp.where(lane < LORA_W // 2, jnp.tanh(wa), wa)


def _rwkv_prep_call(shifted, wup, aup, w0, a0, k_k, k_a, bd, T):
    R = shifted.shape[0]
    TT = min(ROW_TILE, T)

    def body(k_ref, wa_ref, wup_ref, aup_ref, w0_ref, a0_ref, kk_ref, ka_ref, bd_ref, w_o, kt_o, akk_o, kk_o):
        twa = _lora_in(wa_ref[...])
        pre = [_dot(twa, m_ref[z]) for m_ref in (wup_ref, aup_ref) for z in range(2)]
        outs = _rwkv_pw(k_ref[...], pre[0], pre[1], pre[2], pre[3], w0_ref[...], a0_ref[...], kk_ref[...],
                        ka_ref[...], bd_ref[...], False)
        w_o[0], w_o[1], kt_o[0], kt_o[1], akk_o[0], akk_o[1] = outs[:6]
        kk_o[...] = outs[6]

    col = lambda c, w: pl.BlockSpec((TT, w), lambda i: (i, c))
    two = pl.BlockSpec((2, TT, RWKV_W), lambda i: (0, i, 0))
    sds = jax.ShapeDtypeStruct
    return pl.pallas_call(
        body, name="rwkv_prep", grid=(R // TT,),
        in_specs=[col(1, RWKV_W), col(3 * RWKV_W // LORA_W, LORA_W), _full(wup.shape), _full(aup.shape),
                  _full((2, RWKV_W)), _full((2, RWKV_W)), _full((1, RWKV_W)), _full((1, RWKV_W)), _full((256, 256))],
        out_specs=(two, two, two, col(0, RWKV_W)),
        out_shape=(sds((2, R, RWKV_W), F32),) * 3 + (sds((R, RWKV_W), F32),),
        compiler_params=_cp(("arbitrary",)),
    )(shifted, shifted, wup, aup, w0, a0, k_k, k_a, bd)


def _rwkv_prep_bwd_call(shifted, cts, wup, aup, w0, a0, k_k, k_a, bd, T):
    R = shifted.shape[0]
    TT = min(ROW_TILE, T)

    def body(k_ref, wa_ref, dw0, dkt0, dakk0, dkk0, dr0, dv0, dw1, dkt1, dakk1, dkk1, dr1, dv1, dr2_ref, dv2_ref, dkts_ref,
             wup_ref, aup_ref, w0_ref, a0_ref, kk_ref, ka_ref, bd_ref,
             dsh_ref, gwup_ref, gaup_ref, gw0_ref, ga0_ref, gkk_ref, gka_ref):
        dw_ref, dkt_ref, dakk_ref, dkk_ref, dr_ref, dv_ref = ((dw0, dw1), (dkt0, dkt1), (dakk0, dakk1), (dkk0, dkk1),
                                                              (dr0, dr1), (dv0, dv1))
        i = pl.program_id(0)
        wa = wa_ref[...]
        twa = _lora_in(wa)
        pre = [_dot(twa, m_ref[z]) for m_ref in (wup_ref, aup_ref) for z in range(2)]
        fn = functools.partial(_rwkv_pw, bd=bd_ref[...], diff=True)
        _, vjp = jax.vjp(fn, k_ref[...], pre[0], pre[1], pre[2], pre[3], w0_ref[...], a0_ref[...], kk_ref[...],
                         ka_ref[...])
        dkts = dkts_ref[...]
        dk, dpw0, dpw1, dpa0, dpa1, gw0, ga0, gkk, gka = vjp(
            (dw_ref[0][...], dw_ref[1][...], dkt_ref[0][...] + dkts, dkt_ref[1][...] + dkts, dakk_ref[0][...],
             dakk_ref[1][...], dkk_ref[0][...] + dkk_ref[1][...]))
        dtwa = (_dot_nt(dpw0, wup_ref[0]) + _dot_nt(dpw1, wup_ref[1]) + _dot_nt(dpa0, aup_ref[0])
                + _dot_nt(dpa1, aup_ref[1]))
        lane = lax.broadcasted_iota(jnp.int32, (1, LORA_W), 1)
        dsh_ref[:, 0:RWKV_W] = dr_ref[0][...] + dr_ref[1][...] + dr2_ref[...]
        dsh_ref[:, RWKV_W:2 * RWKV_W] = dk
        dsh_ref[:, 2 * RWKV_W:3 * RWKV_W] = dv_ref[0][...] + dv_ref[1][...] + dv2_ref[...]
        dsh_ref[:, 3 * RWKV_W:] = jnp.where(lane < LORA_W // 2, dtwa * (1.0 - twa * twa), dtwa)
        acc = ((gwup_ref.at[0], _dot_tn(twa, dpw0)), (gwup_ref.at[1], _dot_tn(twa, dpw1)),
               (gaup_ref.at[0], _dot_tn(twa, dpa0)), (gaup_ref.at[1], _dot_tn(twa, dpa1)),
               (gw0_ref, gw0), (ga0_ref, ga0), (gkk_ref, gkk), (gka_ref, gka))

        @pl.when(i == 0)
        def _():
            for ref, val in acc:
                ref[...] = val

        @pl.when(i > 0)
        def _():
            for ref, val in acc:
                ref[...] += val

    col = lambda c, w: pl.BlockSpec((TT, w), lambda i: (i, c))
    one = col(0, RWKV_W)
    sds = jax.ShapeDtypeStruct
    return pl.pallas_call(
        body, name="rwkv_prep_bwd", grid=(R // TT,),
        in_specs=[col(1, RWKV_W), col(3 * RWKV_W // LORA_W, LORA_W)] + [one] * 15 + [
                  _full(wup.shape), _full(aup.shape), _full((2, RWKV_W)), _full((2, RWKV_W)), _full((1, RWKV_W)),
                  _full((1, RWKV_W)), _full((256, 256))],
        out_specs=(pl.BlockSpec((TT, SHIFT_W), lambda i: (i, 0)), _full(wup.shape), _full(aup.shape),
                   _full((2, RWKV_W)), _full((2, RWKV_W)), _full((1, RWKV_W)), _full((1, RWKV_W))),
        out_shape=(sds((R, SHIFT_W), F32), sds(wup.shape, F32), sds(aup.shape, F32), sds((2, RWKV_W), F32),
                   sds((2, RWKV_W), F32), sds((1, RWKV_W), F32), sds((1, RWKV_W), F32)),
        compiler_params=_cp(("arbitrary",)),
    )(shifted, shifted, *cts, wup, aup, w0, a0, k_k, k_a, bd)


def _col_lhs(row, eye_b):
    return eye_b * row.astype(MXU_DTYPE)


def _colsum(x):
    return jnp.sum(x, axis=0, keepdims=True)


def _stacked_segsum(tiles, bd):
    res = _seg_dot(jnp.concatenate(tiles, axis=0), bd)
    return [res[j * HEAD_DIM:(j + 1) * HEAD_DIM] for j in range(len(tiles))]


def _scan_specs(B, T, C, nC):
    def blk(z, col, rev):
        idx = (lambda g: (z, 0, nC - 1 - g, col)) if rev else (lambda g: (z, 0, g, col))
        return pl.BlockSpec((None, B, C, RWKV_W), idx)

    def blk3(col, rev):
        idx = (lambda g: (0, nC - 1 - g, col)) if rev else (lambda g: (0, g, col))
        return pl.BlockSpec((B, C, RWKV_W), idx)

    return blk, blk3


def _scan_fwd_call(w, kt, akk, kk, shifted, eye_b, eye_f, bd, B, T):
    C = min(SCAN_CHUNK, T)
    nC = T // C
    blk, blk3 = _scan_specs(B, T, C, nC)

    def body(w0, kt0, akk0, kk0, v0, r0, w1, kt1, akk1, kk1, v1, r1, eb_ref, ef_ref, bd_ref, y0, y1, st, S):
        @pl.when(pl.program_id(0) == 0)
        def _():
            S[...] = jnp.zeros_like(S)

        st[0] = S[...].astype(MXU_DTYPE)
        dirs = ((w0, kt0, akk0, kk0, v0, r0, y0), (w1, kt1, akk1, kk1, v1, r1, y1))

        def step(s, carry):
            for z in range(2):
                row = s if z == 0 else C - 1 - s
                prev = jnp.maximum(s - 1, 0) if z == 0 else jnp.minimum(C - s, C - 1)
                wr, ktr, akkr, kkr, vr, rr, yr = dirs[z]
                tiles = []
                for b in range(B):
                    Sb = st[s, z * B + b]
                    tiles += [Sb * kkr[b, pl.ds(row, 1), :].astype(MXU_DTYPE),
                              _col_lhs(vr[b, pl.ds(row, 1), :], eb_ref[...]),
                              Sb * rr[b, pl.ds(prev, 1), :].astype(MXU_DTYPE)]
                res = _stacked_segsum(tiles, bd_ref[...])
                for b in range(B):
                    c = z * B + b
                    sab, vb, yb = res[3 * b:3 * b + 3]
                    ld = lambda ref: ref[b, pl.ds(row, 1), :]
                    Sn = S[c] * ld(wr) - sab * ld(akkr) + vb * ld(ktr)
                    S[c] = Sn
                    st[s + 1, c] = Sn.astype(MXU_DTYPE)
                    yr[b, pl.ds(prev, 1), :] = _colsum(ef_ref[...] * yb)
            return carry

        lax.fori_loop(0, C, step, 0, unroll=SCAN_UNROLL)
        for z in range(2):
            last = C - 1 if z == 0 else 0
            rr, yr = dirs[z][5], dirs[z][6]
            res = _stacked_segsum([st[C, z * B + b] * rr[b, last:last + 1, :].astype(MXU_DTYPE) for b in range(B)],
                                  bd_ref[...])
            for b in range(B):
                yr[b, last:last + 1, :] = _colsum(ef_ref[...] * res[b])

    ins, specs = [], []
    for z, rev in ((0, False), (1, True)):
        ins += [w, kt, akk, kk, shifted, shifted]
        specs += [blk(z, 0, rev), blk(z, 0, rev), blk(z, 0, rev), blk3(0, rev), blk3(2, rev), blk3(0, rev)]
    sds = jax.ShapeDtypeStruct
    return pl.pallas_call(
        body, name="scan_fwd", grid=(nC,),
        in_specs=specs + [_full((HEAD_DIM, RWKV_W)), _full((HEAD_DIM, RWKV_W)), _full((256, 256))],
        out_specs=(blk3(0, False), blk3(0, True),
                   pl.BlockSpec((None, C + 1, 2 * B, HEAD_DIM, RWKV_W), lambda g: (g, 0, 0, 0, 0))),
        out_shape=(sds((B, T, RWKV_W), F32), sds((B, T, RWKV_W), F32),
                   sds((nC, C + 1, 2 * B, HEAD_DIM, RWKV_W), MXU_DTYPE)),
        scratch_shapes=[pltpu.VMEM((2 * B, HEAD_DIM, RWKV_W), F32)],
        compiler_params=_cp(("arbitrary",)),
    )(*ins, eye_b, eye_f, bd)


def _scan_bwd_call(w, kt, akk, kk, shifted, v_heads, dys, st, eye_b, eye_f, bd, B, T):
    C = min(SCAN_CHUNK, T)
    nC = T // C
    blk, blk3 = _scan_specs(B, T, C, nC)
    nin = 7

    def body(*refs):
        d0, d1 = refs[:nin], refs[nin:2 * nin]
        st_ref, eb_ref, ef_ref, sel_ref, hm_ref, bd_ref = refs[2 * nin:2 * nin + 6]
        o0, o1 = refs[2 * nin + 6:2 * nin + 12], refs[2 * nin + 12:2 * nin + 18]
        COL, DYC, G = refs[2 * nin + 18:]

        @pl.when(pl.program_id(0) == 0)
        def _():
            G[...] = jnp.zeros_like(G)

        dirs = (d0 + (o0,), d1 + (o1,))

        def column_operands(s, z):
            row = s if z == 0 else C - 1 - s
            _, _, _, kkr, _, _, dyr, _ = dirs[z]
            tiles = []
            for b in range(B):
                tiles += [st_ref[s, z * B + b] * kkr[b, pl.ds(row, 1), :].astype(MXU_DTYPE),
                          _col_lhs(dyr[b, pl.ds(row, 1), :], eb_ref[...])]
            return tiles

        def keep_columns(res, z):
            for b in range(B):
                for k in range(2):
                    COL[k, z * B + b] = res[2 * b + k].astype(MXU_DTYPE)
                DYC[z * B + b] = res[2 * b + 1]

        for z in range(2):
            keep_columns(_stacked_segsum(column_operands(C - 1, z), bd_ref[...]), z)

        def bwd(it, carry):
            s = C - 1 - it
            for z in range(2):
                row = s if z == 0 else C - 1 - s
                wr, ktr, akkr, kkr, vr, rr, dyr, (dw_o, dkt_o, dakk_o, dkk_o, dr_o, dv_o) = dirs[z]
                tiles, Gcs = [], []
                for b in range(B):
                    c = z * B + b
                    Gc = G[c] + DYC[c] * rr[b, pl.ds(row, 1), :]
                    Gb = Gc.astype(MXU_DTYPE)
                    Gcs.append((Gc, Gb))
                    tiles += [Gb * akkr[b, pl.ds(row, 1), :].astype(MXU_DTYPE),
                              Gb * ktr[b, pl.ds(row, 1), :].astype(MXU_DTYPE)]
                res = _stacked_segsum(tiles + column_operands(jnp.maximum(s - 1, 0), z), bd_ref[...])
                for b in range(B):
                    c = z * B + b
                    Gc, Gb = Gcs[b]
                    gab, dvb = res[2 * b], res[2 * b + 1]
                    ld = lambda ref: ref[b, pl.ds(row, 1), :]
                    G[c] = Gc * ld(wr) - gab * ld(kkr)
                    Sb = st_ref[s, c]
                    prods = jnp.concatenate([Gb, st_ref[s + 1, c] * COL[1, c], Gb * Sb, Gb * COL[0, c],
                                             gab.astype(MXU_DTYPE) * Sb], axis=0)
                    v_rows = jnp.concatenate([vr[b, pl.ds(row, 1)][0], jnp.zeros((8, 3 * HEAD_DIM), F32)], axis=1)
                    lhs = jnp.concatenate([sel_ref[...], v_rows], axis=0).astype(MXU_DTYPE)
                    sums = jnp.dot(lhs, prods, preferred_element_type=F32)
                    for k, (ref, sign) in enumerate(((dr_o, 1.0), (dw_o, 1.0), (dakk_o, -1.0), (dkk_o, -1.0))):
                        ref[b, pl.ds(row, 1), :] = sign * sums[k:k + 1, :]
                    dkt_o[b, pl.ds(row, 1), :] = _colsum(sums[8:16] * hm_ref[...])
                    dv_o[b, pl.ds(row, 1), :] = _colsum(ef_ref[...] * dvb)
                keep_columns(res[2 * B:], z)
            return carry

        lax.fori_loop(0, C, bwd, 0, unroll=SCAN_UNROLL)

    ins, specs = [], []
    for z, rev in ((0, True), (1, False)):
        heads = pl.BlockSpec((B, C) + v_heads.shape[2:], (lambda g: (0, nC - 1 - g, 0, 0)) if rev else (lambda g: (0, g, 0, 0)))
        ins += [w, kt, akk, kk, v_heads, shifted, dys]
        specs += [blk(z, 0, rev), blk(z, 0, rev), blk(z, 0, rev), blk3(0, rev), heads, blk3(0, rev), blk3(0, rev)]
    sel = (jnp.arange(8)[:, None] + 1 == (jnp.arange(5 * HEAD_DIM) // HEAD_DIM)[None, :]).astype(F32)
    head_rows = (jnp.arange(RWKV_W // HEAD_DIM)[:, None] == (jnp.arange(RWKV_W) // HEAD_DIM)[None, :]).astype(F32)
    ins += [st, eye_b, eye_f, sel, head_rows, bd]
    specs += [pl.BlockSpec((None, C + 1, 2 * B, HEAD_DIM, RWKV_W), lambda g: (nC - 1 - g, 0, 0, 0, 0)),
              _full((HEAD_DIM, RWKV_W)), _full((HEAD_DIM, RWKV_W)), _full(sel.shape), _full(head_rows.shape),
              _full((256, 256))]
    sds = jax.ShapeDtypeStruct
    out_specs = tuple(blk3(0, True) for _ in range(6)) + tuple(blk3(0, False) for _ in range(6))
    res = pl.pallas_call(
        body, name="scan_bwd", grid=(nC,), in_specs=specs, out_specs=out_specs,
        out_shape=tuple(sds((B, T, RWKV_W), F32) for _ in range(12)),
        scratch_shapes=[pltpu.VMEM((2, 2 * B, HEAD_DIM, RWKV_W), MXU_DTYPE), pltpu.VMEM((2 * B, HEAD_DIM, RWKV_W), F32),
                        pltpu.VMEM((2 * B, HEAD_DIM, RWKV_W), F32)],
        compiler_params=_cp(("arbitrary",)),
    )(*ins)
    return list(res)


def _out_head_call(x2, tgt2, gate, y_att, g_att, y0, y1, shifted, kt, g_rw, w_out, g_post, gn_w, gn_b, r_k, bd, T):
    R = x2.shape[0]
    TT = min(ROW_TILE, T)
    tpe = T // TT

    def body(x_ref, t_ref, gate_ref, ya_ref, ga_ref, y0_ref, y1_ref, r_ref, v_ref, kt_ref, grw_ref, w_ref, gp_ref,
             gnw_ref, gnb_ref, rk_ref, bd_ref,
             loss_o, dy_o, dya_o, dga_o, dys_o, dr_o, dv_o, dkts_o, dgrw_o, dgate_o, gw_o, ggp_o, ggnw_o, ggnb_o, grk_o):
        i = pl.program_id(0)
        bd = bd_ref[...]
        mix = functools.partial(_mix_fn, bd=bd, diff=True)
        (ma, mr), mix_vjp = jax.vjp(mix, ya_ref[...], ga_ref[...], y0_ref[...] + y1_ref[...], r_ref[...], v_ref[...],
                                    kt_ref[0] + kt_ref[1], grw_ref[...], gnw_ref[...], gnb_ref[...], rk_ref[...])
        out = _dot(ma, w_ref[0:ATT_W, :]) + _dot(mr, w_ref[ATT_W:, :])
        loss, loss_vjp = jax.vjp(_loss_fn, out, x_ref[...], t_ref[...], gate_ref[0], gp_ref[...])
        d_out, dy, _, dgate, dgp = loss_vjp(jnp.ones((1, 1), F32))
        dy_o[...] = dy
        dma = _dot_nt(d_out, w_ref[0:ATT_W, :])
        dmr = _dot_nt(d_out, w_ref[ATT_W:, :])
        dya_o[...], dga_o[...], dys_o[...], dr_o[...], dv_o[...], dkts_o[...], dgrw_o[...], dgnw, dgnb, drk = \
            mix_vjp((dma, dmr))
        gw = jnp.concatenate([_dot_tn(ma, d_out), _dot_tn(mr, d_out)], axis=0)
        acc = ((loss_o, jnp.broadcast_to(loss, (8, 128))), (gw_o, gw), (ggp_o, dgp), (ggnw_o, dgnw), (ggnb_o, dgnb),
               (grk_o, drk))

        @pl.when(i == 0)
        def _():
            for ref, val in acc:
                ref[...] = val

        @pl.when(i > 0)
        def _():
            for ref, val in acc:
                ref[...] += val

        @pl.when(i % tpe == 0)
        def _():
            dgate_o[0] = dgate

        @pl.when(i % tpe > 0)
        def _():
            dgate_o[0] += dgate

    row = lambda w, c=0: pl.BlockSpec((TT, w), lambda i: (i, c))
    two = pl.BlockSpec((2, TT, RWKV_W), lambda i: (0, i, 0))
    per_ex = pl.BlockSpec((1, 1, D_MODEL), lambda i: (i // tpe, 0, 0))
    sds = jax.ShapeDtypeStruct
    r512 = sds((R, RWKV_W), F32)
    return pl.pallas_call(
        body, name="out_head", grid=(R // TT,),
        in_specs=[row(D_MODEL), row(D_MODEL), per_ex, row(ATT_W), row(ATT_W), row(RWKV_W), row(RWKV_W), row(RWKV_W, 0),
                  row(RWKV_W, 2), two,
                  row(RWKV_W), _full(w_out.shape), _full((1, D_MODEL)), _full((1, RWKV_W)), _full((1, RWKV_W)),
                  _full((1, RWKV_W)), _full((256, 256))],
        out_specs=(_full((8, 128)), row(D_MODEL), row(ATT_W), row(ATT_W), row(RWKV_W), row(RWKV_W), row(RWKV_W),
                   row(RWKV_W), row(RWKV_W), per_ex, _full((D_MODEL, D_MODEL)), _full((1, D_MODEL)), _full((1, RWKV_W)),
                   _full((1, RWKV_W)), _full((1, RWKV_W))),
        out_shape=(sds((8, 128), F32), sds((R, D_MODEL), F32), r512, r512, r512, r512, r512, r512, r512,
                   sds((R // T, 1, D_MODEL), F32), sds((D_MODEL, D_MODEL), F32), sds((1, D_MODEL), F32),
                   sds((1, RWKV_W), F32), sds((1, RWKV_W), F32), sds((1, RWKV_W), F32)),
        compiler_params=_cp(("arbitrary",)),
    )(x2, tgt2, gate, y_att, g_att, y0, y1, shifted, shifted, kt, g_rw, w_out, g_post, gn_w, gn_b, r_k, bd)


def _in_proj_bwd_call(x2, dy, shift, scale, g_pre, w_in, qg, kg, cos, sin, bd, q_raw, k_raw, dqr, dkp, dvp,
                      d_gatt, d_rin, d_grw, T):
    R = x2.shape[0]
    TT = min(ROW_TILE, T)
    tpe = T // TT

    def body(x_ref, dy_ref, sh_ref, sc_ref, gp_ref, w_ref, qg_ref, kg_ref, cos_ref, sin_ref, bd_ref, q_ref, k_ref,
             dqr_ref, dkp_ref, dvp_ref, dga_ref, drin_ref, dgrw_ref,
             dx_o, dproj_o, dsh_o, dsc_o, ggp_o, gqg_o, gkg_o):
        i = pl.program_id(0)
        cos, sin, bd = cos_ref[...], sin_ref[...], bd_ref[...]
        left = lax.broadcasted_iota(jnp.int32, (1, KV_W), 1) < HEAD_DIM

        def kv_grad(ref):
            a = ref[0] + ref[1]
            b = ref[2] + ref[3]
            return jnp.where(left, a + pltpu.roll(a, HEAD_DIM, 1), b + pltpu.roll(b, HEAD_DIM, 1))

        qfn = functools.partial(_qk_fn, cos=jnp.tile(cos, (1, 4)), sin=jnp.tile(sin, (1, 4)), bd=bd, scale=ATT_SCALE,
                                diff=True)
        _, q_vjp = jax.vjp(qfn, q_ref[...], qg_ref[...])
        dq, gqg = q_vjp(dqr_ref[...])
        kfn = functools.partial(_qk_fn, cos=cos, sin=sin, bd=bd, scale=1.0, diff=True)
        _, k_vjp = jax.vjp(kfn, k_ref[...], kg_ref[...])
        dk, gkg = k_vjp(kv_grad(dkp_ref))
        pieces = ((C_Q, C_K, dq), (C_K, C_V, dk), (C_V, C_GA, kv_grad(dvp_ref)), (C_GA, C_RIN, dga_ref[...]),
                  (C_RIN, C_GRW, drin_ref[...]), (C_GRW, C_END, dgrw_ref[...]))
        dh = jnp.zeros((TT, D_MODEL), F32)
        for c0, c1, val in pieces:
            vb = val.astype(MXU_DTYPE)
            dproj_o[:, c0:c1] = vb
            dh = dh + _dot(vb, w_ref[c0:c1, :])
        _, pre_vjp = jax.vjp(_pre_fn, x_ref[...], sh_ref[0], sc_ref[0], gp_ref[...])
        dx, dsh, dsc, ggp = pre_vjp(dh)
        dx_o[...] = dx + dy_ref[...]
        acc = ((ggp_o, ggp), (gqg_o, gqg), (gkg_o, gkg))

        @pl.when(i == 0)
        def _():
            for ref, val in acc:
                ref[...] = val

        @pl.when(i > 0)
        def _():
            for ref, val in acc:
                ref[...] += val

        @pl.when(i % tpe == 0)
        def _():
            dsh_o[0] = dsh
            dsc_o[0] = dsc

        @pl.when(i % tpe > 0)
        def _():
            dsh_o[0] += dsh
            dsc_o[0] += dsc

    row = lambda w: pl.BlockSpec((TT, w), lambda i: (i, 0))
    per_ex = pl.BlockSpec((1, 1, D_MODEL), lambda i: (i // tpe, 0, 0))
    tab = pl.BlockSpec((TT, KV_W), lambda i: (i % tpe, 0))
    pad = pl.BlockSpec((4, TT, KV_W), lambda i: (0, i, 0))
    sds = jax.ShapeDtypeStruct
    nb = R // T
    return pl.pallas_call(
        body, name="in_proj_bwd", grid=(R // TT,),
        in_specs=[row(D_MODEL), row(D_MODEL), per_ex, per_ex, _full((1, D_MODEL)), _full(w_in.shape), _full((1, ATT_W)),
                  _full((1, KV_W)), tab, tab, _full((256, 256)), row(ATT_W), row(KV_W), row(ATT_W), pad, pad,
                  row(ATT_W), row(SHIFT_W), row(RWKV_W)],
        out_specs=(row(D_MODEL), row(C_END), per_ex, per_ex, _full((1, D_MODEL)), _full((1, ATT_W)), _full((1, KV_W))),
        out_shape=(sds((R, D_MODEL), F32), sds((R, C_END), MXU_DTYPE), sds((nb, 1, D_MODEL), F32),
                   sds((nb, 1, D_MODEL), F32), sds((1, D_MODEL), F32), sds((1, ATT_W), F32), sds((1, KV_W), F32)),
        compiler_params=_cp(("arbitrary",)),
    )(x2, dy, shift, scale, g_pre, w_in, qg, kg, cos, sin, bd, q_raw, k_raw, dqr, dkp, dvp, d_gatt, d_rin, d_grw)


def _w_in_grad_call(hb, dproj):
    R = hb.shape[0]
    TT = min(W_GRAD_ROWS, R)
    CB = 1152

    def body(h_ref, d_ref, o_ref):
        g = _dot_tn(h_ref[...], d_ref[...])

        @pl.when(pl.program_id(1) == 0)
        def _():
            o_ref[...] = g

        @pl.when(pl.program_id(1) > 0)
        def _():
            o_ref[...] += g

    return pl.pallas_call(
        body, name="w_in_grad", grid=(C_END // CB, R // TT),
        in_specs=[pl.BlockSpec((TT, D_MODEL), lambda j, i: (i, 0)), pl.BlockSpec((TT, CB), lambda j, i: (i, j))],
        out_specs=pl.BlockSpec((D_MODEL, CB), lambda j, i: (0, j)),
        out_shape=jax.ShapeDtypeStruct((D_MODEL, C_END), F32), compiler_params=_cp(("arbitrary", "arbitrary")),
    )(hb, dproj)


def _adam_refs(p_ref, w_ref, m_ref, v_ref, g_o, d_o, m_o, v_o):
    g = p_ref[0].astype(F32)
    for j in range(1, p_ref.shape[0]):
        g = g + p_ref[j].astype(F32)
    m2 = ADAM_B1 * m_ref[...] + (1.0 - ADAM_B1) * g
    v2 = ADAM_B2 * v_ref[...] + (1.0 - ADAM_B2) * jnp.square(g)
    m_hat = m2 / (1.0 - ADAM_B1 ** ADAM_STEP)
    v_hat = v2 / (1.0 - ADAM_B2 ** ADAM_STEP)
    g_o[...] = g
    d_o[...] = -ADAM_LR * (m_hat / (jnp.sqrt(v_hat) + ADAM_EPS) + ADAM_WD * w_ref[...])
    m_o[...] = m2
    v_o[...] = v2


def _adam_small_call(items, name):
    n = len(items)

    def body(*refs):
        for k in range(n):
            _adam_refs(*refs[4 * k:4 * k + 4], *refs[4 * n + 4 * k:4 * n + 4 * k + 4])

    out_shape = tuple(jax.ShapeDtypeStruct(w.shape, F32) for _, w, _, _ in items for _ in range(4))
    out = pl.pallas_call(body, name=name, out_shape=out_shape)(*[a for item in items for a in item])
    return [out[4 * k:4 * k + 4] for k in range(n)]


def _adam_call(parts, w, m, v, name, row_tile=None):
    P, M, N = parts.shape
    TM = M if row_tile is None else row_tile

    def body(*refs):
        _adam_refs(*refs)

    blk = pl.BlockSpec((TM, N), lambda i: (i, 0))
    return pl.pallas_call(
        body, name=name, grid=(M // TM,),
        in_specs=[pl.BlockSpec((P, TM, N), lambda i: (0, i, 0)), blk, blk, blk], out_specs=(blk,) * 4,
        out_shape=(jax.ShapeDtypeStruct((M, N), F32),) * 4, compiler_params=_cp(("arbitrary",)),
    )(parts, w, m, v)


_SMALL_ROWS = 136


def _pack_small(taps, w_up, w0, a_up, a0):
    flat = jnp.concatenate([taps.reshape(-1), w_up.reshape(-1), w0.reshape(-1), a_up.reshape(-1), a0.reshape(-1)])
    return jnp.pad(flat, (0, _SMALL_ROWS * 128 - flat.shape[0])).reshape(_SMALL_ROWS, 128)


def _unpack_small(packed):
    n = packed.shape[0]
    flat = packed.reshape(n, -1)
    out, o = [], 0
    for shape in ((3, 208), (2, 64, 64), (2, 64), (2, 64, 64), (2, 64)):
        size = 1
        for s in shape:
            size *= s
        out.append(flat[:, o:o + size].reshape((n,) + shape))
        o += size
    return out


def _cols_to_full(blocks):
    nd = blocks.ndim
    moved = jnp.moveaxis(blocks, 0, nd - 2)
    return moved.reshape(moved.shape[:-2] + (moved.shape[-2] * moved.shape[-1],))


def _full_to_cols(full):
    k = full.shape[-1] // NDEV
    return jnp.moveaxis(full.reshape(full.shape[:-1] + (NDEV, k)), -2, 0)


_REP_SIZES = (("g_pre", 1024), ("q_norm_g", 64), ("k_norm_g", 64), ("k_k", 512), ("k_a", 512), ("r_k", 512),
              ("gn_w", 512), ("gn_b", 512), ("g_post", 1024))
_REP_ROWS = 40


def kernel(x, c, w_ada, b_ada, g_pre, w_in, q_norm_g, k_norm_g, shift_taps, w_up, w0, a_up, a0, k_k, k_a, r_k, gn_w, gn_b, w_out, g_post, loss_target, m_w_ada, m_b_ada, m_g_pre, m_w_in, m_q_norm_g, m_k_norm_g, m_shift_taps, m_w_up, m_w0, m_a_up, m_a0, m_k_k, m_k_a, m_r_k, m_gn_w, m_gn_b, m_w_out, m_g_post, v_w_ada, v_b_ada, v_g_pre, v_w_in, v_q_norm_g, v_k_norm_g, v_shift_taps, v_w_up, v_w0, v_a_up, v_a0, v_k_k, v_k_a, v_r_k, v_gn_w, v_gn_b, v_w_out, v_g_post):
    B, T, _ = x.shape
    R = B * T
    me = 4 * lax.axis_index("x") + 2 * lax.axis_index("y") + lax.axis_index("c")
    x2 = x.reshape(R, D_MODEL)
    tgt2 = loss_target.reshape(R, D_MODEL)

    seg = jnp.arange(256) // HEAD_DIM
    bd = (seg[:, None] == seg[None, :]).astype(MXU_DTYPE)
    eye = (jnp.arange(HEAD_DIM)[:, None] == (jnp.arange(RWKV_W) % HEAD_DIM)[None, :])
    eye_b, eye_f = eye.astype(MXU_DTYPE), eye.astype(F32)
    cos, sin = _rope_tables(T)

    c_g, w_in_g, w_out_g, small_g = _exchange(
        [c, w_in[0].T.astype(MXU_DTYPE), w_out[0].astype(MXU_DTYPE),
         _pack_small(shift_taps[0], w_up[0], w0[0], a_up[0], a0[0])], ["all"] * 4, "gather_params")
    c_all = c_g.reshape(NDEV * B, D_MODEL)
    w_in_f = w_in_g.reshape(C_END, D_MODEL)
    w_out_f = w_out_g.reshape(D_MODEL, D_MODEL)
    taps_b, w_up_b, w0_b, a_up_b, a0_b = _unpack_small(small_g)
    taps_f = jnp.pad(_cols_to_full(taps_b), ((0, 5), (0, 0)))
    w_up_f, a_up_f = _cols_to_full(w_up_b), _cols_to_full(a_up_b)
    w0_f, a0_f = _cols_to_full(w0_b), _cols_to_full(a0_b)
    wup_pad = jnp.pad(w_up_f, ((0, 0), (0, 64), (0, 0))).astype(MXU_DTYPE)
    aup_pad = jnp.pad(a_up_f, ((0, 0), (64, 0), (0, 0))).astype(MXU_DTYPE)

    ncol = w_ada.shape[2]
    b_cols = lax.dynamic_slice(b_ada, (0, me * ncol), (1, ncol))
    mod_cols = _mod_call(c_all, w_ada[0].astype(MXU_DTYPE), b_cols)
    (mod_g,) = _exchange([mod_cols], ["all"], "gather_mod")
    mod = lax.dynamic_slice(_cols_to_full(mod_g), (me * B, 0), (B, 3 * D_MODEL))
    shift, scale, gate = [mod[:, j * D_MODEL:(j + 1) * D_MODEL].reshape(B, 1, D_MODEL) for j in range(3)]

    qg = jnp.tile(q_norm_g, (1, ATT_W // HEAD_DIM))
    kg = jnp.tile(k_norm_g, (1, KV_W // HEAD_DIM))
    rk_row = r_k.reshape(1, RWKV_W)

    hb, qr, kpad, vpad, q_raw, k_raw, g_att, rin, g_rw = _in_proj_call(
        x2, shift, scale, g_pre, w_in_f, qg, kg, cos, sin, bd, T)
    y_att = _att_fwd_call(qr, kpad, vpad, B, T)
    shifted = _shift_fwd_call(rin, taps_f, T)
    w_s, kt_s, akk_s, kk_s = _rwkv_prep_call(shifted, wup_pad, aup_pad, w0_f, a0_f, k_k, k_a, bd, T)
    sh3 = shifted.reshape(B, T, SHIFT_W)
    r4 = lambda a: a.reshape(2, B, T, RWKV_W)
    y0, y1, st = _scan_fwd_call(r4(w_s), r4(kt_s), r4(akk_s), kk_s.reshape(B, T, RWKV_W), sh3, eye_b, eye_f, bd, B, T)

    (loss_blk, dy, d_yatt, d_gatt, d_ys, d_r2, d_v2, d_kts, d_grw, d_gate, g_wout, g_gpost, g_gnw, g_gnb,
     g_rk) = _out_head_call(x2, tgt2, gate, y_att, g_att, y0.reshape(R, RWKV_W), y1.reshape(R, RWKV_W), shifted, kt_s,
                            g_rw, w_out_f, g_post, gn_w, gn_b, rk_row, bd, T)
    v_heads = jnp.pad(sh3[:, :, 2 * RWKV_W:3 * RWKV_W].reshape(B, T, RWKV_W // HEAD_DIM, HEAD_DIM),
                      ((0, 0), (0, 0), (0, 0), (0, HEAD_DIM)))
    scan_cts = _scan_bwd_call(r4(w_s), r4(kt_s), r4(akk_s), kk_s.reshape(B, T, RWKV_W), sh3, v_heads,
                              d_ys.reshape(B, T, RWKV_W), st, eye_b, eye_f, bd, B, T)
    scan_cts = [a.reshape(R, RWKV_W) for a in scan_cts]
    d_shifted, g_wup, g_aup, g_w0, g_a0, g_kk, g_ka = _rwkv_prep_bwd_call(
        shifted, scan_cts + [d_r2, d_v2, d_kts], wup_pad, aup_pad, w0_f, a0_f, k_k, k_a, bd, T)
    d_rin, g_taps = _shift_bwd_call(rin, d_shifted, taps_f, T)
    dqr, dkp, dvp = _att_bwd_call(qr, kpad, vpad, d_yatt, B, T)
    grad_x, dproj, d_shift, d_scale, g_gpre, g_qg, g_kg = _in_proj_bwd_call(
        x2, dy, shift, scale, g_pre, w_in_f, qg, kg, cos, sin, bd, q_raw, k_raw, dqr, dkp, dvp, d_gatt, d_rin, d_grw, T)
    g_win = _w_in_grad_call(hb, dproj)

    rep = jnp.concatenate([g_gpre.reshape(-1), g_qg.reshape(-1, HEAD_DIM).sum(0), g_kg.reshape(-1, HEAD_DIM).sum(0),
                           g_kk.reshape(-1), g_ka.reshape(-1), g_rk.reshape(-1), g_gnw.reshape(-1), g_gnb.reshape(-1),
                           g_gpost.reshape(-1), loss_blk[0, :1]])
    rep = jnp.pad(rep, (0, _REP_ROWS * 128 - rep.shape[0])).reshape(_REP_ROWS, 128)
    dmod = jnp.concatenate([d_shift, d_scale, d_gate], axis=2).reshape(B, 3 * D_MODEL)
    small_parts = jax.vmap(_pack_small)(_full_to_cols(g_taps[:3]), _full_to_cols(g_wup[:, :64, :]), _full_to_cols(g_w0),
                                        _full_to_cols(g_aup[:, 64:, :]), _full_to_cols(g_a0))
    core = lax.axis_index("c")
    halves = [a.reshape((NDEV // 2, 2) + a.shape[1:]).astype(MXU_DTYPE)
              for a in (_full_to_cols(g_win), g_wout.reshape(NDEV, D_MODEL // NDEV, D_MODEL))]
    pick = lambda a, j: lax.dynamic_index_in_dim(a, j, axis=1, keepdims=False)
    s_win, s_wout = _pair_sum_call([pick(a, core) for a in halves], [pick(a, 1 - core) for a in halves], "reduce_pair")
    p_win, p_wout, p_small, dmod_g, rep_g = _exchange(
        [s_win, s_wout, small_parts, dmod, rep], ["chips", "chips", "scatter", "all", "all"], "reduce_grads")
    dmod_all = dmod_g.reshape(NDEV * B, 3 * D_MODEL)
    g_wada = _wada_grad_call(c_all, lax.dynamic_slice(dmod_all, (0, me * ncol), (NDEV * B, ncol)))

    res, small = {}, []

    def adam(name, parts, w, m, v, row_tile=None, alone=False):
        two_d = (-1, w.shape[-1])
        item = (parts.reshape((parts.shape[0],) + w.reshape(two_d).shape), w.reshape(two_d), m.reshape(two_d),
                v.reshape(two_d))
        if alone:
            res[name] = [o.reshape(w.shape) for o in _adam_call(*item, "adam_" + name, row_tile)]
        else:
            small.append((name, w.shape, item))

    adam("w_ada", g_wada[None], w_ada, m_w_ada, v_w_ada, alone=True)
    adam("b_ada", dmod_all.reshape(NDEV * B, 1, 3 * D_MODEL), b_ada, m_b_ada, v_b_ada)
    adam("w_in", p_win, w_in, m_w_in, v_w_in, 128, alone=True)
    adam("w_out", p_wout, w_out, m_w_out, v_w_out, alone=True)
    taps_p, wup_p, w0_p, aup_p, a0_p = _unpack_small(p_small)
    adam("shift_taps", taps_p, shift_taps, m_shift_taps, v_shift_taps)
    adam("w_up", wup_p, w_up, m_w_up, v_w_up)
    adam("w0", w0_p, w0, m_w0, v_w0)
    adam("a_up", aup_p, a_up, m_a_up, v_a_up)
    adam("a0", a0_p, a0, m_a0, v_a0)
    rep_flat = rep_g.reshape(NDEV, -1)
    off = 0
    given = dict(g_pre=(g_pre, m_g_pre, v_g_pre), q_norm_g=(q_norm_g, m_q_norm_g, v_q_norm_g),
                 k_norm_g=(k_norm_g, m_k_norm_g, v_k_norm_g), k_k=(k_k, m_k_k, v_k_k), k_a=(k_a, m_k_a, v_k_a),
                 r_k=(r_k, m_r_k, v_r_k), gn_w=(gn_w, m_gn_w, v_gn_w), gn_b=(gn_b, m_gn_b, v_gn_b),
                 g_post=(g_post, m_g_post, v_g_post))
    for name, size in _REP_SIZES:
        adam(name, rep_flat[:, off:off + size], *given[name])
        off += size
    for (name, shape, _), out in zip(small, _adam_small_call([item for _, _, item in small], "adam_small")):
        res[name] = [o.reshape(shape) for o in out]

    loss = jnp.sum(rep_flat[:, off])
    order = ["w_ada", "b_ada", "g_pre", "w_in", "q_norm_g", "k_norm_g", "shift_taps", "w_up", "w0", "a_up", "a0", "k_k",
             "k_a", "r_k", "gn_w", "gn_b", "w_out", "g_post"]
    return (loss, grad_x.reshape(B, T, D_MODEL), *[res[n][0] for n in order], *[res[n][1] for n in order],
            *[res[n][2] for n in order], *[res[n][3] for n in order])
```

```python
import functools

import jax
import jax.numpy as jnp
from jax import lax
from jax.experimental import pallas as pl
from jax.experimental.pallas import tpu as pltpu

F32 = jnp.float32
MXU_DTYPE = jnp.bfloat16
MESH = pl.DeviceIdType.MESH
NDEV = 8

D_MODEL = 1024
HEAD_DIM = 64
ATT_W = 512
KV_W = 128
RWKV_W = 512
LORA_W = 128
SHIFT_W = 3 * RWKV_W + LORA_W
GRID_W = 64
ROPE_THETA = 10000.0
DECAY_SCALE = 0.6065306597126334
NORM_EPS = 1e-6
GN_EPS = 64e-5
L2_EPS = 1e-12
ATT_SCALE = HEAD_DIM ** -0.5
C_Q, C_K, C_V, C_GA, C_RIN, C_GRW, C_END = 0, 512, 640, 768, 1280, 2944, 3456

ADAM_LR, ADAM_B1, ADAM_B2, ADAM_EPS, ADAM_WD, ADAM_STEP = 0.001, 0.9, 0.999, 1e-08, 0.01, 10

ROW_TILE = 256
W_GRAD_ROWS = 2048
ATT_TILE_FWD = 256
ATT_TILE_BWD = 512
SCAN_CHUNK = 64
SCAN_UNROLL = 16
VMEM_LIMIT = 56 * 1024 * 1024


def _cp(sem=None):
    return pltpu.CompilerParams(dimension_semantics=sem, vmem_limit_bytes=VMEM_LIMIT)


def _dot(a, b, dims=(((1,), (0,)), ((), ()))):
    return lax.dot_general(a.astype(MXU_DTYPE), b.astype(MXU_DTYPE), dims, preferred_element_type=F32)


def _dot_nt(a, b):
    return _dot(a, b, (((1,), (1,)), ((), ())))


def _dot_tn(a, b):
    return _dot(a, b, (((0,), (0,)), ((), ())))


def _seg_dot(xb, bd):
    n = xb.shape[1]
    if n <= 256:
        return jnp.dot(xb, bd[:n, :n], preferred_element_type=F32)
    parts = [jnp.dot(xb[:, c:c + 256], bd, preferred_element_type=F32) for c in range(0, n, 256)]
    return jnp.concatenate(parts, axis=1)


def _segsum_raw(x, bd):
    rows = x.shape[0]
    hi = x.astype(MXU_DTYPE)
    lo = (x - hi.astype(F32)).astype(MXU_DTYPE)
    both = _seg_dot(jnp.concatenate([hi, lo], axis=0), bd)
    return both[:rows] + both[rows:]


@jax.custom_vjp
def _segsum_d(x, bd):
    return _segsum_raw(x, bd)


def _segsum_d_fwd(x, bd):
    return _segsum_raw(x, bd), bd


def _segsum_d_bwd(bd, ct):
    return _segsum_raw(ct, bd), jnp.zeros_like(bd)


_segsum_d.defvjp(_segsum_d_fwd, _segsum_d_bwd)


def _rope_tables(T):
    t = jnp.arange(T, dtype=F32)
    row = jnp.floor(t / GRID_W)
    col = t - row * GRID_W
    n_freq = HEAD_DIM // 4
    inv_freq = ROPE_THETA ** (-jnp.arange(n_freq, dtype=F32) / n_freq)
    d = jnp.arange(HEAD_DIM)
    pos = jnp.where((d < HEAD_DIM // 2)[None, :], row[:, None], col[:, None])
    ang = pos * inv_freq[d % n_freq][None, :]
    sign = jnp.where((d % 32) < 16, -1.0, 1.0).astype(F32)[None, :]
    cos = jnp.cos(ang)
    sin = jnp.sin(ang) * sign
    return jnp.tile(cos, (1, 2)), jnp.tile(sin, (1, 2))


def _rope_raw(x, cos, sin):
    n = x.shape[1]
    lane = lax.broadcasted_iota(jnp.int32, (1, n), 1)
    first = (lane % 32) < 16
    partner = jnp.where(first, pltpu.roll(x, n - 16, 1), pltpu.roll(x, 16, 1))
    return x * cos + partner * sin


@jax.custom_vjp
def _rope_d(x, cos, sin):
    return _rope_raw(x, cos, sin)


def _rope_d_fwd(x, cos, sin):
    return _rope_raw(x, cos, sin), (cos, sin)


def _rope_d_bwd(res, ct):
    cos, sin = res
    return _rope_raw(ct, cos, -sin), jnp.zeros_like(cos), jnp.zeros_like(sin)


_rope_d.defvjp(_rope_d_fwd, _rope_d_bwd)


def _rms(x, g):
    return x * lax.rsqrt(jnp.mean(x * x, axis=-1, keepdims=True) + NORM_EPS) * g


def _pre_fn(x, shift, scale, g_pre):
    return _rms(x, g_pre) * (1.0 + scale) + shift


def _qk_fn(q, g, cos, sin, bd, scale, diff):
    segsum = _segsum_d if diff else _segsum_raw
    rope = _rope_d if diff else _rope_raw
    qn = q * lax.rsqrt(segsum(q * q, bd) * (1.0 / HEAD_DIM) + NORM_EPS) * g
    return rope(qn, cos, sin) * scale


def _silu(x):
    return x * jax.nn.sigmoid(x)


def _rwkv_pw(k, pw0, pw1, pa0, pa1, w0, a0, k_k, k_a, bd, diff):
    segsum = _segsum_d if diff else _segsum_raw
    kk = k * k_k
    kk = kk * lax.rsqrt(segsum(kk * kk, bd) + L2_EPS)
    ws, kts, akks = [], [], []
    for z, (pw, pa) in enumerate(((pw0, pa0), (pw1, pa1))):
        w = jnp.exp(-DECAY_SCALE * jax.nn.sigmoid(w0[z:z + 1, :] + pw))
        a = jax.nn.sigmoid(a0[z:z + 1, :] + pa)
        ws.append(w)
        kts.append(k * (1.0 + (a - 1.0) * k_a))
        akks.append(a * kk)
    return ws[0], ws[1], kts[0], kts[1], akks[0], akks[1], kk


def _mix_fn(y_att, g_att, ys, r, v, kts, g_rw, gn_w, gn_b, r_k, bd, diff):
    segsum = _segsum_d if diff else _segsum_raw
    mu = segsum(ys, bd) * (1.0 / HEAD_DIM)
    d = ys - mu
    var = segsum(d * d, bd) * (1.0 / HEAD_DIM)
    yn = d * lax.rsqrt(var + GN_EPS) * gn_w + gn_b
    bonus = segsum(r * kts * r_k, bd) * v
    return y_att * _silu(g_att), (yn + bonus) * _silu(g_rw)


def _loss_fn(out, x, tgt, gate, g_post):
    e = x + gate * _rms(out, g_post) - tgt
    s = jnp.sum(e * e, axis=1, keepdims=True)
    return jnp.sum(s, axis=0, keepdims=True) * (0.5 / D_MODEL)


def _exchange(arrays, modes, name):
    n = len(arrays)
    out_shape = tuple(
        jax.ShapeDtypeStruct(((NDEV,) + tuple(a.shape)) if mode == "all" else tuple(a.shape), a.dtype)
        for a, mode in zip(arrays, modes))
    chips = (4, 2, 6)

    def body(*refs):
        ins, outs = refs[:n], refs[n:2 * n]
        send_sems, recv_sems, local_sems = refs[2 * n:]
        ix, iy, ic = lax.axis_index("x"), lax.axis_index("y"), lax.axis_index("c")
        me = 4 * ix + 2 * iy + ic

        def peer(m):
            px = 1 - ix if (m >> 2) & 1 else ix
            py = 1 - iy if (m >> 1) & 1 else iy
            pc = 1 - ic if m & 1 else ic
            return (px, py, pc), 4 * px + 2 * py + pc

        def copy(k, j, src_ref, slot, to):
            return pltpu.make_async_remote_copy(src_ref=src_ref, dst_ref=outs[k].at[slot], send_sem=send_sems.at[k, j],
                                                recv_sem=recv_sems.at[k, j], device_id=to, device_id_type=MESH)

        local, sends, arrivals, forwards = [], [], [], []
        for k in range(n):
            if modes[k] == "scatter":
                local.append(pltpu.make_async_copy(ins[k].at[me], outs[k].at[me], local_sems.at[k]))
                for m in range(1, NDEV):
                    to, p = peer(m)
                    sends.append(copy(k, m - 1, ins[k].at[p], me, to))
                    arrivals.append(copy(k, m - 1, ins[k].at[p], p, to))
            elif modes[k] == "chips":
                mine = me // 2
                local.append(pltpu.make_async_copy(ins[k].at[mine], outs[k].at[mine], local_sems.at[k]))
                for j, m in enumerate(chips):
                    to, p = peer(m)
                    sends.append(copy(k, j, ins[k].at[p // 2], mine, to))
                    arrivals.append(copy(k, j, ins[k].at[p // 2], p // 2, to))
            else:
                local.append(pltpu.make_async_copy(ins[k], outs[k].at[me], local_sems.at[k]))
                sib, sib_slot = peer(1)
                sends.append(copy(k, 0, ins[k], me, sib))
                for j, m in enumerate(chips):
                    to, p = peer(m)
                    sends.append(copy(k, 1 + j, ins[k], me, to))
                    forwards.append((copy(k, 1 + j, ins[k], p, to), copy(k, 4 + j, outs[k].at[p], p, sib)))
                    arrivals.append(copy(k, 4 + j, ins[k], peer(m ^ 1)[1], sib))
                arrivals.append(copy(k, 0, ins[k], sib_slot, sib))
        for cp in local + sends:
            cp.start()
        for arrived, onward in forwards:
            arrived.wait_recv()
            onward.start()
        for cp in arrivals:
            cp.wait_recv()
        for cp in sends + [onward for _, onward in forwards]:
            cp.wait_send()
        for cp in local:
            cp.wait()

    any_spec = pl.BlockSpec(memory_space=pl.ANY)
    return pl.pallas_call(
        body, name=name, out_shape=out_shape,
        in_specs=[any_spec] * n, out_specs=tuple([any_spec] * n),
        scratch_shapes=[pltpu.SemaphoreType.DMA((n, NDEV - 1)), pltpu.SemaphoreType.DMA((n, NDEV - 1)),
                        pltpu.SemaphoreType.DMA((n,))],
    )(*arrays)


def _pair_sum_call(mine, send, name):
    n = len(mine)

    def body(*refs):
        mine_r, send_r, out_r, land_r = (refs[j * n:(j + 1) * n] for j in range(4))
        send_sems, recv_sems = refs[4 * n:]
        sibling = (lax.axis_index("x"), lax.axis_index("y"), 1 - lax.axis_index("c"))
        swaps = [pltpu.make_async_remote_copy(src_ref=send_r[k], dst_ref=land_r[k], send_sem=send_sems.at[k],
                                              recv_sem=recv_sems.at[k], device_id=sibling, device_id_type=MESH)
                 for k in range(n)]
        for cp in swaps:
            cp.start()
        for k, cp in enumerate(swaps):
            cp.wait()
            out_r[k][...] = (mine_r[k][...].astype(F32) + land_r[k][...].astype(F32)).astype(out_r[k].dtype)

    return pl.pallas_call(
        body, name=name, out_shape=tuple(jax.ShapeDtypeStruct(a.shape, a.dtype) for a in mine),
        scratch_shapes=[pltpu.VMEM(a.shape, a.dtype) for a in mine] + [pltpu.SemaphoreType.DMA((n,)),
                                                                         pltpu.SemaphoreType.DMA((n,))],
        compiler_params=pltpu.CompilerParams(vmem_limit_bytes=VMEM_LIMIT),
    )(*mine, *send)


def _mod_call(c_all, w_ada, b_cols):
    def body(c_ref, w_ref, b_ref, o_ref):
        o_ref[...] = _dot(_silu(c_ref[...]), w_ref[...]) + b_ref[...]

    return pl.pallas_call(body, name="mod_fwd",
                          out_shape=jax.ShapeDtypeStruct((c_all.shape[0], w_ada.shape[1]), F32))(c_all, w_ada, b_cols)


def _wada_grad_call(c_all, dmod_cols):
    def body(c_ref, d_ref, o_ref):
        o_ref[...] = _dot_tn(_silu(c_ref[...]), d_ref[...])

    return pl.pallas_call(body, name="w_ada_grad",
                          out_shape=jax.ShapeDtypeStruct((c_all.shape[1], dmod_cols.shape[1]), F32))(c_all, dmod_cols)


def _full(shape):
    nd = len(shape)
    return pl.BlockSpec(shape, lambda *_: (0,) * nd)


def _in_proj_call(x2, shift, scale, g_pre, w_in, qg, kg, cos, sin, bd, T):
    R = x2.shape[0]
    TT = min(ROW_TILE, T)
    tpe = T // TT

    def body(x_ref, sh_ref, sc_ref, gp_ref, w_ref, qg_ref, kg_ref, cos_ref, sin_ref, bd_ref,
             hb_ref, qr_ref, kpad_ref, vpad_ref, qraw_ref, kraw_ref, gatt_ref, rin_ref, grw_ref):
        h = _pre_fn(x_ref[...], sh_ref[0], sc_ref[0], gp_ref[...])
        hb = h.astype(MXU_DTYPE)
        hb_ref[...] = hb

        def proj(c0, c1):
            return _dot_nt(hb, w_ref[c0:c1, :])

        q = proj(C_Q, C_K)
        k = proj(C_K, C_V)
        v = proj(C_V, C_GA)
        gatt_ref[...] = proj(C_GA, C_RIN)
        rin_ref[...] = proj(C_RIN, C_GRW)
        grw_ref[...] = proj(C_GRW, C_END)
        qraw_ref[...] = q
        kraw_ref[...] = k
        cos, sin, bd = cos_ref[...], sin_ref[...], bd_ref[...]
        qr = _qk_fn(q, qg_ref[...], jnp.tile(cos, (1, 4)), jnp.tile(sin, (1, 4)), bd, ATT_SCALE, False)
        qr_ref[...] = qr.astype(MXU_DTYPE)
        kr = _qk_fn(k, kg_ref[...], cos, sin, bd, 1.0, False)
        left = lax.broadcasted_iota(jnp.int32, (1, KV_W), 1) < HEAD_DIM
        for ref, val in ((kpad_ref, kr), (vpad_ref, v)):
            h0l = jnp.where(left, val, 0.0)
            h1r = jnp.where(left, 0.0, val)
            ref[0] = h0l.astype(MXU_DTYPE)
            ref[1] = pltpu.roll(h0l, HEAD_DIM, 1).astype(MXU_DTYPE)
            ref[2] = pltpu.roll(h1r, HEAD_DIM, 1).astype(MXU_DTYPE)
            ref[3] = h1r.astype(MXU_DTYPE)

    row = lambda w: pl.BlockSpec((TT, w), lambda i: (i, 0))
    per_ex = pl.BlockSpec((1, 1, D_MODEL), lambda i: (i // tpe, 0, 0))
    tab = pl.BlockSpec((TT, KV_W), lambda i: (i % tpe, 0))
    pad = pl.BlockSpec((4, TT, KV_W), lambda i: (0, i, 0))
    sds = jax.ShapeDtypeStruct
    return pl.pallas_call(
        body, name="in_proj", grid=(R // TT,),
        in_specs=[row(D_MODEL), per_ex, per_ex, _full((1, D_MODEL)), _full(w_in.shape), _full((1, ATT_W)),
                  _full((1, KV_W)), tab, tab, _full((256, 256))],
        out_specs=(row(D_MODEL), row(ATT_W), pad, pad, row(ATT_W), row(KV_W), row(ATT_W), row(SHIFT_W), row(RWKV_W)),
        out_shape=(sds((R, D_MODEL), MXU_DTYPE), sds((R, ATT_W), MXU_DTYPE), sds((4, R, KV_W), MXU_DTYPE),
                   sds((4, R, KV_W), MXU_DTYPE), sds((R, ATT_W), F32), sds((R, KV_W), F32), sds((R, ATT_W), F32),
                   sds((R, SHIFT_W), F32), sds((R, RWKV_W), F32)),
        compiler_params=_cp(("arbitrary",)),
    )(x2, shift, scale, g_pre, w_in, qg, kg, cos, sin, bd)


def _softmax_parts(s):
    e = jnp.exp(s - jnp.max(s, axis=1, keepdims=True))
    return e, 1.0 / jnp.sum(e, axis=1, keepdims=True)


def _att_specs(T, TQ):
    nq = T // TQ
    qspec = pl.BlockSpec((TQ, KV_W), lambda b, p, i: (b * nq + i, p))
    side = lambda s: pl.BlockSpec((None, T, KV_W), lambda b, p, i: (2 * (p // 2) + s, b, 0))
    return nq, qspec, side


def _att_fwd_call(qr, kpad, vpad, B, T):
    TQ = min(ATT_TILE_FWD, T)
    nq, qspec, side = _att_specs(T, TQ)

    def body(q_ref, kl_ref, kr_ref, vl_ref, vr_ref, o_ref):
        q = q_ref[...]
        ea, inv_a = _softmax_parts(_dot_nt(q, kl_ref[...]))
        eb, inv_b = _softmax_parts(_dot_nt(q, kr_ref[...]))
        o_ref[...] = _dot(ea, vl_ref[...]) * inv_a + _dot(eb, vr_ref[...]) * inv_b

    return pl.pallas_call(
        body, name="att_fwd", grid=(B, 4, nq),
        in_specs=[qspec, side(0), side(1), side(0), side(1)], out_specs=qspec,
        out_shape=jax.ShapeDtypeStruct((B * T, ATT_W), F32),
        compiler_params=_cp(("arbitrary",) * 3),
    )(qr, kpad, kpad, vpad, vpad)


def _att_bwd_call(qr, kpad, vpad, d_o, B, T):
    TQ = min(ATT_TILE_BWD, T)
    nq, qspec, side = _att_specs(T, TQ)

    def body(q_ref, kl_ref, kr_ref, vl_ref, vr_ref, do_ref, dq_ref, dk_ref, dv_ref):
        i = pl.program_id(2)
        q, do = q_ref[...], do_ref[...]
        left = lax.broadcasted_iota(jnp.int32, (1, KV_W), 1) < HEAD_DIM
        dq = jnp.zeros((TQ, KV_W), F32)
        dk = jnp.zeros((T, KV_W), F32)
        dv = jnp.zeros((T, KV_W), F32)
        for k_ref, v_ref, mask in ((kl_ref, vl_ref, left), (kr_ref, vr_ref, jnp.logical_not(left))):
            kk, vv = k_ref[...], v_ref[...]
            e, inv = _softmax_parts(_dot_nt(q, kk))
            dp = _dot_nt(do, vv)
            ds = e * (dp - inv * jnp.sum(e * dp, axis=1, keepdims=True))
            dq = dq + _dot(ds, kk) * inv
            dk = dk + _dot_tn(ds, jnp.where(mask, q * inv, 0.0))
            dv = dv + _dot_tn(e, jnp.where(mask, do * inv, 0.0))
        dq_ref[...] = dq

        @pl.when(i == 0)
        def _():
            dk_ref[...] = dk
            dv_ref[...] = dv

        @pl.when(i > 0)
        def _():
            dk_ref[...] += dk
            dv_ref[...] += dv

    acc = pl.BlockSpec((None, T, KV_W), lambda b, p, i: (p, b, 0))
    sds = jax.ShapeDtypeStruct
    return pl.pallas_call(
        body, name="att_bwd", grid=(B, 4, nq),
        in_specs=[qspec, side(0), side(1), side(0), side(1), qspec], out_specs=(qspec, acc, acc),
        out_shape=(sds((B * T, ATT_W), F32), sds((4, B * T, KV_W), F32), sds((4, B * T, KV_W), F32)),
        compiler_params=_cp(("arbitrary",) * 3),
    )(qr, kpad, kpad, vpad, vpad, d_o)


def _shift_specs(R, T, TT, width):
    tpe = T // TT
    nb8 = R // 8
    cur = pl.BlockSpec((TT, width), lambda i: (i, 0))
    prev = pl.BlockSpec((8, width), lambda i: (jnp.maximum(i * (TT // 8) - 1, 0), 0))
    nxt = pl.BlockSpec((8, width), lambda i: (jnp.minimum((i + 1) * (TT // 8), nb8 - 1), 0))
    return tpe, cur, prev, nxt


def _neighbours(cur, prev8, next8, i, tpe, TT):
    rows = lax.broadcasted_iota(jnp.int32, (TT, 1), 0)
    first = jnp.where(i % tpe == 0, 0.0, 1.0)
    last = jnp.where(i % tpe == tpe - 1, 0.0, 1.0)
    before = jnp.where(rows == 0, prev8[7:8, :] * first, pltpu.roll(cur, 1, 0))
    after = jnp.where(rows == TT - 1, next8[0:1, :] * last, pltpu.roll(cur, TT - 1, 0))
    return before, after


def _shift_fwd_call(x, taps, T):
    R, width = x.shape
    TT = min(ROW_TILE, T)
    tpe, cur, prev, nxt = _shift_specs(R, T, TT, width)

    def body(x_ref, p_ref, n_ref, t_ref, o_ref):
        xc = x_ref[...]
        before, after = _neighbours(xc, p_ref[...], n_ref[...], pl.program_id(0), tpe, TT)
        o_ref[...] = t_ref[0:1, :] * before + t_ref[1:2, :] * xc + t_ref[2:3, :] * after

    return pl.pallas_call(
        body, name="shift_fwd", grid=(R // TT,), in_specs=[cur, prev, nxt, _full(taps.shape)], out_specs=cur,
        out_shape=jax.ShapeDtypeStruct((R, width), F32), compiler_params=_cp(("arbitrary",)),
    )(x, x, x, taps)


def _shift_bwd_call(x, d, taps, T):
    R, width = x.shape
    TT = min(ROW_TILE, T)
    tpe, cur, prev, nxt = _shift_specs(R, T, TT, width)

    def body(x_ref, xp_ref, xn_ref, d_ref, dp_ref, dn_ref, t_ref, dx_ref, dt_ref):
        i = pl.program_id(0)
        xc, dc = x_ref[...], d_ref[...]
        d_before, d_after = _neighbours(dc, dp_ref[...], dn_ref[...], i, tpe, TT)
        dx_ref[...] = t_ref[2:3, :] * d_before + t_ref[1:2, :] * dc + t_ref[0:1, :] * d_after
        x_before, x_after = _neighbours(xc, xp_ref[...], xn_ref[...], i, tpe, TT)
        @pl.when(i == 0)
        def _():
            dt_ref[...] = jnp.zeros_like(dt_ref)

        for j, xs in enumerate((x_before, xc, x_after)):
            dt_ref[j:j + 1, :] += jnp.sum(dc * xs, axis=0, keepdims=True)

    return pl.pallas_call(
        body, name="shift_bwd", grid=(R // TT,),
        in_specs=[cur, prev, nxt, cur, prev, nxt, _full(taps.shape)], out_specs=(cur, _full((8, width))),
        out_shape=(jax.ShapeDtypeStruct((R, width), F32), jax.ShapeDtypeStruct((8, width), F32)),
        compiler_params=_cp(("arbitrary",)),
    )(x, x, x, d, d, d, taps)


def _lora_in(wa):
    lane = lax.broadcasted_iota(jnp.int32, (1, LORA_W), 1)
    return jnp.where(lane < LORA_W // 2, jnp.tanh(wa), wa)


def _rwkv_prep_call(shifted, wup, aup, w0, a0, k_k, k_a, bd, T):
    R = shifted.shape[0]
    TT = min(ROW_TILE, T)

    def body(k_ref, wa_ref, wup_ref, aup_ref, w0_ref, a0_ref, kk_ref, ka_ref, bd_ref, w_o, kt_o, akk_o, kk_o):
        twa = _lora_in(wa_ref[...])
        pre = [_dot(twa, m_ref[z]) for m_ref in (wup_ref, aup_ref) for z in range(2)]
        outs = _rwkv_pw(k_ref[...], pre[0], pre[1], pre[2], pre[3], w0_ref[...], a0_ref[...], kk_ref[...],
                        ka_ref[...], bd_ref[...], False)
        w_o[0], w_o[1], kt_o[0], kt_o[1], akk_o[0], akk_o[1] = outs[:6]
        kk_o[...] = outs[6]

    col = lambda c, w: pl.BlockSpec((TT, w), lambda i: (i, c))
    two = pl.BlockSpec((2, TT, RWKV_W), lambda i: (0, i, 0))
    sds = jax.ShapeDtypeStruct
    return pl.pallas_call(
        body, name="rwkv_prep", grid=(R // TT,),
        in_specs=[col(1, RWKV_W), col(3 * RWKV_W // LORA_W, LORA_W), _full(wup.shape), _full(aup.shape),
                  _full((2, RWKV_W)), _full((2, RWKV_W)), _full((1, RWKV_W)), _full((1, RWKV_W)), _full((256, 256))],
        out_specs=(two, two, two, col(0, RWKV_W)),
        out_shape=(sds((2, R, RWKV_W), F32),) * 3 + (sds((R, RWKV_W), F32),),
        compiler_params=_cp(("arbitrary",)),
    )(shifted, shifted, wup, aup, w0, a0, k_k, k_a, bd)


def _rwkv_prep_bwd_call(shifted, cts, wup, aup, w0, a0, k_k, k_a, bd, T):
    R = shifted.shape[0]
    TT = min(ROW_TILE, T)

    def body(k_ref, wa_ref, dw0, dkt0, dakk0, dkk0, dr0, dv0, dw1, dkt1, dakk1, dkk1, dr1, dv1, dr2_ref, dv2_ref, dkts_ref,
             wup_ref, aup_ref, w0_ref, a0_ref, kk_ref, ka_ref, bd_ref,
             dsh_ref, gwup_ref, gaup_ref, gw0_ref, ga0_ref, gkk_ref, gka_ref):
        dw_ref, dkt_ref, dakk_ref, dkk_ref, dr_ref, dv_ref = ((dw0, dw1), (dkt0, dkt1), (dakk0, dakk1), (dkk0, dkk1),
                                                              (dr0, dr1), (dv0, dv1))
        i = pl.program_id(0)
        wa = wa_ref[...]
        twa = _lora_in(wa)
        pre = [_dot(twa, m_ref[z]) for m_ref in (wup_ref, aup_ref) for z in range(2)]
        fn = functools.partial(_rwkv_pw, bd=bd_ref[...], diff=True)
        _, vjp = jax.vjp(fn, k_ref[...], pre[0], pre[1], pre[2], pre[3], w0_ref[...], a0_ref[...], kk_ref[...],
                         ka_ref[...])
        dkts = dkts_ref[...]
        dk, dpw0, dpw1, dpa0, dpa1, gw0, ga0, gkk, gka = vjp(
            (dw_ref[0][...], dw_ref[1][...], dkt_ref[0][...] + dkts, dkt_ref[1][...] + dkts, dakk_ref[0][...],
             dakk_ref[1][...], dkk_ref[0][...] + dkk_ref[1][...]))
        dtwa = (_dot_nt(dpw0, wup_ref[0]) + _dot_nt(dpw1, wup_ref[1]) + _dot_nt(dpa0, aup_ref[0])
                + _dot_nt(dpa1, aup_ref[1]))
        lane = lax.broadcasted_iota(jnp.int32, (1, LORA_W), 1)
        dsh_ref[:, 0:RWKV_W] = dr_ref[0][...] + dr_ref[1][...] + dr2_ref[...]
        dsh_ref[:, RWKV_W:2 * RWKV_W] = dk
        dsh_ref[:, 2 * RWKV_W:3 * RWKV_W] = dv_ref[0][...] + dv_ref[1][...] + dv2_ref[...]
        dsh_ref[:, 3 * RWKV_W:] = jnp.where(lane < LORA_W // 2, dtwa * (1.0 - twa * twa), dtwa)
        acc = ((gwup_ref.at[0], _dot_tn(twa, dpw0)), (gwup_ref.at[1], _dot_tn(twa, dpw1)),
               (gaup_ref.at[0], _dot_tn(twa, dpa0)), (gaup_ref.at[1], _dot_tn(twa, dpa1)),
               (gw0_ref, gw0), (ga0_ref, ga0), (gkk_ref, gkk), (gka_ref, gka))

        @pl.when(i == 0)
        def _():
            for ref, val in acc:
                ref[...] = val

        @pl.when(i > 0)
        def _():
            for ref, val in acc:
                ref[...] += val

    col = lambda c, w: pl.BlockSpec((TT, w), lambda i: (i, c))
    one = col(0, RWKV_W)
    sds = jax.ShapeDtypeStruct
    return pl.pallas_call(
        body, name="rwkv_prep_bwd", grid=(R // TT,),
        in_specs=[col(1, RWKV_W), col(3 * RWKV_W // LORA_W, LORA_W)] + [one] * 15 + [
                  _full(wup.shape), _full(aup.shape), _full((2, RWKV_W)), _full((2, RWKV_W)), _full((1, RWKV_W)),
                  _full((1, RWKV_W)), _full((256, 256))],
        out_specs=(pl.BlockSpec((TT, SHIFT_W), lambda i: (i, 0)), _full(wup.shape), _full(aup.shape),
                   _full((2, RWKV_W)), _full((2, RWKV_W)), _full((1, RWKV_W)), _full((1, RWKV_W))),
        out_shape=(sds((R, SHIFT_W), F32), sds(wup.shape, F32), sds(aup.shape, F32), sds((2, RWKV_W), F32),
                   sds((2, RWKV_W), F32), sds((1, RWKV_W), F32), sds((1, RWKV_W), F32)),
        compiler_params=_cp(("arbitrary",)),
    )(shifted, shifted, *cts, wup, aup, w0, a0, k_k, k_a, bd)


def _col_lhs(row, eye_b):
    return eye_b * row.astype(MXU_DTYPE)


def _colsum(x):
    return jnp.sum(x, axis=0, keepdims=True)


def _stacked_segsum(tiles, bd):
    res = _seg_dot(jnp.concatenate(tiles, axis=0), bd)
    return [res[j * HEAD_DIM:(j + 1) * HEAD_DIM] for j in range(len(tiles))]


def _scan_specs(B, T, C, nC):
    def blk(z, col, rev):
        idx = (lambda g: (z, 0, nC - 1 - g, col)) if rev else (lambda g: (z, 0, g, col))
        return pl.BlockSpec((None, B, C, RWKV_W), idx)

    def blk3(col, rev):
        idx = (lambda g: (0, nC - 1 - g, col)) if rev else (lambda g: (0, g, col))
        return pl.BlockSpec((B, C, RWKV_W), idx)

    return blk, blk3


def _scan_fwd_call(w, kt, akk, kk, shifted, eye_b, eye_f, bd, B, T):
    C = min(SCAN_CHUNK, T)
    nC = T // C
    blk, blk3 = _scan_specs(B, T, C, nC)

    def body(w0, kt0, akk0, kk0, v0, r0, w1, kt1, akk1, kk1, v1, r1, eb_ref, ef_ref, bd_ref, y0, y1, st, S):
        @pl.when(pl.program_id(0) == 0)
        def _():
            S[...] = jnp.zeros_like(S)

        st[0] = S[...].astype(MXU_DTYPE)
        dirs = ((w0, kt0, akk0, kk0, v0, r0, y0), (w1, kt1, akk1, kk1, v1, r1, y1))

        def step(s, carry):
            for z in range(2):
                row = s if z == 0 else C - 1 - s
                prev = jnp.maximum(s - 1, 0) if z == 0 else jnp.minimum(C - s, C - 1)
                wr, ktr, akkr, kkr, vr, rr, yr = dirs[z]
                tiles = []
                for b in range(B):
                    Sb = st[s, z * B + b]
                    tiles += [Sb * kkr[b, pl.ds(row, 1), :].astype(MXU_DTYPE),
                              _col_lhs(vr[b, pl.ds(row, 1), :], eb_ref[...]),
                              Sb * rr[b, pl.ds(prev, 1), :].astype(MXU_DTYPE)]
                res = _stacked_segsum(tiles, bd_ref[...])
                for b in range(B):
                    c = z * B + b
                    sab, vb, yb = res[3 * b:3 * b + 3]
                    ld = lambda ref: ref[b, pl.ds(row, 1), :]
                    Sn = S[c] * ld(wr) - sab * ld(akkr) + vb * ld(ktr)
                    S[c] = Sn
                    st[s + 1, c] = Sn.astype(MXU_DTYPE)
                    yr[b, pl.ds(prev, 1), :] = _colsum(ef_ref[...] * yb)
            return carry

        lax.fori_loop(0, C, step, 0, unroll=SCAN_UNROLL)
        for z in range(2):
            last = C - 1 if z == 0 else 0
            rr, yr = dirs[z][5], dirs[z][6]
            res = _stacked_segsum([st[C, z * B + b] * rr[b, last:last + 1, :].astype(MXU_DTYPE) for b in range(B)],
                                  bd_ref[...])
            for b in range(B):
                yr[b, last:last + 1, :] = _colsum(ef_ref[...] * res[b])

    ins, specs = [], []
    for z, rev in ((0, False), (1, True)):
        ins += [w, kt, akk, kk, shifted, shifted]
        specs += [blk(z, 0, rev), blk(z, 0, rev), blk(z, 0, rev), blk3(0, rev), blk3(2, rev), blk3(0, rev)]
    sds = jax.ShapeDtypeStruct
    return pl.pallas_call(
        body, name="scan_fwd", grid=(nC,),
        in_specs=specs + [_full((HEAD_DIM, RWKV_W)), _full((HEAD_DIM, RWKV_W)), _full((256, 256))],
        out_specs=(blk3(0, False), blk3(0, True),
                   pl.BlockSpec((None, C + 1, 2 * B, HEAD_DIM, RWKV_W), lambda g: (g, 0, 0, 0, 0))),
        out_shape=(sds((B, T, RWKV_W), F32), sds((B, T, RWKV_W), F32),
                   sds((nC, C + 1, 2 * B, HEAD_DIM, RWKV_W), MXU_DTYPE)),
        scratch_shapes=[pltpu.VMEM((2 * B, HEAD_DIM, RWKV_W), F32)],
        compiler_params=_cp(("arbitrary",)),
    )(*ins, eye_b, eye_f, bd)


def _scan_bwd_call(w, kt, akk, kk, shifted, v_heads, dys, st, eye_b, eye_f, bd, B, T):
    C = min(SCAN_CHUNK, T)
    nC = T // C
    blk, blk3 = _scan_specs(B, T, C, nC)
    nin = 7

    def body(*refs):
        d0, d1 = refs[:nin], refs[nin:2 * nin]
        st_ref, eb_ref, ef_ref, sel_ref, hm_ref, bd_ref = refs[2 * nin:2 * nin + 6]
        o0, o1 = refs[2 * nin + 6:2 * nin + 12], refs[2 * nin + 12:2 * nin + 18]
        COL, DYC, G = refs[2 * nin + 18:]

        @pl.when(pl.program_id(0) == 0)
        def _():
            G[...] = jnp.zeros_like(G)

        dirs = (d0 + (o0,), d1 + (o1,))

        def column_operands(s, z):
            row = s if z == 0 else C - 1 - s
            _, _, _, kkr, _, _, dyr, _ = dirs[z]
            tiles = []
            for b in range(B):
                tiles += [st_ref[s, z * B + b] * kkr[b, pl.ds(row, 1), :].astype(MXU_DTYPE),
                          _col_lhs(dyr[b, pl.ds(row, 1), :], eb_ref[...])]
            return tiles

        def keep_columns(res, z):
            for b in range(B):
                for k in range(2):
                    COL[k, z * B + b] = res[2 * b + k].astype(MXU_DTYPE)
                DYC[z * B + b] = res[2 * b + 1]

        for z in range(2):
            keep_columns(_stacked_segsum(column_operands(C - 1, z), bd_ref[...]), z)

        def bwd(it, carry):
            s = C - 1 - it
            for z in range(2):
                row = s if z == 0 else C - 1 - s
                wr, ktr, akkr, kkr, vr, rr, dyr, (dw_o, dkt_o, dakk_o, dkk_o, dr_o, dv_o) = dirs[z]
                tiles, Gcs = [], []
                for b in range(B):
                    c = z * B + b
                    Gc = G[c] + DYC[c] * rr[b, pl.ds(row, 1), :]
                    Gb = Gc.astype(MXU_DTYPE)
                    Gcs.append((Gc, Gb))
                    tiles += [Gb * akkr[b, pl.ds(row, 1), :].astype(MXU_DTYPE),
                              Gb * ktr[b, pl.ds(row, 1), :].astype(MXU_DTYPE)]
                res = _stacked_segsum(tiles + column_operands(jnp.maximum(s - 1, 0), z), bd_ref[...])
                for b in range(B):
                    c = z * B + b
                    Gc, Gb = Gcs[b]
                    gab, dvb = res[2 * b], res[2 * b + 1]
                    ld = lambda ref: ref[b, pl.ds(row, 1), :]
                    G[c] = Gc * ld(wr) - gab * ld(kkr)
                    Sb = st_ref[s, c]
                    prods = jnp.concatenate([st_ref[s + 1, c] * COL[1, c], Gb * Sb, Gb * COL[0, c],
                                             gab.astype(MXU_DTYPE) * Sb], axis=0)
                    sums = jnp.dot(sel_ref[...], prods, preferred_element_type=F32)
                    for k, (ref, sign) in enumerate(((dr_o, 1.0), (dw_o, 1.0), (dakk_o, -1.0), (dkk_o, -1.0))):
                        ref[b, pl.ds(row, 1), :] = sign * sums[k:k + 1, :]
                    v_rows = jnp.concatenate([vr[b, pl.ds(row, 1)][0], jnp.zeros((8, HEAD_DIM), F32)], axis=0)
                    dkt_o[b, pl.ds(row, 1), :] = _colsum(_dot(v_rows, Gb)[:8] * hm_ref[...])
                    dv_o[b, pl.ds(row, 1), :] = _colsum(ef_ref[...] * dvb)
                keep_columns(res[2 * B:], z)
            return carry

        lax.fori_loop(0, C, bwd, 0, unroll=SCAN_UNROLL)

    ins, specs = [], []
    for z, rev in ((0, True), (1, False)):
        heads = pl.BlockSpec((B, C) + v_heads.shape[2:], (lambda g: (0, nC - 1 - g, 0, 0)) if rev else (lambda g: (0, g, 0, 0)))
        ins += [w, kt, akk, kk, v_heads, shifted, dys]
        specs += [blk(z, 0, rev), blk(z, 0, rev), blk(z, 0, rev), blk3(0, rev), heads, blk3(0, rev), blk3(0, rev)]
    sel = (jnp.arange(16)[:, None] == (jnp.arange(4 * HEAD_DIM) // HEAD_DIM)[None, :]).astype(MXU_DTYPE)
    head_rows = (jnp.arange(RWKV_W // HEAD_DIM)[:, None] == (jnp.arange(RWKV_W) // HEAD_DIM)[None, :]).astype(F32)
    ins += [st, eye_b, eye_f, sel, head_rows, bd]
    specs += [pl.BlockSpec((None, C + 1, 2 * B, HEAD_DIM, RWKV_W), lambda g: (nC - 1 - g, 0, 0, 0, 0)),
              _full((HEAD_DIM, RWKV_W)), _full((HEAD_DIM, RWKV_W)), _full(sel.shape), _full(head_rows.shape),
              _full((256, 256))]
    sds = jax.ShapeDtypeStruct
    out_specs = tuple(blk3(0, True) for _ in range(6)) + tuple(blk3(0, False) for _ in range(6))
    res = pl.pallas_call(
        body, name="scan_bwd", grid=(nC,), in_specs=specs, out_specs=out_specs,
        out_shape=tuple(sds((B, T, RWKV_W), F32) for _ in range(12)),
        scratch_shapes=[pltpu.VMEM((2, 2 * B, HEAD_DIM, RWKV_W), MXU_DTYPE), pltpu.VMEM((2 * B, HEAD_DIM, RWKV_W), F32),
                        pltpu.VMEM((2 * B, HEAD_DIM, RWKV_W), F32)],
        compiler_params=_cp(("arbitrary",)),
    )(*ins)
    return list(res)


def _out_head_call(x2, tgt2, gate, y_att, g_att, y0, y1, shifted, kt, g_rw, w_out, g_post, gn_w, gn_b, r_k, bd, T):
    R = x2.shape[0]
    TT = min(ROW_TILE, T)
    tpe = T // TT

    def body(x_ref, t_ref, gate_ref, ya_ref, ga_ref, y0_ref, y1_ref, r_ref, v_ref, kt_ref, grw_ref, w_ref, gp_ref,
             gnw_ref, gnb_ref, rk_ref, bd_ref,
             loss_o, dy_o, dya_o, dga_o, dys_o, dr_o, dv_o, dkts_o, dgrw_o, dgate_o, gw_o, ggp_o, ggnw_o, ggnb_o, grk_o):
        i = pl.program_id(0)
        bd = bd_ref[...]
        mix = functools.partial(_mix_fn, bd=bd, diff=True)
        (ma, mr), mix_vjp = jax.vjp(mix, ya_ref[...], ga_ref[...], y0_ref[...] + y1_ref[...], r_ref[...], v_ref[...],
                                    kt_ref[0] + kt_ref[1], grw_ref[...], gnw_ref[...], gnb_ref[...], rk_ref[...])
        out = _dot(ma, w_ref[0:ATT_W, :]) + _dot(mr, w_ref[ATT_W:, :])
        loss, loss_vjp = jax.vjp(_loss_fn, out, x_ref[...], t_ref[...], gate_ref[0], gp_ref[...])
        d_out, dy, _, dgate, dgp = loss_vjp(jnp.ones((1, 1), F32))
        dy_o[...] = dy
        dma = _dot_nt(d_out, w_ref[0:ATT_W, :])
        dmr = _dot_nt(d_out, w_ref[ATT_W:, :])
        dya_o[...], dga_o[...], dys_o[...], dr_o[...], dv_o[...], dkts_o[...], dgrw_o[...], dgnw, dgnb, drk = \
            mix_vjp((dma, dmr))
        gw = jnp.concatenate([_dot_tn(ma, d_out), _dot_tn(mr, d_out)], axis=0)
        acc = ((loss_o, jnp.broadcast_to(loss, (8, 128))), (gw_o, gw), (ggp_o, dgp), (ggnw_o, dgnw), (ggnb_o, dgnb),
               (grk_o, drk))

        @pl.when(i == 0)
        def _():
            for ref, val in acc:
                ref[...] = val

        @pl.when(i > 0)
        def _():
            for ref, val in acc:
                ref[...] += val

        @pl.when(i % tpe == 0)
        def _():
            dgate_o[0] = dgate

        @pl.when(i % tpe > 0)
        def _():
            dgate_o[0] += dgate

    row = lambda w, c=0: pl.BlockSpec((TT, w), lambda i: (i, c))
    two = pl.BlockSpec((2, TT, RWKV_W), lambda i: (0, i, 0))
    per_ex = pl.BlockSpec((1, 1, D_MODEL), lambda i: (i // tpe, 0, 0))
    sds = jax.ShapeDtypeStruct
    r512 = sds((R, RWKV_W), F32)
    return pl.pallas_call(
        body, name="out_head", grid=(R // TT,),
        in_specs=[row(D_MODEL), row(D_MODEL), per_ex, row(ATT_W), row(ATT_W), row(RWKV_W), row(RWKV_W), row(RWKV_W, 0),
                  row(RWKV_W, 2), two,
                  row(RWKV_W), _full(w_out.shape), _full((1, D_MODEL)), _full((1, RWKV_W)), _full((1, RWKV_W)),
                  _full((1, RWKV_W)), _full((256, 256))],
        out_specs=(_full((8, 128)), row(D_MODEL), row(ATT_W), row(ATT_W), row(RWKV_W), row(RWKV_W), row(RWKV_W),
                   row(RWKV_W), row(RWKV_W), per_ex, _full((D_MODEL, D_MODEL)), _full((1, D_MODEL)), _full((1, RWKV_W)),
                   _full((1, RWKV_W)), _full((1, RWKV_W))),
        out_shape=(sds((8, 128), F32), sds((R, D_MODEL), F32), r512, r512, r512, r512, r512, r512, r512,
                   sds((R // T, 1, D_MODEL), F32), sds((D_MODEL, D_MODEL), F32), sds((1, D_MODEL), F32),
                   sds((1, RWKV_W), F32), sds((1, RWKV_W), F32), sds((1, RWKV_W), F32)),
        compiler_params=_cp(("arbitrary",)),
    )(x2, tgt2, gate, y_att, g_att, y0, y1, shifted, shifted, kt, g_rw, w_out, g_post, gn_w, gn_b, r_k, bd)


def _in_proj_bwd_call(x2, dy, shift, scale, g_pre, w_in, qg, kg, cos, sin, bd, q_raw, k_raw, dqr, dkp, dvp,
                      d_gatt, d_rin, d_grw, T):
    R = x2.shape[0]
    TT = min(ROW_TILE, T)
    tpe = T // TT

    def body(x_ref, dy_ref, sh_ref, sc_ref, gp_ref, w_ref, qg_ref, kg_ref, cos_ref, sin_ref, bd_ref, q_ref, k_ref,
             dqr_ref, dkp_ref, dvp_ref, dga_ref, drin_ref, dgrw_ref,
             dx_o, dproj_o, dsh_o, dsc_o, ggp_o, gqg_o, gkg_o):
        i = pl.program_id(0)
        cos, sin, bd = cos_ref[...], sin_ref[...], bd_ref[...]
        left = lax.broadcasted_iota(jnp.int32, (1, KV_W), 1) < HEAD_DIM

        def kv_grad(ref):
            a = ref[0] + ref[1]
            b = ref[2] + ref[3]
            return jnp.where(left, a + pltpu.roll(a, HEAD_DIM, 1), b + pltpu.roll(b, HEAD_DIM, 1))

        qfn = functools.partial(_qk_fn, cos=jnp.tile(cos, (1, 4)), sin=jnp.tile(sin, (1, 4)), bd=bd, scale=ATT_SCALE,
                                diff=True)
        _, q_vjp = jax.vjp(qfn, q_ref[...], qg_ref[...])
        dq, gqg = q_vjp(dqr_ref[...])
        kfn = functools.partial(_qk_fn, cos=cos, sin=sin, bd=bd, scale=1.0, diff=True)
        _, k_vjp = jax.vjp(kfn, k_ref[...], kg_ref[...])
        dk, gkg = k_vjp(kv_grad(dkp_ref))
        pieces = ((C_Q, C_K, dq), (C_K, C_V, dk), (C_V, C_GA, kv_grad(dvp_ref)), (C_GA, C_RIN, dga_ref[...]),
                  (C_RIN, C_GRW, drin_ref[...]), (C_GRW, C_END, dgrw_ref[...]))
        dh = jnp.zeros((TT, D_MODEL), F32)
        for c0, c1, val in pieces:
            vb = val.astype(MXU_DTYPE)
            dproj_o[:, c0:c1] = vb
            dh = dh + _dot(vb, w_ref[c0:c1, :])
        _, pre_vjp = jax.vjp(_pre_fn, x_ref[...], sh_ref[0], sc_ref[0], gp_ref[...])
        dx, dsh, dsc, ggp = pre_vjp(dh)
        dx_o[...] = dx + dy_ref[...]
        acc = ((ggp_o, ggp), (gqg_o, gqg), (gkg_o, gkg))

        @pl.when(i == 0)
        def _():
            for ref, val in acc:
                ref[...] = val

        @pl.when(i > 0)
        def _():
            for ref, val in acc:
                ref[...] += val

        @pl.when(i % tpe == 0)
        def _():
            dsh_o[0] = dsh
            dsc_o[0] = dsc

        @pl.when(i % tpe > 0)
        def _():
            dsh_o[0] += dsh
            dsc_o[0] += dsc

    row = lambda w: pl.BlockSpec((TT, w), lambda i: (i, 0))
    per_ex = pl.BlockSpec((1, 1, D_MODEL), lambda i: (i // tpe, 0, 0))
    tab = pl.BlockSpec((TT, KV_W), lambda i: (i % tpe, 0))
    pad = pl.BlockSpec((4, TT, KV_W), lambda i: (0, i, 0))
    sds = jax.ShapeDtypeStruct
    nb = R // T
    return pl.pallas_call(
        body, name="in_proj_bwd", grid=(R // TT,),
        in_specs=[row(D_MODEL), row(D_MODEL), per_ex, per_ex, _full((1, D_MODEL)), _full(w_in.shape), _full((1, ATT_W)),
                  _full((1, KV_W)), tab, tab, _full((256, 256)), row(ATT_W), row(KV_W), row(ATT_W), pad, pad,
                  row(ATT_W), row(SHIFT_W), row(RWKV_W)],
        out_specs=(row(D_MODEL), row(C_END), per_ex, per_ex, _full((1, D_MODEL)), _full((1, ATT_W)), _full((1, KV_W))),
        out_shape=(sds((R, D_MODEL), F32), sds((R, C_END), MXU_DTYPE), sds((nb, 1, D_MODEL), F32),
                   sds((nb, 1, D_MODEL), F32), sds((1, D_MODEL), F32), sds((1, ATT_W), F32), sds((1, KV_W), F32)),
        compiler_params=_cp(("arbitrary",)),
    )(x2, dy, shift, scale, g_pre, w_in, qg, kg, cos, sin, bd, q_raw, k_raw, dqr, dkp, dvp, d_gatt, d_rin, d_grw)


def _w_in_grad_call(hb, dproj):
    R = hb.shape[0]
    TT = min(W_GRAD_ROWS, R)
    CB = 1152

    def body(h_ref, d_ref, o_ref):
        g = _dot_tn(h_ref[...], d_ref[...])

        @pl.when(pl.program_id(1) == 0)
        def _():
            o_ref[...] = g

        @pl.when(pl.program_id(1) > 0)
        def _():
            o_ref[...] += g

    return pl.pallas_call(
        body, name="w_in_grad", grid=(C_END // CB, R // TT),
        in_specs=[pl.BlockSpec((TT, D_MODEL), lambda j, i: (i, 0)), pl.BlockSpec((TT, CB), lambda j, i: (i, j))],
        out_specs=pl.BlockSpec((D_MODEL, CB), lambda j, i: (0, j)),
        out_shape=jax.ShapeDtypeStruct((D_MODEL, C_END), F32), compiler_params=_cp(("arbitrary", "arbitrary")),
    )(hb, dproj)


def _adam_refs(p_ref, w_ref, m_ref, v_ref, g_o, d_o, m_o, v_o):
    g = p_ref[0].astype(F32)
    for j in range(1, p_ref.shape[0]):
        g = g + p_ref[j].astype(F32)
    m2 = ADAM_B1 * m_ref[...] + (1.0 - ADAM_B1) * g
    v2 = ADAM_B2 * v_ref[...] + (1.0 - ADAM_B2) * jnp.square(g)
    m_hat = m2 / (1.0 - ADAM_B1 ** ADAM_STEP)
    v_hat = v2 / (1.0 - ADAM_B2 ** ADAM_STEP)
    g_o[...] = g
    d_o[...] = -ADAM_LR * (m_hat / (jnp.sqrt(v_hat) + ADAM_EPS) + ADAM_WD * w_ref[...])
    m_o[...] = m2
    v_o[...] = v2


def _adam_small_call(items, name):
    n = len(items)

    def body(*refs):
        for k in range(n):
            _adam_refs(*refs[4 * k:4 * k + 4], *refs[4 * n + 4 * k:4 * n + 4 * k + 4])

    out_shape = tuple(jax.ShapeDtypeStruct(w.shape, F32) for _, w, _, _ in items for _ in range(4))
    out = pl.pallas_call(body, name=name, out_shape=out_shape)(*[a for item in items for a in item])
    return [out[4 * k:4 * k + 4] for k in range(n)]


def _adam_call(parts, w, m, v, name, row_tile=None):
    P, M, N = parts.shape
    TM = M if row_tile is None else row_tile

    def body(*refs):
        _adam_refs(*refs)

    blk = pl.BlockSpec((TM, N), lambda i: (i, 0))
    return pl.pallas_call(
        body, name=name, grid=(M // TM,),
        in_specs=[pl.BlockSpec((P, TM, N), lambda i: (0, i, 0)), blk, blk, blk], out_specs=(blk,) * 4,
        out_shape=(jax.ShapeDtypeStruct((M, N), F32),) * 4, compiler_params=_cp(("arbitrary",)),
    )(parts, w, m, v)


_SMALL_ROWS = 136


def _pack_small(taps, w_up, w0, a_up, a0):
    flat = jnp.concatenate([taps.reshape(-1), w_up.reshape(-1), w0.reshape(-1), a_up.reshape(-1), a0.reshape(-1)])
    return jnp.pad(flat, (0, _SMALL_ROWS * 128 - flat.shape[0])).reshape(_SMALL_ROWS, 128)


def _unpack_small(packed):
    n = packed.shape[0]
    flat = packed.reshape(n, -1)
    out, o = [], 0
    for shape in ((3, 208), (2, 64, 64), (2, 64), (2, 64, 64), (2, 64)):
        size = 1
        for s in shape:
            size *= s
        out.append(flat[:, o:o + size].reshape((n,) + shape))
        o += size
    return out


def _cols_to_full(blocks):
    nd = blocks.ndim
    moved = jnp.moveaxis(blocks, 0, nd - 2)
    return moved.reshape(moved.shape[:-2] + (moved.shape[-2] * moved.shape[-1],))


def _full_to_cols(full):
    k = full.shape[-1] // NDEV
    return jnp.moveaxis(full.reshape(full.shape[:-1] + (NDEV, k)), -2, 0)


_REP_SIZES = (("g_pre", 1024), ("q_norm_g", 64), ("k_norm_g", 64), ("k_k", 512), ("k_a", 512), ("r_k", 512),
              ("gn_w", 512), ("gn_b", 512), ("g_post", 1024))
_REP_ROWS = 40


def kernel(x, c, w_ada, b_ada, g_pre, w_in, q_norm_g, k_norm_g, shift_taps, w_up, w0, a_up, a0, k_k, k_a, r_k, gn_w, gn_b, w_out, g_post, loss_target, m_w_ada, m_b_ada, m_g_pre, m_w_in, m_q_norm_g, m_k_norm_g, m_shift_taps, m_w_up, m_w0, m_a_up, m_a0, m_k_k, m_k_a, m_r_k, m_gn_w, m_gn_b, m_w_out, m_g_post, v_w_ada, v_b_ada, v_g_pre, v_w_in, v_q_norm_g, v_k_norm_g, v_shift_taps, v_w_up, v_w0, v_a_up, v_a0, v_k_k, v_k_a, v_r_k, v_gn_w, v_gn_b, v_w_out, v_g_post):
    B, T, _ = x.shape
    R = B * T
    me = 4 * lax.axis_index("x") + 2 * lax.axis_index("y") + lax.axis_index("c")
    x2 = x.reshape(R, D_MODEL)
    tgt2 = loss_target.reshape(R, D_MODEL)

    seg = jnp.arange(256) // HEAD_DIM
    bd = (seg[:, None] == seg[None, :]).astype(MXU_DTYPE)
    eye = (jnp.arange(HEAD_DIM)[:, None] == (jnp.arange(RWKV_W) % HEAD_DIM)[None, :])
    eye_b, eye_f = eye.astype(MXU_DTYPE), eye.astype(F32)
    cos, sin = _rope_tables(T)

    c_g, w_in_g, w_out_g, small_g = _exchange(
        [c, w_in[0].T.astype(MXU_DTYPE), w_out[0].astype(MXU_DTYPE),
         _pack_small(shift_taps[0], w_up[0], w0[0], a_up[0], a0[0])], ["all"] * 4, "gather_params")
    c_all = c_g.reshape(NDEV * B, D_MODEL)
    w_in_f = w_in_g.reshape(C_END, D_MODEL)
    w_out_f = w_out_g.reshape(D_MODEL, D_MODEL)
    taps_b, w_up_b, w0_b, a_up_b, a0_b = _unpack_small(small_g)
    taps_f = jnp.pad(_cols_to_full(taps_b), ((0, 5), (0, 0)))
    w_up_f, a_up_f = _cols_to_full(w_up_b), _cols_to_full(a_up_b)
    w0_f, a0_f = _cols_to_full(w0_b), _cols_to_full(a0_b)
    wup_pad = jnp.pad(w_up_f, ((0, 0), (0, 64), (0, 0))).astype(MXU_DTYPE)
    aup_pad = jnp.pad(a_up_f, ((0, 0), (64, 0), (0, 0))).astype(MXU_DTYPE)

    ncol = w_ada.shape[2]
    b_cols = lax.dynamic_slice(b_ada, (0, me * ncol), (1, ncol))
    mod_cols = _mod_call(c_all, w_ada[0].astype(MXU_DTYPE), b_cols)
    (mod_g,) = _exchange([mod_cols], ["all"], "gather_mod")
    mod = lax.dynamic_slice(_cols_to_full(mod_g), (me * B, 0), (B, 3 * D_MODEL))
    shift, scale, gate = [mod[:, j * D_MODEL:(j + 1) * D_MODEL].reshape(B, 1, D_MODEL) for j in range(3)]

    qg = jnp.tile(q_norm_g, (1, ATT_W // HEAD_DIM))
    kg = jnp.tile(k_norm_g, (1, KV_W // HEAD_DIM))
    rk_row = r_k.reshape(1, RWKV_W)

    hb, qr, kpad, vpad, q_raw, k_raw, g_att, rin, g_rw = _in_proj_call(
        x2, shift, scale, g_pre, w_in_f, qg, kg, cos, sin, bd, T)
    y_att = _att_fwd_call(qr, kpad, vpad, B, T)
    shifted = _shift_fwd_call(rin, taps_f, T)
    w_s, kt_s, akk_s, kk_s = _rwkv_prep_call(shifted, wup_pad, aup_pad, w0_f, a0_f, k_k, k_a, bd, T)
    sh3 = shifted.reshape(B, T, SHIFT_W)
    r4 = lambda a: a.reshape(2, B, T, RWKV_W)
    y0, y1, st = _scan_fwd_call(r4(w_s), r4(kt_s), r4(akk_s), kk_s.reshape(B, T, RWKV_W), sh3, eye_b, eye_f, bd, B, T)

    (loss_blk, dy, d_yatt, d_gatt, d_ys, d_r2, d_v2, d_kts, d_grw, d_gate, g_wout, g_gpost, g_gnw, g_gnb,
     g_rk) = _out_head_call(x2, tgt2, gate, y_att, g_att, y0.reshape(R, RWKV_W), y1.reshape(R, RWKV_W), shifted, kt_s,
                            g_rw, w_out_f, g_post, gn_w, gn_b, rk_row, bd, T)
    v_heads = sh3[:, :, 2 * RWKV_W:3 * RWKV_W].reshape(B, T, RWKV_W // HEAD_DIM, HEAD_DIM)
    scan_cts = _scan_bwd_call(r4(w_s), r4(kt_s), r4(akk_s), kk_s.reshape(B, T, RWKV_W), sh3, v_heads,
                              d_ys.reshape(B, T, RWKV_W), st, eye_b, eye_f, bd, B, T)
    scan_cts = [a.reshape(R, RWKV_W) for a in scan_cts]
    d_shifted, g_wup, g_aup, g_w0, g_a0, g_kk, g_ka = _rwkv_prep_bwd_call(
        shifted, scan_cts + [d_r2, d_v2, d_kts], wup_pad, aup_pad, w0_f, a0_f, k_k, k_a, bd, T)
    d_rin, g_taps = _shift_bwd_call(rin, d_shifted, taps_f, T)
    dqr, dkp, dvp = _att_bwd_call(qr, kpad, vpad, d_yatt, B, T)
    grad_x, dproj, d_shift, d_scale, g_gpre, g_qg, g_kg = _in_proj_bwd_call(
        x2, dy, shift, scale, g_pre, w_in_f, qg, kg, cos, sin, bd, q_raw, k_raw, dqr, dkp, dvp, d_gatt, d_rin, d_grw, T)
    g_win = _w_in_grad_call(hb, dproj)

    rep = jnp.concatenate([g_gpre.reshape(-1), g_qg.reshape(-1, HEAD_DIM).sum(0), g_kg.reshape(-1, HEAD_DIM).sum(0),
                           g_kk.reshape(-1), g_ka.reshape(-1), g_rk.reshape(-1), g_gnw.reshape(-1), g_gnb.reshape(-1),
                           g_gpost.reshape(-1), loss_blk[0, :1]])
    rep = jnp.pad(rep, (0, _REP_ROWS * 128 - rep.shape[0])).reshape(_REP_ROWS, 128)
    dmod = jnp.concatenate([d_shift, d_scale, d_gate], axis=2).reshape(B, 3 * D_MODEL)
    small_parts = jax.vmap(_pack_small)(_full_to_cols(g_taps[:3]), _full_to_cols(g_wup[:, :64, :]), _full_to_cols(g_w0),
                                        _full_to_cols(g_aup[:, 64:, :]), _full_to_cols(g_a0))
    core = lax.axis_index("c")
    halves = [a.reshape((NDEV // 2, 2) + a.shape[1:]).astype(MXU_DTYPE)
              for a in (_full_to_cols(g_win), g_wout.reshape(NDEV, D_MODEL // NDEV, D_MODEL))]
    pick = lambda a, j: lax.dynamic_index_in_dim(a, j, axis=1, keepdims=False)
    s_win, s_wout = _pair_sum_call([pick(a, core) for a in halves], [pick(a, 1 - core) for a in halves], "reduce_pair")
    p_win, p_wout, p_small, dmod_g, rep_g = _exchange(
        [s_win, s_wout, small_parts, dmod, rep], ["chips", "chips", "scatter", "all", "all"], "reduce_grads")
    dmod_all = dmod_g.reshape(NDEV * B, 3 * D_MODEL)
    g_wada = _wada_grad_call(c_all, lax.dynamic_slice(dmod_all, (0, me * ncol), (NDEV * B, ncol)))

    res, small = {}, []

    def adam(name, parts, w, m, v, row_tile=None, alone=False):
        two_d = (-1, w.shape[-1])
        item = (parts.reshape((parts.shape[0],) + w.reshape(two_d).shape), w.reshape(two_d), m.reshape(two_d),
                v.reshape(two_d))
        if alone:
            res[name] = [o.reshape(w.shape) for o in _adam_call(*item, "adam_" + name, row_tile)]
        else:
            small.append((name, w.shape, item))

    adam("w_ada", g_wada[None], w_ada, m_w_ada, v_w_ada, alone=True)
    adam("b_ada", dmod_all.reshape(NDEV * B, 1, 3 * D_MODEL), b_ada, m_b_ada, v_b_ada)
    adam("w_in", p_win, w_in, m_w_in, v_w_in, 128, alone=True)
    adam("w_out", p_wout, w_out, m_w_out, v_w_out, alone=True)
    taps_p, wup_p, w0_p, aup_p, a0_p = _unpack_small(p_small)
    adam("shift_taps", taps_p, shift_taps, m_shift_taps, v_shift_taps)
    adam("w_up", wup_p, w_up, m_w_up, v_w_up)
    adam("w0", w0_p, w0, m_w0, v_w0)
    adam("a_up", aup_p, a_up, m_a_up, v_a_up)
    adam("a0", a0_p, a0, m_a0, v_a0)
    rep_flat = rep_g.reshape(NDEV, -1)
    off = 0
    given = dict(g_pre=(g_pre, m_g_pre, v_g_pre), q_norm_g=(q_norm_g, m_q_norm_g, v_q_norm_g),
                 k_norm_g=(k_norm_g, m_k_norm_g, v_k_norm_g), k_k=(k_k, m_k_k, v_k_k), k_a=(k_a, m_k_a, v_k_a),
                 r_k=(r_k, m_r_k, v_r_k), gn_w=(gn_w, m_gn_w, v_gn_w), gn_b=(gn_b, m_gn_b, v_gn_b),
                 g_post=(g_post, m_g_post, v_g_post))
    for name, size in _REP_SIZES:
        adam(name, rep_flat[:, off:off + size], *given[name])
        off += size
    for (name, shape, _), out in zip(small, _adam_small_call([item for _, _, item in small], "adam_small")):
        res[name] = [o.reshape(shape) for o in out]

    loss = jnp.sum(rep_flat[:, off])
    order = ["w_ada", "b_ada", "g_pre", "w_in", "q_norm_g", "k_norm_g", "shift_taps", "w_up", "w0", "a_up", "a0", "k_k",
             "k_a", "r_k", "gn_w", "gn_b", "w_out", "g_post"]
    return (loss, grad_x.reshape(B, T, D_MODEL), *[res[n][0] for n in order], *[res[n][1] for n in order],
            *[res[n][2] for n in order], *[res[n][3] for n in order])
```

```python
import functools

import jax
import jax.numpy as jnp
from jax import lax
from jax.experimental import pallas as pl
from jax.experimental.pallas import tpu as pltpu

F32 = jnp.float32
MXU_DTYPE = jnp.bfloat16
MESH = pl.DeviceIdType.MESH
NDEV = 8

D_MODEL = 1024
HEAD_DIM = 64
ATT_W = 512
KV_W = 128
RWKV_W = 512
LORA_W = 128
SHIFT_W = 3 * RWKV_W + LORA_W
GRID_W = 64
ROPE_THETA = 10000.0
DECAY_SCALE = 0.6065306597126334
NORM_EPS = 1e-6
GN_EPS = 64e-5
L2_EPS = 1e-12
ATT_SCALE = HEAD_DIM ** -0.5
C_Q, C_K, C_V, C_GA, C_RIN, C_GRW, C_END = 0, 512, 640, 768, 1280, 2944, 3456

ADAM_LR, ADAM_B1, ADAM_B2, ADAM_EPS, ADAM_WD, ADAM_STEP = 0.001, 0.9, 0.999, 1e-08, 0.01, 10

ROW_TILE = 256
W_GRAD_ROWS = 2048
ATT_TILE_FWD = 256
ATT_TILE_BWD = 512
SCAN_CHUNK = 64
SCAN_UNROLL = 16
VMEM_LIMIT = 56 * 1024 * 1024


def _cp(sem=None):
    return pltpu.CompilerParams(dimension_semantics=sem, vmem_limit_bytes=VMEM_LIMIT)


def _dot(a, b, dims=(((1,), (0,)), ((), ()))):
    return lax.dot_general(a.astype(MXU_DTYPE), b.astype(MXU_DTYPE), dims, preferred_element_type=F32)


def _dot_nt(a, b):
    return _dot(a, b, (((1,), (1,)), ((), ())))


def _dot_tn(a, b):
    return _dot(a, b, (((0,), (0,)), ((), ())))


def _seg_dot(xb, bd):
    n = xb.shape[1]
    if n <= 256:
        return jnp.dot(xb, bd[:n, :n], preferred_element_type=F32)
    parts = [jnp.dot(xb[:, c:c + 256], bd, preferred_element_type=F32) for c in range(0, n, 256)]
    return jnp.concatenate(parts, axis=1)


def _segsum_raw(x, bd):
    rows = x.shape[0]
    hi = x.astype(MXU_DTYPE)
    lo = (x - hi.astype(F32)).astype(MXU_DTYPE)
    both = _seg_dot(jnp.concatenate([hi, lo], axis=0), bd)
    return both[:rows] + both[rows:]


@jax.custom_vjp
def _segsum_d(x, bd):
    return _segsum_raw(x, bd)


def _segsum_d_fwd(x, bd):
    return _segsum_raw(x, bd), bd


def _segsum_d_bwd(bd, ct):
    return _segsum_raw(ct, bd), jnp.zeros_like(bd)


_segsum_d.defvjp(_segsum_d_fwd, _segsum_d_bwd)


def _rope_tables(T):
    t = jnp.arange(T, dtype=F32)
    row = jnp.floor(t / GRID_W)
    col = t - row * GRID_W
    n_freq = HEAD_DIM // 4
    inv_freq = ROPE_THETA ** (-jnp.arange(n_freq, dtype=F32) / n_freq)
    d = jnp.arange(HEAD_DIM)
    pos = jnp.where((d < HEAD_DIM // 2)[None, :], row[:, None], col[:, None])
    ang = pos * inv_freq[d % n_freq][None, :]
    sign = jnp.where((d % 32) < 16, -1.0, 1.0).astype(F32)[None, :]
    cos = jnp.cos(ang)
    sin = jnp.sin(ang) * sign
    return jnp.tile(cos, (1, 2)), jnp.tile(sin, (1, 2))


def _rope_raw(x, cos, sin):
    n = x.shape[1]
    lane = lax.broadcasted_iota(jnp.int32, (1, n), 1)
    first = (lane % 32) < 16
    partner = jnp.where(first, pltpu.roll(x, n - 16, 1), pltpu.roll(x, 16, 1))
    return x * cos + partner * sin


@jax.custom_vjp
def _rope_d(x, cos, sin):
    return _rope_raw(x, cos, sin)


def _rope_d_fwd(x, cos, sin):
    return _rope_raw(x, cos, sin), (cos, sin)


def _rope_d_bwd(res, ct):
    cos, sin = res
    return _rope_raw(ct, cos, -sin), jnp.zeros_like(cos), jnp.zeros_like(sin)


_rope_d.defvjp(_rope_d_fwd, _rope_d_bwd)


def _rms(x, g):
    return x * lax.rsqrt(jnp.mean(x * x, axis=-1, keepdims=True) + NORM_EPS) * g


def _pre_fn(x, shift, scale, g_pre):
    return _rms(x, g_pre) * (1.0 + scale) + shift


def _qk_fn(q, g, cos, sin, bd, scale, diff):
    segsum = _segsum_d if diff else _segsum_raw
    rope = _rope_d if diff else _rope_raw
    qn = q * lax.rsqrt(segsum(q * q, bd) * (1.0 / HEAD_DIM) + NORM_EPS) * g
    return rope(qn, cos, sin) * scale


def _silu(x):
    return x * jax.nn.sigmoid(x)


def _rwkv_pw(k, pw0, pw1, pa0, pa1, w0, a0, k_k, k_a, bd, diff):
    segsum = _segsum_d if diff else _segsum_raw
    kk = k * k_k
    kk = kk * lax.rsqrt(segsum(kk * kk, bd) + L2_EPS)
    ws, kts, akks = [], [], []
    for z, (pw, pa) in enumerate(((pw0, pa0), (pw1, pa1))):
        w = jnp.exp(-DECAY_SCALE * jax.nn.sigmoid(w0[z:z + 1, :] + pw))
        a = jax.nn.sigmoid(a0[z:z + 1, :] + pa)
        ws.append(w)
        kts.append(k * (1.0 + (a - 1.0) * k_a))
        akks.append(a * kk)
    return ws[0], ws[1], kts[0], kts[1], akks[0], akks[1], kk


def _mix_fn(y_att, g_att, ys, r, v, kts, g_rw, gn_w, gn_b, r_k, bd, diff):
    segsum = _segsum_d if diff else _segsum_raw
    mu = segsum(ys, bd) * (1.0 / HEAD_DIM)
    d = ys - mu
    var = segsum(d * d, bd) * (1.0 / HEAD_DIM)
    yn = d * lax.rsqrt(var + GN_EPS) * gn_w + gn_b
    bonus = segsum(r * kts * r_k, bd) * v
    return y_att * _silu(g_att), (yn + bonus) * _silu(g_rw)


def _loss_fn(out, x, tgt, gate, g_post):
    e = x + gate * _rms(out, g_post) - tgt
    s = jnp.sum(e * e, axis=1, keepdims=True)
    return jnp.sum(s, axis=0, keepdims=True) * (0.5 / D_MODEL)


def _exchange(arrays, modes, name):
    n = len(arrays)
    out_shape = tuple(
        jax.ShapeDtypeStruct(((NDEV,) + tuple(a.shape)) if mode == "all" else tuple(a.shape), a.dtype)
        for a, mode in zip(arrays, modes))
    chips = (4, 2, 6)

    def body(*refs):
        ins, outs = refs[:n], refs[n:2 * n]
        send_sems, recv_sems, local_sems = refs[2 * n:]
        ix, iy, ic = lax.axis_index("x"), lax.axis_index("y"), lax.axis_index("c")
        me = 4 * ix + 2 * iy + ic

        def peer(m):
            px = 1 - ix if (m >> 2) & 1 else ix
            py = 1 - iy if (m >> 1) & 1 else iy
            pc = 1 - ic if m & 1 else ic
            return (px, py, pc), 4 * px + 2 * py + pc

        def copy(k, j, src_ref, slot, to):
            return pltpu.make_async_remote_copy(src_ref=src_ref, dst_ref=outs[k].at[slot], send_sem=send_sems.at[k, j],
                                                recv_sem=recv_sems.at[k, j], device_id=to, device_id_type=MESH)

        local, sends, arrivals, forwards = [], [], [], []
        for k in range(n):
            if modes[k] == "scatter":
                local.append(pltpu.make_async_copy(ins[k].at[me], outs[k].at[me], local_sems.at[k]))
                for m in range(1, NDEV):
                    to, p = peer(m)
                    sends.append(copy(k, m - 1, ins[k].at[p], me, to))
                    arrivals.append(copy(k, m - 1, ins[k].at[p], p, to))
            elif modes[k] == "chips":
                mine = me // 2
                local.append(pltpu.make_async_copy(ins[k].at[mine], outs[k].at[mine], local_sems.at[k]))
                for j, m in enumerate(chips):
                    to, p = peer(m)
                    sends.append(copy(k, j, ins[k].at[p // 2], mine, to))
                    arrivals.append(copy(k, j, ins[k].at[p // 2], p // 2, to))
            else:
                local.append(pltpu.make_async_copy(ins[k], outs[k].at[me], local_sems.at[k]))
                sib, sib_slot = peer(1)
                sends.append(copy(k, 0, ins[k], me, sib))
                for j, m in enumerate(chips):
                    to, p = peer(m)
                    sends.append(copy(k, 1 + j, ins[k], me, to))
                    forwards.append((copy(k, 1 + j, ins[k], p, to), copy(k, 4 + j, outs[k].at[p], p, sib)))
                    arrivals.append(copy(k, 4 + j, ins[k], peer(m ^ 1)[1], sib))
                arrivals.append(copy(k, 0, ins[k], sib_slot, sib))
        for cp in local + sends:
            cp.start()
        for arrived, onward in forwards:
            arrived.wait_recv()
            onward.start()
        for cp in arrivals:
            cp.wait_recv()
        for cp in sends + [onward for _, onward in forwards]:
            cp.wait_send()
        for cp in local:
            cp.wait()

    any_spec = pl.BlockSpec(memory_space=pl.ANY)
    return pl.pallas_call(
        body, name=name, out_shape=out_shape,
        in_specs=[any_spec] * n, out_specs=tuple([any_spec] * n),
        scratch_shapes=[pltpu.SemaphoreType.DMA((n, NDEV - 1)), pltpu.SemaphoreType.DMA((n, NDEV - 1)),
                        pltpu.SemaphoreType.DMA((n,))],
    )(*arrays)


def _pair_sum_call(mine, send, name):
    n = len(mine)

    def body(*refs):
        mine_r, send_r, out_r, land_r = (refs[j * n:(j + 1) * n] for j in range(4))
        send_sems, recv_sems = refs[4 * n:]
        sibling = (lax.axis_index("x"), lax.axis_index("y"), 1 - lax.axis_index("c"))
        swaps = [pltpu.make_async_remote_copy(src_ref=send_r[k], dst_ref=land_r[k], send_sem=send_sems.at[k],
                                              recv_sem=recv_sems.at[k], device_id=sibling, device_id_type=MESH)
                 for k in range(n)]
        for cp in swaps:
            cp.start()
        for k, cp in enumerate(swaps):
            cp.wait()
            out_r[k][...] = (mine_r[k][...].astype(F32) + land_r[k][...].astype(F32)).astype(out_r[k].dtype)

    return pl.pallas_call(
        body, name=name, out_shape=tuple(jax.ShapeDtypeStruct(a.shape, a.dtype) for a in mine),
        scratch_shapes=[pltpu.VMEM(a.shape, a.dtype) for a in mine] + [pltpu.SemaphoreType.DMA((n,)),
                                                                         pltpu.SemaphoreType.DMA((n,))],
        compiler_params=pltpu.CompilerParams(vmem_limit_bytes=VMEM_LIMIT),
    )(*mine, *send)


def _mod_call(c_all, w_ada, b_cols):
    def body(c_ref, w_ref, b_ref, o_ref):
        o_ref[...] = _dot(_silu(c_ref[...]), w_ref[...]) + b_ref[...]

    return pl.pallas_call(body, name="mod_fwd",
                          out_shape=jax.ShapeDtypeStruct((c_all.shape[0], w_ada.shape[1]), F32))(c_all, w_ada, b_cols)


def _wada_grad_call(c_all, dmod_cols):
    def body(c_ref, d_ref, o_ref):
        o_ref[...] = _dot_tn(_silu(c_ref[...]), d_ref[...])

    return pl.pallas_call(body, name="w_ada_grad",
                          out_shape=jax.ShapeDtypeStruct((c_all.shape[1], dmod_cols.shape[1]), F32))(c_all, dmod_cols)


def _full(shape):
    nd = len(shape)
    return pl.BlockSpec(shape, lambda *_: (0,) * nd)


def _in_proj_call(x2, shift, scale, g_pre, w_in, qg, kg, cos, sin, bd, T):
    R = x2.shape[0]
    TT = min(ROW_TILE, T)
    tpe = T // TT

    def body(x_ref, sh_ref, sc_ref, gp_ref, w_ref, qg_ref, kg_ref, cos_ref, sin_ref, bd_ref,
             hb_ref, qr_ref, kpad_ref, vpad_ref, qraw_ref, kraw_ref, gatt_ref, rin_ref, grw_ref):
        h = _pre_fn(x_ref[...], sh_ref[0], sc_ref[0], gp_ref[...])
        hb = h.astype(MXU_DTYPE)
        hb_ref[...] = hb

        def proj(c0, c1):
            return _dot_nt(hb, w_ref[c0:c1, :])

        q = proj(C_Q, C_K)
        k = proj(C_K, C_V)
        v = proj(C_V, C_GA)
        gatt_ref[...] = proj(C_GA, C_RIN)
        rin_ref[...] = proj(C_RIN, C_GRW)
        grw_ref[...] = proj(C_GRW, C_END)
        qraw_ref[...] = q
        kraw_ref[...] = k
        cos, sin, bd = cos_ref[...], sin_ref[...], bd_ref[...]
        qr = _qk_fn(q, qg_ref[...], jnp.tile(cos, (1, 4)), jnp.tile(sin, (1, 4)), bd, ATT_SCALE, False)
        qr_ref[...] = qr.astype(MXU_DTYPE)
        kr = _qk_fn(k, kg_ref[...], cos, sin, bd, 1.0, False)
        left = lax.broadcasted_iota(jnp.int32, (1, KV_W), 1) < HEAD_DIM
        for ref, val in ((kpad_ref, kr), (vpad_ref, v)):
            h0l = jnp.where(left, val, 0.0)
            h1r = jnp.where(left, 0.0, val)
            ref[0] = h0l.astype(MXU_DTYPE)
            ref[1] = pltpu.roll(h0l, HEAD_DIM, 1).astype(MXU_DTYPE)
            ref[2] = pltpu.roll(h1r, HEAD_DIM, 1).astype(MXU_DTYPE)
            ref[3] = h1r.astype(MXU_DTYPE)

    row = lambda w: pl.BlockSpec((TT, w), lambda i: (i, 0))
    per_ex = pl.BlockSpec((1, 1, D_MODEL), lambda i: (i // tpe, 0, 0))
    tab = pl.BlockSpec((TT, KV_W), lambda i: (i % tpe, 0))
    pad = pl.BlockSpec((4, TT, KV_W), lambda i: (0, i, 0))
    sds = jax.ShapeDtypeStruct
    return pl.pallas_call(
        body, name="in_proj", grid=(R // TT,),
        in_specs=[row(D_MODEL), per_ex, per_ex, _full((1, D_MODEL)), _full(w_in.shape), _full((1, ATT_W)),
                  _full((1, KV_W)), tab, tab, _full((256, 256))],
        out_specs=(row(D_MODEL), row(ATT_W), pad, pad, row(ATT_W), row(KV_W), row(ATT_W), row(SHIFT_W), row(RWKV_W)),
        out_shape=(sds((R, D_MODEL), MXU_DTYPE), sds((R, ATT_W), MXU_DTYPE), sds((4, R, KV_W), MXU_DTYPE),
                   sds((4, R, KV_W), MXU_DTYPE), sds((R, ATT_W), F32), sds((R, KV_W), F32), sds((R, ATT_W), F32),
                   sds((R, SHIFT_W), F32), sds((R, RWKV_W), F32)),
        compiler_params=_cp(("arbitrary",)),
    )(x2, shift, scale, g_pre, w_in, qg, kg, cos, sin, bd)


def _softmax_parts(s):
    e = jnp.exp(s - jnp.max(s, axis=1, keepdims=True))
    return e, 1.0 / jnp.sum(e, axis=1, keepdims=True)


def _att_specs(T, TQ):
    nq = T // TQ
    qspec = pl.BlockSpec((TQ, KV_W), lambda b, p, i: (b * nq + i, p))
    side = lambda s: pl.BlockSpec((None, T, KV_W), lambda b, p, i: (2 * (p // 2) + s, b, 0))
    return nq, qspec, side


def _att_fwd_call(qr, kpad, vpad, B, T):
    TQ = min(ATT_TILE_FWD, T)
    nq, qspec, side = _att_specs(T, TQ)

    def body(q_ref, kl_ref, kr_ref, vl_ref, vr_ref, o_ref):
        q = q_ref[...]
        ea, inv_a = _softmax_parts(_dot_nt(q, kl_ref[...]))
        eb, inv_b = _softmax_parts(_dot_nt(q, kr_ref[...]))
        o_ref[...] = _dot(ea, vl_ref[...]) * inv_a + _dot(eb, vr_ref[...]) * inv_b

    return pl.pallas_call(
        body, name="att_fwd", grid=(B, 4, nq),
        in_specs=[qspec, side(0), side(1), side(0), side(1)], out_specs=qspec,
        out_shape=jax.ShapeDtypeStruct((B * T, ATT_W), F32),
        compiler_params=_cp(("arbitrary",) * 3),
    )(qr, kpad, kpad, vpad, vpad)


def _att_bwd_call(qr, kpad, vpad, d_o, B, T):
    TQ = min(ATT_TILE_BWD, T)
    nq, qspec, side = _att_specs(T, TQ)

    def body(q_ref, kl_ref, kr_ref, vl_ref, vr_ref, do_ref, dq_ref, dk_ref, dv_ref):
        i = pl.program_id(2)
        q, do = q_ref[...], do_ref[...]
        left = lax.broadcasted_iota(jnp.int32, (1, KV_W), 1) < HEAD_DIM
        dq = jnp.zeros((TQ, KV_W), F32)
        dk = jnp.zeros((T, KV_W), F32)
        dv = jnp.zeros((T, KV_W), F32)
        for k_ref, v_ref, mask in ((kl_ref, vl_ref, left), (kr_ref, vr_ref, jnp.logical_not(left))):
            kk, vv = k_ref[...], v_ref[...]
            e, inv = _softmax_parts(_dot_nt(q, kk))
            dp = _dot_nt(do, vv)
            ds = e * (dp - inv * jnp.sum(e * dp, axis=1, keepdims=True))
            dq = dq + _dot(ds, kk) * inv
            dk = dk + _dot_tn(ds, jnp.where(mask, q * inv, 0.0))
            dv = dv + _dot_tn(e, jnp.where(mask, do * inv, 0.0))
        dq_ref[...] = dq

        @pl.when(i == 0)
        def _():
            dk_ref[...] = dk
            dv_ref[...] = dv

        @pl.when(i > 0)
        def _():
            dk_ref[...] += dk
            dv_ref[...] += dv

    acc = pl.BlockSpec((None, T, KV_W), lambda b, p, i: (p, b, 0))
    sds = jax.ShapeDtypeStruct
    return pl.pallas_call(
        body, name="att_bwd", grid=(B, 4, nq),
        in_specs=[qspec, side(0), side(1), side(0), side(1), qspec], out_specs=(qspec, acc, acc),
        out_shape=(sds((B * T, ATT_W), F32), sds((4, B * T, KV_W), F32), sds((4, B * T, KV_W), F32)),
        compiler_params=_cp(("arbitrary",) * 3),
    )(qr, kpad, kpad, vpad, vpad, d_o)


def _shift_specs(R, T, TT, width):
    tpe = T // TT
    nb8 = R // 8
    cur = pl.BlockSpec((TT, width), lambda i: (i, 0))
    prev = pl.BlockSpec((8, width), lambda i: (jnp.maximum(i * (TT // 8) - 1, 0), 0))
    nxt = pl.BlockSpec((8, width), lambda i: (jnp.minimum((i + 1) * (TT // 8), nb8 - 1), 0))
    return tpe, cur, prev, nxt


def _neighbours(cur, prev8, next8, i, tpe, TT):
    rows = lax.broadcasted_iota(jnp.int32, (TT, 1), 0)
    first = jnp.where(i % tpe == 0, 0.0, 1.0)
    last = jnp.where(i % tpe == tpe - 1, 0.0, 1.0)
    before = jnp.where(rows == 0, prev8[7:8, :] * first, pltpu.roll(cur, 1, 0))
    after = jnp.where(rows == TT - 1, next8[0:1, :] * last, pltpu.roll(cur, TT - 1, 0))
    return before, after


def _shift_fwd_call(x, taps, T):
    R, width = x.shape
    TT = min(ROW_TILE, T)
    tpe, cur, prev, nxt = _shift_specs(R, T, TT, width)

    def body(x_ref, p_ref, n_ref, t_ref, o_ref, vh_ref):
        xc = x_ref[...]
        before, after = _neighbours(xc, p_ref[...], n_ref[...], pl.program_id(0), tpe, TT)
        out = t_ref[0:1, :] * before + t_ref[1:2, :] * xc + t_ref[2:3, :] * after
        o_ref[...] = out
        left = lax.broadcasted_iota(jnp.int32, (1, KV_W), 1) < HEAD_DIM
        for p in range(RWKV_W // KV_W):
            pair = out[:, 2 * RWKV_W + p * KV_W:2 * RWKV_W + (p + 1) * KV_W]
            vh_ref[:, 2 * p * KV_W:(2 * p + 1) * KV_W] = jnp.where(left, pair, 0.0)
            vh_ref[:, (2 * p + 1) * KV_W:(2 * p + 2) * KV_W] = jnp.where(left, pltpu.roll(pair, HEAD_DIM, 1), 0.0)

    return pl.pallas_call(
        body, name="shift_fwd", grid=(R // TT,), in_specs=[cur, prev, nxt, _full(taps.shape)],
        out_specs=(cur, pl.BlockSpec((TT, 2 * RWKV_W), lambda i: (i, 0))),
        out_shape=(jax.ShapeDtypeStruct((R, width), F32), jax.ShapeDtypeStruct((R, 2 * RWKV_W), F32)),
        compiler_params=_cp(("arbitrary",)),
    )(x, x, x, taps)


def _shift_bwd_call(x, d, taps, T):
    R, width = x.shape
    TT = min(ROW_TILE, T)
    tpe, cur, prev, nxt = _shift_specs(R, T, TT, width)

    def body(x_ref, xp_ref, xn_ref, d_ref, dp_ref, dn_ref, t_ref, dx_ref, dt_ref):
        i = pl.program_id(0)
        xc, dc = x_ref[...], d_ref[...]
        d_before, d_after = _neighbours(dc, dp_ref[...], dn_ref[...], i, tpe, TT)
        dx_ref[...] = t_ref[2:3, :] * d_before + t_ref[1:2, :] * dc + t_ref[0:1, :] * d_after
        x_before, x_after = _neighbours(xc, xp_ref[...], xn_ref[...], i, tpe, TT)
        @pl.when(i == 0)
        def _():
            dt_ref[...] = jnp.zeros_like(dt_ref)

        for j, xs in enumerate((x_before, xc, x_after)):
            dt_ref[j:j + 1, :] += jnp.sum(dc * xs, axis=0, keepdims=True)

    return pl.pallas_call(
        body, name="shift_bwd", grid=(R // TT,),
        in_specs=[cur, prev, nxt, cur, prev, nxt, _full(taps.shape)], out_specs=(cur, _full((8, width))),
        out_shape=(jax.ShapeDtypeStruct((R, width), F32), jax.ShapeDtypeStruct((8, width), F32)),
        compiler_params=_cp(("arbitrary",)),
    )(x, x, x, d, d, d, taps)


def _lora_in(wa):
    lane = lax.broadcasted_iota(jnp.int32, (1, LORA_W), 1)
    return jnp.where(lane < LORA_W // 2, jnp.tanh(wa), wa)


def _rwkv_prep_call(shifted, wup, aup, w0, a0, k_k, k_a, bd, T):
    R = shifted.shape[0]
    TT = min(ROW_TILE, T)

    def body(k_ref, wa_ref, wup_ref, aup_ref, w0_ref, a0_ref, kk_ref, ka_ref, bd_ref, w_o, kt_o, akk_o, kk_o):
        twa = _lora_in(wa_ref[...])
        pre = [_dot(twa, m_ref[z]) for m_ref in (wup_ref, aup_ref) for z in range(2)]
        outs = _rwkv_pw(k_ref[...], pre[0], pre[1], pre[2], pre[3], w0_ref[...], a0_ref[...], kk_ref[...],
                        ka_ref[...], bd_ref[...], False)
        w_o[0], w_o[1], kt_o[0], kt_o[1], akk_o[0], akk_o[1] = outs[:6]
        kk_o[...] = outs[6]

    col = lambda c, w: pl.BlockSpec((TT, w), lambda i: (i, c))
    two = pl.BlockSpec((2, TT, RWKV_W), lambda i: (0, i, 0))
    sds = jax.ShapeDtypeStruct
    return pl.pallas_call(
        body, name="rwkv_prep", grid=(R // TT,),
        in_specs=[col(1, RWKV_W), col(3 * RWKV_W // LORA_W, LORA_W), _full(wup.shape), _full(aup.shape),
                  _full((2, RWKV_W)), _full((2, RWKV_W)), _full((1, RWKV_W)), _full((1, RWKV_W)), _full((256, 256))],
        out_specs=(two, two, two, col(0, RWKV_W)),
        out_shape=(sds((2, R, RWKV_W), F32),) * 3 + (sds((R, RWKV_W), F32),),
        compiler_params=_cp(("arbitrary",)),
    )(shifted, shifted, wup, aup, w0, a0, k_k, k_a, bd)


def _rwkv_prep_bwd_call(shifted, cts, wup, aup, w0, a0, k_k, k_a, bd, T):
    R = shifted.shape[0]
    TT = min(ROW_TILE, T)

    def body(k_ref, wa_ref, dw0, dkt0, dakk0, dkk0, dr0, dv0, dw1, dkt1, dakk1, dkk1, dr1, dv1, dr2_ref, dv2_ref, dkts_ref,
             wup_ref, aup_ref, w0_ref, a0_ref, kk_ref, ka_ref, bd_ref,
             dsh_ref, gwup_ref, gaup_ref, gw0_ref, ga0_ref, gkk_ref, gka_ref):
        dw_ref, dkt_ref, dakk_ref, dkk_ref, dr_ref, dv_ref = ((dw0, dw1), (dkt0, dkt1), (dakk0, dakk1), (dkk0, dkk1),
                                                              (dr0, dr1), (dv0, dv1))
        i = pl.program_id(0)
        wa = wa_ref[...]
        twa = _lora_in(wa)
        pre = [_dot(twa, m_ref[z]) for m_ref in (wup_ref, aup_ref) for z in range(2)]
        fn = functools.partial(_rwkv_pw, bd=bd_ref[...], diff=True)
        _, vjp = jax.vjp(fn, k_ref[...], pre[0], pre[1], pre[2], pre[3], w0_ref[...], a0_ref[...], kk_ref[...],
                         ka_ref[...])
        dkts = dkts_ref[...]
        dk, dpw0, dpw1, dpa0, dpa1, gw0, ga0, gkk, gka = vjp(
            (dw_ref[0][...], dw_ref[1][...], dkt_ref[0][...] + dkts, dkt_ref[1][...] + dkts, dakk_ref[0][...],
             dakk_ref[1][...], dkk_ref[0][...] + dkk_ref[1][...]))
        dtwa = (_dot_nt(dpw0, wup_ref[0]) + _dot_nt(dpw1, wup_ref[1]) + _dot_nt(dpa0, aup_ref[0])
                + _dot_nt(dpa1, aup_ref[1]))
        lane = lax.broadcasted_iota(jnp.int32, (1, LORA_W), 1)
        dsh_ref[:, 0:RWKV_W] = dr_ref[0][...] + dr_ref[1][...] + dr2_ref[...]
        dsh_ref[:, RWKV_W:2 * RWKV_W] = dk
        dsh_ref[:, 2 * RWKV_W:3 * RWKV_W] = dv_ref[0][...] + dv_ref[1][...] + dv2_ref[...]
        dsh_ref[:, 3 * RWKV_W:] = jnp.where(lane < LORA_W // 2, dtwa * (1.0 - twa * twa), dtwa)
        acc = ((gwup_ref.at[0], _dot_tn(twa, dpw0)), (gwup_ref.at[1], _dot_tn(twa, dpw1)),
               (gaup_ref.at[0], _dot_tn(twa, dpa0)), (gaup_ref.at[1], _dot_tn(twa, dpa1)),
               (gw0_ref, gw0), (ga0_ref, ga0), (gkk_ref, gkk), (gka_ref, gka))

        @pl.when(i == 0)
        def _():
            for ref, val in acc:
                ref[...] = val

        @pl.when(i > 0)
        def _():
            for ref, val in acc:
                ref[...] += val

    col = lambda c, w: pl.BlockSpec((TT, w), lambda i: (i, c))
    one = col(0, RWKV_W)
    sds = jax.ShapeDtypeStruct
    return pl.pallas_call(
        body, name="rwkv_prep_bwd", grid=(R // TT,),
        in_specs=[col(1, RWKV_W), col(3 * RWKV_W // LORA_W, LORA_W)] + [one] * 15 + [
                  _full(wup.shape), _full(aup.shape), _full((2, RWKV_W)), _full((2, RWKV_W)), _full((1, RWKV_W)),
                  _full((1, RWKV_W)), _full((256, 256))],
        out_specs=(pl.BlockSpec((TT, SHIFT_W), lambda i: (i, 0)), _full(wup.shape), _full(aup.shape),
                   _full((2, RWKV_W)), _full((2, RWKV_W)), _full((1, RWKV_W)), _full((1, RWKV_W))),
        out_shape=(sds((R, SHIFT_W), F32), sds(wup.shape, F32), sds(aup.shape, F32), sds((2, RWKV_W), F32),
                   sds((2, RWKV_W), F32), sds((1, RWKV_W), F32), sds((1, RWKV_W), F32)),
        compiler_params=_cp(("arbitrary",)),
    )(shifted, shifted, *cts, wup, aup, w0, a0, k_k, k_a, bd)


def _col_lhs(row, eye_b):
    return eye_b * row.astype(MXU_DTYPE)


def _colsum(x):
    return jnp.sum(x, axis=0, keepdims=True)


def _stacked_segsum(tiles, bd):
    res = _seg_dot(jnp.concatenate(tiles, axis=0), bd)
    return [res[j * HEAD_DIM:(j + 1) * HEAD_DIM] for j in range(len(tiles))]


def _scan_specs(B, T, C, nC):
    def blk(z, col, rev):
        idx = (lambda g: (z, 0, nC - 1 - g, col)) if rev else (lambda g: (z, 0, g, col))
        return pl.BlockSpec((None, B, C, RWKV_W), idx)

    def blk3(col, rev):
        idx = (lambda g: (0, nC - 1 - g, col)) if rev else (lambda g: (0, g, col))
        return pl.BlockSpec((B, C, RWKV_W), idx)

    return blk, blk3


def _scan_fwd_call(w, kt, akk, kk, shifted, eye_b, eye_f, bd, B, T):
    C = min(SCAN_CHUNK, T)
    nC = T // C
    blk, blk3 = _scan_specs(B, T, C, nC)

    def body(w0, kt0, akk0, kk0, v0, r0, w1, kt1, akk1, kk1, v1, r1, eb_ref, ef_ref, bd_ref, y0, y1, st, S):
        @pl.when(pl.program_id(0) == 0)
        def _():
            S[...] = jnp.zeros_like(S)

        st[0] = S[...].astype(MXU_DTYPE)
        dirs = ((w0, kt0, akk0, kk0, v0, r0, y0), (w1, kt1, akk1, kk1, v1, r1, y1))

        def step(s, carry):
            for z in range(2):
                row = s if z == 0 else C - 1 - s
                prev = jnp.maximum(s - 1, 0) if z == 0 else jnp.minimum(C - s, C - 1)
                wr, ktr, akkr, kkr, vr, rr, yr = dirs[z]
                tiles = []
                for b in range(B):
                    Sb = st[s, z * B + b]
                    tiles += [Sb * kkr[b, pl.ds(row, 1), :].astype(MXU_DTYPE),
                              _col_lhs(vr[b, pl.ds(row, 1), :], eb_ref[...]),
                              Sb * rr[b, pl.ds(prev, 1), :].astype(MXU_DTYPE)]
                res = _stacked_segsum(tiles, bd_ref[...])
                for b in range(B):
                    c = z * B + b
                    sab, vb, yb = res[3 * b:3 * b + 3]
                    ld = lambda ref: ref[b, pl.ds(row, 1), :]
                    Sn = S[c] * ld(wr) - sab * ld(akkr) + vb * ld(ktr)
                    S[c] = Sn
                    st[s + 1, c] = Sn.astype(MXU_DTYPE)
                    yr[b, pl.ds(prev, 1), :] = _colsum(ef_ref[...] * yb)
            return carry

        lax.fori_loop(0, C, step, 0, unroll=SCAN_UNROLL)
        for z in range(2):
            last = C - 1 if z == 0 else 0
            rr, yr = dirs[z][5], dirs[z][6]
            res = _stacked_segsum([st[C, z * B + b] * rr[b, last:last + 1, :].astype(MXU_DTYPE) for b in range(B)],
                                  bd_ref[...])
            for b in range(B):
                yr[b, last:last + 1, :] = _colsum(ef_ref[...] * res[b])

    ins, specs = [], []
    for z, rev in ((0, False), (1, True)):
        ins += [w, kt, akk, kk, shifted, shifted]
        specs += [blk(z, 0, rev), blk(z, 0, rev), blk(z, 0, rev), blk3(0, rev), blk3(2, rev), blk3(0, rev)]
    sds = jax.ShapeDtypeStruct
    return pl.pallas_call(
        body, name="scan_fwd", grid=(nC,),
        in_specs=specs + [_full((HEAD_DIM, RWKV_W)), _full((HEAD_DIM, RWKV_W)), _full((256, 256))],
        out_specs=(blk3(0, False), blk3(0, True),
                   pl.BlockSpec((None, C + 1, 2 * B, HEAD_DIM, RWKV_W), lambda g: (g, 0, 0, 0, 0))),
        out_shape=(sds((B, T, RWKV_W), F32), sds((B, T, RWKV_W), F32),
                   sds((nC, C + 1, 2 * B, HEAD_DIM, RWKV_W), MXU_DTYPE)),
        scratch_shapes=[pltpu.VMEM((2 * B, HEAD_DIM, RWKV_W), F32)],
        compiler_params=_cp(("arbitrary",)),
    )(*ins, eye_b, eye_f, bd)


def _scan_bwd_call(w, kt, akk, kk, shifted, v_heads, dys, st, eye_b, eye_f, bd, B, T):
    C = min(SCAN_CHUNK, T)
    nC = T // C
    blk, blk3 = _scan_specs(B, T, C, nC)
    nin = 7

    def body(*refs):
        d0, d1 = refs[:nin], refs[nin:2 * nin]
        st_ref, eb_ref, ef_ref, sel_ref, hm_ref, bd_ref = refs[2 * nin:2 * nin + 6]
        o0, o1 = refs[2 * nin + 6:2 * nin + 12], refs[2 * nin + 12:2 * nin + 18]
        COL, DYC, G = refs[2 * nin + 18:]

        @pl.when(pl.program_id(0) == 0)
        def _():
            G[...] = jnp.zeros_like(G)

        dirs = (d0 + (o0,), d1 + (o1,))

        def column_operands(s, z):
            row = s if z == 0 else C - 1 - s
            _, _, _, kkr, _, _, dyr, _ = dirs[z]
            tiles = []
            for b in range(B):
                tiles += [st_ref[s, z * B + b] * kkr[b, pl.ds(row, 1), :].astype(MXU_DTYPE),
                          _col_lhs(dyr[b, pl.ds(row, 1), :], eb_ref[...])]
            return tiles

        def keep_columns(res, z):
            for b in range(B):
                for k in range(2):
                    COL[k, z * B + b] = res[2 * b + k].astype(MXU_DTYPE)
                DYC[z * B + b] = res[2 * b + 1]

        for z in range(2):
            keep_columns(_stacked_segsum(column_operands(C - 1, z), bd_ref[...]), z)

        def bwd(it, carry):
            s = C - 1 - it
            for z in range(2):
                row = s if z == 0 else C - 1 - s
                wr, ktr, akkr, kkr, vr, rr, dyr, (dw_o, dkt_o, dakk_o, dkk_o, dr_o, dv_o) = dirs[z]
                tiles, Gcs = [], []
                for b in range(B):
                    c = z * B + b
                    Gc = G[c] + DYC[c] * rr[b, pl.ds(row, 1), :]
                    Gb = Gc.astype(MXU_DTYPE)
                    Gcs.append((Gc, Gb))
                    tiles += [Gb * akkr[b, pl.ds(row, 1), :].astype(MXU_DTYPE),
                              Gb * ktr[b, pl.ds(row, 1), :].astype(MXU_DTYPE)]
                res = _stacked_segsum(tiles + column_operands(jnp.maximum(s - 1, 0), z), bd_ref[...])
                for b in range(B):
                    c = z * B + b
                    Gc, Gb = Gcs[b]
                    gab, dvb = res[2 * b], res[2 * b + 1]
                    ld = lambda ref: ref[b, pl.ds(row, 1), :]
                    G[c] = Gc * ld(wr) - gab * ld(kkr)
                    Sb = st_ref[s, c]
                    prods = jnp.concatenate([Gb, st_ref[s + 1, c] * COL[1, c], Gb * Sb, Gb * COL[0, c],
                                             gab.astype(MXU_DTYPE) * Sb], axis=0)
                    v_rows = jnp.concatenate([vr[b, pl.ds(row, 1)][0], jnp.zeros((8, 3 * HEAD_DIM), F32)], axis=1)
                    lhs = jnp.concatenate([sel_ref[...], v_rows], axis=0).astype(MXU_DTYPE)
                    sums = jnp.dot(lhs, prods, preferred_element_type=F32)
                    for k, (ref, sign) in enumerate(((dr_o, 1.0), (dw_o, 1.0), (dakk_o, -1.0), (dkk_o, -1.0))):
                        ref[b, pl.ds(row, 1), :] = sign * sums[k:k + 1, :]
                    dkt_o[b, pl.ds(row, 1), :] = _colsum(sums[8:16] * hm_ref[...])
                    dv_o[b, pl.ds(row, 1), :] = _colsum(ef_ref[...] * dvb)
                keep_columns(res[2 * B:], z)
            return carry

        lax.fori_loop(0, C, bwd, 0, unroll=SCAN_UNROLL)

    ins, specs = [], []
    for z, rev in ((0, True), (1, False)):
        heads = pl.BlockSpec((B, C) + v_heads.shape[2:], (lambda g: (0, nC - 1 - g, 0, 0)) if rev else (lambda g: (0, g, 0, 0)))
        ins += [w, kt, akk, kk, v_heads, shifted, dys]
        specs += [blk(z, 0, rev), blk(z, 0, rev), blk(z, 0, rev), blk3(0, rev), heads, blk3(0, rev), blk3(0, rev)]
    sel = (jnp.arange(8)[:, None] + 1 == (jnp.arange(5 * HEAD_DIM) // HEAD_DIM)[None, :]).astype(F32)
    head_rows = (jnp.arange(RWKV_W // HEAD_DIM)[:, None] == (jnp.arange(RWKV_W) // HEAD_DIM)[None, :]).astype(F32)
    ins += [st, eye_b, eye_f, sel, head_rows, bd]
    specs += [pl.BlockSpec((None, C + 1, 2 * B, HEAD_DIM, RWKV_W), lambda g: (nC - 1 - g, 0, 0, 0, 0)),
              _full((HEAD_DIM, RWKV_W)), _full((HEAD_DIM, RWKV_W)), _full(sel.shape), _full(head_rows.shape),
              _full((256, 256))]
    sds = jax.ShapeDtypeStruct
    out_specs = tuple(blk3(0, True) for _ in range(6)) + tuple(blk3(0, False) for _ in range(6))
    res = pl.pallas_call(
        body, name="scan_bwd", grid=(nC,), in_specs=specs, out_specs=out_specs,
        out_shape=tuple(sds((B, T, RWKV_W), F32) for _ in range(12)),
        scratch_shapes=[pltpu.VMEM((2, 2 * B, HEAD_DIM, RWKV_W), MXU_DTYPE), pltpu.VMEM((2 * B, HEAD_DIM, RWKV_W), F32),
                        pltpu.VMEM((2 * B, HEAD_DIM, RWKV_W), F32)],
        compiler_params=_cp(("arbitrary",)),
    )(*ins)
    return list(res)


def _out_head_call(x2, tgt2, gate, y_att, g_att, y0, y1, shifted, kt, g_rw, w_out, g_post, gn_w, gn_b, r_k, bd, T):
    R = x2.shape[0]
    TT = min(ROW_TILE, T)
    tpe = T // TT

    def body(x_ref, t_ref, gate_ref, ya_ref, ga_ref, y0_ref, y1_ref, r_ref, v_ref, kt_ref, grw_ref, w_ref, gp_ref,
             gnw_ref, gnb_ref, rk_ref, bd_ref,
             loss_o, dy_o, dya_o, dga_o, dys_o, dr_o, dv_o, dkts_o, dgrw_o, dgate_o, gw_o, ggp_o, ggnw_o, ggnb_o, grk_o):
        i = pl.program_id(0)
        bd = bd_ref[...]
        mix = functools.partial(_mix_fn, bd=bd, diff=True)
        (ma, mr), mix_vjp = jax.vjp(mix, ya_ref[...], ga_ref[...], y0_ref[...] + y1_ref[...], r_ref[...], v_ref[...],
                                    kt_ref[0] + kt_ref[1], grw_ref[...], gnw_ref[...], gnb_ref[...], rk_ref[...])
        out = _dot(ma, w_ref[0:ATT_W, :]) + _dot(mr, w_ref[ATT_W:, :])
        loss, loss_vjp = jax.vjp(_loss_fn, out, x_ref[...], t_ref[...], gate_ref[0], gp_ref[...])
        d_out, dy, _, dgate, dgp = loss_vjp(jnp.ones((1, 1), F32))
        dy_o[...] = dy
        dma = _dot_nt(d_out, w_ref[0:ATT_W, :])
        dmr = _dot_nt(d_out, w_ref[ATT_W:, :])
        dya_o[...], dga_o[...], dys_o[...], dr_o[...], dv_o[...], dkts_o[...], dgrw_o[...], dgnw, dgnb, drk = \
            mix_vjp((dma, dmr))
        gw = jnp.concatenate([_dot_tn(ma, d_out), _dot_tn(mr, d_out)], axis=0)
        acc = ((loss_o, jnp.broadcast_to(loss, (8, 128))), (gw_o, gw), (ggp_o, dgp), (ggnw_o, dgnw), (ggnb_o, dgnb),
               (grk_o, drk))

        @pl.when(i == 0)
        def _():
            for ref, val in acc:
                ref[...] = val

        @pl.when(i > 0)
        def _():
            for ref, val in acc:
                ref[...] += val

        @pl.when(i % tpe == 0)
        def _():
            dgate_o[0] = dgate

        @pl.when(i % tpe > 0)
        def _():
            dgate_o[0] += dgate

    row = lambda w, c=0: pl.BlockSpec((TT, w), lambda i: (i, c))
    two = pl.BlockSpec((2, TT, RWKV_W), lambda i: (0, i, 0))
    per_ex = pl.BlockSpec((1, 1, D_MODEL), lambda i: (i // tpe, 0, 0))
    sds = jax.ShapeDtypeStruct
    r512 = sds((R, RWKV_W), F32)
    return pl.pallas_call(
        body, name="out_head", grid=(R // TT,),
        in_specs=[row(D_MODEL), row(D_MODEL), per_ex, row(ATT_W), row(ATT_W), row(RWKV_W), row(RWKV_W), row(RWKV_W, 0),
                  row(RWKV_W, 2), two,
                  row(RWKV_W), _full(w_out.shape), _full((1, D_MODEL)), _full((1, RWKV_W)), _full((1, RWKV_W)),
                  _full((1, RWKV_W)), _full((256, 256))],
        out_specs=(_full((8, 128)), row(D_MODEL), row(ATT_W), row(ATT_W), row(RWKV_W), row(RWKV_W), row(RWKV_W),
                   row(RWKV_W), row(RWKV_W), per_ex, _full((D_MODEL, D_MODEL)), _full((1, D_MODEL)), _full((1, RWKV_W)),
                   _full((1, RWKV_W)), _full((1, RWKV_W))),
        out_shape=(sds((8, 128), F32), sds((R, D_MODEL), F32), r512, r512, r512, r512, r512, r512, r512,
                   sds((R // T, 1, D_MODEL), F32), sds((D_MODEL, D_MODEL), F32), sds((1, D_MODEL), F32),
                   sds((1, RWKV_W), F32), sds((1, RWKV_W), F32), sds((1, RWKV_W), F32)),
        compiler_params=_cp(("arbitrary",)),
    )(x2, tgt2, gate, y_att, g_att, y0, y1, shifted, shifted, kt, g_rw, w_out, g_post, gn_w, gn_b, r_k, bd)


def _in_proj_bwd_call(x2, dy, shift, scale, g_pre, w_in, qg, kg, cos, sin, bd, q_raw, k_raw, dqr, dkp, dvp,
                      d_gatt, d_rin, d_grw, T):
    R = x2.shape[0]
    TT = min(ROW_TILE, T)
    tpe = T // TT

    def body(x_ref, dy_ref, sh_ref, sc_ref, gp_ref, w_ref, qg_ref, kg_ref, cos_ref, sin_ref, bd_ref, q_ref, k_ref,
             dqr_ref, dkp_ref, dvp_ref, dga_ref, drin_ref, dgrw_ref,
             dx_o, dproj_o, dsh_o, dsc_o, ggp_o, gqg_o, gkg_o):
        i = pl.program_id(0)
        cos, sin, bd = cos_ref[...], sin_ref[...], bd_ref[...]
        left = lax.broadcasted_iota(jnp.int32, (1, KV_W), 1) < HEAD_DIM

        def kv_grad(ref):
            a = ref[0] + ref[1]
            b = ref[2] + ref[3]
            return jnp.where(left, a + pltpu.roll(a, HEAD_DIM, 1), b + pltpu.roll(b, HEAD_DIM, 1))

        qfn = functools.partial(_qk_fn, cos=jnp.tile(cos, (1, 4)), sin=jnp.tile(sin, (1, 4)), bd=bd, scale=ATT_SCALE,
                                diff=True)
        _, q_vjp = jax.vjp(qfn, q_ref[...], qg_ref[...])
        dq, gqg = q_vjp(dqr_ref[...])
        kfn = functools.partial(_qk_fn, cos=cos, sin=sin, bd=bd, scale=1.0, diff=True)
        _, k_vjp = jax.vjp(kfn, k_ref[...], kg_ref[...])
        dk, gkg = k_vjp(kv_grad(dkp_ref))
        pieces = ((C_Q, C_K, dq), (C_K, C_V, dk), (C_V, C_GA, kv_grad(dvp_ref)), (C_GA, C_RIN, dga_ref[...]),
                  (C_RIN, C_GRW, drin_ref[...]), (C_GRW, C_END, dgrw_ref[...]))
        dh = jnp.zeros((TT, D_MODEL), F32)
        for c0, c1, val in pieces:
            vb = val.astype(MXU_DTYPE)
            dproj_o[:, c0:c1] = vb
            dh = dh + _dot(vb, w_ref[c0:c1, :])
        _, pre_vjp = jax.vjp(_pre_fn, x_ref[...], sh_ref[0], sc_ref[0], gp_ref[...])
        dx, dsh, dsc, ggp = pre_vjp(dh)
        dx_o[...] = dx + dy_ref[...]
        acc = ((ggp_o, ggp), (gqg_o, gqg), (gkg_o, gkg))

        @pl.when(i == 0)
        def _():
            for ref, val in acc:
                ref[...] = val

        @pl.when(i > 0)
        def _():
            for ref, val in acc:
                ref[...] += val

        @pl.when(i % tpe == 0)
        def _():
            dsh_o[0] = dsh
            dsc_o[0] = dsc

        @pl.when(i % tpe > 0)
        def _():
            dsh_o[0] += dsh
            dsc_o[0] += dsc

    row = lambda w: pl.BlockSpec((TT, w), lambda i: (i, 0))
    per_ex = pl.BlockSpec((1, 1, D_MODEL), lambda i: (i // tpe, 0, 0))
    tab = pl.BlockSpec((TT, KV_W), lambda i: (i % tpe, 0))
    pad = pl.BlockSpec((4, TT, KV_W), lambda i: (0, i, 0))
    sds = jax.ShapeDtypeStruct
    nb = R // T
    return pl.pallas_call(
        body, name="in_proj_bwd", grid=(R // TT,),
        in_specs=[row(D_MODEL), row(D_MODEL), per_ex, per_ex, _full((1, D_MODEL)), _full(w_in.shape), _full((1, ATT_W)),
                  _full((1, KV_W)), tab, tab, _full((256, 256)), row(ATT_W), row(KV_W), row(ATT_W), pad, pad,
                  row(ATT_W), row(SHIFT_W), row(RWKV_W)],
        out_specs=(row(D_MODEL), row(C_END), per_ex, per_ex, _full((1, D_MODEL)), _full((1, ATT_W)), _full((1, KV_W))),
        out_shape=(sds((R, D_MODEL), F32), sds((R, C_END), MXU_DTYPE), sds((nb, 1, D_MODEL), F32),
                   sds((nb, 1, D_MODEL), F32), sds((1, D_MODEL), F32), sds((1, ATT_W), F32), sds((1, KV_W), F32)),
        compiler_params=_cp(("arbitrary",)),
    )(x2, dy, shift, scale, g_pre, w_in, qg, kg, cos, sin, bd, q_raw, k_raw, dqr, dkp, dvp, d_gatt, d_rin, d_grw)


def _w_in_grad_call(hb, dproj):
    R = hb.shape[0]
    TT = min(W_GRAD_ROWS, R)
    CB = 1152

    def body(h_ref, d_ref, o_ref):
        g = _dot_tn(h_ref[...], d_ref[...])

        @pl.when(pl.program_id(1) == 0)
        def _():
            o_ref[...] = g

        @pl.when(pl.program_id(1) > 0)
        def _():
            o_ref[...] += g

    return pl.pallas_call(
        body, name="w_in_grad", grid=(C_END // CB, R // TT),
        in_specs=[pl.BlockSpec((TT, D_MODEL), lambda j, i: (i, 0)), pl.BlockSpec((TT, CB), lambda j, i: (i, j))],
        out_specs=pl.BlockSpec((D_MODEL, CB), lambda j, i: (0, j)),
        out_shape=jax.ShapeDtypeStruct((D_MODEL, C_END), F32), compiler_params=_cp(("arbitrary", "arbitrary")),
    )(hb, dproj)


def _adam_refs(p_ref, w_ref, m_ref, v_ref, g_o, d_o, m_o, v_o):
    g = p_ref[0].astype(F32)
    for j in range(1, p_ref.shape[0]):
        g = g + p_ref[j].astype(F32)
    m2 = ADAM_B1 * m_ref[...] + (1.0 - ADAM_B1) * g
    v2 = ADAM_B2 * v_ref[...] + (1.0 - ADAM_B2) * jnp.square(g)
    m_hat = m2 / (1.0 - ADAM_B1 ** ADAM_STEP)
    v_hat = v2 / (1.0 - ADAM_B2 ** ADAM_STEP)
    g_o[...] = g
    d_o[...] = -ADAM_LR * (m_hat / (jnp.sqrt(v_hat) + ADAM_EPS) + ADAM_WD * w_ref[...])
    m_o[...] = m2
    v_o[...] = v2


def _adam_small_call(items, name):
    n = len(items)

    def body(*refs):
        for k in range(n):
            _adam_refs(*refs[4 * k:4 * k + 4], *refs[4 * n + 4 * k:4 * n + 4 * k + 4])

    out_shape = tuple(jax.ShapeDtypeStruct(w.shape, F32) for _, w, _, _ in items for _ in range(4))
    out = pl.pallas_call(body, name=name, out_shape=out_shape)(*[a for item in items for a in item])
    return [out[4 * k:4 * k + 4] for k in range(n)]


def _adam_call(parts, w, m, v, name, row_tile=None):
    P, M, N = parts.shape
    TM = M if row_tile is None else row_tile

    def body(*refs):
        _adam_refs(*refs)

    blk = pl.BlockSpec((TM, N), lambda i: (i, 0))
    return pl.pallas_call(
        body, name=name, grid=(M // TM,),
        in_specs=[pl.BlockSpec((P, TM, N), lambda i: (0, i, 0)), blk, blk, blk], out_specs=(blk,) * 4,
        out_shape=(jax.ShapeDtypeStruct((M, N), F32),) * 4, compiler_params=_cp(("arbitrary",)),
    )(parts, w, m, v)


_SMALL_ROWS = 136


def _pack_small(taps, w_up, w0, a_up, a0):
    flat = jnp.concatenate([taps.reshape(-1), w_up.reshape(-1), w0.reshape(-1), a_up.reshape(-1), a0.reshape(-1)])
    return jnp.pad(flat, (0, _SMALL_ROWS * 128 - flat.shape[0])).reshape(_SMALL_ROWS, 128)


def _unpack_small(packed):
    n = packed.shape[0]
    flat = packed.reshape(n, -1)
    out, o = [], 0
    for shape in ((3, 208), (2, 64, 64), (2, 64), (2, 64, 64), (2, 64)):
        size = 1
        for s in shape:
            size *= s
        out.append(flat[:, o:o + size].reshape((n,) + shape))
        o += size
    return out


def _cols_to_full(blocks):
    nd = blocks.ndim
    moved = jnp.moveaxis(blocks, 0, nd - 2)
    return moved.reshape(moved.shape[:-2] + (moved.shape[-2] * moved.shape[-1],))


def _full_to_cols(full):
    k = full.shape[-1] // NDEV
    return jnp.moveaxis(full.reshape(full.shape[:-1] + (NDEV, k)), -2, 0)


_REP_SIZES = (("g_pre", 1024), ("q_norm_g", 64), ("k_norm_g", 64), ("k_k", 512), ("k_a", 512), ("r_k", 512),
              ("gn_w", 512), ("gn_b", 512), ("g_post", 1024))
_REP_ROWS = 40


def kernel(x, c, w_ada, b_ada, g_pre, w_in, q_norm_g, k_norm_g, shift_taps, w_up, w0, a_up, a0, k_k, k_a, r_k, gn_w, gn_b, w_out, g_post, loss_target, m_w_ada, m_b_ada, m_g_pre, m_w_in, m_q_norm_g, m_k_norm_g, m_shift_taps, m_w_up, m_w0, m_a_up, m_a0, m_k_k, m_k_a, m_r_k, m_gn_w, m_gn_b, m_w_out, m_g_post, v_w_ada, v_b_ada, v_g_pre, v_w_in, v_q_norm_g, v_k_norm_g, v_shift_taps, v_w_up, v_w0, v_a_up, v_a0, v_k_k, v_k_a, v_r_k, v_gn_w, v_gn_b, v_w_out, v_g_post):
    B, T, _ = x.shape
    R = B * T
    me = 4 * lax.axis_index("x") + 2 * lax.axis_index("y") + lax.axis_index("c")
    x2 = x.reshape(R, D_MODEL)
    tgt2 = loss_target.reshape(R, D_MODEL)

    seg = jnp.arange(256) // HEAD_DIM
    bd = (seg[:, None] == seg[None, :]).astype(MXU_DTYPE)
    eye = (jnp.arange(HEAD_DIM)[:, None] == (jnp.arange(RWKV_W) % HEAD_DIM)[None, :])
    eye_b, eye_f = eye.astype(MXU_DTYPE), eye.astype(F32)
    cos, sin = _rope_tables(T)

    c_g, w_in_g, w_out_g, small_g = _exchange(
        [c, w_in[0].T.astype(MXU_DTYPE), w_out[0].astype(MXU_DTYPE),
         _pack_small(shift_taps[0], w_up[0], w0[0], a_up[0], a0[0])], ["all"] * 4, "gather_params")
    c_all = c_g.reshape(NDEV * B, D_MODEL)
    w_in_f = w_in_g.reshape(C_END, D_MODEL)
    w_out_f = w_out_g.reshape(D_MODEL, D_MODEL)
    taps_b, w_up_b, w0_b, a_up_b, a0_b = _unpack_small(small_g)
    taps_f = jnp.pad(_cols_to_full(taps_b), ((0, 5), (0, 0)))
    w_up_f, a_up_f = _cols_to_full(w_up_b), _cols_to_full(a_up_b)
    w0_f, a0_f = _cols_to_full(w0_b), _cols_to_full(a0_b)
    wup_pad = jnp.pad(w_up_f, ((0, 0), (0, 64), (0, 0))).astype(MXU_DTYPE)
    aup_pad = jnp.pad(a_up_f, ((0, 0), (64, 0), (0, 0))).astype(MXU_DTYPE)

    ncol = w_ada.shape[2]
    b_cols = lax.dynamic_slice(b_ada, (0, me * ncol), (1, ncol))
    mod_cols = _mod_call(c_all, w_ada[0].astype(MXU_DTYPE), b_cols)
    (mod_g,) = _exchange([mod_cols], ["all"], "gather_mod")
    mod = lax.dynamic_slice(_cols_to_full(mod_g), (me * B, 0), (B, 3 * D_MODEL))
    shift, scale, gate = [mod[:, j * D_MODEL:(j + 1) * D_MODEL].reshape(B, 1, D_MODEL) for j in range(3)]

    qg = jnp.tile(q_norm_g, (1, ATT_W // HEAD_DIM))
    kg = jnp.tile(k_norm_g, (1, KV_W // HEAD_DIM))
    rk_row = r_k.reshape(1, RWKV_W)

    hb, qr, kpad, vpad, q_raw, k_raw, g_att, rin, g_rw = _in_proj_call(
        x2, shift, scale, g_pre, w_in_f, qg, kg, cos, sin, bd, T)
    y_att = _att_fwd_call(qr, kpad, vpad, B, T)
    shifted, v_rows = _shift_fwd_call(rin, taps_f, T)
    w_s, kt_s, akk_s, kk_s = _rwkv_prep_call(shifted, wup_pad, aup_pad, w0_f, a0_f, k_k, k_a, bd, T)
    sh3 = shifted.reshape(B, T, SHIFT_W)
    r4 = lambda a: a.reshape(2, B, T, RWKV_W)
    y0, y1, st = _scan_fwd_call(r4(w_s), r4(kt_s), r4(akk_s), kk_s.reshape(B, T, RWKV_W), sh3, eye_b, eye_f, bd, B, T)

    (loss_blk, dy, d_yatt, d_gatt, d_ys, d_r2, d_v2, d_kts, d_grw, d_gate, g_wout, g_gpost, g_gnw, g_gnb,
     g_rk) = _out_head_call(x2, tgt2, gate, y_att, g_att, y0.reshape(R, RWKV_W), y1.reshape(R, RWKV_W), shifted, kt_s,
                            g_rw, w_out_f, g_post, gn_w, gn_b, rk_row, bd, T)
    v_heads = v_rows.reshape(B, T, RWKV_W // HEAD_DIM, 2 * HEAD_DIM)
    scan_cts = _scan_bwd_call(r4(w_s), r4(kt_s), r4(akk_s), kk_s.reshape(B, T, RWKV_W), sh3, v_heads,
                              d_ys.reshape(B, T, RWKV_W), st, eye_b, eye_f, bd, B, T)
    scan_cts = [a.reshape(R, RWKV_W) for a in scan_cts]
    d_shifted, g_wup, g_aup, g_w0, g_a0, g_kk, g_ka = _rwkv_prep_bwd_call(
        shifted, scan_cts + [d_r2, d_v2, d_kts], wup_pad, aup_pad, w0_f, a0_f, k_k, k_a, bd, T)
    d_rin, g_taps = _shift_bwd_call(rin, d_shifted, taps_f, T)
    dqr, dkp, dvp = _att_bwd_call(qr, kpad, vpad, d_yatt, B, T)
    grad_x, dproj, d_shift, d_scale, g_gpre, g_qg, g_kg = _in_proj_bwd_call(
        x2, dy, shift, scale, g_pre, w_in_f, qg, kg, cos, sin, bd, q_raw, k_raw, dqr, dkp, dvp, d_gatt, d_rin, d_grw, T)
    g_win = _w_in_grad_call(hb, dproj)

    rep = jnp.concatenate([g_gpre.reshape(-1), g_qg.reshape(-1, HEAD_DIM).sum(0), g_kg.reshape(-1, HEAD_DIM).sum(0),
                           g_kk.reshape(-1), g_ka.reshape(-1), g_rk.reshape(-1), g_gnw.reshape(-1), g_gnb.reshape(-1),
                           g_gpost.reshape(-1), loss_blk[0, :1]])
    rep = jnp.pad(rep, (0, _REP_ROWS * 128 - rep.shape[0])).reshape(_REP_ROWS, 128)
    dmod = jnp.concatenate([d_shift, d_scale, d_gate], axis=2).reshape(B, 3 * D_MODEL)
    small_parts = jax.vmap(_pack_small)(_full_to_cols(g_taps[:3]), _full_to_cols(g_wup[:, :64, :]), _full_to_cols(g_w0),
                                        _full_to_cols(g_aup[:, 64:, :]), _full_to_cols(g_a0))
    core = lax.axis_index("c")
    halves = [a.reshape((NDEV // 2, 2) + a.shape[1:]).astype(MXU_DTYPE)
              for a in (_full_to_cols(g_win), g_wout.reshape(NDEV, D_MODEL // NDEV, D_MODEL))]
    pick = lambda a, j: lax.dynamic_index_in_dim(a, j, axis=1, keepdims=False)
    s_win, s_wout = _pair_sum_call([pick(a, core) for a in halves], [pick(a, 1 - core) for a in halves], "reduce_pair")
    p_win, p_wout, p_small, dmod_g, rep_g = _exchange(
        [s_win, s_wout, small_parts, dmod, rep], ["chips", "chips", "scatter", "all", "all"], "reduce_grads")
    dmod_all = dmod_g.reshape(NDEV * B, 3 * D_MODEL)
    g_wada = _wada_grad_call(c_all, lax.dynamic_slice(dmod_all, (0, me * ncol), (NDEV * B, ncol)))

    res, small = {}, []

    def adam(name, parts, w, m, v, row_tile=None, alone=False):
        two_d = (-1, w.shape[-1])
        item = (parts.reshape((parts.shape[0],) + w.reshape(two_d).shape), w.reshape(two_d), m.reshape(two_d),
                v.reshape(two_d))
        if alone:
            res[name] = [o.reshape(w.shape) for o in _adam_call(*item, "adam_" + name, row_tile)]
        else:
            small.append((name, w.shape, item))

    adam("w_ada", g_wada[None], w_ada, m_w_ada, v_w_ada, alone=True)
    adam("b_ada", dmod_all.reshape(NDEV * B, 1, 3 * D_MODEL), b_ada, m_b_ada, v_b_ada)
    adam("w_in", p_win, w_in, m_w_in, v_w_in, 128, alone=True)
    adam("w_out", p_wout, w_out, m_w_out, v_w_out, alone=True)
    taps_p, wup_p, w0_p, aup_p, a0_p = _unpack_small(p_small)
    adam("shift_taps", taps_p, shift_taps, m_shift_taps, v_shift_taps)
    adam("w_up", wup_p, w_up, m_w_up, v_w_up)
    adam("w0", w0_p, w0, m_w0, v_w0)
    adam("a_up", aup_p, a_up, m_a_up, v_a_up)
    adam("a0", a0_p, a0, m_a0, v_a0)
    rep_flat = rep_g.reshape(NDEV, -1)
    off = 0
    given = dict(g_pre=(g_pre, m_g_pre, v_g_pre), q_norm_g=(q_norm_g, m_q_norm_g, v_q_norm_g),
                 k_norm_g=(k_norm_g, m_k_norm_g, v_k_norm_g), k_k=(k_k, m_k_k, v_k_k), k_a=(k_a, m_k_a, v_k_a),
                 r_k=(r_k, m_r_k, v_r_k), gn_w=(gn_w, m_gn_w, v_gn_w), gn_b=(gn_b, m_gn_b, v_gn_b),
                 g_post=(g_post, m_g_post, v_g_post))
    for name, size in _REP_SIZES:
        adam(name, rep_flat[:, off:off + size], *given[name])
        off += size
    for (name, shape, _), out in zip(small, _adam_small_call([item for _, _, item in small], "adam_small")):
        res[name] = [o.reshape(shape) for o in out]

    loss = jnp.sum(rep_flat[:, off])
    order = ["w_ada", "b_ada", "g_pre", "w_in", "q_norm_g", "k_norm_g", "shift_taps", "w_up", "w0", "a_up", "a0", "k_k",
             "k_a", "r_k", "gn_w", "gn_b", "w_out", "g_post"]
    return (loss, grad_x.reshape(B, T, D_MODEL), *[res[n][0] for n in order], *[res[n][1] for n in order],
            *[res[n][2] for n in order], *[res[n][3] for n in order])
```

```python
import functools

import jax
import jax.numpy as jnp
from jax import lax
from jax.experimental import pallas as pl
from jax.experimental.pallas import tpu as pltpu

F32 = jnp.float32
MXU_DTYPE = jnp.bfloat16
MESH = pl.DeviceIdType.MESH
NDEV = 8

D_MODEL = 1024
HEAD_DIM = 64
ATT_W = 512
KV_W = 128
RWKV_W = 512
LORA_W = 128
SHIFT_W = 3 * RWKV_W + LORA_W
GRID_W = 64
ROPE_THETA = 10000.0
DECAY_SCALE = 0.6065306597126334
NORM_EPS = 1e-6
GN_EPS = 64e-5
L2_EPS = 1e-12
ATT_SCALE = HEAD_DIM ** -0.5
C_Q, C_K, C_V, C_GA, C_RIN, C_GRW, C_END = 0, 512, 640, 768, 1280, 2944, 3456

ADAM_LR, ADAM_B1, ADAM_B2, ADAM_EPS, ADAM_WD, ADAM_STEP = 0.001, 0.9, 0.999, 1e-08, 0.01, 10

ROW_TILE = 256
W_GRAD_ROWS = 2048
ATT_TILE_FWD = 256
ATT_TILE_BWD = 512
SCAN_CHUNK = 64
SCAN_UNROLL = 16
VMEM_LIMIT = 56 * 1024 * 1024


def _cp(sem=None):
    return pltpu.CompilerParams(dimension_semantics=sem, vmem_limit_bytes=VMEM_LIMIT)


def _dot(a, b, dims=(((1,), (0,)), ((), ()))):
    return lax.dot_general(a.astype(MXU_DTYPE), b.astype(MXU_DTYPE), dims, preferred_element_type=F32)


def _dot_nt(a, b):
    return _dot(a, b, (((1,), (1,)), ((), ())))


def _dot_tn(a, b):
    return _dot(a, b, (((0,), (0,)), ((), ())))


def _seg_dot(xb, bd):
    n = xb.shape[1]
    if n <= 256:
        return jnp.dot(xb, bd[:n, :n], preferred_element_type=F32)
    parts = [jnp.dot(xb[:, c:c + 256], bd, preferred_element_type=F32) for c in range(0, n, 256)]
    return jnp.concatenate(parts, axis=1)


def _segsum_raw(x, bd):
    rows = x.shape[0]
    hi = x.astype(MXU_DTYPE)
    lo = (x - hi.astype(F32)).astype(MXU_DTYPE)
    both = _seg_dot(jnp.concatenate([hi, lo], axis=0), bd)
    return both[:rows] + both[rows:]


@jax.custom_vjp
def _segsum_d(x, bd):
    return _segsum_raw(x, bd)


def _segsum_d_fwd(x, bd):
    return _segsum_raw(x, bd), bd


def _segsum_d_bwd(bd, ct):
    return _segsum_raw(ct, bd), jnp.zeros_like(bd)


_segsum_d.defvjp(_segsum_d_fwd, _segsum_d_bwd)


def _rope_tables(T):
    t = jnp.arange(T, dtype=F32)
    row = jnp.floor(t / GRID_W)
    col = t - row * GRID_W
    n_freq = HEAD_DIM // 4
    inv_freq = ROPE_THETA ** (-jnp.arange(n_freq, dtype=F32) / n_freq)
    d = jnp.arange(HEAD_DIM)
    pos = jnp.where((d < HEAD_DIM // 2)[None, :], row[:, None], col[:, None])
    ang = pos * inv_freq[d % n_freq][None, :]
    sign = jnp.where((d % 32) < 16, -1.0, 1.0).astype(F32)[None, :]
    cos = jnp.cos(ang)
    sin = jnp.sin(ang) * sign
    return jnp.tile(cos, (1, 2)), jnp.tile(sin, (1, 2))


def _rope_raw(x, cos, sin):
    n = x.shape[1]
    lane = lax.broadcasted_iota(jnp.int32, (1, n), 1)
    first = (lane % 32) < 16
    partner = jnp.where(first, pltpu.roll(x, n - 16, 1), pltpu.roll(x, 16, 1))
    return x * cos + partner * sin


@jax.custom_vjp
def _rope_d(x, cos, sin):
    return _rope_raw(x, cos, sin)


def _rope_d_fwd(x, cos, sin):
    return _rope_raw(x, cos, sin), (cos, sin)


def _rope_d_bwd(res, ct):
    cos, sin = res
    return _rope_raw(ct, cos, -sin), jnp.zeros_like(cos), jnp.zeros_like(sin)


_rope_d.defvjp(_rope_d_fwd, _rope_d_bwd)


def _rms(x, g):
    return x * lax.rsqrt(jnp.mean(x * x, axis=-1, keepdims=True) + NORM_EPS) * g


def _pre_fn(x, shift, scale, g_pre):
    return _rms(x, g_pre) * (1.0 + scale) + shift


def _qk_fn(q, g, cos, sin, bd, scale, diff):
    segsum = _segsum_d if diff else _segsum_raw
    rope = _rope_d if diff else _rope_raw
    qn = q * lax.rsqrt(segsum(q * q, bd) * (1.0 / HEAD_DIM) + NORM_EPS) * g
    return rope(qn, cos, sin) * scale


def _silu(x):
    return x * jax.nn.sigmoid(x)


def _rwkv_pw(k, pw0, pw1, pa0, pa1, w0, a0, k_k, k_a, bd, diff):
    segsum = _segsum_d if diff else _segsum_raw
    kk = k * k_k
    kk = kk * lax.rsqrt(segsum(kk * kk, bd) + L2_EPS)
    ws, kts, akks = [], [], []
    for z, (pw, pa) in enumerate(((pw0, pa0), (pw1, pa1))):
        w = jnp.exp(-DECAY_SCALE * jax.nn.sigmoid(w0[z:z + 1, :] + pw))
        a = jax.nn.sigmoid(a0[z:z + 1, :] + pa)
        ws.append(w)
        kts.append(k * (1.0 + (a - 1.0) * k_a))
        akks.append(a * kk)
    return ws[0], ws[1], kts[0], kts[1], akks[0], akks[1], kk


def _mix_fn(y_att, g_att, ys, r, v, kts, g_rw, gn_w, gn_b, r_k, bd, diff):
    segsum = _segsum_d if diff else _segsum_raw
    mu = segsum(ys, bd) * (1.0 / HEAD_DIM)
    d = ys - mu
    var = segsum(d * d, bd) * (1.0 / HEAD_DIM)
    yn = d * lax.rsqrt(var + GN_EPS) * gn_w + gn_b
    bonus = segsum(r * kts * r_k, bd) * v
    return y_att * _silu(g_att), (yn + bonus) * _silu(g_rw)


def _loss_fn(out, x, tgt, gate, g_post):
    e = x + gate * _rms(out, g_post) - tgt
    s = jnp.sum(e * e, axis=1, keepdims=True)
    return jnp.sum(s, axis=0, keepdims=True) * (0.5 / D_MODEL)


def _exchange(arrays, modes, name):
    n = len(arrays)
    out_shape = tuple(
        jax.ShapeDtypeStruct(((NDEV,) + tuple(a.shape)) if mode == "all" else tuple(a.shape), a.dtype)
        for a, mode in zip(arrays, modes))
    chips = (4, 2, 6)

    def body(*refs):
        ins, outs = refs[:n], refs[n:2 * n]
        send_sems, recv_sems, local_sems = refs[2 * n:]
        ix, iy, ic = lax.axis_index("x"), lax.axis_index("y"), lax.axis_index("c")
        me = 4 * ix + 2 * iy + ic

        def peer(m):
            px = 1 - ix if (m >> 2) & 1 else ix
            py = 1 - iy if (m >> 1) & 1 else iy
            pc = 1 - ic if m & 1 else ic
            return (px, py, pc), 4 * px + 2 * py + pc

        def copy(k, j, src_ref, slot, to):
            return pltpu.make_async_remote_copy(src_ref=src_ref, dst_ref=outs[k].at[slot], send_sem=send_sems.at[k, j],
                                                recv_sem=recv_sems.at[k, j], device_id=to, device_id_type=MESH)

        local, sends, arrivals, forwards = [], [], [], []
        for k in range(n):
            if modes[k] == "scatter":
                local.append(pltpu.make_async_copy(ins[k].at[me], outs[k].at[me], local_sems.at[k]))
                for m in range(1, NDEV):
                    to, p = peer(m)
                    sends.append(copy(k, m - 1, ins[k].at[p], me, to))
                    arrivals.append(copy(k, m - 1, ins[k].at[p], p, to))
            elif modes[k] == "chips":
                mine = me // 2
                local.append(pltpu.make_async_copy(ins[k].at[mine], outs[k].at[mine], local_sems.at[k]))
                for j, m in enumerate(chips):
                    to, p = peer(m)
                    sends.append(copy(k, j, ins[k].at[p // 2], mine, to))
                    arrivals.append(copy(k, j, ins[k].at[p // 2], p // 2, to))
            else:
                local.append(pltpu.make_async_copy(ins[k], outs[k].at[me], local_sems.at[k]))
                sib, sib_slot = peer(1)
                sends.append(copy(k, 0, ins[k], me, sib))
                for j, m in enumerate(chips):
                    to, p = peer(m)
                    sends.append(copy(k, 1 + j, ins[k], me, to))
                    forwards.append((copy(k, 1 + j, ins[k], p, to), copy(k, 4 + j, outs[k].at[p], p, sib)))
                    arrivals.append(copy(k, 4 + j, ins[k], peer(m ^ 1)[1], sib))
                arrivals.append(copy(k, 0, ins[k], sib_slot, sib))
        for cp in local + sends:
            cp.start()
        for arrived, onward in forwards:
            arrived.wait_recv()
            onward.start()
        for cp in arrivals:
            cp.wait_recv()
        for cp in sends + [onward for _, onward in forwards]:
            cp.wait_send()
        for cp in local:
            cp.wait()

    any_spec = pl.BlockSpec(memory_space=pl.ANY)
    return pl.pallas_call(
        body, name=name, out_shape=out_shape,
        in_specs=[any_spec] * n, out_specs=tuple([any_spec] * n),
        scratch_shapes=[pltpu.SemaphoreType.DMA((n, NDEV - 1)), pltpu.SemaphoreType.DMA((n, NDEV - 1)),
                        pltpu.SemaphoreType.DMA((n,))],
    )(*arrays)


def _pair_sum_call(parts, name):
    n = len(parts)

    def body(*refs):
        in_r, out_r, mine_r, land_r = (refs[j * n:(j + 1) * n] for j in range(4))
        send_sems, recv_sems, local_sems = refs[4 * n:]
        core = lax.axis_index("c")
        sibling = (lax.axis_index("x"), lax.axis_index("y"), 1 - core)
        local = [pltpu.make_async_copy(in_r[k].at[core], mine_r[k], local_sems.at[k]) for k in range(n)]
        swaps = [pltpu.make_async_remote_copy(src_ref=in_r[k].at[1 - core], dst_ref=land_r[k], send_sem=send_sems.at[k],
                                              recv_sem=recv_sems.at[k], device_id=sibling, device_id_type=MESH)
                 for k in range(n)]
        for cp in local + swaps:
            cp.start()
        for k in range(n):
            local[k].wait()
            swaps[k].wait()
            out_r[k][...] = (mine_r[k][...].astype(F32) + land_r[k][...].astype(F32)).astype(out_r[k].dtype)

    halves = [jax.ShapeDtypeStruct(a.shape[1:], a.dtype) for a in parts]
    return pl.pallas_call(
        body, name=name, out_shape=tuple(halves), in_specs=[pl.BlockSpec(memory_space=pl.ANY)] * n,
        scratch_shapes=[pltpu.VMEM(h.shape, h.dtype) for h in halves] * 2 + [pltpu.SemaphoreType.DMA((n,))] * 3,
        compiler_params=pltpu.CompilerParams(vmem_limit_bytes=VMEM_LIMIT),
    )(*parts)


def _mod_call(c_all, w_ada, b_cols):
    def body(c_ref, w_ref, b_ref, o_ref):
        o_ref[...] = _dot(_silu(c_ref[...]), w_ref[...]) + b_ref[...]

    return pl.pallas_call(body, name="mod_fwd",
                          out_shape=jax.ShapeDtypeStruct((c_all.shape[0], w_ada.shape[1]), F32))(c_all, w_ada, b_cols)


def _wada_grad_call(c_all, dmod_cols):
    def body(c_ref, d_ref, o_ref):
        o_ref[...] = _dot_tn(_silu(c_ref[...]), d_ref[...])

    return pl.pallas_call(body, name="w_ada_grad",
                          out_shape=jax.ShapeDtypeStruct((c_all.shape[1], dmod_cols.shape[1]), F32))(c_all, dmod_cols)


def _full(shape):
    nd = len(shape)
    return pl.BlockSpec(shape, lambda *_: (0,) * nd)


def _in_proj_call(x2, shift, scale, g_pre, w_in, qg, kg, cos, sin, bd, T):
    R = x2.shape[0]
    TT = min(ROW_TILE, T)
    tpe = T // TT

    def body(x_ref, sh_ref, sc_ref, gp_ref, w_ref, qg_ref, kg_ref, cos_ref, sin_ref, bd_ref,
             hb_ref, qr_ref, kpad_ref, vpad_ref, qraw_ref, kraw_ref, gatt_ref, rin_ref, grw_ref):
        h = _pre_fn(x_ref[...], sh_ref[0], sc_ref[0], gp_ref[...])
        hb = h.astype(MXU_DTYPE)
        hb_ref[...] = hb

        def proj(c0, c1):
            return _dot_nt(hb, w_ref[c0:c1, :])

        q = proj(C_Q, C_K)
        k = proj(C_K, C_V)
        v = proj(C_V, C_GA)
        gatt_ref[...] = proj(C_GA, C_RIN)
        rin_ref[...] = proj(C_RIN, C_GRW)
        grw_ref[...] = proj(C_GRW, C_END)
        qraw_ref[...] = q
        kraw_ref[...] = k
        cos, sin, bd = cos_ref[...], sin_ref[...], bd_ref[...]
        qr = _qk_fn(q, qg_ref[...], jnp.tile(cos, (1, 4)), jnp.tile(sin, (1, 4)), bd, ATT_SCALE, False)
        qr_ref[...] = qr.astype(MXU_DTYPE)
        kr = _qk_fn(k, kg_ref[...], cos, sin, bd, 1.0, False)
        left = lax.broadcasted_iota(jnp.int32, (1, KV_W), 1) < HEAD_DIM
        for ref, val in ((kpad_ref, kr), (vpad_ref, v)):
            h0l = jnp.where(left, val, 0.0)
            h1r = jnp.where(left, 0.0, val)
            ref[0] = h0l.astype(MXU_DTYPE)
            ref[1] = pltpu.roll(h0l, HEAD_DIM, 1).astype(MXU_DTYPE)
            ref[2] = pltpu.roll(h1r, HEAD_DIM, 1).astype(MXU_DTYPE)
            ref[3] = h1r.astype(MXU_DTYPE)

    row = lambda w: pl.BlockSpec((TT, w), lambda i: (i, 0))
    per_ex = pl.BlockSpec((1, 1, D_MODEL), lambda i: (i // tpe, 0, 0))
    tab = pl.BlockSpec((TT, KV_W), lambda i: (i % tpe, 0))
    pad = pl.BlockSpec((4, TT, KV_W), lambda i: (0, i, 0))
    sds = jax.ShapeDtypeStruct
    return pl.pallas_call(
        body, name="in_proj", grid=(R // TT,),
        in_specs=[row(D_MODEL), per_ex, per_ex, _full((1, D_MODEL)), _full(w_in.shape), _full((1, ATT_W)),
                  _full((1, KV_W)), tab, tab, _full((256, 256))],
        out_specs=(row(D_MODEL), row(ATT_W), pad, pad, row(ATT_W), row(KV_W), row(ATT_W), row(SHIFT_W), row(RWKV_W)),
        out_shape=(sds((R, D_MODEL), MXU_DTYPE), sds((R, ATT_W), MXU_DTYPE), sds((4, R, KV_W), MXU_DTYPE),
                   sds((4, R, KV_W), MXU_DTYPE), sds((R, ATT_W), F32), sds((R, KV_W), F32), sds((R, ATT_W), F32),
                   sds((R, SHIFT_W), F32), sds((R, RWKV_W), F32)),
        compiler_params=_cp(("arbitrary",)),
    )(x2, shift, scale, g_pre, w_in, qg, kg, cos, sin, bd)


def _softmax_parts(s):
    e = jnp.exp(s - jnp.max(s, axis=1, keepdims=True))
    return e, 1.0 / jnp.sum(e, axis=1, keepdims=True)


def _att_specs(T, TQ):
    nq = T // TQ
    qspec = pl.BlockSpec((TQ, KV_W), lambda b, p, i: (b * nq + i, p))
    side = lambda s: pl.BlockSpec((None, T, KV_W), lambda b, p, i: (2 * (p // 2) + s, b, 0))
    return nq, qspec, side


def _att_fwd_call(qr, kpad, vpad, B, T):
    TQ = min(ATT_TILE_FWD, T)
    nq, qspec, side = _att_specs(T, TQ)

    def body(q_ref, kl_ref, kr_ref, vl_ref, vr_ref, o_ref):
        q = q_ref[...]
        ea, inv_a = _softmax_parts(_dot_nt(q, kl_ref[...]))
        eb, inv_b = _softmax_parts(_dot_nt(q, kr_ref[...]))
        o_ref[...] = _dot(ea, vl_ref[...]) * inv_a + _dot(eb, vr_ref[...]) * inv_b

    return pl.pallas_call(
        body, name="att_fwd", grid=(B, 4, nq),
        in_specs=[qspec, side(0), side(1), side(0), side(1)], out_specs=qspec,
        out_shape=jax.ShapeDtypeStruct((B * T, ATT_W), F32),
        compiler_params=_cp(("arbitrary",) * 3),
    )(qr, kpad, kpad, vpad, vpad)


def _att_bwd_call(qr, kpad, vpad, d_o, B, T):
    TQ = min(ATT_TILE_BWD, T)
    nq, qspec, side = _att_specs(T, TQ)

    def body(q_ref, kl_ref, kr_ref, vl_ref, vr_ref, do_ref, dq_ref, dk_ref, dv_ref):
        i = pl.program_id(2)
        q, do = q_ref[...], do_ref[...]
        left = lax.broadcasted_iota(jnp.int32, (1, KV_W), 1) < HEAD_DIM
        dq = jnp.zeros((TQ, KV_W), F32)
        dk = jnp.zeros((T, KV_W), F32)
        dv = jnp.zeros((T, KV_W), F32)
        for k_ref, v_ref, mask in ((kl_ref, vl_ref, left), (kr_ref, vr_ref, jnp.logical_not(left))):
            kk, vv = k_ref[...], v_ref[...]
            e, inv = _softmax_parts(_dot_nt(q, kk))
            dp = _dot_nt(do, vv)
            ds = e * (dp - inv * jnp.sum(e * dp, axis=1, keepdims=True))
            dq = dq + _dot(ds, kk) * inv
            dk = dk + _dot_tn(ds, jnp.where(mask, q * inv, 0.0))
            dv = dv + _dot_tn(e, jnp.where(mask, do * inv, 0.0))
        dq_ref[...] = dq

        @pl.when(i == 0)
        def _():
            dk_ref[...] = dk
            dv_ref[...] = dv

        @pl.when(i > 0)
        def _():
            dk_ref[...] += dk
            dv_ref[...] += dv

    acc = pl.BlockSpec((None, T, KV_W), lambda b, p, i: (p, b, 0))
    sds = jax.ShapeDtypeStruct
    return pl.pallas_call(
        body, name="att_bwd", grid=(B, 4, nq),
        in_specs=[qspec, side(0), side(1), side(0), side(1), qspec], out_specs=(qspec, acc, acc),
        out_shape=(sds((B * T, ATT_W), F32), sds((4, B * T, KV_W), F32), sds((4, B * T, KV_W), F32)),
        compiler_params=_cp(("arbitrary",) * 3),
    )(qr, kpad, kpad, vpad, vpad, d_o)


def _shift_specs(R, T, TT, width):
    tpe = T // TT
    nb8 = R // 8
    cur = pl.BlockSpec((TT, width), lambda i: (i, 0))
    prev = pl.BlockSpec((8, width), lambda i: (jnp.maximum(i * (TT // 8) - 1, 0), 0))
    nxt = pl.BlockSpec((8, width), lambda i: (jnp.minimum((i + 1) * (TT // 8), nb8 - 1), 0))
    return tpe, cur, prev, nxt


def _neighbours(cur, prev8, next8, i, tpe, TT):
    rows = lax.broadcasted_iota(jnp.int32, (TT, 1), 0)
    first = jnp.where(i % tpe == 0, 0.0, 1.0)
    last = jnp.where(i % tpe == tpe - 1, 0.0, 1.0)
    before = jnp.where(rows == 0, prev8[7:8, :] * first, pltpu.roll(cur, 1, 0))
    after = jnp.where(rows == TT - 1, next8[0:1, :] * last, pltpu.roll(cur, TT - 1, 0))
    return before, after


def _shift_fwd_call(x, taps, T):
    R, width = x.shape
    TT = min(ROW_TILE, T)
    tpe, cur, prev, nxt = _shift_specs(R, T, TT, width)

    def body(x_ref, p_ref, n_ref, t_ref, o_ref, vh_ref):
        xc = x_ref[...]
        before, after = _neighbours(xc, p_ref[...], n_ref[...], pl.program_id(0), tpe, TT)
        out = t_ref[0:1, :] * before + t_ref[1:2, :] * xc + t_ref[2:3, :] * after
        o_ref[...] = out
        left = lax.broadcasted_iota(jnp.int32, (1, KV_W), 1) < HEAD_DIM
        for p in range(RWKV_W // KV_W):
            pair = out[:, 2 * RWKV_W + p * KV_W:2 * RWKV_W + (p + 1) * KV_W]
            vh_ref[:, 2 * p * KV_W:(2 * p + 1) * KV_W] = jnp.where(left, pair, 0.0)
            vh_ref[:, (2 * p + 1) * KV_W:(2 * p + 2) * KV_W] = jnp.where(left, pltpu.roll(pair, HEAD_DIM, 1), 0.0)

    return pl.pallas_call(
        body, name="shift_fwd", grid=(R // TT,), in_specs=[cur, prev, nxt, _full(taps.shape)],
        out_specs=(cur, pl.BlockSpec((TT, 2 * RWKV_W), lambda i: (i, 0))),
        out_shape=(jax.ShapeDtypeStruct((R, width), F32), jax.ShapeDtypeStruct((R, 2 * RWKV_W), F32)),
        compiler_params=_cp(("arbitrary",)),
    )(x, x, x, taps)


def _shift_bwd_call(x, d, taps, T):
    R, width = x.shape
    TT = min(ROW_TILE, T)
    tpe, cur, prev, nxt = _shift_specs(R, T, TT, width)

    def body(x_ref, xp_ref, xn_ref, d_ref, dp_ref, dn_ref, t_ref, dx_ref, dt_ref):
        i = pl.program_id(0)
        xc, dc = x_ref[...], d_ref[...]
        d_before, d_after = _neighbours(dc, dp_ref[...], dn_ref[...], i, tpe, TT)
        dx_ref[...] = t_ref[2:3, :] * d_before + t_ref[1:2, :] * dc + t_ref[0:1, :] * d_after
        x_before, x_after = _neighbours(xc, xp_ref[...], xn_ref[...], i, tpe, TT)
        @pl.when(i == 0)
        def _():
            dt_ref[...] = jnp.zeros_like(dt_ref)

        for j, xs in enumerate((x_before, xc, x_after)):
            dt_ref[j:j + 1, :] += jnp.sum(dc * xs, axis=0, keepdims=True)

    return pl.pallas_call(
        body, name="shift_bwd", grid=(R // TT,),
        in_specs=[cur, prev, nxt, cur, prev, nxt, _full(taps.shape)], out_specs=(cur, _full((8, width))),
        out_shape=(jax.ShapeDtypeStruct((R, width), F32), jax.ShapeDtypeStruct((8, width), F32)),
        compiler_params=_cp(("arbitrary",)),
    )(x, x, x, d, d, d, taps)


def _lora_in(wa):
    lane = lax.broadcasted_iota(jnp.int32, (1, LORA_W), 1)
    return jnp.where(lane < LORA_W // 2, jnp.tanh(wa), wa)


def _rwkv_prep_call(shifted, wup, aup, w0, a0, k_k, k_a, bd, T):
    R = shifted.shape[0]
    TT = min(ROW_TILE, T)

    def body(k_ref, wa_ref, wup_ref, aup_ref, w0_ref, a0_ref, kk_ref, ka_ref, bd_ref, w_o, kt_o, akk_o, kk_o):
        twa = _lora_in(wa_ref[...])
        pre = [_dot(twa, m_ref[z]) for m_ref in (wup_ref, aup_ref) for z in range(2)]
        outs = _rwkv_pw(k_ref[...], pre[0], pre[1], pre[2], pre[3], w0_ref[...], a0_ref[...], kk_ref[...],
                        ka_ref[...], bd_ref[...], False)
        w_o[0], w_o[1], kt_o[0], kt_o[1], akk_o[0], akk_o[1] = outs[:6]
        kk_o[...] = outs[6]

    col = lambda c, w: pl.BlockSpec((TT, w), lambda i: (i, c))
    two = pl.BlockSpec((2, TT, RWKV_W), lambda i: (0, i, 0))
    sds = jax.ShapeDtypeStruct
    return pl.pallas_call(
        body, name="rwkv_prep", grid=(R // TT,),
        in_specs=[col(1, RWKV_W), col(3 * RWKV_W // LORA_W, LORA_W), _full(wup.shape), _full(aup.shape),
                  _full((2, RWKV_W)), _full((2, RWKV_W)), _full((1, RWKV_W)), _full((1, RWKV_W)), _full((256, 256))],
        out_specs=(two, two, two, col(0, RWKV_W)),
        out_shape=(sds((2, R, RWKV_W), F32),) * 3 + (sds((R, RWKV_W), F32),),
        compiler_params=_cp(("arbitrary",)),
    )(shifted, shifted, wup, aup, w0, a0, k_k, k_a, bd)


def _rwkv_prep_bwd_call(shifted, cts, wup, aup, w0, a0, k_k, k_a, bd, T):
    R = shifted.shape[0]
    TT = min(ROW_TILE, T)

    def body(k_ref, wa_ref, dw0, dkt0, dakk0, dkk0, dr0, dv0, dw1, dkt1, dakk1, dkk1, dr1, dv1, dr2_ref, dv2_ref, dkts_ref,
             wup_ref, aup_ref, w0_ref, a0_ref, kk_ref, ka_ref, bd_ref,
             dsh_ref, gwup_ref, gaup_ref, gw0_ref, ga0_ref, gkk_ref, gka_ref):
        dw_ref, dkt_ref, dakk_ref, dkk_ref, dr_ref, dv_ref = ((dw0, dw1), (dkt0, dkt1), (dakk0, dakk1), (dkk0, dkk1),
                                                              (dr0, dr1), (dv0, dv1))
        i = pl.program_id(0)
        wa = wa_ref[...]
        twa = _lora_in(wa)
        pre = [_dot(twa, m_ref[z]) for m_ref in (wup_ref, aup_ref) for z in range(2)]
        fn = functools.partial(_rwkv_pw, bd=bd_ref[...], diff=True)
        _, vjp = jax.vjp(fn, k_ref[...], pre[0], pre[1], pre[2], pre[3], w0_ref[...], a0_ref[...], kk_ref[...],
                         ka_ref[...])
        dkts = dkts_ref[...]
        dk, dpw0, dpw1, dpa0, dpa1, gw0, ga0, gkk, gka = vjp(
            (dw_ref[0][...], dw_ref[1][...], dkt_ref[0][...] + dkts, dkt_ref[1][...] + dkts, dakk_ref[0][...],
             dakk_ref[1][...], dkk_ref[0][...] + dkk_ref[1][...]))
        dtwa = (_dot_nt(dpw0, wup_ref[0]) + _dot_nt(dpw1, wup_ref[1]) + _dot_nt(dpa0, aup_ref[0])
                + _dot_nt(dpa1, aup_ref[1]))
        lane = lax.broadcasted_iota(jnp.int32, (1, LORA_W), 1)
        dsh_ref[:, 0:RWKV_W] = dr_ref[0][...] + dr_ref[1][...] + dr2_ref[...]
        dsh_ref[:, RWKV_W:2 * RWKV_W] = dk
        dsh_ref[:, 2 * RWKV_W:3 * RWKV_W] = dv_ref[0][...] + dv_ref[1][...] + dv2_ref[...]
        dsh_ref[:, 3 * RWKV_W:] = jnp.where(lane < LORA_W // 2, dtwa * (1.0 - twa * twa), dtwa)
        acc = ((gwup_ref.at[0], _dot_tn(twa, dpw0)), (gwup_ref.at[1], _dot_tn(twa, dpw1)),
               (gaup_ref.at[0], _dot_tn(twa, dpa0)), (gaup_ref.at[1], _dot_tn(twa, dpa1)),
               (gw0_ref, gw0), (ga0_ref, ga0), (gkk_ref, gkk), (gka_ref, gka))

        @pl.when(i == 0)
        def _():
            for ref, val in acc:
                ref[...] = val

        @pl.when(i > 0)
        def _():
            for ref, val in acc:
                ref[...] += val

    col = lambda c, w: pl.BlockSpec((TT, w), lambda i: (i, c))
    one = col(0, RWKV_W)
    sds = jax.ShapeDtypeStruct
    return pl.pallas_call(
        body, name="rwkv_prep_bwd", grid=(R // TT,),
        in_specs=[col(1, RWKV_W), col(3 * RWKV_W // LORA_W, LORA_W)] + [one] * 15 + [
                  _full(wup.shape), _full(aup.shape), _full((2, RWKV_W)), _full((2, RWKV_W)), _full((1, RWKV_W)),
                  _full((1, RWKV_W)), _full((256, 256))],
        out_specs=(pl.BlockSpec((TT, SHIFT_W), lambda i: (i, 0)), _full(wup.shape), _full(aup.shape),
                   _full((2, RWKV_W)), _full((2, RWKV_W)), _full((1, RWKV_W)), _full((1, RWKV_W))),
        out_shape=(sds((R, SHIFT_W), F32), sds(wup.shape, F32), sds(aup.shape, F32), sds((2, RWKV_W), F32),
                   sds((2, RWKV_W), F32), sds((1, RWKV_W), F32), sds((1, RWKV_W), F32)),
        compiler_params=_cp(("arbitrary",)),
    )(shifted, shifted, *cts, wup, aup, w0, a0, k_k, k_a, bd)


def _col_lhs(row, eye_b):
    return eye_b * row.astype(MXU_DTYPE)


def _colsum(x):
    return jnp.sum(x, axis=0, keepdims=True)


def _stacked_segsum(tiles, bd):
    res = _seg_dot(jnp.concatenate(tiles, axis=0), bd)
    return [res[j * HEAD_DIM:(j + 1) * HEAD_DIM] for j in range(len(tiles))]


def _scan_specs(B, T, C, nC):
    def blk(z, col, rev):
        idx = (lambda g: (z, 0, nC - 1 - g, col)) if rev else (lambda g: (z, 0, g, col))
        return pl.BlockSpec((None, B, C, RWKV_W), idx)

    def blk3(col, rev):
        idx = (lambda g: (0, nC - 1 - g, col)) if rev else (lambda g: (0, g, col))
        return pl.BlockSpec((B, C, RWKV_W), idx)

    return blk, blk3


def _scan_fwd_call(w, kt, akk, kk, shifted, eye_b, eye_f, bd, B, T):
    C = min(SCAN_CHUNK, T)
    nC = T // C
    blk, blk3 = _scan_specs(B, T, C, nC)

    def body(w0, kt0, akk0, kk0, v0, r0, w1, kt1, akk1, kk1, v1, r1, eb_ref, ef_ref, bd_ref, y0, y1, st, S):
        @pl.when(pl.program_id(0) == 0)
        def _():
            S[...] = jnp.zeros_like(S)

        st[0] = S[...].astype(MXU_DTYPE)
        dirs = ((w0, kt0, akk0, kk0, v0, r0, y0), (w1, kt1, akk1, kk1, v1, r1, y1))

        def step(s, carry):
            for z in range(2):
                row = s if z == 0 else C - 1 - s
                prev = jnp.maximum(s - 1, 0) if z == 0 else jnp.minimum(C - s, C - 1)
                wr, ktr, akkr, kkr, vr, rr, yr = dirs[z]
                tiles = []
                for b in range(B):
                    Sb = st[s, z * B + b]
                    tiles += [Sb * kkr[b, pl.ds(row, 1), :].astype(MXU_DTYPE),
                              _col_lhs(vr[b, pl.ds(row, 1), :], eb_ref[...]),
                              Sb * rr[b, pl.ds(prev, 1), :].astype(MXU_DTYPE)]
                res = _stacked_segsum(tiles, bd_ref[...])
                for b in range(B):
                    c = z * B + b
                    sab, vb, yb = res[3 * b:3 * b + 3]
                    ld = lambda ref: ref[b, pl.ds(row, 1), :]
                    Sn = S[c] * ld(wr) - sab * ld(akkr) + vb * ld(ktr)
                    S[c] = Sn
                    st[s + 1, c] = Sn.astype(MXU_DTYPE)
                    yr[b, pl.ds(prev, 1), :] = _colsum(ef_ref[...] * yb)
            return carry

        lax.fori_loop(0, C, step, 0, unroll=SCAN_UNROLL)
        for z in range(2):
            last = C - 1 if z == 0 else 0
            rr, yr = dirs[z][5], dirs[z][6]
            res = _stacked_segsum([st[C, z * B + b] * rr[b, last:last + 1, :].astype(MXU_DTYPE) for b in range(B)],
                                  bd_ref[...])
            for b in range(B):
                yr[b, last:last + 1, :] = _colsum(ef_ref[...] * res[b])

    ins, specs = [], []
    for z, rev in ((0, False), (1, True)):
        ins += [w, kt, akk, kk, shifted, shifted]
        specs += [blk(z, 0, rev), blk(z, 0, rev), blk(z, 0, rev), blk3(0, rev), blk3(2, rev), blk3(0, rev)]
    sds = jax.ShapeDtypeStruct
    return pl.pallas_call(
        body, name="scan_fwd", grid=(nC,),
        in_specs=specs + [_full((HEAD_DIM, RWKV_W)), _full((HEAD_DIM, RWKV_W)), _full((256, 256))],
        out_specs=(blk3(0, False), blk3(0, True),
                   pl.BlockSpec((None, C + 1, 2 * B, HEAD_DIM, RWKV_W), lambda g: (g, 0, 0, 0, 0))),
        out_shape=(sds((B, T, RWKV_W), F32), sds((B, T, RWKV_W), F32),
                   sds((nC, C + 1, 2 * B, HEAD_DIM, RWKV_W), MXU_DTYPE)),
        scratch_shapes=[pltpu.VMEM((2 * B, HEAD_DIM, RWKV_W), F32)],
        compiler_params=_cp(("arbitrary",)),
    )(*ins, eye_b, eye_f, bd)


def _scan_bwd_call(w, kt, akk, kk, shifted, v_heads, dys, st, eye_b, eye_f, bd, B, T):
    C = min(SCAN_CHUNK, T)
    nC = T // C
    blk, blk3 = _scan_specs(B, T, C, nC)
    nin = 7

    def body(*refs):
        d0, d1 = refs[:nin], refs[nin:2 * nin]
        st_ref, eb_ref, ef_ref, sel_ref, hm_ref, bd_ref = refs[2 * nin:2 * nin + 6]
        o0, o1 = refs[2 * nin + 6:2 * nin + 12], refs[2 * nin + 12:2 * nin + 18]
        COL, DYC, G = refs[2 * nin + 18:]

        @pl.when(pl.program_id(0) == 0)
        def _():
            G[...] = jnp.zeros_like(G)

        dirs = (d0 + (o0,), d1 + (o1,))

        def column_operands(s, z):
            row = s if z == 0 else C - 1 - s
            _, _, _, kkr, _, _, dyr, _ = dirs[z]
            tiles = []
            for b in range(B):
                tiles += [st_ref[s, z * B + b] * kkr[b, pl.ds(row, 1), :].astype(MXU_DTYPE),
                          _col_lhs(dyr[b, pl.ds(row, 1), :], eb_ref[...])]
            return tiles

        def keep_columns(res, z):
            for b in range(B):
                for k in range(2):
                    COL[k, z * B + b] = res[2 * b + k].astype(MXU_DTYPE)
                DYC[z * B + b] = res[2 * b + 1]

        for z in range(2):
            keep_columns(_stacked_segsum(column_operands(C - 1, z), bd_ref[...]), z)

        def bwd(it, carry):
            s = C - 1 - it
            for z in range(2):
                row = s if z == 0 else C - 1 - s
                wr, ktr, akkr, kkr, vr, rr, dyr, (dw_o, dkt_o, dakk_o, dkk_o, dr_o, dv_o) = dirs[z]
                tiles, Gcs = [], []
                for b in range(B):
                    c = z * B + b
                    Gc = G[c] + DYC[c] * rr[b, pl.ds(row, 1), :]
                    Gb = Gc.astype(MXU_DTYPE)
                    Gcs.append((Gc, Gb))
                    tiles += [Gb * akkr[b, pl.ds(row, 1), :].astype(MXU_DTYPE),
                              Gb * ktr[b, pl.ds(row, 1), :].astype(MXU_DTYPE)]
                res = _stacked_segsum(tiles + column_operands(jnp.maximum(s - 1, 0), z), bd_ref[...])
                for b in range(B):
                    c = z * B + b
                    Gc, Gb = Gcs[b]
                    gab, dvb = res[2 * b], res[2 * b + 1]
                    ld = lambda ref: ref[b, pl.ds(row, 1), :]
                    G[c] = Gc * ld(wr) - gab * ld(kkr)
                    Sb = st_ref[s, c]
                    prods = jnp.concatenate([Gb, st_ref[s + 1, c] * COL[1, c], Gb * Sb, Gb * COL[0, c],
                                             gab.astype(MXU_DTYPE) * Sb], axis=0)
                    v_rows = jnp.concatenate([vr[b, pl.ds(row, 1)][0], jnp.zeros((8, 3 * HEAD_DIM), F32)], axis=1)
                    lhs = jnp.concatenate([sel_ref[...], v_rows], axis=0).astype(MXU_DTYPE)
                    sums = jnp.dot(lhs, prods, preferred_element_type=F32)
                    for k, (ref, sign) in enumerate(((dr_o, 1.0), (dw_o, 1.0), (dakk_o, -1.0), (dkk_o, -1.0))):
                        ref[b, pl.ds(row, 1), :] = sign * sums[k:k + 1, :]
                    dkt_o[b, pl.ds(row, 1), :] = _colsum(sums[8:16] * hm_ref[...])
                    dv_o[b, pl.ds(row, 1), :] = _colsum(ef_ref[...] * dvb)
                keep_columns(res[2 * B:], z)
            return carry

        lax.fori_loop(0, C, bwd, 0, unroll=SCAN_UNROLL)

    ins, specs = [], []
    for z, rev in ((0, True), (1, False)):
        heads = pl.BlockSpec((B, C) + v_heads.shape[2:], (lambda g: (0, nC - 1 - g, 0, 0)) if rev else (lambda g: (0, g, 0, 0)))
        ins += [w, kt, akk, kk, v_heads, shifted, dys]
        specs += [blk(z, 0, rev), blk(z, 0, rev), blk(z, 0, rev), blk3(0, rev), heads, blk3(0, rev), blk3(0, rev)]
    sel = (jnp.arange(8)[:, None] + 1 == (jnp.arange(5 * HEAD_DIM) // HEAD_DIM)[None, :]).astype(F32)
    head_rows = (jnp.arange(RWKV_W // HEAD_DIM)[:, None] == (jnp.arange(RWKV_W) // HEAD_DIM)[None, :]).astype(F32)
    ins += [st, eye_b, eye_f, sel, head_rows, bd]
    specs += [pl.BlockSpec((None, C + 1, 2 * B, HEAD_DIM, RWKV_W), lambda g: (nC - 1 - g, 0, 0, 0, 0)),
              _full((HEAD_DIM, RWKV_W)), _full((HEAD_DIM, RWKV_W)), _full(sel.shape), _full(head_rows.shape),
              _full((256, 256))]
    sds = jax.ShapeDtypeStruct
    out_specs = tuple(blk3(0, True) for _ in range(6)) + tuple(blk3(0, False) for _ in range(6))
    res = pl.pallas_call(
        body, name="scan_bwd", grid=(nC,), in_specs=specs, out_specs=out_specs,
        out_shape=tuple(sds((B, T, RWKV_W), F32) for _ in range(12)),
        scratch_shapes=[pltpu.VMEM((2, 2 * B, HEAD_DIM, RWKV_W), MXU_DTYPE), pltpu.VMEM((2 * B, HEAD_DIM, RWKV_W), F32),
                        pltpu.VMEM((2 * B, HEAD_DIM, RWKV_W), F32)],
        compiler_params=_cp(("arbitrary",)),
    )(*ins)
    return list(res)


def _out_head_call(x2, tgt2, gate, y_att, g_att, y0, y1, shifted, kt, g_rw, w_out, g_post, gn_w, gn_b, r_k, bd, T):
    R = x2.shape[0]
    TT = min(ROW_TILE, T)
    tpe = T // TT

    def body(x_ref, t_ref, gate_ref, ya_ref, ga_ref, y0_ref, y1_ref, r_ref, v_ref, kt_ref, grw_ref, w_ref, gp_ref,
             gnw_ref, gnb_ref, rk_ref, bd_ref,
             loss_o, dy_o, dya_o, dga_o, dys_o, dr_o, dv_o, dkts_o, dgrw_o, dgate_o, gw_o, ggp_o, ggnw_o, ggnb_o, grk_o):
        i = pl.program_id(0)
        bd = bd_ref[...]
        mix = functools.partial(_mix_fn, bd=bd, diff=True)
        (ma, mr), mix_vjp = jax.vjp(mix, ya_ref[...], ga_ref[...], y0_ref[...] + y1_ref[...], r_ref[...], v_ref[...],
                                    kt_ref[0] + kt_ref[1], grw_ref[...], gnw_ref[...], gnb_ref[...], rk_ref[...])
        out = _dot(ma, w_ref[0:ATT_W, :]) + _dot(mr, w_ref[ATT_W:, :])
        loss, loss_vjp = jax.vjp(_loss_fn, out, x_ref[...], t_ref[...], gate_ref[0], gp_ref[...])
        d_out, dy, _, dgate, dgp = loss_vjp(jnp.ones((1, 1), F32))
        dy_o[...] = dy
        dma = _dot_nt(d_out, w_ref[0:ATT_W, :])
        dmr = _dot_nt(d_out, w_ref[ATT_W:, :])
        dya_o[...], dga_o[...], dys_o[...], dr_o[...], dv_o[...], dkts_o[...], dgrw_o[...], dgnw, dgnb, drk = \
            mix_vjp((dma, dmr))
        gw = jnp.concatenate([_dot_tn(ma, d_out), _dot_tn(mr, d_out)], axis=0)
        acc = ((loss_o, jnp.broadcast_to(loss, (8, 128))), (gw_o, gw), (ggp_o, dgp), (ggnw_o, dgnw), (ggnb_o, dgnb),
               (grk_o, drk))

        @pl.when(i == 0)
        def _():
            for ref, val in acc:
                ref[...] = val

        @pl.when(i > 0)
        def _():
            for ref, val in acc:
                ref[...] += val

        @pl.when(i % tpe == 0)
        def _():
            dgate_o[0] = dgate

        @pl.when(i % tpe > 0)
        def _():
            dgate_o[0] += dgate

    row = lambda w, c=0: pl.BlockSpec((TT, w), lambda i: (i, c))
    two = pl.BlockSpec((2, TT, RWKV_W), lambda i: (0, i, 0))
    per_ex = pl.BlockSpec((1, 1, D_MODEL), lambda i: (i // tpe, 0, 0))
    sds = jax.ShapeDtypeStruct
    r512 = sds((R, RWKV_W), F32)
    return pl.pallas_call(
        body, name="out_head", grid=(R // TT,),
        in_specs=[row(D_MODEL), row(D_MODEL), per_ex, row(ATT_W), row(ATT_W), row(RWKV_W), row(RWKV_W), row(RWKV_W, 0),
                  row(RWKV_W, 2), two,
                  row(RWKV_W), _full(w_out.shape), _full((1, D_MODEL)), _full((1, RWKV_W)), _full((1, RWKV_W)),
                  _full((1, RWKV_W)), _full((256, 256))],
        out_specs=(_full((8, 128)), row(D_MODEL), row(ATT_W), row(ATT_W), row(RWKV_W), row(RWKV_W), row(RWKV_W),
                   row(RWKV_W), row(RWKV_W), per_ex, _full((D_MODEL, D_MODEL)), _full((1, D_MODEL)), _full((1, RWKV_W)),
                   _full((1, RWKV_W)), _full((1, RWKV_W))),
        out_shape=(sds((8, 128), F32), sds((R, D_MODEL), F32), r512, r512, r512, r512, r512, r512, r512,
                   sds((R // T, 1, D_MODEL), F32), sds((D_MODEL, D_MODEL), F32), sds((1, D_MODEL), F32),
                   sds((1, RWKV_W), F32), sds((1, RWKV_W), F32), sds((1, RWKV_W), F32)),
        compiler_params=_cp(("arbitrary",)),
    )(x2, tgt2, gate, y_att, g_att, y0, y1, shifted, shifted, kt, g_rw, w_out, g_post, gn_w, gn_b, r_k, bd)


def _in_proj_bwd_call(x2, dy, shift, scale, g_pre, w_in, qg, kg, cos, sin, bd, q_raw, k_raw, dqr, dkp, dvp,
                      d_gatt, d_rin, d_grw, T):
    R = x2.shape[0]
    TT = min(ROW_TILE, T)
    tpe = T // TT

    def body(x_ref, dy_ref, sh_ref, sc_ref, gp_ref, w_ref, qg_ref, kg_ref, cos_ref, sin_ref, bd_ref, q_ref, k_ref,
             dqr_ref, dkp_ref, dvp_ref, dga_ref, drin_ref, dgrw_ref,
             dx_o, dproj_o, dsh_o, dsc_o, ggp_o, gqg_o, gkg_o):
        i = pl.program_id(0)
        cos, sin, bd = cos_ref[...], sin_ref[...], bd_ref[...]
        left = lax.broadcasted_iota(jnp.int32, (1, KV_W), 1) < HEAD_DIM

        def kv_grad(ref):
            a = ref[0] + ref[1]
            b = ref[2] + ref[3]
            return jnp.where(left, a + pltpu.roll(a, HEAD_DIM, 1), b + pltpu.roll(b, HEAD_DIM, 1))

        qfn = functools.partial(_qk_fn, cos=jnp.tile(cos, (1, 4)), sin=jnp.tile(sin, (1, 4)), bd=bd, scale=ATT_SCALE,
                                diff=True)
        _, q_vjp = jax.vjp(qfn, q_ref[...], qg_ref[...])
        dq, gqg = q_vjp(dqr_ref[...])
        kfn = functools.partial(_qk_fn, cos=cos, sin=sin, bd=bd, scale=1.0, diff=True)
        _, k_vjp = jax.vjp(kfn, k_ref[...], kg_ref[...])
        dk, gkg = k_vjp(kv_grad(dkp_ref))
        pieces = ((C_Q, C_K, dq), (C_K, C_V, dk), (C_V, C_GA, kv_grad(dvp_ref)), (C_GA, C_RIN, dga_ref[...]),
                  (C_RIN, C_GRW, drin_ref[...]), (C_GRW, C_END, dgrw_ref[...]))
        dh = jnp.zeros((TT, D_MODEL), F32)
        for c0, c1, val in pieces:
            vb = val.astype(MXU_DTYPE)
            dproj_o[:, c0:c1] = vb
            dh = dh + _dot(vb, w_ref[c0:c1, :])
        _, pre_vjp = jax.vjp(_pre_fn, x_ref[...], sh_ref[0], sc_ref[0], gp_ref[...])
        dx, dsh, dsc, ggp = pre_vjp(dh)
        dx_o[...] = dx + dy_ref[...]
        acc = ((ggp_o, ggp), (gqg_o, gqg), (gkg_o, gkg))

        @pl.when(i == 0)
        def _():
            for ref, val in acc:
                ref[...] = val

        @pl.when(i > 0)
        def _():
            for ref, val in acc:
                ref[...] += val

        @pl.when(i % tpe == 0)
        def _():
            dsh_o[0] = dsh
            dsc_o[0] = dsc

        @pl.when(i % tpe > 0)
        def _():
            dsh_o[0] += dsh
            dsc_o[0] += dsc

    row = lambda w: pl.BlockSpec((TT, w), lambda i: (i, 0))
    per_ex = pl.BlockSpec((1, 1, D_MODEL), lambda i: (i // tpe, 0, 0))
    tab = pl.BlockSpec((TT, KV_W), lambda i: (i % tpe, 0))
    pad = pl.BlockSpec((4, TT, KV_W), lambda i: (0, i, 0))
    sds = jax.ShapeDtypeStruct
    nb = R // T
    return pl.pallas_call(
        body, name="in_proj_bwd", grid=(R // TT,),
        in_specs=[row(D_MODEL), row(D_MODEL), per_ex, per_ex, _full((1, D_MODEL)), _full(w_in.shape), _full((1, ATT_W)),
                  _full((1, KV_W)), tab, tab, _full((256, 256)), row(ATT_W), row(KV_W), row(ATT_W), pad, pad,
                  row(ATT_W), row(SHIFT_W), row(RWKV_W)],
        out_specs=(row(D_MODEL), row(C_END), per_ex, per_ex, _full((1, D_MODEL)), _full((1, ATT_W)), _full((1, KV_W))),
        out_shape=(sds((R, D_MODEL), F32), sds((R, C_END), MXU_DTYPE), sds((nb, 1, D_MODEL), F32),
                   sds((nb, 1, D_MODEL), F32), sds((1, D_MODEL), F32), sds((1, ATT_W), F32), sds((1, KV_W), F32)),
        compiler_params=_cp(("arbitrary",)),
    )(x2, dy, shift, scale, g_pre, w_in, qg, kg, cos, sin, bd, q_raw, k_raw, dqr, dkp, dvp, d_gatt, d_rin, d_grw)


def _w_in_grad_call(hb, dproj):
    R = hb.shape[0]
    TT = min(W_GRAD_ROWS, R)
    CB = 1152
    last = R // TT - 1

    def body(h_ref, d_ref, o_ref, acc):
        g = _dot_tn(h_ref[...], d_ref[...])

        @pl.when(pl.program_id(1) == 0)
        def _():
            acc[...] = g

        @pl.when(pl.program_id(1) > 0)
        def _():
            acc[...] += g

        @pl.when(pl.program_id(1) == last)
        def _():
            o_ref[...] = acc[...].astype(o_ref.dtype)

    return pl.pallas_call(
        body, name="w_in_grad", grid=(C_END // CB, R // TT),
        in_specs=[pl.BlockSpec((TT, D_MODEL), lambda j, i: (i, 0)), pl.BlockSpec((TT, CB), lambda j, i: (i, j))],
        out_specs=pl.BlockSpec((D_MODEL, CB), lambda j, i: (0, j)),
        out_shape=jax.ShapeDtypeStruct((D_MODEL, C_END), MXU_DTYPE),
        scratch_shapes=[pltpu.VMEM((D_MODEL, CB), F32)], compiler_params=_cp(("arbitrary", "arbitrary")),
    )(hb, dproj)


def _adam_refs(p_ref, w_ref, m_ref, v_ref, g_o, d_o, m_o, v_o):
    g = p_ref[0].astype(F32)
    for j in range(1, p_ref.shape[0]):
        g = g + p_ref[j].astype(F32)
    m2 = ADAM_B1 * m_ref[...] + (1.0 - ADAM_B1) * g
    v2 = ADAM_B2 * v_ref[...] + (1.0 - ADAM_B2) * jnp.square(g)
    m_hat = m2 / (1.0 - ADAM_B1 ** ADAM_STEP)
    v_hat = v2 / (1.0 - ADAM_B2 ** ADAM_STEP)
    g_o[...] = g
    d_o[...] = -ADAM_LR * (m_hat / (jnp.sqrt(v_hat) + ADAM_EPS) + ADAM_WD * w_ref[...])
    m_o[...] = m2
    v_o[...] = v2


def _adam_small_call(items, name):
    n = len(items)

    def body(*refs):
        for k in range(n):
            _adam_refs(*refs[4 * k:4 * k + 4], *refs[4 * n + 4 * k:4 * n + 4 * k + 4])

    out_shape = tuple(jax.ShapeDtypeStruct(w.shape, F32) for _, w, _, _ in items for _ in range(4))
    out = pl.pallas_call(body, name=name, out_shape=out_shape)(*[a for item in items for a in item])
    return [out[4 * k:4 * k + 4] for k in range(n)]


def _adam_call(parts, w, m, v, name, row_tile=None):
    P, M, N = parts.shape
    TM = M if row_tile is None else row_tile

    def body(*refs):
        _adam_refs(*refs)

    blk = pl.BlockSpec((TM, N), lambda i: (i, 0))
    return pl.pallas_call(
        body, name=name, grid=(M // TM,),
        in_specs=[pl.BlockSpec((P, TM, N), lambda i: (0, i, 0)), blk, blk, blk], out_specs=(blk,) * 4,
        out_shape=(jax.ShapeDtypeStruct((M, N), F32),) * 4, compiler_params=_cp(("arbitrary",)),
    )(parts, w, m, v)


_SMALL_ROWS = 136


def _pack_small(taps, w_up, w0, a_up, a0):
    flat = jnp.concatenate([taps.reshape(-1), w_up.reshape(-1), w0.reshape(-1), a_up.reshape(-1), a0.reshape(-1)])
    return jnp.pad(flat, (0, _SMALL_ROWS * 128 - flat.shape[0])).reshape(_SMALL_ROWS, 128)


def _unpack_small(packed):
    n = packed.shape[0]
    flat = packed.reshape(n, -1)
    out, o = [], 0
    for shape in ((3, 208), (2, 64, 64), (2, 64), (2, 64, 64), (2, 64)):
        size = 1
        for s in shape:
            size *= s
        out.append(flat[:, o:o + size].reshape((n,) + shape))
        o += size
    return out


def _cols_to_full(blocks):
    nd = blocks.ndim
    moved = jnp.moveaxis(blocks, 0, nd - 2)
    return moved.reshape(moved.shape[:-2] + (moved.shape[-2] * moved.shape[-1],))


def _full_to_cols(full):
    k = full.shape[-1] // NDEV
    return jnp.moveaxis(full.reshape(full.shape[:-1] + (NDEV, k)), -2, 0)


_REP_SIZES = (("g_pre", 1024), ("q_norm_g", 64), ("k_norm_g", 64), ("k_k", 512), ("k_a", 512), ("r_k", 512),
              ("gn_w", 512), ("gn_b", 512), ("g_post", 1024))
_REP_ROWS = 40


def kernel(x, c, w_ada, b_ada, g_pre, w_in, q_norm_g, k_norm_g, shift_taps, w_up, w0, a_up, a0, k_k, k_a, r_k, gn_w, gn_b, w_out, g_post, loss_target, m_w_ada, m_b_ada, m_g_pre, m_w_in, m_q_norm_g, m_k_norm_g, m_shift_taps, m_w_up, m_w0, m_a_up, m_a0, m_k_k, m_k_a, m_r_k, m_gn_w, m_gn_b, m_w_out, m_g_post, v_w_ada, v_b_ada, v_g_pre, v_w_in, v_q_norm_g, v_k_norm_g, v_shift_taps, v_w_up, v_w0, v_a_up, v_a0, v_k_k, v_k_a, v_r_k, v_gn_w, v_gn_b, v_w_out, v_g_post):
    B, T, _ = x.shape
    R = B * T
    me = 4 * lax.axis_index("x") + 2 * lax.axis_index("y") + lax.axis_index("c")
    x2 = x.reshape(R, D_MODEL)
    tgt2 = loss_target.reshape(R, D_MODEL)

    seg = jnp.arange(256) // HEAD_DIM
    bd = (seg[:, None] == seg[None, :]).astype(MXU_DTYPE)
    eye = (jnp.arange(HEAD_DIM)[:, None] == (jnp.arange(RWKV_W) % HEAD_DIM)[None, :])
    eye_b, eye_f = eye.astype(MXU_DTYPE), eye.astype(F32)
    cos, sin = _rope_tables(T)

    c_g, w_in_g, w_out_g, small_g = _exchange(
        [c, w_in[0].T.astype(MXU_DTYPE), w_out[0].astype(MXU_DTYPE),
         _pack_small(shift_taps[0], w_up[0], w0[0], a_up[0], a0[0])], ["all"] * 4, "gather_params")
    c_all = c_g.reshape(NDEV * B, D_MODEL)
    w_in_f = w_in_g.reshape(C_END, D_MODEL)
    w_out_f = w_out_g.reshape(D_MODEL, D_MODEL)
    taps_b, w_up_b, w0_b, a_up_b, a0_b = _unpack_small(small_g)
    taps_f = jnp.pad(_cols_to_full(taps_b), ((0, 5), (0, 0)))
    w_up_f, a_up_f = _cols_to_full(w_up_b), _cols_to_full(a_up_b)
    w0_f, a0_f = _cols_to_full(w0_b), _cols_to_full(a0_b)
    wup_pad = jnp.pad(w_up_f, ((0, 0), (0, 64), (0, 0))).astype(MXU_DTYPE)
    aup_pad = jnp.pad(a_up_f, ((0, 0), (64, 0), (0, 0))).astype(MXU_DTYPE)

    ncol = w_ada.shape[2]
    b_cols = lax.dynamic_slice(b_ada, (0, me * ncol), (1, ncol))
    mod_cols = _mod_call(c_all, w_ada[0].astype(MXU_DTYPE), b_cols)
    (mod_g,) = _exchange([mod_cols], ["all"], "gather_mod")
    mod = lax.dynamic_slice(_cols_to_full(mod_g), (me * B, 0), (B, 3 * D_MODEL))
    shift, scale, gate = [mod[:, j * D_MODEL:(j + 1) * D_MODEL].reshape(B, 1, D_MODEL) for j in range(3)]

    qg = jnp.tile(q_norm_g, (1, ATT_W // HEAD_DIM))
    kg = jnp.tile(k_norm_g, (1, KV_W // HEAD_DIM))
    rk_row = r_k.reshape(1, RWKV_W)

    hb, qr, kpad, vpad, q_raw, k_raw, g_att, rin, g_rw = _in_proj_call(
        x2, shift, scale, g_pre, w_in_f, qg, kg, cos, sin, bd, T)
    y_att = _att_fwd_call(qr, kpad, vpad, B, T)
    shifted, v_rows = _shift_fwd_call(rin, taps_f, T)
    w_s, kt_s, akk_s, kk_s = _rwkv_prep_call(shifted, wup_pad, aup_pad, w0_f, a0_f, k_k, k_a, bd, T)
    sh3 = shifted.reshape(B, T, SHIFT_W)
    r4 = lambda a: a.reshape(2, B, T, RWKV_W)
    y0, y1, st = _scan_fwd_call(r4(w_s), r4(kt_s), r4(akk_s), kk_s.reshape(B, T, RWKV_W), sh3, eye_b, eye_f, bd, B, T)

    (loss_blk, dy, d_yatt, d_gatt, d_ys, d_r2, d_v2, d_kts, d_grw, d_gate, g_wout, g_gpost, g_gnw, g_gnb,
     g_rk) = _out_head_call(x2, tgt2, gate, y_att, g_att, y0.reshape(R, RWKV_W), y1.reshape(R, RWKV_W), shifted, kt_s,
                            g_rw, w_out_f, g_post, gn_w, gn_b, rk_row, bd, T)
    v_heads = v_rows.reshape(B, T, RWKV_W // HEAD_DIM, 2 * HEAD_DIM)
    scan_cts = _scan_bwd_call(r4(w_s), r4(kt_s), r4(akk_s), kk_s.reshape(B, T, RWKV_W), sh3, v_heads,
                              d_ys.reshape(B, T, RWKV_W), st, eye_b, eye_f, bd, B, T)
    scan_cts = [a.reshape(R, RWKV_W) for a in scan_cts]
    d_shifted, g_wup, g_aup, g_w0, g_a0, g_kk, g_ka = _rwkv_prep_bwd_call(
        shifted, scan_cts + [d_r2, d_v2, d_kts], wup_pad, aup_pad, w0_f, a0_f, k_k, k_a, bd, T)
    d_rin, g_taps = _shift_bwd_call(rin, d_shifted, taps_f, T)
    dqr, dkp, dvp = _att_bwd_call(qr, kpad, vpad, d_yatt, B, T)
    grad_x, dproj, d_shift, d_scale, g_gpre, g_qg, g_kg = _in_proj_bwd_call(
        x2, dy, shift, scale, g_pre, w_in_f, qg, kg, cos, sin, bd, q_raw, k_raw, dqr, dkp, dvp, d_gatt, d_rin, d_grw, T)
    g_win = _w_in_grad_call(hb, dproj)

    rep = jnp.concatenate([g_gpre.reshape(-1), g_qg.reshape(-1, HEAD_DIM).sum(0), g_kg.reshape(-1, HEAD_DIM).sum(0),
                           g_kk.reshape(-1), g_ka.reshape(-1), g_rk.reshape(-1), g_gnw.reshape(-1), g_gnb.reshape(-1),
                           g_gpost.reshape(-1), loss_blk[0, :1]])
    rep = jnp.pad(rep, (0, _REP_ROWS * 128 - rep.shape[0])).reshape(_REP_ROWS, 128)
    dmod = jnp.concatenate([d_shift, d_scale, d_gate], axis=2).reshape(B, 3 * D_MODEL)
    small_parts = jax.vmap(_pack_small)(_full_to_cols(g_taps[:3]), _full_to_cols(g_wup[:, :64, :]), _full_to_cols(g_w0),
                                        _full_to_cols(g_aup[:, 64:, :]), _full_to_cols(g_a0))
    by_core = lambda a: jnp.swapaxes(a.reshape((NDEV // 2, 2) + a.shape[1:]), 0, 1).astype(MXU_DTYPE)
    s_win, s_wout = _pair_sum_call(
        [by_core(_full_to_cols(g_win)), by_core(g_wout.reshape(NDEV, D_MODEL // NDEV, D_MODEL))], "reduce_pair")
    p_win, p_wout, p_small, dmod_g, rep_g = _exchange(
        [s_win, s_wout, small_parts, dmod, rep], ["chips", "chips", "scatter", "all", "all"], "reduce_grads")
    dmod_all = dmod_g.reshape(NDEV * B, 3 * D_MODEL)
    g_wada = _wada_grad_call(c_all, lax.dynamic_slice(dmod_all, (0, me * ncol), (NDEV * B, ncol)))

    res, small = {}, []

    def adam(name, parts, w, m, v, row_tile=None, alone=False):
        two_d = (-1, w.shape[-1])
        item = (parts.reshape((parts.shape[0],) + w.reshape(two_d).shape), w.reshape(two_d), m.reshape(two_d),
                v.reshape(two_d))
        if alone:
            res[name] = [o.reshape(w.shape) for o in _adam_call(*item, "adam_" + name, row_tile)]
        else:
            small.append((name, w.shape, item))

    adam("w_ada", g_wada[None], w_ada, m_w_ada, v_w_ada, alone=True)
    adam("b_ada", dmod_all.reshape(NDEV * B, 1, 3 * D_MODEL), b_ada, m_b_ada, v_b_ada)
    adam("w_in", p_win, w_in, m_w_in, v_w_in, 128, alone=True)
    adam("w_out", p_wout, w_out, m_w_out, v_w_out, alone=True)
    taps_p, wup_p, w0_p, aup_p, a0_p = _unpack_small(p_small)
    adam("shift_taps", taps_p, shift_taps, m_shift_taps, v_shift_taps)
    adam("w_up", wup_p, w_up, m_w_up, v_w_up)
    adam("w0", w0_p, w0, m_w0, v_w0)
    adam("a_up", aup_p, a_up, m_a_up, v_a_up)
    adam("a0", a0_p, a0, m_a0, v_a0)
    rep_flat = rep_g.reshape(NDEV, -1)
    off = 0
    given = dict(g_pre=(g_pre, m_g_pre, v_g_pre), q_norm_g=(q_norm_g, m_q_norm_g, v_q_norm_g),
                 k_norm_g=(k_norm_g, m_k_norm_g, v_k_norm_g), k_k=(k_k, m_k_k, v_k_k), k_a=(k_a, m_k_a, v_k_a),
                 r_k=(r_k, m_r_k, v_r_k), gn_w=(gn_w, m_gn_w, v_gn_w), gn_b=(gn_b, m_gn_b, v_gn_b),
                 g_post=(g_post, m_g_post, v_g_post))
    for name, size in _REP_SIZES:
        adam(name, rep_flat[:, off:off + size], *given[name])
        off += size
    for (name, shape, _), out in zip(small, _adam_small_call([item for _, _, item in small], "adam_small")):
        res[name] = [o.reshape(shape) for o in out]

    loss = jnp.sum(rep_flat[:, off])
    order = ["w_ada", "b_ada", "g_pre", "w_in", "q_norm_g", "k_norm_g", "shift_taps", "w_up", "w0", "a_up", "a0", "k_k",
             "k_a", "r_k", "gn_w", "gn_b", "w_out", "g_post"]
    return (loss, grad_x.reshape(B, T, D_MODEL), *[res[n][0] for n in order], *[res[n][1] for n in order],
            *[res[n][2] for n in order], *[res[n][3] for n in order])
```

```python
import functools

import jax
import jax.numpy as jnp
from jax import lax
from jax.experimental import pallas as pl
from jax.experimental.pallas import tpu as pltpu

F32 = jnp.float32
MXU_DTYPE = jnp.bfloat16
MESH = pl.DeviceIdType.MESH
NDEV = 8

D_MODEL = 1024
HEAD_DIM = 64
ATT_W = 512
KV_W = 128
RWKV_W = 512
LORA_W = 128
SHIFT_W = 3 * RWKV_W + LORA_W
GRID_W = 64
ROPE_THETA = 10000.0
DECAY_SCALE = 0.6065306597126334
NORM_EPS = 1e-6
GN_EPS = 64e-5
L2_EPS = 1e-12
ATT_SCALE = HEAD_DIM ** -0.5
C_Q, C_K, C_V, C_GA, C_RIN, C_GRW, C_END = 0, 512, 640, 768, 1280, 2944, 3456

ADAM_LR, ADAM_B1, ADAM_B2, ADAM_EPS, ADAM_WD, ADAM_STEP = 0.001, 0.9, 0.999, 1e-08, 0.01, 10

ROW_TILE = 256
W_GRAD_ROWS = 2048
ATT_TILE_FWD = 256
ATT_TILE_BWD = 1024
SCAN_CHUNK = 64
SCAN_UNROLL = 16
VMEM_LIMIT = 56 * 1024 * 1024


def _cp(sem=None):
    return pltpu.CompilerParams(dimension_semantics=sem, vmem_limit_bytes=VMEM_LIMIT)


def _dot(a, b, dims=(((1,), (0,)), ((), ()))):
    return lax.dot_general(a.astype(MXU_DTYPE), b.astype(MXU_DTYPE), dims, preferred_element_type=F32)


def _dot_nt(a, b):
    return _dot(a, b, (((1,), (1,)), ((), ())))


def _dot_tn(a, b):
    return _dot(a, b, (((0,), (0,)), ((), ())))


def _seg_dot(xb, bd):
    n = xb.shape[1]
    if n <= 256:
        return jnp.dot(xb, bd[:n, :n], preferred_element_type=F32)
    parts = [jnp.dot(xb[:, c:c + 256], bd, preferred_element_type=F32) for c in range(0, n, 256)]
    return jnp.concatenate(parts, axis=1)


def _segsum_raw(x, bd):
    rows = x.shape[0]
    hi = x.astype(MXU_DTYPE)
    lo = (x - hi.astype(F32)).astype(MXU_DTYPE)
    both = _seg_dot(jnp.concatenate([hi, lo], axis=0), bd)
    return both[:rows] + both[rows:]


@jax.custom_vjp
def _segsum_d(x, bd):
    return _segsum_raw(x, bd)


def _segsum_d_fwd(x, bd):
    return _segsum_raw(x, bd), bd


def _segsum_d_bwd(bd, ct):
    return _segsum_raw(ct, bd), jnp.zeros_like(bd)


_segsum_d.defvjp(_segsum_d_fwd, _segsum_d_bwd)


def _rope_tables(T):
    t = jnp.arange(T, dtype=F32)
    row = jnp.floor(t / GRID_W)
    col = t - row * GRID_W
    n_freq = HEAD_DIM // 4
    inv_freq = ROPE_THETA ** (-jnp.arange(n_freq, dtype=F32) / n_freq)
    d = jnp.arange(HEAD_DIM)
    pos = jnp.where((d < HEAD_DIM // 2)[None, :], row[:, None], col[:, None])
    ang = pos * inv_freq[d % n_freq][None, :]
    sign = jnp.where((d % 32) < 16, -1.0, 1.0).astype(F32)[None, :]
    cos = jnp.cos(ang)
    sin = jnp.sin(ang) * sign
    return jnp.tile(cos, (1, 2)), jnp.tile(sin, (1, 2))


def _rope_raw(x, cos, sin):
    n = x.shape[1]
    lane = lax.broadcasted_iota(jnp.int32, (1, n), 1)
    first = (lane % 32) < 16
    partner = jnp.where(first, pltpu.roll(x, n - 16, 1), pltpu.roll(x, 16, 1))
    return x * cos + partner * sin


@jax.custom_vjp
def _rope_d(x, cos, sin):
    return _rope_raw(x, cos, sin)


def _rope_d_fwd(x, cos, sin):
    return _rope_raw(x, cos, sin), (cos, sin)


def _rope_d_bwd(res, ct):
    cos, sin = res
    return _rope_raw(ct, cos, -sin), jnp.zeros_like(cos), jnp.zeros_like(sin)


_rope_d.defvjp(_rope_d_fwd, _rope_d_bwd)


def _rms(x, g):
    return x * lax.rsqrt(jnp.mean(x * x, axis=-1, keepdims=True) + NORM_EPS) * g


def _pre_fn(x, shift, scale, g_pre):
    return _rms(x, g_pre) * (1.0 + scale) + shift


def _qk_fn(q, g, cos, sin, bd, scale, diff):
    segsum = _segsum_d if diff else _segsum_raw
    rope = _rope_d if diff else _rope_raw
    qn = q * lax.rsqrt(segsum(q * q, bd) * (1.0 / HEAD_DIM) + NORM_EPS) * g
    return rope(qn, cos, sin) * scale


def _silu(x):
    return x * jax.nn.sigmoid(x)


def _rwkv_pw(k, pw0, pw1, pa0, pa1, w0, a0, k_k, k_a, bd, diff):
    segsum = _segsum_d if diff else _segsum_raw
    kk = k * k_k
    kk = kk * lax.rsqrt(segsum(kk * kk, bd) + L2_EPS)
    ws, kts, akks = [], [], []
    for z, (pw, pa) in enumerate(((pw0, pa0), (pw1, pa1))):
        w = jnp.exp(-DECAY_SCALE * jax.nn.sigmoid(w0[z:z + 1, :] + pw))
        a = jax.nn.sigmoid(a0[z:z + 1, :] + pa)
        ws.append(w)
        kts.append(k * (1.0 + (a - 1.0) * k_a))
        akks.append(a * kk)
    return ws[0], ws[1], kts[0], kts[1], akks[0], akks[1], kk


def _mix_fn(y_att, g_att, ys, r, v, kts, g_rw, gn_w, gn_b, r_k, bd, diff):
    segsum = _segsum_d if diff else _segsum_raw
    mu = segsum(ys, bd) * (1.0 / HEAD_DIM)
    d = ys - mu
    var = segsum(d * d, bd) * (1.0 / HEAD_DIM)
    yn = d * lax.rsqrt(var + GN_EPS) * gn_w + gn_b
    bonus = segsum(r * kts * r_k, bd) * v
    return y_att * _silu(g_att), (yn + bonus) * _silu(g_rw)


def _loss_fn(out, x, tgt, gate, g_post):
    e = x + gate * _rms(out, g_post) - tgt
    s = jnp.sum(e * e, axis=1, keepdims=True)
    return jnp.sum(s, axis=0, keepdims=True) * (0.5 / D_MODEL)


def _exchange(arrays, modes, name):
    n = len(arrays)
    out_shape = tuple(
        jax.ShapeDtypeStruct(((NDEV,) + tuple(a.shape)) if mode == "all" else tuple(a.shape), a.dtype)
        for a, mode in zip(arrays, modes))
    chips = (4, 2, 6)

    def body(*refs):
        ins, outs = refs[:n], refs[n:2 * n]
        send_sems, recv_sems, local_sems = refs[2 * n:]
        ix, iy, ic = lax.axis_index("x"), lax.axis_index("y"), lax.axis_index("c")
        me = 4 * ix + 2 * iy + ic

        def peer(m):
            px = 1 - ix if (m >> 2) & 1 else ix
            py = 1 - iy if (m >> 1) & 1 else iy
            pc = 1 - ic if m & 1 else ic
            return (px, py, pc), 4 * px + 2 * py + pc

        def copy(k, j, src_ref, slot, to):
            return pltpu.make_async_remote_copy(src_ref=src_ref, dst_ref=outs[k].at[slot], send_sem=send_sems.at[k, j],
                                                recv_sem=recv_sems.at[k, j], device_id=to, device_id_type=MESH)

        local, sends, arrivals, forwards = [], [], [], []
        for k in range(n):
            if modes[k] == "scatter":
                local.append(pltpu.make_async_copy(ins[k].at[me], outs[k].at[me], local_sems.at[k]))
                for m in range(1, NDEV):
                    to, p = peer(m)
                    sends.append(copy(k, m - 1, ins[k].at[p], me, to))
                    arrivals.append(copy(k, m - 1, ins[k].at[p], p, to))
            elif modes[k] == "chips":
                mine = me // 2
                local.append(pltpu.make_async_copy(ins[k].at[mine], outs[k].at[mine], local_sems.at[k]))
                for j, m in enumerate(chips):
                    to, p = peer(m)
                    sends.append(copy(k, j, ins[k].at[p // 2], mine, to))
                    arrivals.append(copy(k, j, ins[k].at[p // 2], p // 2, to))
            else:
                local.append(pltpu.make_async_copy(ins[k], outs[k].at[me], local_sems.at[k]))
                sib, sib_slot = peer(1)
                sends.append(copy(k, 0, ins[k], me, sib))
                for j, m in enumerate(chips):
                    to, p = peer(m)
                    sends.append(copy(k, 1 + j, ins[k], me, to))
                    forwards.append((copy(k, 1 + j, ins[k], p, to), copy(k, 4 + j, outs[k].at[p], p, sib)))
                    arrivals.append(copy(k, 4 + j, ins[k], peer(m ^ 1)[1], sib))
                arrivals.append(copy(k, 0, ins[k], sib_slot, sib))
        for cp in local + sends:
            cp.start()
        for arrived, onward in forwards:
            arrived.wait_recv()
            onward.start()
        for cp in arrivals:
            cp.wait_recv()
        for cp in sends + [onward for _, onward in forwards]:
            cp.wait_send()
        for cp in local:
            cp.wait()

    any_spec = pl.BlockSpec(memory_space=pl.ANY)
    return pl.pallas_call(
        body, name=name, out_shape=out_shape,
        in_specs=[any_spec] * n, out_specs=tuple([any_spec] * n),
        scratch_shapes=[pltpu.SemaphoreType.DMA((n, NDEV - 1)), pltpu.SemaphoreType.DMA((n, NDEV - 1)),
                        pltpu.SemaphoreType.DMA((n,))],
    )(*arrays)


def _pair_sum_call(parts, name):
    n = len(parts)

    def body(*refs):
        in_r, out_r, mine_r, land_r = (refs[j * n:(j + 1) * n] for j in range(4))
        send_sems, recv_sems, local_sems = refs[4 * n:]
        core = lax.axis_index("c")
        sibling = (lax.axis_index("x"), lax.axis_index("y"), 1 - core)
        local = [pltpu.make_async_copy(in_r[k].at[core], mine_r[k], local_sems.at[k]) for k in range(n)]
        swaps = [pltpu.make_async_remote_copy(src_ref=in_r[k].at[1 - core], dst_ref=land_r[k], send_sem=send_sems.at[k],
                                              recv_sem=recv_sems.at[k], device_id=sibling, device_id_type=MESH)
                 for k in range(n)]
        for cp in local + swaps:
            cp.start()
        for k in range(n):
            local[k].wait()
            swaps[k].wait()
            out_r[k][...] = (mine_r[k][...].astype(F32) + land_r[k][...].astype(F32)).astype(out_r[k].dtype)

    halves = [jax.ShapeDtypeStruct(a.shape[1:], a.dtype) for a in parts]
    return pl.pallas_call(
        body, name=name, out_shape=tuple(halves), in_specs=[pl.BlockSpec(memory_space=pl.ANY)] * n,
        scratch_shapes=[pltpu.VMEM(h.shape, h.dtype) for h in halves] * 2 + [pltpu.SemaphoreType.DMA((n,))] * 3,
        compiler_params=pltpu.CompilerParams(vmem_limit_bytes=VMEM_LIMIT),
    )(*parts)


def _mod_call(c_all, w_ada, b_cols):
    def body(c_ref, w_ref, b_ref, o_ref):
        o_ref[...] = _dot(_silu(c_ref[...]), w_ref[...]) + b_ref[...]

    return pl.pallas_call(body, name="mod_fwd",
                          out_shape=jax.ShapeDtypeStruct((c_all.shape[0], w_ada.shape[1]), F32))(c_all, w_ada, b_cols)


def _wada_grad_call(c_all, dmod_cols):
    def body(c_ref, d_ref, o_ref):
        o_ref[...] = _dot_tn(_silu(c_ref[...]), d_ref[...])

    return pl.pallas_call(body, name="w_ada_grad",
                          out_shape=jax.ShapeDtypeStruct((c_all.shape[1], dmod_cols.shape[1]), F32))(c_all, dmod_cols)


def _full(shape):
    nd = len(shape)
    return pl.BlockSpec(shape, lambda *_: (0,) * nd)


def _in_proj_call(x2, shift, scale, g_pre, w_in, qg, kg, cos, sin, bd, T):
    R = x2.shape[0]
    TT = min(ROW_TILE, T)
    tpe = T // TT

    def body(x_ref, sh_ref, sc_ref, gp_ref, w_ref, qg_ref, kg_ref, cos_ref, sin_ref, bd_ref,
             hb_ref, qr_ref, kpad_ref, vpad_ref, qraw_ref, kraw_ref, gatt_ref, rin_ref, grw_ref):
        h = _pre_fn(x_ref[...], sh_ref[0], sc_ref[0], gp_ref[...])
        hb = h.astype(MXU_DTYPE)
        hb_ref[...] = hb

        def proj(c0, c1):
            return _dot_nt(hb, w_ref[c0:c1, :])

        q = proj(C_Q, C_K)
        k = proj(C_K, C_V)
        v = proj(C_V, C_GA)
        gatt_ref[...] = proj(C_GA, C_RIN)
        rin_ref[...] = proj(C_RIN, C_GRW)
        grw_ref[...] = proj(C_GRW, C_END)
        qraw_ref[...] = q
        kraw_ref[...] = k
        cos, sin, bd = cos_ref[...], sin_ref[...], bd_ref[...]
        qr = _qk_fn(q, qg_ref[...], jnp.tile(cos, (1, 4)), jnp.tile(sin, (1, 4)), bd, ATT_SCALE, False)
        qr_ref[...] = qr.astype(MXU_DTYPE)
        kr = _qk_fn(k, kg_ref[...], cos, sin, bd, 1.0, False)
        left = lax.broadcasted_iota(jnp.int32, (1, KV_W), 1) < HEAD_DIM
        for ref, val in ((kpad_ref, kr), (vpad_ref, v)):
            h0l = jnp.where(left, val, 0.0)
            h1r = jnp.where(left, 0.0, val)
            ref[0] = h0l.astype(MXU_DTYPE)
            ref[1] = pltpu.roll(h0l, HEAD_DIM, 1).astype(MXU_DTYPE)
            ref[2] = pltpu.roll(h1r, HEAD_DIM, 1).astype(MXU_DTYPE)
            ref[3] = h1r.astype(MXU_DTYPE)

    row = lambda w: pl.BlockSpec((TT, w), lambda i: (i, 0))
    per_ex = pl.BlockSpec((1, 1, D_MODEL), lambda i: (i // tpe, 0, 0))
    tab = pl.BlockSpec((TT, KV_W), lambda i: (i % tpe, 0))
    pad = pl.BlockSpec((4, TT, KV_W), lambda i: (0, i, 0))
    sds = jax.ShapeDtypeStruct
    return pl.pallas_call(
        body, name="in_proj", grid=(R // TT,),
        in_specs=[row(D_MODEL), per_ex, per_ex, _full((1, D_MODEL)), _full(w_in.shape), _full((1, ATT_W)),
                  _full((1, KV_W)), tab, tab, _full((256, 256))],
        out_specs=(row(D_MODEL), row(ATT_W), pad, pad, row(ATT_W), row(KV_W), row(ATT_W), row(SHIFT_W), row(RWKV_W)),
        out_shape=(sds((R, D_MODEL), MXU_DTYPE), sds((R, ATT_W), MXU_DTYPE), sds((4, R, KV_W), MXU_DTYPE),
                   sds((4, R, KV_W), MXU_DTYPE), sds((R, ATT_W), F32), sds((R, KV_W), F32), sds((R, ATT_W), F32),
                   sds((R, SHIFT_W), F32), sds((R, RWKV_W), F32)),
        compiler_params=_cp(("arbitrary",)),
    )(x2, shift, scale, g_pre, w_in, qg, kg, cos, sin, bd)


def _softmax_parts(s):
    e = jnp.exp(s - jnp.max(s, axis=1, keepdims=True))
    return e, 1.0 / jnp.sum(e, axis=1, keepdims=True)


def _att_specs(T, TQ):
    nq = T // TQ
    qspec = pl.BlockSpec((TQ, KV_W), lambda b, p, i: (b * nq + i, p))
    side = lambda s: pl.BlockSpec((None, T, KV_W), lambda b, p, i: (2 * (p // 2) + s, b, 0))
    return nq, qspec, side


def _att_fwd_call(qr, kpad, vpad, B, T):
    TQ = min(ATT_TILE_FWD, T)
    nq, qspec, side = _att_specs(T, TQ)

    def body(q_ref, kl_ref, kr_ref, vl_ref, vr_ref, o_ref):
        q = q_ref[...]
        ea, inv_a = _softmax_parts(_dot_nt(q, kl_ref[...]))
        eb, inv_b = _softmax_parts(_dot_nt(q, kr_ref[...]))
        o_ref[...] = _dot(ea, vl_ref[...]) * inv_a + _dot(eb, vr_ref[...]) * inv_b

    return pl.pallas_call(
        body, name="att_fwd", grid=(B, 4, nq),
        in_specs=[qspec, side(0), side(1), side(0), side(1)], out_specs=qspec,
        out_shape=jax.ShapeDtypeStruct((B * T, ATT_W), F32),
        compiler_params=_cp(("arbitrary",) * 3),
    )(qr, kpad, kpad, vpad, vpad)


def _att_bwd_call(qr, kpad, vpad, d_o, B, T):
    TQ = min(ATT_TILE_BWD, T)
    nq, qspec, side = _att_specs(T, TQ)

    def body(q_ref, kl_ref, kr_ref, vl_ref, vr_ref, do_ref, dq_ref, dk_ref, dv_ref):
        i = pl.program_id(2)
        q, do = q_ref[...], do_ref[...]
        left = lax.broadcasted_iota(jnp.int32, (1, KV_W), 1) < HEAD_DIM
        dq = jnp.zeros((TQ, KV_W), F32)
        dk = jnp.zeros((T, KV_W), F32)
        dv = jnp.zeros((T, KV_W), F32)
        for k_ref, v_ref, mask in ((kl_ref, vl_ref, left), (kr_ref, vr_ref, jnp.logical_not(left))):
            kk, vv = k_ref[...], v_ref[...]
            e, inv = _softmax_parts(_dot_nt(q, kk))
            dp = _dot_nt(do, vv)
            ds = e * (dp - inv * jnp.sum(e * dp, axis=1, keepdims=True))
            dq = dq + _dot(ds, kk) * inv
            dk = dk + _dot_tn(ds, jnp.where(mask, q * inv, 0.0))
            dv = dv + _dot_tn(e, jnp.where(mask, do * inv, 0.0))
        dq_ref[...] = dq

        @pl.when(i == 0)
        def _():
            dk_ref[...] = dk
            dv_ref[...] = dv

        @pl.when(i > 0)
        def _():
            dk_ref[...] += dk
            dv_ref[...] += dv

    acc = pl.BlockSpec((None, T, KV_W), lambda b, p, i: (p, b, 0))
    sds = jax.ShapeDtypeStruct
    return pl.pallas_call(
        body, name="att_bwd", grid=(B, 4, nq),
        in_specs=[qspec, side(0), side(1), side(0), side(1), qspec], out_specs=(qspec, acc, acc),
        out_shape=(sds((B * T, ATT_W), F32), sds((4, B * T, KV_W), F32), sds((4, B * T, KV_W), F32)),
        compiler_params=_cp(("arbitrary",) * 3),
    )(qr, kpad, kpad, vpad, vpad, d_o)


def _shift_specs(R, T, TT, width):
    tpe = T // TT
    nb8 = R // 8
    cur = pl.BlockSpec((TT, width), lambda i: (i, 0))
    prev = pl.BlockSpec((8, width), lambda i: (jnp.maximum(i * (TT // 8) - 1, 0), 0))
    nxt = pl.BlockSpec((8, width), lambda i: (jnp.minimum((i + 1) * (TT // 8), nb8 - 1), 0))
    return tpe, cur, prev, nxt


def _neighbours(cur, prev8, next8, i, tpe, TT):
    rows = lax.broadcasted_iota(jnp.int32, (TT, 1), 0)
    first = jnp.where(i % tpe == 0, 0.0, 1.0)
    last = jnp.where(i % tpe == tpe - 1, 0.0, 1.0)
    before = jnp.where(rows == 0, prev8[7:8, :] * first, pltpu.roll(cur, 1, 0))
    after = jnp.where(rows == TT - 1, next8[0:1, :] * last, pltpu.roll(cur, TT - 1, 0))
    return before, after


def _shift_fwd_call(x, taps, T):
    R, width = x.shape
    TT = min(ROW_TILE, T)
    tpe, cur, prev, nxt = _shift_specs(R, T, TT, width)

    def body(x_ref, p_ref, n_ref, t_ref, o_ref, vh_ref):
        xc = x_ref[...]
        before, after = _neighbours(xc, p_ref[...], n_ref[...], pl.program_id(0), tpe, TT)
        out = t_ref[0:1, :] * before + t_ref[1:2, :] * xc + t_ref[2:3, :] * after
        o_ref[...] = out
        left = lax.broadcasted_iota(jnp.int32, (1, KV_W), 1) < HEAD_DIM
        for p in range(RWKV_W // KV_W):
            pair = out[:, 2 * RWKV_W + p * KV_W:2 * RWKV_W + (p + 1) * KV_W]
            vh_ref[:, 2 * p * KV_W:(2 * p + 1) * KV_W] = jnp.where(left, pair, 0.0)
            vh_ref[:, (2 * p + 1) * KV_W:(2 * p + 2) * KV_W] = jnp.where(left, pltpu.roll(pair, HEAD_DIM, 1), 0.0)

    return pl.pallas_call(
        body, name="shift_fwd", grid=(R // TT,), in_specs=[cur, prev, nxt, _full(taps.shape)],
        out_specs=(cur, pl.BlockSpec((TT, 2 * RWKV_W), lambda i: (i, 0))),
        out_shape=(jax.ShapeDtypeStruct((R, width), F32), jax.ShapeDtypeStruct((R, 2 * RWKV_W), F32)),
        compiler_params=_cp(("arbitrary",)),
    )(x, x, x, taps)


def _shift_bwd_call(x, d, taps, T):
    R, width = x.shape
    TT = min(ROW_TILE, T)
    tpe, cur, prev, nxt = _shift_specs(R, T, TT, width)

    def body(x_ref, xp_ref, xn_ref, d_ref, dp_ref, dn_ref, t_ref, dx_ref, dt_ref):
        i = pl.program_id(0)
        xc, dc = x_ref[...], d_ref[...]
        d_before, d_after = _neighbours(dc, dp_ref[...], dn_ref[...], i, tpe, TT)
        dx_ref[...] = t_ref[2:3, :] * d_before + t_ref[1:2, :] * dc + t_ref[0:1, :] * d_after
        x_before, x_after = _neighbours(xc, xp_ref[...], xn_ref[...], i, tpe, TT)
        @pl.when(i == 0)
        def _():
            dt_ref[...] = jnp.zeros_like(dt_ref)

        for j, xs in enumerate((x_before, xc, x_after)):
            dt_ref[j:j + 1, :] += jnp.sum(dc * xs, axis=0, keepdims=True)

    return pl.pallas_call(
        body, name="shift_bwd", grid=(R // TT,),
        in_specs=[cur, prev, nxt, cur, prev, nxt, _full(taps.shape)], out_specs=(cur, _full((8, width))),
        out_shape=(jax.ShapeDtypeStruct((R, width), F32), jax.ShapeDtypeStruct((8, width), F32)),
        compiler_params=_cp(("arbitrary",)),
    )(x, x, x, d, d, d, taps)


def _lora_in(wa):
    lane = lax.broadcasted_iota(jnp.int32, (1, LORA_W), 1)
    return jnp.where(lane < LORA_W // 2, jnp.tanh(wa), wa)


def _rwkv_prep_call(shifted, wup, aup, w0, a0, k_k, k_a, bd, T):
    R = shifted.shape[0]
    TT = min(ROW_TILE, T)

    def body(k_ref, wa_ref, wup_ref, aup_ref, w0_ref, a0_ref, kk_ref, ka_ref, bd_ref, w_o, kt_o, akk_o, kk_o):
        twa = _lora_in(wa_ref[...])
        pre = [_dot(twa, m_ref[z]) for m_ref in (wup_ref, aup_ref) for z in range(2)]
        outs = _rwkv_pw(k_ref[...], pre[0], pre[1], pre[2], pre[3], w0_ref[...], a0_ref[...], kk_ref[...],
                        ka_ref[...], bd_ref[...], False)
        w_o[0], w_o[1], kt_o[0], kt_o[1], akk_o[0], akk_o[1] = outs[:6]
        kk_o[...] = outs[6]

    col = lambda c, w: pl.BlockSpec((TT, w), lambda i: (i, c))
    two = pl.BlockSpec((2, TT, RWKV_W), lambda i: (0, i, 0))
    sds = jax.ShapeDtypeStruct
    return pl.pallas_call(
        body, name="rwkv_prep", grid=(R // TT,),
        in_specs=[col(1, RWKV_W), col(3 * RWKV_W // LORA_W, LORA_W), _full(wup.shape), _full(aup.shape),
                  _full((2, RWKV_W)), _full((2, RWKV_W)), _full((1, RWKV_W)), _full((1, RWKV_W)), _full((256, 256))],
        out_specs=(two, two, two, col(0, RWKV_W)),
        out_shape=(sds((2, R, RWKV_W), F32),) * 3 + (sds((R, RWKV_W), F32),),
        compiler_params=_cp(("arbitrary",)),
    )(shifted, shifted, wup, aup, w0, a0, k_k, k_a, bd)


def _rwkv_prep_bwd_call(shifted, cts, wup, aup, w0, a0, k_k, k_a, bd, T):
    R = shifted.shape[0]
    TT = min(ROW_TILE, T)

    def body(k_ref, wa_ref, dw0, dkt0, dakk0, dkk0, dr0, dv0, dw1, dkt1, dakk1, dkk1, dr1, dv1, dr2_ref, dv2_ref, dkts_ref,
             wup_ref, aup_ref, w0_ref, a0_ref, kk_ref, ka_ref, bd_ref,
             dsh_ref, gwup_ref, gaup_ref, gw0_ref, ga0_ref, gkk_ref, gka_ref):
        dw_ref, dkt_ref, dakk_ref, dkk_ref, dr_ref, dv_ref = ((dw0, dw1), (dkt0, dkt1), (dakk0, dakk1), (dkk0, dkk1),
                                                              (dr0, dr1), (dv0, dv1))
        i = pl.program_id(0)
        wa = wa_ref[...]
        twa = _lora_in(wa)
        pre = [_dot(twa, m_ref[z]) for m_ref in (wup_ref, aup_ref) for z in range(2)]
        fn = functools.partial(_rwkv_pw, bd=bd_ref[...], diff=True)
        _, vjp = jax.vjp(fn, k_ref[...], pre[0], pre[1], pre[2], pre[3], w0_ref[...], a0_ref[...], kk_ref[...],
                         ka_ref[...])
        dkts = dkts_ref[...]
        dk, dpw0, dpw1, dpa0, dpa1, gw0, ga0, gkk, gka = vjp(
            (dw_ref[0][...], dw_ref[1][...], dkt_ref[0][...] + dkts, dkt_ref[1][...] + dkts, dakk_ref[0][...],
             dakk_ref[1][...], dkk_ref[0][...] + dkk_ref[1][...]))
        dtwa = (_dot_nt(dpw0, wup_ref[0]) + _dot_nt(dpw1, wup_ref[1]) + _dot_nt(dpa0, aup_ref[0])
                + _dot_nt(dpa1, aup_ref[1]))
        lane = lax.broadcasted_iota(jnp.int32, (1, LORA_W), 1)
        dsh_ref[:, 0:RWKV_W] = dr_ref[0][...] + dr_ref[1][...] + dr2_ref[...]
        dsh_ref[:, RWKV_W:2 * RWKV_W] = dk
        dsh_ref[:, 2 * RWKV_W:3 * RWKV_W] = dv_ref[0][...] + dv_ref[1][...] + dv2_ref[...]
        dsh_ref[:, 3 * RWKV_W:] = jnp.where(lane < LORA_W // 2, dtwa * (1.0 - twa * twa), dtwa)
        acc = ((gwup_ref.at[0], _dot_tn(twa, dpw0)), (gwup_ref.at[1], _dot_tn(twa, dpw1)),
               (gaup_ref.at[0], _dot_tn(twa, dpa0)), (gaup_ref.at[1], _dot_tn(twa, dpa1)),
               (gw0_ref, gw0), (ga0_ref, ga0), (gkk_ref, gkk), (gka_ref, gka))

        @pl.when(i == 0)
        def _():
            for ref, val in acc:
                ref[...] = val

        @pl.when(i > 0)
        def _():
            for ref, val in acc:
                ref[...] += val

    col = lambda c, w: pl.BlockSpec((TT, w), lambda i: (i, c))
    one = col(0, RWKV_W)
    sds = jax.ShapeDtypeStruct
    return pl.pallas_call(
        body, name="rwkv_prep_bwd", grid=(R // TT,),
        in_specs=[col(1, RWKV_W), col(3 * RWKV_W // LORA_W, LORA_W)] + [one] * 15 + [
                  _full(wup.shape), _full(aup.shape), _full((2, RWKV_W)), _full((2, RWKV_W)), _full((1, RWKV_W)),
                  _full((1, RWKV_W)), _full((256, 256))],
        out_specs=(pl.BlockSpec((TT, SHIFT_W), lambda i: (i, 0)), _full(wup.shape), _full(aup.shape),
                   _full((2, RWKV_W)), _full((2, RWKV_W)), _full((1, RWKV_W)), _full((1, RWKV_W))),
        out_shape=(sds((R, SHIFT_W), F32), sds(wup.shape, F32), sds(aup.shape, F32), sds((2, RWKV_W), F32),
                   sds((2, RWKV_W), F32), sds((1, RWKV_W), F32), sds((1, RWKV_W), F32)),
        compiler_params=_cp(("arbitrary",)),
    )(shifted, shifted, *cts, wup, aup, w0, a0, k_k, k_a, bd)


def _col_lhs(row, eye_b):
    return eye_b * row.astype(MXU_DTYPE)


def _colsum(x):
    return jnp.sum(x, axis=0, keepdims=True)


def _stacked_segsum(tiles, bd):
    res = _seg_dot(jnp.concatenate(tiles, axis=0), bd)
    return [res[j * HEAD_DIM:(j + 1) * HEAD_DIM] for j in range(len(tiles))]


def _scan_specs(B, T, C, nC):
    def blk(z, col, rev):
        idx = (lambda g: (z, 0, nC - 1 - g, col)) if rev else (lambda g: (z, 0, g, col))
        return pl.BlockSpec((None, B, C, RWKV_W), idx)

    def blk3(col, rev):
        idx = (lambda g: (0, nC - 1 - g, col)) if rev else (lambda g: (0, g, col))
        return pl.BlockSpec((B, C, RWKV_W), idx)

    return blk, blk3


def _scan_fwd_call(w, kt, akk, kk, shifted, eye_b, eye_f, bd, B, T):
    C = min(SCAN_CHUNK, T)
    nC = T // C
    blk, blk3 = _scan_specs(B, T, C, nC)

    def body(w0, kt0, akk0, kk0, v0, r0, w1, kt1, akk1, kk1, v1, r1, eb_ref, ef_ref, bd_ref, y0, y1, st, S):
        @pl.when(pl.program_id(0) == 0)
        def _():
            S[...] = jnp.zeros_like(S)

        st[0] = S[...].astype(MXU_DTYPE)
        dirs = ((w0, kt0, akk0, kk0, v0, r0, y0), (w1, kt1, akk1, kk1, v1, r1, y1))

        def step(s, carry):
            for z in range(2):
                row = s if z == 0 else C - 1 - s
                prev = jnp.maximum(s - 1, 0) if z == 0 else jnp.minimum(C - s, C - 1)
                wr, ktr, akkr, kkr, vr, rr, yr = dirs[z]
                tiles = []
                for b in range(B):
                    Sb = st[s, z * B + b]
                    tiles += [Sb * kkr[b, pl.ds(row, 1), :].astype(MXU_DTYPE),
                              _col_lhs(vr[b, pl.ds(row, 1), :], eb_ref[...]),
                              Sb * rr[b, pl.ds(prev, 1), :].astype(MXU_DTYPE)]
                res = _stacked_segsum(tiles, bd_ref[...])
                for b in range(B):
                    c = z * B + b
                    sab, vb, yb = res[3 * b:3 * b + 3]
                    ld = lambda ref: ref[b, pl.ds(row, 1), :]
                    Sn = S[c] * ld(wr) - sab * ld(akkr) + vb * ld(ktr)
                    S[c] = Sn
                    st[s + 1, c] = Sn.astype(MXU_DTYPE)
                    yr[b, pl.ds(prev, 1), :] = _colsum(ef_ref[...] * yb)
            return carry

        lax.fori_loop(0, C, step, 0, unroll=SCAN_UNROLL)
        for z in range(2):
            last = C - 1 if z == 0 else 0
            rr, yr = dirs[z][5], dirs[z][6]
            res = _stacked_segsum([st[C, z * B + b] * rr[b, last:last + 1, :].astype(MXU_DTYPE) for b in range(B)],
                                  bd_ref[...])
            for b in range(B):
                yr[b, last:last + 1, :] = _colsum(ef_ref[...] * res[b])

    ins, specs = [], []
    for z, rev in ((0, False), (1, True)):
        ins += [w, kt, akk, kk, shifted, shifted]
        specs += [blk(z, 0, rev), blk(z, 0, rev), blk(z, 0, rev), blk3(0, rev), blk3(2, rev), blk3(0, rev)]
    sds = jax.ShapeDtypeStruct
    return pl.pallas_call(
        body, name="scan_fwd", grid=(nC,),
        in_specs=specs + [_full((HEAD_DIM, RWKV_W)), _full((HEAD_DIM, RWKV_W)), _full((256, 256))],
        out_specs=(blk3(0, False), blk3(0, True),
                   pl.BlockSpec((None, C + 1, 2 * B, HEAD_DIM, RWKV_W), lambda g: (g, 0, 0, 0, 0))),
        out_shape=(sds((B, T, RWKV_W), F32), sds((B, T, RWKV_W), F32),
                   sds((nC, C + 1, 2 * B, HEAD_DIM, RWKV_W), MXU_DTYPE)),
        scratch_shapes=[pltpu.VMEM((2 * B, HEAD_DIM, RWKV_W), F32)],
        compiler_params=_cp(("arbitrary",)),
    )(*ins, eye_b, eye_f, bd)


def _scan_bwd_call(w, kt, akk, kk, shifted, v_heads, dys, st, eye_b, eye_f, bd, B, T):
    C = min(SCAN_CHUNK, T)
    nC = T // C
    blk, blk3 = _scan_specs(B, T, C, nC)
    nin = 7

    def body(*refs):
        d0, d1 = refs[:nin], refs[nin:2 * nin]
        st_ref, eb_ref, ef_ref, sel_ref, hm_ref, bd_ref = refs[2 * nin:2 * nin + 6]
        o0, o1 = refs[2 * nin + 6:2 * nin + 12], refs[2 * nin + 12:2 * nin + 18]
        COL, DYC, G = refs[2 * nin + 18:]

        @pl.when(pl.program_id(0) == 0)
        def _():
            G[...] = jnp.zeros_like(G)

        dirs = (d0 + (o0,), d1 + (o1,))

        def column_operands(s, z):
            row = s if z == 0 else C - 1 - s
            _, _, _, kkr, _, _, dyr, _ = dirs[z]
            tiles = []
            for b in range(B):
                tiles += [st_ref[s, z * B + b] * kkr[b, pl.ds(row, 1), :].astype(MXU_DTYPE),
                          _col_lhs(dyr[b, pl.ds(row, 1), :], eb_ref[...])]
            return tiles

        def keep_columns(res, z):
            for b in range(B):
                for k in range(2):
                    COL[k, z * B + b] = res[2 * b + k].astype(MXU_DTYPE)
                DYC[z * B + b] = res[2 * b + 1]

        for z in range(2):
            keep_columns(_stacked_segsum(column_operands(C - 1, z), bd_ref[...]), z)

        def bwd(it, carry):
            s = C - 1 - it
            for z in range(2):
                row = s if z == 0 else C - 1 - s
                wr, ktr, akkr, kkr, vr, rr, dyr, (dw_o, dkt_o, dakk_o, dkk_o, dr_o, dv_o) = dirs[z]
                tiles, Gcs = [], []
                for b in range(B):
                    c = z * B + b
                    Gc = G[c] + DYC[c] * rr[b, pl.ds(row, 1), :]
                    Gb = Gc.astype(MXU_DTYPE)
                    Gcs.append((Gc, Gb))
                    tiles += [Gb * akkr[b, pl.ds(row, 1), :].astype(MXU_DTYPE),
                              Gb * ktr[b, pl.ds(row, 1), :].astype(MXU_DTYPE)]
                res = _stacked_segsum(tiles + column_operands(jnp.maximum(s - 1, 0), z), bd_ref[...])
                for b in range(B):
                    c = z * B + b
                    Gc, Gb = Gcs[b]
                    gab, dvb = res[2 * b], res[2 * b + 1]
                    ld = lambda ref: ref[b, pl.ds(row, 1), :]
                    G[c] = Gc * ld(wr) - gab * ld(kkr)
                    Sb = st_ref[s, c]
                    prods = jnp.concatenate([Gb, st_ref[s + 1, c] * COL[1, c], Gb * Sb, Gb * COL[0, c],
                                             gab.astype(MXU_DTYPE) * Sb], axis=0)
                    v_rows = jnp.concatenate([vr[b, pl.ds(row, 1)][0], jnp.zeros((8, 3 * HEAD_DIM), F32)], axis=1)
                    lhs = jnp.concatenate([sel_ref[...], v_rows], axis=0).astype(MXU_DTYPE)
                    sums = jnp.dot(lhs, prods, preferred_element_type=F32)
                    for k, (ref, sign) in enumerate(((dr_o, 1.0), (dw_o, 1.0), (dakk_o, -1.0), (dkk_o, -1.0))):
                        ref[b, pl.ds(row, 1), :] = sign * sums[k:k + 1, :]
                    dkt_o[b, pl.ds(row, 1), :] = _colsum(sums[8:16] * hm_ref[...])
                    dv_o[b, pl.ds(row, 1), :] = _colsum(ef_ref[...] * dvb)
                keep_columns(res[2 * B:], z)
            return carry

        lax.fori_loop(0, C, bwd, 0, unroll=SCAN_UNROLL)

    ins, specs = [], []
    for z, rev in ((0, True), (1, False)):
        heads = pl.BlockSpec((B, C) + v_heads.shape[2:], (lambda g: (0, nC - 1 - g, 0, 0)) if rev else (lambda g: (0, g, 0, 0)))
        ins += [w, kt, akk, kk, v_heads, shifted, dys]
        specs += [blk(z, 0, rev), blk(z, 0, rev), blk(z, 0, rev), blk3(0, rev), heads, blk3(0, rev), blk3(0, rev)]
    sel = (jnp.arange(8)[:, None] + 1 == (jnp.arange(5 * HEAD_DIM) // HEAD_DIM)[None, :]).astype(F32)
    head_rows = (jnp.arange(RWKV_W // HEAD_DIM)[:, None] == (jnp.arange(RWKV_W) // HEAD_DIM)[None, :]).astype(F32)
    ins += [st, eye_b, eye_f, sel, head_rows, bd]
    specs += [pl.BlockSpec((None, C + 1, 2 * B, HEAD_DIM, RWKV_W), lambda g: (nC - 1 - g, 0, 0, 0, 0)),
              _full((HEAD_DIM, RWKV_W)), _full((HEAD_DIM, RWKV_W)), _full(sel.shape), _full(head_rows.shape),
              _full((256, 256))]
    sds = jax.ShapeDtypeStruct
    out_specs = tuple(blk3(0, True) for _ in range(6)) + tuple(blk3(0, False) for _ in range(6))
    res = pl.pallas_call(
        body, name="scan_bwd", grid=(nC,), in_specs=specs, out_specs=out_specs,
        out_shape=tuple(sds((B, T, RWKV_W), F32) for _ in range(12)),
        scratch_shapes=[pltpu.VMEM((2, 2 * B, HEAD_DIM, RWKV_W), MXU_DTYPE), pltpu.VMEM((2 * B, HEAD_DIM, RWKV_W), F32),
                        pltpu.VMEM((2 * B, HEAD_DIM, RWKV_W), F32)],
        compiler_params=_cp(("arbitrary",)),
    )(*ins)
    return list(res)


def _out_head_call(x2, tgt2, gate, y_att, g_att, y0, y1, shifted, kt, g_rw, w_out, g_post, gn_w, gn_b, r_k, bd, T):
    R = x2.shape[0]
    TT = min(ROW_TILE, T)
    tpe = T // TT

    def body(x_ref, t_ref, gate_ref, ya_ref, ga_ref, y0_ref, y1_ref, r_ref, v_ref, kt_ref, grw_ref, w_ref, gp_ref,
             gnw_ref, gnb_ref, rk_ref, bd_ref,
             loss_o, dy_o, dya_o, dga_o, dys_o, dr_o, dv_o, dkts_o, dgrw_o, dgate_o, gw_o, ggp_o, ggnw_o, ggnb_o, grk_o):
        i = pl.program_id(0)
        bd = bd_ref[...]
        mix = functools.partial(_mix_fn, bd=bd, diff=True)
        (ma, mr), mix_vjp = jax.vjp(mix, ya_ref[...], ga_ref[...], y0_ref[...] + y1_ref[...], r_ref[...], v_ref[...],
                                    kt_ref[0] + kt_ref[1], grw_ref[...], gnw_ref[...], gnb_ref[...], rk_ref[...])
        out = _dot(ma, w_ref[0:ATT_W, :]) + _dot(mr, w_ref[ATT_W:, :])
        loss, loss_vjp = jax.vjp(_loss_fn, out, x_ref[...], t_ref[...], gate_ref[0], gp_ref[...])
        d_out, dy, _, dgate, dgp = loss_vjp(jnp.ones((1, 1), F32))
        dy_o[...] = dy
        dma = _dot_nt(d_out, w_ref[0:ATT_W, :])
        dmr = _dot_nt(d_out, w_ref[ATT_W:, :])
        dya_o[...], dga_o[...], dys_o[...], dr_o[...], dv_o[...], dkts_o[...], dgrw_o[...], dgnw, dgnb, drk = \
            mix_vjp((dma, dmr))
        gw = jnp.concatenate([_dot_tn(ma, d_out), _dot_tn(mr, d_out)], axis=0)
        acc = ((loss_o, jnp.broadcast_to(loss, (8, 128))), (gw_o, gw), (ggp_o, dgp), (ggnw_o, dgnw), (ggnb_o, dgnb),
               (grk_o, drk))

        @pl.when(i == 0)
        def _():
            for ref, val in acc:
                ref[...] = val

        @pl.when(i > 0)
        def _():
            for ref, val in acc:
                ref[...] += val

        @pl.when(i % tpe == 0)
        def _():
            dgate_o[0] = dgate

        @pl.when(i % tpe > 0)
        def _():
            dgate_o[0] += dgate

    row = lambda w, c=0: pl.BlockSpec((TT, w), lambda i: (i, c))
    two = pl.BlockSpec((2, TT, RWKV_W), lambda i: (0, i, 0))
    per_ex = pl.BlockSpec((1, 1, D_MODEL), lambda i: (i // tpe, 0, 0))
    sds = jax.ShapeDtypeStruct
    r512 = sds((R, RWKV_W), F32)
    return pl.pallas_call(
        body, name="out_head", grid=(R // TT,),
        in_specs=[row(D_MODEL), row(D_MODEL), per_ex, row(ATT_W), row(ATT_W), row(RWKV_W), row(RWKV_W), row(RWKV_W, 0),
                  row(RWKV_W, 2), two,
                  row(RWKV_W), _full(w_out.shape), _full((1, D_MODEL)), _full((1, RWKV_W)), _full((1, RWKV_W)),
                  _full((1, RWKV_W)), _full((256, 256))],
        out_specs=(_full((8, 128)), row(D_MODEL), row(ATT_W), row(ATT_W), row(RWKV_W), row(RWKV_W), row(RWKV_W),
                   row(RWKV_W), row(RWKV_W), per_ex, _full((D_MODEL, D_MODEL)), _full((1, D_MODEL)), _full((1, RWKV_W)),
                   _full((1, RWKV_W)), _full((1, RWKV_W))),
        out_shape=(sds((8, 128), F32), sds((R, D_MODEL), F32), r512, r512, r512, r512, r512, r512, r512,
                   sds((R // T, 1, D_MODEL), F32), sds((D_MODEL, D_MODEL), F32), sds((1, D_MODEL), F32),
                   sds((1, RWKV_W), F32), sds((1, RWKV_W), F32), sds((1, RWKV_W), F32)),
        compiler_params=_cp(("arbitrary",)),
    )(x2, tgt2, gate, y_att, g_att, y0, y1, shifted, shifted, kt, g_rw, w_out, g_post, gn_w, gn_b, r_k, bd)


def _in_proj_bwd_call(x2, dy, shift, scale, g_pre, w_in, qg, kg, cos, sin, bd, q_raw, k_raw, dqr, dkp, dvp,
                      d_gatt, d_rin, d_grw, T):
    R = x2.shape[0]
    TT = min(ROW_TILE, T)
    tpe = T // TT

    def body(x_ref, dy_ref, sh_ref, sc_ref, gp_ref, w_ref, qg_ref, kg_ref, cos_ref, sin_ref, bd_ref, q_ref, k_ref,
             dqr_ref, dkp_ref, dvp_ref, dga_ref, drin_ref, dgrw_ref,
             dx_o, dproj_o, dsh_o, dsc_o, ggp_o, gqg_o, gkg_o):
        i = pl.program_id(0)
        cos, sin, bd = cos_ref[...], sin_ref[...], bd_ref[...]
        left = lax.broadcasted_iota(jnp.int32, (1, KV_W), 1) < HEAD_DIM

        def kv_grad(ref):
            a = ref[0] + ref[1]
            b = ref[2] + ref[3]
            return jnp.where(left, a + pltpu.roll(a, HEAD_DIM, 1), b + pltpu.roll(b, HEAD_DIM, 1))

        qfn = functools.partial(_qk_fn, cos=jnp.tile(cos, (1, 4)), sin=jnp.tile(sin, (1, 4)), bd=bd, scale=ATT_SCALE,
                                diff=True)
        _, q_vjp = jax.vjp(qfn, q_ref[...], qg_ref[...])
        dq, gqg = q_vjp(dqr_ref[...])
        kfn = functools.partial(_qk_fn, cos=cos, sin=sin, bd=bd, scale=1.0, diff=True)
        _, k_vjp = jax.vjp(kfn, k_ref[...], kg_ref[...])
        dk, gkg = k_vjp(kv_grad(dkp_ref))
        pieces = ((C_Q, C_K, dq), (C_K, C_V, dk), (C_V, C_GA, kv_grad(dvp_ref)), (C_GA, C_RIN, dga_ref[...]),
                  (C_RIN, C_GRW, drin_ref[...]), (C_GRW, C_END, dgrw_ref[...]))
        dh = jnp.zeros((TT, D_MODEL), F32)
        for c0, c1, val in pieces:
            vb = val.astype(MXU_DTYPE)
            dproj_o[:, c0:c1] = vb
            dh = dh + _dot(vb, w_ref[c0:c1, :])
        _, pre_vjp = jax.vjp(_pre_fn, x_ref[...], sh_ref[0], sc_ref[0], gp_ref[...])
        dx, dsh, dsc, ggp = pre_vjp(dh)
        dx_o[...] = dx + dy_ref[...]
        acc = ((ggp_o, ggp), (gqg_o, gqg), (gkg_o, gkg))

        @pl.when(i == 0)
        def _():
            for ref, val in acc:
                ref[...] = val

        @pl.when(i > 0)
        def _():
            for ref, val in acc:
                ref[...] += val

        @pl.when(i % tpe == 0)
        def _():
            dsh_o[0] = dsh
            dsc_o[0] = dsc

        @pl.when(i % tpe > 0)
        def _():
            dsh_o[0] += dsh
            dsc_o[0] += dsc

    row = lambda w: pl.BlockSpec((TT, w), lambda i: (i, 0))
    per_ex = pl.BlockSpec((1, 1, D_MODEL), lambda i: (i // tpe, 0, 0))
    tab = pl.BlockSpec((TT, KV_W), lambda i: (i % tpe, 0))
    pad = pl.BlockSpec((4, TT, KV_W), lambda i: (0, i, 0))
    sds = jax.ShapeDtypeStruct
    nb = R // T
    return pl.pallas_call(
        body, name="in_proj_bwd", grid=(R // TT,),
        in_specs=[row(D_MODEL), row(D_MODEL), per_ex, per_ex, _full((1, D_MODEL)), _full(w_in.shape), _full((1, ATT_W)),
                  _full((1, KV_W)), tab, tab, _full((256, 256)), row(ATT_W), row(KV_W), row(ATT_W), pad, pad,
                  row(ATT_W), row(SHIFT_W), row(RWKV_W)],
        out_specs=(row(D_MODEL), row(C_END), per_ex, per_ex, _full((1, D_MODEL)), _full((1, ATT_W)), _full((1, KV_W))),
        out_shape=(sds((R, D_MODEL), F32), sds((R, C_END), MXU_DTYPE), sds((nb, 1, D_MODEL), F32),
                   sds((nb, 1, D_MODEL), F32), sds((1, D_MODEL), F32), sds((1, ATT_W), F32), sds((1, KV_W), F32)),
        compiler_params=_cp(("arbitrary",)),
    )(x2, dy, shift, scale, g_pre, w_in, qg, kg, cos, sin, bd, q_raw, k_raw, dqr, dkp, dvp, d_gatt, d_rin, d_grw)


def _w_in_grad_call(hb, dproj):
    R = hb.shape[0]
    TT = min(W_GRAD_ROWS, R)
    CB = 1152
    last = R // TT - 1

    def body(h_ref, d_ref, o_ref, acc):
        g = _dot_tn(h_ref[...], d_ref[...])

        @pl.when(pl.program_id(1) == 0)
        def _():
            acc[...] = g

        @pl.when(pl.program_id(1) > 0)
        def _():
            acc[...] += g

        @pl.when(pl.program_id(1) == last)
        def _():
            o_ref[...] = acc[...].astype(o_ref.dtype)

    return pl.pallas_call(
        body, name="w_in_grad", grid=(C_END // CB, R // TT),
        in_specs=[pl.BlockSpec((TT, D_MODEL), lambda j, i: (i, 0)), pl.BlockSpec((TT, CB), lambda j, i: (i, j))],
        out_specs=pl.BlockSpec((D_MODEL, CB), lambda j, i: (0, j)),
        out_shape=jax.ShapeDtypeStruct((D_MODEL, C_END), MXU_DTYPE),
        scratch_shapes=[pltpu.VMEM((D_MODEL, CB), F32)], compiler_params=_cp(("arbitrary", "arbitrary")),
    )(hb, dproj)


def _adam_refs(p_ref, w_ref, m_ref, v_ref, g_o, d_o, m_o, v_o):
    g = p_ref[0].astype(F32)
    for j in range(1, p_ref.shape[0]):
        g = g + p_ref[j].astype(F32)
    m2 = ADAM_B1 * m_ref[...] + (1.0 - ADAM_B1) * g
    v2 = ADAM_B2 * v_ref[...] + (1.0 - ADAM_B2) * jnp.square(g)
    m_hat = m2 / (1.0 - ADAM_B1 ** ADAM_STEP)
    v_hat = v2 / (1.0 - ADAM_B2 ** ADAM_STEP)
    g_o[...] = g
    d_o[...] = -ADAM_LR * (m_hat / (jnp.sqrt(v_hat) + ADAM_EPS) + ADAM_WD * w_ref[...])
    m_o[...] = m2
    v_o[...] = v2


def _adam_small_call(items, name):
    n = len(items)

    def body(*refs):
        for k in range(n):
            _adam_refs(*refs[4 * k:4 * k + 4], *refs[4 * n + 4 * k:4 * n + 4 * k + 4])

    out_shape = tuple(jax.ShapeDtypeStruct(w.shape, F32) for _, w, _, _ in items for _ in range(4))
    out = pl.pallas_call(body, name=name, out_shape=out_shape)(*[a for item in items for a in item])
    return [out[4 * k:4 * k + 4] for k in range(n)]


def _adam_call(parts, w, m, v, name, row_tile=None):
    P, M, N = parts.shape
    TM = M if row_tile is None else row_tile

    def body(*refs):
        _adam_refs(*refs)

    blk = pl.BlockSpec((TM, N), lambda i: (i, 0))
    return pl.pallas_call(
        body, name=name, grid=(M // TM,),
        in_specs=[pl.BlockSpec((P, TM, N), lambda i: (0, i, 0)), blk, blk, blk], out_specs=(blk,) * 4,
        out_shape=(jax.ShapeDtypeStruct((M, N), F32),) * 4, compiler_params=_cp(("arbitrary",)),
    )(parts, w, m, v)


_SMALL_ROWS = 136


def _pack_small(taps, w_up, w0, a_up, a0):
    flat = jnp.concatenate([taps.reshape(-1), w_up.reshape(-1), w0.reshape(-1), a_up.reshape(-1), a0.reshape(-1)])
    return jnp.pad(flat, (0, _SMALL_ROWS * 128 - flat.shape[0])).reshape(_SMALL_ROWS, 128)


def _unpack_small(packed):
    n = packed.shape[0]
    flat = packed.reshape(n, -1)
    out, o = [], 0
    for shape in ((3, 208), (2, 64, 64), (2, 64), (2, 64, 64), (2, 64)):
        size = 1
        for s in shape:
            size *= s
        out.append(flat[:, o:o + size].reshape((n,) + shape))
        o += size
    return out


def _cols_to_full(blocks):
    nd = blocks.ndim
    moved = jnp.moveaxis(blocks, 0, nd - 2)
    return moved.reshape(moved.shape[:-2] + (moved.shape[-2] * moved.shape[-1],))


def _full_to_cols(full):
    k = full.shape[-1] // NDEV
    return jnp.moveaxis(full.reshape(full.shape[:-1] + (NDEV, k)), -2, 0)


_REP_SIZES = (("g_pre", 1024), ("q_norm_g", 64), ("k_norm_g", 64), ("k_k", 512), ("k_a", 512), ("r_k", 512),
              ("gn_w", 512), ("gn_b", 512), ("g_post", 1024))
_REP_ROWS = 40


def kernel(x, c, w_ada, b_ada, g_pre, w_in, q_norm_g, k_norm_g, shift_taps, w_up, w0, a_up, a0, k_k, k_a, r_k, gn_w, gn_b, w_out, g_post, loss_target, m_w_ada, m_b_ada, m_g_pre, m_w_in, m_q_norm_g, m_k_norm_g, m_shift_taps, m_w_up, m_w0, m_a_up, m_a0, m_k_k, m_k_a, m_r_k, m_gn_w, m_gn_b, m_w_out, m_g_post, v_w_ada, v_b_ada, v_g_pre, v_w_in, v_q_norm_g, v_k_norm_g, v_shift_taps, v_w_up, v_w0, v_a_up, v_a0, v_k_k, v_k_a, v_r_k, v_gn_w, v_gn_b, v_w_out, v_g_post):
    B, T, _ = x.shape
    R = B * T
    me = 4 * lax.axis_index("x") + 2 * lax.axis_index("y") + lax.axis_index("c")
    x2 = x.reshape(R, D_MODEL)
    tgt2 = loss_target.reshape(R, D_MODEL)

    seg = jnp.arange(256) // HEAD_DIM
    bd = (seg[:, None] == seg[None, :]).astype(MXU_DTYPE)
    eye = (jnp.arange(HEAD_DIM)[:, None] == (jnp.arange(RWKV_W) % HEAD_DIM)[None, :])
    eye_b, eye_f = eye.astype(MXU_DTYPE), eye.astype(F32)
    cos, sin = _rope_tables(T)

    c_g, w_in_g, w_out_g, small_g = _exchange(
        [c, w_in[0].T.astype(MXU_DTYPE), w_out[0].astype(MXU_DTYPE),
         _pack_small(shift_taps[0], w_up[0], w0[0], a_up[0], a0[0])], ["all"] * 4, "gather_params")
    c_all = c_g.reshape(NDEV * B, D_MODEL)
    w_in_f = w_in_g.reshape(C_END, D_MODEL)
    w_out_f = w_out_g.reshape(D_MODEL, D_MODEL)
    taps_b, w_up_b, w0_b, a_up_b, a0_b = _unpack_small(small_g)
    taps_f = jnp.pad(_cols_to_full(taps_b), ((0, 5), (0, 0)))
    w_up_f, a_up_f = _cols_to_full(w_up_b), _cols_to_full(a_up_b)
    w0_f, a0_f = _cols_to_full(w0_b), _cols_to_full(a0_b)
    wup_pad = jnp.pad(w_up_f, ((0, 0), (0, 64), (0, 0))).astype(MXU_DTYPE)
    aup_pad = jnp.pad(a_up_f, ((0, 0), (64, 0), (0, 0))).astype(MXU_DTYPE)

    ncol = w_ada.shape[2]
    b_cols = lax.dynamic_slice(b_ada, (0, me * ncol), (1, ncol))
    mod_cols = _mod_call(c_all, w_ada[0].astype(MXU_DTYPE), b_cols)
    (mod_g,) = _exchange([mod_cols], ["all"], "gather_mod")
    mod = lax.dynamic_slice(_cols_to_full(mod_g), (me * B, 0), (B, 3 * D_MODEL))
    shift, scale, gate = [mod[:, j * D_MODEL:(j + 1) * D_MODEL].reshape(B, 1, D_MODEL) for j in range(3)]

    qg = jnp.tile(q_norm_g, (1, ATT_W // HEAD_DIM))
    kg = jnp.tile(k_norm_g, (1, KV_W // HEAD_DIM))
    rk_row = r_k.reshape(1, RWKV_W)

    hb, qr, kpad, vpad, q_raw, k_raw, g_att, rin, g_rw = _in_proj_call(
        x2, shift, scale, g_pre, w_in_f, qg, kg, cos, sin, bd, T)
    y_att = _att_fwd_call(qr, kpad, vpad, B, T)
    shifted, v_rows = _shift_fwd_call(rin, taps_f, T)
    w_s, kt_s, akk_s, kk_s = _rwkv_prep_call(shifted, wup_pad, aup_pad, w0_f, a0_f, k_k, k_a, bd, T)
    sh3 = shifted.reshape(B, T, SHIFT_W)
    r4 = lambda a: a.reshape(2, B, T, RWKV_W)
    y0, y1, st = _scan_fwd_call(r4(w_s), r4(kt_s), r4(akk_s), kk_s.reshape(B, T, RWKV_W), sh3, eye_b, eye_f, bd, B, T)

    (loss_blk, dy, d_yatt, d_gatt, d_ys, d_r2, d_v2, d_kts, d_grw, d_gate, g_wout, g_gpost, g_gnw, g_gnb,
     g_rk) = _out_head_call(x2, tgt2, gate, y_att, g_att, y0.reshape(R, RWKV_W), y1.reshape(R, RWKV_W), shifted, kt_s,
                            g_rw, w_out_f, g_post, gn_w, gn_b, rk_row, bd, T)
    v_heads = v_rows.reshape(B, T, RWKV_W // HEAD_DIM, 2 * HEAD_DIM)
    scan_cts = _scan_bwd_call(r4(w_s), r4(kt_s), r4(akk_s), kk_s.reshape(B, T, RWKV_W), sh3, v_heads,
                              d_ys.reshape(B, T, RWKV_W), st, eye_b, eye_f, bd, B, T)
    scan_cts = [a.reshape(R, RWKV_W) for a in scan_cts]
    d_shifted, g_wup, g_aup, g_w0, g_a0, g_kk, g_ka = _rwkv_prep_bwd_call(
        shifted, scan_cts + [d_r2, d_v2, d_kts], wup_pad, aup_pad, w0_f, a0_f, k_k, k_a, bd, T)
    d_rin, g_taps = _shift_bwd_call(rin, d_shifted, taps_f, T)
    dqr, dkp, dvp = _att_bwd_call(qr, kpad, vpad, d_yatt, B, T)
    grad_x, dproj, d_shift, d_scale, g_gpre, g_qg, g_kg = _in_proj_bwd_call(
        x2, dy, shift, scale, g_pre, w_in_f, qg, kg, cos, sin, bd, q_raw, k_raw, dqr, dkp, dvp, d_gatt, d_rin, d_grw, T)
    g_win = _w_in_grad_call(hb, dproj)

    rep = jnp.concatenate([g_gpre.reshape(-1), g_qg.reshape(-1, HEAD_DIM).sum(0), g_kg.reshape(-1, HEAD_DIM).sum(0),
                           g_kk.reshape(-1), g_ka.reshape(-1), g_rk.reshape(-1), g_gnw.reshape(-1), g_gnb.reshape(-1),
                           g_gpost.reshape(-1), loss_blk[0, :1]])
    rep = jnp.pad(rep, (0, _REP_ROWS * 128 - rep.shape[0])).reshape(_REP_ROWS, 128)
    dmod = jnp.concatenate([d_shift, d_scale, d_gate], axis=2).reshape(B, 3 * D_MODEL)
    small_parts = jax.vmap(_pack_small)(_full_to_cols(g_taps[:3]), _full_to_cols(g_wup[:, :64, :]), _full_to_cols(g_w0),
                                        _full_to_cols(g_aup[:, 64:, :]), _full_to_cols(g_a0))
    by_core = lambda a: jnp.swapaxes(a.reshape((NDEV // 2, 2) + a.shape[1:]), 0, 1).astype(MXU_DTYPE)
    s_win, s_wout = _pair_sum_call(
        [by_core(_full_to_cols(g_win)), by_core(g_wout.reshape(NDEV, D_MODEL // NDEV, D_MODEL))], "reduce_pair")
    p_win, p_wout, p_small, dmod_g, rep_g = _exchange(
        [s_win, s_wout, small_parts, dmod, rep], ["chips", "chips", "scatter", "all", "all"], "reduce_grads")
    dmod_all = dmod_g.reshape(NDEV * B, 3 * D_MODEL)
    g_wada = _wada_grad_call(c_all, lax.dynamic_slice(dmod_all, (0, me * ncol), (NDEV * B, ncol)))

    res, small = {}, []

    def adam(name, parts, w, m, v, row_tile=None, alone=False):
        two_d = (-1, w.shape[-1])
        item = (parts.reshape((parts.shape[0],) + w.reshape(two_d).shape), w.reshape(two_d), m.reshape(two_d),
                v.reshape(two_d))
        if alone:
            res[name] = [o.reshape(w.shape) for o in _adam_call(*item, "adam_" + name, row_tile)]
        else:
            small.append((name, w.shape, item))

    adam("w_ada", g_wada[None], w_ada, m_w_ada, v_w_ada, alone=True)
    adam("b_ada", dmod_all.reshape(NDEV * B, 1, 3 * D_MODEL), b_ada, m_b_ada, v_b_ada)
    adam("w_in", p_win, w_in, m_w_in, v_w_in, 128, alone=True)
    adam("w_out", p_wout, w_out, m_w_out, v_w_out, alone=True)
    taps_p, wup_p, w0_p, aup_p, a0_p = _unpack_small(p_small)
    adam("shift_taps", taps_p, shift_taps, m_shift_taps, v_shift_taps)
    adam("w_up", wup_p, w_up, m_w_up, v_w_up)
    adam("w0", w0_p, w0, m_w0, v_w0)
    adam("a_up", aup_p, a_up, m_a_up, v_a_up)
    adam("a0", a0_p, a0, m_a0, v_a0)
    rep_flat = rep_g.reshape(NDEV, -1)
    off = 0
    given = dict(g_pre=(g_pre, m_g_pre, v_g_pre), q_norm_g=(q_norm_g, m_q_norm_g, v_q_norm_g),
                 k_norm_g=(k_norm_g, m_k_norm_g, v_k_norm_g), k_k=(k_k, m_k_k, v_k_k), k_a=(k_a, m_k_a, v_k_a),
                 r_k=(r_k, m_r_k, v_r_k), gn_w=(gn_w, m_gn_w, v_gn_w), gn_b=(gn_b, m_gn_b, v_gn_b),
                 g_post=(g_post, m_g_post, v_g_post))
    for name, size in _REP_SIZES:
        adam(name, rep_flat[:, off:off + size], *given[name])
        off += size
    for (name, shape, _), out in zip(small, _adam_small_call([item for _, _, item in small], "adam_small")):
        res[name] = [o.reshape(shape) for o in out]

    loss = jnp.sum(rep_flat[:, off])
    order = ["w_ada", "b_ada", "g_pre", "w_in", "q_norm_g", "k_norm_g", "shift_taps", "w_up", "w0", "a_up", "a0", "k_k",
             "k_a", "r_k", "gn_w", "gn_b", "w_out", "g_post"]
    return (loss, grad_x.reshape(B, T, D_MODEL), *[res[n][0] for n in order], *[res[n][1] for n in order],
            *[res[n][2] for n in order], *[res[n][3] for n in order])
```

```python
import functools

import jax
import jax.numpy as jnp
from jax import lax
from jax.experimental import pallas as pl
from jax.experimental.pallas import tpu as pltpu

F32 = jnp.float32
MXU_DTYPE = jnp.bfloat16
MESH = pl.DeviceIdType.MESH
NDEV = 8

D_MODEL = 1024
HEAD_DIM = 64
ATT_W = 512
KV_W = 128
RWKV_W = 512
LORA_W = 128
SHIFT_W = 3 * RWKV_W + LORA_W
GRID_W = 64
ROPE_THETA = 10000.0
DECAY_SCALE = 0.6065306597126334
NORM_EPS = 1e-6
GN_EPS = 64e-5
L2_EPS = 1e-12
ATT_SCALE = HEAD_DIM ** -0.5
C_Q, C_K, C_V, C_GA, C_RIN, C_GRW, C_END = 0, 512, 640, 768, 1280, 2944, 3456

ADAM_LR, ADAM_B1, ADAM_B2, ADAM_EPS, ADAM_WD, ADAM_STEP = 0.001, 0.9, 0.999, 1e-08, 0.01, 10

ROW_TILE = 256
SHIFT_TILE = 512
W_GRAD_ROWS = 2048
ATT_TILE_FWD = 256
ATT_TILE_BWD = 1024
SCAN_CHUNK = 64
SCAN_UNROLL = 16
VMEM_LIMIT = 56 * 1024 * 1024


def _cp(sem=None):
    return pltpu.CompilerParams(dimension_semantics=sem, vmem_limit_bytes=VMEM_LIMIT)


def _dot(a, b, dims=(((1,), (0,)), ((), ()))):
    return lax.dot_general(a.astype(MXU_DTYPE), b.astype(MXU_DTYPE), dims, preferred_element_type=F32)


def _dot_nt(a, b):
    return _dot(a, b, (((1,), (1,)), ((), ())))


def _dot_tn(a, b):
    return _dot(a, b, (((0,), (0,)), ((), ())))


def _seg_dot(xb, bd):
    n = xb.shape[1]
    if n <= 256:
        return jnp.dot(xb, bd[:n, :n], preferred_element_type=F32)
    parts = [jnp.dot(xb[:, c:c + 256], bd, preferred_element_type=F32) for c in range(0, n, 256)]
    return jnp.concatenate(parts, axis=1)


def _segsum_raw(x, bd):
    rows = x.shape[0]
    hi = x.astype(MXU_DTYPE)
    lo = (x - hi.astype(F32)).astype(MXU_DTYPE)
    both = _seg_dot(jnp.concatenate([hi, lo], axis=0), bd)
    return both[:rows] + both[rows:]


@jax.custom_vjp
def _segsum_d(x, bd):
    return _segsum_raw(x, bd)


def _segsum_d_fwd(x, bd):
    return _segsum_raw(x, bd), bd


def _segsum_d_bwd(bd, ct):
    return _segsum_raw(ct, bd), jnp.zeros_like(bd)


_segsum_d.defvjp(_segsum_d_fwd, _segsum_d_bwd)


def _rope_tables(T):
    t = jnp.arange(T, dtype=F32)
    row = jnp.floor(t / GRID_W)
    col = t - row * GRID_W
    n_freq = HEAD_DIM // 4
    inv_freq = ROPE_THETA ** (-jnp.arange(n_freq, dtype=F32) / n_freq)
    d = jnp.arange(HEAD_DIM)
    pos = jnp.where((d < HEAD_DIM // 2)[None, :], row[:, None], col[:, None])
    ang = pos * inv_freq[d % n_freq][None, :]
    sign = jnp.where((d % 32) < 16, -1.0, 1.0).astype(F32)[None, :]
    cos = jnp.cos(ang)
    sin = jnp.sin(ang) * sign
    return jnp.tile(cos, (1, 2)), jnp.tile(sin, (1, 2))


def _rope_raw(x, cos, sin):
    n = x.shape[1]
    lane = lax.broadcasted_iota(jnp.int32, (1, n), 1)
    first = (lane % 32) < 16
    partner = jnp.where(first, pltpu.roll(x, n - 16, 1), pltpu.roll(x, 16, 1))
    return x * cos + partner * sin


@jax.custom_vjp
def _rope_d(x, cos, sin):
    return _rope_raw(x, cos, sin)


def _rope_d_fwd(x, cos, sin):
    return _rope_raw(x, cos, sin), (cos, sin)


def _rope_d_bwd(res, ct):
    cos, sin = res
    return _rope_raw(ct, cos, -sin), jnp.zeros_like(cos), jnp.zeros_like(sin)


_rope_d.defvjp(_rope_d_fwd, _rope_d_bwd)


def _rms(x, g):
    return x * lax.rsqrt(jnp.mean(x * x, axis=-1, keepdims=True) + NORM_EPS) * g


def _pre_fn(x, shift, scale, g_pre):
    return _rms(x, g_pre) * (1.0 + scale) + shift


def _qk_fn(q, g, cos, sin, bd, scale, diff):
    segsum = _segsum_d if diff else _segsum_raw
    rope = _rope_d if diff else _rope_raw
    qn = q * lax.rsqrt(segsum(q * q, bd) * (1.0 / HEAD_DIM) + NORM_EPS) * g
    return rope(qn, cos, sin) * scale


def _silu(x):
    return x * jax.nn.sigmoid(x)


def _rwkv_pw(k, pw0, pw1, pa0, pa1, w0, a0, k_k, k_a, bd, diff):
    segsum = _segsum_d if diff else _segsum_raw
    kk = k * k_k
    kk = kk * lax.rsqrt(segsum(kk * kk, bd) + L2_EPS)
    ws, kts, akks = [], [], []
    for z, (pw, pa) in enumerate(((pw0, pa0), (pw1, pa1))):
        w = jnp.exp(-DECAY_SCALE * jax.nn.sigmoid(w0[z:z + 1, :] + pw))
        a = jax.nn.sigmoid(a0[z:z + 1, :] + pa)
        ws.append(w)
        kts.append(k * (1.0 + (a - 1.0) * k_a))
        akks.append(a * kk)
    return ws[0], ws[1], kts[0], kts[1], akks[0], akks[1], kk


def _mix_fn(y_att, g_att, ys, r, v, kts, g_rw, gn_w, gn_b, r_k, bd, diff):
    segsum = _segsum_d if diff else _segsum_raw
    mu = segsum(ys, bd) * (1.0 / HEAD_DIM)
    d = ys - mu
    var = segsum(d * d, bd) * (1.0 / HEAD_DIM)
    yn = d * lax.rsqrt(var + GN_EPS) * gn_w + gn_b
    bonus = segsum(r * kts * r_k, bd) * v
    return y_att * _silu(g_att), (yn + bonus) * _silu(g_rw)


def _loss_fn(out, x, tgt, gate, g_post):
    e = x + gate * _rms(out, g_post) - tgt
    s = jnp.sum(e * e, axis=1, keepdims=True)
    return jnp.sum(s, axis=0, keepdims=True) * (0.5 / D_MODEL)


def _exchange(arrays, modes, name):
    n = len(arrays)
    out_shape = tuple(
        jax.ShapeDtypeStruct(((NDEV,) + tuple(a.shape)) if mode == "all" else tuple(a.shape), a.dtype)
        for a, mode in zip(arrays, modes))
    chips = (4, 2, 6)

    def body(*refs):
        ins, outs = refs[:n], refs[n:2 * n]
        send_sems, recv_sems, local_sems = refs[2 * n:]
        ix, iy, ic = lax.axis_index("x"), lax.axis_index("y"), lax.axis_index("c")
        me = 4 * ix + 2 * iy + ic

        def peer(m):
            px = 1 - ix if (m >> 2) & 1 else ix
            py = 1 - iy if (m >> 1) & 1 else iy
            pc = 1 - ic if m & 1 else ic
            return (px, py, pc), 4 * px + 2 * py + pc

        def copy(k, j, src_ref, slot, to):
            return pltpu.make_async_remote_copy(src_ref=src_ref, dst_ref=outs[k].at[slot], send_sem=send_sems.at[k, j],
                                                recv_sem=recv_sems.at[k, j], device_id=to, device_id_type=MESH)

        local, sends, arrivals, forwards = [], [], [], []
        for k in range(n):
            if modes[k] == "scatter":
                local.append(pltpu.make_async_copy(ins[k].at[me], outs[k].at[me], local_sems.at[k]))
                for m in range(1, NDEV):
                    to, p = peer(m)
                    sends.append(copy(k, m - 1, ins[k].at[p], me, to))
                    arrivals.append(copy(k, m - 1, ins[k].at[p], p, to))
            elif modes[k] == "chips":
                mine = me // 2
                local.append(pltpu.make_async_copy(ins[k].at[mine], outs[k].at[mine], local_sems.at[k]))
                for j, m in enumerate(chips):
                    to, p = peer(m)
                    sends.append(copy(k, j, ins[k].at[p // 2], mine, to))
                    arrivals.append(copy(k, j, ins[k].at[p // 2], p // 2, to))
            else:
                local.append(pltpu.make_async_copy(ins[k], outs[k].at[me], local_sems.at[k]))
                sib, sib_slot = peer(1)
                sends.append(copy(k, 0, ins[k], me, sib))
                for j, m in enumerate(chips):
                    to, p = peer(m)
                    sends.append(copy(k, 1 + j, ins[k], me, to))
                    forwards.append((copy(k, 1 + j, ins[k], p, to), copy(k, 4 + j, outs[k].at[p], p, sib)))
                    arrivals.append(copy(k, 4 + j, ins[k], peer(m ^ 1)[1], sib))
                arrivals.append(copy(k, 0, ins[k], sib_slot, sib))
        for cp in local + sends:
            cp.start()
        for arrived, onward in forwards:
            arrived.wait_recv()
            onward.start()
        for cp in arrivals:
            cp.wait_recv()
        for cp in sends + [onward for _, onward in forwards]:
            cp.wait_send()
        for cp in local:
            cp.wait()

    any_spec = pl.BlockSpec(memory_space=pl.ANY)
    return pl.pallas_call(
        body, name=name, out_shape=out_shape,
        in_specs=[any_spec] * n, out_specs=tuple([any_spec] * n),
        scratch_shapes=[pltpu.SemaphoreType.DMA((n, NDEV - 1)), pltpu.SemaphoreType.DMA((n, NDEV - 1)),
                        pltpu.SemaphoreType.DMA((n,))],
    )(*arrays)


def _pair_sum_call(parts, name):
    n = len(parts)

    def body(*refs):
        in_r, out_r, mine_r, land_r = (refs[j * n:(j + 1) * n] for j in range(4))
        send_sems, recv_sems, local_sems = refs[4 * n:]
        core = lax.axis_index("c")
        sibling = (lax.axis_index("x"), lax.axis_index("y"), 1 - core)
        local = [pltpu.make_async_copy(in_r[k].at[core], mine_r[k], local_sems.at[k]) for k in range(n)]
        swaps = [pltpu.make_async_remote_copy(src_ref=in_r[k].at[1 - core], dst_ref=land_r[k], send_sem=send_sems.at[k],
                                              recv_sem=recv_sems.at[k], device_id=sibling, device_id_type=MESH)
                 for k in range(n)]
        for cp in local + swaps:
            cp.start()
        for k in range(n):
            local[k].wait()
            swaps[k].wait()
            out_r[k][...] = (mine_r[k][...].astype(F32) + land_r[k][...].astype(F32)).astype(out_r[k].dtype)

    halves = [jax.ShapeDtypeStruct(a.shape[1:], a.dtype) for a in parts]
    return pl.pallas_call(
        body, name=name, out_shape=tuple(halves), in_specs=[pl.BlockSpec(memory_space=pl.ANY)] * n,
        scratch_shapes=[pltpu.VMEM(h.shape, h.dtype) for h in halves] * 2 + [pltpu.SemaphoreType.DMA((n,))] * 3,
        compiler_params=pltpu.CompilerParams(vmem_limit_bytes=VMEM_LIMIT),
    )(*parts)


def _mod_call(c_all, w_ada, b_cols):
    def body(c_ref, w_ref, b_ref, o_ref):
        o_ref[...] = _dot(_silu(c_ref[...]), w_ref[...]) + b_ref[...]

    return pl.pallas_call(body, name="mod_fwd",
                          out_shape=jax.ShapeDtypeStruct((c_all.shape[0], w_ada.shape[1]), F32))(c_all, w_ada, b_cols)


def _wada_grad_call(c_all, dmod_cols):
    def body(c_ref, d_ref, o_ref):
        o_ref[...] = _dot_tn(_silu(c_ref[...]), d_ref[...])

    return pl.pallas_call(body, name="w_ada_grad",
                          out_shape=jax.ShapeDtypeStruct((c_all.shape[1], dmod_cols.shape[1]), F32))(c_all, dmod_cols)


def _full(shape):
    nd = len(shape)
    return pl.BlockSpec(shape, lambda *_: (0,) * nd)


def _in_proj_call(x2, shift, scale, g_pre, w_in, qg, kg, cos, sin, bd, T):
    R = x2.shape[0]
    TT = min(ROW_TILE, T)
    tpe = T // TT

    def body(x_ref, sh_ref, sc_ref, gp_ref, w_ref, qg_ref, kg_ref, cos_ref, sin_ref, bd_ref,
             hb_ref, qr_ref, kpad_ref, vpad_ref, qraw_ref, kraw_ref, gatt_ref, rin_ref, grw_ref):
        h = _pre_fn(x_ref[...], sh_ref[0], sc_ref[0], gp_ref[...])
        hb = h.astype(MXU_DTYPE)
        hb_ref[...] = hb

        def proj(c0, c1):
            return _dot_nt(hb, w_ref[c0:c1, :])

        q = proj(C_Q, C_K)
        k = proj(C_K, C_V)
        v = proj(C_V, C_GA)
        gatt_ref[...] = proj(C_GA, C_RIN)
        rin_ref[...] = proj(C_RIN, C_GRW)
        grw_ref[...] = proj(C_GRW, C_END)
        qraw_ref[...] = q
        kraw_ref[...] = k
        cos, sin, bd = cos_ref[...], sin_ref[...], bd_ref[...]
        qr = _qk_fn(q, qg_ref[...], jnp.tile(cos, (1, 4)), jnp.tile(sin, (1, 4)), bd, ATT_SCALE, False)
        qr_ref[...] = qr.astype(MXU_DTYPE)
        kr = _qk_fn(k, kg_ref[...], cos, sin, bd, 1.0, False)
        left = lax.broadcasted_iota(jnp.int32, (1, KV_W), 1) < HEAD_DIM
        for ref, val in ((kpad_ref, kr), (vpad_ref, v)):
            h0l = jnp.where(left, val, 0.0)
            h1r = jnp.where(left, 0.0, val)
            ref[0] = h0l.astype(MXU_DTYPE)
            ref[1] = pltpu.roll(h0l, HEAD_DIM, 1).astype(MXU_DTYPE)
            ref[2] = pltpu.roll(h1r, HEAD_DIM, 1).astype(MXU_DTYPE)
            ref[3] = h1r.astype(MXU_DTYPE)

    row = lambda w: pl.BlockSpec((TT, w), lambda i: (i, 0))
    per_ex = pl.BlockSpec((1, 1, D_MODEL), lambda i: (i // tpe, 0, 0))
    tab = pl.BlockSpec((TT, KV_W), lambda i: (i % tpe, 0))
    pad = pl.BlockSpec((4, TT, KV_W), lambda i: (0, i, 0))
    sds = jax.ShapeDtypeStruct
    return pl.pallas_call(
        body, name="in_proj", grid=(R // TT,),
        in_specs=[row(D_MODEL), per_ex, per_ex, _full((1, D_MODEL)), _full(w_in.shape), _full((1, ATT_W)),
                  _full((1, KV_W)), tab, tab, _full((256, 256))],
        out_specs=(row(D_MODEL), row(ATT_W), pad, pad, row(ATT_W), row(KV_W), row(ATT_W), row(SHIFT_W), row(RWKV_W)),
        out_shape=(sds((R, D_MODEL), MXU_DTYPE), sds((R, ATT_W), MXU_DTYPE), sds((4, R, KV_W), MXU_DTYPE),
                   sds((4, R, KV_W), MXU_DTYPE), sds((R, ATT_W), F32), sds((R, KV_W), F32), sds((R, ATT_W), F32),
                   sds((R, SHIFT_W), F32), sds((R, RWKV_W), F32)),
        compiler_params=_cp(("arbitrary",)),
    )(x2, shift, scale, g_pre, w_in, qg, kg, cos, sin, bd)


def _softmax_parts(s):
    e = jnp.exp(s - jnp.max(s, axis=1, keepdims=True))
    return e, 1.0 / jnp.sum(e, axis=1, keepdims=True)


def _att_specs(T, TQ):
    nq = T // TQ
    qspec = pl.BlockSpec((TQ, KV_W), lambda b, p, i: (b * nq + i, p))
    side = lambda s: pl.BlockSpec((None, T, KV_W), lambda b, p, i: (2 * (p // 2) + s, b, 0))
    return nq, qspec, side


def _att_fwd_call(qr, kpad, vpad, B, T):
    TQ = min(ATT_TILE_FWD, T)
    nq, qspec, side = _att_specs(T, TQ)

    def body(q_ref, kl_ref, kr_ref, vl_ref, vr_ref, o_ref):
        q = q_ref[...]
        ea, inv_a = _softmax_parts(_dot_nt(q, kl_ref[...]))
        eb, inv_b = _softmax_parts(_dot_nt(q, kr_ref[...]))
        o_ref[...] = _dot(ea, vl_ref[...]) * inv_a + _dot(eb, vr_ref[...]) * inv_b

    return pl.pallas_call(
        body, name="att_fwd", grid=(B, 4, nq),
        in_specs=[qspec, side(0), side(1), side(0), side(1)], out_specs=qspec,
        out_shape=jax.ShapeDtypeStruct((B * T, ATT_W), F32),
        compiler_params=_cp(("arbitrary",) * 3),
    )(qr, kpad, kpad, vpad, vpad)


def _att_bwd_call(qr, kpad, vpad, d_o, B, T):
    TQ = min(ATT_TILE_BWD, T)
    nq, qspec, side = _att_specs(T, TQ)

    def body(q_ref, kl_ref, kr_ref, vl_ref, vr_ref, do_ref, dq_ref, dk_ref, dv_ref):
        i = pl.program_id(2)
        q, do = q_ref[...], do_ref[...]
        left = lax.broadcasted_iota(jnp.int32, (1, KV_W), 1) < HEAD_DIM
        dq = jnp.zeros((TQ, KV_W), F32)
        dk = jnp.zeros((T, KV_W), F32)
        dv = jnp.zeros((T, KV_W), F32)
        for k_ref, v_ref, mask in ((kl_ref, vl_ref, left), (kr_ref, vr_ref, jnp.logical_not(left))):
            kk, vv = k_ref[...], v_ref[...]
            e, inv = _softmax_parts(_dot_nt(q, kk))
            dp = _dot_nt(do, vv)
            ds = e * (dp - inv * jnp.sum(e * dp, axis=1, keepdims=True))
            dq = dq + _dot(ds, kk) * inv
            dk = dk + _dot_tn(ds, jnp.where(mask, q * inv, 0.0))
            dv = dv + _dot_tn(e, jnp.where(mask, do * inv, 0.0))
        dq_ref[...] = dq

        @pl.when(i == 0)
        def _():
            dk_ref[...] = dk
            dv_ref[...] = dv

        @pl.when(i > 0)
        def _():
            dk_ref[...] += dk
            dv_ref[...] += dv

    acc = pl.BlockSpec((None, T, KV_W), lambda b, p, i: (p, b, 0))
    sds = jax.ShapeDtypeStruct
    return pl.pallas_call(
        body, name="att_bwd", grid=(B, 4, nq),
        in_specs=[qspec, side(0), side(1), side(0), side(1), qspec], out_specs=(qspec, acc, acc),
        out_shape=(sds((B * T, ATT_W), F32), sds((4, B * T, KV_W), F32), sds((4, B * T, KV_W), F32)),
        compiler_params=_cp(("arbitrary",) * 3),
    )(qr, kpad, kpad, vpad, vpad, d_o)


def _shift_specs(R, T, TT, width):
    tpe = T // TT
    nb8 = R // 8
    cur = pl.BlockSpec((TT, width), lambda i: (i, 0))
    prev = pl.BlockSpec((8, width), lambda i: (jnp.maximum(i * (TT // 8) - 1, 0), 0))
    nxt = pl.BlockSpec((8, width), lambda i: (jnp.minimum((i + 1) * (TT // 8), nb8 - 1), 0))
    return tpe, cur, prev, nxt


def _neighbours(cur, prev8, next8, i, tpe, TT):
    rows = lax.broadcasted_iota(jnp.int32, (TT, 1), 0)
    first = jnp.where(i % tpe == 0, 0.0, 1.0)
    last = jnp.where(i % tpe == tpe - 1, 0.0, 1.0)
    before = jnp.where(rows == 0, prev8[7:8, :] * first, pltpu.roll(cur, 1, 0))
    after = jnp.where(rows == TT - 1, next8[0:1, :] * last, pltpu.roll(cur, TT - 1, 0))
    return before, after


def _shift_fwd_call(x, taps, T):
    R, width = x.shape
    TT = min(SHIFT_TILE, T)
    tpe, cur, prev, nxt = _shift_specs(R, T, TT, width)

    def body(x_ref, p_ref, n_ref, t_ref, o_ref, vh_ref):
        xc = x_ref[...]
        before, after = _neighbours(xc, p_ref[...], n_ref[...], pl.program_id(0), tpe, TT)
        out = t_ref[0:1, :] * before + t_ref[1:2, :] * xc + t_ref[2:3, :] * after
        o_ref[...] = out
        left = lax.broadcasted_iota(jnp.int32, (1, KV_W), 1) < HEAD_DIM
        for p in range(RWKV_W // KV_W):
            pair = out[:, 2 * RWKV_W + p * KV_W:2 * RWKV_W + (p + 1) * KV_W]
            vh_ref[:, 2 * p * KV_W:(2 * p + 1) * KV_W] = jnp.where(left, pair, 0.0)
            vh_ref[:, (2 * p + 1) * KV_W:(2 * p + 2) * KV_W] = jnp.where(left, pltpu.roll(pair, HEAD_DIM, 1), 0.0)

    return pl.pallas_call(
        body, name="shift_fwd", grid=(R // TT,), in_specs=[cur, prev, nxt, _full(taps.shape)],
        out_specs=(cur, pl.BlockSpec((TT, 2 * RWKV_W), lambda i: (i, 0))),
        out_shape=(jax.ShapeDtypeStruct((R, width), F32), jax.ShapeDtypeStruct((R, 2 * RWKV_W), F32)),
        compiler_params=_cp(("arbitrary",)),
    )(x, x, x, taps)


def _shift_bwd_call(x, d, taps, T):
    R, width = x.shape
    TT = min(SHIFT_TILE, T)
    tpe, cur, prev, nxt = _shift_specs(R, T, TT, width)

    def body(x_ref, xp_ref, xn_ref, d_ref, dp_ref, dn_ref, t_ref, dx_ref, dt_ref):
        i = pl.program_id(0)
        xc, dc = x_ref[...], d_ref[...]
        d_before, d_after = _neighbours(dc, dp_ref[...], dn_ref[...], i, tpe, TT)
        dx_ref[...] = t_ref[2:3, :] * d_before + t_ref[1:2, :] * dc + t_ref[0:1, :] * d_after
        x_before, x_after = _neighbours(xc, xp_ref[...], xn_ref[...], i, tpe, TT)
        @pl.when(i == 0)
        def _():
            dt_ref[...] = jnp.zeros_like(dt_ref)

        for j, xs in enumerate((x_before, xc, x_after)):
            dt_ref[j:j + 1, :] += jnp.sum(dc * xs, axis=0, keepdims=True)

    return pl.pallas_call(
        body, name="shift_bwd", grid=(R // TT,),
        in_specs=[cur, prev, nxt, cur, prev, nxt, _full(taps.shape)], out_specs=(cur, _full((8, width))),
        out_shape=(jax.ShapeDtypeStruct((R, width), F32), jax.ShapeDtypeStruct((8, width), F32)),
        compiler_params=_cp(("arbitrary",)),
    )(x, x, x, d, d, d, taps)


def _lora_in(wa):
    lane = lax.broadcasted_iota(jnp.int32, (1, LORA_W), 1)
    return jnp.where(lane < LORA_W // 2, jnp.tanh(wa), wa)


def _rwkv_prep_call(shifted, wup, aup, w0, a0, k_k, k_a, bd, T):
    R = shifted.shape[0]
    TT = min(ROW_TILE, T)

    def body(k_ref, wa_ref, wup_ref, aup_ref, w0_ref, a0_ref, kk_ref, ka_ref, bd_ref, w_o, kt_o, akk_o, kk_o):
        twa = _lora_in(wa_ref[...])
        pre = [_dot(twa, m_ref[z]) for m_ref in (wup_ref, aup_ref) for z in range(2)]
        outs = _rwkv_pw(k_ref[...], pre[0], pre[1], pre[2], pre[3], w0_ref[...], a0_ref[...], kk_ref[...],
                        ka_ref[...], bd_ref[...], False)
        w_o[0], w_o[1], kt_o[0], kt_o[1], akk_o[0], akk_o[1] = outs[:6]
        kk_o[...] = outs[6]

    col = lambda c, w: pl.BlockSpec((TT, w), lambda i: (i, c))
    two = pl.BlockSpec((2, TT, RWKV_W), lambda i: (0, i, 0))
    sds = jax.ShapeDtypeStruct
    return pl.pallas_call(
        body, name="rwkv_prep", grid=(R // TT,),
        in_specs=[col(1, RWKV_W), col(3 * RWKV_W // LORA_W, LORA_W), _full(wup.shape), _full(aup.shape),
                  _full((2, RWKV_W)), _full((2, RWKV_W)), _full((1, RWKV_W)), _full((1, RWKV_W)), _full((256, 256))],
        out_specs=(two, two, two, col(0, RWKV_W)),
        out_shape=(sds((2, R, RWKV_W), F32),) * 3 + (sds((R, RWKV_W), F32),),
        compiler_params=_cp(("arbitrary",)),
    )(shifted, shifted, wup, aup, w0, a0, k_k, k_a, bd)


def _rwkv_prep_bwd_call(shifted, cts, wup, aup, w0, a0, k_k, k_a, bd, T):
    R = shifted.shape[0]
    TT = min(ROW_TILE, T)

    def body(k_ref, wa_ref, dw0, dkt0, dakk0, dkk0, dr0, dv0, dw1, dkt1, dakk1, dkk1, dr1, dv1, dr2_ref, dv2_ref, dkts_ref,
             wup_ref, aup_ref, w0_ref, a0_ref, kk_ref, ka_ref, bd_ref,
             dsh_ref, gwup_ref, gaup_ref, gw0_ref, ga0_ref, gkk_ref, gka_ref):
        dw_ref, dkt_ref, dakk_ref, dkk_ref, dr_ref, dv_ref = ((dw0, dw1), (dkt0, dkt1), (dakk0, dakk1), (dkk0, dkk1),
                                                              (dr0, dr1), (dv0, dv1))
        i = pl.program_id(0)
        wa = wa_ref[...]
        twa = _lora_in(wa)
        pre = [_dot(twa, m_ref[z]) for m_ref in (wup_ref, aup_ref) for z in range(2)]
        fn = functools.partial(_rwkv_pw, bd=bd_ref[...], diff=True)
        _, vjp = jax.vjp(fn, k_ref[...], pre[0], pre[1], pre[2], pre[3], w0_ref[...], a0_ref[...], kk_ref[...],
                         ka_ref[...])
        dkts = dkts_ref[...]
        dk, dpw0, dpw1, dpa0, dpa1, gw0, ga0, gkk, gka = vjp(
            (dw_ref[0][...], dw_ref[1][...], dkt_ref[0][...] + dkts, dkt_ref[1][...] + dkts, dakk_ref[0][...],
             dakk_ref[1][...], dkk_ref[0][...] + dkk_ref[1][...]))
        dtwa = (_dot_nt(dpw0, wup_ref[0]) + _dot_nt(dpw1, wup_ref[1]) + _dot_nt(dpa0, aup_ref[0])
                + _dot_nt(dpa1, aup_ref[1]))
        lane = lax.broadcasted_iota(jnp.int32, (1, LORA_W), 1)
        dsh_ref[:, 0:RWKV_W] = dr_ref[0][...] + dr_ref[1][...] + dr2_ref[...]
        dsh_ref[:, RWKV_W:2 * RWKV_W] = dk
        dsh_ref[:, 2 * RWKV_W:3 * RWKV_W] = dv_ref[0][...] + dv_ref[1][...] + dv2_ref[...]
        dsh_ref[:, 3 * RWKV_W:] = jnp.where(lane < LORA_W // 2, dtwa * (1.0 - twa * twa), dtwa)
        acc = ((gwup_ref.at[0], _dot_tn(twa, dpw0)), (gwup_ref.at[1], _dot_tn(twa, dpw1)),
               (gaup_ref.at[0], _dot_tn(twa, dpa0)), (gaup_ref.at[1], _dot_tn(twa, dpa1)),
               (gw0_ref, gw0), (ga0_ref, ga0), (gkk_ref, gkk), (gka_ref, gka))

        @pl.when(i == 0)
        def _():
            for ref, val in acc:
                ref[...] = val

        @pl.when(i > 0)
        def _():
            for ref, val in acc:
                ref[...] += val

    col = lambda c, w: pl.BlockSpec((TT, w), lambda i: (i, c))
    one = col(0, RWKV_W)
    sds = jax.ShapeDtypeStruct
    return pl.pallas_call(
        body, name="rwkv_prep_bwd", grid=(R // TT,),
        in_specs=[col(1, RWKV_W), col(3 * RWKV_W // LORA_W, LORA_W)] + [one] * 15 + [
                  _full(wup.shape), _full(aup.shape), _full((2, RWKV_W)), _full((2, RWKV_W)), _full((1, RWKV_W)),
                  _full((1, RWKV_W)), _full((256, 256))],
        out_specs=(pl.BlockSpec((TT, SHIFT_W), lambda i: (i, 0)), _full(wup.shape), _full(aup.shape),
                   _full((2, RWKV_W)), _full((2, RWKV_W)), _full((1, RWKV_W)), _full((1, RWKV_W))),
        out_shape=(sds((R, SHIFT_W), F32), sds(wup.shape, F32), sds(aup.shape, F32), sds((2, RWKV_W), F32),
                   sds((2, RWKV_W), F32), sds((1, RWKV_W), F32), sds((1, RWKV_W), F32)),
        compiler_params=_cp(("arbitrary",)),
    )(shifted, shifted, *cts, wup, aup, w0, a0, k_k, k_a, bd)


def _col_lhs(row, eye_b):
    return eye_b * row.astype(MXU_DTYPE)


def _colsum(x):
    return jnp.sum(x, axis=0, keepdims=True)


def _stacked_segsum(tiles, bd):
    res = _seg_dot(jnp.concatenate(tiles, axis=0), bd)
    return [res[j * HEAD_DIM:(j + 1) * HEAD_DIM] for j in range(len(tiles))]


def _scan_specs(B, T, C, nC):
    def blk(z, col, rev):
        idx = (lambda g: (z, 0, nC - 1 - g, col)) if rev else (lambda g: (z, 0, g, col))
        return pl.BlockSpec((None, B, C, RWKV_W), idx)

    def blk3(col, rev):
        idx = (lambda g: (0, nC - 1 - g, col)) if rev else (lambda g: (0, g, col))
        return pl.BlockSpec((B, C, RWKV_W), idx)

    return blk, blk3


def _scan_fwd_call(w, kt, akk, kk, shifted, eye_b, eye_f, bd, B, T):
    C = min(SCAN_CHUNK, T)
    nC = T // C
    blk, blk3 = _scan_specs(B, T, C, nC)

    def body(w0, kt0, akk0, kk0, v0, r0, w1, kt1, akk1, kk1, v1, r1, eb_ref, ef_ref, bd_ref, y0, y1, st, S):
        @pl.when(pl.program_id(0) == 0)
        def _():
            S[...] = jnp.zeros_like(S)

        st[0] = S[...].astype(MXU_DTYPE)
        dirs = ((w0, kt0, akk0, kk0, v0, r0, y0), (w1, kt1, akk1, kk1, v1, r1, y1))

        def step(s, carry):
            for z in range(2):
                row = s if z == 0 else C - 1 - s
                prev = jnp.maximum(s - 1, 0) if z == 0 else jnp.minimum(C - s, C - 1)
                wr, ktr, akkr, kkr, vr, rr, yr = dirs[z]
                tiles = []
                for b in range(B):
                    Sb = st[s, z * B + b]
                    tiles += [Sb * kkr[b, pl.ds(row, 1), :].astype(MXU_DTYPE),
                              _col_lhs(vr[b, pl.ds(row, 1), :], eb_ref[...]),
                              Sb * rr[b, pl.ds(prev, 1), :].astype(MXU_DTYPE)]
                res = _stacked_segsum(tiles, bd_ref[...])
                for b in range(B):
                    c = z * B + b
                    sab, vb, yb = res[3 * b:3 * b + 3]
                    ld = lambda ref: ref[b, pl.ds(row, 1), :]
                    Sn = S[c] * ld(wr) - sab * ld(akkr) + vb * ld(ktr)
                    S[c] = Sn
                    st[s + 1, c] = Sn.astype(MXU_DTYPE)
                    yr[b, pl.ds(prev, 1), :] = _colsum(ef_ref[...] * yb)
            return carry

        lax.fori_loop(0, C, step, 0, unroll=SCAN_UNROLL)
        for z in range(2):
            last = C - 1 if z == 0 else 0
            rr, yr = dirs[z][5], dirs[z][6]
            res = _stacked_segsum([st[C, z * B + b] * rr[b, last:last + 1, :].astype(MXU_DTYPE) for b in range(B)],
                                  bd_ref[...])
            for b in range(B):
                yr[b, last:last + 1, :] = _colsum(ef_ref[...] * res[b])

    ins, specs = [], []
    for z, rev in ((0, False), (1, True)):
        ins += [w, kt, akk, kk, shifted, shifted]
        specs += [blk(z, 0, rev), blk(z, 0, rev), blk(z, 0, rev), blk3(0, rev), blk3(2, rev), blk3(0, rev)]
    sds = jax.ShapeDtypeStruct
    return pl.pallas_call(
        body, name="scan_fwd", grid=(nC,),
        in_specs=specs + [_full((HEAD_DIM, RWKV_W)), _full((HEAD_DIM, RWKV_W)), _full((256, 256))],
        out_specs=(blk3(0, False), blk3(0, True),
                   pl.BlockSpec((None, C + 1, 2 * B, HEAD_DIM, RWKV_W), lambda g: (g, 0, 0, 0, 0))),
        out_shape=(sds((B, T, RWKV_W), F32), sds((B, T, RWKV_W), F32),
                   sds((nC, C + 1, 2 * B, HEAD_DIM, RWKV_W), MXU_DTYPE)),
        scratch_shapes=[pltpu.VMEM((2 * B, HEAD_DIM, RWKV_W), F32)],
        compiler_params=_cp(("arbitrary",)),
    )(*ins, eye_b, eye_f, bd)


def _scan_bwd_call(w, kt, akk, kk, shifted, v_heads, dys, st, eye_b, eye_f, bd, B, T):
    C = min(SCAN_CHUNK, T)
    nC = T // C
    blk, blk3 = _scan_specs(B, T, C, nC)
    nin = 7

    def body(*refs):
        d0, d1 = refs[:nin], refs[nin:2 * nin]
        st_ref, eb_ref, ef_ref, sel_ref, hm_ref, bd_ref = refs[2 * nin:2 * nin + 6]
        o0, o1 = refs[2 * nin + 6:2 * nin + 12], refs[2 * nin + 12:2 * nin + 18]
        COL, DYC, G = refs[2 * nin + 18:]

        @pl.when(pl.program_id(0) == 0)
        def _():
            G[...] = jnp.zeros_like(G)

        dirs = (d0 + (o0,), d1 + (o1,))

        def column_operands(s, z):
            row = s if z == 0 else C - 1 - s
            _, _, _, kkr, _, _, dyr, _ = dirs[z]
            tiles = []
            for b in range(B):
                tiles += [st_ref[s, z * B + b] * kkr[b, pl.ds(row, 1), :].astype(MXU_DTYPE),
                          _col_lhs(dyr[b, pl.ds(row, 1), :], eb_ref[...])]
            return tiles

        def keep_columns(res, z):
            for b in range(B):
                for k in range(2):
                    COL[k, z * B + b] = res[2 * b + k].astype(MXU_DTYPE)
                DYC[z * B + b] = res[2 * b + 1]

        for z in range(2):
            keep_columns(_stacked_segsum(column_operands(C - 1, z), bd_ref[...]), z)

        def bwd(it, carry):
            s = C - 1 - it
            for z in range(2):
                row = s if z == 0 else C - 1 - s
                wr, ktr, akkr, kkr, vr, rr, dyr, (dw_o, dkt_o, dakk_o, dkk_o, dr_o, dv_o) = dirs[z]
                tiles, Gcs = [], []
                for b in range(B):
                    c = z * B + b
                    Gc = G[c] + DYC[c] * rr[b, pl.ds(row, 1), :]
                    Gb = Gc.astype(MXU_DTYPE)
                    Gcs.append((Gc, Gb))
                    tiles += [Gb * akkr[b, pl.ds(row, 1), :].astype(MXU_DTYPE),
                              Gb * ktr[b, pl.ds(row, 1), :].astype(MXU_DTYPE)]
                res = _stacked_segsum(tiles + column_operands(jnp.maximum(s - 1, 0), z), bd_ref[...])
                for b in range(B):
                    c = z * B + b
                    Gc, Gb = Gcs[b]
                    gab, dvb = res[2 * b], res[2 * b + 1]
                    ld = lambda ref: ref[b, pl.ds(row, 1), :]
                    G[c] = Gc * ld(wr) - gab * ld(kkr)
                    Sb = st_ref[s, c]
                    prods = jnp.concatenate([Gb, st_ref[s + 1, c] * COL[1, c], Gb * Sb, Gb * COL[0, c],
                                             gab.astype(MXU_DTYPE) * Sb], axis=0)
                    v_rows = jnp.concatenate([vr[b, pl.ds(row, 1)][0], jnp.zeros((8, 3 * HEAD_DIM), F32)], axis=1)
                    lhs = jnp.concatenate([sel_ref[...], v_rows], axis=0).astype(MXU_DTYPE)
                    sums = jnp.dot(lhs, prods, preferred_element_type=F32)
                    for k, (ref, sign) in enumerate(((dr_o, 1.0), (dw_o, 1.0), (dakk_o, -1.0), (dkk_o, -1.0))):
                        ref[b, pl.ds(row, 1), :] = sign * sums[k:k + 1, :]
                    dkt_o[b, pl.ds(row, 1), :] = _colsum(sums[8:16] * hm_ref[...])
                    dv_o[b, pl.ds(row, 1), :] = _colsum(ef_ref[...] * dvb)
                keep_columns(res[2 * B:], z)
            return carry

        lax.fori_loop(0, C, bwd, 0, unroll=SCAN_UNROLL)

    ins, specs = [], []
    for z, rev in ((0, True), (1, False)):
        heads = pl.BlockSpec((B, C) + v_heads.shape[2:], (lambda g: (0, nC - 1 - g, 0, 0)) if rev else (lambda g: (0, g, 0, 0)))
        ins += [w, kt, akk, kk, v_heads, shifted, dys]
        specs += [blk(z, 0, rev), blk(z, 0, rev), blk(z, 0, rev), blk3(0, rev), heads, blk3(0, rev), blk3(0, rev)]
    sel = (jnp.arange(8)[:, None] + 1 == (jnp.arange(5 * HEAD_DIM) // HEAD_DIM)[None, :]).astype(F32)
    head_rows = (jnp.arange(RWKV_W // HEAD_DIM)[:, None] == (jnp.arange(RWKV_W) // HEAD_DIM)[None, :]).astype(F32)
    ins += [st, eye_b, eye_f, sel, head_rows, bd]
    specs += [pl.BlockSpec((None, C + 1, 2 * B, HEAD_DIM, RWKV_W), lambda g: (nC - 1 - g, 0, 0, 0, 0)),
              _full((HEAD_DIM, RWKV_W)), _full((HEAD_DIM, RWKV_W)), _full(sel.shape), _full(head_rows.shape),
              _full((256, 256))]
    sds = jax.ShapeDtypeStruct
    out_specs = tuple(blk3(0, True) for _ in range(6)) + tuple(blk3(0, False) for _ in range(6))
    res = pl.pallas_call(
        body, name="scan_bwd", grid=(nC,), in_specs=specs, out_specs=out_specs,
        out_shape=tuple(sds((B, T, RWKV_W), F32) for _ in range(12)),
        scratch_shapes=[pltpu.VMEM((2, 2 * B, HEAD_DIM, RWKV_W), MXU_DTYPE), pltpu.VMEM((2 * B, HEAD_DIM, RWKV_W), F32),
                        pltpu.VMEM((2 * B, HEAD_DIM, RWKV_W), F32)],
        compiler_params=_cp(("arbitrary",)),
    )(*ins)
    return list(res)


def _out_head_call(x2, tgt2, gate, y_att, g_att, y0, y1, shifted, kt, g_rw, w_out, g_post, gn_w, gn_b, r_k, bd, T):
    R = x2.shape[0]
    TT = min(ROW_TILE, T)
    tpe = T // TT

    def body(x_ref, t_ref, gate_ref, ya_ref, ga_ref, y0_ref, y1_ref, r_ref, v_ref, kt_ref, grw_ref, w_ref, gp_ref,
             gnw_ref, gnb_ref, rk_ref, bd_ref,
             loss_o, dy_o, dya_o, dga_o, dys_o, dr_o, dv_o, dkts_o, dgrw_o, dgate_o, gw_o, ggp_o, ggnw_o, ggnb_o, grk_o):
        i = pl.program_id(0)
        bd = bd_ref[...]
        mix = functools.partial(_mix_fn, bd=bd, diff=True)
        (ma, mr), mix_vjp = jax.vjp(mix, ya_ref[...], ga_ref[...], y0_ref[...] + y1_ref[...], r_ref[...], v_ref[...],
                                    kt_ref[0] + kt_ref[1], grw_ref[...], gnw_ref[...], gnb_ref[...], rk_ref[...])
        out = _dot(ma, w_ref[0:ATT_W, :]) + _dot(mr, w_ref[ATT_W:, :])
        loss, loss_vjp = jax.vjp(_loss_fn, out, x_ref[...], t_ref[...], gate_ref[0], gp_ref[...])
        d_out, dy, _, dgate, dgp = loss_vjp(jnp.ones((1, 1), F32))
        dy_o[...] = dy
        dma = _dot_nt(d_out, w_ref[0:ATT_W, :])
        dmr = _dot_nt(d_out, w_ref[ATT_W:, :])
        dya_o[...], dga_o[...], dys_o[...], dr_o[...], dv_o[...], dkts_o[...], dgrw_o[...], dgnw, dgnb, drk = \
            mix_vjp((dma, dmr))
        gw = jnp.concatenate([_dot_tn(ma, d_out), _dot_tn(mr, d_out)], axis=0)
        acc = ((loss_o, jnp.broadcast_to(loss, (8, 128))), (gw_o, gw), (ggp_o, dgp), (ggnw_o, dgnw), (ggnb_o, dgnb),
               (grk_o, drk))

        @pl.when(i == 0)
        def _():
            for ref, val in acc:
                ref[...] = val

        @pl.when(i > 0)
        def _():
            for ref, val in acc:
                ref[...] += val

        @pl.when(i % tpe == 0)
        def _():
            dgate_o[0] = dgate

        @pl.when(i % tpe > 0)
        def _():
            dgate_o[0] += dgate

    row = lambda w, c=0: pl.BlockSpec((TT, w), lambda i: (i, c))
    two = pl.BlockSpec((2, TT, RWKV_W), lambda i: (0, i, 0))
    per_ex = pl.BlockSpec((1, 1, D_MODEL), lambda i: (i // tpe, 0, 0))
    sds = jax.ShapeDtypeStruct
    r512 = sds((R, RWKV_W), F32)
    return pl.pallas_call(
        body, name="out_head", grid=(R // TT,),
        in_specs=[row(D_MODEL), row(D_MODEL), per_ex, row(ATT_W), row(ATT_W), row(RWKV_W), row(RWKV_W), row(RWKV_W, 0),
                  row(RWKV_W, 2), two,
                  row(RWKV_W), _full(w_out.shape), _full((1, D_MODEL)), _full((1, RWKV_W)), _full((1, RWKV_W)),
                  _full((1, RWKV_W)), _full((256, 256))],
        out_specs=(_full((8, 128)), row(D_MODEL), row(ATT_W), row(ATT_W), row(RWKV_W), row(RWKV_W), row(RWKV_W),
                   row(RWKV_W), row(RWKV_W), per_ex, _full((D_MODEL, D_MODEL)), _full((1, D_MODEL)), _full((1, RWKV_W)),
                   _full((1, RWKV_W)), _full((1, RWKV_W))),
        out_shape=(sds((8, 128), F32), sds((R, D_MODEL), F32), r512, r512, r512, r512, r512, r512, r512,
                   sds((R // T, 1, D_MODEL), F32), sds((D_MODEL, D_MODEL), F32), sds((1, D_MODEL), F32),
                   sds((1, RWKV_W), F32), sds((1, RWKV_W), F32), sds((1, RWKV_W), F32)),
        compiler_params=_cp(("arbitrary",)),
    )(x2, tgt2, gate, y_att, g_att, y0, y1, shifted, shifted, kt, g_rw, w_out, g_post, gn_w, gn_b, r_k, bd)


def _in_proj_bwd_call(x2, dy, shift, scale, g_pre, w_in, qg, kg, cos, sin, bd, q_raw, k_raw, dqr, dkp, dvp,
                      d_gatt, d_rin, d_grw, T):
    R = x2.shape[0]
    TT = min(ROW_TILE, T)
    tpe = T // TT

    def body(x_ref, dy_ref, sh_ref, sc_ref, gp_ref, w_ref, qg_ref, kg_ref, cos_ref, sin_ref, bd_ref, q_ref, k_ref,
             dqr_ref, dkp_ref, dvp_ref, dga_ref, drin_ref, dgrw_ref,
             dx_o, dproj_o, dsh_o, dsc_o, ggp_o, gqg_o, gkg_o):
        i = pl.program_id(0)
        cos, sin, bd = cos_ref[...], sin_ref[...], bd_ref[...]
        left = lax.broadcasted_iota(jnp.int32, (1, KV_W), 1) < HEAD_DIM

        def kv_grad(ref):
            a = ref[0] + ref[1]
            b = ref[2] + ref[3]
            return jnp.where(left, a + pltpu.roll(a, HEAD_DIM, 1), b + pltpu.roll(b, HEAD_DIM, 1))

        qfn = functools.partial(_qk_fn, cos=jnp.tile(cos, (1, 4)), sin=jnp.tile(sin, (1, 4)), bd=bd, scale=ATT_SCALE,
                                diff=True)
        _, q_vjp = jax.vjp(qfn, q_ref[...], qg_ref[...])
        dq, gqg = q_vjp(dqr_ref[...])
        kfn = functools.partial(_qk_fn, cos=cos, sin=sin, bd=bd, scale=1.0, diff=True)
        _, k_vjp = jax.vjp(kfn, k_ref[...], kg_ref[...])
        dk, gkg = k_vjp(kv_grad(dkp_ref))
        pieces = ((C_Q, C_K, dq), (C_K, C_V, dk), (C_V, C_GA, kv_grad(dvp_ref)), (C_GA, C_RIN, dga_ref[...]),
                  (C_RIN, C_GRW, drin_ref[...]), (C_GRW, C_END, dgrw_ref[...]))
        dh = jnp.zeros((TT, D_MODEL), F32)
        for c0, c1, val in pieces:
            vb = val.astype(MXU_DTYPE)
            dproj_o[:, c0:c1] = vb
            dh = dh + _dot(vb, w_ref[c0:c1, :])
        _, pre_vjp = jax.vjp(_pre_fn, x_ref[...], sh_ref[0], sc_ref[0], gp_ref[...])
        dx, dsh, dsc, ggp = pre_vjp(dh)
        dx_o[...] = dx + dy_ref[...]
        acc = ((ggp_o, ggp), (gqg_o, gqg), (gkg_o, gkg))

        @pl.when(i == 0)
        def _():
            for ref, val in acc:
                ref[...] = val

        @pl.when(i > 0)
        def _():
            for ref, val in acc:
                ref[...] += val

        @pl.when(i % tpe == 0)
        def _():
            dsh_o[0] = dsh
            dsc_o[0] = dsc

        @pl.when(i % tpe > 0)
        def _():
            dsh_o[0] += dsh
            dsc_o[0] += dsc

    row = lambda w: pl.BlockSpec((TT, w), lambda i: (i, 0))
    per_ex = pl.BlockSpec((1, 1, D_MODEL), lambda i: (i // tpe, 0, 0))
    tab = pl.BlockSpec((TT, KV_W), lambda i: (i % tpe, 0))
    pad = pl.BlockSpec((4, TT, KV_W), lambda i: (0, i, 0))
    sds = jax.ShapeDtypeStruct
    nb = R // T
    return pl.pallas_call(
        body, name="in_proj_bwd", grid=(R // TT,),
        in_specs=[row(D_MODEL), row(D_MODEL), per_ex, per_ex, _full((1, D_MODEL)), _full(w_in.shape), _full((1, ATT_W)),
                  _full((1, KV_W)), tab, tab, _full((256, 256)), row(ATT_W), row(KV_W), row(ATT_W), pad, pad,
                  row(ATT_W), row(SHIFT_W), row(RWKV_W)],
        out_specs=(row(D_MODEL), row(C_END), per_ex, per_ex, _full((1, D_MODEL)), _full((1, ATT_W)), _full((1, KV_W))),
        out_shape=(sds((R, D_MODEL), F32), sds((R, C_END), MXU_DTYPE), sds((nb, 1, D_MODEL), F32),
                   sds((nb, 1, D_MODEL), F32), sds((1, D_MODEL), F32), sds((1, ATT_W), F32), sds((1, KV_W), F32)),
        compiler_params=_cp(("arbitrary",)),
    )(x2, dy, shift, scale, g_pre, w_in, qg, kg, cos, sin, bd, q_raw, k_raw, dqr, dkp, dvp, d_gatt, d_rin, d_grw)


def _w_in_grad_call(hb, dproj):
    R = hb.shape[0]
    TT = min(W_GRAD_ROWS, R)
    CB = 1152
    last = R // TT - 1

    def body(h_ref, d_ref, o_ref, acc):
        g = _dot_tn(h_ref[...], d_ref[...])

        @pl.when(pl.program_id(1) == 0)
        def _():
            acc[...] = g

        @pl.when(pl.program_id(1) > 0)
        def _():
            acc[...] += g

        @pl.when(pl.program_id(1) == last)
        def _():
            o_ref[...] = acc[...].astype(o_ref.dtype)

    return pl.pallas_call(
        body, name="w_in_grad", grid=(C_END // CB, R // TT),
        in_specs=[pl.BlockSpec((TT, D_MODEL), lambda j, i: (i, 0)), pl.BlockSpec((TT, CB), lambda j, i: (i, j))],
        out_specs=pl.BlockSpec((D_MODEL, CB), lambda j, i: (0, j)),
        out_shape=jax.ShapeDtypeStruct((D_MODEL, C_END), MXU_DTYPE),
        scratch_shapes=[pltpu.VMEM((D_MODEL, CB), F32)], compiler_params=_cp(("arbitrary", "arbitrary")),
    )(hb, dproj)


def _adam_refs(p_ref, w_ref, m_ref, v_ref, g_o, d_o, m_o, v_o):
    g = p_ref[0].astype(F32)
    for j in range(1, p_ref.shape[0]):
        g = g + p_ref[j].astype(F32)
    m2 = ADAM_B1 * m_ref[...] + (1.0 - ADAM_B1) * g
    v2 = ADAM_B2 * v_ref[...] + (1.0 - ADAM_B2) * jnp.square(g)
    m_hat = m2 / (1.0 - ADAM_B1 ** ADAM_STEP)
    v_hat = v2 / (1.0 - ADAM_B2 ** ADAM_STEP)
    g_o[...] = g
    d_o[...] = -ADAM_LR * (m_hat / (jnp.sqrt(v_hat) + ADAM_EPS) + ADAM_WD * w_ref[...])
    m_o[...] = m2
    v_o[...] = v2


def _adam_small_call(items, name):
    n = len(items)

    def body(*refs):
        for k in range(n):
            _adam_refs(*refs[4 * k:4 * k + 4], *refs[4 * n + 4 * k:4 * n + 4 * k + 4])

    out_shape = tuple(jax.ShapeDtypeStruct(w.shape, F32) for _, w, _, _ in items for _ in range(4))
    out = pl.pallas_call(body, name=name, out_shape=out_shape)(*[a for item in items for a in item])
    return [out[4 * k:4 * k + 4] for k in range(n)]


def _adam_call(parts, w, m, v, name, row_tile=None):
    P, M, N = parts.shape
    TM = M if row_tile is None else row_tile

    def body(*refs):
        _adam_refs(*refs)

    blk = pl.BlockSpec((TM, N), lambda i: (i, 0))
    return pl.pallas_call(
        body, name=name, grid=(M // TM,),
        in_specs=[pl.BlockSpec((P, TM, N), lambda i: (0, i, 0)), blk, blk, blk], out_specs=(blk,) * 4,
        out_shape=(jax.ShapeDtypeStruct((M, N), F32),) * 4, compiler_params=_cp(("arbitrary",)),
    )(parts, w, m, v)


_SMALL_ROWS = 136


def _pack_small(taps, w_up, w0, a_up, a0):
    flat = jnp.concatenate([taps.reshape(-1), w_up.reshape(-1), w0.reshape(-1), a_up.reshape(-1), a0.reshape(-1)])
    return jnp.pad(flat, (0, _SMALL_ROWS * 128 - flat.shape[0])).reshape(_SMALL_ROWS, 128)


def _unpack_small(packed):
    n = packed.shape[0]
    flat = packed.reshape(n, -1)
    out, o = [], 0
    for shape in ((3, 208), (2, 64, 64), (2, 64), (2, 64, 64), (2, 64)):
        size = 1
        for s in shape:
            size *= s
        out.append(flat[:, o:o + size].reshape((n,) + shape))
        o += size
    return out


def _cols_to_full(blocks):
    nd = blocks.ndim
    moved = jnp.moveaxis(blocks, 0, nd - 2)
    return moved.reshape(moved.shape[:-2] + (moved.shape[-2] * moved.shape[-1],))


def _full_to_cols(full):
    k = full.shape[-1] // NDEV
    return jnp.moveaxis(full.reshape(full.shape[:-1] + (NDEV, k)), -2, 0)


_REP_SIZES = (("g_pre", 1024), ("q_norm_g", 64), ("k_norm_g", 64), ("k_k", 512), ("k_a", 512), ("r_k", 512),
              ("gn_w", 512), ("gn_b", 512), ("g_post", 1024))
_REP_ROWS = 40


def kernel(x, c, w_ada, b_ada, g_pre, w_in, q_norm_g, k_norm_g, shift_taps, w_up, w0, a_up, a0, k_k, k_a, r_k, gn_w, gn_b, w_out, g_post, loss_target, m_w_ada, m_b_ada, m_g_pre, m_w_in, m_q_norm_g, m_k_norm_g, m_shift_taps, m_w_up, m_w0, m_a_up, m_a0, m_k_k, m_k_a, m_r_k, m_gn_w, m_gn_b, m_w_out, m_g_post, v_w_ada, v_b_ada, v_g_pre, v_w_in, v_q_norm_g, v_k_norm_g, v_shift_taps, v_w_up, v_w0, v_a_up, v_a0, v_k_k, v_k_a, v_r_k, v_gn_w, v_gn_b, v_w_out, v_g_post):
    B, T, _ = x.shape
    R = B * T
    me = 4 * lax.axis_index("x") + 2 * lax.axis_index("y") + lax.axis_index("c")
    x2 = x.reshape(R, D_MODEL)
    tgt2 = loss_target.reshape(R, D_MODEL)

    seg = jnp.arange(256) // HEAD_DIM
    bd = (seg[:, None] == seg[None, :]).astype(MXU_DTYPE)
    eye = (jnp.arange(HEAD_DIM)[:, None] == (jnp.arange(RWKV_W) % HEAD_DIM)[None, :])
    eye_b, eye_f = eye.astype(MXU_DTYPE), eye.astype(F32)
    cos, sin = _rope_tables(T)

    c_g, w_in_g, w_out_g, small_g = _exchange(
        [c, w_in[0].T.astype(MXU_DTYPE), w_out[0].astype(MXU_DTYPE),
         _pack_small(shift_taps[0], w_up[0], w0[0], a_up[0], a0[0])], ["all"] * 4, "gather_params")
    c_all = c_g.reshape(NDEV * B, D_MODEL)
    w_in_f = w_in_g.reshape(C_END, D_MODEL)
    w_out_f = w_out_g.reshape(D_MODEL, D_MODEL)
    taps_b, w_up_b, w0_b, a_up_b, a0_b = _unpack_small(small_g)
    taps_f = jnp.pad(_cols_to_full(taps_b), ((0, 5), (0, 0)))
    w_up_f, a_up_f = _cols_to_full(w_up_b), _cols_to_full(a_up_b)
    w0_f, a0_f = _cols_to_full(w0_b), _cols_to_full(a0_b)
    wup_pad = jnp.pad(w_up_f, ((0, 0), (0, 64), (0, 0))).astype(MXU_DTYPE)
    aup_pad = jnp.pad(a_up_f, ((0, 0), (64, 0), (0, 0))).astype(MXU_DTYPE)

    ncol = w_ada.shape[2]
    b_cols = lax.dynamic_slice(b_ada, (0, me * ncol), (1, ncol))
    mod_cols = _mod_call(c_all, w_ada[0].astype(MXU_DTYPE), b_cols)
    (mod_g,) = _exchange([mod_cols], ["all"], "gather_mod")
    mod = lax.dynamic_slice(_cols_to_full(mod_g), (me * B, 0), (B, 3 * D_MODEL))
    shift, scale, gate = [mod[:, j * D_MODEL:(j + 1) * D_MODEL].reshape(B, 1, D_MODEL) for j in range(3)]

    qg = jnp.tile(q_norm_g, (1, ATT_W // HEAD_DIM))
    kg = jnp.tile(k_norm_g, (1, KV_W // HEAD_DIM))
    rk_row = r_k.reshape(1, RWKV_W)

    hb, qr, kpad, vpad, q_raw, k_raw, g_att, rin, g_rw = _in_proj_call(
        x2, shift, scale, g_pre, w_in_f, qg, kg, cos, sin, bd, T)
    y_att = _att_fwd_call(qr, kpad, vpad, B, T)
    shifted, v_rows = _shift_fwd_call(rin, taps_f, T)
    w_s, kt_s, akk_s, kk_s = _rwkv_prep_call(shifted, wup_pad, aup_pad, w0_f, a0_f, k_k, k_a, bd, T)
    sh3 = shifted.reshape(B, T, SHIFT_W)
    r4 = lambda a: a.reshape(2, B, T, RWKV_W)
    y0, y1, st = _scan_fwd_call(r4(w_s), r4(kt_s), r4(akk_s), kk_s.reshape(B, T, RWKV_W), sh3, eye_b, eye_f, bd, B, T)

    (loss_blk, dy, d_yatt, d_gatt, d_ys, d_r2, d_v2, d_kts, d_grw, d_gate, g_wout, g_gpost, g_gnw, g_gnb,
     g_rk) = _out_head_call(x2, tgt2, gate, y_att, g_att, y0.reshape(R, RWKV_W), y1.reshape(R, RWKV_W), shifted, kt_s,
                            g_rw, w_out_f, g_post, gn_w, gn_b, rk_row, bd, T)
    v_heads = v_rows.reshape(B, T, RWKV_W // HEAD_DIM, 2 * HEAD_DIM)
    scan_cts = _scan_bwd_call(r4(w_s), r4(kt_s), r4(akk_s), kk_s.reshape(B, T, RWKV_W), sh3, v_heads,
                              d_ys.reshape(B, T, RWKV_W), st, eye_b, eye_f, bd, B, T)
    scan_cts = [a.reshape(R, RWKV_W) for a in scan_cts]
    d_shifted, g_wup, g_aup, g_w0, g_a0, g_kk, g_ka = _rwkv_prep_bwd_call(
        shifted, scan_cts + [d_r2, d_v2, d_kts], wup_pad, aup_pad, w0_f, a0_f, k_k, k_a, bd, T)
    d_rin, g_taps = _shift_bwd_call(rin, d_shifted, taps_f, T)
    dqr, dkp, dvp = _att_bwd_call(qr, kpad, vpad, d_yatt, B, T)
    grad_x, dproj, d_shift, d_scale, g_gpre, g_qg, g_kg = _in_proj_bwd_call(
        x2, dy, shift, scale, g_pre, w_in_f, qg, kg, cos, sin, bd, q_raw, k_raw, dqr, dkp, dvp, d_gatt, d_rin, d_grw, T)
    g_win = _w_in_grad_call(hb, dproj)

    rep = jnp.concatenate([g_gpre.reshape(-1), g_qg.reshape(-1, HEAD_DIM).sum(0), g_kg.reshape(-1, HEAD_DIM).sum(0),
                           g_kk.reshape(-1), g_ka.reshape(-1), g_rk.reshape(-1), g_gnw.reshape(-1), g_gnb.reshape(-1),
                           g_gpost.reshape(-1), loss_blk[0, :1]])
    rep = jnp.pad(rep, (0, _REP_ROWS * 128 - rep.shape[0])).reshape(_REP_ROWS, 128)
    dmod = jnp.concatenate([d_shift, d_scale, d_gate], axis=2).reshape(B, 3 * D_MODEL)
    small_parts = jax.vmap(_pack_small)(_full_to_cols(g_taps[:3]), _full_to_cols(g_wup[:, :64, :]), _full_to_cols(g_w0),
                                        _full_to_cols(g_aup[:, 64:, :]), _full_to_cols(g_a0))
    by_core = lambda a: jnp.swapaxes(a.reshape((NDEV // 2, 2) + a.shape[1:]), 0, 1).astype(MXU_DTYPE)
    s_win, s_wout = _pair_sum_call(
        [by_core(_full_to_cols(g_win)), by_core(g_wout.reshape(NDEV, D_MODEL // NDEV, D_MODEL))], "reduce_pair")
    p_win, p_wout, p_small, dmod_g, rep_g = _exchange(
        [s_win, s_wout, small_parts, dmod, rep], ["chips", "chips", "scatter", "all", "all"], "reduce_grads")
    dmod_all = dmod_g.reshape(NDEV * B, 3 * D_MODEL)
    g_wada = _wada_grad_call(c_all, lax.dynamic_slice(dmod_all, (0, me * ncol), (NDEV * B, ncol)))

    res, small = {}, []

    def adam(name, parts, w, m, v, row_tile=None, alone=False):
        two_d = (-1, w.shape[-1])
        item = (parts.reshape((parts.shape[0],) + w.reshape(two_d).shape), w.reshape(two_d), m.reshape(two_d),
                v.reshape(two_d))
        if alone:
            res[name] = [o.reshape(w.shape) for o in _adam_call(*item, "adam_" + name, row_tile)]
        else:
            small.append((name, w.shape, item))

    adam("w_ada", g_wada[None], w_ada, m_w_ada, v_w_ada, alone=True)
    adam("b_ada", dmod_all.reshape(NDEV * B, 1, 3 * D_MODEL), b_ada, m_b_ada, v_b_ada)
    adam("w_in", p_win, w_in, m_w_in, v_w_in, 512, alone=True)
    adam("w_out", p_wout, w_out, m_w_out, v_w_out, alone=True)
    taps_p, wup_p, w0_p, aup_p, a0_p = _unpack_small(p_small)
    adam("shift_taps", taps_p, shift_taps, m_shift_taps, v_shift_taps)
    adam("w_up", wup_p, w_up, m_w_up, v_w_up)
    adam("w0", w0_p, w0, m_w0, v_w0)
    adam("a_up", aup_p, a_up, m_a_up, v_a_up)
    adam("a0", a0_p, a0, m_a0, v_a0)
    rep_flat = rep_g.reshape(NDEV, -1)
    off = 0
    given = dict(g_pre=(g_pre, m_g_pre, v_g_pre), q_norm_g=(q_norm_g, m_q_norm_g, v_q_norm_g),
                 k_norm_g=(k_norm_g, m_k_norm_g, v_k_norm_g), k_k=(k_k, m_k_k, v_k_k), k_a=(k_a, m_k_a, v_k_a),
                 r_k=(r_k, m_r_k, v_r_k), gn_w=(gn_w, m_gn_w, v_gn_w), gn_b=(gn_b, m_gn_b, v_gn_b),
                 g_post=(g_post, m_g_post, v_g_post))
    for name, size in _REP_SIZES:
        adam(name, rep_flat[:, off:off + size], *given[name])
        off += size
    for (name, shape, _), out in zip(small, _adam_small_call([item for _, _, item in small], "adam_small")):
        res[name] = [o.reshape(shape) for o in out]

    loss = jnp.sum(rep_flat[:, off])
    order = ["w_ada", "b_ada", "g_pre", "w_in", "q_norm_g", "k_norm_g", "shift_taps", "w_up", "w0", "a_up", "a0", "k_k",
             "k_a", "r_k", "gn_w", "gn_b", "w_out", "g_post"]
    return (loss, grad_x.reshape(B, T, D_MODEL), *[res[n][0] for n in order], *[res[n][1] for n in order],
            *[res[n][2] for n in order], *[res[n][3] for n in order])
```

```python
import functools

import jax
import jax.numpy as jnp
from jax import lax
from jax.experimental import pallas as pl
from jax.experimental.pallas import tpu as pltpu

F32 = jnp.float32
MXU_DTYPE = jnp.bfloat16
MESH = pl.DeviceIdType.MESH
NDEV = 8

D_MODEL = 1024
HEAD_DIM = 64
ATT_W = 512
KV_W = 128
RWKV_W = 512
LORA_W = 128
SHIFT_W = 3 * RWKV_W + LORA_W
GRID_W = 64
ROPE_THETA = 10000.0
DECAY_SCALE = 0.6065306597126334
NORM_EPS = 1e-6
GN_EPS = 64e-5
L2_EPS = 1e-12
ATT_SCALE = HEAD_DIM ** -0.5
C_Q, C_K, C_V, C_GA, C_RIN, C_GRW, C_END = 0, 512, 640, 768, 1280, 2944, 3456

ADAM_LR, ADAM_B1, ADAM_B2, ADAM_EPS, ADAM_WD, ADAM_STEP = 0.001, 0.9, 0.999, 1e-08, 0.01, 10

ROW_TILE = 256
SHIFT_TILE = 512
W_GRAD_ROWS = 2048
ATT_TILE_FWD = 256
ATT_TILE_BWD = 1024
SCAN_CHUNK = 64
SCAN_UNROLL = 16
VMEM_LIMIT = 56 * 1024 * 1024


def _cp(sem=None):
    return pltpu.CompilerParams(dimension_semantics=sem, vmem_limit_bytes=VMEM_LIMIT)


def _dot(a, b, dims=(((1,), (0,)), ((), ()))):
    return lax.dot_general(a.astype(MXU_DTYPE), b.astype(MXU_DTYPE), dims, preferred_element_type=F32)


def _dot_nt(a, b):
    return _dot(a, b, (((1,), (1,)), ((), ())))


def _dot_tn(a, b):
    return _dot(a, b, (((0,), (0,)), ((), ())))


def _seg_dot(xb, bd):
    n = xb.shape[1]
    if n <= 256:
        return jnp.dot(xb, bd[:n, :n], preferred_element_type=F32)
    parts = [jnp.dot(xb[:, c:c + 256], bd, preferred_element_type=F32) for c in range(0, n, 256)]
    return jnp.concatenate(parts, axis=1)


def _segsum_raw(x, bd):
    rows = x.shape[0]
    hi = x.astype(MXU_DTYPE)
    lo = (x - hi.astype(F32)).astype(MXU_DTYPE)
    both = _seg_dot(jnp.concatenate([hi, lo], axis=0), bd)
    return both[:rows] + both[rows:]


@jax.custom_vjp
def _segsum_d(x, bd):
    return _segsum_raw(x, bd)


def _segsum_d_fwd(x, bd):
    return _segsum_raw(x, bd), bd


def _segsum_d_bwd(bd, ct):
    return _segsum_raw(ct, bd), jnp.zeros_like(bd)


_segsum_d.defvjp(_segsum_d_fwd, _segsum_d_bwd)


def _rope_tables(T):
    t = jnp.arange(T, dtype=F32)
    row = jnp.floor(t / GRID_W)
    col = t - row * GRID_W
    n_freq = HEAD_DIM // 4
    inv_freq = ROPE_THETA ** (-jnp.arange(n_freq, dtype=F32) / n_freq)
    d = jnp.arange(HEAD_DIM)
    pos = jnp.where((d < HEAD_DIM // 2)[None, :], row[:, None], col[:, None])
    ang = pos * inv_freq[d % n_freq][None, :]
    sign = jnp.where((d % 32) < 16, -1.0, 1.0).astype(F32)[None, :]
    cos = jnp.cos(ang)
    sin = jnp.sin(ang) * sign
    return jnp.tile(cos, (1, 2)), jnp.tile(sin, (1, 2))


def _rope_raw(x, cos, sin):
    n = x.shape[1]
    lane = lax.broadcasted_iota(jnp.int32, (1, n), 1)
    first = (lane % 32) < 16
    partner = jnp.where(first, pltpu.roll(x, n - 16, 1), pltpu.roll(x, 16, 1))
    return x * cos + partner * sin


@jax.custom_vjp
def _rope_d(x, cos, sin):
    return _rope_raw(x, cos, sin)


def _rope_d_fwd(x, cos, sin):
    return _rope_raw(x, cos, sin), (cos, sin)


def _rope_d_bwd(res, ct):
    cos, sin = res
    return _rope_raw(ct, cos, -sin), jnp.zeros_like(cos), jnp.zeros_like(sin)


_rope_d.defvjp(_rope_d_fwd, _rope_d_bwd)


def _rms(x, g):
    return x * lax.rsqrt(jnp.mean(x * x, axis=-1, keepdims=True) + NORM_EPS) * g


def _pre_fn(x, shift, scale, g_pre):
    return _rms(x, g_pre) * (1.0 + scale) + shift


def _qk_fn(q, g, cos, sin, bd, scale, diff):
    segsum = _segsum_d if diff else _segsum_raw
    rope = _rope_d if diff else _rope_raw
    qn = q * lax.rsqrt(segsum(q * q, bd) * (1.0 / HEAD_DIM) + NORM_EPS) * g
    return rope(qn, cos, sin) * scale


def _silu(x):
    return x * jax.nn.sigmoid(x)


def _rwkv_pw(k, pw0, pw1, pa0, pa1, w0, a0, k_k, k_a, bd, diff):
    segsum = _segsum_d if diff else _segsum_raw
    kk = k * k_k
    kk = kk * lax.rsqrt(segsum(kk * kk, bd) + L2_EPS)
    ws, kts, akks = [], [], []
    for z, (pw, pa) in enumerate(((pw0, pa0), (pw1, pa1))):
        w = jnp.exp(-DECAY_SCALE * jax.nn.sigmoid(w0[z:z + 1, :] + pw))
        a = jax.nn.sigmoid(a0[z:z + 1, :] + pa)
        ws.append(w)
        kts.append(k * (1.0 + (a - 1.0) * k_a))
        akks.append(a * kk)
    return ws[0], ws[1], kts[0], kts[1], akks[0], akks[1], kk


def _mix_fn(y_att, g_att, ys, r, v, kts, g_rw, gn_w, gn_b, r_k, bd, diff):
    segsum = _segsum_d if diff else _segsum_raw
    mu = segsum(ys, bd) * (1.0 / HEAD_DIM)
    d = ys - mu
    var = segsum(d * d, bd) * (1.0 / HEAD_DIM)
    yn = d * lax.rsqrt(var + GN_EPS) * gn_w + gn_b
    bonus = segsum(r * kts * r_k, bd) * v
    return y_att * _silu(g_att), (yn + bonus) * _silu(g_rw)


def _loss_fn(out, x, tgt, gate, g_post):
    e = x + gate * _rms(out, g_post) - tgt
    s = jnp.sum(e * e, axis=1, keepdims=True)
    return jnp.sum(s, axis=0, keepdims=True) * (0.5 / D_MODEL)


def _exchange(arrays, modes, name):
    n = len(arrays)
    out_shape = tuple(
        jax.ShapeDtypeStruct(((NDEV,) + tuple(a.shape)) if mode == "all" else tuple(a.shape), a.dtype)
        for a, mode in zip(arrays, modes))
    chips = (4, 2, 6)

    def body(*refs):
        ins, outs = refs[:n], refs[n:2 * n]
        send_sems, recv_sems, local_sems = refs[2 * n:]
        ix, iy, ic = lax.axis_index("x"), lax.axis_index("y"), lax.axis_index("c")
        me = 4 * ix + 2 * iy + ic

        def peer(m):
            px = 1 - ix if (m >> 2) & 1 else ix
            py = 1 - iy if (m >> 1) & 1 else iy
            pc = 1 - ic if m & 1 else ic
            return (px, py, pc), 4 * px + 2 * py + pc

        def copy(k, j, src_ref, slot, to):
            return pltpu.make_async_remote_copy(src_ref=src_ref, dst_ref=outs[k].at[slot], send_sem=send_sems.at[k, j],
                                                recv_sem=recv_sems.at[k, j], device_id=to, device_id_type=MESH)

        local, sends, arrivals, forwards = [], [], [], []
        for k in range(n):
            if modes[k] == "scatter":
                local.append(pltpu.make_async_copy(ins[k].at[me], outs[k].at[me], local_sems.at[k]))
                for m in range(1, NDEV):
                    to, p = peer(m)
                    sends.append(copy(k, m - 1, ins[k].at[p], me, to))
                    arrivals.append(copy(k, m - 1, ins[k].at[p], p, to))
            elif modes[k] == "chips":
                mine = me // 2
                local.append(pltpu.make_async_copy(ins[k].at[mine], outs[k].at[mine], local_sems.at[k]))
                for j, m in enumerate(chips):
                    to, p = peer(m)
                    sends.append(copy(k, j, ins[k].at[p // 2], mine, to))
                    arrivals.append(copy(k, j, ins[k].at[p // 2], p // 2, to))
            else:
                local.append(pltpu.make_async_copy(ins[k], outs[k].at[me], local_sems.at[k]))
                sib, sib_slot = peer(1)
                sends.append(copy(k, 0, ins[k], me, sib))
                for j, m in enumerate(chips):
                    to, p = peer(m)
                    sends.append(copy(k, 1 + j, ins[k], me, to))
                    forwards.append((copy(k, 1 + j, ins[k], p, to), copy(k, 4 + j, outs[k].at[p], p, sib)))
                    arrivals.append(copy(k, 4 + j, ins[k], peer(m ^ 1)[1], sib))
                arrivals.append(copy(k, 0, ins[k], sib_slot, sib))
        for cp in local + sends:
            cp.start()
        for arrived, onward in forwards:
            arrived.wait_recv()
            onward.start()
        for cp in arrivals:
            cp.wait_recv()
        for cp in sends + [onward for _, onward in forwards]:
            cp.wait_send()
        for cp in local:
            cp.wait()

    any_spec = pl.BlockSpec(memory_space=pl.ANY)
    return pl.pallas_call(
        body, name=name, out_shape=out_shape,
        in_specs=[any_spec] * n, out_specs=tuple([any_spec] * n),
        scratch_shapes=[pltpu.SemaphoreType.DMA((n, NDEV - 1)), pltpu.SemaphoreType.DMA((n, NDEV - 1)),
                        pltpu.SemaphoreType.DMA((n,))],
    )(*arrays)


def _pair_sum_call(parts, name):
    n = len(parts)

    def body(*refs):
        in_r, out_r, mine_r, land_r = (refs[j * n:(j + 1) * n] for j in range(4))
        send_sems, recv_sems, local_sems = refs[4 * n:]
        core = lax.axis_index("c")
        sibling = (lax.axis_index("x"), lax.axis_index("y"), 1 - core)
        local = [pltpu.make_async_copy(in_r[k].at[core], mine_r[k], local_sems.at[k]) for k in range(n)]
        swaps = [pltpu.make_async_remote_copy(src_ref=in_r[k].at[1 - core], dst_ref=land_r[k], send_sem=send_sems.at[k],
                                              recv_sem=recv_sems.at[k], device_id=sibling, device_id_type=MESH)
                 for k in range(n)]
        for cp in local + swaps:
            cp.start()
        for k in range(n):
            local[k].wait()
            swaps[k].wait()
            out_r[k][...] = (mine_r[k][...].astype(F32) + land_r[k][...].astype(F32)).astype(out_r[k].dtype)

    halves = [jax.ShapeDtypeStruct(a.shape[1:], a.dtype) for a in parts]
    return pl.pallas_call(
        body, name=name, out_shape=tuple(halves), in_specs=[pl.BlockSpec(memory_space=pl.ANY)] * n,
        scratch_shapes=[pltpu.VMEM(h.shape, h.dtype) for h in halves] * 2 + [pltpu.SemaphoreType.DMA((n,))] * 3,
        compiler_params=pltpu.CompilerParams(vmem_limit_bytes=VMEM_LIMIT),
    )(*parts)


def _mod_call(c_all, w_ada, b_cols):
    def body(c_ref, w_ref, b_ref, o_ref):
        o_ref[...] = _dot(_silu(c_ref[...]), w_ref[...]) + b_ref[...]

    return pl.pallas_call(body, name="mod_fwd",
                          out_shape=jax.ShapeDtypeStruct((c_all.shape[0], w_ada.shape[1]), F32))(c_all, w_ada, b_cols)


def _wada_grad_call(c_all, dmod_cols):
    def body(c_ref, d_ref, o_ref):
        o_ref[...] = _dot_tn(_silu(c_ref[...]), d_ref[...])

    return pl.pallas_call(body, name="w_ada_grad",
                          out_shape=jax.ShapeDtypeStruct((c_all.shape[1], dmod_cols.shape[1]), F32))(c_all, dmod_cols)


def _full(shape):
    nd = len(shape)
    return pl.BlockSpec(shape, lambda *_: (0,) * nd)


def _in_proj_call(x2, shift, scale, g_pre, w_in, qg, kg, cos, sin, bd, T):
    R = x2.shape[0]
    TT = min(SHIFT_TILE, T)
    tpe = T // TT

    def body(x_ref, sh_ref, sc_ref, gp_ref, w_ref, qg_ref, kg_ref, cos_ref, sin_ref, bd_ref,
             hb_ref, qr_ref, kpad_ref, vpad_ref, qraw_ref, kraw_ref, gatt_ref, rin_ref, grw_ref):
        h = _pre_fn(x_ref[...], sh_ref[0], sc_ref[0], gp_ref[...])
        hb = h.astype(MXU_DTYPE)
        hb_ref[...] = hb

        def proj(c0, c1):
            return _dot_nt(hb, w_ref[c0:c1, :])

        q = proj(C_Q, C_K)
        k = proj(C_K, C_V)
        v = proj(C_V, C_GA)
        gatt_ref[...] = proj(C_GA, C_RIN)
        rin_ref[...] = proj(C_RIN, C_GRW)
        grw_ref[...] = proj(C_GRW, C_END)
        qraw_ref[...] = q
        kraw_ref[...] = k
        cos, sin, bd = cos_ref[...], sin_ref[...], bd_ref[...]
        qr = _qk_fn(q, qg_ref[...], jnp.tile(cos, (1, 4)), jnp.tile(sin, (1, 4)), bd, ATT_SCALE, False)
        qr_ref[...] = qr.astype(MXU_DTYPE)
        kr = _qk_fn(k, kg_ref[...], cos, sin, bd, 1.0, False)
        left = lax.broadcasted_iota(jnp.int32, (1, KV_W), 1) < HEAD_DIM
        for ref, val in ((kpad_ref, kr), (vpad_ref, v)):
            h0l = jnp.where(left, val, 0.0)
            h1r = jnp.where(left, 0.0, val)
            ref[0] = h0l.astype(MXU_DTYPE)
            ref[1] = pltpu.roll(h0l, HEAD_DIM, 1).astype(MXU_DTYPE)
            ref[2] = pltpu.roll(h1r, HEAD_DIM, 1).astype(MXU_DTYPE)
            ref[3] = h1r.astype(MXU_DTYPE)

    row = lambda w: pl.BlockSpec((TT, w), lambda i: (i, 0))
    per_ex = pl.BlockSpec((1, 1, D_MODEL), lambda i: (i // tpe, 0, 0))
    tab = pl.BlockSpec((TT, KV_W), lambda i: (i % tpe, 0))
    pad = pl.BlockSpec((4, TT, KV_W), lambda i: (0, i, 0))
    sds = jax.ShapeDtypeStruct
    return pl.pallas_call(
        body, name="in_proj", grid=(R // TT,),
        in_specs=[row(D_MODEL), per_ex, per_ex, _full((1, D_MODEL)), _full(w_in.shape), _full((1, ATT_W)),
                  _full((1, KV_W)), tab, tab, _full((256, 256))],
        out_specs=(row(D_MODEL), row(ATT_W), pad, pad, row(ATT_W), row(KV_W), row(ATT_W), row(SHIFT_W), row(RWKV_W)),
        out_shape=(sds((R, D_MODEL), MXU_DTYPE), sds((R, ATT_W), MXU_DTYPE), sds((4, R, KV_W), MXU_DTYPE),
                   sds((4, R, KV_W), MXU_DTYPE), sds((R, ATT_W), F32), sds((R, KV_W), F32), sds((R, ATT_W), F32),
                   sds((R, SHIFT_W), F32), sds((R, RWKV_W), F32)),
        compiler_params=_cp(("arbitrary",)),
    )(x2, shift, scale, g_pre, w_in, qg, kg, cos, sin, bd)


def _softmax_parts(s):
    e = jnp.exp(s - jnp.max(s, axis=1, keepdims=True))
    return e, 1.0 / jnp.sum(e, axis=1, keepdims=True)


def _att_specs(T, TQ):
    nq = T // TQ
    qspec = pl.BlockSpec((TQ, KV_W), lambda b, p, i: (b * nq + i, p))
    side = lambda s: pl.BlockSpec((None, T, KV_W), lambda b, p, i: (2 * (p // 2) + s, b, 0))
    return nq, qspec, side


def _att_fwd_call(qr, kpad, vpad, B, T):
    TQ = min(ATT_TILE_FWD, T)
    nq, qspec, side = _att_specs(T, TQ)

    def body(q_ref, kl_ref, kr_ref, vl_ref, vr_ref, o_ref):
        q = q_ref[...]
        ea, inv_a = _softmax_parts(_dot_nt(q, kl_ref[...]))
        eb, inv_b = _softmax_parts(_dot_nt(q, kr_ref[...]))
        o_ref[...] = _dot(ea, vl_ref[...]) * inv_a + _dot(eb, vr_ref[...]) * inv_b

    return pl.pallas_call(
        body, name="att_fwd", grid=(B, 4, nq),
        in_specs=[qspec, side(0), side(1), side(0), side(1)], out_specs=qspec,
        out_shape=jax.ShapeDtypeStruct((B * T, ATT_W), F32),
        compiler_params=_cp(("arbitrary",) * 3),
    )(qr, kpad, kpad, vpad, vpad)


def _att_bwd_call(qr, kpad, vpad, d_o, B, T):
    TQ = min(ATT_TILE_BWD, T)
    nq, qspec, side = _att_specs(T, TQ)

    def body(q_ref, kl_ref, kr_ref, vl_ref, vr_ref, do_ref, dq_ref, dk_ref, dv_ref):
        i = pl.program_id(2)
        q, do = q_ref[...], do_ref[...]
        left = lax.broadcasted_iota(jnp.int32, (1, KV_W), 1) < HEAD_DIM
        dq = jnp.zeros((TQ, KV_W), F32)
        dk = jnp.zeros((T, KV_W), F32)
        dv = jnp.zeros((T, KV_W), F32)
        for k_ref, v_ref, mask in ((kl_ref, vl_ref, left), (kr_ref, vr_ref, jnp.logical_not(left))):
            kk, vv = k_ref[...], v_ref[...]
            e, inv = _softmax_parts(_dot_nt(q, kk))
            dp = _dot_nt(do, vv)
            ds = e * (dp - inv * jnp.sum(e * dp, axis=1, keepdims=True))
            dq = dq + _dot(ds, kk) * inv
            dk = dk + _dot_tn(ds, jnp.where(mask, q * inv, 0.0))
            dv = dv + _dot_tn(e, jnp.where(mask, do * inv, 0.0))
        dq_ref[...] = dq

        @pl.when(i == 0)
        def _():
            dk_ref[...] = dk
            dv_ref[...] = dv

        @pl.when(i > 0)
        def _():
            dk_ref[...] += dk
            dv_ref[...] += dv

    acc = pl.BlockSpec((None, T, KV_W), lambda b, p, i: (p, b, 0))
    sds = jax.ShapeDtypeStruct
    return pl.pallas_call(
        body, name="att_bwd", grid=(B, 4, nq),
        in_specs=[qspec, side(0), side(1), side(0), side(1), qspec], out_specs=(qspec, acc, acc),
        out_shape=(sds((B * T, ATT_W), F32), sds((4, B * T, KV_W), F32), sds((4, B * T, KV_W), F32)),
        compiler_params=_cp(("arbitrary",) * 3),
    )(qr, kpad, kpad, vpad, vpad, d_o)


def _shift_specs(R, T, TT, width):
    tpe = T // TT
    nb8 = R // 8
    cur = pl.BlockSpec((TT, width), lambda i: (i, 0))
    prev = pl.BlockSpec((8, width), lambda i: (jnp.maximum(i * (TT // 8) - 1, 0), 0))
    nxt = pl.BlockSpec((8, width), lambda i: (jnp.minimum((i + 1) * (TT // 8), nb8 - 1), 0))
    return tpe, cur, prev, nxt


def _neighbours(cur, prev8, next8, i, tpe, TT):
    rows = lax.broadcasted_iota(jnp.int32, (TT, 1), 0)
    first = jnp.where(i % tpe == 0, 0.0, 1.0)
    last = jnp.where(i % tpe == tpe - 1, 0.0, 1.0)
    before = jnp.where(rows == 0, prev8[7:8, :] * first, pltpu.roll(cur, 1, 0))
    after = jnp.where(rows == TT - 1, next8[0:1, :] * last, pltpu.roll(cur, TT - 1, 0))
    return before, after


def _shift_fwd_call(x, taps, T):
    R, width = x.shape
    TT = min(SHIFT_TILE, T)
    tpe, cur, prev, nxt = _shift_specs(R, T, TT, width)

    def body(x_ref, p_ref, n_ref, t_ref, o_ref, vh_ref):
        xc = x_ref[...]
        before, after = _neighbours(xc, p_ref[...], n_ref[...], pl.program_id(0), tpe, TT)
        out = t_ref[0:1, :] * before + t_ref[1:2, :] * xc + t_ref[2:3, :] * after
        o_ref[...] = out
        left = lax.broadcasted_iota(jnp.int32, (1, KV_W), 1) < HEAD_DIM
        for p in range(RWKV_W // KV_W):
            pair = out[:, 2 * RWKV_W + p * KV_W:2 * RWKV_W + (p + 1) * KV_W]
            vh_ref[:, 2 * p * KV_W:(2 * p + 1) * KV_W] = jnp.where(left, pair, 0.0)
            vh_ref[:, (2 * p + 1) * KV_W:(2 * p + 2) * KV_W] = jnp.where(left, pltpu.roll(pair, HEAD_DIM, 1), 0.0)

    return pl.pallas_call(
        body, name="shift_fwd", grid=(R // TT,), in_specs=[cur, prev, nxt, _full(taps.shape)],
        out_specs=(cur, pl.BlockSpec((TT, 2 * RWKV_W), lambda i: (i, 0))),
        out_shape=(jax.ShapeDtypeStruct((R, width), F32), jax.ShapeDtypeStruct((R, 2 * RWKV_W), F32)),
        compiler_params=_cp(("arbitrary",)),
    )(x, x, x, taps)


def _shift_bwd_call(x, d, taps, T):
    R, width = x.shape
    TT = min(SHIFT_TILE, T)
    tpe, cur, prev, nxt = _shift_specs(R, T, TT, width)

    def body(x_ref, xp_ref, xn_ref, d_ref, dp_ref, dn_ref, t_ref, dx_ref, dt_ref):
        i = pl.program_id(0)
        xc, dc = x_ref[...], d_ref[...]
        d_before, d_after = _neighbours(dc, dp_ref[...], dn_ref[...], i, tpe, TT)
        dx_ref[...] = t_ref[2:3, :] * d_before + t_ref[1:2, :] * dc + t_ref[0:1, :] * d_after
        x_before, x_after = _neighbours(xc, xp_ref[...], xn_ref[...], i, tpe, TT)
        @pl.when(i == 0)
        def _():
            dt_ref[...] = jnp.zeros_like(dt_ref)

        for j, xs in enumerate((x_before, xc, x_after)):
            dt_ref[j:j + 1, :] += jnp.sum(dc * xs, axis=0, keepdims=True)

    return pl.pallas_call(
        body, name="shift_bwd", grid=(R // TT,),
        in_specs=[cur, prev, nxt, cur, prev, nxt, _full(taps.shape)], out_specs=(cur, _full((8, width))),
        out_shape=(jax.ShapeDtypeStruct((R, width), F32), jax.ShapeDtypeStruct((8, width), F32)),
        compiler_params=_cp(("arbitrary",)),
    )(x, x, x, d, d, d, taps)


def _lora_in(wa):
    lane = lax.broadcasted_iota(jnp.int32, (1, LORA_W), 1)
    return jnp.where(lane < LORA_W // 2, jnp.tanh(wa), wa)


def _rwkv_prep_call(shifted, wup, aup, w0, a0, k_k, k_a, bd, T):
    R = shifted.shape[0]
    TT = min(SHIFT_TILE, T)

    def body(k_ref, wa_ref, wup_ref, aup_ref, w0_ref, a0_ref, kk_ref, ka_ref, bd_ref, w_o, kt_o, akk_o, kk_o):
        twa = _lora_in(wa_ref[...])
        pre = [_dot(twa, m_ref[z]) for m_ref in (wup_ref, aup_ref) for z in range(2)]
        outs = _rwkv_pw(k_ref[...], pre[0], pre[1], pre[2], pre[3], w0_ref[...], a0_ref[...], kk_ref[...],
                        ka_ref[...], bd_ref[...], False)
        w_o[0], w_o[1], kt_o[0], kt_o[1], akk_o[0], akk_o[1] = outs[:6]
        kk_o[...] = outs[6]

    col = lambda c, w: pl.BlockSpec((TT, w), lambda i: (i, c))
    two = pl.BlockSpec((2, TT, RWKV_W), lambda i: (0, i, 0))
    sds = jax.ShapeDtypeStruct
    return pl.pallas_call(
        body, name="rwkv_prep", grid=(R // TT,),
        in_specs=[col(1, RWKV_W), col(3 * RWKV_W // LORA_W, LORA_W), _full(wup.shape), _full(aup.shape),
                  _full((2, RWKV_W)), _full((2, RWKV_W)), _full((1, RWKV_W)), _full((1, RWKV_W)), _full((256, 256))],
        out_specs=(two, two, two, col(0, RWKV_W)),
        out_shape=(sds((2, R, RWKV_W), F32),) * 3 + (sds((R, RWKV_W), F32),),
        compiler_params=_cp(("arbitrary",)),
    )(shifted, shifted, wup, aup, w0, a0, k_k, k_a, bd)


def _rwkv_prep_bwd_call(shifted, cts, wup, aup, w0, a0, k_k, k_a, bd, T):
    R = shifted.shape[0]
    TT = min(ROW_TILE, T)

    def body(k_ref, wa_ref, dw0, dkt0, dakk0, dkk0, dr0, dv0, dw1, dkt1, dakk1, dkk1, dr1, dv1, dr2_ref, dv2_ref, dkts_ref,
             wup_ref, aup_ref, w0_ref, a0_ref, kk_ref, ka_ref, bd_ref,
             dsh_ref, gwup_ref, gaup_ref, gw0_ref, ga0_ref, gkk_ref, gka_ref):
        dw_ref, dkt_ref, dakk_ref, dkk_ref, dr_ref, dv_ref = ((dw0, dw1), (dkt0, dkt1), (dakk0, dakk1), (dkk0, dkk1),
                                                              (dr0, dr1), (dv0, dv1))
        i = pl.program_id(0)
        wa = wa_ref[...]
        twa = _lora_in(wa)
        pre = [_dot(twa, m_ref[z]) for m_ref in (wup_ref, aup_ref) for z in range(2)]
        fn = functools.partial(_rwkv_pw, bd=bd_ref[...], diff=True)
        _, vjp = jax.vjp(fn, k_ref[...], pre[0], pre[1], pre[2], pre[3], w0_ref[...], a0_ref[...], kk_ref[...],
                         ka_ref[...])
        dkts = dkts_ref[...]
        dk, dpw0, dpw1, dpa0, dpa1, gw0, ga0, gkk, gka = vjp(
            (dw_ref[0][...], dw_ref[1][...], dkt_ref[0][...] + dkts, dkt_ref[1][...] + dkts, dakk_ref[0][...],
             dakk_ref[1][...], dkk_ref[0][...] + dkk_ref[1][...]))
        dtwa = (_dot_nt(dpw0, wup_ref[0]) + _dot_nt(dpw1, wup_ref[1]) + _dot_nt(dpa0, aup_ref[0])
                + _dot_nt(dpa1, aup_ref[1]))
        lane = lax.broadcasted_iota(jnp.int32, (1, LORA_W), 1)
        dsh_ref[:, 0:RWKV_W] = dr_ref[0][...] + dr_ref[1][...] + dr2_ref[...]
        dsh_ref[:, RWKV_W:2 * RWKV_W] = dk
        dsh_ref[:, 2 * RWKV_W:3 * RWKV_W] = dv_ref[0][...] + dv_ref[1][...] + dv2_ref[...]
        dsh_ref[:, 3 * RWKV_W:] = jnp.where(lane < LORA_W // 2, dtwa * (1.0 - twa * twa), dtwa)
        acc = ((gwup_ref.at[0], _dot_tn(twa, dpw0)), (gwup_ref.at[1], _dot_tn(twa, dpw1)),
               (gaup_ref.at[0], _dot_tn(twa, dpa0)), (gaup_ref.at[1], _dot_tn(twa, dpa1)),
               (gw0_ref, gw0), (ga0_ref, ga0), (gkk_ref, gkk), (gka_ref, gka))

        @pl.when(i == 0)
        def _():
            for ref, val in acc:
                ref[...] = val

        @pl.when(i > 0)
        def _():
            for ref, val in acc:
                ref[...] += val

    col = lambda c, w: pl.BlockSpec((TT, w), lambda i: (i, c))
    one = col(0, RWKV_W)
    sds = jax.ShapeDtypeStruct
    return pl.pallas_call(
        body, name="rwkv_prep_bwd", grid=(R // TT,),
        in_specs=[col(1, RWKV_W), col(3 * RWKV_W // LORA_W, LORA_W)] + [one] * 15 + [
                  _full(wup.shape), _full(aup.shape), _full((2, RWKV_W)), _full((2, RWKV_W)), _full((1, RWKV_W)),
                  _full((1, RWKV_W)), _full((256, 256))],
        out_specs=(pl.BlockSpec((TT, SHIFT_W), lambda i: (i, 0)), _full(wup.shape), _full(aup.shape),
                   _full((2, RWKV_W)), _full((2, RWKV_W)), _full((1, RWKV_W)), _full((1, RWKV_W))),
        out_shape=(sds((R, SHIFT_W), F32), sds(wup.shape, F32), sds(aup.shape, F32), sds((2, RWKV_W), F32),
                   sds((2, RWKV_W), F32), sds((1, RWKV_W), F32), sds((1, RWKV_W), F32)),
        compiler_params=_cp(("arbitrary",)),
    )(shifted, shifted, *cts, wup, aup, w0, a0, k_k, k_a, bd)


def _col_lhs(row, eye_b):
    return eye_b * row.astype(MXU_DTYPE)


def _colsum(x):
    return jnp.sum(x, axis=0, keepdims=True)


def _stacked_segsum(tiles, bd):
    res = _seg_dot(jnp.concatenate(tiles, axis=0), bd)
    return [res[j * HEAD_DIM:(j + 1) * HEAD_DIM] for j in range(len(tiles))]


def _scan_specs(B, T, C, nC):
    def blk(z, col, rev):
        idx = (lambda g: (z, 0, nC - 1 - g, col)) if rev else (lambda g: (z, 0, g, col))
        return pl.BlockSpec((None, B, C, RWKV_W), idx)

    def blk3(col, rev):
        idx = (lambda g: (0, nC - 1 - g, col)) if rev else (lambda g: (0, g, col))
        return pl.BlockSpec((B, C, RWKV_W), idx)

    return blk, blk3


def _scan_fwd_call(w, kt, akk, kk, shifted, eye_b, eye_f, bd, B, T):
    C = min(SCAN_CHUNK, T)
    nC = T // C
    blk, blk3 = _scan_specs(B, T, C, nC)

    def body(w0, kt0, akk0, kk0, v0, r0, w1, kt1, akk1, kk1, v1, r1, eb_ref, ef_ref, bd_ref, y0, y1, st, S):
        @pl.when(pl.program_id(0) == 0)
        def _():
            S[...] = jnp.zeros_like(S)

        st[0] = S[...].astype(MXU_DTYPE)
        dirs = ((w0, kt0, akk0, kk0, v0, r0, y0), (w1, kt1, akk1, kk1, v1, r1, y1))

        def step(s, carry):
            for z in range(2):
                row = s if z == 0 else C - 1 - s
                prev = jnp.maximum(s - 1, 0) if z == 0 else jnp.minimum(C - s, C - 1)
                wr, ktr, akkr, kkr, vr, rr, yr = dirs[z]
                tiles = []
                for b in range(B):
                    Sb = st[s, z * B + b]
                    tiles += [Sb * kkr[b, pl.ds(row, 1), :].astype(MXU_DTYPE),
                              _col_lhs(vr[b, pl.ds(row, 1), :], eb_ref[...]),
                              Sb * rr[b, pl.ds(prev, 1), :].astype(MXU_DTYPE)]
                res = _stacked_segsum(tiles, bd_ref[...])
                for b in range(B):
                    c = z * B + b
                    sab, vb, yb = res[3 * b:3 * b + 3]
                    ld = lambda ref: ref[b, pl.ds(row, 1), :]
                    Sn = S[c] * ld(wr) - sab * ld(akkr) + vb * ld(ktr)
                    S[c] = Sn
                    st[s + 1, c] = Sn.astype(MXU_DTYPE)
                    yr[b, pl.ds(prev, 1), :] = _colsum(ef_ref[...] * yb)
            return carry

        lax.fori_loop(0, C, step, 0, unroll=SCAN_UNROLL)
        for z in range(2):
            last = C - 1 if z == 0 else 0
            rr, yr = dirs[z][5], dirs[z][6]
            res = _stacked_segsum([st[C, z * B + b] * rr[b, last:last + 1, :].astype(MXU_DTYPE) for b in range(B)],
                                  bd_ref[...])
            for b in range(B):
                yr[b, last:last + 1, :] = _colsum(ef_ref[...] * res[b])

    ins, specs = [], []
    for z, rev in ((0, False), (1, True)):
        ins += [w, kt, akk, kk, shifted, shifted]
        specs += [blk(z, 0, rev), blk(z, 0, rev), blk(z, 0, rev), blk3(0, rev), blk3(2, rev), blk3(0, rev)]
    sds = jax.ShapeDtypeStruct
    return pl.pallas_call(
        body, name="scan_fwd", grid=(nC,),
        in_specs=specs + [_full((HEAD_DIM, RWKV_W)), _full((HEAD_DIM, RWKV_W)), _full((256, 256))],
        out_specs=(blk3(0, False), blk3(0, True),
                   pl.BlockSpec((None, C + 1, 2 * B, HEAD_DIM, RWKV_W), lambda g: (g, 0, 0, 0, 0))),
        out_shape=(sds((B, T, RWKV_W), F32), sds((B, T, RWKV_W), F32),
                   sds((nC, C + 1, 2 * B, HEAD_DIM, RWKV_W), MXU_DTYPE)),
        scratch_shapes=[pltpu.VMEM((2 * B, HEAD_DIM, RWKV_W), F32)],
        compiler_params=_cp(("arbitrary",)),
    )(*ins, eye_b, eye_f, bd)


def _scan_bwd_call(w, kt, akk, kk, shifted, v_heads, dys, st, eye_b, eye_f, bd, B, T):
    C = min(SCAN_CHUNK, T)
    nC = T // C
    blk, blk3 = _scan_specs(B, T, C, nC)
    nin = 7

    def body(*refs):
        d0, d1 = refs[:nin], refs[nin:2 * nin]
        st_ref, eb_ref, ef_ref, sel_ref, hm_ref, bd_ref = refs[2 * nin:2 * nin + 6]
        o0, o1 = refs[2 * nin + 6:2 * nin + 12], refs[2 * nin + 12:2 * nin + 18]
        COL, DYC, G = refs[2 * nin + 18:]

        @pl.when(pl.program_id(0) == 0)
        def _():
            G[...] = jnp.zeros_like(G)

        dirs = (d0 + (o0,), d1 + (o1,))

        def column_operands(s, z):
            row = s if z == 0 else C - 1 - s
            _, _, _, kkr, _, _, dyr, _ = dirs[z]
            tiles = []
            for b in range(B):
                tiles += [st_ref[s, z * B + b] * kkr[b, pl.ds(row, 1), :].astype(MXU_DTYPE),
                          _col_lhs(dyr[b, pl.ds(row, 1), :], eb_ref[...])]
            return tiles

        def keep_columns(res, z):
            for b in range(B):
                for k in range(2):
                    COL[k, z * B + b] = res[2 * b + k].astype(MXU_DTYPE)
                DYC[z * B + b] = res[2 * b + 1]

        for z in range(2):
            keep_columns(_stacked_segsum(column_operands(C - 1, z), bd_ref[...]), z)

        def bwd(it, carry):
            s = C - 1 - it
            for z in range(2):
                row = s if z == 0 else C - 1 - s
                wr, ktr, akkr, kkr, vr, rr, dyr, (dw_o, dkt_o, dakk_o, dkk_o, dr_o, dv_o) = dirs[z]
                tiles, Gcs = [], []
                for b in range(B):
                    c = z * B + b
                    Gc = G[c] + DYC[c] * rr[b, pl.ds(row, 1), :]
                    Gb = Gc.astype(MXU_DTYPE)
                    Gcs.append((Gc, Gb))
                    tiles += [Gb * akkr[b, pl.ds(row, 1), :].astype(MXU_DTYPE),
                              Gb * ktr[b, pl.ds(row, 1), :].astype(MXU_DTYPE)]
                res = _stacked_segsum(tiles + column_operands(jnp.maximum(s - 1, 0), z), bd_ref[...])
                for b in range(B):
                    c = z * B + b
                    Gc, Gb = Gcs[b]
                    gab, dvb = res[2 * b], res[2 * b + 1]
                    ld = lambda ref: ref[b, pl.ds(row, 1), :]
                    G[c] = Gc * ld(wr) - gab * ld(kkr)
                    Sb = st_ref[s, c]
                    prods = jnp.concatenate([Gb, st_ref[s + 1, c] * COL[1, c], Gb * Sb, Gb * COL[0, c],
                                             gab.astype(MXU_DTYPE) * Sb], axis=0)
                    v_rows = jnp.concatenate([vr[b, pl.ds(row, 1)][0], jnp.zeros((8, 3 * HEAD_DIM), F32)], axis=1)
                    lhs = jnp.concatenate([sel_ref[...], v_rows], axis=0).astype(MXU_DTYPE)
                    sums = jnp.dot(lhs, prods, preferred_element_type=F32)
                    for k, (ref, sign) in enumerate(((dr_o, 1.0), (dw_o, 1.0), (dakk_o, -1.0), (dkk_o, -1.0))):
                        ref[b, pl.ds(row, 1), :] = sign * sums[k:k + 1, :]
                    dkt_o[b, pl.ds(row, 1), :] = _colsum(sums[8:16] * hm_ref[...])
                    dv_o[b, pl.ds(row, 1), :] = _colsum(ef_ref[...] * dvb)
                keep_columns(res[2 * B:], z)
            return carry

        lax.fori_loop(0, C, bwd, 0, unroll=SCAN_UNROLL)

    ins, specs = [], []
    for z, rev in ((0, True), (1, False)):
        heads = pl.BlockSpec((B, C) + v_heads.shape[2:], (lambda g: (0, nC - 1 - g, 0, 0)) if rev else (lambda g: (0, g, 0, 0)))
        ins += [w, kt, akk, kk, v_heads, shifted, dys]
        specs += [blk(z, 0, rev), blk(z, 0, rev), blk(z, 0, rev), blk3(0, rev), heads, blk3(0, rev), blk3(0, rev)]
    sel = (jnp.arange(8)[:, None] + 1 == (jnp.arange(5 * HEAD_DIM) // HEAD_DIM)[None, :]).astype(F32)
    head_rows = (jnp.arange(RWKV_W // HEAD_DIM)[:, None] == (jnp.arange(RWKV_W) // HEAD_DIM)[None, :]).astype(F32)
    ins += [st, eye_b, eye_f, sel, head_rows, bd]
    specs += [pl.BlockSpec((None, C + 1, 2 * B, HEAD_DIM, RWKV_W), lambda g: (nC - 1 - g, 0, 0, 0, 0)),
              _full((HEAD_DIM, RWKV_W)), _full((HEAD_DIM, RWKV_W)), _full(sel.shape), _full(head_rows.shape),
              _full((256, 256))]
    sds = jax.ShapeDtypeStruct
    out_specs = tuple(blk3(0, True) for _ in range(6)) + tuple(blk3(0, False) for _ in range(6))
    res = pl.pallas_call(
        body, name="scan_bwd", grid=(nC,), in_specs=specs, out_specs=out_specs,
        out_shape=tuple(sds((B, T, RWKV_W), F32) for _ in range(12)),
        scratch_shapes=[pltpu.VMEM((2, 2 * B, HEAD_DIM, RWKV_W), MXU_DTYPE), pltpu.VMEM((2 * B, HEAD_DIM, RWKV_W), F32),
                        pltpu.VMEM((2 * B, HEAD_DIM, RWKV_W), F32)],
        compiler_params=_cp(("arbitrary",)),
    )(*ins)
    return list(res)


def _out_head_call(x2, tgt2, gate, y_att, g_att, y0, y1, shifted, kt, g_rw, w_out, g_post, gn_w, gn_b, r_k, bd, T):
    R = x2.shape[0]
    TT = min(ROW_TILE, T)
    tpe = T // TT

    def body(x_ref, t_ref, gate_ref, ya_ref, ga_ref, y0_ref, y1_ref, r_ref, v_ref, kt_ref, grw_ref, w_ref, gp_ref,
             gnw_ref, gnb_ref, rk_ref, bd_ref,
             loss_o, dy_o, dya_o, dga_o, dys_o, dr_o, dv_o, dkts_o, dgrw_o, dgate_o, gw_o, ggp_o, ggnw_o, ggnb_o, grk_o):
        i = pl.program_id(0)
        bd = bd_ref[...]
        mix = functools.partial(_mix_fn, bd=bd, diff=True)
        (ma, mr), mix_vjp = jax.vjp(mix, ya_ref[...], ga_ref[...], y0_ref[...] + y1_ref[...], r_ref[...], v_ref[...],
                                    kt_ref[0] + kt_ref[1], grw_ref[...], gnw_ref[...], gnb_ref[...], rk_ref[...])
        out = _dot(ma, w_ref[0:ATT_W, :]) + _dot(mr, w_ref[ATT_W:, :])
        loss, loss_vjp = jax.vjp(_loss_fn, out, x_ref[...], t_ref[...], gate_ref[0], gp_ref[...])
        d_out, dy, _, dgate, dgp = loss_vjp(jnp.ones((1, 1), F32))
        dy_o[...] = dy
        dma = _dot_nt(d_out, w_ref[0:ATT_W, :])
        dmr = _dot_nt(d_out, w_ref[ATT_W:, :])
        dya_o[...], dga_o[...], dys_o[...], dr_o[...], dv_o[...], dkts_o[...], dgrw_o[...], dgnw, dgnb, drk = \
            mix_vjp((dma, dmr))
        gw = jnp.concatenate([_dot_tn(ma, d_out), _dot_tn(mr, d_out)], axis=0)
        acc = ((loss_o, jnp.broadcast_to(loss, (8, 128))), (gw_o, gw), (ggp_o, dgp), (ggnw_o, dgnw), (ggnb_o, dgnb),
               (grk_o, drk))

        @pl.when(i == 0)
        def _():
            for ref, val in acc:
                ref[...] = val

        @pl.when(i > 0)
        def _():
            for ref, val in acc:
                ref[...] += val

        @pl.when(i % tpe == 0)
        def _():
            dgate_o[0] = dgate

        @pl.when(i % tpe > 0)
        def _():
            dgate_o[0] += dgate

    row = lambda w, c=0: pl.BlockSpec((TT, w), lambda i: (i, c))
    two = pl.BlockSpec((2, TT, RWKV_W), lambda i: (0, i, 0))
    per_ex = pl.BlockSpec((1, 1, D_MODEL), lambda i: (i // tpe, 0, 0))
    sds = jax.ShapeDtypeStruct
    r512 = sds((R, RWKV_W), F32)
    return pl.pallas_call(
        body, name="out_head", grid=(R // TT,),
        in_specs=[row(D_MODEL), row(D_MODEL), per_ex, row(ATT_W), row(ATT_W), row(RWKV_W), row(RWKV_W), row(RWKV_W, 0),
                  row(RWKV_W, 2), two,
                  row(RWKV_W), _full(w_out.shape), _full((1, D_MODEL)), _full((1, RWKV_W)), _full((1, RWKV_W)),
                  _full((1, RWKV_W)), _full((256, 256))],
        out_specs=(_full((8, 128)), row(D_MODEL), row(ATT_W), row(ATT_W), row(RWKV_W), row(RWKV_W), row(RWKV_W),
                   row(RWKV_W), row(RWKV_W), per_ex, _full((D_MODEL, D_MODEL)), _full((1, D_MODEL)), _full((1, RWKV_W)),
                   _full((1, RWKV_W)), _full((1, RWKV_W))),
        out_shape=(sds((8, 128), F32), sds((R, D_MODEL), F32), r512, r512, r512, r512, r512, r512, r512,
                   sds((R // T, 1, D_MODEL), F32), sds((D_MODEL, D_MODEL), F32), sds((1, D_MODEL), F32),
                   sds((1, RWKV_W), F32), sds((1, RWKV_W), F32), sds((1, RWKV_W), F32)),
        compiler_params=_cp(("arbitrary",)),
    )(x2, tgt2, gate, y_att, g_att, y0, y1, shifted, shifted, kt, g_rw, w_out, g_post, gn_w, gn_b, r_k, bd)


def _in_proj_bwd_call(x2, dy, shift, scale, g_pre, w_in, qg, kg, cos, sin, bd, q_raw, k_raw, dqr, dkp, dvp,
                      d_gatt, d_rin, d_grw, T):
    R = x2.shape[0]
    TT = min(ROW_TILE, T)
    tpe = T // TT

    def body(x_ref, dy_ref, sh_ref, sc_ref, gp_ref, w_ref, qg_ref, kg_ref, cos_ref, sin_ref, bd_ref, q_ref, k_ref,
             dqr_ref, dkp_ref, dvp_ref, dga_ref, drin_ref, dgrw_ref,
             dx_o, dproj_o, dsh_o, dsc_o, ggp_o, gqg_o, gkg_o):
        i = pl.program_id(0)
        cos, sin, bd = cos_ref[...], sin_ref[...], bd_ref[...]
        left = lax.broadcasted_iota(jnp.int32, (1, KV_W), 1) < HEAD_DIM

        def kv_grad(ref):
            a = ref[0] + ref[1]
            b = ref[2] + ref[3]
            return jnp.where(left, a + pltpu.roll(a, HEAD_DIM, 1), b + pltpu.roll(b, HEAD_DIM, 1))

        qfn = functools.partial(_qk_fn, cos=jnp.tile(cos, (1, 4)), sin=jnp.tile(sin, (1, 4)), bd=bd, scale=ATT_SCALE,
                                diff=True)
        _, q_vjp = jax.vjp(qfn, q_ref[...], qg_ref[...])
        dq, gqg = q_vjp(dqr_ref[...])
        kfn = functools.partial(_qk_fn, cos=cos, sin=sin, bd=bd, scale=1.0, diff=True)
        _, k_vjp = jax.vjp(kfn, k_ref[...], kg_ref[...])
        dk, gkg = k_vjp(kv_grad(dkp_ref))
        pieces = ((C_Q, C_K, dq), (C_K, C_V, dk), (C_V, C_GA, kv_grad(dvp_ref)), (C_GA, C_RIN, dga_ref[...]),
                  (C_RIN, C_GRW, drin_ref[...]), (C_GRW, C_END, dgrw_ref[...]))
        dh = jnp.zeros((TT, D_MODEL), F32)
        for c0, c1, val in pieces:
            vb = val.astype(MXU_DTYPE)
            dproj_o[:, c0:c1] = vb
            dh = dh + _dot(vb, w_ref[c0:c1, :])
        _, pre_vjp = jax.vjp(_pre_fn, x_ref[...], sh_ref[0], sc_ref[0], gp_ref[...])
        dx, dsh, dsc, ggp = pre_vjp(dh)
        dx_o[...] = dx + dy_ref[...]
        acc = ((ggp_o, ggp), (gqg_o, gqg), (gkg_o, gkg))

        @pl.when(i == 0)
        def _():
            for ref, val in acc:
                ref[...] = val

        @pl.when(i > 0)
        def _():
            for ref, val in acc:
                ref[...] += val

        @pl.when(i % tpe == 0)
        def _():
            dsh_o[0] = dsh
            dsc_o[0] = dsc

        @pl.when(i % tpe > 0)
        def _():
            dsh_o[0] += dsh
            dsc_o[0] += dsc

    row = lambda w: pl.BlockSpec((TT, w), lambda i: (i, 0))
    per_ex = pl.BlockSpec((1, 1, D_MODEL), lambda i: (i // tpe, 0, 0))
    tab = pl.BlockSpec((TT, KV_W), lambda i: (i % tpe, 0))
    pad = pl.BlockSpec((4, TT, KV_W), lambda i: (0, i, 0))
    sds = jax.ShapeDtypeStruct
    nb = R // T
    return pl.pallas_call(
        body, name="in_proj_bwd", grid=(R // TT,),
        in_specs=[row(D_MODEL), row(D_MODEL), per_ex, per_ex, _full((1, D_MODEL)), _full(w_in.shape), _full((1, ATT_W)),
                  _full((1, KV_W)), tab, tab, _full((256, 256)), row(ATT_W), row(KV_W), row(ATT_W), pad, pad,
                  row(ATT_W), row(SHIFT_W), row(RWKV_W)],
        out_specs=(row(D_MODEL), row(C_END), per_ex, per_ex, _full((1, D_MODEL)), _full((1, ATT_W)), _full((1, KV_W))),
        out_shape=(sds((R, D_MODEL), F32), sds((R, C_END), MXU_DTYPE), sds((nb, 1, D_MODEL), F32),
                   sds((nb, 1, D_MODEL), F32), sds((1, D_MODEL), F32), sds((1, ATT_W), F32), sds((1, KV_W), F32)),
        compiler_params=_cp(("arbitrary",)),
    )(x2, dy, shift, scale, g_pre, w_in, qg, kg, cos, sin, bd, q_raw, k_raw, dqr, dkp, dvp, d_gatt, d_rin, d_grw)


def _w_in_grad_call(hb, dproj):
    R = hb.shape[0]
    TT = min(W_GRAD_ROWS, R)
    CB = 1152
    last = R // TT - 1

    def body(h_ref, d_ref, o_ref, acc):
        g = _dot_tn(h_ref[...], d_ref[...])

        @pl.when(pl.program_id(1) == 0)
        def _():
            acc[...] = g

        @pl.when(pl.program_id(1) > 0)
        def _():
            acc[...] += g

        @pl.when(pl.program_id(1) == last)
        def _():
            o_ref[...] = acc[...].astype(o_ref.dtype)

    return pl.pallas_call(
        body, name="w_in_grad", grid=(C_END // CB, R // TT),
        in_specs=[pl.BlockSpec((TT, D_MODEL), lambda j, i: (i, 0)), pl.BlockSpec((TT, CB), lambda j, i: (i, j))],
        out_specs=pl.BlockSpec((D_MODEL, CB), lambda j, i: (0, j)),
        out_shape=jax.ShapeDtypeStruct((D_MODEL, C_END), MXU_DTYPE),
        scratch_shapes=[pltpu.VMEM((D_MODEL, CB), F32)], compiler_params=_cp(("arbitrary", "arbitrary")),
    )(hb, dproj)


def _adam_refs(p_ref, w_ref, m_ref, v_ref, g_o, d_o, m_o, v_o):
    g = p_ref[0].astype(F32)
    for j in range(1, p_ref.shape[0]):
        g = g + p_ref[j].astype(F32)
    m2 = ADAM_B1 * m_ref[...] + (1.0 - ADAM_B1) * g
    v2 = ADAM_B2 * v_ref[...] + (1.0 - ADAM_B2) * jnp.square(g)
    m_hat = m2 / (1.0 - ADAM_B1 ** ADAM_STEP)
    v_hat = v2 / (1.0 - ADAM_B2 ** ADAM_STEP)
    g_o[...] = g
    d_o[...] = -ADAM_LR * (m_hat / (jnp.sqrt(v_hat) + ADAM_EPS) + ADAM_WD * w_ref[...])
    m_o[...] = m2
    v_o[...] = v2


def _adam_small_call(items, name):
    n = len(items)

    def body(*refs):
        for k in range(n):
            _adam_refs(*refs[4 * k:4 * k + 4], *refs[4 * n + 4 * k:4 * n + 4 * k + 4])

    out_shape = tuple(jax.ShapeDtypeStruct(w.shape, F32) for _, w, _, _ in items for _ in range(4))
    out = pl.pallas_call(body, name=name, out_shape=out_shape)(*[a for item in items for a in item])
    return [out[4 * k:4 * k + 4] for k in range(n)]


def _adam_call(parts, w, m, v, name, row_tile=None):
    P, M, N = parts.shape
    TM = M if row_tile is None else row_tile

    def body(*refs):
        _adam_refs(*refs)

    blk = pl.BlockSpec((TM, N), lambda i: (i, 0))
    return pl.pallas_call(
        body, name=name, grid=(M // TM,),
        in_specs=[pl.BlockSpec((P, TM, N), lambda i: (0, i, 0)), blk, blk, blk], out_specs=(blk,) * 4,
        out_shape=(jax.ShapeDtypeStruct((M, N), F32),) * 4, compiler_params=_cp(("arbitrary",)),
    )(parts, w, m, v)


_SMALL_ROWS = 136


def _pack_small(taps, w_up, w0, a_up, a0):
    flat = jnp.concatenate([taps.reshape(-1), w_up.reshape(-1), w0.reshape(-1), a_up.reshape(-1), a0.reshape(-1)])
    return jnp.pad(flat, (0, _SMALL_ROWS * 128 - flat.shape[0])).reshape(_SMALL_ROWS, 128)


def _unpack_small(packed):
    n = packed.shape[0]
    flat = packed.reshape(n, -1)
    out, o = [], 0
    for shape in ((3, 208), (2, 64, 64), (2, 64), (2, 64, 64), (2, 64)):
        size = 1
        for s in shape:
            size *= s
        out.append(flat[:, o:o + size].reshape((n,) + shape))
        o += size
    return out


def _cols_to_full(blocks):
    nd = blocks.ndim
    moved = jnp.moveaxis(blocks, 0, nd - 2)
    return moved.reshape(moved.shape[:-2] + (moved.shape[-2] * moved.shape[-1],))


def _full_to_cols(full):
    k = full.shape[-1] // NDEV
    return jnp.moveaxis(full.reshape(full.shape[:-1] + (NDEV, k)), -2, 0)


_REP_SIZES = (("g_pre", 1024), ("q_norm_g", 64), ("k_norm_g", 64), ("k_k", 512), ("k_a", 512), ("r_k", 512),
              ("gn_w", 512), ("gn_b", 512), ("g_post", 1024))
_REP_ROWS = 40


def kernel(x, c, w_ada, b_ada, g_pre, w_in, q_norm_g, k_norm_g, shift_taps, w_up, w0, a_up, a0, k_k, k_a, r_k, gn_w, gn_b, w_out, g_post, loss_target, m_w_ada, m_b_ada, m_g_pre, m_w_in, m_q_norm_g, m_k_norm_g, m_shift_taps, m_w_up, m_w0, m_a_up, m_a0, m_k_k, m_k_a, m_r_k, m_gn_w, m_gn_b, m_w_out, m_g_post, v_w_ada, v_b_ada, v_g_pre, v_w_in, v_q_norm_g, v_k_norm_g, v_shift_taps, v_w_up, v_w0, v_a_up, v_a0, v_k_k, v_k_a, v_r_k, v_gn_w, v_gn_b, v_w_out, v_g_post):
    B, T, _ = x.shape
    R = B * T
    me = 4 * lax.axis_index("x") + 2 * lax.axis_index("y") + lax.axis_index("c")
    x2 = x.reshape(R, D_MODEL)
    tgt2 = loss_target.reshape(R, D_MODEL)

    seg = jnp.arange(256) // HEAD_DIM
    bd = (seg[:, None] == seg[None, :]).astype(MXU_DTYPE)
    eye = (jnp.arange(HEAD_DIM)[:, None] == (jnp.arange(RWKV_W) % HEAD_DIM)[None, :])
    eye_b, eye_f = eye.astype(MXU_DTYPE), eye.astype(F32)
    cos, sin = _rope_tables(T)

    c_g, w_in_g, w_out_g, small_g = _exchange(
        [c, w_in[0].T.astype(MXU_DTYPE), w_out[0].astype(MXU_DTYPE),
         _pack_small(shift_taps[0], w_up[0], w0[0], a_up[0], a0[0])], ["all"] * 4, "gather_params")
    c_all = c_g.reshape(NDEV * B, D_MODEL)
    w_in_f = w_in_g.reshape(C_END, D_MODEL)
    w_out_f = w_out_g.reshape(D_MODEL, D_MODEL)
    taps_b, w_up_b, w0_b, a_up_b, a0_b = _unpack_small(small_g)
    taps_f = jnp.pad(_cols_to_full(taps_b), ((0, 5), (0, 0)))
    w_up_f, a_up_f = _cols_to_full(w_up_b), _cols_to_full(a_up_b)
    w0_f, a0_f = _cols_to_full(w0_b), _cols_to_full(a0_b)
    wup_pad = jnp.pad(w_up_f, ((0, 0), (0, 64), (0, 0))).astype(MXU_DTYPE)
    aup_pad = jnp.pad(a_up_f, ((0, 0), (64, 0), (0, 0))).astype(MXU_DTYPE)

    ncol = w_ada.shape[2]
    b_cols = lax.dynamic_slice(b_ada, (0, me * ncol), (1, ncol))
    mod_cols = _mod_call(c_all, w_ada[0].astype(MXU_DTYPE), b_cols)
    (mod_g,) = _exchange([mod_cols], ["all"], "gather_mod")
    mod = lax.dynamic_slice(_cols_to_full(mod_g), (me * B, 0), (B, 3 * D_MODEL))
    shift, scale, gate = [mod[:, j * D_MODEL:(j + 1) * D_MODEL].reshape(B, 1, D_MODEL) for j in range(3)]

    qg = jnp.tile(q_norm_g, (1, ATT_W // HEAD_DIM))
    kg = jnp.tile(k_norm_g, (1, KV_W // HEAD_DIM))
    rk_row = r_k.reshape(1, RWKV_W)

    hb, qr, kpad, vpad, q_raw, k_raw, g_att, rin, g_rw = _in_proj_call(
        x2, shift, scale, g_pre, w_in_f, qg, kg, cos, sin, bd, T)
    y_att = _att_fwd_call(qr, kpad, vpad, B, T)
    shifted, v_rows = _shift_fwd_call(rin, taps_f, T)
    w_s, kt_s, akk_s, kk_s = _rwkv_prep_call(shifted, wup_pad, aup_pad, w0_f, a0_f, k_k, k_a, bd, T)
    sh3 = shifted.reshape(B, T, SHIFT_W)
    r4 = lambda a: a.reshape(2, B, T, RWKV_W)
    y0, y1, st = _scan_fwd_call(r4(w_s), r4(kt_s), r4(akk_s), kk_s.reshape(B, T, RWKV_W), sh3, eye_b, eye_f, bd, B, T)

    (loss_blk, dy, d_yatt, d_gatt, d_ys, d_r2, d_v2, d_kts, d_grw, d_gate, g_wout, g_gpost, g_gnw, g_gnb,
     g_rk) = _out_head_call(x2, tgt2, gate, y_att, g_att, y0.reshape(R, RWKV_W), y1.reshape(R, RWKV_W), shifted, kt_s,
                            g_rw, w_out_f, g_post, gn_w, gn_b, rk_row, bd, T)
    v_heads = v_rows.reshape(B, T, RWKV_W // HEAD_DIM, 2 * HEAD_DIM)
    scan_cts = _scan_bwd_call(r4(w_s), r4(kt_s), r4(akk_s), kk_s.reshape(B, T, RWKV_W), sh3, v_heads,
                              d_ys.reshape(B, T, RWKV_W), st, eye_b, eye_f, bd, B, T)
    scan_cts = [a.reshape(R, RWKV_W) for a in scan_cts]
    d_shifted, g_wup, g_aup, g_w0, g_a0, g_kk, g_ka = _rwkv_prep_bwd_call(
        shifted, scan_cts + [d_r2, d_v2, d_kts], wup_pad, aup_pad, w0_f, a0_f, k_k, k_a, bd, T)
    d_rin, g_taps = _shift_bwd_call(rin, d_shifted, taps_f, T)
    dqr, dkp, dvp = _att_bwd_call(qr, kpad, vpad, d_yatt, B, T)
    grad_x, dproj, d_shift, d_scale, g_gpre, g_qg, g_kg = _in_proj_bwd_call(
        x2, dy, shift, scale, g_pre, w_in_f, qg, kg, cos, sin, bd, q_raw, k_raw, dqr, dkp, dvp, d_gatt, d_rin, d_grw, T)
    g_win = _w_in_grad_call(hb, dproj)

    rep = jnp.concatenate([g_gpre.reshape(-1), g_qg.reshape(-1, HEAD_DIM).sum(0), g_kg.reshape(-1, HEAD_DIM).sum(0),
                           g_kk.reshape(-1), g_ka.reshape(-1), g_rk.reshape(-1), g_gnw.reshape(-1), g_gnb.reshape(-1),
                           g_gpost.reshape(-1), loss_blk[0, :1]])
    rep = jnp.pad(rep, (0, _REP_ROWS * 128 - rep.shape[0])).reshape(_REP_ROWS, 128)
    dmod = jnp.concatenate([d_shift, d_scale, d_gate], axis=2).reshape(B, 3 * D_MODEL)
    small_parts = jax.vmap(_pack_small)(_full_to_cols(g_taps[:3]), _full_to_cols(g_wup[:, :64, :]), _full_to_cols(g_w0),
                                        _full_to_cols(g_aup[:, 64:, :]), _full_to_cols(g_a0))
    by_core = lambda a: jnp.swapaxes(a.reshape((NDEV // 2, 2) + a.shape[1:]), 0, 1).astype(MXU_DTYPE)
    s_win, s_wout = _pair_sum_call(
        [by_core(_full_to_cols(g_win)), by_core(g_wout.reshape(NDEV, D_MODEL // NDEV, D_MODEL))], "reduce_pair")
    p_win, p_wout, p_small, dmod_g, rep_g = _exchange(
        [s_win, s_wout, small_parts, dmod, rep], ["chips", "chips", "scatter", "all", "all"], "reduce_grads")
    dmod_all = dmod_g.reshape(NDEV * B, 3 * D_MODEL)
    g_wada = _wada_grad_call(c_all, lax.dynamic_slice(dmod_all, (0, me * ncol), (NDEV * B, ncol)))

    res, small = {}, []

    def adam(name, parts, w, m, v, row_tile=None, alone=False):
        two_d = (-1, w.shape[-1])
        item = (parts.reshape((parts.shape[0],) + w.reshape(two_d).shape), w.reshape(two_d), m.reshape(two_d),
                v.reshape(two_d))
        if alone:
            res[name] = [o.reshape(w.shape) for o in _adam_call(*item, "adam_" + name, row_tile)]
        else:
            small.append((name, w.shape, item))

    adam("w_ada", g_wada[None], w_ada, m_w_ada, v_w_ada, alone=True)
    adam("b_ada", dmod_all.reshape(NDEV * B, 1, 3 * D_MODEL), b_ada, m_b_ada, v_b_ada)
    adam("w_in", p_win, w_in, m_w_in, v_w_in, 512, alone=True)
    adam("w_out", p_wout, w_out, m_w_out, v_w_out, alone=True)
    taps_p, wup_p, w0_p, aup_p, a0_p = _unpack_small(p_small)
    adam("shift_taps", taps_p, shift_taps, m_shift_taps, v_shift_taps)
    adam("w_up", wup_p, w_up, m_w_up, v_w_up)
    adam("w0", w0_p, w0, m_w0, v_w0)
    adam("a_up", aup_p, a_up, m_a_up, v_a_up)
    adam("a0", a0_p, a0, m_a0, v_a0)
    rep_flat = rep_g.reshape(NDEV, -1)
    off = 0
    given = dict(g_pre=(g_pre, m_g_pre, v_g_pre), q_norm_g=(q_norm_g, m_q_norm_g, v_q_norm_g),
                 k_norm_g=(k_norm_g, m_k_norm_g, v_k_norm_g), k_k=(k_k, m_k_k, v_k_k), k_a=(k_a, m_k_a, v_k_a),
                 r_k=(r_k, m_r_k, v_r_k), gn_w=(gn_w, m_gn_w, v_gn_w), gn_b=(gn_b, m_gn_b, v_gn_b),
                 g_post=(g_post, m_g_post, v_g_post))
    for name, size in _REP_SIZES:
        adam(name, rep_flat[:, off:off + size], *given[name])
        off += size
    for (name, shape, _), out in zip(small, _adam_small_call([item for _, _, item in small], "adam_small")):
        res[name] = [o.reshape(shape) for o in out]

    loss = jnp.sum(rep_flat[:, off])
    order = ["w_ada", "b_ada", "g_pre", "w_in", "q_norm_g", "k_norm_g", "shift_taps", "w_up", "w0", "a_up", "a0", "k_k",
             "k_a", "r_k", "gn_w", "gn_b", "w_out", "g_post"]
    return (loss, grad_x.reshape(B, T, D_MODEL), *[res[n][0] for n in order], *[res[n][1] for n in order],
            *[res[n][2] for n in order], *[res[n][3] for n in order])
```

```python
import functools

import jax
import jax.numpy as jnp
from jax import lax
from jax.experimental import pallas as pl
from jax.experimental.pallas import tpu as pltpu

F32 = jnp.float32
MXU_DTYPE = jnp.bfloat16
MESH = pl.DeviceIdType.MESH
NDEV = 8

D_MODEL = 1024
HEAD_DIM = 64
ATT_W = 512
KV_W = 128
RWKV_W = 512
LORA_W = 128
SHIFT_W = 3 * RWKV_W + LORA_W
GRID_W = 64
ROPE_THETA = 10000.0
DECAY_SCALE = 0.6065306597126334
NORM_EPS = 1e-6
GN_EPS = 64e-5
L2_EPS = 1e-12
ATT_SCALE = HEAD_DIM ** -0.5
C_Q, C_K, C_V, C_GA, C_RIN, C_GRW, C_END = 0, 512, 640, 768, 1280, 2944, 3456

ADAM_LR, ADAM_B1, ADAM_B2, ADAM_EPS, ADAM_WD, ADAM_STEP = 0.001, 0.9, 0.999, 1e-08, 0.01, 10

ROW_TILE = 256
SHIFT_TILE = 512
W_GRAD_ROWS = 2048
ATT_TILE_FWD = 256
ATT_TILE_BWD = 1024
SCAN_CHUNK = 64
SCAN_UNROLL = 16
VMEM_LIMIT = 56 * 1024 * 1024


def _cp(sem=None):
    return pltpu.CompilerParams(dimension_semantics=sem, vmem_limit_bytes=VMEM_LIMIT)


def _dot(a, b, dims=(((1,), (0,)), ((), ()))):
    return lax.dot_general(a.astype(MXU_DTYPE), b.astype(MXU_DTYPE), dims, preferred_element_type=F32)


def _dot_nt(a, b):
    return _dot(a, b, (((1,), (1,)), ((), ())))


def _dot_tn(a, b):
    return _dot(a, b, (((0,), (0,)), ((), ())))


def _seg_dot(xb, bd):
    n = xb.shape[1]
    if n <= 256:
        return jnp.dot(xb, bd[:n, :n], preferred_element_type=F32)
    parts = [jnp.dot(xb[:, c:c + 256], bd, preferred_element_type=F32) for c in range(0, n, 256)]
    return jnp.concatenate(parts, axis=1)


def _segsum_raw(x, bd):
    rows = x.shape[0]
    hi = x.astype(MXU_DTYPE)
    lo = (x - hi.astype(F32)).astype(MXU_DTYPE)
    both = _seg_dot(jnp.concatenate([hi, lo], axis=0), bd)
    return both[:rows] + both[rows:]


@jax.custom_vjp
def _segsum_d(x, bd):
    return _segsum_raw(x, bd)


def _segsum_d_fwd(x, bd):
    return _segsum_raw(x, bd), bd


def _segsum_d_bwd(bd, ct):
    return _segsum_raw(ct, bd), jnp.zeros_like(bd)


_segsum_d.defvjp(_segsum_d_fwd, _segsum_d_bwd)


def _rope_tables(T):
    t = jnp.arange(T, dtype=F32)
    row = jnp.floor(t / GRID_W)
    col = t - row * GRID_W
    n_freq = HEAD_DIM // 4
    inv_freq = ROPE_THETA ** (-jnp.arange(n_freq, dtype=F32) / n_freq)
    d = jnp.arange(HEAD_DIM)
    pos = jnp.where((d < HEAD_DIM // 2)[None, :], row[:, None], col[:, None])
    ang = pos * inv_freq[d % n_freq][None, :]
    sign = jnp.where((d % 32) < 16, -1.0, 1.0).astype(F32)[None, :]
    cos = jnp.cos(ang)
    sin = jnp.sin(ang) * sign
    return jnp.tile(cos, (1, 2)), jnp.tile(sin, (1, 2))


def _rope_raw(x, cos, sin):
    n = x.shape[1]
    lane = lax.broadcasted_iota(jnp.int32, (1, n), 1)
    first = (lane % 32) < 16
    partner = jnp.where(first, pltpu.roll(x, n - 16, 1), pltpu.roll(x, 16, 1))
    return x * cos + partner * sin


@jax.custom_vjp
def _rope_d(x, cos, sin):
    return _rope_raw(x, cos, sin)


def _rope_d_fwd(x, cos, sin):
    return _rope_raw(x, cos, sin), (cos, sin)


def _rope_d_bwd(res, ct):
    cos, sin = res
    return _rope_raw(ct, cos, -sin), jnp.zeros_like(cos), jnp.zeros_like(sin)


_rope_d.defvjp(_rope_d_fwd, _rope_d_bwd)


def _rms(x, g):
    return x * lax.rsqrt(jnp.mean(x * x, axis=-1, keepdims=True) + NORM_EPS) * g


def _pre_fn(x, shift, scale, g_pre):
    return _rms(x, g_pre) * (1.0 + scale) + shift


def _qk_fn(q, g, cos, sin, bd, scale, diff):
    segsum = _segsum_d if diff else _segsum_raw
    rope = _rope_d if diff else _rope_raw
    qn = q * lax.rsqrt(segsum(q * q, bd) * (1.0 / HEAD_DIM) + NORM_EPS) * g
    return rope(qn, cos, sin) * scale


def _silu(x):
    return x * jax.nn.sigmoid(x)


def _rwkv_pw(k, pw0, pw1, pa0, pa1, w0, a0, k_k, k_a, bd, diff):
    segsum = _segsum_d if diff else _segsum_raw
    kk = k * k_k
    kk = kk * lax.rsqrt(segsum(kk * kk, bd) + L2_EPS)
    ws, kts, akks = [], [], []
    for z, (pw, pa) in enumerate(((pw0, pa0), (pw1, pa1))):
        w = jnp.exp(-DECAY_SCALE * jax.nn.sigmoid(w0[z:z + 1, :] + pw))
        a = jax.nn.sigmoid(a0[z:z + 1, :] + pa)
        ws.append(w)
        kts.append(k * (1.0 + (a - 1.0) * k_a))
        akks.append(a * kk)
    return ws[0], ws[1], kts[0], kts[1], akks[0], akks[1], kk


def _mix_fn(y_att, g_att, ys, r, v, kts, g_rw, gn_w, gn_b, r_k, bd, diff):
    segsum = _segsum_d if diff else _segsum_raw
    mu = segsum(ys, bd) * (1.0 / HEAD_DIM)
    d = ys - mu
    var = segsum(d * d, bd) * (1.0 / HEAD_DIM)
    yn = d * lax.rsqrt(var + GN_EPS) * gn_w + gn_b
    bonus = segsum(r * kts * r_k, bd) * v
    return y_att * _silu(g_att), (yn + bonus) * _silu(g_rw)


def _loss_fn(out, x, tgt, gate, g_post):
    e = x + gate * _rms(out, g_post) - tgt
    s = jnp.sum(e * e, axis=1, keepdims=True)
    return jnp.sum(s, axis=0, keepdims=True) * (0.5 / D_MODEL)


def _exchange(arrays, modes, name):
    n = len(arrays)
    out_shape = tuple(
        jax.ShapeDtypeStruct(((NDEV,) + tuple(a.shape)) if mode == "all" else tuple(a.shape), a.dtype)
        for a, mode in zip(arrays, modes))
    chips = (4, 2, 6)

    def body(*refs):
        ins, outs = refs[:n], refs[n:2 * n]
        send_sems, recv_sems, local_sems = refs[2 * n:]
        ix, iy, ic = lax.axis_index("x"), lax.axis_index("y"), lax.axis_index("c")
        me = 4 * ix + 2 * iy + ic

        def peer(m):
            px = 1 - ix if (m >> 2) & 1 else ix
            py = 1 - iy if (m >> 1) & 1 else iy
            pc = 1 - ic if m & 1 else ic
            return (px, py, pc), 4 * px + 2 * py + pc

        def copy(k, j, src_ref, slot, to):
            return pltpu.make_async_remote_copy(src_ref=src_ref, dst_ref=outs[k].at[slot], send_sem=send_sems.at[k, j],
                                                recv_sem=recv_sems.at[k, j], device_id=to, device_id_type=MESH)

        local, sends, arrivals, forwards = [], [], [], []
        for k in range(n):
            if modes[k] == "scatter":
                local.append(pltpu.make_async_copy(ins[k].at[me], outs[k].at[me], local_sems.at[k]))
                for m in range(1, NDEV):
                    to, p = peer(m)
                    sends.append(copy(k, m - 1, ins[k].at[p], me, to))
                    arrivals.append(copy(k, m - 1, ins[k].at[p], p, to))
            elif modes[k] == "chips":
                mine = me // 2
                local.append(pltpu.make_async_copy(ins[k].at[mine], outs[k].at[mine], local_sems.at[k]))
                for j, m in enumerate(chips):
                    to, p = peer(m)
                    sends.append(copy(k, j, ins[k].at[p // 2], mine, to))
                    arrivals.append(copy(k, j, ins[k].at[p // 2], p // 2, to))
            else:
                local.append(pltpu.make_async_copy(ins[k], outs[k].at[me], local_sems.at[k]))
                sib, sib_slot = peer(1)
                sends.append(copy(k, 0, ins[k], me, sib))
                for j, m in enumerate(chips):
                    to, p = peer(m)
                    sends.append(copy(k, 1 + j, ins[k], me, to))
                    forwards.append((copy(k, 1 + j, ins[k], p, to), copy(k, 4 + j, outs[k].at[p], p, sib)))
                    arrivals.append(copy(k, 4 + j, ins[k], peer(m ^ 1)[1], sib))
                arrivals.append(copy(k, 0, ins[k], sib_slot, sib))
        for cp in local + sends:
            cp.start()
        for arrived, onward in forwards:
            arrived.wait_recv()
            onward.start()
        for cp in arrivals:
            cp.wait_recv()
        for cp in sends + [onward for _, onward in forwards]:
            cp.wait_send()
        for cp in local:
            cp.wait()

    any_spec = pl.BlockSpec(memory_space=pl.ANY)
    return pl.pallas_call(
        body, name=name, out_shape=out_shape,
        in_specs=[any_spec] * n, out_specs=tuple([any_spec] * n),
        scratch_shapes=[pltpu.SemaphoreType.DMA((n, NDEV - 1)), pltpu.SemaphoreType.DMA((n, NDEV - 1)),
                        pltpu.SemaphoreType.DMA((n,))],
    )(*arrays)


def _pair_sum_call(parts, name):
    n = len(parts)

    def body(*refs):
        in_r, out_r, mine_r, land_r = (refs[j * n:(j + 1) * n] for j in range(4))
        send_sems, recv_sems, local_sems = refs[4 * n:]
        core = lax.axis_index("c")
        sibling = (lax.axis_index("x"), lax.axis_index("y"), 1 - core)
        local = [pltpu.make_async_copy(in_r[k].at[core], mine_r[k], local_sems.at[k]) for k in range(n)]
        swaps = [pltpu.make_async_remote_copy(src_ref=in_r[k].at[1 - core], dst_ref=land_r[k], send_sem=send_sems.at[k],
                                              recv_sem=recv_sems.at[k], device_id=sibling, device_id_type=MESH)
                 for k in range(n)]
        for cp in local + swaps:
            cp.start()
        for k in range(n):
            local[k].wait()
            swaps[k].wait()
            out_r[k][...] = (mine_r[k][...].astype(F32) + land_r[k][...].astype(F32)).astype(out_r[k].dtype)

    halves = [jax.ShapeDtypeStruct(a.shape[1:], a.dtype) for a in parts]
    return pl.pallas_call(
        body, name=name, out_shape=tuple(halves), in_specs=[pl.BlockSpec(memory_space=pl.ANY)] * n,
        scratch_shapes=[pltpu.VMEM(h.shape, h.dtype) for h in halves] * 2 + [pltpu.SemaphoreType.DMA((n,))] * 3,
        compiler_params=pltpu.CompilerParams(vmem_limit_bytes=VMEM_LIMIT),
    )(*parts)


def _mod_call(c_all, w_ada, b_cols):
    def body(c_ref, w_ref, b_ref, o_ref):
        o_ref[...] = _dot(_silu(c_ref[...]), w_ref[...]) + b_ref[...]

    return pl.pallas_call(body, name="mod_fwd",
                          out_shape=jax.ShapeDtypeStruct((c_all.shape[0], w_ada.shape[1]), F32))(c_all, w_ada, b_cols)


def _wada_grad_call(c_all, dmod_cols):
    def body(c_ref, d_ref, o_ref):
        o_ref[...] = _dot_tn(_silu(c_ref[...]), d_ref[...])

    return pl.pallas_call(body, name="w_ada_grad",
                          out_shape=jax.ShapeDtypeStruct((c_all.shape[1], dmod_cols.shape[1]), F32))(c_all, dmod_cols)


def _full(shape):
    nd = len(shape)
    return pl.BlockSpec(shape, lambda *_: (0,) * nd)


def _in_proj_call(x2, shift, scale, g_pre, w_in, qg, kg, cos, sin, bd, T):
    R = x2.shape[0]
    TT = min(SHIFT_TILE, T)
    tpe = T // TT

    def body(x_ref, sh_ref, sc_ref, gp_ref, w_ref, qg_ref, kg_ref, cos_ref, sin_ref, bd_ref,
             hb_ref, qr_ref, kpad_ref, vpad_ref, qraw_ref, kraw_ref, gatt_ref, rin_ref, grw_ref):
        h = _pre_fn(x_ref[...], sh_ref[0], sc_ref[0], gp_ref[...])
        hb = h.astype(MXU_DTYPE)
        hb_ref[...] = hb

        def proj(c0, c1):
            return _dot_nt(hb, w_ref[c0:c1, :])

        q = proj(C_Q, C_K)
        k = proj(C_K, C_V)
        v = proj(C_V, C_GA)
        gatt_ref[...] = proj(C_GA, C_RIN)
        rin_ref[...] = proj(C_RIN, C_GRW)
        grw_ref[...] = proj(C_GRW, C_END)
        qraw_ref[...] = q
        kraw_ref[...] = k
        cos, sin, bd = cos_ref[...], sin_ref[...], bd_ref[...]
        qr = _qk_fn(q, qg_ref[...], jnp.tile(cos, (1, 4)), jnp.tile(sin, (1, 4)), bd, ATT_SCALE, False)
        qr_ref[...] = qr.astype(MXU_DTYPE)
        kr = _qk_fn(k, kg_ref[...], cos, sin, bd, 1.0, False)
        left = lax.broadcasted_iota(jnp.int32, (1, KV_W), 1) < HEAD_DIM
        for ref, val in ((kpad_ref, kr), (vpad_ref, v)):
            h0l = jnp.where(left, val, 0.0)
            h1r = jnp.where(left, 0.0, val)
            ref[0] = h0l.astype(MXU_DTYPE)
            ref[1] = pltpu.roll(h0l, HEAD_DIM, 1).astype(MXU_DTYPE)
            ref[2] = pltpu.roll(h1r, HEAD_DIM, 1).astype(MXU_DTYPE)
            ref[3] = h1r.astype(MXU_DTYPE)

    row = lambda w: pl.BlockSpec((TT, w), lambda i: (i, 0))
    per_ex = pl.BlockSpec((1, 1, D_MODEL), lambda i: (i // tpe, 0, 0))
    tab = pl.BlockSpec((TT, KV_W), lambda i: (i % tpe, 0))
    pad = pl.BlockSpec((4, TT, KV_W), lambda i: (0, i, 0))
    sds = jax.ShapeDtypeStruct
    return pl.pallas_call(
        body, name="in_proj", grid=(R // TT,),
        in_specs=[row(D_MODEL), per_ex, per_ex, _full((1, D_MODEL)), _full(w_in.shape), _full((1, ATT_W)),
                  _full((1, KV_W)), tab, tab, _full((256, 256))],
        out_specs=(row(D_MODEL), row(ATT_W), pad, pad, row(ATT_W), row(KV_W), row(ATT_W), row(SHIFT_W), row(RWKV_W)),
        out_shape=(sds((R, D_MODEL), MXU_DTYPE), sds((R, ATT_W), MXU_DTYPE), sds((4, R, KV_W), MXU_DTYPE),
                   sds((4, R, KV_W), MXU_DTYPE), sds((R, ATT_W), F32), sds((R, KV_W), F32), sds((R, ATT_W), F32),
                   sds((R, SHIFT_W), F32), sds((R, RWKV_W), F32)),
        compiler_params=_cp(("arbitrary",)),
    )(x2, shift, scale, g_pre, w_in, qg, kg, cos, sin, bd)


def _softmax_parts(s):
    e = jnp.exp(s - jnp.max(s, axis=1, keepdims=True))
    return e, 1.0 / jnp.sum(e, axis=1, keepdims=True)


def _att_specs(T, TQ):
    nq = T // TQ
    qspec = pl.BlockSpec((TQ, KV_W), lambda b, p, i: (b * nq + i, p))
    side = lambda s: pl.BlockSpec((None, T, KV_W), lambda b, p, i: (2 * (p // 2) + s, b, 0))
    return nq, qspec, side


def _att_fwd_call(qr, kpad, vpad, B, T):
    TQ = min(ATT_TILE_FWD, T)
    nq, qspec, side = _att_specs(T, TQ)

    def body(q_ref, kl_ref, kr_ref, vl_ref, vr_ref, o_ref):
        q = q_ref[...]
        ea, inv_a = _softmax_parts(_dot_nt(q, kl_ref[...]))
        eb, inv_b = _softmax_parts(_dot_nt(q, kr_ref[...]))
        o_ref[...] = _dot(ea, vl_ref[...]) * inv_a + _dot(eb, vr_ref[...]) * inv_b

    return pl.pallas_call(
        body, name="att_fwd", grid=(B, 4, nq),
        in_specs=[qspec, side(0), side(1), side(0), side(1)], out_specs=qspec,
        out_shape=jax.ShapeDtypeStruct((B * T, ATT_W), F32),
        compiler_params=_cp(("arbitrary",) * 3),
    )(qr, kpad, kpad, vpad, vpad)


def _att_bwd_call(qr, kpad, vpad, d_o, B, T):
    TQ = min(ATT_TILE_BWD, T)
    nq, qspec, side = _att_specs(T, TQ)

    def body(q_ref, kl_ref, kr_ref, vl_ref, vr_ref, do_ref, dq_ref, dk_ref, dv_ref):
        i = pl.program_id(2)
        q, do = q_ref[...], do_ref[...]
        left = lax.broadcasted_iota(jnp.int32, (1, KV_W), 1) < HEAD_DIM
        dq = jnp.zeros((TQ, KV_W), F32)
        dk = jnp.zeros((T, KV_W), F32)
        dv = jnp.zeros((T, KV_W), F32)
        for k_ref, v_ref, mask in ((kl_ref, vl_ref, left), (kr_ref, vr_ref, jnp.logical_not(left))):
            kk, vv = k_ref[...], v_ref[...]
            e, inv = _softmax_parts(_dot_nt(q, kk))
            dp = _dot_nt(do, vv)
            ds = e * (dp - inv * jnp.sum(e * dp, axis=1, keepdims=True))
            dq = dq + _dot(ds, kk) * inv
            dk = dk + _dot_tn(ds, jnp.where(mask, q * inv, 0.0))
            dv = dv + _dot_tn(e, jnp.where(mask, do * inv, 0.0))
        dq_ref[...] = dq

        @pl.when(i == 0)
        def _():
            dk_ref[...] = dk
            dv_ref[...] = dv

        @pl.when(i > 0)
        def _():
            dk_ref[...] += dk
            dv_ref[...] += dv

    acc = pl.BlockSpec((None, T, KV_W), lambda b, p, i: (p, b, 0))
    sds = jax.ShapeDtypeStruct
    return pl.pallas_call(
        body, name="att_bwd", grid=(B, 4, nq),
        in_specs=[qspec, side(0), side(1), side(0), side(1), qspec], out_specs=(qspec, acc, acc),
        out_shape=(sds((B * T, ATT_W), F32), sds((4, B * T, KV_W), F32), sds((4, B * T, KV_W), F32)),
        compiler_params=_cp(("arbitrary",) * 3),
    )(qr, kpad, kpad, vpad, vpad, d_o)


def _shift_specs(R, T, TT, width):
    tpe = T // TT
    nb8 = R // 8
    cur = pl.BlockSpec((TT, width), lambda i: (i, 0))
    prev = pl.BlockSpec((8, width), lambda i: (jnp.maximum(i * (TT // 8) - 1, 0), 0))
    nxt = pl.BlockSpec((8, width), lambda i: (jnp.minimum((i + 1) * (TT // 8), nb8 - 1), 0))
    return tpe, cur, prev, nxt


def _neighbours(cur, prev8, next8, i, tpe, TT):
    rows = lax.broadcasted_iota(jnp.int32, (TT, 1), 0)
    first = jnp.where(i % tpe == 0, 0.0, 1.0)
    last = jnp.where(i % tpe == tpe - 1, 0.0, 1.0)
    before = jnp.where(rows == 0, prev8[7:8, :] * first, pltpu.roll(cur, 1, 0))
    after = jnp.where(rows == TT - 1, next8[0:1, :] * last, pltpu.roll(cur, TT - 1, 0))
    return before, after


def _shift_fwd_call(x, taps, T):
    R, width = x.shape
    TT = min(SHIFT_TILE, T)
    tpe, cur, prev, nxt = _shift_specs(R, T, TT, width)

    def body(x_ref, p_ref, n_ref, t_ref, o_ref, vh_ref):
        xc = x_ref[...]
        before, after = _neighbours(xc, p_ref[...], n_ref[...], pl.program_id(0), tpe, TT)
        out = t_ref[0:1, :] * before + t_ref[1:2, :] * xc + t_ref[2:3, :] * after
        o_ref[...] = out
        left = lax.broadcasted_iota(jnp.int32, (1, KV_W), 1) < HEAD_DIM
        for p in range(RWKV_W // KV_W):
            pair = out[:, 2 * RWKV_W + p * KV_W:2 * RWKV_W + (p + 1) * KV_W]
            vh_ref[:, 2 * p * KV_W:(2 * p + 1) * KV_W] = jnp.where(left, pair, 0.0)
            vh_ref[:, (2 * p + 1) * KV_W:(2 * p + 2) * KV_W] = jnp.where(left, pltpu.roll(pair, HEAD_DIM, 1), 0.0)

    return pl.pallas_call(
        body, name="shift_fwd", grid=(R // TT,), in_specs=[cur, prev, nxt, _full(taps.shape)],
        out_specs=(cur, pl.BlockSpec((TT, 2 * RWKV_W), lambda i: (i, 0))),
        out_shape=(jax.ShapeDtypeStruct((R, width), F32), jax.ShapeDtypeStruct((R, 2 * RWKV_W), F32)),
        compiler_params=_cp(("arbitrary",)),
    )(x, x, x, taps)


def _shift_bwd_call(x, d, taps, T):
    R, width = x.shape
    TT = min(SHIFT_TILE, T)
    tpe, cur, prev, nxt = _shift_specs(R, T, TT, width)

    def body(x_ref, xp_ref, xn_ref, d_ref, dp_ref, dn_ref, t_ref, dx_ref, dt_ref):
        i = pl.program_id(0)
        xc, dc = x_ref[...], d_ref[...]
        d_before, d_after = _neighbours(dc, dp_ref[...], dn_ref[...], i, tpe, TT)
        dx_ref[...] = t_ref[2:3, :] * d_before + t_ref[1:2, :] * dc + t_ref[0:1, :] * d_after
        x_before, x_after = _neighbours(xc, xp_ref[...], xn_ref[...], i, tpe, TT)
        @pl.when(i == 0)
        def _():
            dt_ref[...] = jnp.zeros_like(dt_ref)

        for j, xs in enumerate((x_before, xc, x_after)):
            dt_ref[j:j + 1, :] += jnp.sum(dc * xs, axis=0, keepdims=True)

    return pl.pallas_call(
        body, name="shift_bwd", grid=(R // TT,),
        in_specs=[cur, prev, nxt, cur, prev, nxt, _full(taps.shape)], out_specs=(cur, _full((8, width))),
        out_shape=(jax.ShapeDtypeStruct((R, width), F32), jax.ShapeDtypeStruct((8, width), F32)),
        compiler_params=_cp(("arbitrary",)),
    )(x, x, x, d, d, d, taps)


def _lora_in(wa):
    lane = lax.broadcasted_iota(jnp.int32, (1, LORA_W), 1)
    return jnp.where(lane < LORA_W // 2, jnp.tanh(wa), wa)


def _rwkv_prep_call(shifted, wup, aup, w0, a0, k_k, k_a, bd, T):
    R = shifted.shape[0]
    TT = min(SHIFT_TILE, T)

    def body(k_ref, wa_ref, wup_ref, aup_ref, w0_ref, a0_ref, kk_ref, ka_ref, bd_ref, w_o, kt_o, akk_o, kk_o):
        twa = _lora_in(wa_ref[...])
        pre = [_dot(twa, m_ref[z]) for m_ref in (wup_ref, aup_ref) for z in range(2)]
        outs = _rwkv_pw(k_ref[...], pre[0], pre[1], pre[2], pre[3], w0_ref[...], a0_ref[...], kk_ref[...],
                        ka_ref[...], bd_ref[...], False)
        w_o[0], w_o[1], kt_o[0], kt_o[1], akk_o[0], akk_o[1] = outs[:6]
        kk_o[...] = outs[6]

    col = lambda c, w: pl.BlockSpec((TT, w), lambda i: (i, c))
    two = pl.BlockSpec((2, TT, RWKV_W), lambda i: (0, i, 0))
    sds = jax.ShapeDtypeStruct
    return pl.pallas_call(
        body, name="rwkv_prep", grid=(R // TT,),
        in_specs=[col(1, RWKV_W), col(3 * RWKV_W // LORA_W, LORA_W), _full(wup.shape), _full(aup.shape),
                  _full((2, RWKV_W)), _full((2, RWKV_W)), _full((1, RWKV_W)), _full((1, RWKV_W)), _full((256, 256))],
        out_specs=(two, two, two, col(0, RWKV_W)),
        out_shape=(sds((2, R, RWKV_W), F32),) * 3 + (sds((R, RWKV_W), F32),),
        compiler_params=_cp(("arbitrary",)),
    )(shifted, shifted, wup, aup, w0, a0, k_k, k_a, bd)


def _rwkv_prep_bwd_call(shifted, cts, wup, aup, w0, a0, k_k, k_a, bd, T):
    R = shifted.shape[0]
    TT = min(SHIFT_TILE, T)

    def body(k_ref, wa_ref, dw0, dkt0, dakk0, dkk0, dr0, dv0, dw1, dkt1, dakk1, dkk1, dr1, dv1, dr2_ref, dv2_ref, dkts_ref,
             wup_ref, aup_ref, w0_ref, a0_ref, kk_ref, ka_ref, bd_ref,
             dsh_ref, gwup_ref, gaup_ref, gw0_ref, ga0_ref, gkk_ref, gka_ref):
        dw_ref, dkt_ref, dakk_ref, dkk_ref, dr_ref, dv_ref = ((dw0, dw1), (dkt0, dkt1), (dakk0, dakk1), (dkk0, dkk1),
                                                              (dr0, dr1), (dv0, dv1))
        i = pl.program_id(0)
        wa = wa_ref[...]
        twa = _lora_in(wa)
        pre = [_dot(twa, m_ref[z]) for m_ref in (wup_ref, aup_ref) for z in range(2)]
        fn = functools.partial(_rwkv_pw, bd=bd_ref[...], diff=True)
        _, vjp = jax.vjp(fn, k_ref[...], pre[0], pre[1], pre[2], pre[3], w0_ref[...], a0_ref[...], kk_ref[...],
                         ka_ref[...])
        dkts = dkts_ref[...]
        dk, dpw0, dpw1, dpa0, dpa1, gw0, ga0, gkk, gka = vjp(
            (dw_ref[0][...], dw_ref[1][...], dkt_ref[0][...] + dkts, dkt_ref[1][...] + dkts, dakk_ref[0][...],
             dakk_ref[1][...], dkk_ref[0][...] + dkk_ref[1][...]))
        dtwa = (_dot_nt(dpw0, wup_ref[0]) + _dot_nt(dpw1, wup_ref[1]) + _dot_nt(dpa0, aup_ref[0])
                + _dot_nt(dpa1, aup_ref[1]))
        lane = lax.broadcasted_iota(jnp.int32, (1, LORA_W), 1)
        dsh_ref[:, 0:RWKV_W] = dr_ref[0][...] + dr_ref[1][...] + dr2_ref[...]
        dsh_ref[:, RWKV_W:2 * RWKV_W] = dk
        dsh_ref[:, 2 * RWKV_W:3 * RWKV_W] = dv_ref[0][...] + dv_ref[1][...] + dv2_ref[...]
        dsh_ref[:, 3 * RWKV_W:] = jnp.where(lane < LORA_W // 2, dtwa * (1.0 - twa * twa), dtwa)
        acc = ((gwup_ref.at[0], _dot_tn(twa, dpw0)), (gwup_ref.at[1], _dot_tn(twa, dpw1)),
               (gaup_ref.at[0], _dot_tn(twa, dpa0)), (gaup_ref.at[1], _dot_tn(twa, dpa1)),
               (gw0_ref, gw0), (ga0_ref, ga0), (gkk_ref, gkk), (gka_ref, gka))

        @pl.when(i == 0)
        def _():
            for ref, val in acc:
                ref[...] = val

        @pl.when(i > 0)
        def _():
            for ref, val in acc:
                ref[...] += val

    col = lambda c, w: pl.BlockSpec((TT, w), lambda i: (i, c))
    one = col(0, RWKV_W)
    sds = jax.ShapeDtypeStruct
    return pl.pallas_call(
        body, name="rwkv_prep_bwd", grid=(R // TT,),
        in_specs=[col(1, RWKV_W), col(3 * RWKV_W // LORA_W, LORA_W)] + [one] * 15 + [
                  _full(wup.shape), _full(aup.shape), _full((2, RWKV_W)), _full((2, RWKV_W)), _full((1, RWKV_W)),
                  _full((1, RWKV_W)), _full((256, 256))],
        out_specs=(pl.BlockSpec((TT, SHIFT_W), lambda i: (i, 0)), _full(wup.shape), _full(aup.shape),
                   _full((2, RWKV_W)), _full((2, RWKV_W)), _full((1, RWKV_W)), _full((1, RWKV_W))),
        out_shape=(sds((R, SHIFT_W), F32), sds(wup.shape, F32), sds(aup.shape, F32), sds((2, RWKV_W), F32),
                   sds((2, RWKV_W), F32), sds((1, RWKV_W), F32), sds((1, RWKV_W), F32)),
        compiler_params=_cp(("arbitrary",)),
    )(shifted, shifted, *cts, wup, aup, w0, a0, k_k, k_a, bd)


def _col_lhs(row, eye_b):
    return eye_b * row.astype(MXU_DTYPE)


def _colsum(x):
    return jnp.sum(x, axis=0, keepdims=True)


def _stacked_segsum(tiles, bd):
    res = _seg_dot(jnp.concatenate(tiles, axis=0), bd)
    return [res[j * HEAD_DIM:(j + 1) * HEAD_DIM] for j in range(len(tiles))]


def _scan_specs(B, T, C, nC):
    def blk(z, col, rev):
        idx = (lambda g: (z, 0, nC - 1 - g, col)) if rev else (lambda g: (z, 0, g, col))
        return pl.BlockSpec((None, B, C, RWKV_W), idx)

    def blk3(col, rev):
        idx = (lambda g: (0, nC - 1 - g, col)) if rev else (lambda g: (0, g, col))
        return pl.BlockSpec((B, C, RWKV_W), idx)

    return blk, blk3


def _scan_fwd_call(w, kt, akk, kk, shifted, eye_b, eye_f, bd, B, T):
    C = min(SCAN_CHUNK, T)
    nC = T // C
    blk, blk3 = _scan_specs(B, T, C, nC)

    def body(w0, kt0, akk0, kk0, v0, r0, w1, kt1, akk1, kk1, v1, r1, eb_ref, ef_ref, bd_ref, y0, y1, st, S):
        @pl.when(pl.program_id(0) == 0)
        def _():
            S[...] = jnp.zeros_like(S)

        st[0] = S[...].astype(MXU_DTYPE)
        dirs = ((w0, kt0, akk0, kk0, v0, r0, y0), (w1, kt1, akk1, kk1, v1, r1, y1))

        def step(s, carry):
            for z in range(2):
                row = s if z == 0 else C - 1 - s
                prev = jnp.maximum(s - 1, 0) if z == 0 else jnp.minimum(C - s, C - 1)
                wr, ktr, akkr, kkr, vr, rr, yr = dirs[z]
                tiles = []
                for b in range(B):
                    Sb = st[s, z * B + b]
                    tiles += [Sb * kkr[b, pl.ds(row, 1), :].astype(MXU_DTYPE),
                              _col_lhs(vr[b, pl.ds(row, 1), :], eb_ref[...]),
                              Sb * rr[b, pl.ds(prev, 1), :].astype(MXU_DTYPE)]
                res = _stacked_segsum(tiles, bd_ref[...])
                for b in range(B):
                    c = z * B + b
                    sab, vb, yb = res[3 * b:3 * b + 3]
                    ld = lambda ref: ref[b, pl.ds(row, 1), :]
                    Sn = S[c] * ld(wr) - sab * ld(akkr) + vb * ld(ktr)
                    S[c] = Sn
                    st[s + 1, c] = Sn.astype(MXU_DTYPE)
                    yr[b, pl.ds(prev, 1), :] = _colsum(ef_ref[...] * yb)
            return carry

        lax.fori_loop(0, C, step, 0, unroll=SCAN_UNROLL)
        for z in range(2):
            last = C - 1 if z == 0 else 0
            rr, yr = dirs[z][5], dirs[z][6]
            res = _stacked_segsum([st[C, z * B + b] * rr[b, last:last + 1, :].astype(MXU_DTYPE) for b in range(B)],
                                  bd_ref[...])
            for b in range(B):
                yr[b, last:last + 1, :] = _colsum(ef_ref[...] * res[b])

    ins, specs = [], []
    for z, rev in ((0, False), (1, True)):
        ins += [w, kt, akk, kk, shifted, shifted]
        specs += [blk(z, 0, rev), blk(z, 0, rev), blk(z, 0, rev), blk3(0, rev), blk3(2, rev), blk3(0, rev)]
    sds = jax.ShapeDtypeStruct
    return pl.pallas_call(
        body, name="scan_fwd", grid=(nC,),
        in_specs=specs + [_full((HEAD_DIM, RWKV_W)), _full((HEAD_DIM, RWKV_W)), _full((256, 256))],
        out_specs=(blk3(0, False), blk3(0, True),
                   pl.BlockSpec((None, C + 1, 2 * B, HEAD_DIM, RWKV_W), lambda g: (g, 0, 0, 0, 0))),
        out_shape=(sds((B, T, RWKV_W), F32), sds((B, T, RWKV_W), F32),
                   sds((nC, C + 1, 2 * B, HEAD_DIM, RWKV_W), MXU_DTYPE)),
        scratch_shapes=[pltpu.VMEM((2 * B, HEAD_DIM, RWKV_W), F32)],
        compiler_params=_cp(("arbitrary",)),
    )(*ins, eye_b, eye_f, bd)


def _scan_bwd_call(w, kt, akk, kk, shifted, v_heads, dys, st, eye_b, eye_f, bd, B, T):
    C = min(SCAN_CHUNK, T)
    nC = T // C
    blk, blk3 = _scan_specs(B, T, C, nC)
    nin = 7

    def body(*refs):
        d0, d1 = refs[:nin], refs[nin:2 * nin]
        st_ref, eb_ref, ef_ref, sel_ref, hm_ref, bd_ref = refs[2 * nin:2 * nin + 6]
        o0, o1 = refs[2 * nin + 6:2 * nin + 12], refs[2 * nin + 12:2 * nin + 18]
        COL, DYC, G = refs[2 * nin + 18:]

        @pl.when(pl.program_id(0) == 0)
        def _():
            G[...] = jnp.zeros_like(G)

        dirs = (d0 + (o0,), d1 + (o1,))

        def column_operands(s, z):
            row = s if z == 0 else C - 1 - s
            _, _, _, kkr, _, _, dyr, _ = dirs[z]
            tiles = []
            for b in range(B):
                tiles += [st_ref[s, z * B + b] * kkr[b, pl.ds(row, 1), :].astype(MXU_DTYPE),
                          _col_lhs(dyr[b, pl.ds(row, 1), :], eb_ref[...])]
            return tiles

        def keep_columns(res, z):
            for b in range(B):
                for k in range(2):
                    COL[k, z * B + b] = res[2 * b + k].astype(MXU_DTYPE)
                DYC[z * B + b] = res[2 * b + 1]

        for z in range(2):
            keep_columns(_stacked_segsum(column_operands(C - 1, z), bd_ref[...]), z)

        def bwd(it, carry):
            s = C - 1 - it
            for z in range(2):
                row = s if z == 0 else C - 1 - s
                wr, ktr, akkr, kkr, vr, rr, dyr, (dw_o, dkt_o, dakk_o, dkk_o, dr_o, dv_o) = dirs[z]
                tiles, Gcs = [], []
                for b in range(B):
                    c = z * B + b
                    Gc = G[c] + DYC[c] * rr[b, pl.ds(row, 1), :]
                    Gb = Gc.astype(MXU_DTYPE)
                    Gcs.append((Gc, Gb))
                    tiles += [Gb * akkr[b, pl.ds(row, 1), :].astype(MXU_DTYPE),
                              Gb * ktr[b, pl.ds(row, 1), :].astype(MXU_DTYPE)]
                res = _stacked_segsum(tiles + column_operands(jnp.maximum(s - 1, 0), z), bd_ref[...])
                for b in range(B):
                    c = z * B + b
                    Gc, Gb = Gcs[b]
                    gab, dvb = res[2 * b], res[2 * b + 1]
                    ld = lambda ref: ref[b, pl.ds(row, 1), :]
                    G[c] = Gc * ld(wr) - gab * ld(kkr)
                    Sb = st_ref[s, c]
                    prods = jnp.concatenate([Gb, st_ref[s + 1, c] * COL[1, c], Gb * Sb, Gb * COL[0, c],
                                             gab.astype(MXU_DTYPE) * Sb], axis=0)
                    v_rows = jnp.concatenate([vr[b, pl.ds(row, 1)][0], jnp.zeros((8, 3 * HEAD_DIM), F32)], axis=1)
                    lhs = jnp.concatenate([sel_ref[...], v_rows], axis=0).astype(MXU_DTYPE)
                    sums = jnp.dot(lhs, prods, preferred_element_type=F32)
                    for k, (ref, sign) in enumerate(((dr_o, 1.0), (dw_o, 1.0), (dakk_o, -1.0), (dkk_o, -1.0))):
                        ref[b, pl.ds(row, 1), :] = sign * sums[k:k + 1, :]
                    dkt_o[b, pl.ds(row, 1), :] = _colsum(sums[8:16] * hm_ref[...])
                    dv_o[b, pl.ds(row, 1), :] = _colsum(ef_ref[...] * dvb)
                keep_columns(res[2 * B:], z)
            return carry

        lax.fori_loop(0, C, bwd, 0, unroll=SCAN_UNROLL)

    ins, specs = [], []
    for z, rev in ((0, True), (1, False)):
        heads = pl.BlockSpec((B, C) + v_heads.shape[2:], (lambda g: (0, nC - 1 - g, 0, 0)) if rev else (lambda g: (0, g, 0, 0)))
        ins += [w, kt, akk, kk, v_heads, shifted, dys]
        specs += [blk(z, 0, rev), blk(z, 0, rev), blk(z, 0, rev), blk3(0, rev), heads, blk3(0, rev), blk3(0, rev)]
    sel = (jnp.arange(8)[:, None] + 1 == (jnp.arange(5 * HEAD_DIM) // HEAD_DIM)[None, :]).astype(F32)
    head_rows = (jnp.arange(RWKV_W // HEAD_DIM)[:, None] == (jnp.arange(RWKV_W) // HEAD_DIM)[None, :]).astype(F32)
    ins += [st, eye_b, eye_f, sel, head_rows, bd]
    specs += [pl.BlockSpec((None, C + 1, 2 * B, HEAD_DIM, RWKV_W), lambda g: (nC - 1 - g, 0, 0, 0, 0)),
              _full((HEAD_DIM, RWKV_W)), _full((HEAD_DIM, RWKV_W)), _full(sel.shape), _full(head_rows.shape),
              _full((256, 256))]
    sds = jax.ShapeDtypeStruct
    out_specs = tuple(blk3(0, True) for _ in range(6)) + tuple(blk3(0, False) for _ in range(6))
    res = pl.pallas_call(
        body, name="scan_bwd", grid=(nC,), in_specs=specs, out_specs=out_specs,
        out_shape=tuple(sds((B, T, RWKV_W), F32) for _ in range(12)),
        scratch_shapes=[pltpu.VMEM((2, 2 * B, HEAD_DIM, RWKV_W), MXU_DTYPE), pltpu.VMEM((2 * B, HEAD_DIM, RWKV_W), F32),
                        pltpu.VMEM((2 * B, HEAD_DIM, RWKV_W), F32)],
        compiler_params=_cp(("arbitrary",)),
    )(*ins)
    return list(res)


def _out_head_call(x2, tgt2, gate, y_att, g_att, y0, y1, shifted, kt, g_rw, w_out, g_post, gn_w, gn_b, r_k, bd, T):
    R = x2.shape[0]
    TT = min(ROW_TILE, T)
    tpe = T // TT

    def body(x_ref, t_ref, gate_ref, ya_ref, ga_ref, y0_ref, y1_ref, r_ref, v_ref, kt_ref, grw_ref, w_ref, gp_ref,
             gnw_ref, gnb_ref, rk_ref, bd_ref,
             loss_o, dy_o, dya_o, dga_o, dys_o, dr_o, dv_o, dkts_o, dgrw_o, dgate_o, gw_o, ggp_o, ggnw_o, ggnb_o, grk_o):
        i = pl.program_id(0)
        bd = bd_ref[...]
        mix = functools.partial(_mix_fn, bd=bd, diff=True)
        (ma, mr), mix_vjp = jax.vjp(mix, ya_ref[...], ga_ref[...], y0_ref[...] + y1_ref[...], r_ref[...], v_ref[...],
                                    kt_ref[0] + kt_ref[1], grw_ref[...], gnw_ref[...], gnb_ref[...], rk_ref[...])
        out = _dot(ma, w_ref[0:ATT_W, :]) + _dot(mr, w_ref[ATT_W:, :])
        loss, loss_vjp = jax.vjp(_loss_fn, out, x_ref[...], t_ref[...], gate_ref[0], gp_ref[...])
        d_out, dy, _, dgate, dgp = loss_vjp(jnp.ones((1, 1), F32))
        dy_o[...] = dy
        dma = _dot_nt(d_out, w_ref[0:ATT_W, :])
        dmr = _dot_nt(d_out, w_ref[ATT_W:, :])
        dya_o[...], dga_o[...], dys_o[...], dr_o[...], dv_o[...], dkts_o[...], dgrw_o[...], dgnw, dgnb, drk = \
            mix_vjp((dma, dmr))
        gw = jnp.concatenate([_dot_tn(ma, d_out), _dot_tn(mr, d_out)], axis=0)
        acc = ((loss_o, jnp.broadcast_to(loss, (8, 128))), (gw_o, gw), (ggp_o, dgp), (ggnw_o, dgnw), (ggnb_o, dgnb),
               (grk_o, drk))

        @pl.when(i == 0)
        def _():
            for ref, val in acc:
                ref[...] = val

        @pl.when(i > 0)
        def _():
            for ref, val in acc:
                ref[...] += val

        @pl.when(i % tpe == 0)
        def _():
            dgate_o[0] = dgate

        @pl.when(i % tpe > 0)
        def _():
            dgate_o[0] += dgate

    row = lambda w, c=0: pl.BlockSpec((TT, w), lambda i: (i, c))
    two = pl.BlockSpec((2, TT, RWKV_W), lambda i: (0, i, 0))
    per_ex = pl.BlockSpec((1, 1, D_MODEL), lambda i: (i // tpe, 0, 0))
    sds = jax.ShapeDtypeStruct
    r512 = sds((R, RWKV_W), F32)
    return pl.pallas_call(
        body, name="out_head", grid=(R // TT,),
        in_specs=[row(D_MODEL), row(D_MODEL), per_ex, row(ATT_W), row(ATT_W), row(RWKV_W), row(RWKV_W), row(RWKV_W, 0),
                  row(RWKV_W, 2), two,
                  row(RWKV_W), _full(w_out.shape), _full((1, D_MODEL)), _full((1, RWKV_W)), _full((1, RWKV_W)),
                  _full((1, RWKV_W)), _full((256, 256))],
        out_specs=(_full((8, 128)), row(D_MODEL), row(ATT_W), row(ATT_W), row(RWKV_W), row(RWKV_W), row(RWKV_W),
                   row(RWKV_W), row(RWKV_W), per_ex, _full((D_MODEL, D_MODEL)), _full((1, D_MODEL)), _full((1, RWKV_W)),
                   _full((1, RWKV_W)), _full((1, RWKV_W))),
        out_shape=(sds((8, 128), F32), sds((R, D_MODEL), F32), r512, r512, r512, r512, r512, r512, r512,
                   sds((R // T, 1, D_MODEL), F32), sds((D_MODEL, D_MODEL), F32), sds((1, D_MODEL), F32),
                   sds((1, RWKV_W), F32), sds((1, RWKV_W), F32), sds((1, RWKV_W), F32)),
        compiler_params=_cp(("arbitrary",)),
    )(x2, tgt2, gate, y_att, g_att, y0, y1, shifted, shifted, kt, g_rw, w_out, g_post, gn_w, gn_b, r_k, bd)


def _in_proj_bwd_call(x2, dy, shift, scale, g_pre, w_in, qg, kg, cos, sin, bd, q_raw, k_raw, dqr, dkp, dvp,
                      d_gatt, d_rin, d_grw, T):
    R = x2.shape[0]
    TT = min(SHIFT_TILE, T)
    tpe = T // TT

    def body(x_ref, dy_ref, sh_ref, sc_ref, gp_ref, w_ref, qg_ref, kg_ref, cos_ref, sin_ref, bd_ref, q_ref, k_ref,
             dqr_ref, dkp_ref, dvp_ref, dga_ref, drin_ref, dgrw_ref,
             dx_o, dproj_o, dsh_o, dsc_o, ggp_o, gqg_o, gkg_o):
        i = pl.program_id(0)
        cos, sin, bd = cos_ref[...], sin_ref[...], bd_ref[...]
        left = lax.broadcasted_iota(jnp.int32, (1, KV_W), 1) < HEAD_DIM

        def kv_grad(ref):
            a = ref[0] + ref[1]
            b = ref[2] + ref[3]
            return jnp.where(left, a + pltpu.roll(a, HEAD_DIM, 1), b + pltpu.roll(b, HEAD_DIM, 1))

        qfn = functools.partial(_qk_fn, cos=jnp.tile(cos, (1, 4)), sin=jnp.tile(sin, (1, 4)), bd=bd, scale=ATT_SCALE,
                                diff=True)
        _, q_vjp = jax.vjp(qfn, q_ref[...], qg_ref[...])
        dq, gqg = q_vjp(dqr_ref[...])
        kfn = functools.partial(_qk_fn, cos=cos, sin=sin, bd=bd, scale=1.0, diff=True)
        _, k_vjp = jax.vjp(kfn, k_ref[...], kg_ref[...])
        dk, gkg = k_vjp(kv_grad(dkp_ref))
        pieces = ((C_Q, C_K, dq), (C_K, C_V, dk), (C_V, C_GA, kv_grad(dvp_ref)), (C_GA, C_RIN, dga_ref[...]),
                  (C_RIN, C_GRW, drin_ref[...]), (C_GRW, C_END, dgrw_ref[...]))
        dh = jnp.zeros((TT, D_MODEL), F32)
        for c0, c1, val in pieces:
            vb = val.astype(MXU_DTYPE)
            dproj_o[:, c0:c1] = vb
            dh = dh + _dot(vb, w_ref[c0:c1, :])
        _, pre_vjp = jax.vjp(_pre_fn, x_ref[...], sh_ref[0], sc_ref[0], gp_ref[...])
        dx, dsh, dsc, ggp = pre_vjp(dh)
        dx_o[...] = dx + dy_ref[...]
        acc = ((ggp_o, ggp), (gqg_o, gqg), (gkg_o, gkg))

        @pl.when(i == 0)
        def _():
            for ref, val in acc:
                ref[...] = val

        @pl.when(i > 0)
        def _():
            for ref, val in acc:
                ref[...] += val

        @pl.when(i % tpe == 0)
        def _():
            dsh_o[0] = dsh
            dsc_o[0] = dsc

        @pl.when(i % tpe > 0)
        def _():
            dsh_o[0] += dsh
            dsc_o[0] += dsc

    row = lambda w: pl.BlockSpec((TT, w), lambda i: (i, 0))
    per_ex = pl.BlockSpec((1, 1, D_MODEL), lambda i: (i // tpe, 0, 0))
    tab = pl.BlockSpec((TT, KV_W), lambda i: (i % tpe, 0))
    pad = pl.BlockSpec((4, TT, KV_W), lambda i: (0, i, 0))
    sds = jax.ShapeDtypeStruct
    nb = R // T
    return pl.pallas_call(
        body, name="in_proj_bwd", grid=(R // TT,),
        in_specs=[row(D_MODEL), row(D_MODEL), per_ex, per_ex, _full((1, D_MODEL)), _full(w_in.shape), _full((1, ATT_W)),
                  _full((1, KV_W)), tab, tab, _full((256, 256)), row(ATT_W), row(KV_W), row(ATT_W), pad, pad,
                  row(ATT_W), row(SHIFT_W), row(RWKV_W)],
        out_specs=(row(D_MODEL), row(C_END), per_ex, per_ex, _full((1, D_MODEL)), _full((1, ATT_W)), _full((1, KV_W))),
        out_shape=(sds((R, D_MODEL), F32), sds((R, C_END), MXU_DTYPE), sds((nb, 1, D_MODEL), F32),
                   sds((nb, 1, D_MODEL), F32), sds((1, D_MODEL), F32), sds((1, ATT_W), F32), sds((1, KV_W), F32)),
        compiler_params=_cp(("arbitrary",)),
    )(x2, dy, shift, scale, g_pre, w_in, qg, kg, cos, sin, bd, q_raw, k_raw, dqr, dkp, dvp, d_gatt, d_rin, d_grw)


def _w_in_grad_call(hb, dproj):
    R = hb.shape[0]
    TT = min(W_GRAD_ROWS, R)
    CB = 1152
    last = R // TT - 1

    def body(h_ref, d_ref, o_ref, acc):
        g = _dot_tn(h_ref[...], d_ref[...])

        @pl.when(pl.program_id(1) == 0)
        def _():
            acc[...] = g

        @pl.when(pl.program_id(1) > 0)
        def _():
            acc[...] += g

        @pl.when(pl.program_id(1) == last)
        def _():
            o_ref[...] = acc[...].astype(o_ref.dtype)

    return pl.pallas_call(
        body, name="w_in_grad", grid=(C_END // CB, R // TT),
        in_specs=[pl.BlockSpec((TT, D_MODEL), lambda j, i: (i, 0)), pl.BlockSpec((TT, CB), lambda j, i: (i, j))],
        out_specs=pl.BlockSpec((D_MODEL, CB), lambda j, i: (0, j)),
        out_shape=jax.ShapeDtypeStruct((D_MODEL, C_END), MXU_DTYPE),
        scratch_shapes=[pltpu.VMEM((D_MODEL, CB), F32)], compiler_params=_cp(("arbitrary", "arbitrary")),
    )(hb, dproj)


def _adam_refs(p_ref, w_ref, m_ref, v_ref, g_o, d_o, m_o, v_o):
    g = p_ref[0].astype(F32)
    for j in range(1, p_ref.shape[0]):
        g = g + p_ref[j].astype(F32)
    m2 = ADAM_B1 * m_ref[...] + (1.0 - ADAM_B1) * g
    v2 = ADAM_B2 * v_ref[...] + (1.0 - ADAM_B2) * jnp.square(g)
    m_hat = m2 / (1.0 - ADAM_B1 ** ADAM_STEP)
    v_hat = v2 / (1.0 - ADAM_B2 ** ADAM_STEP)
    g_o[...] = g
    d_o[...] = -ADAM_LR * (m_hat / (jnp.sqrt(v_hat) + ADAM_EPS) + ADAM_WD * w_ref[...])
    m_o[...] = m2
    v_o[...] = v2


def _adam_small_call(items, name):
    n = len(items)

    def body(*refs):
        for k in range(n):
            _adam_refs(*refs[4 * k:4 * k + 4], *refs[4 * n + 4 * k:4 * n + 4 * k + 4])

    out_shape = tuple(jax.ShapeDtypeStruct(w.shape, F32) for _, w, _, _ in items for _ in range(4))
    out = pl.pallas_call(body, name=name, out_shape=out_shape)(*[a for item in items for a in item])
    return [out[4 * k:4 * k + 4] for k in range(n)]


def _adam_call(parts, w, m, v, name, row_tile=None):
    P, M, N = parts.shape
    TM = M if row_tile is None else row_tile

    def body(*refs):
        _adam_refs(*refs)

    blk = pl.BlockSpec((TM, N), lambda i: (i, 0))
    return pl.pallas_call(
        body, name=name, grid=(M // TM,),
        in_specs=[pl.BlockSpec((P, TM, N), lambda i: (0, i, 0)), blk, blk, blk], out_specs=(blk,) * 4,
        out_shape=(jax.ShapeDtypeStruct((M, N), F32),) * 4, compiler_params=_cp(("arbitrary",)),
    )(parts, w, m, v)


_SMALL_ROWS = 136


def _pack_small(taps, w_up, w0, a_up, a0):
    flat = jnp.concatenate([taps.reshape(-1), w_up.reshape(-1), w0.reshape(-1), a_up.reshape(-1), a0.reshape(-1)])
    return jnp.pad(flat, (0, _SMALL_ROWS * 128 - flat.shape[0])).reshape(_SMALL_ROWS, 128)


def _unpack_small(packed):
    n = packed.shape[0]
    flat = packed.reshape(n, -1)
    out, o = [], 0
    for shape in ((3, 208), (2, 64, 64), (2, 64), (2, 64, 64), (2, 64)):
        size = 1
        for s in shape:
            size *= s
        out.append(flat[:, o:o + size].reshape((n,) + shape))
        o += size
    return out


def _cols_to_full(blocks):
    nd = blocks.ndim
    moved = jnp.moveaxis(blocks, 0, nd - 2)
    return moved.reshape(moved.shape[:-2] + (moved.shape[-2] * moved.shape[-1],))


def _full_to_cols(full):
    k = full.shape[-1] // NDEV
    return jnp.moveaxis(full.reshape(full.shape[:-1] + (NDEV, k)), -2, 0)


_REP_SIZES = (("g_pre", 1024), ("q_norm_g", 64), ("k_norm_g", 64), ("k_k", 512), ("k_a", 512), ("r_k", 512),
              ("gn_w", 512), ("gn_b", 512), ("g_post", 1024))
_REP_ROWS = 40


def kernel(x, c, w_ada, b_ada, g_pre, w_in, q_norm_g, k_norm_g, shift_taps, w_up, w0, a_up, a0, k_k, k_a, r_k, gn_w, gn_b, w_out, g_post, loss_target, m_w_ada, m_b_ada, m_g_pre, m_w_in, m_q_norm_g, m_k_norm_g, m_shift_taps, m_w_up, m_w0, m_a_up, m_a0, m_k_k, m_k_a, m_r_k, m_gn_w, m_gn_b, m_w_out, m_g_post, v_w_ada, v_b_ada, v_g_pre, v_w_in, v_q_norm_g, v_k_norm_g, v_shift_taps, v_w_up, v_w0, v_a_up, v_a0, v_k_k, v_k_a, v_r_k, v_gn_w, v_gn_b, v_w_out, v_g_post):
    B, T, _ = x.shape
    R = B * T
    me = 4 * lax.axis_index("x") + 2 * lax.axis_index("y") + lax.axis_index("c")
    x2 = x.reshape(R, D_MODEL)
    tgt2 = loss_target.reshape(R, D_MODEL)

    seg = jnp.arange(256) // HEAD_DIM
    bd = (seg[:, None] == seg[None, :]).astype(MXU_DTYPE)
    eye = (jnp.arange(HEAD_DIM)[:, None] == (jnp.arange(RWKV_W) % HEAD_DIM)[None, :])
    eye_b, eye_f = eye.astype(MXU_DTYPE), eye.astype(F32)
    cos, sin = _rope_tables(T)

    c_g, w_in_g, w_out_g, small_g = _exchange(
        [c, w_in[0].T.astype(MXU_DTYPE), w_out[0].astype(MXU_DTYPE),
         _pack_small(shift_taps[0], w_up[0], w0[0], a_up[0], a0[0])], ["all"] * 4, "gather_params")
    c_all = c_g.reshape(NDEV * B, D_MODEL)
    w_in_f = w_in_g.reshape(C_END, D_MODEL)
    w_out_f = w_out_g.reshape(D_MODEL, D_MODEL)
    taps_b, w_up_b, w0_b, a_up_b, a0_b = _unpack_small(small_g)
    taps_f = jnp.pad(_cols_to_full(taps_b), ((0, 5), (0, 0)))
    w_up_f, a_up_f = _cols_to_full(w_up_b), _cols_to_full(a_up_b)
    w0_f, a0_f = _cols_to_full(w0_b), _cols_to_full(a0_b)
    wup_pad = jnp.pad(w_up_f, ((0, 0), (0, 64), (0, 0))).astype(MXU_DTYPE)
    aup_pad = jnp.pad(a_up_f, ((0, 0), (64, 0), (0, 0))).astype(MXU_DTYPE)

    ncol = w_ada.shape[2]
    b_cols = lax.dynamic_slice(b_ada, (0, me * ncol), (1, ncol))
    mod_cols = _mod_call(c_all, w_ada[0].astype(MXU_DTYPE), b_cols)
    (mod_g,) = _exchange([mod_cols], ["all"], "gather_mod")
    mod = lax.dynamic_slice(_cols_to_full(mod_g), (me * B, 0), (B, 3 * D_MODEL))
    shift, scale, gate = [mod[:, j * D_MODEL:(j + 1) * D_MODEL].reshape(B, 1, D_MODEL) for j in range(3)]

    qg = jnp.tile(q_norm_g, (1, ATT_W // HEAD_DIM))
    kg = jnp.tile(k_norm_g, (1, KV_W // HEAD_DIM))
    rk_row = r_k.reshape(1, RWKV_W)

    hb, qr, kpad, vpad, q_raw, k_raw, g_att, rin, g_rw = _in_proj_call(
        x2, shift, scale, g_pre, w_in_f, qg, kg, cos, sin, bd, T)
    y_att = _att_fwd_call(qr, kpad, vpad, B, T)
    shifted, v_rows = _shift_fwd_call(rin, taps_f, T)
    w_s, kt_s, akk_s, kk_s = _rwkv_prep_call(shifted, wup_pad, aup_pad, w0_f, a0_f, k_k, k_a, bd, T)
    sh3 = shifted.reshape(B, T, SHIFT_W)
    r4 = lambda a: a.reshape(2, B, T, RWKV_W)
    y0, y1, st = _scan_fwd_call(r4(w_s), r4(kt_s), r4(akk_s), kk_s.reshape(B, T, RWKV_W), sh3, eye_b, eye_f, bd, B, T)

    (loss_blk, dy, d_yatt, d_gatt, d_ys, d_r2, d_v2, d_kts, d_grw, d_gate, g_wout, g_gpost, g_gnw, g_gnb,
     g_rk) = _out_head_call(x2, tgt2, gate, y_att, g_att, y0.reshape(R, RWKV_W), y1.reshape(R, RWKV_W), shifted, kt_s,
                            g_rw, w_out_f, g_post, gn_w, gn_b, rk_row, bd, T)
    v_heads = v_rows.reshape(B, T, RWKV_W // HEAD_DIM, 2 * HEAD_DIM)
    scan_cts = _scan_bwd_call(r4(w_s), r4(kt_s), r4(akk_s), kk_s.reshape(B, T, RWKV_W), sh3, v_heads,
                              d_ys.reshape(B, T, RWKV_W), st, eye_b, eye_f, bd, B, T)
    scan_cts = [a.reshape(R, RWKV_W) for a in scan_cts]
    d_shifted, g_wup, g_aup, g_w0, g_a0, g_kk, g_ka = _rwkv_prep_bwd_call(
        shifted, scan_cts + [d_r2, d_v2, d_kts], wup_pad, aup_pad, w0_f, a0_f, k_k, k_a, bd, T)
    d_rin, g_taps = _shift_bwd_call(rin, d_shifted, taps_f, T)
    dqr, dkp, dvp = _att_bwd_call(qr, kpad, vpad, d_yatt, B, T)
    grad_x, dproj, d_shift, d_scale, g_gpre, g_qg, g_kg = _in_proj_bwd_call(
        x2, dy, shift, scale, g_pre, w_in_f, qg, kg, cos, sin, bd, q_raw, k_raw, dqr, dkp, dvp, d_gatt, d_rin, d_grw, T)
    g_win = _w_in_grad_call(hb, dproj)

    rep = jnp.concatenate([g_gpre.reshape(-1), g_qg.reshape(-1, HEAD_DIM).sum(0), g_kg.reshape(-1, HEAD_DIM).sum(0),
                           g_kk.reshape(-1), g_ka.reshape(-1), g_rk.reshape(-1), g_gnw.reshape(-1), g_gnb.reshape(-1),
                           g_gpost.reshape(-1), loss_blk[0, :1]])
    rep = jnp.pad(rep, (0, _REP_ROWS * 128 - rep.shape[0])).reshape(_REP_ROWS, 128)
    dmod = jnp.concatenate([d_shift, d_scale, d_gate], axis=2).reshape(B, 3 * D_MODEL)
    small_parts = jax.vmap(_pack_small)(_full_to_cols(g_taps[:3]), _full_to_cols(g_wup[:, :64, :]), _full_to_cols(g_w0),
                                        _full_to_cols(g_aup[:, 64:, :]), _full_to_cols(g_a0))
    by_core = lambda a: jnp.swapaxes(a.reshape((NDEV // 2, 2) + a.shape[1:]), 0, 1).astype(MXU_DTYPE)
    s_win, s_wout = _pair_sum_call(
        [by_core(_full_to_cols(g_win)), by_core(g_wout.reshape(NDEV, D_MODEL // NDEV, D_MODEL))], "reduce_pair")
    p_win, p_wout, p_small, dmod_g, rep_g = _exchange(
        [s_win, s_wout, small_parts, dmod, rep], ["chips", "chips", "scatter", "all", "all"], "reduce_grads")
    dmod_all = dmod_g.reshape(NDEV * B, 3 * D_MODEL)
    g_wada = _wada_grad_call(c_all, lax.dynamic_slice(dmod_all, (0, me * ncol), (NDEV * B, ncol)))

    res, small = {}, []

    def adam(name, parts, w, m, v, row_tile=None, alone=False):
        two_d = (-1, w.shape[-1])
        item = (parts.reshape((parts.shape[0],) + w.reshape(two_d).shape), w.reshape(two_d), m.reshape(two_d),
                v.reshape(two_d))
        if alone:
            res[name] = [o.reshape(w.shape) for o in _adam_call(*item, "adam_" + name, row_tile)]
        else:
            small.append((name, w.shape, item))

    adam("w_ada", g_wada[None], w_ada, m_w_ada, v_w_ada, alone=True)
    adam("b_ada", dmod_all.reshape(NDEV * B, 1, 3 * D_MODEL), b_ada, m_b_ada, v_b_ada)
    adam("w_in", p_win, w_in, m_w_in, v_w_in, 512, alone=True)
    adam("w_out", p_wout, w_out, m_w_out, v_w_out, alone=True)
    taps_p, wup_p, w0_p, aup_p, a0_p = _unpack_small(p_small)
    adam("shift_taps", taps_p, shift_taps, m_shift_taps, v_shift_taps)
    adam("w_up", wup_p, w_up, m_w_up, v_w_up)
    adam("w0", w0_p, w0, m_w0, v_w0)
    adam("a_up", aup_p, a_up, m_a_up, v_a_up)
    adam("a0", a0_p, a0, m_a0, v_a0)
    rep_flat = rep_g.reshape(NDEV, -1)
    off = 0
    given = dict(g_pre=(g_pre, m_g_pre, v_g_pre), q_norm_g=(q_norm_g, m_q_norm_g, v_q_norm_g),
                 k_norm_g=(k_norm_g, m_k_norm_g, v_k_norm_g), k_k=(k_k, m_k_k, v_k_k), k_a=(k_a, m_k_a, v_k_a),
                 r_k=(r_k, m_r_k, v_r_k), gn_w=(gn_w, m_gn_w, v_gn_w), gn_b=(gn_b, m_gn_b, v_gn_b),
                 g_post=(g_post, m_g_post, v_g_post))
    for name, size in _REP_SIZES:
        adam(name, rep_flat[:, off:off + size], *given[name])
        off += size
    for (name, shape, _), out in zip(small, _adam_small_call([item for _, _, item in small], "adam_small")):
        res[name] = [o.reshape(shape) for o in out]

    loss = jnp.sum(rep_flat[:, off])
    order = ["w_ada", "b_ada", "g_pre", "w_in", "q_norm_g", "k_norm_g", "shift_taps", "w_up", "w0", "a_up", "a0", "k_k",
             "k_a", "r_k", "gn_w", "gn_b", "w_out", "g_post"]
    return (loss, grad_x.reshape(B, T, D_MODEL), *[res[n][0] for n in order], *[res[n][1] for n in order],
            *[res[n][2] for n in order], *[res[n][3] for n in order])
```

```python
import functools

import jax
import jax.numpy as jnp
from jax import lax
from jax.experimental import pallas as pl
from jax.experimental.pallas import tpu as pltpu

F32 = jnp.float32
MXU_DTYPE = jnp.bfloat16
MESH = pl.DeviceIdType.MESH
NDEV = 8

D_MODEL = 1024
HEAD_DIM = 64
ATT_W = 512
KV_W = 128
RWKV_W = 512
LORA_W = 128
SHIFT_W = 3 * RWKV_W + LORA_W
GRID_W = 64
ROPE_THETA = 10000.0
DECAY_SCALE = 0.6065306597126334
NORM_EPS = 1e-6
GN_EPS = 64e-5
L2_EPS = 1e-12
ATT_SCALE = HEAD_DIM ** -0.5
C_Q, C_K, C_V, C_GA, C_RIN, C_GRW, C_END = 0, 512, 640, 768, 1280, 2944, 3456

ADAM_LR, ADAM_B1, ADAM_B2, ADAM_EPS, ADAM_WD, ADAM_STEP = 0.001, 0.9, 0.999, 1e-08, 0.01, 10

ROW_TILE = 256
SHIFT_TILE = 512
W_GRAD_ROWS = 4096
ATT_TILE_FWD = 256
ATT_TILE_BWD = 1024
SCAN_CHUNK = 64
SCAN_UNROLL = 16
VMEM_LIMIT = 56 * 1024 * 1024


def _cp(sem=None):
    return pltpu.CompilerParams(dimension_semantics=sem, vmem_limit_bytes=VMEM_LIMIT)


def _dot(a, b, dims=(((1,), (0,)), ((), ()))):
    return lax.dot_general(a.astype(MXU_DTYPE), b.astype(MXU_DTYPE), dims, preferred_element_type=F32)


def _dot_nt(a, b):
    return _dot(a, b, (((1,), (1,)), ((), ())))


def _dot_tn(a, b):
    return _dot(a, b, (((0,), (0,)), ((), ())))


def _seg_dot(xb, bd):
    n = xb.shape[1]
    if n <= 256:
        return jnp.dot(xb, bd[:n, :n], preferred_element_type=F32)
    parts = [jnp.dot(xb[:, c:c + 256], bd, preferred_element_type=F32) for c in range(0, n, 256)]
    return jnp.concatenate(parts, axis=1)


def _segsum_raw(x, bd):
    rows = x.shape[0]
    hi = x.astype(MXU_DTYPE)
    lo = (x - hi.astype(F32)).astype(MXU_DTYPE)
    both = _seg_dot(jnp.concatenate([hi, lo], axis=0), bd)
    return both[:rows] + both[rows:]


@jax.custom_vjp
def _segsum_d(x, bd):
    return _segsum_raw(x, bd)


def _segsum_d_fwd(x, bd):
    return _segsum_raw(x, bd), bd


def _segsum_d_bwd(bd, ct):
    return _segsum_raw(ct, bd), jnp.zeros_like(bd)


_segsum_d.defvjp(_segsum_d_fwd, _segsum_d_bwd)


def _rope_tables(T):
    t = jnp.arange(T, dtype=F32)
    row = jnp.floor(t / GRID_W)
    col = t - row * GRID_W
    n_freq = HEAD_DIM // 4
    inv_freq = ROPE_THETA ** (-jnp.arange(n_freq, dtype=F32) / n_freq)
    d = jnp.arange(HEAD_DIM)
    pos = jnp.where((d < HEAD_DIM // 2)[None, :], row[:, None], col[:, None])
    ang = pos * inv_freq[d % n_freq][None, :]
    sign = jnp.where((d % 32) < 16, -1.0, 1.0).astype(F32)[None, :]
    cos = jnp.cos(ang)
    sin = jnp.sin(ang) * sign
    return jnp.tile(cos, (1, 2)), jnp.tile(sin, (1, 2))


def _rope_raw(x, cos, sin):
    n = x.shape[1]
    lane = lax.broadcasted_iota(jnp.int32, (1, n), 1)
    first = (lane % 32) < 16
    partner = jnp.where(first, pltpu.roll(x, n - 16, 1), pltpu.roll(x, 16, 1))
    return x * cos + partner * sin


@jax.custom_vjp
def _rope_d(x, cos, sin):
    return _rope_raw(x, cos, sin)


def _rope_d_fwd(x, cos, sin):
    return _rope_raw(x, cos, sin), (cos, sin)


def _rope_d_bwd(res, ct):
    cos, sin = res
    return _rope_raw(ct, cos, -sin), jnp.zeros_like(cos), jnp.zeros_like(sin)


_rope_d.defvjp(_rope_d_fwd, _rope_d_bwd)


def _rms(x, g):
    return x * lax.rsqrt(jnp.mean(x * x, axis=-1, keepdims=True) + NORM_EPS) * g


def _pre_fn(x, shift, scale, g_pre):
    return _rms(x, g_pre) * (1.0 + scale) + shift


def _qk_fn(q, g, cos, sin, bd, scale, diff):
    segsum = _segsum_d if diff else _segsum_raw
    rope = _rope_d if diff else _rope_raw
    qn = q * lax.rsqrt(segsum(q * q, bd) * (1.0 / HEAD_DIM) + NORM_EPS) * g
    return rope(qn, cos, sin) * scale


def _silu(x):
    return x * jax.nn.sigmoid(x)


def _rwkv_pw(k, pw0, pw1, pa0, pa1, w0, a0, k_k, k_a, bd, diff):
    segsum = _segsum_d if diff else _segsum_raw
    kk = k * k_k
    kk = kk * lax.rsqrt(segsum(kk * kk, bd) + L2_EPS)
    ws, kts, akks = [], [], []
    for z, (pw, pa) in enumerate(((pw0, pa0), (pw1, pa1))):
        w = jnp.exp(-DECAY_SCALE * jax.nn.sigmoid(w0[z:z + 1, :] + pw))
        a = jax.nn.sigmoid(a0[z:z + 1, :] + pa)
        ws.append(w)
        kts.append(k * (1.0 + (a - 1.0) * k_a))
        akks.append(a * kk)
    return ws[0], ws[1], kts[0], kts[1], akks[0], akks[1], kk


def _mix_fn(y_att, g_att, ys, r, v, kts, g_rw, gn_w, gn_b, r_k, bd, diff):
    segsum = _segsum_d if diff else _segsum_raw
    mu = segsum(ys, bd) * (1.0 / HEAD_DIM)
    d = ys - mu
    var = segsum(d * d, bd) * (1.0 / HEAD_DIM)
    yn = d * lax.rsqrt(var + GN_EPS) * gn_w + gn_b
    bonus = segsum(r * kts * r_k, bd) * v
    return y_att * _silu(g_att), (yn + bonus) * _silu(g_rw)


def _loss_fn(out, x, tgt, gate, g_post):
    e = x + gate * _rms(out, g_post) - tgt
    s = jnp.sum(e * e, axis=1, keepdims=True)
    return jnp.sum(s, axis=0, keepdims=True) * (0.5 / D_MODEL)


def _exchange(arrays, modes, name):
    n = len(arrays)
    out_shape = tuple(
        jax.ShapeDtypeStruct(((NDEV,) + tuple(a.shape)) if mode == "all" else tuple(a.shape), a.dtype)
        for a, mode in zip(arrays, modes))
    chips = (4, 2, 6)

    def body(*refs):
        ins, outs = refs[:n], refs[n:2 * n]
        send_sems, recv_sems, local_sems = refs[2 * n:]
        ix, iy, ic = lax.axis_index("x"), lax.axis_index("y"), lax.axis_index("c")
        me = 4 * ix + 2 * iy + ic

        def peer(m):
            px = 1 - ix if (m >> 2) & 1 else ix
            py = 1 - iy if (m >> 1) & 1 else iy
            pc = 1 - ic if m & 1 else ic
            return (px, py, pc), 4 * px + 2 * py + pc

        def copy(k, j, src_ref, slot, to):
            return pltpu.make_async_remote_copy(src_ref=src_ref, dst_ref=outs[k].at[slot], send_sem=send_sems.at[k, j],
                                                recv_sem=recv_sems.at[k, j], device_id=to, device_id_type=MESH)

        local, sends, arrivals, forwards = [], [], [], []
        for k in range(n):
            if modes[k] == "scatter":
                local.append(pltpu.make_async_copy(ins[k].at[me], outs[k].at[me], local_sems.at[k]))
                for m in range(1, NDEV):
                    to, p = peer(m)
                    sends.append(copy(k, m - 1, ins[k].at[p], me, to))
                    arrivals.append(copy(k, m - 1, ins[k].at[p], p, to))
            elif modes[k] == "chips":
                mine = me // 2
                local.append(pltpu.make_async_copy(ins[k].at[mine], outs[k].at[mine], local_sems.at[k]))
                for j, m in enumerate(chips):
                    to, p = peer(m)
                    sends.append(copy(k, j, ins[k].at[p // 2], mine, to))
                    arrivals.append(copy(k, j, ins[k].at[p // 2], p // 2, to))
            else:
                local.append(pltpu.make_async_copy(ins[k], outs[k].at[me], local_sems.at[k]))
                sib, sib_slot = peer(1)
                sends.append(copy(k, 0, ins[k], me, sib))
                for j, m in enumerate(chips):
                    to, p = peer(m)
                    sends.append(copy(k, 1 + j, ins[k], me, to))
                    forwards.append((copy(k, 1 + j, ins[k], p, to), copy(k, 4 + j, outs[k].at[p], p, sib)))
                    arrivals.append(copy(k, 4 + j, ins[k], peer(m ^ 1)[1], sib))
                arrivals.append(copy(k, 0, ins[k], sib_slot, sib))
        for cp in local + sends:
            cp.start()
        for arrived, onward in forwards:
            arrived.wait_recv()
            onward.start()
        for cp in arrivals:
            cp.wait_recv()
        for cp in sends + [onward for _, onward in forwards]:
            cp.wait_send()
        for cp in local:
            cp.wait()

    any_spec = pl.BlockSpec(memory_space=pl.ANY)
    return pl.pallas_call(
        body, name=name, out_shape=out_shape,
        in_specs=[any_spec] * n, out_specs=tuple([any_spec] * n),
        scratch_shapes=[pltpu.SemaphoreType.DMA((n, NDEV - 1)), pltpu.SemaphoreType.DMA((n, NDEV - 1)),
                        pltpu.SemaphoreType.DMA((n,))],
    )(*arrays)


def _pair_sum_call(parts, name):
    n = len(parts)

    def body(*refs):
        in_r, out_r, mine_r, land_r = (refs[j * n:(j + 1) * n] for j in range(4))
        send_sems, recv_sems, local_sems = refs[4 * n:]
        core = lax.axis_index("c")
        sibling = (lax.axis_index("x"), lax.axis_index("y"), 1 - core)
        local = [pltpu.make_async_copy(in_r[k].at[core], mine_r[k], local_sems.at[k]) for k in range(n)]
        swaps = [pltpu.make_async_remote_copy(src_ref=in_r[k].at[1 - core], dst_ref=land_r[k], send_sem=send_sems.at[k],
                                              recv_sem=recv_sems.at[k], device_id=sibling, device_id_type=MESH)
                 for k in range(n)]
        for cp in local + swaps:
            cp.start()
        for k in range(n):
            local[k].wait()
            swaps[k].wait()
            out_r[k][...] = (mine_r[k][...].astype(F32) + land_r[k][...].astype(F32)).astype(out_r[k].dtype)

    halves = [jax.ShapeDtypeStruct(a.shape[1:], a.dtype) for a in parts]
    return pl.pallas_call(
        body, name=name, out_shape=tuple(halves), in_specs=[pl.BlockSpec(memory_space=pl.ANY)] * n,
        scratch_shapes=[pltpu.VMEM(h.shape, h.dtype) for h in halves] * 2 + [pltpu.SemaphoreType.DMA((n,))] * 3,
        compiler_params=pltpu.CompilerParams(vmem_limit_bytes=VMEM_LIMIT),
    )(*parts)


def _mod_call(c_all, w_ada, b_cols):
    def body(c_ref, w_ref, b_ref, o_ref):
        o_ref[...] = _dot(_silu(c_ref[...]), w_ref[...]) + b_ref[...]

    return pl.pallas_call(body, name="mod_fwd",
                          out_shape=jax.ShapeDtypeStruct((c_all.shape[0], w_ada.shape[1]), F32))(c_all, w_ada, b_cols)


def _wada_grad_call(c_all, dmod_cols):
    def body(c_ref, d_ref, o_ref):
        o_ref[...] = _dot_tn(_silu(c_ref[...]), d_ref[...])

    return pl.pallas_call(body, name="w_ada_grad",
                          out_shape=jax.ShapeDtypeStruct((c_all.shape[1], dmod_cols.shape[1]), F32))(c_all, dmod_cols)


def _full(shape):
    nd = len(shape)
    return pl.BlockSpec(shape, lambda *_: (0,) * nd)


def _in_proj_call(x2, shift, scale, g_pre, w_in, qg, kg, cos, sin, bd, T):
    R = x2.shape[0]
    TT = min(SHIFT_TILE, T)
    tpe = T // TT

    def body(x_ref, sh_ref, sc_ref, gp_ref, w_ref, qg_ref, kg_ref, cos_ref, sin_ref, bd_ref,
             hb_ref, qr_ref, kpad_ref, vpad_ref, qraw_ref, kraw_ref, gatt_ref, rin_ref, grw_ref):
        h = _pre_fn(x_ref[...], sh_ref[0], sc_ref[0], gp_ref[...])
        hb = h.astype(MXU_DTYPE)
        hb_ref[...] = hb

        def proj(c0, c1):
            return _dot_nt(hb, w_ref[c0:c1, :])

        q = proj(C_Q, C_K)
        k = proj(C_K, C_V)
        v = proj(C_V, C_GA)
        gatt_ref[...] = proj(C_GA, C_RIN)
        rin_ref[...] = proj(C_RIN, C_GRW)
        grw_ref[...] = proj(C_GRW, C_END)
        qraw_ref[...] = q
        kraw_ref[...] = k
        cos, sin, bd = cos_ref[...], sin_ref[...], bd_ref[...]
        qr = _qk_fn(q, qg_ref[...], jnp.tile(cos, (1, 4)), jnp.tile(sin, (1, 4)), bd, ATT_SCALE, False)
        qr_ref[...] = qr.astype(MXU_DTYPE)
        kr = _qk_fn(k, kg_ref[...], cos, sin, bd, 1.0, False)
        left = lax.broadcasted_iota(jnp.int32, (1, KV_W), 1) < HEAD_DIM
        for ref, val in ((kpad_ref, kr), (vpad_ref, v)):
            h0l = jnp.where(left, val, 0.0)
            h1r = jnp.where(left, 0.0, val)
            ref[0] = h0l.astype(MXU_DTYPE)
            ref[1] = pltpu.roll(h0l, HEAD_DIM, 1).astype(MXU_DTYPE)
            ref[2] = pltpu.roll(h1r, HEAD_DIM, 1).astype(MXU_DTYPE)
            ref[3] = h1r.astype(MXU_DTYPE)

    row = lambda w: pl.BlockSpec((TT, w), lambda i: (i, 0))
    per_ex = pl.BlockSpec((1, 1, D_MODEL), lambda i: (i // tpe, 0, 0))
    tab = pl.BlockSpec((TT, KV_W), lambda i: (i % tpe, 0))
    pad = pl.BlockSpec((4, TT, KV_W), lambda i: (0, i, 0))
    sds = jax.ShapeDtypeStruct
    return pl.pallas_call(
        body, name="in_proj", grid=(R // TT,),
        in_specs=[row(D_MODEL), per_ex, per_ex, _full((1, D_MODEL)), _full(w_in.shape), _full((1, ATT_W)),
                  _full((1, KV_W)), tab, tab, _full((256, 256))],
        out_specs=(row(D_MODEL), row(ATT_W), pad, pad, row(ATT_W), row(KV_W), row(ATT_W), row(SHIFT_W), row(RWKV_W)),
        out_shape=(sds((R, D_MODEL), MXU_DTYPE), sds((R, ATT_W), MXU_DTYPE), sds((4, R, KV_W), MXU_DTYPE),
                   sds((4, R, KV_W), MXU_DTYPE), sds((R, ATT_W), F32), sds((R, KV_W), F32), sds((R, ATT_W), F32),
                   sds((R, SHIFT_W), F32), sds((R, RWKV_W), F32)),
        compiler_params=_cp(("arbitrary",)),
    )(x2, shift, scale, g_pre, w_in, qg, kg, cos, sin, bd)


def _softmax_parts(s):
    e = jnp.exp(s - jnp.max(s, axis=1, keepdims=True))
    return e, 1.0 / jnp.sum(e, axis=1, keepdims=True)


def _att_specs(T, TQ):
    nq = T // TQ
    qspec = pl.BlockSpec((TQ, KV_W), lambda b, p, i: (b * nq + i, p))
    side = lambda s: pl.BlockSpec((None, T, KV_W), lambda b, p, i: (2 * (p // 2) + s, b, 0))
    return nq, qspec, side


def _att_fwd_call(qr, kpad, vpad, B, T):
    TQ = min(ATT_TILE_FWD, T)
    nq, qspec, side = _att_specs(T, TQ)

    def body(q_ref, kl_ref, kr_ref, vl_ref, vr_ref, o_ref):
        q = q_ref[...]
        ea, inv_a = _softmax_parts(_dot_nt(q, kl_ref[...]))
        eb, inv_b = _softmax_parts(_dot_nt(q, kr_ref[...]))
        o_ref[...] = _dot(ea, vl_ref[...]) * inv_a + _dot(eb, vr_ref[...]) * inv_b

    return pl.pallas_call(
        body, name="att_fwd", grid=(B, 4, nq),
        in_specs=[qspec, side(0), side(1), side(0), side(1)], out_specs=qspec,
        out_shape=jax.ShapeDtypeStruct((B * T, ATT_W), F32),
        compiler_params=_cp(("arbitrary",) * 3),
    )(qr, kpad, kpad, vpad, vpad)


def _att_bwd_call(qr, kpad, vpad, d_o, B, T):
    TQ = min(ATT_TILE_BWD, T)
    nq, qspec, side = _att_specs(T, TQ)

    def body(q_ref, kl_ref, kr_ref, vl_ref, vr_ref, do_ref, dq_ref, dk_ref, dv_ref):
        i = pl.program_id(2)
        q, do = q_ref[...], do_ref[...]
        left = lax.broadcasted_iota(jnp.int32, (1, KV_W), 1) < HEAD_DIM
        dq = jnp.zeros((TQ, KV_W), F32)
        dk = jnp.zeros((T, KV_W), F32)
        dv = jnp.zeros((T, KV_W), F32)
        for k_ref, v_ref, mask in ((kl_ref, vl_ref, left), (kr_ref, vr_ref, jnp.logical_not(left))):
            kk, vv = k_ref[...], v_ref[...]
            e, inv = _softmax_parts(_dot_nt(q, kk))
            dp = _dot_nt(do, vv)
            ds = e * (dp - inv * jnp.sum(e * dp, axis=1, keepdims=True))
            dq = dq + _dot(ds, kk) * inv
            dk = dk + _dot_tn(ds, jnp.where(mask, q * inv, 0.0))
            dv = dv + _dot_tn(e, jnp.where(mask, do * inv, 0.0))
        dq_ref[...] = dq

        @pl.when(i == 0)
        def _():
            dk_ref[...] = dk
            dv_ref[...] = dv

        @pl.when(i > 0)
        def _():
            dk_ref[...] += dk
            dv_ref[...] += dv

    acc = pl.BlockSpec((None, T, KV_W), lambda b, p, i: (p, b, 0))
    sds = jax.ShapeDtypeStruct
    return pl.pallas_call(
        body, name="att_bwd", grid=(B, 4, nq),
        in_specs=[qspec, side(0), side(1), side(0), side(1), qspec], out_specs=(qspec, acc, acc),
        out_shape=(sds((B * T, ATT_W), F32), sds((4, B * T, KV_W), F32), sds((4, B * T, KV_W), F32)),
        compiler_params=_cp(("arbitrary",) * 3),
    )(qr, kpad, kpad, vpad, vpad, d_o)


def _shift_specs(R, T, TT, width):
    tpe = T // TT
    nb8 = R // 8
    cur = pl.BlockSpec((TT, width), lambda i: (i, 0))
    prev = pl.BlockSpec((8, width), lambda i: (jnp.maximum(i * (TT // 8) - 1, 0), 0))
    nxt = pl.BlockSpec((8, width), lambda i: (jnp.minimum((i + 1) * (TT // 8), nb8 - 1), 0))
    return tpe, cur, prev, nxt


def _neighbours(cur, prev8, next8, i, tpe, TT):
    rows = lax.broadcasted_iota(jnp.int32, (TT, 1), 0)
    first = jnp.where(i % tpe == 0, 0.0, 1.0)
    last = jnp.where(i % tpe == tpe - 1, 0.0, 1.0)
    before = jnp.where(rows == 0, prev8[7:8, :] * first, pltpu.roll(cur, 1, 0))
    after = jnp.where(rows == TT - 1, next8[0:1, :] * last, pltpu.roll(cur, TT - 1, 0))
    return before, after


def _shift_fwd_call(x, taps, T):
    R, width = x.shape
    TT = min(SHIFT_TILE, T)
    tpe, cur, prev, nxt = _shift_specs(R, T, TT, width)

    def body(x_ref, p_ref, n_ref, t_ref, o_ref, vh_ref):
        xc = x_ref[...]
        before, after = _neighbours(xc, p_ref[...], n_ref[...], pl.program_id(0), tpe, TT)
        out = t_ref[0:1, :] * before + t_ref[1:2, :] * xc + t_ref[2:3, :] * after
        o_ref[...] = out
        left = lax.broadcasted_iota(jnp.int32, (1, KV_W), 1) < HEAD_DIM
        for p in range(RWKV_W // KV_W):
            pair = out[:, 2 * RWKV_W + p * KV_W:2 * RWKV_W + (p + 1) * KV_W]
            vh_ref[:, 2 * p * KV_W:(2 * p + 1) * KV_W] = jnp.where(left, pair, 0.0)
            vh_ref[:, (2 * p + 1) * KV_W:(2 * p + 2) * KV_W] = jnp.where(left, pltpu.roll(pair, HEAD_DIM, 1), 0.0)

    return pl.pallas_call(
        body, name="shift_fwd", grid=(R // TT,), in_specs=[cur, prev, nxt, _full(taps.shape)],
        out_specs=(cur, pl.BlockSpec((TT, 2 * RWKV_W), lambda i: (i, 0))),
        out_shape=(jax.ShapeDtypeStruct((R, width), F32), jax.ShapeDtypeStruct((R, 2 * RWKV_W), F32)),
        compiler_params=_cp(("arbitrary",)),
    )(x, x, x, taps)


def _shift_bwd_call(x, d, taps, T):
    R, width = x.shape
    TT = min(SHIFT_TILE, T)
    tpe, cur, prev, nxt = _shift_specs(R, T, TT, width)

    def body(x_ref, xp_ref, xn_ref, d_ref, dp_ref, dn_ref, t_ref, dx_ref, dt_ref):
        i = pl.program_id(0)
        xc, dc = x_ref[...], d_ref[...]
        d_before, d_after = _neighbours(dc, dp_ref[...], dn_ref[...], i, tpe, TT)
        dx_ref[...] = t_ref[2:3, :] * d_before + t_ref[1:2, :] * dc + t_ref[0:1, :] * d_after
        x_before, x_after = _neighbours(xc, xp_ref[...], xn_ref[...], i, tpe, TT)
        @pl.when(i == 0)
        def _():
            dt_ref[...] = jnp.zeros_like(dt_ref)

        for j, xs in enumerate((x_before, xc, x_after)):
            dt_ref[j:j + 1, :] += jnp.sum(dc * xs, axis=0, keepdims=True)

    return pl.pallas_call(
        body, name="shift_bwd", grid=(R // TT,),
        in_specs=[cur, prev, nxt, cur, prev, nxt, _full(taps.shape)], out_specs=(cur, _full((8, width))),
        out_shape=(jax.ShapeDtypeStruct((R, width), F32), jax.ShapeDtypeStruct((8, width), F32)),
        compiler_params=_cp(("arbitrary",)),
    )(x, x, x, d, d, d, taps)


def _lora_in(wa):
    lane = lax.broadcasted_iota(jnp.int32, (1, LORA_W), 1)
    return jnp.where(lane < LORA_W // 2, jnp.tanh(wa), wa)


def _rwkv_prep_call(shifted, wup, aup, w0, a0, k_k, k_a, bd, T):
    R = shifted.shape[0]
    TT = min(SHIFT_TILE, T)

    def body(k_ref, wa_ref, wup_ref, aup_ref, w0_ref, a0_ref, kk_ref, ka_ref, bd_ref, w_o, kt_o, akk_o, kk_o):
        twa = _lora_in(wa_ref[...])
        pre = [_dot(twa, m_ref[z]) for m_ref in (wup_ref, aup_ref) for z in range(2)]
        outs = _rwkv_pw(k_ref[...], pre[0], pre[1], pre[2], pre[3], w0_ref[...], a0_ref[...], kk_ref[...],
                        ka_ref[...], bd_ref[...], False)
        w_o[0], w_o[1], kt_o[0], kt_o[1], akk_o[0], akk_o[1] = outs[:6]
        kk_o[...] = outs[6]

    col = lambda c, w: pl.BlockSpec((TT, w), lambda i: (i, c))
    two = pl.BlockSpec((2, TT, RWKV_W), lambda i: (0, i, 0))
    sds = jax.ShapeDtypeStruct
    return pl.pallas_call(
        body, name="rwkv_prep", grid=(R // TT,),
        in_specs=[col(1, RWKV_W), col(3 * RWKV_W // LORA_W, LORA_W), _full(wup.shape), _full(aup.shape),
                  _full((2, RWKV_W)), _full((2, RWKV_W)), _full((1, RWKV_W)), _full((1, RWKV_W)), _full((256, 256))],
        out_specs=(two, two, two, col(0, RWKV_W)),
        out_shape=(sds((2, R, RWKV_W), F32),) * 3 + (sds((R, RWKV_W), F32),),
        compiler_params=_cp(("arbitrary",)),
    )(shifted, shifted, wup, aup, w0, a0, k_k, k_a, bd)


def _rwkv_prep_bwd_call(shifted, cts, wup, aup, w0, a0, k_k, k_a, bd, T):
    R = shifted.shape[0]
    TT = min(SHIFT_TILE, T)

    def body(k_ref, wa_ref, dw0, dkt0, dakk0, dkk0, dr0, dv0, dw1, dkt1, dakk1, dkk1, dr1, dv1, dr2_ref, dv2_ref, dkts_ref,
             wup_ref, aup_ref, w0_ref, a0_ref, kk_ref, ka_ref, bd_ref,
             dsh_ref, gwup_ref, gaup_ref, gw0_ref, ga0_ref, gkk_ref, gka_ref):
        dw_ref, dkt_ref, dakk_ref, dkk_ref, dr_ref, dv_ref = ((dw0, dw1), (dkt0, dkt1), (dakk0, dakk1), (dkk0, dkk1),
                                                              (dr0, dr1), (dv0, dv1))
        i = pl.program_id(0)
        wa = wa_ref[...]
        twa = _lora_in(wa)
        pre = [_dot(twa, m_ref[z]) for m_ref in (wup_ref, aup_ref) for z in range(2)]
        fn = functools.partial(_rwkv_pw, bd=bd_ref[...], diff=True)
        _, vjp = jax.vjp(fn, k_ref[...], pre[0], pre[1], pre[2], pre[3], w0_ref[...], a0_ref[...], kk_ref[...],
                         ka_ref[...])
        dkts = dkts_ref[...]
        dk, dpw0, dpw1, dpa0, dpa1, gw0, ga0, gkk, gka = vjp(
            (dw_ref[0][...], dw_ref[1][...], dkt_ref[0][...] + dkts, dkt_ref[1][...] + dkts, dakk_ref[0][...],
             dakk_ref[1][...], dkk_ref[0][...] + dkk_ref[1][...]))
        dtwa = (_dot_nt(dpw0, wup_ref[0]) + _dot_nt(dpw1, wup_ref[1]) + _dot_nt(dpa0, aup_ref[0])
                + _dot_nt(dpa1, aup_ref[1]))
        lane = lax.broadcasted_iota(jnp.int32, (1, LORA_W), 1)
        dsh_ref[:, 0:RWKV_W] = dr_ref[0][...] + dr_ref[1][...] + dr2_ref[...]
        dsh_ref[:, RWKV_W:2 * RWKV_W] = dk
        dsh_ref[:, 2 * RWKV_W:3 * RWKV_W] = dv_ref[0][...] + dv_ref[1][...] + dv2_ref[...]
        dsh_ref[:, 3 * RWKV_W:] = jnp.where(lane < LORA_W // 2, dtwa * (1.0 - twa * twa), dtwa)
        acc = ((gwup_ref.at[0], _dot_tn(twa, dpw0)), (gwup_ref.at[1], _dot_tn(twa, dpw1)),
               (gaup_ref.at[0], _dot_tn(twa, dpa0)), (gaup_ref.at[1], _dot_tn(twa, dpa1)),
               (gw0_ref, gw0), (ga0_ref, ga0), (gkk_ref, gkk), (gka_ref, gka))

        @pl.when(i == 0)
        def _():
            for ref, val in acc:
                ref[...] = val

        @pl.when(i > 0)
        def _():
            for ref, val in acc:
                ref[...] += val

    col = lambda c, w: pl.BlockSpec((TT, w), lambda i: (i, c))
    one = col(0, RWKV_W)
    sds = jax.ShapeDtypeStruct
    return pl.pallas_call(
        body, name="rwkv_prep_bwd", grid=(R // TT,),
        in_specs=[col(1, RWKV_W), col(3 * RWKV_W // LORA_W, LORA_W)] + [one] * 15 + [
                  _full(wup.shape), _full(aup.shape), _full((2, RWKV_W)), _full((2, RWKV_W)), _full((1, RWKV_W)),
                  _full((1, RWKV_W)), _full((256, 256))],
        out_specs=(pl.BlockSpec((TT, SHIFT_W), lambda i: (i, 0)), _full(wup.shape), _full(aup.shape),
                   _full((2, RWKV_W)), _full((2, RWKV_W)), _full((1, RWKV_W)), _full((1, RWKV_W))),
        out_shape=(sds((R, SHIFT_W), F32), sds(wup.shape, F32), sds(aup.shape, F32), sds((2, RWKV_W), F32),
                   sds((2, RWKV_W), F32), sds((1, RWKV_W), F32), sds((1, RWKV_W), F32)),
        compiler_params=_cp(("arbitrary",)),
    )(shifted, shifted, *cts, wup, aup, w0, a0, k_k, k_a, bd)


def _col_lhs(row, eye_b):
    return eye_b * row.astype(MXU_DTYPE)


def _colsum(x):
    return jnp.sum(x, axis=0, keepdims=True)


def _stacked_segsum(tiles, bd):
    res = _seg_dot(jnp.concatenate(tiles, axis=0), bd)
    return [res[j * HEAD_DIM:(j + 1) * HEAD_DIM] for j in range(len(tiles))]


def _scan_specs(B, T, C, nC):
    def blk(z, col, rev):
        idx = (lambda g: (z, 0, nC - 1 - g, col)) if rev else (lambda g: (z, 0, g, col))
        return pl.BlockSpec((None, B, C, RWKV_W), idx)

    def blk3(col, rev):
        idx = (lambda g: (0, nC - 1 - g, col)) if rev else (lambda g: (0, g, col))
        return pl.BlockSpec((B, C, RWKV_W), idx)

    return blk, blk3


def _scan_fwd_call(w, kt, akk, kk, shifted, eye_b, eye_f, bd, B, T):
    C = min(SCAN_CHUNK, T)
    nC = T // C
    blk, blk3 = _scan_specs(B, T, C, nC)

    def body(w0, kt0, akk0, kk0, v0, r0, w1, kt1, akk1, kk1, v1, r1, eb_ref, ef_ref, bd_ref, y0, y1, st, S):
        @pl.when(pl.program_id(0) == 0)
        def _():
            S[...] = jnp.zeros_like(S)

        st[0] = S[...].astype(MXU_DTYPE)
        dirs = ((w0, kt0, akk0, kk0, v0, r0, y0), (w1, kt1, akk1, kk1, v1, r1, y1))

        def step(s, carry):
            for z in range(2):
                row = s if z == 0 else C - 1 - s
                prev = jnp.maximum(s - 1, 0) if z == 0 else jnp.minimum(C - s, C - 1)
                wr, ktr, akkr, kkr, vr, rr, yr = dirs[z]
                tiles = []
                for b in range(B):
                    Sb = st[s, z * B + b]
                    tiles += [Sb * kkr[b, pl.ds(row, 1), :].astype(MXU_DTYPE),
                              _col_lhs(vr[b, pl.ds(row, 1), :], eb_ref[...]),
                              Sb * rr[b, pl.ds(prev, 1), :].astype(MXU_DTYPE)]
                res = _stacked_segsum(tiles, bd_ref[...])
                for b in range(B):
                    c = z * B + b
                    sab, vb, yb = res[3 * b:3 * b + 3]
                    ld = lambda ref: ref[b, pl.ds(row, 1), :]
                    Sn = S[c] * ld(wr) - sab * ld(akkr) + vb * ld(ktr)
                    S[c] = Sn
                    st[s + 1, c] = Sn.astype(MXU_DTYPE)
                    yr[b, pl.ds(prev, 1), :] = _colsum(ef_ref[...] * yb)
            return carry

        lax.fori_loop(0, C, step, 0, unroll=SCAN_UNROLL)
        for z in range(2):
            last = C - 1 if z == 0 else 0
            rr, yr = dirs[z][5], dirs[z][6]
            res = _stacked_segsum([st[C, z * B + b] * rr[b, last:last + 1, :].astype(MXU_DTYPE) for b in range(B)],
                                  bd_ref[...])
            for b in range(B):
                yr[b, last:last + 1, :] = _colsum(ef_ref[...] * res[b])

    ins, specs = [], []
    for z, rev in ((0, False), (1, True)):
        ins += [w, kt, akk, kk, shifted, shifted]
        specs += [blk(z, 0, rev), blk(z, 0, rev), blk(z, 0, rev), blk3(0, rev), blk3(2, rev), blk3(0, rev)]
    sds = jax.ShapeDtypeStruct
    return pl.pallas_call(
        body, name="scan_fwd", grid=(nC,),
        in_specs=specs + [_full((HEAD_DIM, RWKV_W)), _full((HEAD_DIM, RWKV_W)), _full((256, 256))],
        out_specs=(blk3(0, False), blk3(0, True),
                   pl.BlockSpec((None, C + 1, 2 * B, HEAD_DIM, RWKV_W), lambda g: (g, 0, 0, 0, 0))),
        out_shape=(sds((B, T, RWKV_W), F32), sds((B, T, RWKV_W), F32),
                   sds((nC, C + 1, 2 * B, HEAD_DIM, RWKV_W), MXU_DTYPE)),
        scratch_shapes=[pltpu.VMEM((2 * B, HEAD_DIM, RWKV_W), F32)],
        compiler_params=_cp(("arbitrary",)),
    )(*ins, eye_b, eye_f, bd)


def _scan_bwd_call(w, kt, akk, kk, shifted, v_heads, dys, st, eye_b, eye_f, bd, B, T):
    C = min(SCAN_CHUNK, T)
    nC = T // C
    blk, blk3 = _scan_specs(B, T, C, nC)
    nin = 7

    def body(*refs):
        d0, d1 = refs[:nin], refs[nin:2 * nin]
        st_ref, eb_ref, ef_ref, sel_ref, hm_ref, bd_ref = refs[2 * nin:2 * nin + 6]
        o0, o1 = refs[2 * nin + 6:2 * nin + 12], refs[2 * nin + 12:2 * nin + 18]
        COL, DYC, G = refs[2 * nin + 18:]

        @pl.when(pl.program_id(0) == 0)
        def _():
            G[...] = jnp.zeros_like(G)

        dirs = (d0 + (o0,), d1 + (o1,))

        def column_operands(s, z):
            row = s if z == 0 else C - 1 - s
            _, _, _, kkr, _, _, dyr, _ = dirs[z]
            tiles = []
            for b in range(B):
                tiles += [st_ref[s, z * B + b] * kkr[b, pl.ds(row, 1), :].astype(MXU_DTYPE),
                          _col_lhs(dyr[b, pl.ds(row, 1), :], eb_ref[...])]
            return tiles

        def keep_columns(res, z):
            for b in range(B):
                for k in range(2):
                    COL[k, z * B + b] = res[2 * b + k].astype(MXU_DTYPE)
                DYC[z * B + b] = res[2 * b + 1]

        for z in range(2):
            keep_columns(_stacked_segsum(column_operands(C - 1, z), bd_ref[...]), z)

        def bwd(it, carry):
            s = C - 1 - it
            for z in range(2):
                row = s if z == 0 else C - 1 - s
                wr, ktr, akkr, kkr, vr, rr, dyr, (dw_o, dkt_o, dakk_o, dkk_o, dr_o, dv_o) = dirs[z]
                tiles, Gcs = [], []
                for b in range(B):
                    c = z * B + b
                    Gc = G[c] + DYC[c] * rr[b, pl.ds(row, 1), :]
                    Gb = Gc.astype(MXU_DTYPE)
                    Gcs.append((Gc, Gb))
                    tiles += [Gb * akkr[b, pl.ds(row, 1), :].astype(MXU_DTYPE),
                              Gb * ktr[b, pl.ds(row, 1), :].astype(MXU_DTYPE)]
                res = _stacked_segsum(tiles + column_operands(jnp.maximum(s - 1, 0), z), bd_ref[...])
                for b in range(B):
                    c = z * B + b
                    Gc, Gb = Gcs[b]
                    gab, dvb = res[2 * b], res[2 * b + 1]
                    ld = lambda ref: ref[b, pl.ds(row, 1), :]
                    G[c] = Gc * ld(wr) - gab * ld(kkr)
                    Sb = st_ref[s, c]
                    prods = jnp.concatenate([Gb, st_ref[s + 1, c] * COL[1, c], Gb * Sb, Gb * COL[0, c],
                                             gab.astype(MXU_DTYPE) * Sb], axis=0)
                    v_rows = jnp.concatenate([vr[b, pl.ds(row, 1)][0], jnp.zeros((8, 3 * HEAD_DIM), F32)], axis=1)
                    lhs = jnp.concatenate([sel_ref[...], v_rows], axis=0).astype(MXU_DTYPE)
                    sums = jnp.dot(lhs, prods, preferred_element_type=F32)
                    for k, (ref, sign) in enumerate(((dr_o, 1.0), (dw_o, 1.0), (dakk_o, -1.0), (dkk_o, -1.0))):
                        ref[b, pl.ds(row, 1), :] = sign * sums[k:k + 1, :]
                    dkt_o[b, pl.ds(row, 1), :] = _colsum(sums[8:16] * hm_ref[...])
                    dv_o[b, pl.ds(row, 1), :] = _colsum(ef_ref[...] * dvb)
                keep_columns(res[2 * B:], z)
            return carry

        lax.fori_loop(0, C, bwd, 0, unroll=SCAN_UNROLL)

    ins, specs = [], []
    for z, rev in ((0, True), (1, False)):
        heads = pl.BlockSpec((B, C) + v_heads.shape[2:], (lambda g: (0, nC - 1 - g, 0, 0)) if rev else (lambda g: (0, g, 0, 0)))
        ins += [w, kt, akk, kk, v_heads, shifted, dys]
        specs += [blk(z, 0, rev), blk(z, 0, rev), blk(z, 0, rev), blk3(0, rev), heads, blk3(0, rev), blk3(0, rev)]
    sel = (jnp.arange(8)[:, None] + 1 == (jnp.arange(5 * HEAD_DIM) // HEAD_DIM)[None, :]).astype(F32)
    head_rows = (jnp.arange(RWKV_W // HEAD_DIM)[:, None] == (jnp.arange(RWKV_W) // HEAD_DIM)[None, :]).astype(F32)
    ins += [st, eye_b, eye_f, sel, head_rows, bd]
    specs += [pl.BlockSpec((None, C + 1, 2 * B, HEAD_DIM, RWKV_W), lambda g: (nC - 1 - g, 0, 0, 0, 0)),
              _full((HEAD_DIM, RWKV_W)), _full((HEAD_DIM, RWKV_W)), _full(sel.shape), _full(head_rows.shape),
              _full((256, 256))]
    sds = jax.ShapeDtypeStruct
    out_specs = tuple(blk3(0, True) for _ in range(6)) + tuple(blk3(0, False) for _ in range(6))
    res = pl.pallas_call(
        body, name="scan_bwd", grid=(nC,), in_specs=specs, out_specs=out_specs,
        out_shape=tuple(sds((B, T, RWKV_W), F32) for _ in range(12)),
        scratch_shapes=[pltpu.VMEM((2, 2 * B, HEAD_DIM, RWKV_W), MXU_DTYPE), pltpu.VMEM((2 * B, HEAD_DIM, RWKV_W), F32),
                        pltpu.VMEM((2 * B, HEAD_DIM, RWKV_W), F32)],
        compiler_params=_cp(("arbitrary",)),
    )(*ins)
    return list(res)


def _out_head_call(x2, tgt2, gate, y_att, g_att, y0, y1, shifted, kt, g_rw, w_out, g_post, gn_w, gn_b, r_k, bd, T):
    R = x2.shape[0]
    TT = min(ROW_TILE, T)
    tpe = T // TT

    def body(x_ref, t_ref, gate_ref, ya_ref, ga_ref, y0_ref, y1_ref, r_ref, v_ref, kt_ref, grw_ref, w_ref, gp_ref,
             gnw_ref, gnb_ref, rk_ref, bd_ref,
             loss_o, dy_o, dya_o, dga_o, dys_o, dr_o, dv_o, dkts_o, dgrw_o, dgate_o, gw_o, ggp_o, ggnw_o, ggnb_o, grk_o):
        i = pl.program_id(0)
        bd = bd_ref[...]
        mix = functools.partial(_mix_fn, bd=bd, diff=True)
        (ma, mr), mix_vjp = jax.vjp(mix, ya_ref[...], ga_ref[...], y0_ref[...] + y1_ref[...], r_ref[...], v_ref[...],
                                    kt_ref[0] + kt_ref[1], grw_ref[...], gnw_ref[...], gnb_ref[...], rk_ref[...])
        out = _dot(ma, w_ref[0:ATT_W, :]) + _dot(mr, w_ref[ATT_W:, :])
        loss, loss_vjp = jax.vjp(_loss_fn, out, x_ref[...], t_ref[...], gate_ref[0], gp_ref[...])
        d_out, dy, _, dgate, dgp = loss_vjp(jnp.ones((1, 1), F32))
        dy_o[...] = dy
        dma = _dot_nt(d_out, w_ref[0:ATT_W, :])
        dmr = _dot_nt(d_out, w_ref[ATT_W:, :])
        dya_o[...], dga_o[...], dys_o[...], dr_o[...], dv_o[...], dkts_o[...], dgrw_o[...], dgnw, dgnb, drk = \
            mix_vjp((dma, dmr))
        gw = jnp.concatenate([_dot_tn(ma, d_out), _dot_tn(mr, d_out)], axis=0)
        acc = ((loss_o, jnp.broadcast_to(loss, (8, 128))), (gw_o, gw), (ggp_o, dgp), (ggnw_o, dgnw), (ggnb_o, dgnb),
               (grk_o, drk))

        @pl.when(i == 0)
        def _():
            for ref, val in acc:
                ref[...] = val

        @pl.when(i > 0)
        def _():
            for ref, val in acc:
                ref[...] += val

        @pl.when(i % tpe == 0)
        def _():
            dgate_o[0] = dgate

        @pl.when(i % tpe > 0)
        def _():
            dgate_o[0] += dgate

    row = lambda w, c=0: pl.BlockSpec((TT, w), lambda i: (i, c))
    two = pl.BlockSpec((2, TT, RWKV_W), lambda i: (0, i, 0))
    per_ex = pl.BlockSpec((1, 1, D_MODEL), lambda i: (i // tpe, 0, 0))
    sds = jax.ShapeDtypeStruct
    r512 = sds((R, RWKV_W), F32)
    return pl.pallas_call(
        body, name="out_head", grid=(R // TT,),
        in_specs=[row(D_MODEL), row(D_MODEL), per_ex, row(ATT_W), row(ATT_W), row(RWKV_W), row(RWKV_W), row(RWKV_W, 0),
                  row(RWKV_W, 2), two,
                  row(RWKV_W), _full(w_out.shape), _full((1, D_MODEL)), _full((1, RWKV_W)), _full((1, RWKV_W)),
                  _full((1, RWKV_W)), _full((256, 256))],
        out_specs=(_full((8, 128)), row(D_MODEL), row(ATT_W), row(ATT_W), row(RWKV_W), row(RWKV_W), row(RWKV_W),
                   row(RWKV_W), row(RWKV_W), per_ex, _full((D_MODEL, D_MODEL)), _full((1, D_MODEL)), _full((1, RWKV_W)),
                   _full((1, RWKV_W)), _full((1, RWKV_W))),
        out_shape=(sds((8, 128), F32), sds((R, D_MODEL), F32), r512, r512, r512, r512, r512, r512, r512,
                   sds((R // T, 1, D_MODEL), F32), sds((D_MODEL, D_MODEL), F32), sds((1, D_MODEL), F32),
                   sds((1, RWKV_W), F32), sds((1, RWKV_W), F32), sds((1, RWKV_W), F32)),
        compiler_params=_cp(("arbitrary",)),
    )(x2, tgt2, gate, y_att, g_att, y0, y1, shifted, shifted, kt, g_rw, w_out, g_post, gn_w, gn_b, r_k, bd)


def _in_proj_bwd_call(x2, dy, shift, scale, g_pre, w_in, qg, kg, cos, sin, bd, q_raw, k_raw, dqr, dkp, dvp,
                      d_gatt, d_rin, d_grw, T):
    R = x2.shape[0]
    TT = min(SHIFT_TILE, T)
    tpe = T // TT

    def body(x_ref, dy_ref, sh_ref, sc_ref, gp_ref, w_ref, qg_ref, kg_ref, cos_ref, sin_ref, bd_ref, q_ref, k_ref,
             dqr_ref, dkp_ref, dvp_ref, dga_ref, drin_ref, dgrw_ref,
             dx_o, dproj_o, dsh_o, dsc_o, ggp_o, gqg_o, gkg_o):
        i = pl.program_id(0)
        cos, sin, bd = cos_ref[...], sin_ref[...], bd_ref[...]
        left = lax.broadcasted_iota(jnp.int32, (1, KV_W), 1) < HEAD_DIM

        def kv_grad(ref):
            a = ref[0] + ref[1]
            b = ref[2] + ref[3]
            return jnp.where(left, a + pltpu.roll(a, HEAD_DIM, 1), b + pltpu.roll(b, HEAD_DIM, 1))

        qfn = functools.partial(_qk_fn, cos=jnp.tile(cos, (1, 4)), sin=jnp.tile(sin, (1, 4)), bd=bd, scale=ATT_SCALE,
                                diff=True)
        _, q_vjp = jax.vjp(qfn, q_ref[...], qg_ref[...])
        dq, gqg = q_vjp(dqr_ref[...])
        kfn = functools.partial(_qk_fn, cos=cos, sin=sin, bd=bd, scale=1.0, diff=True)
        _, k_vjp = jax.vjp(kfn, k_ref[...], kg_ref[...])
        dk, gkg = k_vjp(kv_grad(dkp_ref))
        pieces = ((C_Q, C_K, dq), (C_K, C_V, dk), (C_V, C_GA, kv_grad(dvp_ref)), (C_GA, C_RIN, dga_ref[...]),
                  (C_RIN, C_GRW, drin_ref[...]), (C_GRW, C_END, dgrw_ref[...]))
        dh = jnp.zeros((TT, D_MODEL), F32)
        for c0, c1, val in pieces:
            vb = val.astype(MXU_DTYPE)
            dproj_o[:, c0:c1] = vb
            dh = dh + _dot(vb, w_ref[c0:c1, :])
        _, pre_vjp = jax.vjp(_pre_fn, x_ref[...], sh_ref[0], sc_ref[0], gp_ref[...])
        dx, dsh, dsc, ggp = pre_vjp(dh)
        dx_o[...] = dx + dy_ref[...]
        acc = ((ggp_o, ggp), (gqg_o, gqg), (gkg_o, gkg))

        @pl.when(i == 0)
        def _():
            for ref, val in acc:
                ref[...] = val

        @pl.when(i > 0)
        def _():
            for ref, val in acc:
                ref[...] += val

        @pl.when(i % tpe == 0)
        def _():
            dsh_o[0] = dsh
            dsc_o[0] = dsc

        @pl.when(i % tpe > 0)
        def _():
            dsh_o[0] += dsh
            dsc_o[0] += dsc

    row = lambda w: pl.BlockSpec((TT, w), lambda i: (i, 0))
    per_ex = pl.BlockSpec((1, 1, D_MODEL), lambda i: (i // tpe, 0, 0))
    tab = pl.BlockSpec((TT, KV_W), lambda i: (i % tpe, 0))
    pad = pl.BlockSpec((4, TT, KV_W), lambda i: (0, i, 0))
    sds = jax.ShapeDtypeStruct
    nb = R // T
    return pl.pallas_call(
        body, name="in_proj_bwd", grid=(R // TT,),
        in_specs=[row(D_MODEL), row(D_MODEL), per_ex, per_ex, _full((1, D_MODEL)), _full(w_in.shape), _full((1, ATT_W)),
                  _full((1, KV_W)), tab, tab, _full((256, 256)), row(ATT_W), row(KV_W), row(ATT_W), pad, pad,
                  row(ATT_W), row(SHIFT_W), row(RWKV_W)],
        out_specs=(row(D_MODEL), row(C_END), per_ex, per_ex, _full((1, D_MODEL)), _full((1, ATT_W)), _full((1, KV_W))),
        out_shape=(sds((R, D_MODEL), F32), sds((R, C_END), MXU_DTYPE), sds((nb, 1, D_MODEL), F32),
                   sds((nb, 1, D_MODEL), F32), sds((1, D_MODEL), F32), sds((1, ATT_W), F32), sds((1, KV_W), F32)),
        compiler_params=_cp(("arbitrary",)),
    )(x2, dy, shift, scale, g_pre, w_in, qg, kg, cos, sin, bd, q_raw, k_raw, dqr, dkp, dvp, d_gatt, d_rin, d_grw)


def _w_in_grad_call(hb, dproj):
    R = hb.shape[0]
    TT = min(W_GRAD_ROWS, R)
    CB = 1152
    last = R // TT - 1

    def body(h_ref, d_ref, o_ref, acc):
        g = _dot_tn(h_ref[...], d_ref[...])

        @pl.when(pl.program_id(1) == 0)
        def _():
            acc[...] = g

        @pl.when(pl.program_id(1) > 0)
        def _():
            acc[...] += g

        @pl.when(pl.program_id(1) == last)
        def _():
            o_ref[...] = acc[...].astype(o_ref.dtype)

    return pl.pallas_call(
        body, name="w_in_grad", grid=(C_END // CB, R // TT),
        in_specs=[pl.BlockSpec((TT, D_MODEL), lambda j, i: (i, 0)), pl.BlockSpec((TT, CB), lambda j, i: (i, j))],
        out_specs=pl.BlockSpec((D_MODEL, CB), lambda j, i: (0, j)),
        out_shape=jax.ShapeDtypeStruct((D_MODEL, C_END), MXU_DTYPE),
        scratch_shapes=[pltpu.VMEM((D_MODEL, CB), F32)], compiler_params=_cp(("arbitrary", "arbitrary")),
    )(hb, dproj)


def _adam_refs(p_ref, w_ref, m_ref, v_ref, g_o, d_o, m_o, v_o):
    g = p_ref[0].astype(F32)
    for j in range(1, p_ref.shape[0]):
        g = g + p_ref[j].astype(F32)
    m2 = ADAM_B1 * m_ref[...] + (1.0 - ADAM_B1) * g
    v2 = ADAM_B2 * v_ref[...] + (1.0 - ADAM_B2) * jnp.square(g)
    m_hat = m2 / (1.0 - ADAM_B1 ** ADAM_STEP)
    v_hat = v2 / (1.0 - ADAM_B2 ** ADAM_STEP)
    g_o[...] = g
    d_o[...] = -ADAM_LR * (m_hat / (jnp.sqrt(v_hat) + ADAM_EPS) + ADAM_WD * w_ref[...])
    m_o[...] = m2
    v_o[...] = v2


def _adam_small_call(items, name):
    n = len(items)

    def body(*refs):
        for k in range(n):
            _adam_refs(*refs[4 * k:4 * k + 4], *refs[4 * n + 4 * k:4 * n + 4 * k + 4])

    out_shape = tuple(jax.ShapeDtypeStruct(w.shape, F32) for _, w, _, _ in items for _ in range(4))
    out = pl.pallas_call(body, name=name, out_shape=out_shape)(*[a for item in items for a in item])
    return [out[4 * k:4 * k + 4] for k in range(n)]


def _adam_call(parts, w, m, v, name, row_tile=None):
    P, M, N = parts.shape
    TM = M if row_tile is None else row_tile

    def body(*refs):
        _adam_refs(*refs)

    blk = pl.BlockSpec((TM, N), lambda i: (i, 0))
    return pl.pallas_call(
        body, name=name, grid=(M // TM,),
        in_specs=[pl.BlockSpec((P, TM, N), lambda i: (0, i, 0)), blk, blk, blk], out_specs=(blk,) * 4,
        out_shape=(jax.ShapeDtypeStruct((M, N), F32),) * 4, compiler_params=_cp(("arbitrary",)),
    )(parts, w, m, v)


_SMALL_ROWS = 136


def _pack_small(taps, w_up, w0, a_up, a0):
    flat = jnp.concatenate([taps.reshape(-1), w_up.reshape(-1), w0.reshape(-1), a_up.reshape(-1), a0.reshape(-1)])
    return jnp.pad(flat, (0, _SMALL_ROWS * 128 - flat.shape[0])).reshape(_SMALL_ROWS, 128)


def _unpack_small(packed):
    n = packed.shape[0]
    flat = packed.reshape(n, -1)
    out, o = [], 0
    for shape in ((3, 208), (2, 64, 64), (2, 64), (2, 64, 64), (2, 64)):
        size = 1
        for s in shape:
            size *= s
        out.append(flat[:, o:o + size].reshape((n,) + shape))
        o += size
    return out


def _cols_to_full(blocks):
    nd = blocks.ndim
    moved = jnp.moveaxis(blocks, 0, nd - 2)
    return moved.reshape(moved.shape[:-2] + (moved.shape[-2] * moved.shape[-1],))


def _full_to_cols(full):
    k = full.shape[-1] // NDEV
    return jnp.moveaxis(full.reshape(full.shape[:-1] + (NDEV, k)), -2, 0)


_REP_SIZES = (("g_pre", 1024), ("q_norm_g", 64), ("k_norm_g", 64), ("k_k", 512), ("k_a", 512), ("r_k", 512),
              ("gn_w", 512), ("gn_b", 512), ("g_post", 1024))
_REP_ROWS = 40


def kernel(x, c, w_ada, b_ada, g_pre, w_in, q_norm_g, k_norm_g, shift_taps, w_up, w0, a_up, a0, k_k, k_a, r_k, gn_w, gn_b, w_out, g_post, loss_target, m_w_ada, m_b_ada, m_g_pre, m_w_in, m_q_norm_g, m_k_norm_g, m_shift_taps, m_w_up, m_w0, m_a_up, m_a0, m_k_k, m_k_a, m_r_k, m_gn_w, m_gn_b, m_w_out, m_g_post, v_w_ada, v_b_ada, v_g_pre, v_w_in, v_q_norm_g, v_k_norm_g, v_shift_taps, v_w_up, v_w0, v_a_up, v_a0, v_k_k, v_k_a, v_r_k, v_gn_w, v_gn_b, v_w_out, v_g_post):
    B, T, _ = x.shape
    R = B * T
    me = 4 * lax.axis_index("x") + 2 * lax.axis_index("y") + lax.axis_index("c")
    x2 = x.reshape(R, D_MODEL)
    tgt2 = loss_target.reshape(R, D_MODEL)

    seg = jnp.arange(256) // HEAD_DIM
    bd = (seg[:, None] == seg[None, :]).astype(MXU_DTYPE)
    eye = (jnp.arange(HEAD_DIM)[:, None] == (jnp.arange(RWKV_W) % HEAD_DIM)[None, :])
    eye_b, eye_f = eye.astype(MXU_DTYPE), eye.astype(F32)
    cos, sin = _rope_tables(T)

    c_g, w_in_g, w_out_g, small_g = _exchange(
        [c, w_in[0].T.astype(MXU_DTYPE), w_out[0].astype(MXU_DTYPE),
         _pack_small(shift_taps[0], w_up[0], w0[0], a_up[0], a0[0])], ["all"] * 4, "gather_params")
    c_all = c_g.reshape(NDEV * B, D_MODEL)
    w_in_f = w_in_g.reshape(C_END, D_MODEL)
    w_out_f = w_out_g.reshape(D_MODEL, D_MODEL)
    taps_b, w_up_b, w0_b, a_up_b, a0_b = _unpack_small(small_g)
    taps_f = jnp.pad(_cols_to_full(taps_b), ((0, 5), (0, 0)))
    w_up_f, a_up_f = _cols_to_full(w_up_b), _cols_to_full(a_up_b)
    w0_f, a0_f = _cols_to_full(w0_b), _cols_to_full(a0_b)
    wup_pad = jnp.pad(w_up_f, ((0, 0), (0, 64), (0, 0))).astype(MXU_DTYPE)
    aup_pad = jnp.pad(a_up_f, ((0, 0), (64, 0), (0, 0))).astype(MXU_DTYPE)

    ncol = w_ada.shape[2]
    b_cols = lax.dynamic_slice(b_ada, (0, me * ncol), (1, ncol))
    mod_cols = _mod_call(c_all, w_ada[0].astype(MXU_DTYPE), b_cols)
    (mod_g,) = _exchange([mod_cols], ["all"], "gather_mod")
    mod = lax.dynamic_slice(_cols_to_full(mod_g), (me * B, 0), (B, 3 * D_MODEL))
    shift, scale, gate = [mod[:, j * D_MODEL:(j + 1) * D_MODEL].reshape(B, 1, D_MODEL) for j in range(3)]

    qg = jnp.tile(q_norm_g, (1, ATT_W // HEAD_DIM))
    kg = jnp.tile(k_norm_g, (1, KV_W // HEAD_DIM))
    rk_row = r_k.reshape(1, RWKV_W)

    hb, qr, kpad, vpad, q_raw, k_raw, g_att, rin, g_rw = _in_proj_call(
        x2, shift, scale, g_pre, w_in_f, qg, kg, cos, sin, bd, T)
    y_att = _att_fwd_call(qr, kpad, vpad, B, T)
    shifted, v_rows = _shift_fwd_call(rin, taps_f, T)
    w_s, kt_s, akk_s, kk_s = _rwkv_prep_call(shifted, wup_pad, aup_pad, w0_f, a0_f, k_k, k_a, bd, T)
    sh3 = shifted.reshape(B, T, SHIFT_W)
    r4 = lambda a: a.reshape(2, B, T, RWKV_W)
    y0, y1, st = _scan_fwd_call(r4(w_s), r4(kt_s), r4(akk_s), kk_s.reshape(B, T, RWKV_W), sh3, eye_b, eye_f, bd, B, T)

    (loss_blk, dy, d_yatt, d_gatt, d_ys, d_r2, d_v2, d_kts, d_grw, d_gate, g_wout, g_gpost, g_gnw, g_gnb,
     g_rk) = _out_head_call(x2, tgt2, gate, y_att, g_att, y0.reshape(R, RWKV_W), y1.reshape(R, RWKV_W), shifted, kt_s,
                            g_rw, w_out_f, g_post, gn_w, gn_b, rk_row, bd, T)
    v_heads = v_rows.reshape(B, T, RWKV_W // HEAD_DIM, 2 * HEAD_DIM)
    scan_cts = _scan_bwd_call(r4(w_s), r4(kt_s), r4(akk_s), kk_s.reshape(B, T, RWKV_W), sh3, v_heads,
                              d_ys.reshape(B, T, RWKV_W), st, eye_b, eye_f, bd, B, T)
    scan_cts = [a.reshape(R, RWKV_W) for a in scan_cts]
    d_shifted, g_wup, g_aup, g_w0, g_a0, g_kk, g_ka = _rwkv_prep_bwd_call(
        shifted, scan_cts + [d_r2, d_v2, d_kts], wup_pad, aup_pad, w0_f, a0_f, k_k, k_a, bd, T)
    d_rin, g_taps = _shift_bwd_call(rin, d_shifted, taps_f, T)
    dqr, dkp, dvp = _att_bwd_call(qr, kpad, vpad, d_yatt, B, T)
    grad_x, dproj, d_shift, d_scale, g_gpre, g_qg, g_kg = _in_proj_bwd_call(
        x2, dy, shift, scale, g_pre, w_in_f, qg, kg, cos, sin, bd, q_raw, k_raw, dqr, dkp, dvp, d_gatt, d_rin, d_grw, T)
    g_win = _w_in_grad_call(hb, dproj)

    rep = jnp.concatenate([g_gpre.reshape(-1), g_qg.reshape(-1, HEAD_DIM).sum(0), g_kg.reshape(-1, HEAD_DIM).sum(0),
                           g_kk.reshape(-1), g_ka.reshape(-1), g_rk.reshape(-1), g_gnw.reshape(-1), g_gnb.reshape(-1),
                           g_gpost.reshape(-1), loss_blk[0, :1]])
    rep = jnp.pad(rep, (0, _REP_ROWS * 128 - rep.shape[0])).reshape(_REP_ROWS, 128)
    dmod = jnp.concatenate([d_shift, d_scale, d_gate], axis=2).reshape(B, 3 * D_MODEL)
    small_parts = jax.vmap(_pack_small)(_full_to_cols(g_taps[:3]), _full_to_cols(g_wup[:, :64, :]), _full_to_cols(g_w0),
                                        _full_to_cols(g_aup[:, 64:, :]), _full_to_cols(g_a0))
    by_core = lambda a: jnp.swapaxes(a.reshape((NDEV // 2, 2) + a.shape[1:]), 0, 1).astype(MXU_DTYPE)
    s_win, s_wout = _pair_sum_call(
        [by_core(_full_to_cols(g_win)), by_core(g_wout.reshape(NDEV, D_MODEL // NDEV, D_MODEL))], "reduce_pair")
    p_win, p_wout, p_small, dmod_g, rep_g = _exchange(
        [s_win, s_wout, small_parts, dmod, rep], ["chips", "chips", "scatter", "all", "all"], "reduce_grads")
    dmod_all = dmod_g.reshape(NDEV * B, 3 * D_MODEL)
    g_wada = _wada_grad_call(c_all, lax.dynamic_slice(dmod_all, (0, me * ncol), (NDEV * B, ncol)))

    res, small = {}, []

    def adam(name, parts, w, m, v, row_tile=None, alone=False):
        two_d = (-1, w.shape[-1])
        item = (parts.reshape((parts.shape[0],) + w.reshape(two_d).shape), w.reshape(two_d), m.reshape(two_d),
                v.reshape(two_d))
        if alone:
            res[name] = [o.reshape(w.shape) for o in _adam_call(*item, "adam_" + name, row_tile)]
        else:
            small.append((name, w.shape, item))

    adam("w_ada", g_wada[None], w_ada, m_w_ada, v_w_ada, alone=True)
    adam("b_ada", dmod_all.reshape(NDEV * B, 1, 3 * D_MODEL), b_ada, m_b_ada, v_b_ada)
    adam("w_in", p_win, w_in, m_w_in, v_w_in, 512, alone=True)
    adam("w_out", p_wout, w_out, m_w_out, v_w_out, alone=True)
    taps_p, wup_p, w0_p, aup_p, a0_p = _unpack_small(p_small)
    adam("shift_taps", taps_p, shift_taps, m_shift_taps, v_shift_taps)
    adam("w_up", wup_p, w_up, m_w_up, v_w_up)
    adam("w0", w0_p, w0, m_w0, v_w0)
    adam("a_up", aup_p, a_up, m_a_up, v_a_up)
    adam("a0", a0_p, a0, m_a0, v_a0)
    rep_flat = rep_g.reshape(NDEV, -1)
    off = 0
    given = dict(g_pre=(g_pre, m_g_pre, v_g_pre), q_norm_g=(q_norm_g, m_q_norm_g, v_q_norm_g),
                 k_norm_g=(k_norm_g, m_k_norm_g, v_k_norm_g), k_k=(k_k, m_k_k, v_k_k), k_a=(k_a, m_k_a, v_k_a),
                 r_k=(r_k, m_r_k, v_r_k), gn_w=(gn_w, m_gn_w, v_gn_w), gn_b=(gn_b, m_gn_b, v_gn_b),
                 g_post=(g_post, m_g_post, v_g_post))
    for name, size in _REP_SIZES:
        adam(name, rep_flat[:, off:off + size], *given[name])
        off += size
    for (name, shape, _), out in zip(small, _adam_small_call([item for _, _, item in small], "adam_small")):
        res[name] = [o.reshape(shape) for o in out]

    loss = jnp.sum(rep_flat[:, off])
    order = ["w_ada", "b_ada", "g_pre", "w_in", "q_norm_g", "k_norm_g", "shift_taps", "w_up", "w0", "a_up", "a0", "k_k",
             "k_a", "r_k", "gn_w", "gn_b", "w_out", "g_post"]
    return (loss, grad_x.reshape(B, T, D_MODEL), *[res[n][0] for n in order], *[res[n][1] for n in order],
            *[res[n][2] for n in order], *[res[n][3] for n in order])
```

```python
import functools

import jax
import jax.numpy as jnp
from jax import lax
from jax.experimental import pallas as pl
from jax.experimental.pallas import tpu as pltpu

F32 = jnp.float32
MXU_DTYPE = jnp.bfloat16
MESH = pl.DeviceIdType.MESH
NDEV = 8

D_MODEL = 1024
HEAD_DIM = 64
ATT_W = 512
KV_W = 128
RWKV_W = 512
LORA_W = 128
SHIFT_W = 3 * RWKV_W + LORA_W
GRID_W = 64
ROPE_THETA = 10000.0
DECAY_SCALE = 0.6065306597126334
NORM_EPS = 1e-6
GN_EPS = 64e-5
L2_EPS = 1e-12
ATT_SCALE = HEAD_DIM ** -0.5
C_Q, C_K, C_V, C_GA, C_RIN, C_GRW, C_END = 0, 512, 640, 768, 1280, 2944, 3456

ADAM_LR, ADAM_B1, ADAM_B2, ADAM_EPS, ADAM_WD, ADAM_STEP = 0.001, 0.9, 0.999, 1e-08, 0.01, 10

ROW_TILE = 256
SHIFT_TILE = 512
W_GRAD_ROWS = 4096
ATT_TILE_FWD = 256
ATT_TILE_BWD = 1024
SCAN_CHUNK = 64
SCAN_UNROLL = 16
VMEM_LIMIT = 56 * 1024 * 1024


def _cp(sem=None):
    return pltpu.CompilerParams(dimension_semantics=sem, vmem_limit_bytes=VMEM_LIMIT)


def _dot(a, b, dims=(((1,), (0,)), ((), ()))):
    return lax.dot_general(a.astype(MXU_DTYPE), b.astype(MXU_DTYPE), dims, preferred_element_type=F32)


def _dot_nt(a, b):
    return _dot(a, b, (((1,), (1,)), ((), ())))


def _dot_tn(a, b):
    return _dot(a, b, (((0,), (0,)), ((), ())))


def _seg_dot(xb, bd):
    n = xb.shape[1]
    if n <= 256:
        return jnp.dot(xb, bd[:n, :n], preferred_element_type=F32)
    parts = [jnp.dot(xb[:, c:c + 256], bd, preferred_element_type=F32) for c in range(0, n, 256)]
    return jnp.concatenate(parts, axis=1)


def _segsum_raw(x, bd):
    rows = x.shape[0]
    hi = x.astype(MXU_DTYPE)
    lo = (x - hi.astype(F32)).astype(MXU_DTYPE)
    both = _seg_dot(jnp.concatenate([hi, lo], axis=0), bd)
    return both[:rows] + both[rows:]


@jax.custom_vjp
def _segsum_d(x, bd):
    return _segsum_raw(x, bd)


def _segsum_d_fwd(x, bd):
    return _segsum_raw(x, bd), bd


def _segsum_d_bwd(bd, ct):
    return _segsum_raw(ct, bd), jnp.zeros_like(bd)


_segsum_d.defvjp(_segsum_d_fwd, _segsum_d_bwd)


def _rope_tables(T):
    t = jnp.arange(T, dtype=F32)
    row = jnp.floor(t / GRID_W)
    col = t - row * GRID_W
    n_freq = HEAD_DIM // 4
    inv_freq = ROPE_THETA ** (-jnp.arange(n_freq, dtype=F32) / n_freq)
    d = jnp.arange(HEAD_DIM)
    pos = jnp.where((d < HEAD_DIM // 2)[None, :], row[:, None], col[:, None])
    ang = pos * inv_freq[d % n_freq][None, :]
    sign = jnp.where((d % 32) < 16, -1.0, 1.0).astype(F32)[None, :]
    cos = jnp.cos(ang)
    sin = jnp.sin(ang) * sign
    return jnp.tile(cos, (1, 2)), jnp.tile(sin, (1, 2))


def _rope_raw(x, cos, sin):
    n = x.shape[1]
    lane = lax.broadcasted_iota(jnp.int32, (1, n), 1)
    first = (lane % 32) < 16
    partner = jnp.where(first, pltpu.roll(x, n - 16, 1), pltpu.roll(x, 16, 1))
    return x * cos + partner * sin


@jax.custom_vjp
def _rope_d(x, cos, sin):
    return _rope_raw(x, cos, sin)


def _rope_d_fwd(x, cos, sin):
    return _rope_raw(x, cos, sin), (cos, sin)


def _rope_d_bwd(res, ct):
    cos, sin = res
    return _rope_raw(ct, cos, -sin), jnp.zeros_like(cos), jnp.zeros_like(sin)


_rope_d.defvjp(_rope_d_fwd, _rope_d_bwd)


def _rms(x, g):
    return x * lax.rsqrt(jnp.mean(x * x, axis=-1, keepdims=True) + NORM_EPS) * g


def _pre_fn(x, shift, scale, g_pre):
    return _rms(x, g_pre) * (1.0 + scale) + shift


def _qk_fn(q, g, cos, sin, bd, scale, diff):
    segsum = _segsum_d if diff else _segsum_raw
    rope = _rope_d if diff else _rope_raw
    qn = q * lax.rsqrt(segsum(q * q, bd) * (1.0 / HEAD_DIM) + NORM_EPS) * g
    return rope(qn, cos, sin) * scale


def _silu(x):
    return x * jax.nn.sigmoid(x)


def _rwkv_pw(k, pw0, pw1, pa0, pa1, w0, a0, k_k, k_a, bd, diff):
    segsum = _segsum_d if diff else _segsum_raw
    kk = k * k_k
    kk = kk * lax.rsqrt(segsum(kk * kk, bd) + L2_EPS)
    ws, kts, akks = [], [], []
    for z, (pw, pa) in enumerate(((pw0, pa0), (pw1, pa1))):
        w = jnp.exp(-DECAY_SCALE * jax.nn.sigmoid(w0[z:z + 1, :] + pw))
        a = jax.nn.sigmoid(a0[z:z + 1, :] + pa)
        ws.append(w)
        kts.append(k * (1.0 + (a - 1.0) * k_a))
        akks.append(a * kk)
    return ws[0], ws[1], kts[0], kts[1], akks[0], akks[1], kk


def _mix_fn(y_att, g_att, ys, r, v, kts, g_rw, gn_w, gn_b, r_k, bd, diff):
    segsum = _segsum_d if diff else _segsum_raw
    mu = segsum(ys, bd) * (1.0 / HEAD_DIM)
    d = ys - mu
    var = segsum(d * d, bd) * (1.0 / HEAD_DIM)
    yn = d * lax.rsqrt(var + GN_EPS) * gn_w + gn_b
    bonus = segsum(r * kts * r_k, bd) * v
    return y_att * _silu(g_att), (yn + bonus) * _silu(g_rw)


def _loss_fn(out, x, tgt, gate, g_post):
    e = x + gate * _rms(out, g_post) - tgt
    s = jnp.sum(e * e, axis=1, keepdims=True)
    return jnp.sum(s, axis=0, keepdims=True) * (0.5 / D_MODEL)


def _exchange(arrays, modes, name):
    n = len(arrays)
    out_shape = tuple(
        jax.ShapeDtypeStruct(((NDEV,) + tuple(a.shape)) if mode == "all" else tuple(a.shape), a.dtype)
        for a, mode in zip(arrays, modes))
    chips = (4, 2, 6)

    def body(*refs):
        ins, outs = refs[:n], refs[n:2 * n]
        send_sems, recv_sems, local_sems = refs[2 * n:]
        ix, iy, ic = lax.axis_index("x"), lax.axis_index("y"), lax.axis_index("c")
        me = 4 * ix + 2 * iy + ic

        def peer(m):
            px = 1 - ix if (m >> 2) & 1 else ix
            py = 1 - iy if (m >> 1) & 1 else iy
            pc = 1 - ic if m & 1 else ic
            return (px, py, pc), 4 * px + 2 * py + pc

        def copy(k, j, src_ref, slot, to):
            return pltpu.make_async_remote_copy(src_ref=src_ref, dst_ref=outs[k].at[slot], send_sem=send_sems.at[k, j],
                                                recv_sem=recv_sems.at[k, j], device_id=to, device_id_type=MESH)

        local, sends, arrivals, forwards = [], [], [], []
        for k in range(n):
            if modes[k] == "scatter":
                local.append(pltpu.make_async_copy(ins[k].at[me], outs[k].at[me], local_sems.at[k]))
                for m in range(1, NDEV):
                    to, p = peer(m)
                    sends.append(copy(k, m - 1, ins[k].at[p], me, to))
                    arrivals.append(copy(k, m - 1, ins[k].at[p], p, to))
            elif modes[k] == "chips":
                mine = me // 2
                local.append(pltpu.make_async_copy(ins[k].at[mine], outs[k].at[mine], local_sems.at[k]))
                for j, m in enumerate(chips):
                    to, p = peer(m)
                    sends.append(copy(k, j, ins[k].at[p // 2], mine, to))
                    arrivals.append(copy(k, j, ins[k].at[p // 2], p // 2, to))
            else:
                local.append(pltpu.make_async_copy(ins[k], outs[k].at[me], local_sems.at[k]))
                sib, sib_slot = peer(1)
                sends.append(copy(k, 0, ins[k], me, sib))
                for j, m in enumerate(chips):
                    to, p = peer(m)
                    sends.append(copy(k, 1 + j, ins[k], me, to))
                    forwards.append((copy(k, 1 + j, ins[k], p, to), copy(k, 4 + j, outs[k].at[p], p, sib)))
                    arrivals.append(copy(k, 4 + j, ins[k], peer(m ^ 1)[1], sib))
                arrivals.append(copy(k, 0, ins[k], sib_slot, sib))
        for cp in local + sends:
            cp.start()
        for arrived, onward in forwards:
            arrived.wait_recv()
            onward.start()
        for cp in arrivals:
            cp.wait_recv()
        for cp in sends + [onward for _, onward in forwards]:
            cp.wait_send()
        for cp in local:
            cp.wait()

    any_spec = pl.BlockSpec(memory_space=pl.ANY)
    return pl.pallas_call(
        body, name=name, out_shape=out_shape,
        in_specs=[any_spec] * n, out_specs=tuple([any_spec] * n),
        scratch_shapes=[pltpu.SemaphoreType.DMA((n, NDEV - 1)), pltpu.SemaphoreType.DMA((n, NDEV - 1)),
                        pltpu.SemaphoreType.DMA((n,))],
    )(*arrays)


def _pair_sum_call(parts, name):
    n = len(parts)

    def body(*refs):
        in_r, out_r, mine_r, land_r = (refs[j * n:(j + 1) * n] for j in range(4))
        send_sems, recv_sems, local_sems = refs[4 * n:]
        core = lax.axis_index("c")
        sibling = (lax.axis_index("x"), lax.axis_index("y"), 1 - core)
        local = [pltpu.make_async_copy(in_r[k].at[core], mine_r[k], local_sems.at[k]) for k in range(n)]
        swaps = [pltpu.make_async_remote_copy(src_ref=in_r[k].at[1 - core], dst_ref=land_r[k], send_sem=send_sems.at[k],
                                              recv_sem=recv_sems.at[k], device_id=sibling, device_id_type=MESH)
                 for k in range(n)]
        for cp in local + swaps:
            cp.start()
        for k in range(n):
            local[k].wait()
            swaps[k].wait()
            out_r[k][...] = (mine_r[k][...].astype(F32) + land_r[k][...].astype(F32)).astype(out_r[k].dtype)

    halves = [jax.ShapeDtypeStruct(a.shape[1:], a.dtype) for a in parts]
    return pl.pallas_call(
        body, name=name, out_shape=tuple(halves), in_specs=[pl.BlockSpec(memory_space=pl.ANY)] * n,
        scratch_shapes=[pltpu.VMEM(h.shape, h.dtype) for h in halves] * 2 + [pltpu.SemaphoreType.DMA((n,))] * 3,
        compiler_params=pltpu.CompilerParams(vmem_limit_bytes=VMEM_LIMIT),
    )(*parts)


def _mod_call(c_all, w_ada, b_cols):
    def body(c_ref, w_ref, b_ref, o_ref):
        o_ref[...] = _dot(_silu(c_ref[...]), w_ref[...]) + b_ref[...]

    return pl.pallas_call(body, name="mod_fwd",
                          out_shape=jax.ShapeDtypeStruct((c_all.shape[0], w_ada.shape[1]), F32))(c_all, w_ada, b_cols)


def _wada_grad_call(c_all, dmod_cols):
    def body(c_ref, d_ref, o_ref):
        o_ref[...] = _dot_tn(_silu(c_ref[...]), d_ref[...])

    return pl.pallas_call(body, name="w_ada_grad",
                          out_shape=jax.ShapeDtypeStruct((c_all.shape[1], dmod_cols.shape[1]), F32))(c_all, dmod_cols)


def _full(shape):
    nd = len(shape)
    return pl.BlockSpec(shape, lambda *_: (0,) * nd)


def _in_proj_call(x2, shift, scale, g_pre, w_in, qg, kg, cos, sin, bd, T):
    R = x2.shape[0]
    TT = min(SHIFT_TILE, T)
    tpe = T // TT

    def body(x_ref, sh_ref, sc_ref, gp_ref, w_ref, qg_ref, kg_ref, cos_ref, sin_ref, bd_ref,
             hb_ref, qr_ref, kpad_ref, vpad_ref, qraw_ref, kraw_ref, gatt_ref, rin_ref, grw_ref):
        h = _pre_fn(x_ref[...], sh_ref[0], sc_ref[0], gp_ref[...])
        hb = h.astype(MXU_DTYPE)
        hb_ref[...] = hb

        def proj(c0, c1):
            return _dot_nt(hb, w_ref[c0:c1, :])

        q = proj(C_Q, C_K)
        k = proj(C_K, C_V)
        v = proj(C_V, C_GA)
        gatt_ref[...] = proj(C_GA, C_RIN)
        rin_ref[...] = proj(C_RIN, C_GRW)
        grw_ref[...] = proj(C_GRW, C_END)
        qraw_ref[...] = q
        kraw_ref[...] = k
        cos, sin, bd = cos_ref[...], sin_ref[...], bd_ref[...]
        qr = _qk_fn(q, qg_ref[...], jnp.tile(cos, (1, 4)), jnp.tile(sin, (1, 4)), bd, ATT_SCALE, False)
        qr_ref[...] = qr.astype(MXU_DTYPE)
        kr = _qk_fn(k, kg_ref[...], cos, sin, bd, 1.0, False)
        left = lax.broadcasted_iota(jnp.int32, (1, KV_W), 1) < HEAD_DIM
        for ref, val in ((kpad_ref, kr), (vpad_ref, v)):
            h0l = jnp.where(left, val, 0.0)
            h1r = jnp.where(left, 0.0, val)
            ref[0] = h0l.astype(MXU_DTYPE)
            ref[1] = pltpu.roll(h0l, HEAD_DIM, 1).astype(MXU_DTYPE)
            ref[2] = pltpu.roll(h1r, HEAD_DIM, 1).astype(MXU_DTYPE)
            ref[3] = h1r.astype(MXU_DTYPE)

    row = lambda w: pl.BlockSpec((TT, w), lambda i: (i, 0))
    per_ex = pl.BlockSpec((1, 1, D_MODEL), lambda i: (i // tpe, 0, 0))
    tab = pl.BlockSpec((TT, KV_W), lambda i: (i % tpe, 0))
    pad = pl.BlockSpec((4, TT, KV_W), lambda i: (0, i, 0))
    sds = jax.ShapeDtypeStruct
    return pl.pallas_call(
        body, name="in_proj", grid=(R // TT,),
        in_specs=[row(D_MODEL), per_ex, per_ex, _full((1, D_MODEL)), _full(w_in.shape), _full((1, ATT_W)),
                  _full((1, KV_W)), tab, tab, _full((256, 256))],
        out_specs=(row(D_MODEL), row(ATT_W), pad, pad, row(ATT_W), row(KV_W), row(ATT_W), row(SHIFT_W), row(RWKV_W)),
        out_shape=(sds((R, D_MODEL), MXU_DTYPE), sds((R, ATT_W), MXU_DTYPE), sds((4, R, KV_W), MXU_DTYPE),
                   sds((4, R, KV_W), MXU_DTYPE), sds((R, ATT_W), F32), sds((R, KV_W), F32), sds((R, ATT_W), F32),
                   sds((R, SHIFT_W), F32), sds((R, RWKV_W), F32)),
        compiler_params=_cp(("arbitrary",)),
    )(x2, shift, scale, g_pre, w_in, qg, kg, cos, sin, bd)


def _softmax_parts(s):
    e = jnp.exp(s - jnp.max(s, axis=1, keepdims=True))
    return e, 1.0 / jnp.sum(e, axis=1, keepdims=True)


def _att_specs(T, TQ):
    nq = T // TQ
    qspec = pl.BlockSpec((TQ, KV_W), lambda b, p, i: (b * nq + i, p))
    side = lambda s: pl.BlockSpec((None, T, KV_W), lambda b, p, i: (2 * (p // 2) + s, b, 0))
    return nq, qspec, side


def _att_fwd_call(qr, kpad, vpad, B, T):
    TQ = min(ATT_TILE_FWD, T)
    nq, qspec, side = _att_specs(T, TQ)

    def body(q_ref, kl_ref, kr_ref, vl_ref, vr_ref, o_ref):
        q = q_ref[...]
        ea, inv_a = _softmax_parts(_dot_nt(q, kl_ref[...]))
        eb, inv_b = _softmax_parts(_dot_nt(q, kr_ref[...]))
        o_ref[...] = _dot(ea, vl_ref[...]) * inv_a + _dot(eb, vr_ref[...]) * inv_b

    return pl.pallas_call(
        body, name="att_fwd", grid=(B, 4, nq),
        in_specs=[qspec, side(0), side(1), side(0), side(1)], out_specs=qspec,
        out_shape=jax.ShapeDtypeStruct((B * T, ATT_W), F32),
        compiler_params=_cp(("arbitrary",) * 3),
    )(qr, kpad, kpad, vpad, vpad)


def _att_bwd_call(qr, kpad, vpad, d_o, B, T):
    TQ = min(ATT_TILE_BWD, T)
    nq, qspec, side = _att_specs(T, TQ)

    def body(q_ref, kl_ref, kr_ref, vl_ref, vr_ref, do_ref, dq_ref, dk_ref, dv_ref):
        i = pl.program_id(2)
        q, do = q_ref[...], do_ref[...]
        left = lax.broadcasted_iota(jnp.int32, (1, KV_W), 1) < HEAD_DIM
        dq = jnp.zeros((TQ, KV_W), F32)
        dk = jnp.zeros((T, KV_W), F32)
        dv = jnp.zeros((T, KV_W), F32)
        for k_ref, v_ref, mask in ((kl_ref, vl_ref, left), (kr_ref, vr_ref, jnp.logical_not(left))):
            kk, vv = k_ref[...], v_ref[...]
            e, inv = _softmax_parts(_dot_nt(q, kk))
            dp = _dot_nt(do, vv)
            ds = e * (dp - inv * jnp.sum(e * dp, axis=1, keepdims=True))
            dq = dq + _dot(ds, kk) * inv
            dk = dk + _dot_tn(ds, jnp.where(mask, q * inv, 0.0))
            dv = dv + _dot_tn(e, jnp.where(mask, do * inv, 0.0))
        dq_ref[...] = dq

        @pl.when(i == 0)
        def _():
            dk_ref[...] = dk
            dv_ref[...] = dv

        @pl.when(i > 0)
        def _():
            dk_ref[...] += dk
            dv_ref[...] += dv

    acc = pl.BlockSpec((None, T, KV_W), lambda b, p, i: (p, b, 0))
    sds = jax.ShapeDtypeStruct
    return pl.pallas_call(
        body, name="att_bwd", grid=(B, 4, nq),
        in_specs=[qspec, side(0), side(1), side(0), side(1), qspec], out_specs=(qspec, acc, acc),
        out_shape=(sds((B * T, ATT_W), F32), sds((4, B * T, KV_W), F32), sds((4, B * T, KV_W), F32)),
        compiler_params=_cp(("arbitrary",) * 3),
    )(qr, kpad, kpad, vpad, vpad, d_o)


def _shift_specs(R, T, TT, width):
    tpe = T // TT
    nb8 = R // 8
    cur = pl.BlockSpec((TT, width), lambda i: (i, 0))
    prev = pl.BlockSpec((8, width), lambda i: (jnp.maximum(i * (TT // 8) - 1, 0), 0))
    nxt = pl.BlockSpec((8, width), lambda i: (jnp.minimum((i + 1) * (TT // 8), nb8 - 1), 0))
    return tpe, cur, prev, nxt


def _neighbours(cur, prev8, next8, i, tpe, TT):
    rows = lax.broadcasted_iota(jnp.int32, (TT, 1), 0)
    first = jnp.where(i % tpe == 0, 0.0, 1.0)
    last = jnp.where(i % tpe == tpe - 1, 0.0, 1.0)
    before = jnp.where(rows == 0, prev8[7:8, :] * first, pltpu.roll(cur, 1, 0))
    after = jnp.where(rows == TT - 1, next8[0:1, :] * last, pltpu.roll(cur, TT - 1, 0))
    return before, after


def _shift_fwd_call(x, taps, T):
    R, width = x.shape
    TT = min(SHIFT_TILE, T)
    tpe, cur, prev, nxt = _shift_specs(R, T, TT, width)

    def body(x_ref, p_ref, n_ref, t_ref, o_ref, vh_ref):
        xc = x_ref[...]
        before, after = _neighbours(xc, p_ref[...], n_ref[...], pl.program_id(0), tpe, TT)
        out = t_ref[0:1, :] * before + t_ref[1:2, :] * xc + t_ref[2:3, :] * after
        o_ref[...] = out
        left = lax.broadcasted_iota(jnp.int32, (1, KV_W), 1) < HEAD_DIM
        for p in range(RWKV_W // KV_W):
            pair = out[:, 2 * RWKV_W + p * KV_W:2 * RWKV_W + (p + 1) * KV_W]
            vh_ref[:, 2 * p * KV_W:(2 * p + 1) * KV_W] = jnp.where(left, pair, 0.0)
            vh_ref[:, (2 * p + 1) * KV_W:(2 * p + 2) * KV_W] = jnp.where(left, pltpu.roll(pair, HEAD_DIM, 1), 0.0)

    return pl.pallas_call(
        body, name="shift_fwd", grid=(R // TT,), in_specs=[cur, prev, nxt, _full(taps.shape)],
        out_specs=(cur, pl.BlockSpec((TT, 2 * RWKV_W), lambda i: (i, 0))),
        out_shape=(jax.ShapeDtypeStruct((R, width), F32), jax.ShapeDtypeStruct((R, 2 * RWKV_W), F32)),
        compiler_params=_cp(("arbitrary",)),
    )(x, x, x, taps)


def _shift_bwd_call(x, d, taps, T):
    R, width = x.shape
    TT = min(SHIFT_TILE, T)
    tpe, cur, prev, nxt = _shift_specs(R, T, TT, width)

    def body(x_ref, xp_ref, xn_ref, d_ref, dp_ref, dn_ref, t_ref, dx_ref, dt_ref):
        i = pl.program_id(0)
        xc, dc = x_ref[...], d_ref[...]
        d_before, d_after = _neighbours(dc, dp_ref[...], dn_ref[...], i, tpe, TT)
        dx_ref[...] = t_ref[2:3, :] * d_before + t_ref[1:2, :] * dc + t_ref[0:1, :] * d_after
        x_before, x_after = _neighbours(xc, xp_ref[...], xn_ref[...], i, tpe, TT)
        @pl.when(i == 0)
        def _():
            dt_ref[...] = jnp.zeros_like(dt_ref)

        for j, xs in enumerate((x_before, xc, x_after)):
            dt_ref[j:j + 1, :] += jnp.sum(dc * xs, axis=0, keepdims=True)

    return pl.pallas_call(
        body, name="shift_bwd", grid=(R // TT,),
        in_specs=[cur, prev, nxt, cur, prev, nxt, _full(taps.shape)], out_specs=(cur, _full((8, width))),
        out_shape=(jax.ShapeDtypeStruct((R, width), F32), jax.ShapeDtypeStruct((8, width), F32)),
        compiler_params=_cp(("arbitrary",)),
    )(x, x, x, d, d, d, taps)


def _lora_in(wa):
    lane = lax.broadcasted_iota(jnp.int32, (1, LORA_W), 1)
    return jnp.where(lane < LORA_W // 2, jnp.tanh(wa), wa)


def _rwkv_prep_call(shifted, wup, aup, w0, a0, k_k, k_a, bd, T):
    R = shifted.shape[0]
    TT = min(SHIFT_TILE, T)

    def body(k_ref, wa_ref, wup_ref, aup_ref, w0_ref, a0_ref, kk_ref, ka_ref, bd_ref, w_o, kt_o, akk_o, kk_o):
        twa = _lora_in(wa_ref[...])
        pre = [_dot(twa, m_ref[z]) for m_ref in (wup_ref, aup_ref) for z in range(2)]
        outs = _rwkv_pw(k_ref[...], pre[0], pre[1], pre[2], pre[3], w0_ref[...], a0_ref[...], kk_ref[...],
                        ka_ref[...], bd_ref[...], False)
        w_o[0], w_o[1], kt_o[0], kt_o[1], akk_o[0], akk_o[1] = outs[:6]
        kk_o[...] = outs[6]

    col = lambda c, w: pl.BlockSpec((TT, w), lambda i: (i, c))
    two = pl.BlockSpec((2, TT, RWKV_W), lambda i: (0, i, 0))
    sds = jax.ShapeDtypeStruct
    return pl.pallas_call(
        body, name="rwkv_prep", grid=(R // TT,),
        in_specs=[col(1, RWKV_W), col(3 * RWKV_W // LORA_W, LORA_W), _full(wup.shape), _full(aup.shape),
                  _full((2, RWKV_W)), _full((2, RWKV_W)), _full((1, RWKV_W)), _full((1, RWKV_W)), _full((256, 256))],
        out_specs=(two, two, two, col(0, RWKV_W)),
        out_shape=(sds((2, R, RWKV_W), F32),) * 3 + (sds((R, RWKV_W), F32),),
        compiler_params=_cp(("arbitrary",)),
    )(shifted, shifted, wup, aup, w0, a0, k_k, k_a, bd)


def _rwkv_prep_bwd_call(shifted, cts, wup, aup, w0, a0, k_k, k_a, bd, T):
    R = shifted.shape[0]
    TT = min(SHIFT_TILE, T)

    def body(k_ref, wa_ref, dw0, dkt0, dakk0, dkk0, dr0, dv0, dw1, dkt1, dakk1, dkk1, dr1, dv1, dr2_ref, dv2_ref, dkts_ref,
             wup_ref, aup_ref, w0_ref, a0_ref, kk_ref, ka_ref, bd_ref,
             dsh_ref, gwup_ref, gaup_ref, gw0_ref, ga0_ref, gkk_ref, gka_ref):
        dw_ref, dkt_ref, dakk_ref, dkk_ref, dr_ref, dv_ref = ((dw0, dw1), (dkt0, dkt1), (dakk0, dakk1), (dkk0, dkk1),
                                                              (dr0, dr1), (dv0, dv1))
        i = pl.program_id(0)
        wa = wa_ref[...]
        twa = _lora_in(wa)
        pre = [_dot(twa, m_ref[z]) for m_ref in (wup_ref, aup_ref) for z in range(2)]
        fn = functools.partial(_rwkv_pw, bd=bd_ref[...], diff=True)
        _, vjp = jax.vjp(fn, k_ref[...], pre[0], pre[1], pre[2], pre[3], w0_ref[...], a0_ref[...], kk_ref[...],
                         ka_ref[...])
        dkts = dkts_ref[...]
        dk, dpw0, dpw1, dpa0, dpa1, gw0, ga0, gkk, gka = vjp(
            (dw_ref[0][...], dw_ref[1][...], dkt_ref[0][...] + dkts, dkt_ref[1][...] + dkts, dakk_ref[0][...],
             dakk_ref[1][...], dkk_ref[0][...] + dkk_ref[1][...]))
        dtwa = (_dot_nt(dpw0, wup_ref[0]) + _dot_nt(dpw1, wup_ref[1]) + _dot_nt(dpa0, aup_ref[0])
                + _dot_nt(dpa1, aup_ref[1]))
        lane = lax.broadcasted_iota(jnp.int32, (1, LORA_W), 1)
        dsh_ref[:, 0:RWKV_W] = dr_ref[0][...] + dr_ref[1][...] + dr2_ref[...]
        dsh_ref[:, RWKV_W:2 * RWKV_W] = dk
        dsh_ref[:, 2 * RWKV_W:3 * RWKV_W] = dv_ref[0][...] + dv_ref[1][...] + dv2_ref[...]
        dsh_ref[:, 3 * RWKV_W:] = jnp.where(lane < LORA_W // 2, dtwa * (1.0 - twa * twa), dtwa)
        acc = ((gwup_ref.at[0], _dot_tn(twa, dpw0)), (gwup_ref.at[1], _dot_tn(twa, dpw1)),
               (gaup_ref.at[0], _dot_tn(twa, dpa0)), (gaup_ref.at[1], _dot_tn(twa, dpa1)),
               (gw0_ref, gw0), (ga0_ref, ga0), (gkk_ref, gkk), (gka_ref, gka))

        @pl.when(i == 0)
        def _():
            for ref, val in acc:
                ref[...] = val

        @pl.when(i > 0)
        def _():
            for ref, val in acc:
                ref[...] += val

    col = lambda c, w: pl.BlockSpec((TT, w), lambda i: (i, c))
    one = col(0, RWKV_W)
    sds = jax.ShapeDtypeStruct
    return pl.pallas_call(
        body, name="rwkv_prep_bwd", grid=(R // TT,),
        in_specs=[col(1, RWKV_W), col(3 * RWKV_W // LORA_W, LORA_W)] + [one] * 15 + [
                  _full(wup.shape), _full(aup.shape), _full((2, RWKV_W)), _full((2, RWKV_W)), _full((1, RWKV_W)),
                  _full((1, RWKV_W)), _full((256, 256))],
        out_specs=(pl.BlockSpec((TT, SHIFT_W), lambda i: (i, 0)), _full(wup.shape), _full(aup.shape),
                   _full((2, RWKV_W)), _full((2, RWKV_W)), _full((1, RWKV_W)), _full((1, RWKV_W))),
        out_shape=(sds((R, SHIFT_W), F32), sds(wup.shape, F32), sds(aup.shape, F32), sds((2, RWKV_W), F32),
                   sds((2, RWKV_W), F32), sds((1, RWKV_W), F32), sds((1, RWKV_W), F32)),
        compiler_params=_cp(("arbitrary",)),
    )(shifted, shifted, *cts, wup, aup, w0, a0, k_k, k_a, bd)


def _rows_b(row):
    return jnp.tile(jnp.broadcast_to(row, (16, row.shape[1])).astype(MXU_DTYPE), (HEAD_DIM // 16, 1))


def _col_lhs(row, eye_b):
    return eye_b * _rows_b(row)


def _colsum(x):
    return jnp.sum(x, axis=0, keepdims=True)


def _stacked_segsum(tiles, bd):
    res = _seg_dot(jnp.concatenate(tiles, axis=0), bd)
    return [res[j * HEAD_DIM:(j + 1) * HEAD_DIM] for j in range(len(tiles))]


def _scan_specs(B, T, C, nC):
    def blk(z, col, rev):
        idx = (lambda g: (z, 0, nC - 1 - g, col)) if rev else (lambda g: (z, 0, g, col))
        return pl.BlockSpec((None, B, C, RWKV_W), idx)

    def blk3(col, rev):
        idx = (lambda g: (0, nC - 1 - g, col)) if rev else (lambda g: (0, g, col))
        return pl.BlockSpec((B, C, RWKV_W), idx)

    return blk, blk3


def _scan_fwd_call(w, kt, akk, kk, shifted, eye_b, eye_f, bd, B, T):
    C = min(SCAN_CHUNK, T)
    nC = T // C
    blk, blk3 = _scan_specs(B, T, C, nC)

    def body(w0, kt0, akk0, kk0, v0, r0, w1, kt1, akk1, kk1, v1, r1, eb_ref, ef_ref, bd_ref, y0, y1, st, S):
        @pl.when(pl.program_id(0) == 0)
        def _():
            S[...] = jnp.zeros_like(S)

        st[0] = S[...].astype(MXU_DTYPE)
        dirs = ((w0, kt0, akk0, kk0, v0, r0, y0), (w1, kt1, akk1, kk1, v1, r1, y1))

        def step(s, carry):
            for z in range(2):
                row = s if z == 0 else C - 1 - s
                prev = jnp.maximum(s - 1, 0) if z == 0 else jnp.minimum(C - s, C - 1)
                wr, ktr, akkr, kkr, vr, rr, yr = dirs[z]
                tiles = []
                for b in range(B):
                    Sb = st[s, z * B + b]
                    tiles += [Sb * _rows_b(kkr[b, pl.ds(row, 1), :]),
                              _col_lhs(vr[b, pl.ds(row, 1), :], eb_ref[...]),
                              Sb * _rows_b(rr[b, pl.ds(prev, 1), :])]
                res = _stacked_segsum(tiles, bd_ref[...])
                for b in range(B):
                    c = z * B + b
                    sab, vb, yb = res[3 * b:3 * b + 3]
                    ld = lambda ref: ref[b, pl.ds(row, 1), :]
                    Sn = S[c] * ld(wr) - sab * ld(akkr) + vb * ld(ktr)
                    S[c] = Sn
                    st[s + 1, c] = Sn.astype(MXU_DTYPE)
                    yr[b, pl.ds(prev, 1), :] = _colsum(ef_ref[...] * yb)
            return carry

        lax.fori_loop(0, C, step, 0, unroll=SCAN_UNROLL)
        for z in range(2):
            last = C - 1 if z == 0 else 0
            rr, yr = dirs[z][5], dirs[z][6]
            res = _stacked_segsum([st[C, z * B + b] * _rows_b(rr[b, last:last + 1, :]) for b in range(B)],
                                  bd_ref[...])
            for b in range(B):
                yr[b, last:last + 1, :] = _colsum(ef_ref[...] * res[b])

    ins, specs = [], []
    for z, rev in ((0, False), (1, True)):
        ins += [w, kt, akk, kk, shifted, shifted]
        specs += [blk(z, 0, rev), blk(z, 0, rev), blk(z, 0, rev), blk3(0, rev), blk3(2, rev), blk3(0, rev)]
    sds = jax.ShapeDtypeStruct
    return pl.pallas_call(
        body, name="scan_fwd", grid=(nC,),
        in_specs=specs + [_full((HEAD_DIM, RWKV_W)), _full((HEAD_DIM, RWKV_W)), _full((256, 256))],
        out_specs=(blk3(0, False), blk3(0, True),
                   pl.BlockSpec((None, C + 1, 2 * B, HEAD_DIM, RWKV_W), lambda g: (g, 0, 0, 0, 0))),
        out_shape=(sds((B, T, RWKV_W), F32), sds((B, T, RWKV_W), F32),
                   sds((nC, C + 1, 2 * B, HEAD_DIM, RWKV_W), MXU_DTYPE)),
        scratch_shapes=[pltpu.VMEM((2 * B, HEAD_DIM, RWKV_W), F32)],
        compiler_params=_cp(("arbitrary",)),
    )(*ins, eye_b, eye_f, bd)


def _scan_bwd_call(w, kt, akk, kk, shifted, v_heads, dys, st, eye_b, eye_f, bd, B, T):
    C = min(SCAN_CHUNK, T)
    nC = T // C
    blk, blk3 = _scan_specs(B, T, C, nC)
    nin = 7

    def body(*refs):
        d0, d1 = refs[:nin], refs[nin:2 * nin]
        st_ref, eb_ref, ef_ref, sel_ref, hm_ref, bd_ref = refs[2 * nin:2 * nin + 6]
        o0, o1 = refs[2 * nin + 6:2 * nin + 12], refs[2 * nin + 12:2 * nin + 18]
        COL, DYC, G = refs[2 * nin + 18:]

        @pl.when(pl.program_id(0) == 0)
        def _():
            G[...] = jnp.zeros_like(G)

        dirs = (d0 + (o0,), d1 + (o1,))

        def column_operands(s, z):
            row = s if z == 0 else C - 1 - s
            _, _, _, kkr, _, _, dyr, _ = dirs[z]
            tiles = []
            for b in range(B):
                tiles += [st_ref[s, z * B + b] * _rows_b(kkr[b, pl.ds(row, 1), :]),
                          _col_lhs(dyr[b, pl.ds(row, 1), :], eb_ref[...])]
            return tiles

        def keep_columns(res, z):
            for b in range(B):
                for k in range(2):
                    COL[k, z * B + b] = res[2 * b + k].astype(MXU_DTYPE)
                DYC[z * B + b] = res[2 * b + 1]

        for z in range(2):
            keep_columns(_stacked_segsum(column_operands(C - 1, z), bd_ref[...]), z)

        def bwd(it, carry):
            s = C - 1 - it
            for z in range(2):
                row = s if z == 0 else C - 1 - s
                wr, ktr, akkr, kkr, vr, rr, dyr, (dw_o, dkt_o, dakk_o, dkk_o, dr_o, dv_o) = dirs[z]
                tiles, Gcs = [], []
                for b in range(B):
                    c = z * B + b
                    Gc = G[c] + DYC[c] * rr[b, pl.ds(row, 1), :]
                    Gb = Gc.astype(MXU_DTYPE)
                    Gcs.append((Gc, Gb))
                    tiles += [Gb * _rows_b(akkr[b, pl.ds(row, 1), :]),
                              Gb * _rows_b(ktr[b, pl.ds(row, 1), :])]
                res = _stacked_segsum(tiles + column_operands(jnp.maximum(s - 1, 0), z), bd_ref[...])
                for b in range(B):
                    c = z * B + b
                    Gc, Gb = Gcs[b]
                    gab, dvb = res[2 * b], res[2 * b + 1]
                    ld = lambda ref: ref[b, pl.ds(row, 1), :]
                    G[c] = Gc * ld(wr) - gab * ld(kkr)
                    Sb = st_ref[s, c]
                    prods = jnp.concatenate([Gb, st_ref[s + 1, c] * COL[1, c], Gb * Sb, Gb * COL[0, c],
                                             gab.astype(MXU_DTYPE) * Sb], axis=0)
                    v_rows = jnp.concatenate([vr[b, pl.ds(row, 1)][0], jnp.zeros((8, 3 * HEAD_DIM), F32)], axis=1)
                    lhs = jnp.concatenate([sel_ref[...], v_rows], axis=0).astype(MXU_DTYPE)
                    sums = jnp.dot(lhs, prods, preferred_element_type=F32)
                    for k, (ref, sign) in enumerate(((dr_o, 1.0), (dw_o, 1.0), (dakk_o, -1.0), (dkk_o, -1.0))):
                        ref[b, pl.ds(row, 1), :] = sign * sums[k:k + 1, :]
                    dkt_o[b, pl.ds(row, 1), :] = _colsum(sums[8:16] * hm_ref[...])
                    dv_o[b, pl.ds(row, 1), :] = _colsum(ef_ref[...] * dvb)
                keep_columns(res[2 * B:], z)
            return carry

        lax.fori_loop(0, C, bwd, 0, unroll=SCAN_UNROLL)

    ins, specs = [], []
    for z, rev in ((0, True), (1, False)):
        heads = pl.BlockSpec((B, C) + v_heads.shape[2:], (lambda g: (0, nC - 1 - g, 0, 0)) if rev else (lambda g: (0, g, 0, 0)))
        ins += [w, kt, akk, kk, v_heads, shifted, dys]
        specs += [blk(z, 0, rev), blk(z, 0, rev), blk(z, 0, rev), blk3(0, rev), heads, blk3(0, rev), blk3(0, rev)]
    sel = (jnp.arange(8)[:, None] + 1 == (jnp.arange(5 * HEAD_DIM) // HEAD_DIM)[None, :]).astype(F32)
    head_rows = (jnp.arange(RWKV_W // HEAD_DIM)[:, None] == (jnp.arange(RWKV_W) // HEAD_DIM)[None, :]).astype(F32)
    ins += [st, eye_b, eye_f, sel, head_rows, bd]
    specs += [pl.BlockSpec((None, C + 1, 2 * B, HEAD_DIM, RWKV_W), lambda g: (nC - 1 - g, 0, 0, 0, 0)),
              _full((HEAD_DIM, RWKV_W)), _full((HEAD_DIM, RWKV_W)), _full(sel.shape), _full(head_rows.shape),
              _full((256, 256))]
    sds = jax.ShapeDtypeStruct
    out_specs = tuple(blk3(0, True) for _ in range(6)) + tuple(blk3(0, False) for _ in range(6))
    res = pl.pallas_call(
        body, name="scan_bwd", grid=(nC,), in_specs=specs, out_specs=out_specs,
        out_shape=tuple(sds((B, T, RWKV_W), F32) for _ in range(12)),
        scratch_shapes=[pltpu.VMEM((2, 2 * B, HEAD_DIM, RWKV_W), MXU_DTYPE), pltpu.VMEM((2 * B, HEAD_DIM, RWKV_W), F32),
                        pltpu.VMEM((2 * B, HEAD_DIM, RWKV_W), F32)],
        compiler_params=_cp(("arbitrary",)),
    )(*ins)
    return list(res)


def _out_head_call(x2, tgt2, gate, y_att, g_att, y0, y1, shifted, kt, g_rw, w_out, g_post, gn_w, gn_b, r_k, bd, T):
    R = x2.shape[0]
    TT = min(ROW_TILE, T)
    tpe = T // TT

    def body(x_ref, t_ref, gate_ref, ya_ref, ga_ref, y0_ref, y1_ref, r_ref, v_ref, kt_ref, grw_ref, w_ref, gp_ref,
             gnw_ref, gnb_ref, rk_ref, bd_ref,
             loss_o, dy_o, dya_o, dga_o, dys_o, dr_o, dv_o, dkts_o, dgrw_o, dgate_o, gw_o, ggp_o, ggnw_o, ggnb_o, grk_o):
        i = pl.program_id(0)
        bd = bd_ref[...]
        mix = functools.partial(_mix_fn, bd=bd, diff=True)
        (ma, mr), mix_vjp = jax.vjp(mix, ya_ref[...], ga_ref[...], y0_ref[...] + y1_ref[...], r_ref[...], v_ref[...],
                                    kt_ref[0] + kt_ref[1], grw_ref[...], gnw_ref[...], gnb_ref[...], rk_ref[...])
        out = _dot(ma, w_ref[0:ATT_W, :]) + _dot(mr, w_ref[ATT_W:, :])
        loss, loss_vjp = jax.vjp(_loss_fn, out, x_ref[...], t_ref[...], gate_ref[0], gp_ref[...])
        d_out, dy, _, dgate, dgp = loss_vjp(jnp.ones((1, 1), F32))
        dy_o[...] = dy
        dma = _dot_nt(d_out, w_ref[0:ATT_W, :])
        dmr = _dot_nt(d_out, w_ref[ATT_W:, :])
        dya_o[...], dga_o[...], dys_o[...], dr_o[...], dv_o[...], dkts_o[...], dgrw_o[...], dgnw, dgnb, drk = \
            mix_vjp((dma, dmr))
        gw = jnp.concatenate([_dot_tn(ma, d_out), _dot_tn(mr, d_out)], axis=0)
        acc = ((loss_o, jnp.broadcast_to(loss, (8, 128))), (gw_o, gw), (ggp_o, dgp), (ggnw_o, dgnw), (ggnb_o, dgnb),
               (grk_o, drk))

        @pl.when(i == 0)
        def _():
            for ref, val in acc:
                ref[...] = val

        @pl.when(i > 0)
        def _():
            for ref, val in acc:
                ref[...] += val

        @pl.when(i % tpe == 0)
        def _():
            dgate_o[0] = dgate

        @pl.when(i % tpe > 0)
        def _():
            dgate_o[0] += dgate

    row = lambda w, c=0: pl.BlockSpec((TT, w), lambda i: (i, c))
    two = pl.BlockSpec((2, TT, RWKV_W), lambda i: (0, i, 0))
    per_ex = pl.BlockSpec((1, 1, D_MODEL), lambda i: (i // tpe, 0, 0))
    sds = jax.ShapeDtypeStruct
    r512 = sds((R, RWKV_W), F32)
    return pl.pallas_call(
        body, name="out_head", grid=(R // TT,),
        in_specs=[row(D_MODEL), row(D_MODEL), per_ex, row(ATT_W), row(ATT_W), row(RWKV_W), row(RWKV_W), row(RWKV_W, 0),
                  row(RWKV_W, 2), two,
                  row(RWKV_W), _full(w_out.shape), _full((1, D_MODEL)), _full((1, RWKV_W)), _full((1, RWKV_W)),
                  _full((1, RWKV_W)), _full((256, 256))],
        out_specs=(_full((8, 128)), row(D_MODEL), row(ATT_W), row(ATT_W), row(RWKV_W), row(RWKV_W), row(RWKV_W),
                   row(RWKV_W), row(RWKV_W), per_ex, _full((D_MODEL, D_MODEL)), _full((1, D_MODEL)), _full((1, RWKV_W)),
                   _full((1, RWKV_W)), _full((1, RWKV_W))),
        out_shape=(sds((8, 128), F32), sds((R, D_MODEL), F32), r512, r512, r512, r512, r512, r512, r512,
                   sds((R // T, 1, D_MODEL), F32), sds((D_MODEL, D_MODEL), F32), sds((1, D_MODEL), F32),
                   sds((1, RWKV_W), F32), sds((1, RWKV_W), F32), sds((1, RWKV_W), F32)),
        compiler_params=_cp(("arbitrary",)),
    )(x2, tgt2, gate, y_att, g_att, y0, y1, shifted, shifted, kt, g_rw, w_out, g_post, gn_w, gn_b, r_k, bd)


def _in_proj_bwd_call(x2, dy, shift, scale, g_pre, w_in, qg, kg, cos, sin, bd, q_raw, k_raw, dqr, dkp, dvp,
                      d_gatt, d_rin, d_grw, T):
    R = x2.shape[0]
    TT = min(SHIFT_TILE, T)
    tpe = T // TT

    def body(x_ref, dy_ref, sh_ref, sc_ref, gp_ref, w_ref, qg_ref, kg_ref, cos_ref, sin_ref, bd_ref, q_ref, k_ref,
             dqr_ref, dkp_ref, dvp_ref, dga_ref, drin_ref, dgrw_ref,
             dx_o, dproj_o, dsh_o, dsc_o, ggp_o, gqg_o, gkg_o):
        i = pl.program_id(0)
        cos, sin, bd = cos_ref[...], sin_ref[...], bd_ref[...]
        left = lax.broadcasted_iota(jnp.int32, (1, KV_W), 1) < HEAD_DIM

        def kv_grad(ref):
            a = ref[0] + ref[1]
            b = ref[2] + ref[3]
            return jnp.where(left, a + pltpu.roll(a, HEAD_DIM, 1), b + pltpu.roll(b, HEAD_DIM, 1))

        qfn = functools.partial(_qk_fn, cos=jnp.tile(cos, (1, 4)), sin=jnp.tile(sin, (1, 4)), bd=bd, scale=ATT_SCALE,
                                diff=True)
        _, q_vjp = jax.vjp(qfn, q_ref[...], qg_ref[...])
        dq, gqg = q_vjp(dqr_ref[...])
        kfn = functools.partial(_qk_fn, cos=cos, sin=sin, bd=bd, scale=1.0, diff=True)
        _, k_vjp = jax.vjp(kfn, k_ref[...], kg_ref[...])
        dk, gkg = k_vjp(kv_grad(dkp_ref))
        pieces = ((C_Q, C_K, dq), (C_K, C_V, dk), (C_V, C_GA, kv_grad(dvp_ref)), (C_GA, C_RIN, dga_ref[...]),
                  (C_RIN, C_GRW, drin_ref[...]), (C_GRW, C_END, dgrw_ref[...]))
        dh = jnp.zeros((TT, D_MODEL), F32)
        for c0, c1, val in pieces:
            vb = val.astype(MXU_DTYPE)
            dproj_o[:, c0:c1] = vb
            dh = dh + _dot(vb, w_ref[c0:c1, :])
        _, pre_vjp = jax.vjp(_pre_fn, x_ref[...], sh_ref[0], sc_ref[0], gp_ref[...])
        dx, dsh, dsc, ggp = pre_vjp(dh)
        dx_o[...] = dx + dy_ref[...]
        acc = ((ggp_o, ggp), (gqg_o, gqg), (gkg_o, gkg))

        @pl.when(i == 0)
        def _():
            for ref, val in acc:
                ref[...] = val

        @pl.when(i > 0)
        def _():
            for ref, val in acc:
                ref[...] += val

        @pl.when(i % tpe == 0)
        def _():
            dsh_o[0] = dsh
            dsc_o[0] = dsc

        @pl.when(i % tpe > 0)
        def _():
            dsh_o[0] += dsh
            dsc_o[0] += dsc

    row = lambda w: pl.BlockSpec((TT, w), lambda i: (i, 0))
    per_ex = pl.BlockSpec((1, 1, D_MODEL), lambda i: (i // tpe, 0, 0))
    tab = pl.BlockSpec((TT, KV_W), lambda i: (i % tpe, 0))
    pad = pl.BlockSpec((4, TT, KV_W), lambda i: (0, i, 0))
    sds = jax.ShapeDtypeStruct
    nb = R // T
    return pl.pallas_call(
        body, name="in_proj_bwd", grid=(R // TT,),
        in_specs=[row(D_MODEL), row(D_MODEL), per_ex, per_ex, _full((1, D_MODEL)), _full(w_in.shape), _full((1, ATT_W)),
                  _full((1, KV_W)), tab, tab, _full((256, 256)), row(ATT_W), row(KV_W), row(ATT_W), pad, pad,
                  row(ATT_W), row(SHIFT_W), row(RWKV_W)],
        out_specs=(row(D_MODEL), row(C_END), per_ex, per_ex, _full((1, D_MODEL)), _full((1, ATT_W)), _full((1, KV_W))),
        out_shape=(sds((R, D_MODEL), F32), sds((R, C_END), MXU_DTYPE), sds((nb, 1, D_MODEL), F32),
                   sds((nb, 1, D_MODEL), F32), sds((1, D_MODEL), F32), sds((1, ATT_W), F32), sds((1, KV_W), F32)),
        compiler_params=_cp(("arbitrary",)),
    )(x2, dy, shift, scale, g_pre, w_in, qg, kg, cos, sin, bd, q_raw, k_raw, dqr, dkp, dvp, d_gatt, d_rin, d_grw)


def _w_in_grad_call(hb, dproj):
    R = hb.shape[0]
    TT = min(W_GRAD_ROWS, R)
    CB = 1152
    last = R // TT - 1

    def body(h_ref, d_ref, o_ref, acc):
        g = _dot_tn(h_ref[...], d_ref[...])

        @pl.when(pl.program_id(1) == 0)
        def _():
            acc[...] = g

        @pl.when(pl.program_id(1) > 0)
        def _():
            acc[...] += g

        @pl.when(pl.program_id(1) == last)
        def _():
            o_ref[...] = acc[...].astype(o_ref.dtype)

    return pl.pallas_call(
        body, name="w_in_grad", grid=(C_END // CB, R // TT),
        in_specs=[pl.BlockSpec((TT, D_MODEL), lambda j, i: (i, 0)), pl.BlockSpec((TT, CB), lambda j, i: (i, j))],
        out_specs=pl.BlockSpec((D_MODEL, CB), lambda j, i: (0, j)),
        out_shape=jax.ShapeDtypeStruct((D_MODEL, C_END), MXU_DTYPE),
        scratch_shapes=[pltpu.VMEM((D_MODEL, CB), F32)], compiler_params=_cp(("arbitrary", "arbitrary")),
    )(hb, dproj)


def _adam_refs(p_ref, w_ref, m_ref, v_ref, g_o, d_o, m_o, v_o):
    g = p_ref[0].astype(F32)
    for j in range(1, p_ref.shape[0]):
        g = g + p_ref[j].astype(F32)
    m2 = ADAM_B1 * m_ref[...] + (1.0 - ADAM_B1) * g
    v2 = ADAM_B2 * v_ref[...] + (1.0 - ADAM_B2) * jnp.square(g)
    m_hat = m2 / (1.0 - ADAM_B1 ** ADAM_STEP)
    v_hat = v2 / (1.0 - ADAM_B2 ** ADAM_STEP)
    g_o[...] = g
    d_o[...] = -ADAM_LR * (m_hat / (jnp.sqrt(v_hat) + ADAM_EPS) + ADAM_WD * w_ref[...])
    m_o[...] = m2
    v_o[...] = v2


def _adam_small_call(items, name):
    n = len(items)

    def body(*refs):
        for k in range(n):
            _adam_refs(*refs[4 * k:4 * k + 4], *refs[4 * n + 4 * k:4 * n + 4 * k + 4])

    out_shape = tuple(jax.ShapeDtypeStruct(w.shape, F32) for _, w, _, _ in items for _ in range(4))
    out = pl.pallas_call(body, name=name, out_shape=out_shape)(*[a for item in items for a in item])
    return [out[4 * k:4 * k + 4] for k in range(n)]


def _adam_call(parts, w, m, v, name, row_tile=None):
    P, M, N = parts.shape
    TM = M if row_tile is None else row_tile

    def body(*refs):
        _adam_refs(*refs)

    blk = pl.BlockSpec((TM, N), lambda i: (i, 0))
    return pl.pallas_call(
        body, name=name, grid=(M // TM,),
        in_specs=[pl.BlockSpec((P, TM, N), lambda i: (0, i, 0)), blk, blk, blk], out_specs=(blk,) * 4,
        out_shape=(jax.ShapeDtypeStruct((M, N), F32),) * 4, compiler_params=_cp(("arbitrary",)),
    )(parts, w, m, v)


_SMALL_ROWS = 136


def _pack_small(taps, w_up, w0, a_up, a0):
    flat = jnp.concatenate([taps.reshape(-1), w_up.reshape(-1), w0.reshape(-1), a_up.reshape(-1), a0.reshape(-1)])
    return jnp.pad(flat, (0, _SMALL_ROWS * 128 - flat.shape[0])).reshape(_SMALL_ROWS, 128)


def _unpack_small(packed):
    n = packed.shape[0]
    flat = packed.reshape(n, -1)
    out, o = [], 0
    for shape in ((3, 208), (2, 64, 64), (2, 64), (2, 64, 64), (2, 64)):
        size = 1
        for s in shape:
            size *= s
        out.append(flat[:, o:o + size].reshape((n,) + shape))
        o += size
    return out


def _cols_to_full(blocks):
    nd = blocks.ndim
    moved = jnp.moveaxis(blocks, 0, nd - 2)
    return moved.reshape(moved.shape[:-2] + (moved.shape[-2] * moved.shape[-1],))


def _full_to_cols(full):
    k = full.shape[-1] // NDEV
    return jnp.moveaxis(full.reshape(full.shape[:-1] + (NDEV, k)), -2, 0)


_REP_SIZES = (("g_pre", 1024), ("q_norm_g", 64), ("k_norm_g", 64), ("k_k", 512), ("k_a", 512), ("r_k", 512),
              ("gn_w", 512), ("gn_b", 512), ("g_post", 1024))
_REP_ROWS = 40


def kernel(x, c, w_ada, b_ada, g_pre, w_in, q_norm_g, k_norm_g, shift_taps, w_up, w0, a_up, a0, k_k, k_a, r_k, gn_w, gn_b, w_out, g_post, loss_target, m_w_ada, m_b_ada, m_g_pre, m_w_in, m_q_norm_g, m_k_norm_g, m_shift_taps, m_w_up, m_w0, m_a_up, m_a0, m_k_k, m_k_a, m_r_k, m_gn_w, m_gn_b, m_w_out, m_g_post, v_w_ada, v_b_ada, v_g_pre, v_w_in, v_q_norm_g, v_k_norm_g, v_shift_taps, v_w_up, v_w0, v_a_up, v_a0, v_k_k, v_k_a, v_r_k, v_gn_w, v_gn_b, v_w_out, v_g_post):
    B, T, _ = x.shape
    R = B * T
    me = 4 * lax.axis_index("x") + 2 * lax.axis_index("y") + lax.axis_index("c")
    x2 = x.reshape(R, D_MODEL)
    tgt2 = loss_target.reshape(R, D_MODEL)

    seg = jnp.arange(256) // HEAD_DIM
    bd = (seg[:, None] == seg[None, :]).astype(MXU_DTYPE)
    eye = (jnp.arange(HEAD_DIM)[:, None] == (jnp.arange(RWKV_W) % HEAD_DIM)[None, :])
    eye_b, eye_f = eye.astype(MXU_DTYPE), eye.astype(F32)
    cos, sin = _rope_tables(T)

    c_g, w_in_g, w_out_g, small_g = _exchange(
        [c, w_in[0].T.astype(MXU_DTYPE), w_out[0].astype(MXU_DTYPE),
         _pack_small(shift_taps[0], w_up[0], w0[0], a_up[0], a0[0])], ["all"] * 4, "gather_params")
    c_all = c_g.reshape(NDEV * B, D_MODEL)
    w_in_f = w_in_g.reshape(C_END, D_MODEL)
    w_out_f = w_out_g.reshape(D_MODEL, D_MODEL)
    taps_b, w_up_b, w0_b, a_up_b, a0_b = _unpack_small(small_g)
    taps_f = jnp.pad(_cols_to_full(taps_b), ((0, 5), (0, 0)))
    w_up_f, a_up_f = _cols_to_full(w_up_b), _cols_to_full(a_up_b)
    w0_f, a0_f = _cols_to_full(w0_b), _cols_to_full(a0_b)
    wup_pad = jnp.pad(w_up_f, ((0, 0), (0, 64), (0, 0))).astype(MXU_DTYPE)
    aup_pad = jnp.pad(a_up_f, ((0, 0), (64, 0), (0, 0))).astype(MXU_DTYPE)

    ncol = w_ada.shape[2]
    b_cols = lax.dynamic_slice(b_ada, (0, me * ncol), (1, ncol))
    mod_cols = _mod_call(c_all, w_ada[0].astype(MXU_DTYPE), b_cols)
    (mod_g,) = _exchange([mod_cols], ["all"], "gather_mod")
    mod = lax.dynamic_slice(_cols_to_full(mod_g), (me * B, 0), (B, 3 * D_MODEL))
    shift, scale, gate = [mod[:, j * D_MODEL:(j + 1) * D_MODEL].reshape(B, 1, D_MODEL) for j in range(3)]

    qg = jnp.tile(q_norm_g, (1, ATT_W // HEAD_DIM))
    kg = jnp.tile(k_norm_g, (1, KV_W // HEAD_DIM))
    rk_row = r_k.reshape(1, RWKV_W)

    hb, qr, kpad, vpad, q_raw, k_raw, g_att, rin, g_rw = _in_proj_call(
        x2, shift, scale, g_pre, w_in_f, qg, kg, cos, sin, bd, T)
    y_att = _att_fwd_call(qr, kpad, vpad, B, T)
    shifted, v_rows = _shift_fwd_call(rin, taps_f, T)
    w_s, kt_s, akk_s, kk_s = _rwkv_prep_call(shifted, wup_pad, aup_pad, w0_f, a0_f, k_k, k_a, bd, T)
    sh3 = shifted.reshape(B, T, SHIFT_W)
    r4 = lambda a: a.reshape(2, B, T, RWKV_W)
    y0, y1, st = _scan_fwd_call(r4(w_s), r4(kt_s), r4(akk_s), kk_s.reshape(B, T, RWKV_W), sh3, eye_b, eye_f, bd, B, T)

    (loss_blk, dy, d_yatt, d_gatt, d_ys, d_r2, d_v2, d_kts, d_grw, d_gate, g_wout, g_gpost, g_gnw, g_gnb,
     g_rk) = _out_head_call(x2, tgt2, gate, y_att, g_att, y0.reshape(R, RWKV_W), y1.reshape(R, RWKV_W), shifted, kt_s,
                            g_rw, w_out_f, g_post, gn_w, gn_b, rk_row, bd, T)
    v_heads = v_rows.reshape(B, T, RWKV_W // HEAD_DIM, 2 * HEAD_DIM)
    scan_cts = _scan_bwd_call(r4(w_s), r4(kt_s), r4(akk_s), kk_s.reshape(B, T, RWKV_W), sh3, v_heads,
                              d_ys.reshape(B, T, RWKV_W), st, eye_b, eye_f, bd, B, T)
    scan_cts = [a.reshape(R, RWKV_W) for a in scan_cts]
    d_shifted, g_wup, g_aup, g_w0, g_a0, g_kk, g_ka = _rwkv_prep_bwd_call(
        shifted, scan_cts + [d_r2, d_v2, d_kts], wup_pad, aup_pad, w0_f, a0_f, k_k, k_a, bd, T)
    d_rin, g_taps = _shift_bwd_call(rin, d_shifted, taps_f, T)
    dqr, dkp, dvp = _att_bwd_call(qr, kpad, vpad, d_yatt, B, T)
    grad_x, dproj, d_shift, d_scale, g_gpre, g_qg, g_kg = _in_proj_bwd_call(
        x2, dy, shift, scale, g_pre, w_in_f, qg, kg, cos, sin, bd, q_raw, k_raw, dqr, dkp, dvp, d_gatt, d_rin, d_grw, T)
    g_win = _w_in_grad_call(hb, dproj)

    rep = jnp.concatenate([g_gpre.reshape(-1), g_qg.reshape(-1, HEAD_DIM).sum(0), g_kg.reshape(-1, HEAD_DIM).sum(0),
                           g_kk.reshape(-1), g_ka.reshape(-1), g_rk.reshape(-1), g_gnw.reshape(-1), g_gnb.reshape(-1),
                           g_gpost.reshape(-1), loss_blk[0, :1]])
    rep = jnp.pad(rep, (0, _REP_ROWS * 128 - rep.shape[0])).reshape(_REP_ROWS, 128)
    dmod = jnp.concatenate([d_shift, d_scale, d_gate], axis=2).reshape(B, 3 * D_MODEL)
    small_parts = jax.vmap(_pack_small)(_full_to_cols(g_taps[:3]), _full_to_cols(g_wup[:, :64, :]), _full_to_cols(g_w0),
                                        _full_to_cols(g_aup[:, 64:, :]), _full_to_cols(g_a0))
    by_core = lambda a: jnp.swapaxes(a.reshape((NDEV // 2, 2) + a.shape[1:]), 0, 1).astype(MXU_DTYPE)
    s_win, s_wout = _pair_sum_call(
        [by_core(_full_to_cols(g_win)), by_core(g_wout.reshape(NDEV, D_MODEL // NDEV, D_MODEL))], "reduce_pair")
    p_win, p_wout, p_small, dmod_g, rep_g = _exchange(
        [s_win, s_wout, small_parts, dmod, rep], ["chips", "chips", "scatter", "all", "all"], "reduce_grads")
    dmod_all = dmod_g.reshape(NDEV * B, 3 * D_MODEL)
    g_wada = _wada_grad_call(c_all, lax.dynamic_slice(dmod_all, (0, me * ncol), (NDEV * B, ncol)))

    res, small = {}, []

    def adam(name, parts, w, m, v, row_tile=None, alone=False):
        two_d = (-1, w.shape[-1])
        item = (parts.reshape((parts.shape[0],) + w.reshape(two_d).shape), w.reshape(two_d), m.reshape(two_d),
                v.reshape(two_d))
        if alone:
            res[name] = [o.reshape(w.shape) for o in _adam_call(*item, "adam_" + name, row_tile)]
        else:
            small.append((name, w.shape, item))

    adam("w_ada", g_wada[None], w_ada, m_w_ada, v_w_ada, alone=True)
    adam("b_ada", dmod_all.reshape(NDEV * B, 1, 3 * D_MODEL), b_ada, m_b_ada, v_b_ada)
    adam("w_in", p_win, w_in, m_w_in, v_w_in, 512, alone=True)
    adam("w_out", p_wout, w_out, m_w_out, v_w_out, alone=True)
    taps_p, wup_p, w0_p, aup_p, a0_p = _unpack_small(p_small)
    adam("shift_taps", taps_p, shift_taps, m_shift_taps, v_shift_taps)
    adam("w_up", wup_p, w_up, m_w_up, v_w_up)
    adam("w0", w0_p, w0, m_w0, v_w0)
    adam("a_up", aup_p, a_up, m_a_up, v_a_up)
    adam("a0", a0_p, a0, m_a0, v_a0)
    rep_flat = rep_g.reshape(NDEV, -1)
    off = 0
    given = dict(g_pre=(g_pre, m_g_pre, v_g_pre), q_norm_g=(q_norm_g, m_q_norm_g, v_q_norm_g),
                 k_norm_g=(k_norm_g, m_k_norm_g, v_k_norm_g), k_k=(k_k, m_k_k, v_k_k), k_a=(k_a, m_k_a, v_k_a),
                 r_k=(r_k, m_r_k, v_r_k), gn_w=(gn_w, m_gn_w, v_gn_w), gn_b=(gn_b, m_gn_b, v_gn_b),
                 g_post=(g_post, m_g_post, v_g_post))
    for name, size in _REP_SIZES:
        adam(name, rep_flat[:, off:off + size], *given[name])
        off += size
    for (name, shape, _), out in zip(small, _adam_small_call([item for _, _, item in small], "adam_small")):
        res[name] = [o.reshape(shape) for o in out]

    loss = jnp.sum(rep_flat[:, off])
    order = ["w_ada", "b_ada", "g_pre", "w_in", "q_norm_g", "k_norm_g", "shift_taps", "w_up", "w0", "a_up", "a0", "k_k",
             "k_a", "r_k", "gn_w", "gn_b", "w_out", "g_post"]
    return (loss, grad_x.reshape(B, T, D_MODEL), *[res[n][0] for n in order], *[res[n][1] for n in order],
            *[res[n][2] for n in order], *[res[n][3] for n in order])
```
